```python
import math
import jax, jax.numpy as jnp
from jax import lax
import numpy as np

D_MODEL = 1024
BATCH = 32
SEQ = 2048
DEPTH = 1

MIX_W = D_MODEL
POOL_W = (3 * D_MODEL) // 8
SSM_W = (3 * D_MODEL) // 8
ATT_W = D_MODEL - POOL_W - SSM_W
IN_W = 2 * MIX_W
POOL_WINDOWS = (2, 4, 8, 16)
POOL_GROUPS = len(POOL_WINDOWS)
POOL_GW = POOL_W // POOL_GROUPS
SSM_GROUP = 16
SSM_NG = SSM_W // SSM_GROUP
SSM_N = 64
DT_MIN = 1e-3
DT_MAX = 1e-1
N_MEM = 256
MEM_HEADS = 4
MEM_HD = ATT_W // MEM_HEADS
EPS = 1e-6

kernel_name = "hybrid_pool_s5_memattn_layer"


def rmsnorm(x, g):
    xf = x.astype(jnp.float32)
    xf = xf * lax.rsqrt(jnp.mean(xf * xf, axis=-1, keepdims=True) + EPS)
    return (xf * g.astype(jnp.float32)).astype(x.dtype)


def pool_mixer(u, w_pool, pool_scale):
    b, l, _ = u.shape
    uf = u.astype(jnp.float32)
    cs0 = jnp.concatenate([jnp.zeros((b, 1, POOL_W), jnp.float32), jnp.cumsum(uf, axis=1)], axis=1)
    pos = jnp.arange(1, l + 1, dtype=jnp.float32)[None, :, None]
    outs = []
    for gi, w in enumerate(POOL_WINDOWS):
        sl = slice(gi * POOL_GW, (gi + 1) * POOL_GW)
        c = cs0[..., sl]
        lower = jnp.concatenate([jnp.zeros((b, w - 1, POOL_GW), jnp.float32), c[:, :l - w + 1]], axis=1)
        mean = (c[:, 1:] - lower) / jnp.minimum(pos, float(w))
        outs.append(jnp.einsum('blc,cd->bld', mean - uf[..., sl], w_pool[gi].astype(jnp.float32)))
    y = jnp.concatenate(outs, axis=-1) * pool_scale.astype(jnp.float32)
    return y.astype(u.dtype)


def _ssm_combine(e1, e2):
    a1, b1 = e1
    a2, b2 = e2
    return a1 * a2, a2 * b1 + b2


def s5_mixer(u, a_re, a_im, log_dt, b_re, b_im, c_re, c_im, d_skip, w_glu):
    bsz, l, _ = u.shape
    f32 = jnp.float32
    uf = u.astype(f32).reshape(bsz, l, SSM_NG, SSM_GROUP)
    lam = lax.complex(a_re.astype(f32), a_im.astype(f32))
    dt = jnp.exp(log_dt.astype(f32))[:, None]
    lam_bar = jnp.exp(lam * dt)
    b_mat = lax.complex(b_re.astype(f32), b_im.astype(f32))
    c_mat = lax.complex(c_re.astype(f32), c_im.astype(f32))
    b_bar = ((lam_bar - 1.0) / lam)[..., None] * b_mat
    bu = jnp.einsum('blgc,gnc->blgn', uf.astype(jnp.complex64), b_bar)
    lam_all = jnp.broadcast_to(lam_bar, bu.shape)
    _, hs = lax.associative_scan(_ssm_combine, (lam_all, bu), axis=1)
    y = jnp.einsum('blgn,gcn->blgc', hs, c_mat).real + d_skip.astype(f32).reshape(SSM_NG, SSM_GROUP) * uf
    y = jax.nn.gelu(y.reshape(bsz, l, SSM_W))
    z = y @ w_glu.astype(f32)
    out = z[..., :SSM_W] * jax.nn.sigmoid(z[..., SSM_W:])
    return out.astype(u.dtype)


def memory_attention(q, mem, g_mem, w_kv):
    bsz, l, _ = q.shape
    m = rmsnorm(mem, g_mem)
    kv = m @ w_kv
    k = kv[..., :ATT_W].reshape(bsz, -1, MEM_HEADS, MEM_HD).astype(jnp.float32)
    v = kv[..., ATT_W:].reshape(bsz, -1, MEM_HEADS, MEM_HD).astype(jnp.float32)
    qh = q.reshape(bsz, l, MEM_HEADS, MEM_HD).astype(jnp.float32)
    s = jnp.einsum('blhd,bmhd->bhlm', qh, k) * (MEM_HD ** -0.5)
    p = jax.nn.softmax(s, axis=-1)
    o = jnp.einsum('bhlm,bmhd->blhd', p, v).reshape(bsz, l, ATT_W)
    return o.astype(q.dtype)


def _fwd_setup_inputs(seed: int = 0) -> dict:
    key = jax.random.key(seed)
    ks = jax.random.split(key, 20)
    f32 = jnp.float32
    nrm = lambda k, shape, s: jax.random.normal(k, shape, f32) * s
    n_idx = jnp.arange(SSM_N, dtype=f32)
    a_re = -0.5 + nrm(ks[5], (DEPTH, SSM_NG, SSM_N), 1e-2)
    a_im = math.pi * n_idx[None, None, :] + nrm(ks[6], (DEPTH, SSM_NG, SSM_N), 1e-2)
    log_dt = jax.random.uniform(ks[7], (DEPTH, SSM_NG), f32, math.log(DT_MIN), math.log(DT_MAX))
    return {
        "x": nrm(ks[0], (BATCH, SEQ, D_MODEL), 1.0),
        "mem": nrm(ks[1], (BATCH, N_MEM, D_MODEL), 1.0),
        "g_pre": 1.0 + nrm(ks[2], (DEPTH, D_MODEL), 0.02),
        "w_in": nrm(ks[3], (DEPTH, D_MODEL, IN_W), D_MODEL ** -0.5),
        "w_pool": nrm(ks[4], (DEPTH, POOL_GROUPS, POOL_GW, POOL_GW), POOL_GW ** -0.5),
        "pool_scale": 1.0 + nrm(ks[8], (DEPTH, POOL_W), 0.02),
        "a_re": a_re,
        "a_im": a_im,
        "log_dt": log_dt,
        "b_re": nrm(ks[9], (DEPTH, SSM_NG, SSM_N, SSM_GROUP), (2 * SSM_GROUP) ** -0.5),
        "b_im": nrm(ks[10], (DEPTH, SSM_NG, SSM_N, SSM_GROUP), (2 * SSM_GROUP) ** -0.5),
        "c_re": nrm(ks[11], (DEPTH, SSM_NG, SSM_GROUP, SSM_N), (2 * SSM_N) ** -0.5),
        "c_im": nrm(ks[12], (DEPTH, SSM_NG, SSM_GROUP, SSM_N), (2 * SSM_N) ** -0.5),
        "d_skip": nrm(ks[13], (DEPTH, SSM_W), 1.0),
        "w_glu": nrm(ks[14], (DEPTH, SSM_W, 2 * SSM_W), SSM_W ** -0.5),
        "g_mem": 1.0 + nrm(ks[15], (DEPTH, D_MODEL), 0.02),
        "w_kv": nrm(ks[16], (DEPTH, D_MODEL, 2 * ATT_W), D_MODEL ** -0.5),
        "w_out": nrm(ks[17], (DEPTH, MIX_W, D_MODEL), MIX_W ** -0.5),
        "g_post": 1.0 + nrm(ks[18], (DEPTH, D_MODEL), 0.02),
    }


def _fwd_reference(x, mem, g_pre, w_in, w_pool, pool_scale, a_re, a_im, log_dt, b_re, b_im,
              c_re, c_im, d_skip, w_glu, g_mem, w_kv, w_out, g_post):
    for i in range(DEPTH):
        h = rmsnorm(x, g_pre[i])
        proj = h @ w_in[i]
        val, gate = proj[..., :MIX_W], proj[..., MIX_W:]
        u_pool = val[..., :POOL_W]
        u_ssm = val[..., POOL_W:POOL_W + SSM_W]
        q = val[..., POOL_W + SSM_W:]
        y_pool = pool_mixer(u_pool, w_pool[i], pool_scale[i])
        y_ssm = s5_mixer(u_ssm, a_re[i], a_im[i], log_dt[i], b_re[i], b_im[i],
                         c_re[i], c_im[i], d_skip[i], w_glu[i])
        y_att = memory_attention(q, mem, g_mem[i], w_kv[i])
        y = jnp.concatenate([y_pool, y_ssm, y_att], axis=-1) * jax.nn.silu(gate)
        out = y @ w_out[i]
        x = x + rmsnorm(out, g_post[i])
    return x


import jax as _jax
import jax.numpy as _jnp

TWIN_FORMAT = 'train_step'
FWD_PARAMS = ['x', 'mem', 'g_pre', 'w_in', 'w_pool', 'pool_scale', 'a_re', 'a_im', 'log_dt', 'b_re', 'b_im', 'c_re', 'c_im', 'd_skip', 'w_glu', 'g_mem', 'w_kv', 'w_out', 'g_post']
TWIN_WEIGHTS = ['g_pre', 'w_in', 'w_pool', 'pool_scale', 'a_re', 'a_im', 'log_dt', 'b_re', 'b_im', 'c_re', 'c_im', 'd_skip', 'w_glu', 'g_mem', 'w_kv', 'w_out', 'g_post']
TWIN_DIFF_INPUT = 'x'
TWIN_INPUTS = ['x', 'mem', 'g_pre', 'w_in', 'w_pool', 'pool_scale', 'a_re', 'a_im', 'log_dt', 'b_re', 'b_im', 'c_re', 'c_im', 'd_skip', 'w_glu', 'g_mem', 'w_kv', 'w_out', 'g_post', 'loss_target', 'm_g_pre', 'm_w_in', 'm_w_pool', 'm_pool_scale', 'm_a_re', 'm_a_im', 'm_log_dt', 'm_b_re', 'm_b_im', 'm_c_re', 'm_c_im', 'm_d_skip', 'm_w_glu', 'm_g_mem', 'm_w_kv', 'm_w_out', 'm_g_post', 'v_g_pre', 'v_w_in', 'v_w_pool', 'v_pool_scale', 'v_a_re', 'v_a_im', 'v_log_dt', 'v_b_re', 'v_b_im', 'v_c_re', 'v_c_im', 'v_d_skip', 'v_w_glu', 'v_g_mem', 'v_w_kv', 'v_w_out', 'v_g_post']
TWIN_OUTPUTS = ['loss', 'grad_x', 'grad_g_pre', 'grad_w_in', 'grad_w_pool', 'grad_pool_scale', 'grad_a_re', 'grad_a_im', 'grad_log_dt', 'grad_b_re', 'grad_b_im', 'grad_c_re', 'grad_c_im', 'grad_d_skip', 'grad_w_glu', 'grad_g_mem', 'grad_w_kv', 'grad_w_out', 'grad_g_post', 'delta_g_pre', 'delta_w_in', 'delta_w_pool', 'delta_pool_scale', 'delta_a_re', 'delta_a_im', 'delta_log_dt', 'delta_b_re', 'delta_b_im', 'delta_c_re', 'delta_c_im', 'delta_d_skip', 'delta_w_glu', 'delta_g_mem', 'delta_w_kv', 'delta_w_out', 'delta_g_post', 'new_m_g_pre', 'new_m_w_in', 'new_m_w_pool', 'new_m_pool_scale', 'new_m_a_re', 'new_m_a_im', 'new_m_log_dt', 'new_m_b_re', 'new_m_b_im', 'new_m_c_re', 'new_m_c_im', 'new_m_d_skip', 'new_m_w_glu', 'new_m_g_mem', 'new_m_w_kv', 'new_m_w_out', 'new_m_g_post', 'new_v_g_pre', 'new_v_w_in', 'new_v_w_pool', 'new_v_pool_scale', 'new_v_a_re', 'new_v_a_im', 'new_v_log_dt', 'new_v_b_re', 'new_v_b_im', 'new_v_c_re', 'new_v_c_im', 'new_v_d_skip', 'new_v_w_glu', 'new_v_g_mem', 'new_v_w_kv', 'new_v_w_out', 'new_v_g_post']
TWIN_LEAF_KINDS = {'loss': 'loss', 'grad_x': 'grad_x', 'grad_g_pre': 'grad_w', 'grad_w_in': 'grad_w', 'grad_w_pool': 'grad_w', 'grad_pool_scale': 'grad_w', 'grad_a_re': 'grad_w', 'grad_a_im': 'grad_w', 'grad_log_dt': 'grad_w', 'grad_b_re': 'grad_w', 'grad_b_im': 'grad_w', 'grad_c_re': 'grad_w', 'grad_c_im': 'grad_w', 'grad_d_skip': 'grad_w', 'grad_w_glu': 'grad_w', 'grad_g_mem': 'grad_w', 'grad_w_kv': 'grad_w', 'grad_w_out': 'grad_w', 'grad_g_post': 'grad_w', 'delta_g_pre': 'delta_w', 'delta_w_in': 'delta_w', 'delta_w_pool': 'delta_w', 'delta_pool_scale': 'delta_w', 'delta_a_re': 'delta_w', 'delta_a_im': 'delta_w', 'delta_log_dt': 'delta_w', 'delta_b_re': 'delta_w', 'delta_b_im': 'delta_w', 'delta_c_re': 'delta_w', 'delta_c_im': 'delta_w', 'delta_d_skip': 'delta_w', 'delta_w_glu': 'delta_w', 'delta_g_mem': 'delta_w', 'delta_w_kv': 'delta_w', 'delta_w_out': 'delta_w', 'delta_g_post': 'delta_w', 'new_m_g_pre': 'new_m', 'new_m_w_in': 'new_m', 'new_m_w_pool': 'new_m', 'new_m_pool_scale': 'new_m', 'new_m_a_re': 'new_m', 'new_m_a_im': 'new_m', 'new_m_log_dt': 'new_m', 'new_m_b_re': 'new_m', 'new_m_b_im': 'new_m', 'new_m_c_re': 'new_m', 'new_m_c_im': 'new_m', 'new_m_d_skip': 'new_m', 'new_m_w_glu': 'new_m', 'new_m_g_mem': 'new_m', 'new_m_w_kv': 'new_m', 'new_m_w_out': 'new_m', 'new_m_g_post': 'new_m', 'new_v_g_pre': 'new_v', 'new_v_w_in': 'new_v', 'new_v_w_pool': 'new_v', 'new_v_pool_scale': 'new_v', 'new_v_a_re': 'new_v', 'new_v_a_im': 'new_v', 'new_v_log_dt': 'new_v', 'new_v_b_re': 'new_v', 'new_v_b_im': 'new_v', 'new_v_c_re': 'new_v', 'new_v_c_im': 'new_v', 'new_v_d_skip': 'new_v', 'new_v_w_glu': 'new_v', 'new_v_g_mem': 'new_v', 'new_v_w_kv': 'new_v', 'new_v_w_out': 'new_v', 'new_v_g_post': 'new_v'}


def _forward(args):
    return _fwd_reference(*[args[k] for k in FWD_PARAMS])


def _output_shape():
    out = _jax.eval_shape(lambda: _forward(_fwd_setup_inputs(0)))
    return out.shape, out.dtype

N_MICROBATCH = 1
ADAM_LR = 0.001
ADAM_B1 = 0.9
ADAM_B2 = 0.999
ADAM_EPS = 1e-08
ADAM_WD = 0.01
ADAM_STEP = 10
PER_EXAMPLE_BATCH_AXIS = {'x': 0, 'mem': 0, 'loss_target': 0}
SHARED_INPUTS = []
_WEIGHT_DTYPES = {'g_pre': _jnp.float32, 'w_in': _jnp.float32, 'w_pool': _jnp.float32, 'pool_scale': _jnp.float32, 'a_re': _jnp.float32, 'a_im': _jnp.float32, 'log_dt': _jnp.float32, 'b_re': _jnp.float32, 'b_im': _jnp.float32, 'c_re': _jnp.float32, 'c_im': _jnp.float32, 'd_skip': _jnp.float32, 'w_glu': _jnp.float32, 'g_mem': _jnp.float32, 'w_kv': _jnp.float32, 'w_out': _jnp.float32, 'g_post': _jnp.float32}
MOMENT_SCALE = {'g_pre': 5.378959e-01, 'w_in': 4.163477e-01, 'w_pool': 6.550591e-01, 'pool_scale': 6.725785e-01, 'a_re': 1.712915e-02, 'a_im': 1.270657e-02, 'log_dt': 5.411026e+00, 'b_re': 7.910957e-03, 'b_im': 7.896149e-03, 'c_re': 1.586874e-02, 'c_im': 1.649363e-02, 'd_skip': 3.472279e-01, 'w_glu': 2.515013e-01, 'g_mem': 5.272802e-02, 'w_kv': 7.602123e-02, 'w_out': 4.552483e-01, 'g_post': 6.403409e+01}


def _to_microbatches(a, axis):
    t = _jnp.moveaxis(a, axis, 0)
    t = t.reshape((N_MICROBATCH, t.shape[0] // N_MICROBATCH) + t.shape[1:])
    return _jnp.moveaxis(t, 1, axis + 1)


def setup_inputs(seed: int = 0) -> dict:
    inp = _fwd_setup_inputs(seed)
    key = _jax.random.fold_in(_jax.random.key(seed), 7919)
    shape, _ = _output_shape()
    out = dict(inp)
    out["loss_target"] = _jax.random.normal(_jax.random.fold_in(key, 0), shape, _jnp.float32)
    for i, name in enumerate(TWIN_WEIGHTS):
        w = inp[name].astype(_jnp.float32)
        if MOMENT_SCALE is None:
            s = _jnp.sqrt(_jnp.mean(_jnp.square(w)) + 1e-30)
        else:
            s = MOMENT_SCALE[name]
        km, kv = _jax.random.split(_jax.random.fold_in(key, i + 1))
        out[name] = w
        out["m_" + name] = s * _jax.random.normal(km, w.shape, _jnp.float32)
        out["v_" + name] = (s * s) * _jax.random.uniform(kv, w.shape, _jnp.float32, 0.5, 1.5)
    if N_MICROBATCH > 1:
        for name, axis in PER_EXAMPLE_BATCH_AXIS.items():
            out[name] = _to_microbatches(out[name], axis)
    return {'x': out['x'], 'mem': out['mem'], 'g_pre': out['g_pre'], 'w_in': out['w_in'], 'w_pool': out['w_pool'], 'pool_scale': out['pool_scale'], 'a_re': out['a_re'], 'a_im': out['a_im'], 'log_dt': out['log_dt'], 'b_re': out['b_re'], 'b_im': out['b_im'], 'c_re': out['c_re'], 'c_im': out['c_im'], 'd_skip': out['d_skip'], 'w_glu': out['w_glu'], 'g_mem': out['g_mem'], 'w_kv': out['w_kv'], 'w_out': out['w_out'], 'g_post': out['g_post'], 'loss_target': out['loss_target'], 'm_g_pre': out['m_g_pre'], 'm_w_in': out['m_w_in'], 'm_w_pool': out['m_w_pool'], 'm_pool_scale': out['m_pool_scale'], 'm_a_re': out['m_a_re'], 'm_a_im': out['m_a_im'], 'm_log_dt': out['m_log_dt'], 'm_b_re': out['m_b_re'], 'm_b_im': out['m_b_im'], 'm_c_re': out['m_c_re'], 'm_c_im': out['m_c_im'], 'm_d_skip': out['m_d_skip'], 'm_w_glu': out['m_w_glu'], 'm_g_mem': out['m_g_mem'], 'm_w_kv': out['m_w_kv'], 'm_w_out': out['m_w_out'], 'm_g_post': out['m_g_post'], 'v_g_pre': out['v_g_pre'], 'v_w_in': out['v_w_in'], 'v_w_pool': out['v_w_pool'], 'v_pool_scale': out['v_pool_scale'], 'v_a_re': out['v_a_re'], 'v_a_im': out['v_a_im'], 'v_log_dt': out['v_log_dt'], 'v_b_re': out['v_b_re'], 'v_b_im': out['v_b_im'], 'v_c_re': out['v_c_re'], 'v_c_im': out['v_c_im'], 'v_d_skip': out['v_d_skip'], 'v_w_glu': out['v_w_glu'], 'v_g_mem': out['v_g_mem'], 'v_w_kv': out['v_w_kv'], 'v_w_out': out['v_w_out'], 'v_g_post': out['v_g_post']}


def _loss(weights, diff, rest, loss_target):
    with _jax.named_scope("forward"):
        args = {**rest, TWIN_DIFF_INPUT: diff, **{k: w.astype(_WEIGHT_DTYPES[k]) for k, w in weights.items()}}
        y = _forward(args)
    with _jax.named_scope("loss_head"):
        err = _jnp.square(y.astype(_jnp.float32) - loss_target)
        return 0.5 * _jnp.sum(_jnp.mean(err, axis=-1)) if err.ndim else 0.5 * err


def _adamw(w, g, m, v):
    m = ADAM_B1 * m + (1.0 - ADAM_B1) * g
    v = ADAM_B2 * v + (1.0 - ADAM_B2) * _jnp.square(g)
    m_hat = m / (1.0 - ADAM_B1 ** ADAM_STEP)
    v_hat = v / (1.0 - ADAM_B2 ** ADAM_STEP)
    delta = -ADAM_LR * (m_hat / (_jnp.sqrt(v_hat) + ADAM_EPS) + ADAM_WD * w)
    return delta, m, v


def reference(x, mem, g_pre, w_in, w_pool, pool_scale, a_re, a_im, log_dt, b_re, b_im, c_re, c_im, d_skip, w_glu, g_mem, w_kv, w_out, g_post, loss_target, m_g_pre, m_w_in, m_w_pool, m_pool_scale, m_a_re, m_a_im, m_log_dt, m_b_re, m_b_im, m_c_re, m_c_im, m_d_skip, m_w_glu, m_g_mem, m_w_kv, m_w_out, m_g_post, v_g_pre, v_w_in, v_w_pool, v_pool_scale, v_a_re, v_a_im, v_log_dt, v_b_re, v_b_im, v_c_re, v_c_im, v_d_skip, v_w_glu, v_g_mem, v_w_kv, v_w_out, v_g_post):
    given = dict(x=x, mem=mem, g_pre=g_pre, w_in=w_in, w_pool=w_pool, pool_scale=pool_scale, a_re=a_re, a_im=a_im, log_dt=log_dt, b_re=b_re, b_im=b_im, c_re=c_re, c_im=c_im, d_skip=d_skip, w_glu=w_glu, g_mem=g_mem, w_kv=w_kv, w_out=w_out, g_post=g_post, loss_target=loss_target, m_g_pre=m_g_pre, m_w_in=m_w_in, m_w_pool=m_w_pool, m_pool_scale=m_pool_scale, m_a_re=m_a_re, m_a_im=m_a_im, m_log_dt=m_log_dt, m_b_re=m_b_re, m_b_im=m_b_im, m_c_re=m_c_re, m_c_im=m_c_im, m_d_skip=m_d_skip, m_w_glu=m_w_glu, m_g_mem=m_g_mem, m_w_kv=m_w_kv, m_w_out=m_w_out, m_g_post=m_g_post, v_g_pre=v_g_pre, v_w_in=v_w_in, v_w_pool=v_w_pool, v_pool_scale=v_pool_scale, v_a_re=v_a_re, v_a_im=v_a_im, v_log_dt=v_log_dt, v_b_re=v_b_re, v_b_im=v_b_im, v_c_re=v_c_re, v_c_im=v_c_im, v_d_skip=v_d_skip, v_w_glu=v_w_glu, v_g_mem=v_g_mem, v_w_kv=v_w_kv, v_w_out=v_w_out, v_g_post=v_g_post)
    weights = {n: given[n] for n in TWIN_WEIGHTS}
    shared = {n: given[n] for n in SHARED_INPUTS}
    per_example = {n: given[n] for n in ['x', 'mem']}
    grad_fn = _jax.value_and_grad(_loss, argnums=(0, 1))

    def one_microbatch(ex, loss_target):
        ex = dict(ex)
        diff = ex.pop(TWIN_DIFF_INPUT)
        return grad_fn(weights, diff, {**shared, **ex}, loss_target)

    if N_MICROBATCH == 1:
        loss, (grad_w, grad_x) = one_microbatch(per_example, given["loss_target"])
    else:
        def body(carry, xs):
            loss_sum, grad_sum = carry
            l_k, (gw_k, gx_k) = one_microbatch(xs[0], xs[1])
            with _jax.named_scope("update"):
                return (loss_sum + l_k, _jax.tree.map(_jnp.add, grad_sum, gw_k)), gx_k

        init = (_jnp.zeros((), _jnp.float32), _jax.tree.map(_jnp.zeros_like, weights))
        (loss, grad_w), grad_x = _jax.lax.scan(body, init, (per_example, given["loss_target"]))
    with _jax.named_scope("update"):
        delta_w, new_m, new_v = {}, {}, {}
        for n in TWIN_WEIGHTS:
            delta_w[n], new_m[n], new_v[n] = _adamw(weights[n], grad_w[n], given["m_" + n], given["v_" + n])
    return (loss, grad_x, *[grad_w[n] for n in TWIN_WEIGHTS], *[delta_w[n] for n in TWIN_WEIGHTS],
            *[new_m[n] for n in TWIN_WEIGHTS], *[new_v[n] for n in TWIN_WEIGHTS])
```

```python
import functools
import math

import jax
import jax.numpy as jnp
from jax import lax
from jax.experimental import pallas as pl
from jax.experimental.pallas import tpu as pltpu

F32 = jnp.float32
BF16 = jnp.bfloat16
MESH = pl.DeviceIdType.MESH

N_DEV = 8
NCH = 8
SUBLANES = 8
LANES = 128
VMEM_LIMIT = 56 * 1024 * 1024

EPS = 1e-6
POOL_WINDOWS = (2, 4, 8, 16)
MEM_HEADS = 4
SSM_GROUP = 16
SSM_N = 64
ADAM_LR, ADAM_B1, ADAM_B2, ADAM_EPS, ADAM_WD, ADAM_STEP = 0.001, 0.9, 0.999, 1e-08, 0.01, 10


def _mm(a, b):
    return jnp.dot(a, b, preferred_element_type=F32)


def _mm_nt(a, b):
    return lax.dot_general(a, b, (((1,), (1,)), ((), ())), preferred_element_type=F32)


def _mm_tn(a, b):
    return lax.dot_general(a, b, (((0,), (0,)), ((), ())), preferred_element_type=F32)


def _params(*sem):
    return pltpu.CompilerParams(dimension_semantics=sem or None, vmem_limit_bytes=VMEM_LIMIT)


def _adamw(w, g, m, v):
    m = ADAM_B1 * m + (1.0 - ADAM_B1) * g
    v = ADAM_B2 * v + (1.0 - ADAM_B2) * (g * g)
    m_hat = m / (1.0 - ADAM_B1**ADAM_STEP)
    v_hat = v / (1.0 - ADAM_B2**ADAM_STEP)
    delta = -ADAM_LR * (m_hat / (jnp.sqrt(v_hat) + ADAM_EPS) + ADAM_WD * w)
    return delta, m, v


def _gelu(x):
    k = math.sqrt(2.0 / math.pi)
    return 0.5 * x * (1.0 + jnp.tanh(k * (x + 0.044715 * x * x * x)))


def _gelu_grad(x):
    k = math.sqrt(2.0 / math.pi)
    th = jnp.tanh(k * (x + 0.044715 * x * x * x))
    return 0.5 * (1.0 + th) + 0.5 * x * (1.0 - th * th) * (k * (1.0 + 3.0 * 0.044715 * x * x))


def _place():
    return lax.axis_index("x"), lax.axis_index("y"), lax.axis_index("c")


def _gather_body(n, out_dtype, epilogue, n_extra_in, n_extra_out):
    def body(*refs):
        ins = refs[:n]
        extra_in = refs[n : n + n_extra_in]
        outs = refs[n + n_extra_in : 2 * n + n_extra_in]
        extra_out = refs[2 * n + n_extra_in : 2 * n + n_extra_in + n_extra_out]
        send_sems, recv_sems = refs[2 * n + n_extra_in + n_extra_out :]
        x, y, c = _place()
        me, sibling = (x, y, c), (x, y, 1 - c)
        chips = [(1 - x, y), (x, 1 - y), (1 - x, 1 - y)]

        def slot(px, py, pc):
            return 4 * px + 2 * py + pc

        def copy(a, k, block, to):
            ref = outs[a].at[slot(*block)]
            return pltpu.make_async_remote_copy(
                src_ref=ref, dst_ref=ref, send_sem=send_sems.at[a, k], recv_sem=recv_sems.at[a, k],
                device_id=to, device_id_type=MESH)

        for a in range(n):
            outs[a][slot(*me)] = ins[a][...].astype(out_dtype)
        first = []
        for a in range(n):
            first.append(copy(a, 0, me, sibling))
            first += [copy(a, 1 + j, me, (*chip, c)) for j, chip in enumerate(chips)]
        for cp in first:
            cp.start()
        passed = []
        for j, chip in enumerate(chips):
            for a in range(n):
                copy(a, 1 + j, (*chip, c), me).wait_recv()
                cp = copy(a, 4 + j, (*chip, c), sibling)
                cp.start()
                passed.append(cp)
        for a in range(n):
            copy(a, 0, sibling, me).wait_recv()
            for j, chip in enumerate(chips):
                copy(a, 4 + j, (*chip, 1 - c), me).wait_recv()
        for cp in first + passed:
            cp.wait_send()
        if epilogue is not None:
            epilogue(outs, extra_in, extra_out)

    return body


def _gather_weights(blocks):
    n = len(blocks)
    vmem = pl.BlockSpec(memory_space=pltpu.VMEM)
    return pl.pallas_call(
        _gather_body(n, BF16, None, 0, 0),
        name="gather_weights",
        out_shape=[jax.ShapeDtypeStruct((N_DEV, *b.shape), BF16) for b in blocks],
        in_specs=[vmem] * n,
        out_specs=[vmem] * n,
        scratch_shapes=[pltpu.SemaphoreType.DMA((n, 7)), pltpu.SemaphoreType.DMA((n, 7))],
        compiler_params=_params(),
    )(*blocks)


def _reduce_small(part, w, m, v):
    rows = part.shape[0]

    def epilogue(outs, extra_in, extra_out):
        gathered = outs[0]
        w_ref, m_ref, v_ref = extra_in
        g_ref, d_ref, nm_ref, nv_ref = extra_out
        chunk = SUBLANES

        def step(i, _):
            r = pl.ds(pl.multiple_of(i * chunk, chunk), chunk)
            g = gathered[0, r, :]
            for d in range(1, N_DEV):
                g = g + gathered[d, r, :]
            delta, nm, nv = _adamw(w_ref[r, :], g, m_ref[r, :], v_ref[r, :])
            g_ref[r, :] = g
            d_ref[r, :] = delta
            nm_ref[r, :] = nm
            nv_ref[r, :] = nv
            return 0

        lax.fori_loop(0, rows // chunk, step, 0)

    vmem = pl.BlockSpec(memory_space=pltpu.VMEM)
    flat = jax.ShapeDtypeStruct((rows, LANES), F32)
    res = pl.pallas_call(
        _gather_body(1, F32, epilogue, 3, 4),
        name="reduce_small",
        out_shape=[jax.ShapeDtypeStruct((N_DEV, rows, LANES), F32), flat, flat, flat, flat],
        in_specs=[vmem] * 4,
        out_specs=[vmem] * 5,
        scratch_shapes=[pltpu.SemaphoreType.DMA((1, 7)), pltpu.SemaphoreType.DMA((1, 7))],
        compiler_params=_params(),
    )(part, w, m, v)
    return res[1:]


def _reduce_big(parts, ws, ms, vs):
    n = len(parts)
    parts4 = [p.reshape(4, 2, *p.shape[1:]) for p in parts]
    blks = [p.shape[1:] for p in parts]

    def body(*refs):
        part = refs[:n]
        w_in, m_in, v_in = refs[n : 2 * n], refs[2 * n : 3 * n], refs[3 * n : 4 * n]
        outs = refs[4 * n : 8 * n]
        own, r1, r2 = refs[8 * n : 9 * n], refs[9 * n : 10 * n], refs[10 * n : 11 * n]
        s1_send, s1_recv, s2_send, s2_recv, loc = refs[11 * n :]
        x, y, c = _place()
        sibling = (x, y, 1 - c)
        chips = [(1 - x, y), (x, 1 - y), (1 - x, 1 - y)]

        def rowwise(rows, fn):
            chunk = math.gcd(rows, 128)

            def step(i, _):
                fn(pl.ds(pl.multiple_of(i * chunk, chunk), chunk))
                return 0

            lax.fori_loop(0, rows // chunk, step, 0)

        stage1, local = [], []
        for a in range(n):
            cp = pltpu.make_async_remote_copy(
                src_ref=part[a].at[:, 1 - c], dst_ref=r1[a], send_sem=s1_send.at[a], recv_sem=s1_recv.at[a],
                device_id=sibling, device_id_type=MESH)
            cp.start()
            stage1.append(cp)
            lc = pltpu.make_async_copy(part[a].at[:, c], own[a], loc.at[a])
            lc.start()
            local.append(lc)
        stage2 = []
        for a in range(n):
            local[a].wait()
            stage1[a].wait_recv()
            for chip in range(4):

                def add(r, a=a, chip=chip):
                    own[a][chip, r, :] = own[a][chip, r, :] + r1[a][chip, r, :]

                rowwise(blks[a][0], add)
            for k, chip in enumerate(chips):
                cp = pltpu.make_async_remote_copy(
                    src_ref=own[a].at[2 * chip[0] + chip[1]], dst_ref=r2[a].at[k],
                    send_sem=s2_send.at[a, k], recv_sem=s2_recv.at[a, k],
                    device_id=(*chip, c), device_id_type=MESH)
                cp.start()
                stage2.append(cp)
        for a in range(n):
            for k, chip in enumerate(chips):
                stage2[3 * a + k].wait_recv()
            g_ref, d_ref, nm_ref, nv_ref = outs[4 * a : 4 * a + 4]

            def update(r, a=a, g_ref=g_ref, d_ref=d_ref, nm_ref=nm_ref, nv_ref=nv_ref):
                g = own[a][2 * x + y, r, :] + r2[a][0, r, :] + r2[a][1, r, :] + r2[a][2, r, :]
                delta, nm, nv = _adamw(w_in[a][r, :], g, m_in[a][r, :], v_in[a][r, :])
                g_ref[r, :] = g
                d_ref[r, :] = delta
                nm_ref[r, :] = nm
                nv_ref[r, :] = nv

            rowwise(blks[a][0], update)
        for cp in stage1 + stage2:
            cp.wait_send()

    vmem = pl.BlockSpec(memory_space=pltpu.VMEM)
    hbm = pl.BlockSpec(memory_space=pl.ANY)
    out_shape = []
    for b in blks:
        out_shape += [jax.ShapeDtypeStruct(b, F32)] * 4
    scratch = (
        [pltpu.VMEM((4, *b), F32) for b in blks]
        + [pltpu.VMEM((4, *b), F32) for b in blks]
        + [pltpu.VMEM((3, *b), F32) for b in blks]
        + [pltpu.SemaphoreType.DMA((n,)), pltpu.SemaphoreType.DMA((n,)), pltpu.SemaphoreType.DMA((n, 3)),
           pltpu.SemaphoreType.DMA((n, 3)), pltpu.SemaphoreType.DMA((n,))]
    )
    res = pl.pallas_call(
        body,
        name="reduce_big",
        out_shape=out_shape,
        in_specs=[hbm] * n + [vmem] * (3 * n),
        out_specs=[vmem] * (4 * n),
        scratch_shapes=scratch,
        compiler_params=_params(),
    )(*parts4, *ws, *ms, *vs)
    return [tuple(res[4 * a : 4 * a + 4]) for a in range(n)]


def _rms(x):
    return lax.rsqrt(jnp.mean(x * x, axis=-1, keepdims=True) + EPS)


def _in_proj(x2, g_pre, w_in_g, batch, cl, pool_w, ssm_w, att_w):
    tokens, d = x2.shape
    nb = w_in_g.shape[2]
    mix = pool_w + ssm_w + att_w
    half = N_DEV // 2
    assert half * nb == mix and nb == 256 and pool_w == 384 and ssm_w == 384 and att_w == 256

    def body(x_ref, g_ref, w_ref, up_ref, us_ref, q_ref, gate_ref):
        x = x_ref[...]
        h = (x * _rms(x) * g_ref[...]).astype(BF16)
        p = [_mm(h, w_ref[j]) for j in range(half)]
        up_ref[:, 0:256] = p[0]
        up_ref[:, 256:384] = p[1][:, 0:128]
        us_ref[:, 0:128] = p[1][:, 128:256]
        us_ref[:, 128:384] = p[2]
        q_ref[...] = p[3]
        for j in range(half):
            gate_ref[:, j * nb : (j + 1) * nb] = _mm(h, w_ref[half + j])

    return pl.pallas_call(
        body,
        name="in_proj",
        grid=(tokens // cl,),
        in_specs=[
            pl.BlockSpec((cl, d), lambda i: (i, 0)),
            pl.BlockSpec((1, d), lambda i: (0, 0)),
            pl.BlockSpec((N_DEV, d, nb), lambda i: (0, 0, 0)),
        ],
        out_specs=[
            pl.BlockSpec((cl, pool_w), lambda i: (i, 0)),
            pl.BlockSpec((None, cl, ssm_w), lambda i: (i // NCH, 0, i % NCH)),
            pl.BlockSpec((cl, att_w), lambda i: (i, 0)),
            pl.BlockSpec((cl, mix), lambda i: (i, 0)),
        ],
        out_shape=[
            jax.ShapeDtypeStruct((tokens, pool_w), F32),
            jax.ShapeDtypeStruct((batch, cl, NCH * ssm_w), F32),
            jax.ShapeDtypeStruct((tokens, att_w), F32),
            jax.ShapeDtypeStruct((tokens, mix), F32),
        ],
        compiler_params=_params("arbitrary"),
    )(x2, g_pre, w_in_g)


def _in_proj_bwd(x2, dres, du_pool, du_ssm_p, dq, dgate, g_pre, w_in_g, cl):
    tokens, d = x2.shape
    nb = w_in_g.shape[2]
    pool_w, att_w, mix = du_pool.shape[1], dq.shape[1], dgate.shape[1]
    ssm_w = du_ssm_p.shape[2] // NCH
    half = N_DEV // 2
    n_tiles = tokens // cl

    def body(x_ref, dres_ref, dup_ref, dus_ref, dq_ref, dgate_ref, g_ref, w_ref, gx_ref, gw_hbm, gg_ref, acc, sem):
        i = pl.program_id(0)

        @pl.when(i == 0)
        def _():
            acc[...] = jnp.zeros_like(acc)
            gg_ref[...] = jnp.zeros_like(gg_ref)

        x = x_ref[...]
        r = _rms(x)
        xn = x * r
        g = g_ref[...]
        h = (xn * g).astype(BF16)
        dval = jnp.concatenate([dup_ref[...], dus_ref[...], dq_ref[...]], axis=1)
        dh = jnp.zeros((cl, d), F32)
        for j in range(N_DEV):
            src = dval if j < half else dgate_ref[...]
            jj = j % half
            dp = src[:, jj * nb : (jj + 1) * nb].astype(BF16)
            dh = dh + _mm_nt(dp, w_ref[j])
            acc[j] += _mm_tn(h, dp)
        gg_ref[...] += jnp.sum(dh * xn, axis=0, keepdims=True)
        dxn = dh * g
        gx_ref[...] = dres_ref[...] + r * (dxn - xn * jnp.mean(dxn * xn, axis=-1, keepdims=True))

        @pl.when(i == n_tiles - 1)
        def _():
            cp = pltpu.make_async_copy(acc, gw_hbm, sem)
            cp.start()
            cp.wait()

    return pl.pallas_call(
        body,
        name="in_proj_bwd",
        grid=(n_tiles,),
        in_specs=[
            pl.BlockSpec((cl, d), lambda i: (i, 0)),
            pl.BlockSpec((cl, d), lambda i: (i, 0)),
            pl.BlockSpec((cl, pool_w), lambda i: (i, 0)),
            pl.BlockSpec((None, cl, ssm_w), lambda i: (i // NCH, 0, i % NCH)),
            pl.BlockSpec((cl, att_w), lambda i: (i, 0)),
            pl.BlockSpec((cl, mix), lambda i: (i, 0)),
            pl.BlockSpec((1, d), lambda i: (0, 0)),
            pl.BlockSpec((N_DEV, d, nb), lambda i: (0, 0, 0)),
        ],
        out_specs=[
            pl.BlockSpec((cl, d), lambda i: (i, 0)),
            pl.BlockSpec(memory_space=pl.ANY),
            pl.BlockSpec((1, d), lambda i: (0, 0)),
        ],
        out_shape=[
            jax.ShapeDtypeStruct((tokens, d), F32),
            jax.ShapeDtypeStruct((N_DEV, d, nb), F32),
            jax.ShapeDtypeStruct((1, d), F32),
        ],
        scratch_shapes=[pltpu.VMEM((N_DEV, d, nb), F32), pltpu.SemaphoreType.DMA],
        compiler_params=_params("arbitrary"),
    )(x2, dres, du_pool, du_ssm_p, dq, dgate, g_pre, w_in_g)


def _pool_geometry(seq, width):
    gw = width // len(POOL_WINDOWS)
    col = lax.broadcasted_iota(jnp.int32, (1, width), 1)
    win = jnp.full((1, width), float(POOL_WINDOWS[-1]), F32)
    for gi in range(len(POOL_WINDOWS) - 2, -1, -1):
        win = jnp.where(col < (gi + 1) * gw, float(POOL_WINDOWS[gi]), win)
    row = lax.broadcasted_iota(jnp.int32, (seq, width), 0)
    cnt = jnp.minimum((row + 1).astype(F32), win)
    return win, row, cnt


def _window_sums(a, row, win, seq, back):
    sums = []
    s, k = a, 1
    while k < POOL_WINDOWS[-1]:
        if back:
            shifted = jnp.where(row < seq - k, pltpu.roll(s, seq - k, 0), 0.0)
        else:
            shifted = jnp.where(row >= k, pltpu.roll(s, k, 0), 0.0)
        s = s + shifted
        k *= 2
        sums.append((k, s))
    out = sums[-1][1]
    for k, s in reversed(sums[:-1]):
        out = jnp.where(win <= float(k), s, out)
    return out


def _pool_fwd(u2, wp_blk, scale, batch, seq):
    width = u2.shape[1]

    def body(u_ref, w_ref, s_ref, y_ref):
        u = u_ref[...]
        win, row, cnt = _pool_geometry(seq, width)
        diff = _window_sums(u, row, win, seq, False) / cnt - u
        y_ref[...] = _mm(diff.astype(BF16), w_ref[...]) * s_ref[...]

    return pl.pallas_call(
        body,
        name="pool_fwd",
        grid=(batch,),
        in_specs=[
            pl.BlockSpec((seq, width), lambda b: (b, 0)),
            pl.BlockSpec((width, width), lambda b: (0, 0)),
            pl.BlockSpec((1, width), lambda b: (0, 0)),
        ],
        out_specs=pl.BlockSpec((seq, width), lambda b: (b, 0)),
        out_shape=jax.ShapeDtypeStruct(u2.shape, F32),
        compiler_params=_params("arbitrary"),
    )(u2, wp_blk, scale)


def _pool_bwd(u2, dy2, wp_blk, scale, batch, seq):
    width = u2.shape[1]

    def body(u_ref, dy_ref, w_ref, s_ref, du_ref, gw_ref, gs_ref):
        @pl.when(pl.program_id(0) == 0)
        def _():
            gw_ref[...] = jnp.zeros_like(gw_ref)
            gs_ref[...] = jnp.zeros_like(gs_ref)

        u = u_ref[...]
        dy = dy_ref[...]
        win, row, cnt = _pool_geometry(seq, width)
        diff = (_window_sums(u, row, win, seq, False) / cnt - u).astype(BF16)
        gs_ref[...] += jnp.sum(dy * _mm(diff, w_ref[...]), axis=0, keepdims=True)
        dys = (dy * s_ref[...]).astype(BF16)
        gw_ref[...] += _mm_tn(diff, dys)
        dd = _mm_nt(dys, w_ref[...])
        du_ref[...] = _window_sums(dd / cnt, row, win, seq, True) - dd

    return pl.pallas_call(
        body,
        name="pool_bwd",
        grid=(batch,),
        in_specs=[
            pl.BlockSpec((seq, width), lambda b: (b, 0)),
            pl.BlockSpec((seq, width), lambda b: (b, 0)),
            pl.BlockSpec((width, width), lambda b: (0, 0)),
            pl.BlockSpec((1, width), lambda b: (0, 0)),
        ],
        out_specs=[
            pl.BlockSpec((seq, width), lambda b: (b, 0)),
            pl.BlockSpec((width, width), lambda b: (0, 0)),
            pl.BlockSpec((1, width), lambda b: (0, 0)),
        ],
        out_shape=[
            jax.ShapeDtypeStruct(u2.shape, F32),
            jax.ShapeDtypeStruct((width, width), F32),
            jax.ShapeDtypeStruct((1, width), F32),
        ],
        compiler_params=_params("arbitrary"),
    )(u2, dy2, wp_blk, scale)


def _ssm_tables(a_re, a_im, log_dt, b_re, b_im, c_re, c_im):
    groups, n_state = a_re.shape
    s = groups * n_state
    lam = lax.complex(a_re, a_im)
    lam_bar = jnp.exp(lam * jnp.exp(log_dt)[:, None])
    b_bar = ((lam_bar - 1.0) / lam)[..., None] * lax.complex(b_re, b_im)
    eye = jnp.eye(groups, dtype=F32)
    width = groups * b_re.shape[2]

    def rows_to_state(t):
        return jnp.einsum("gnc,gh->gchn", t, eye).reshape(width, s)

    def state_to_rows(t):
        return jnp.einsum("gcn,gh->hngc", t, eye).reshape(s, width)

    lam2 = jnp.stack([jnp.real(lam_bar).reshape(s), jnp.imag(lam_bar).reshape(s)])
    b_blk = jnp.concatenate([rows_to_state(jnp.real(b_bar)), rows_to_state(jnp.imag(b_bar))], axis=1)
    c_blk = jnp.concatenate([state_to_rows(c_re), -state_to_rows(c_im)], axis=0)
    return lam2, b_blk, c_blk


def _lam_power(a_re, a_im, log_dt, power, scale):
    s = a_re.size
    p = scale * jnp.exp(lax.complex(a_re, a_im) * jnp.exp(log_dt)[:, None] * power)
    return jnp.concatenate([jnp.real(p).reshape(1, s), jnp.imag(p).reshape(1, s)], axis=1)


def _scan(src_ref, dst_ref, st_ref, lam8_ref, n_groups, s, reverse, store):
    lb = 512 if s % 512 == 0 else s
    for blk in range(s // lb):
        cr, ci = pl.ds(blk * lb, lb), pl.ds(s + blk * lb, lb)
        lr = lam8_ref[:, cr]
        li = -lam8_ref[:, ci] if reverse else lam8_ref[:, ci]

        def step(i, carry, cr=cr, ci=ci, lr=lr, li=li):
            hr, hi = carry
            grp = n_groups - 1 - i if reverse else i
            rows = pl.ds(pl.multiple_of(grp * SUBLANES, SUBLANES), SUBLANES)
            nr = lr * hr - li * hi + src_ref[rows, cr]
            ni = lr * hi + li * hr + src_ref[rows, ci]
            if store:
                dst_ref[rows, cr] = nr
                dst_ref[rows, ci] = ni
            return nr, ni

        hr, hi = lax.fori_loop(0, n_groups, step, (st_ref[:, cr], st_ref[:, ci]), unroll=2)
        st_ref[:, cr] = hr
        st_ref[:, ci] = hi


def _scan_adjoint(dh_ref, bu_ref, stg_ref, ste_ref, acc_ref, lam8_ref, n_groups, s, final):
    lb = 256 if s % 256 == 0 else s
    for blk in range(s // lb):
        cr, ci = pl.ds(blk * lb, lb), pl.ds(s + blk * lb, lb)
        lr, li = lam8_ref[:, cr], -lam8_ref[:, ci]

        def step(i, carry, cr=cr, ci=ci, lr=lr, li=li):
            gr, gi, er, ei, ar, ai = carry
            grp = n_groups - 1 - i
            rows = pl.ds(pl.multiple_of(grp * SUBLANES, SUBLANES), SUBLANES)
            ner = gr + lr * er - li * ei
            nei = gi + lr * ei + li * er
            ngr = lr * gr - li * gi + dh_ref[rows, cr]
            ngi = lr * gi + li * gr + dh_ref[rows, ci]
            if final:
                br, bi = bu_ref[rows, cr], bu_ref[rows, ci]
                ar = ar + br * ner + bi * nei
                ai = ai + br * nei - bi * ner
                dh_ref[rows, cr] = ngr
                dh_ref[rows, ci] = ngi
            return ngr, ngi, ner, nei, ar, ai

        init = (stg_ref[:, cr], stg_ref[:, ci], ste_ref[:, cr], ste_ref[:, ci], acc_ref[:, cr], acc_ref[:, ci])
        gr, gi, er, ei, ar, ai = lax.fori_loop(0, n_groups, step, init, unroll=2)
        stg_ref[:, cr] = gr
        stg_ref[:, ci] = gi
        ste_ref[:, cr] = er
        ste_ref[:, ci] = ei
        if final:
            acc_ref[:, cr] = ar
            acc_ref[:, ci] = ai


def _chunk_starts(st_ref, init_ref, lcl_ref, s):
    ar, ai = lcl_ref[:, 0:s], lcl_ref[:, s : 2 * s]
    cr = jnp.zeros((1, s), F32)
    ci = jnp.zeros((1, s), F32)
    init_ref[0:1, :] = jnp.zeros((1, 2 * s), F32)
    for k in range(1, NCH):
        cr, ci = (ar * cr - ai * ci + st_ref[k - 1 : k, 0:s], ar * ci + ai * cr + st_ref[k - 1 : k, s : 2 * s])
        init_ref[k : k + 1, 0:s] = cr
        init_ref[k : k + 1, s : 2 * s] = ci


def _chunk_starts_adjoint(stg_ref, ste_ref, initg_ref, inite_ref, lcl_ref, lcl1_ref, s):
    ar, ai = lcl_ref[:, 0:s], -lcl_ref[:, s : 2 * s]
    pr, pi = lcl1_ref[:, 0:s], -lcl1_ref[:, s : 2 * s]
    zero = jnp.zeros((1, s), F32)
    gr, gi, er, ei = zero, zero, zero, zero
    initg_ref[NCH - 1 : NCH, :] = jnp.zeros((1, 2 * s), F32)
    inite_ref[NCH - 1 : NCH, :] = jnp.zeros((1, 2 * s), F32)
    for k in range(NCH - 2, -1, -1):
        ner = ste_ref[k + 1 : k + 2, 0:s] + (ar * er - ai * ei) + (pr * gr - pi * gi)
        nei = ste_ref[k + 1 : k + 2, s : 2 * s] + (ar * ei + ai * er) + (pr * gi + pi * gr)
        ngr = stg_ref[k + 1 : k + 2, 0:s] + ar * gr - ai * gi
        ngi = stg_ref[k + 1 : k + 2, s : 2 * s] + ar * gi + ai * gr
        gr, gi, er, ei = ngr, ngi, ner, nei
        initg_ref[k : k + 1, 0:s] = gr
        initg_ref[k : k + 1, s : 2 * s] = gi
        inite_ref[k : k + 1, 0:s] = er
        inite_ref[k : k + 1, s : 2 * s] = ei


def _ssm_rows(seq):
    rows = min(256, seq // 2)
    assert seq % rows == 0 and rows % SUBLANES == 0
    return rows


def _ssm_fwd(u_p, b_blk, c_blk, lam8, lcl, d_skip, w_glu):
    batch, seq, width = u_p.shape
    s = lam8.shape[1] // 2
    rows = _ssm_rows(seq)
    n_tiles = seq // rows
    n_groups = rows // SUBLANES

    def body(u_ref, b_ref, c_ref, lam_ref, lcl_ref, d_ref, wg_ref, y_ref, pre_ref, z_ref, bu, st, init):
        ph, t = pl.program_id(1), pl.program_id(2)

        @pl.when((ph == 0) & (t == 0))
        def _():
            st[...] = jnp.zeros_like(st)

        @pl.when((ph == 1) & (t == 0))
        def _():
            st[...] = init[...]

        u = u_ref[...]
        bu[...] = _mm(u.astype(BF16), b_ref[...])

        @pl.when(ph == 0)
        def _():
            _scan(bu, bu, st, lam_ref, n_groups, s, False, False)

        @pl.when((ph == 0) & (t == n_tiles - 1))
        def _():
            _chunk_starts(st, init, lcl_ref, s)

        @pl.when(ph == 1)
        def _():
            _scan(bu, bu, st, lam_ref, n_groups, s, False, True)
            pre = _mm(bu[...].astype(BF16), c_ref[...]) + d_ref[...] * u
            z = _mm(_gelu(pre).astype(BF16), wg_ref[...])
            pre_ref[...] = pre
            z_ref[...] = z
            y_ref[...] = z[:, 0:width] * jax.nn.sigmoid(z[:, width : 2 * width])

    tile = lambda b, ph, t: (b, t, 0)
    out_tile = lambda b, ph, t: (b, t * ph, 0)
    const = lambda b, ph, t: (0, 0)
    return pl.pallas_call(
        body,
        name="ssm_fwd",
        grid=(batch, 2, n_tiles),
        in_specs=[
            pl.BlockSpec((None, rows, width), tile),
            pl.BlockSpec(b_blk.shape, const),
            pl.BlockSpec(c_blk.shape, const),
            pl.BlockSpec(lam8.shape, const),
            pl.BlockSpec(lcl.shape, const),
            pl.BlockSpec(d_skip.shape, const),
            pl.BlockSpec(w_glu.shape, const),
        ],
        out_specs=[
            pl.BlockSpec((None, rows, width), out_tile),
            pl.BlockSpec((None, rows, width), out_tile),
            pl.BlockSpec((None, rows, 2 * width), out_tile),
        ],
        out_shape=[
            jax.ShapeDtypeStruct((batch, seq, width), F32),
            jax.ShapeDtypeStruct((batch, seq, width), F32),
            jax.ShapeDtypeStruct((batch, seq, 2 * width), F32),
        ],
        scratch_shapes=[
            pltpu.VMEM((rows, 2 * s), F32),
            pltpu.VMEM((SUBLANES, 2 * s), F32),
            pltpu.VMEM((SUBLANES, 2 * s), F32),
        ],
        compiler_params=_params("arbitrary", "arbitrary", "arbitrary"),
    )(u_p, b_blk, c_blk, lam8, lcl, d_skip, w_glu)


def _ssm_bwd(u_p, pre_p, z_p, dy_p, b_blk, b_blk_t, c_blk_t, lam8, lcl, lcl1, d_skip, w_glu):
    batch, seq, width = u_p.shape
    s = lam8.shape[1] // 2
    rows = _ssm_rows(seq)
    n_tiles = seq // rows
    n_groups = rows // SUBLANES

    def body(u_ref, pre_ref, z_ref, dy_ref, b_ref, bt_ref, ct_ref, lam_ref, lcl_ref, lcl1_ref, d_ref, wg_ref,
             du_ref, gb_hbm, gc_hbm, gwg_ref, gd_ref, glam_ref,
             bu, dh, dpre_all, st, init, stg, ste, initg, inite, acc, gb_acc, gc_acc, sems):
        b, ph, t = pl.program_id(0), pl.program_id(1), pl.program_id(2)
        first = (b == 0) & (ph == 0) & (t == 0)
        last = (b == batch - 1) & (ph == 3) & (t == n_tiles - 1)
        tile = jnp.where(ph < 2, t, n_tiles - 1 - t)
        tile_rows = pl.ds(pl.multiple_of(tile * rows, rows), rows)

        @pl.when(first)
        def _():
            acc[...] = jnp.zeros_like(acc)
            gb_acc[...] = jnp.zeros_like(gb_acc)
            gc_acc[...] = jnp.zeros_like(gc_acc)
            gwg_ref[...] = jnp.zeros_like(gwg_ref)
            gd_ref[...] = jnp.zeros_like(gd_ref)

        u = u_ref[...]

        @pl.when((ph == 0) | (ph == 1) | (ph == 3))
        def _():
            bu[...] = _mm(u.astype(BF16), b_ref[...])

        @pl.when(ph == 0)
        def _():
            @pl.when(t == 0)
            def _():
                st[...] = jnp.zeros_like(st)

            _scan(bu, bu, st, lam_ref, n_groups, s, False, False)

            @pl.when(t == n_tiles - 1)
            def _():
                _chunk_starts(st, init, lcl_ref, s)

        @pl.when(ph == 1)
        def _():
            @pl.when(t == 0)
            def _():
                st[...] = init[...]

            _scan(bu, bu, st, lam_ref, n_groups, s, False, True)
            z = z_ref[...]
            dy = dy_ref[...]
            pre = pre_ref[...]
            z1, sig = z[:, 0:width], jax.nn.sigmoid(z[:, width : 2 * width])
            dz = jnp.concatenate([dy * sig, dy * z1 * sig * (1.0 - sig)], axis=1).astype(BF16)
            gwg_ref[...] += _mm_tn(_gelu(pre).astype(BF16), dz)
            dpre = _mm_nt(dz, wg_ref[...]) * _gelu_grad(pre)
            dpre_all[tile_rows, :] = dpre
            gd_ref[...] += jnp.sum(dpre * u, axis=0, keepdims=True)
            gc_acc[...] += _mm_tn(bu[...].astype(BF16), dpre.astype(BF16))

        @pl.when(ph >= 2)
        def _():
            dh[...] = _mm(dpre_all[tile_rows, :].astype(BF16), ct_ref[...])

        @pl.when(ph == 2)
        def _():
            @pl.when(t == 0)
            def _():
                stg[...] = jnp.zeros_like(stg)
                ste[...] = jnp.zeros_like(ste)

            _scan_adjoint(dh, bu, stg, ste, acc, lam_ref, n_groups, s, False)

            @pl.when(t == n_tiles - 1)
            def _():
                _chunk_starts_adjoint(stg, ste, initg, inite, lcl_ref, lcl1_ref, s)

        @pl.when(ph == 3)
        def _():
            @pl.when(t == 0)
            def _():
                stg[...] = initg[...]
                ste[...] = inite[...]

            _scan_adjoint(dh, bu, stg, ste, acc, lam_ref, n_groups, s, True)
            g = dh[...].astype(BF16)
            du_ref[...] = _mm(g, bt_ref[...]) + dpre_all[tile_rows, :] * d_ref[...]
            gb_acc[...] += _mm_tn(u.astype(BF16), g)

        @pl.when(last)
        def _():
            glam_ref[...] = jnp.sum(acc[...], axis=0, keepdims=True)
            cb = pltpu.make_async_copy(gb_acc, gb_hbm, sems.at[0])
            cc = pltpu.make_async_copy(gc_acc, gc_hbm, sems.at[1])
            cb.start()
            cc.start()
            cb.wait()
            cc.wait()

    def tile(b, ph, t):
        return (b, jnp.where(ph < 2, t, n_tiles - 1 - t), 0)

    def out_tile(b, ph, t):
        return (b, jnp.where(ph == 3, n_tiles - 1 - t, n_tiles - 1), 0)

    const = lambda b, ph, t: (0, 0)
    full = lambda a: pl.BlockSpec(a.shape, const)
    return pl.pallas_call(
        body,
        name="ssm_bwd",
        grid=(batch, 4, n_tiles),
        in_specs=[
            pl.BlockSpec((None, rows, width), tile),
            pl.BlockSpec((None, rows, width), tile),
            pl.BlockSpec((None, rows, 2 * width), tile),
            pl.BlockSpec((None, rows, width), tile),
            full(b_blk), full(b_blk_t), full(c_blk_t), full(lam8), full(lcl), full(lcl1), full(d_skip), full(w_glu),
        ],
        out_specs=[
            pl.BlockSpec((None, rows, width), out_tile),
            pl.BlockSpec(memory_space=pl.ANY),
            pl.BlockSpec(memory_space=pl.ANY),
            pl.BlockSpec(w_glu.shape, const),
            pl.BlockSpec(d_skip.shape, const),
            pl.BlockSpec((1, 2 * s), const),
        ],
        out_shape=[
            jax.ShapeDtypeStruct((batch, seq, width), F32),
            jax.ShapeDtypeStruct(b_blk.shape, F32),
            jax.ShapeDtypeStruct(b_blk_t.shape, F32),
            jax.ShapeDtypeStruct(w_glu.shape, F32),
            jax.ShapeDtypeStruct(d_skip.shape, F32),
            jax.ShapeDtypeStruct((1, 2 * s), F32),
        ],
        scratch_shapes=[
            pltpu.VMEM((rows, 2 * s), F32),
            pltpu.VMEM((rows, 2 * s), F32),
            pltpu.VMEM((seq, width), F32),
        ]
        + [pltpu.VMEM((SUBLANES, 2 * s), F32)] * 7
        + [pltpu.VMEM(b_blk.shape, F32), pltpu.VMEM(b_blk_t.shape, F32), pltpu.SemaphoreType.DMA((2,))],
        compiler_params=_params("arbitrary", "arbitrary", "arbitrary"),
    )(u_p, pre_p, z_p, dy_p, b_blk, b_blk_t, c_blk_t, lam8, lcl, lcl1, d_skip, w_glu)


def _kv_fwd(mem, g_mem, w_kv):
    batch, n_mem, d = mem.shape
    kvw = w_kv.shape[1]

    def body(mem_ref, g_ref, w_ref, kv_ref):
        m = mem_ref[...]
        kv_ref[...] = _mm((m * _rms(m) * g_ref[...]).astype(BF16), w_ref[...])

    return pl.pallas_call(
        body,
        name="kv_fwd",
        grid=(batch,),
        in_specs=[
            pl.BlockSpec((None, n_mem, d), lambda b: (b, 0, 0)),
            pl.BlockSpec((1, d), lambda b: (0, 0)),
            pl.BlockSpec((d, kvw), lambda b: (0, 0)),
        ],
        out_specs=pl.BlockSpec((None, n_mem, kvw), lambda b: (b, 0, 0)),
        out_shape=jax.ShapeDtypeStruct((batch, n_mem, kvw), F32),
        compiler_params=_params("arbitrary"),
    )(mem, g_mem, w_kv)


def _kv_bwd(mem, dkv, g_mem, w_kv):
    batch, n_mem, d = mem.shape
    kvw = w_kv.shape[1]

    def body(mem_ref, dkv_ref, g_ref, w_ref, gw_ref, gg_ref):
        @pl.when(pl.program_id(0) == 0)
        def _():
            gw_ref[...] = jnp.zeros_like(gw_ref)
            gg_ref[...] = jnp.zeros_like(gg_ref)

        m = mem_ref[...]
        mn = m * _rms(m)
        dkv_b = dkv_ref[...].astype(BF16)
        gw_ref[...] += _mm_tn((mn * g_ref[...]).astype(BF16), dkv_b)
        gg_ref[...] += jnp.sum(_mm_nt(dkv_b, w_ref[...]) * mn, axis=0, keepdims=True)

    return pl.pallas_call(
        body,
        name="kv_bwd",
        grid=(batch,),
        in_specs=[
            pl.BlockSpec((None, n_mem, d), lambda b: (b, 0, 0)),
            pl.BlockSpec((None, n_mem, kvw), lambda b: (b, 0, 0)),
            pl.BlockSpec((1, d), lambda b: (0, 0)),
            pl.BlockSpec((d, kvw), lambda b: (0, 0)),
        ],
        out_specs=[pl.BlockSpec((d, kvw), lambda b: (0, 0)), pl.BlockSpec((1, d), lambda b: (0, 0))],
        out_shape=[jax.ShapeDtypeStruct((d, kvw), F32), jax.ShapeDtypeStruct((1, d), F32)],
        compiler_params=_params("arbitrary"),
    )(mem, dkv, g_mem, w_kv)


def _tail(x2, target2, gate, y_pool, y_ssm_p, q, kv, w_out, g_post, cl):
    tokens, d = x2.shape
    pool_w, att_w, mix = y_pool.shape[1], q.shape[1], gate.shape[1]
    ssm_w = y_ssm_p.shape[2] // NCH
    batch, n_mem, kvw = kv.shape
    hd = att_w // MEM_HEADS
    n_tiles = tokens // cl
    qk_scale = hd**-0.5

    def body(x_ref, tg_ref, gate_ref, yp_ref, ys_ref, q_ref, kv_ref, w_ref, g_ref,
             dres_ref, dgate_ref, dyp_ref, dys_ref, dq_ref, dkv_ref, gw_hbm, gg_ref, loss_ref, acc, sem):
        i = pl.program_id(0)

        @pl.when(i == 0)
        def _():
            acc[...] = jnp.zeros_like(acc)
            gg_ref[...] = jnp.zeros_like(gg_ref)
            loss_ref[...] = jnp.zeros_like(loss_ref)

        @pl.when(i % NCH == 0)
        def _():
            dkv_ref[...] = jnp.zeros_like(dkv_ref)

        q = q_ref[...]
        k = kv_ref[:, 0:att_w].astype(BF16)
        v = kv_ref[:, att_w : 2 * att_w].astype(BF16)
        lane = lax.broadcasted_iota(jnp.int32, (1, att_w), 1)
        heads = [(lane >= h * hd) & (lane < (h + 1) * hd) for h in range(MEM_HEADS)]
        probs, q_heads = [], []
        att = jnp.zeros((cl, att_w), F32)
        for mask in heads:
            qh = jnp.where(mask, q, 0.0).astype(BF16)
            sc = _mm_nt(qh, k) * qk_scale
            e = jnp.exp(sc - jnp.max(sc, axis=-1, keepdims=True))
            p = e / jnp.sum(e, axis=-1, keepdims=True)
            att = att + jnp.where(mask, _mm(p.astype(BF16), v), 0.0)
            probs.append(p)
            q_heads.append(qh)

        ycat = jnp.concatenate([yp_ref[...], ys_ref[...], att], axis=1)
        gate = gate_ref[...]
        sig = jax.nn.sigmoid(gate)
        silu = gate * sig
        yg = (ycat * silu).astype(BF16)
        out = _mm(yg, w_ref[...])
        r = _rms(out)
        on = out * r
        g = g_ref[...]
        x = x_ref[...]
        err = x + on * g - tg_ref[...]
        loss_ref[...] += 0.5 * jnp.sum(jnp.mean(err * err, axis=-1, keepdims=True), axis=0, keepdims=True)
        dres = err * (1.0 / d)
        dres_ref[...] = dres
        gg_ref[...] += jnp.sum(dres * on, axis=0, keepdims=True)
        don = dres * g
        dout = (r * (don - on * jnp.mean(don * on, axis=-1, keepdims=True))).astype(BF16)
        acc[...] += _mm_tn(yg, dout)
        dyg = _mm_nt(dout, w_ref[...])
        dgate_ref[...] = dyg * ycat * (sig * (1.0 + gate * (1.0 - sig)))
        dycat = dyg * silu
        dyp_ref[...] = dycat[:, 0:pool_w]
        dys_ref[...] = dycat[:, pool_w : pool_w + ssm_w]
        datt = dycat[:, pool_w + ssm_w : mix]

        dq = jnp.zeros((cl, att_w), F32)
        dk = jnp.zeros((n_mem, att_w), F32)
        dv = jnp.zeros((n_mem, att_w), F32)
        for mask, p, qh in zip(heads, probs, q_heads):
            doh = jnp.where(mask, datt, 0.0).astype(BF16)
            dp = _mm_nt(doh, v)
            ds = (p * (dp - jnp.sum(p * dp, axis=-1, keepdims=True)) * qk_scale).astype(BF16)
            dq = dq + jnp.where(mask, _mm(ds, k), 0.0)
            dk = dk + _mm_tn(ds, qh)
            dv = dv + _mm_tn(p.astype(BF16), doh)
        dq_ref[...] = dq
        dkv_ref[:, 0:att_w] += dk
        dkv_ref[:, att_w : 2 * att_w] += dv

        @pl.when(i == n_tiles - 1)
        def _():
            cp = pltpu.make_async_copy(acc, gw_hbm, sem)
            cp.start()
            cp.wait()

    tok = lambda w: pl.BlockSpec((cl, w), lambda i: (i, 0))
    chunked = pl.BlockSpec((None, cl, ssm_w), lambda i: (i // NCH, 0, i % NCH))
    per_batch = pl.BlockSpec((None, n_mem, kvw), lambda i: (i // NCH, 0, 0))
    return pl.pallas_call(
        body,
        name="tail",
        grid=(n_tiles,),
        in_specs=[
            tok(d), tok(d), tok(mix), tok(pool_w), chunked, tok(att_w), per_batch,
            pl.BlockSpec((mix, d), lambda i: (0, 0)),
            pl.BlockSpec((1, d), lambda i: (0, 0)),
        ],
        out_specs=[
            tok(d), tok(mix), tok(pool_w), chunked, tok(att_w), per_batch,
            pl.BlockSpec(memory_space=pl.ANY),
            pl.BlockSpec((1, d), lambda i: (0, 0)),
            pl.BlockSpec((1, 1), lambda i: (0, 0)),
        ],
        out_shape=[
            jax.ShapeDtypeStruct((tokens, d), F32),
            jax.ShapeDtypeStruct((tokens, mix), F32),
            jax.ShapeDtypeStruct((tokens, pool_w), F32),
            jax.ShapeDtypeStruct(y_ssm_p.shape, F32),
            jax.ShapeDtypeStruct((tokens, att_w), F32),
            jax.ShapeDtypeStruct(kv.shape, F32),
            jax.ShapeDtypeStruct((mix, d), F32),
            jax.ShapeDtypeStruct((1, d), F32),
            jax.ShapeDtypeStruct((1, 1), F32),
        ],
        scratch_shapes=[pltpu.VMEM((mix, d), F32), pltpu.SemaphoreType.DMA],
        compiler_params=_params("arbitrary"),
    )(x2, target2, gate, y_pool, y_ssm_p, q, kv, w_out, g_post)


def _pack(arrays):
    flat = jnp.concatenate([a.reshape(-1) for a in arrays])
    rows = -(-flat.size // (SUBLANES * LANES)) * SUBLANES
    return jnp.pad(flat, (0, rows * LANES - flat.size)).reshape(rows, LANES)


def _unpack(packed, like):
    flat, out, at = packed.reshape(-1), [], 0
    for a in like:
        out.append(flat[at : at + a.size].reshape(a.shape))
        at += a.size
    return out


def kernel(x, mem, g_pre, w_in, w_pool, pool_scale, a_re, a_im, log_dt, b_re, b_im, c_re, c_im, d_skip, w_glu, g_mem, w_kv, w_out, g_post, loss_target, m_g_pre, m_w_in, m_w_pool, m_pool_scale, m_a_re, m_a_im, m_log_dt, m_b_re, m_b_im, m_c_re, m_c_im, m_d_skip, m_w_glu, m_g_mem, m_w_kv, m_w_out, m_g_post, v_g_pre, v_w_in, v_w_pool, v_pool_scale, v_a_re, v_a_im, v_log_dt, v_b_re, v_b_im, v_c_re, v_c_im, v_d_skip, v_w_glu, v_g_mem, v_w_kv, v_w_out, v_g_post):
    batch, seq, d = x.shape
    cl = seq // NCH
    pool_w, ssm_w = pool_scale.shape[1], d_skip.shape[1]
    att_w = w_kv.shape[2] // 2
    tokens = batch * seq
    x2 = x.reshape(tokens, d)
    target2 = loss_target.reshape(tokens, d)

    w_in_g, w_out_g, w_kv_g, w_glu_g = _gather_weights([w_in[0], w_out[0], w_kv[0], w_glu[0]])
    w_out_f = w_out_g.reshape(N_DEV * w_out_g.shape[1], w_out_g.shape[2])
    w_kv_f = w_kv_g.reshape(N_DEV * w_kv_g.shape[1], w_kv_g.shape[2])
    w_glu_f = w_glu_g.transpose(1, 0, 2).reshape(w_glu_g.shape[1], N_DEV * w_glu_g.shape[2])

    wp_blk = jax.scipy.linalg.block_diag(*w_pool[0]).astype(BF16)
    ssm_params = (a_re[0], a_im[0], log_dt[0], b_re[0], b_im[0], c_re[0], c_im[0])
    (lam2, b_blk, c_blk), tables_vjp = jax.vjp(_ssm_tables, *ssm_params)
    s = lam2.shape[1]
    lam8 = jnp.broadcast_to(lam2.reshape(1, 2 * s), (SUBLANES, 2 * s))
    lcl = _lam_power(a_re[0], a_im[0], log_dt[0], float(cl), 1.0)
    lcl1 = _lam_power(a_re[0], a_im[0], log_dt[0], float(cl - 1), float(cl))
    b_bf, c_bf = b_blk.astype(BF16), c_blk.astype(BF16)

    u_pool, u_ssm_p, q, gate = _in_proj(x2, g_pre, w_in_g, batch, cl, pool_w, ssm_w, att_w)
    y_pool = _pool_fwd(u_pool, wp_blk, pool_scale, batch, seq)
    u_ssm = u_ssm_p.reshape(batch, seq, ssm_w)
    y_ssm, pre_ssm, z_ssm = _ssm_fwd(u_ssm, b_bf, c_bf, lam8, lcl, d_skip, w_glu_f)
    kv = _kv_fwd(mem, g_mem, w_kv_f)

    dres, dgate, dy_pool, dy_ssm_p, dq, dkv, gw_out, gg_post, loss_part = _tail(
        x2, target2, gate, y_pool, y_ssm.reshape(batch, cl, NCH * ssm_w), q, kv, w_out_f, g_post, cl)

    gw_kv, gg_mem = _kv_bwd(mem, dkv, g_mem, w_kv_f)
    du_pool, gwp_dense, g_scale = _pool_bwd(u_pool, dy_pool, wp_blk, pool_scale, batch, seq)
    du_ssm, gb_blk, gc_blk, gw_glu, gd_skip, glam = _ssm_bwd(
        u_ssm, pre_ssm, z_ssm, dy_ssm_p.reshape(batch, seq, ssm_w), b_bf, b_bf.T, c_bf.T, lam8, lcl, lcl1,
        d_skip, w_glu_f)
    grad_x2, gw_in, gg_pre = _in_proj_bwd(
        x2, dres, du_pool, du_ssm.reshape(batch, cl, NCH * ssm_w), dq, dgate, g_pre, w_in_g, cl)

    gw = pool_w // len(POOL_WINDOWS)
    gw_pool = jnp.stack([gwp_dense[i * gw : (i + 1) * gw, i * gw : (i + 1) * gw] for i in range(len(POOL_WINDOWS))])
    g_ssm = tables_vjp((glam.reshape(2, s), gb_blk, gc_blk))

    small_w = [g_pre, w_pool, pool_scale, a_re, a_im, log_dt, b_re, b_im, c_re, c_im, d_skip, g_mem, g_post]
    small_m = [m_g_pre, m_w_pool, m_pool_scale, m_a_re, m_a_im, m_log_dt, m_b_re, m_b_im, m_c_re, m_c_im, m_d_skip, m_g_mem, m_g_post]
    small_v = [v_g_pre, v_w_pool, v_pool_scale, v_a_re, v_a_im, v_log_dt, v_b_re, v_b_im, v_c_re, v_c_im, v_d_skip, v_g_mem, v_g_post]
    small_g = [gg_pre, gw_pool, g_scale, *g_ssm, gd_skip, gg_mem, gg_post]
    scalar = jnp.zeros((1,), F32)
    sg, sd, sm, sv = _reduce_small(
        _pack(small_g + [loss_part]), _pack(small_w + [scalar]), _pack(small_m + [scalar]), _pack(small_v + [scalar]))
    like = small_w + [scalar]
    sg, sd, sm, sv = _unpack(sg, like), _unpack(sd, like), _unpack(sm, like), _unpack(sv, like)
    loss = sg[-1].reshape(())

    big = _reduce_big(
        [gw_in, gw_glu.reshape(ssm_w, N_DEV, -1).transpose(1, 0, 2), gw_kv.reshape(N_DEV, -1, gw_kv.shape[1]),
         gw_out.reshape(N_DEV, -1, gw_out.shape[1])],
        [w_in[0], w_glu[0], w_kv[0], w_out[0]],
        [m_w_in[0], m_w_glu[0], m_w_kv[0], m_w_out[0]],
        [v_w_in[0], v_w_glu[0], v_w_kv[0], v_w_out[0]])
    big = {name: tuple(t[None] for t in res) for name, res in zip(["w_in", "w_glu", "w_kv", "w_out"], big)}

    order = ["g_pre", "w_in", "w_pool", "pool_scale", "a_re", "a_im", "log_dt", "b_re", "b_im", "c_re", "c_im",
             "d_skip", "w_glu", "g_mem", "w_kv", "w_out", "g_post"]
    small_names = ["g_pre", "w_pool", "pool_scale", "a_re", "a_im", "log_dt", "b_re", "b_im", "c_re", "c_im",
                   "d_skip", "g_mem", "g_post"]
    outs = [[], [], [], []]
    for name in order:
        if name in big:
            parts = big[name]
        else:
            j = small_names.index(name)
            parts = (sg[j], sd[j], sm[j], sv[j])
        for kind in range(4):
            outs[kind].append(parts[kind])
    return (loss, grad_x2.reshape(batch, seq, d), *outs[0], *outs[1], *outs[2], *outs[3])
```

```python
import functools
import math

import jax
import jax.numpy as jnp
from jax import lax
from jax.experimental import pallas as pl
from jax.experimental.pallas import tpu as pltpu

F32 = jnp.float32
BF16 = jnp.bfloat16
MESH = pl.DeviceIdType.MESH

N_DEV = 8
NCH = 8
SUBLANES = 8
LANES = 128
VMEM_LIMIT = 56 * 1024 * 1024

EPS = 1e-6
POOL_WINDOWS = (2, 4, 8, 16)
MEM_HEADS = 4
SSM_GROUP = 16
SSM_N = 64
ADAM_LR, ADAM_B1, ADAM_B2, ADAM_EPS, ADAM_WD, ADAM_STEP = 0.001, 0.9, 0.999, 1e-08, 0.01, 10


def _mm(a, b):
    return jnp.dot(a, b, preferred_element_type=F32)


def _mm_nt(a, b):
    return lax.dot_general(a, b, (((1,), (1,)), ((), ())), preferred_element_type=F32)


def _mm_tn(a, b):
    return lax.dot_general(a, b, (((0,), (0,)), ((), ())), preferred_element_type=F32)


def _params(*sem):
    return pltpu.CompilerParams(dimension_semantics=sem or None, vmem_limit_bytes=VMEM_LIMIT)


def _adamw(w, g, m, v):
    m = ADAM_B1 * m + (1.0 - ADAM_B1) * g
    v = ADAM_B2 * v + (1.0 - ADAM_B2) * (g * g)
    m_hat = m / (1.0 - ADAM_B1**ADAM_STEP)
    v_hat = v / (1.0 - ADAM_B2**ADAM_STEP)
    delta = -ADAM_LR * (m_hat / (jnp.sqrt(v_hat) + ADAM_EPS) + ADAM_WD * w)
    return delta, m, v


def _gelu(x):
    k = math.sqrt(2.0 / math.pi)
    return 0.5 * x * (1.0 + jnp.tanh(k * (x + 0.044715 * x * x * x)))


def _gelu_grad(x):
    k = math.sqrt(2.0 / math.pi)
    th = jnp.tanh(k * (x + 0.044715 * x * x * x))
    return 0.5 * (1.0 + th) + 0.5 * x * (1.0 - th * th) * (k * (1.0 + 3.0 * 0.044715 * x * x))


def _place():
    return lax.axis_index("x"), lax.axis_index("y"), lax.axis_index("c")


def _gather_body(n, out_dtype, epilogue, n_extra_in, n_extra_out):
    def body(*refs):
        ins = refs[:n]
        extra_in = refs[n : n + n_extra_in]
        outs = refs[n + n_extra_in : 2 * n + n_extra_in]
        extra_out = refs[2 * n + n_extra_in : 2 * n + n_extra_in + n_extra_out]
        send_sems, recv_sems = refs[2 * n + n_extra_in + n_extra_out :]
        x, y, c = _place()
        me, sibling = (x, y, c), (x, y, 1 - c)
        chips = [(1 - x, y), (x, 1 - y), (1 - x, 1 - y)]

        def slot(px, py, pc):
            return 4 * px + 2 * py + pc

        def copy(a, k, block, to):
            ref = outs[a].at[slot(*block)]
            return pltpu.make_async_remote_copy(
                src_ref=ref, dst_ref=ref, send_sem=send_sems.at[a, k], recv_sem=recv_sems.at[a, k],
                device_id=to, device_id_type=MESH)

        for a in range(n):
            outs[a][slot(*me)] = ins[a][...].astype(out_dtype)
        first = []
        for a in range(n):
            first.append(copy(a, 0, me, sibling))
            first += [copy(a, 1 + j, me, (*chip, c)) for j, chip in enumerate(chips)]
        for cp in first:
            cp.start()
        passed = []
        for j, chip in enumerate(chips):
            for a in range(n):
                copy(a, 1 + j, (*chip, c), me).wait_recv()
                cp = copy(a, 4 + j, (*chip, c), sibling)
                cp.start()
                passed.append(cp)
        for a in range(n):
            copy(a, 0, sibling, me).wait_recv()
            for j, chip in enumerate(chips):
                copy(a, 4 + j, (*chip, 1 - c), me).wait_recv()
        for cp in first + passed:
            cp.wait_send()
        if epilogue is not None:
            epilogue(outs, extra_in, extra_out)

    return body


def _gather_weights(blocks):
    n = len(blocks)
    vmem = pl.BlockSpec(memory_space=pltpu.VMEM)
    return pl.pallas_call(
        _gather_body(n, BF16, None, 0, 0),
        name="gather_weights",
        out_shape=[jax.ShapeDtypeStruct((N_DEV, *b.shape), BF16) for b in blocks],
        in_specs=[vmem] * n,
        out_specs=[vmem] * n,
        scratch_shapes=[pltpu.SemaphoreType.DMA((n, 7)), pltpu.SemaphoreType.DMA((n, 7))],
        compiler_params=_params(),
    )(*blocks)


def _reduce_small(part, w, m, v):
    rows = part.shape[0]

    def epilogue(outs, extra_in, extra_out):
        gathered = outs[0]
        w_ref, m_ref, v_ref = extra_in
        g_ref, d_ref, nm_ref, nv_ref = extra_out
        chunk = SUBLANES

        def step(i, _):
            r = pl.ds(pl.multiple_of(i * chunk, chunk), chunk)
            g = gathered[0, r, :]
            for d in range(1, N_DEV):
                g = g + gathered[d, r, :]
            delta, nm, nv = _adamw(w_ref[r, :], g, m_ref[r, :], v_ref[r, :])
            g_ref[r, :] = g
            d_ref[r, :] = delta
            nm_ref[r, :] = nm
            nv_ref[r, :] = nv
            return 0

        lax.fori_loop(0, rows // chunk, step, 0)

    vmem = pl.BlockSpec(memory_space=pltpu.VMEM)
    flat = jax.ShapeDtypeStruct((rows, LANES), F32)
    res = pl.pallas_call(
        _gather_body(1, F32, epilogue, 3, 4),
        name="reduce_small",
        out_shape=[jax.ShapeDtypeStruct((N_DEV, rows, LANES), F32), flat, flat, flat, flat],
        in_specs=[vmem] * 4,
        out_specs=[vmem] * 5,
        scratch_shapes=[pltpu.SemaphoreType.DMA((1, 7)), pltpu.SemaphoreType.DMA((1, 7))],
        compiler_params=_params(),
    )(part, w, m, v)
    return res[1:]


def _reduce_big(parts, ws, ms, vs):
    n = len(parts)
    parts4 = [p.reshape(4, 2, *p.shape[1:]) for p in parts]
    blks = [p.shape[1:] for p in parts]

    def body(*refs):
        part = refs[:n]
        w_in, m_in, v_in = refs[n : 2 * n], refs[2 * n : 3 * n], refs[3 * n : 4 * n]
        outs = refs[4 * n : 8 * n]
        own, r1, r2 = refs[8 * n : 9 * n], refs[9 * n : 10 * n], refs[10 * n : 11 * n]
        s1_send, s1_recv, s2_send, s2_recv, loc = refs[11 * n :]
        x, y, c = _place()
        sibling = (x, y, 1 - c)
        chips = [(1 - x, y), (x, 1 - y), (1 - x, 1 - y)]

        def rowwise(rows, fn):
            chunk = math.gcd(rows, 128)

            def step(i, _):
                fn(pl.ds(pl.multiple_of(i * chunk, chunk), chunk))
                return 0

            lax.fori_loop(0, rows // chunk, step, 0)

        stage1, local = [], []
        for a in range(n):
            cp = pltpu.make_async_remote_copy(
                src_ref=part[a].at[:, 1 - c], dst_ref=r1[a], send_sem=s1_send.at[a], recv_sem=s1_recv.at[a],
                device_id=sibling, device_id_type=MESH)
            cp.start()
            stage1.append(cp)
            lc = pltpu.make_async_copy(part[a].at[:, c], own[a], loc.at[a])
            lc.start()
            local.append(lc)
        stage2 = []
        for a in range(n):
            local[a].wait()
            stage1[a].wait_recv()
            for chip in range(4):

                def add(r, a=a, chip=chip):
                    own[a][chip, r, :] = own[a][chip, r, :] + r1[a][chip, r, :]

                rowwise(blks[a][0], add)
            for k, chip in enumerate(chips):
                cp = pltpu.make_async_remote_copy(
                    src_ref=own[a].at[2 * chip[0] + chip[1]], dst_ref=r2[a].at[k],
                    send_sem=s2_send.at[a, k], recv_sem=s2_recv.at[a, k],
                    device_id=(*chip, c), device_id_type=MESH)
                cp.start()
                stage2.append(cp)
        for a in range(n):
            for k, chip in enumerate(chips):
                stage2[3 * a + k].wait_recv()
            g_ref, d_ref, nm_ref, nv_ref = outs[4 * a : 4 * a + 4]

            def update(r, a=a, g_ref=g_ref, d_ref=d_ref, nm_ref=nm_ref, nv_ref=nv_ref):
                g = own[a][2 * x + y, r, :] + r2[a][0, r, :] + r2[a][1, r, :] + r2[a][2, r, :]
                delta, nm, nv = _adamw(w_in[a][r, :], g, m_in[a][r, :], v_in[a][r, :])
                g_ref[r, :] = g
                d_ref[r, :] = delta
                nm_ref[r, :] = nm
                nv_ref[r, :] = nv

            rowwise(blks[a][0], update)
        for cp in stage1 + stage2:
            cp.wait_send()

    vmem = pl.BlockSpec(memory_space=pltpu.VMEM)
    hbm = pl.BlockSpec(memory_space=pl.ANY)
    out_shape = []
    for b in blks:
        out_shape += [jax.ShapeDtypeStruct(b, F32)] * 4
    scratch = (
        [pltpu.VMEM((4, *b), F32) for b in blks]
        + [pltpu.VMEM((4, *b), F32) for b in blks]
        + [pltpu.VMEM((3, *b), F32) for b in blks]
        + [pltpu.SemaphoreType.DMA((n,)), pltpu.SemaphoreType.DMA((n,)), pltpu.SemaphoreType.DMA((n, 3)),
           pltpu.SemaphoreType.DMA((n, 3)), pltpu.SemaphoreType.DMA((n,))]
    )
    res = pl.pallas_call(
        body,
        name="reduce_big",
        out_shape=out_shape,
        in_specs=[hbm] * n + [vmem] * (3 * n),
        out_specs=[vmem] * (4 * n),
        scratch_shapes=scratch,
        compiler_params=_params(),
    )(*parts4, *ws, *ms, *vs)
    return [tuple(res[4 * a : 4 * a + 4]) for a in range(n)]


def _rms(x):
    return lax.rsqrt(jnp.mean(x * x, axis=-1, keepdims=True) + EPS)


def _token_tile(tokens, want):
    tile = min(want, tokens // 2)
    assert tokens % tile == 0 and tile % 16 == 0
    return tile


def _in_proj(x2, g_pre, w_in_g, pool_w, ssm_w, att_w):
    tokens, d = x2.shape
    nb = w_in_g.shape[2]
    mix = pool_w + ssm_w + att_w
    half = N_DEV // 2
    cl = _token_tile(tokens, 512)
    assert half * nb == mix and nb == 256 and pool_w == 384 and ssm_w == 384 and att_w == 256

    def body(x_ref, g_ref, w_ref, up_ref, us_ref, q_ref, gate_ref):
        x = x_ref[...]
        h = (x * _rms(x) * g_ref[...]).astype(BF16)
        p = [_mm(h, w_ref[j]) for j in range(half)]
        up_ref[:, 0:256] = p[0]
        up_ref[:, 256:384] = p[1][:, 0:128]
        us_ref[:, 0:128] = p[1][:, 128:256]
        us_ref[:, 128:384] = p[2]
        q_ref[...] = p[3]
        for j in range(half):
            gate_ref[:, j * nb : (j + 1) * nb] = _mm(h, w_ref[half + j])

    return pl.pallas_call(
        body,
        name="in_proj",
        grid=(tokens // cl,),
        in_specs=[
            pl.BlockSpec((cl, d), lambda i: (i, 0)),
            pl.BlockSpec((1, d), lambda i: (0, 0)),
            pl.BlockSpec((N_DEV, d, nb), lambda i: (0, 0, 0)),
        ],
        out_specs=[
            pl.BlockSpec((cl, pool_w), lambda i: (i, 0)),
            pl.BlockSpec((cl, ssm_w), lambda i: (i, 0)),
            pl.BlockSpec((cl, att_w), lambda i: (i, 0)),
            pl.BlockSpec((cl, mix), lambda i: (i, 0)),
        ],
        out_shape=[
            jax.ShapeDtypeStruct((tokens, pool_w), F32),
            jax.ShapeDtypeStruct((tokens, ssm_w), F32),
            jax.ShapeDtypeStruct((tokens, att_w), F32),
            jax.ShapeDtypeStruct((tokens, mix), F32),
        ],
        compiler_params=_params("arbitrary"),
    )(x2, g_pre, w_in_g)


def _in_proj_bwd(x2, dres, du_pool, du_ssm, dq, dgate, g_pre, w_in_g):
    tokens, d = x2.shape
    nb = w_in_g.shape[2]
    pool_w, ssm_w, att_w, mix = du_pool.shape[1], du_ssm.shape[1], dq.shape[1], dgate.shape[1]
    half = N_DEV // 2
    cl = _token_tile(tokens, 512)
    n_tiles = tokens // cl

    def body(x_ref, dres_ref, dup_ref, dus_ref, dq_ref, dgate_ref, g_ref, w_ref, gx_ref, gw_hbm, gg_ref, acc, sem):
        i = pl.program_id(0)

        @pl.when(i == 0)
        def _():
            acc[...] = jnp.zeros_like(acc)
            gg_ref[...] = jnp.zeros_like(gg_ref)

        x = x_ref[...]
        r = _rms(x)
        xn = x * r
        g = g_ref[...]
        h = (xn * g).astype(BF16)
        dval = jnp.concatenate([dup_ref[...], dus_ref[...], dq_ref[...]], axis=1)
        dh = jnp.zeros((cl, d), F32)
        for j in range(N_DEV):
            src = dval if j < half else dgate_ref[...]
            jj = j % half
            dp = src[:, jj * nb : (jj + 1) * nb].astype(BF16)
            dh = dh + _mm_nt(dp, w_ref[j])
            acc[j] += _mm_tn(h, dp)
        gg_ref[...] += jnp.sum(dh * xn, axis=0, keepdims=True)
        dxn = dh * g
        gx_ref[...] = dres_ref[...] + r * (dxn - xn * jnp.mean(dxn * xn, axis=-1, keepdims=True))

        @pl.when(i == n_tiles - 1)
        def _():
            cp = pltpu.make_async_copy(acc, gw_hbm, sem)
            cp.start()
            cp.wait()

    return pl.pallas_call(
        body,
        name="in_proj_bwd",
        grid=(n_tiles,),
        in_specs=[
            pl.BlockSpec((cl, d), lambda i: (i, 0)),
            pl.BlockSpec((cl, d), lambda i: (i, 0)),
            pl.BlockSpec((cl, pool_w), lambda i: (i, 0)),
            pl.BlockSpec((cl, ssm_w), lambda i: (i, 0)),
            pl.BlockSpec((cl, att_w), lambda i: (i, 0)),
            pl.BlockSpec((cl, mix), lambda i: (i, 0)),
            pl.BlockSpec((1, d), lambda i: (0, 0)),
            pl.BlockSpec((N_DEV, d, nb), lambda i: (0, 0, 0)),
        ],
        out_specs=[
            pl.BlockSpec((cl, d), lambda i: (i, 0)),
            pl.BlockSpec(memory_space=pl.ANY),
            pl.BlockSpec((1, d), lambda i: (0, 0)),
        ],
        out_shape=[
            jax.ShapeDtypeStruct((tokens, d), F32),
            jax.ShapeDtypeStruct((N_DEV, d, nb), F32),
            jax.ShapeDtypeStruct((1, d), F32),
        ],
        scratch_shapes=[pltpu.VMEM((N_DEV, d, nb), F32), pltpu.SemaphoreType.DMA],
        compiler_params=_params("arbitrary"),
    )(x2, dres, du_pool, du_ssm, dq, dgate, g_pre, w_in_g)


def _pool_geometry(seq, width):
    gw = width // len(POOL_WINDOWS)
    col = lax.broadcasted_iota(jnp.int32, (1, width), 1)
    win = jnp.full((1, width), float(POOL_WINDOWS[-1]), F32)
    for gi in range(len(POOL_WINDOWS) - 2, -1, -1):
        win = jnp.where(col < (gi + 1) * gw, float(POOL_WINDOWS[gi]), win)
    row = lax.broadcasted_iota(jnp.int32, (seq, width), 0)
    cnt = jnp.minimum((row + 1).astype(F32), win)
    return win, row, cnt


def _window_sums(a, row, win, seq, back):
    sums = []
    s, k = a, 1
    while k < POOL_WINDOWS[-1]:
        if back:
            shifted = jnp.where(row < seq - k, pltpu.roll(s, seq - k, 0), 0.0)
        else:
            shifted = jnp.where(row >= k, pltpu.roll(s, k, 0), 0.0)
        s = s + shifted
        k *= 2
        sums.append((k, s))
    out = sums[-1][1]
    for k, s in reversed(sums[:-1]):
        out = jnp.where(win <= float(k), s, out)
    return out


def _pool_fwd(u2, wp_blk, scale, batch, seq):
    width = u2.shape[1]

    def body(u_ref, w_ref, s_ref, y_ref):
        u = u_ref[...]
        win, row, cnt = _pool_geometry(seq, width)
        diff = _window_sums(u, row, win, seq, False) / cnt - u
        y_ref[...] = _mm(diff.astype(BF16), w_ref[...]) * s_ref[...]

    return pl.pallas_call(
        body,
        name="pool_fwd",
        grid=(batch,),
        in_specs=[
            pl.BlockSpec((seq, width), lambda b: (b, 0)),
            pl.BlockSpec((width, width), lambda b: (0, 0)),
            pl.BlockSpec((1, width), lambda b: (0, 0)),
        ],
        out_specs=pl.BlockSpec((seq, width), lambda b: (b, 0)),
        out_shape=jax.ShapeDtypeStruct(u2.shape, F32),
        compiler_params=_params("arbitrary"),
    )(u2, wp_blk, scale)


def _pool_bwd(u2, dy2, wp_blk, scale, batch, seq):
    width = u2.shape[1]

    def body(u_ref, dy_ref, w_ref, s_ref, du_ref, gw_ref, gs_ref):
        @pl.when(pl.program_id(0) == 0)
        def _():
            gw_ref[...] = jnp.zeros_like(gw_ref)
            gs_ref[...] = jnp.zeros_like(gs_ref)

        u = u_ref[...]
        dy = dy_ref[...]
        win, row, cnt = _pool_geometry(seq, width)
        diff = (_window_sums(u, row, win, seq, False) / cnt - u).astype(BF16)
        gs_ref[...] += jnp.sum(dy * _mm(diff, w_ref[...]), axis=0, keepdims=True)
        dys = (dy * s_ref[...]).astype(BF16)
        gw_ref[...] += _mm_tn(diff, dys)
        dd = _mm_nt(dys, w_ref[...])
        du_ref[...] = _window_sums(dd / cnt, row, win, seq, True) - dd

    return pl.pallas_call(
        body,
        name="pool_bwd",
        grid=(batch,),
        in_specs=[
            pl.BlockSpec((seq, width), lambda b: (b, 0)),
            pl.BlockSpec((seq, width), lambda b: (b, 0)),
            pl.BlockSpec((width, width), lambda b: (0, 0)),
            pl.BlockSpec((1, width), lambda b: (0, 0)),
        ],
        out_specs=[
            pl.BlockSpec((seq, width), lambda b: (b, 0)),
            pl.BlockSpec((width, width), lambda b: (0, 0)),
            pl.BlockSpec((1, width), lambda b: (0, 0)),
        ],
        out_shape=[
            jax.ShapeDtypeStruct(u2.shape, F32),
            jax.ShapeDtypeStruct((width, width), F32),
            jax.ShapeDtypeStruct((1, width), F32),
        ],
        compiler_params=_params("arbitrary"),
    )(u2, dy2, wp_blk, scale)


def _state_row(z, n_blocks):
    re = jnp.real(z).reshape(n_blocks, -1)
    im = jnp.imag(z).reshape(n_blocks, -1)
    return jnp.concatenate([re, im], axis=1).reshape(1, -1)


def _ssm_tables(a_re, a_im, log_dt, b_re, b_im, c_re, c_im):
    groups, n_state = a_re.shape
    ch = b_re.shape[2]
    nb = groups * ch // LANES
    gl = groups // nb
    lam = lax.complex(a_re, a_im)
    lam_bar = jnp.exp(lam * jnp.exp(log_dt)[:, None])
    b_bar = ((lam_bar - 1.0) / lam)[..., None] * lax.complex(b_re, b_im)
    eye = jnp.eye(gl, dtype=F32)

    def rows_to_state(t):
        return jnp.einsum("sgnc,gh->sgchn", t.reshape(nb, gl, n_state, ch), eye).reshape(nb, gl * ch, gl * n_state)

    def state_to_rows(t):
        return jnp.einsum("sgcn,gh->shngc", t.reshape(nb, gl, ch, n_state), eye).reshape(nb, gl * n_state, gl * ch)

    b_tab = jnp.concatenate([rows_to_state(jnp.real(b_bar)), rows_to_state(jnp.imag(b_bar))], axis=2)
    c_tab = jnp.concatenate([state_to_rows(c_re), -state_to_rows(c_im)], axis=1)
    return _state_row(lam_bar, nb), b_tab, c_tab


def _lam_power(a_re, a_im, log_dt, power, scale, n_blocks):
    return _state_row(scale * jnp.exp(lax.complex(a_re, a_im) * jnp.exp(log_dt)[:, None] * power), n_blocks)


def _state_blocks(s2, n_blocks, width):
    half = s2 // n_blocks // 2
    assert half % width == 0
    return [(b * 2 * half + o, b * 2 * half + half + o) for b in range(n_blocks) for o in range(0, half, width)]


def _scan(src_ref, dst_ref, st_ref, lam8_ref, n_groups, s, n_blocks, reverse, store):
    lb = 512
    for re0, im0 in _state_blocks(2 * s, n_blocks, lb):
        cr, ci = pl.ds(re0, lb), pl.ds(im0, lb)
        lr = lam8_ref[:, cr]
        li = -lam8_ref[:, ci] if reverse else lam8_ref[:, ci]

        def step(i, carry, cr=cr, ci=ci, lr=lr, li=li):
            hr, hi = carry
            grp = n_groups - 1 - i if reverse else i
            rows = pl.ds(pl.multiple_of(grp * SUBLANES, SUBLANES), SUBLANES)
            nr = lr * hr - li * hi + src_ref[rows, cr]
            ni = lr * hi + li * hr + src_ref[rows, ci]
            if store:
                dst_ref[rows, cr] = nr
                dst_ref[rows, ci] = ni
            return nr, ni

        hr, hi = lax.fori_loop(0, n_groups, step, (st_ref[:, cr], st_ref[:, ci]), unroll=2)
        st_ref[:, cr] = hr
        st_ref[:, ci] = hi


def _scan_adjoint(dh_ref, bu_ref, stg_ref, ste_ref, acc_ref, lam8_ref, n_groups, s, n_blocks, final):
    lb = 256
    for re0, im0 in _state_blocks(2 * s, n_blocks, lb):
        cr, ci = pl.ds(re0, lb), pl.ds(im0, lb)
        lr, li = lam8_ref[:, cr], -lam8_ref[:, ci]

        def step(i, carry, cr=cr, ci=ci, lr=lr, li=li):
            gr, gi, er, ei, ar, ai = carry
            grp = n_groups - 1 - i
            rows = pl.ds(pl.multiple_of(grp * SUBLANES, SUBLANES), SUBLANES)
            ner = gr + lr * er - li * ei
            nei = gi + lr * ei + li * er
            ngr = lr * gr - li * gi + dh_ref[rows, cr]
            ngi = lr * gi + li * gr + dh_ref[rows, ci]
            if final:
                br, bi = bu_ref[rows, cr], bu_ref[rows, ci]
                ar = ar + br * ner + bi * nei
                ai = ai + br * nei - bi * ner
                dh_ref[rows, cr] = ngr
                dh_ref[rows, ci] = ngi
            return ngr, ngi, ner, nei, ar, ai

        init = (stg_ref[:, cr], stg_ref[:, ci], ste_ref[:, cr], ste_ref[:, ci], acc_ref[:, cr], acc_ref[:, ci])
        gr, gi, er, ei, ar, ai = lax.fori_loop(0, n_groups, step, init, unroll=2)
        stg_ref[:, cr] = gr
        stg_ref[:, ci] = gi
        ste_ref[:, cr] = er
        ste_ref[:, ci] = ei
        if final:
            acc_ref[:, cr] = ar
            acc_ref[:, ci] = ai


def _chunk_starts(st_ref, init_ref, lcl_ref, s, n_blocks):
    w = s // n_blocks
    init_ref[0:1, :] = jnp.zeros((1, 2 * s), F32)
    for re0, im0 in _state_blocks(2 * s, n_blocks, w):
        re, im = pl.ds(re0, w), pl.ds(im0, w)
        ar, ai = lcl_ref[:, re], lcl_ref[:, im]
        cr = jnp.zeros((1, w), F32)
        ci = jnp.zeros((1, w), F32)
        for k in range(1, NCH):
            cr, ci = (ar * cr - ai * ci + st_ref[k - 1 : k, re], ar * ci + ai * cr + st_ref[k - 1 : k, im])
            init_ref[k : k + 1, re] = cr
            init_ref[k : k + 1, im] = ci


def _chunk_starts_adjoint(stg_ref, ste_ref, initg_ref, inite_ref, lcl_ref, lcl1_ref, s, n_blocks):
    w = s // n_blocks
    initg_ref[NCH - 1 : NCH, :] = jnp.zeros((1, 2 * s), F32)
    inite_ref[NCH - 1 : NCH, :] = jnp.zeros((1, 2 * s), F32)
    for re0, im0 in _state_blocks(2 * s, n_blocks, w):
        re, im = pl.ds(re0, w), pl.ds(im0, w)
        ar, ai = lcl_ref[:, re], -lcl_ref[:, im]
        pr, pi = lcl1_ref[:, re], -lcl1_ref[:, im]
        zero = jnp.zeros((1, w), F32)
        gr, gi, er, ei = zero, zero, zero, zero
        for k in range(NCH - 2, -1, -1):
            ner = ste_ref[k + 1 : k + 2, re] + (ar * er - ai * ei) + (pr * gr - pi * gi)
            nei = ste_ref[k + 1 : k + 2, im] + (ar * ei + ai * er) + (pr * gi + pi * gr)
            ngr = stg_ref[k + 1 : k + 2, re] + ar * gr - ai * gi
            ngi = stg_ref[k + 1 : k + 2, im] + ar * gi + ai * gr
            gr, gi, er, ei = ngr, ngi, ner, nei
            initg_ref[k : k + 1, re] = gr
            initg_ref[k : k + 1, im] = gi
            inite_ref[k : k + 1, re] = er
            inite_ref[k : k + 1, im] = ei


def _ssm_rows(seq):
    rows = min(512, seq // 2)
    assert seq % rows == 0 and rows % SUBLANES == 0
    return rows


def _to_chunk_major(src_ref, dst_ref, tmp_ref, cl):
    for c in range(src_ref.shape[1] // LANES):
        cols = slice(c * LANES, (c + 1) * LANES)
        tmp_ref[...] = src_ref[:, cols]

        def step(j, _, cols=cols):
            dst_ref[pl.ds(pl.multiple_of(j * NCH, NCH), NCH), cols] = tmp_ref[pl.ds(j, NCH, stride=cl), :]
            return 0

        lax.fori_loop(0, cl, step, 0)


def _from_chunk_major(src_ref, dst_ref, tmp_ref, cl):
    for c in range(src_ref.shape[1] // LANES):
        cols = slice(c * LANES, (c + 1) * LANES)

        def step(j, _, cols=cols):
            tmp_ref[pl.ds(j, NCH, stride=cl), :] = src_ref[pl.ds(pl.multiple_of(j * NCH, NCH), NCH), cols]
            return 0

        lax.fori_loop(0, cl, step, 0)
        dst_ref[:, cols] = tmp_ref[...]


def _blockwise(fn, n_blocks):
    return jnp.concatenate([fn(b) for b in range(n_blocks)], axis=1)


def _ssm_fwd(u, b_tab, c_tab, lam8, lcl, d_skip, w_glu):
    batch, seq, width = u.shape
    s = lam8.shape[1] // 2
    nb = b_tab.shape[0]
    sb = 2 * s // nb
    cl = seq // NCH
    rows = _ssm_rows(seq)
    n_tiles = seq // rows
    n_groups = rows // SUBLANES

    def body(u_ref, b_ref, c_ref, lam_ref, lcl_ref, d_ref, wg_ref, y_ref, pre_ref, z_ref, u_cm, y_cm, bu, st, init, tmp):
        ph, t = pl.program_id(1), pl.program_id(2)
        tile_rows = pl.ds(pl.multiple_of(t * rows, rows), rows)

        @pl.when((ph == 0) & (t == 0))
        def _():
            _to_chunk_major(u_ref, u_cm, tmp, cl)
            st[...] = jnp.zeros_like(st)

        @pl.when((ph == 1) & (t == 0))
        def _():
            st[...] = init[...]

        u_t = u_cm[tile_rows, :]
        u_b = u_t.astype(BF16)
        for blk in range(nb):
            bu[:, blk * sb : (blk + 1) * sb] = _mm(u_b[:, blk * LANES : (blk + 1) * LANES], b_ref[blk])

        @pl.when(ph == 0)
        def _():
            _scan(bu, bu, st, lam_ref, n_groups, s, nb, False, False)

        @pl.when((ph == 0) & (t == n_tiles - 1))
        def _():
            _chunk_starts(st, init, lcl_ref, s, nb)

        @pl.when(ph == 1)
        def _():
            _scan(bu, bu, st, lam_ref, n_groups, s, nb, False, True)
            hs = lambda blk: _mm(bu[:, blk * sb : (blk + 1) * sb].astype(BF16), c_ref[blk])
            pre = _blockwise(hs, nb) + d_ref[...] * u_t
            z = _mm(_gelu(pre).astype(BF16), wg_ref[...])
            pre_ref[...] = pre
            z_ref[...] = z
            y_cm[tile_rows, :] = z[:, 0:width] * jax.nn.sigmoid(z[:, width : 2 * width])

        @pl.when((ph == 1) & (t == n_tiles - 1))
        def _():
            _from_chunk_major(y_cm, y_ref, tmp, cl)

    whole = lambda b, ph, t: (b, 0, 0)
    out_tile = lambda b, ph, t: (b, t * ph, 0)
    full = lambda a: pl.BlockSpec(a.shape, lambda b, ph, t: (0,) * a.ndim)
    return pl.pallas_call(
        body,
        name="ssm_fwd",
        grid=(batch, 2, n_tiles),
        in_specs=[
            pl.BlockSpec((None, seq, width), whole),
            full(b_tab), full(c_tab), full(lam8), full(lcl), full(d_skip), full(w_glu),
        ],
        out_specs=[
            pl.BlockSpec((None, seq, width), whole),
            pl.BlockSpec((None, rows, width), out_tile),
            pl.BlockSpec((None, rows, 2 * width), out_tile),
        ],
        out_shape=[
            jax.ShapeDtypeStruct((batch, seq, width), F32),
            jax.ShapeDtypeStruct((batch, seq, width), F32),
            jax.ShapeDtypeStruct((batch, seq, 2 * width), F32),
        ],
        scratch_shapes=[
            pltpu.VMEM((seq, width), F32),
            pltpu.VMEM((seq, width), F32),
            pltpu.VMEM((rows, 2 * s), F32),
            pltpu.VMEM((SUBLANES, 2 * s), F32),
            pltpu.VMEM((SUBLANES, 2 * s), F32),
            pltpu.VMEM((seq, LANES), F32),
        ],
        compiler_params=_params("arbitrary", "arbitrary", "arbitrary"),
    )(u, b_tab, c_tab, lam8, lcl, d_skip, w_glu)


def _ssm_bwd(u, pre_p, z_p, dy, b_tab, b_tab_t, c_tab_t, lam8, lcl, lcl1, d_skip, w_glu):
    batch, seq, width = u.shape
    s = lam8.shape[1] // 2
    nb = b_tab.shape[0]
    sb = 2 * s // nb
    cl = seq // NCH
    rows = _ssm_rows(seq)
    n_tiles = seq // rows
    n_groups = rows // SUBLANES

    def body(u_ref, pre_ref, z_ref, dy_ref, b_ref, bt_ref, ct_ref, lam_ref, lcl_ref, lcl1_ref, d_ref, wg_ref,
             du_ref, gb_ref, gc_ref, gwg_ref, gd_ref, glam_ref,
             u_cm, dy_cm, bu, dh, dpre_all, st, init, stg, ste, initg, inite, acc, tmp):
        b, ph, t = pl.program_id(0), pl.program_id(1), pl.program_id(2)
        first = (b == 0) & (ph == 0) & (t == 0)
        last = (b == batch - 1) & (ph == 3) & (t == n_tiles - 1)
        tile = jnp.where(ph < 2, t, n_tiles - 1 - t)
        tile_rows = pl.ds(pl.multiple_of(tile * rows, rows), rows)
        lanes = lambda blk: slice(blk * LANES, (blk + 1) * LANES)
        states = lambda blk: slice(blk * sb, (blk + 1) * sb)

        @pl.when(first)
        def _():
            acc[...] = jnp.zeros_like(acc)
            gb_ref[...] = jnp.zeros_like(gb_ref)
            gc_ref[...] = jnp.zeros_like(gc_ref)
            gwg_ref[...] = jnp.zeros_like(gwg_ref)
            gd_ref[...] = jnp.zeros_like(gd_ref)

        @pl.when((ph == 0) & (t == 0))
        def _():
            _to_chunk_major(u_ref, u_cm, tmp, cl)
            _to_chunk_major(dy_ref, dy_cm, tmp, cl)

        u_t = u_cm[tile_rows, :]
        u_b = u_t.astype(BF16)

        @pl.when((ph == 0) | (ph == 1) | (ph == 3))
        def _():
            for blk in range(nb):
                bu[:, states(blk)] = _mm(u_b[:, lanes(blk)], b_ref[blk])

        @pl.when(ph == 0)
        def _():
            @pl.when(t == 0)
            def _():
                st[...] = jnp.zeros_like(st)

            _scan(bu, bu, st, lam_ref, n_groups, s, nb, False, False)

            @pl.when(t == n_tiles - 1)
            def _():
                _chunk_starts(st, init, lcl_ref, s, nb)

        @pl.when(ph == 1)
        def _():
            @pl.when(t == 0)
            def _():
                st[...] = init[...]

            _scan(bu, bu, st, lam_ref, n_groups, s, nb, False, True)
            z = z_ref[...]
            dy_t = dy_cm[tile_rows, :]
            pre = pre_ref[...]
            z1, sig = z[:, 0:width], jax.nn.sigmoid(z[:, width : 2 * width])
            dz = jnp.concatenate([dy_t * sig, dy_t * z1 * sig * (1.0 - sig)], axis=1).astype(BF16)
            gwg_ref[...] += _mm_tn(_gelu(pre).astype(BF16), dz)
            dpre = _mm_nt(dz, wg_ref[...]) * _gelu_grad(pre)
            dpre_all[tile_rows, :] = dpre
            gd_ref[...] += jnp.sum(dpre * u_t, axis=0, keepdims=True)
            dpre_b = dpre.astype(BF16)
            for blk in range(nb):
                gc_ref[blk] += _mm_tn(bu[:, states(blk)].astype(BF16), dpre_b[:, lanes(blk)])

        @pl.when(ph >= 2)
        def _():
            dpre_b = dpre_all[tile_rows, :].astype(BF16)
            for blk in range(nb):
                dh[:, states(blk)] = _mm(dpre_b[:, lanes(blk)], ct_ref[blk])

        @pl.when(ph == 2)
        def _():
            @pl.when(t == 0)
            def _():
                stg[...] = jnp.zeros_like(stg)
                ste[...] = jnp.zeros_like(ste)

            _scan_adjoint(dh, bu, stg, ste, acc, lam_ref, n_groups, s, nb, False)

            @pl.when(t == n_tiles - 1)
            def _():
                _chunk_starts_adjoint(stg, ste, initg, inite, lcl_ref, lcl1_ref, s, nb)

        @pl.when(ph == 3)
        def _():
            @pl.when(t == 0)
            def _():
                stg[...] = initg[...]
                ste[...] = inite[...]

            _scan_adjoint(dh, bu, stg, ste, acc, lam_ref, n_groups, s, nb, True)
            du = lambda blk: _mm(dh[:, states(blk)].astype(BF16), bt_ref[blk])
            dy_cm[tile_rows, :] = _blockwise(du, nb) + dpre_all[tile_rows, :] * d_ref[...]
            for blk in range(nb):
                gb_ref[blk] += _mm_tn(u_b[:, lanes(blk)], dh[:, states(blk)].astype(BF16))

            @pl.when(t == n_tiles - 1)
            def _():
                _from_chunk_major(dy_cm, du_ref, tmp, cl)

        @pl.when(last)
        def _():
            glam_ref[...] = jnp.sum(acc[...], axis=0, keepdims=True)

    def tile(b, ph, t):
        return (b, jnp.where(ph < 2, t, n_tiles - 1 - t), 0)

    whole = lambda b, ph, t: (b, 0, 0)
    full = lambda a: pl.BlockSpec(a.shape, lambda b, ph, t: (0,) * a.ndim)
    return pl.pallas_call(
        body,
        name="ssm_bwd",
        grid=(batch, 4, n_tiles),
        in_specs=[
            pl.BlockSpec((None, seq, width), whole),
            pl.BlockSpec((None, rows, width), tile),
            pl.BlockSpec((None, rows, 2 * width), tile),
            pl.BlockSpec((None, seq, width), whole),
            full(b_tab), full(b_tab_t), full(c_tab_t), full(lam8), full(lcl), full(lcl1), full(d_skip), full(w_glu),
        ],
        out_specs=[
            pl.BlockSpec((None, seq, width), whole),
            full(b_tab), full(b_tab_t), full(w_glu), full(d_skip),
            pl.BlockSpec((1, 2 * s), lambda b, ph, t: (0, 0)),
        ],
        out_shape=[
            jax.ShapeDtypeStruct((batch, seq, width), F32),
            jax.ShapeDtypeStruct(b_tab.shape, F32),
            jax.ShapeDtypeStruct(b_tab_t.shape, F32),
            jax.ShapeDtypeStruct(w_glu.shape, F32),
            jax.ShapeDtypeStruct(d_skip.shape, F32),
            jax.ShapeDtypeStruct((1, 2 * s), F32),
        ],
        scratch_shapes=[
            pltpu.VMEM((seq, width), F32),
            pltpu.VMEM((seq, width), F32),
            pltpu.VMEM((rows, 2 * s), F32),
            pltpu.VMEM((rows, 2 * s), F32),
            pltpu.VMEM((seq, width), F32),
        ]
        + [pltpu.VMEM((SUBLANES, 2 * s), F32)] * 7
        + [pltpu.VMEM((seq, LANES), F32)],
        compiler_params=_params("arbitrary", "arbitrary", "arbitrary"),
    )(u, pre_p, z_p, dy, b_tab, b_tab_t, c_tab_t, lam8, lcl, lcl1, d_skip, w_glu)


def _kv_fwd(mem, g_mem, w_kv):
    batch, n_mem, d = mem.shape
    kvw = w_kv.shape[1]

    def body(mem_ref, g_ref, w_ref, kv_ref):
        m = mem_ref[...]
        kv_ref[...] = _mm((m * _rms(m) * g_ref[...]).astype(BF16), w_ref[...])

    return pl.pallas_call(
        body,
        name="kv_fwd",
        grid=(batch,),
        in_specs=[
            pl.BlockSpec((None, n_mem, d), lambda b: (b, 0, 0)),
            pl.BlockSpec((1, d), lambda b: (0, 0)),
            pl.BlockSpec((d, kvw), lambda b: (0, 0)),
        ],
        out_specs=pl.BlockSpec((None, n_mem, kvw), lambda b: (b, 0, 0)),
        out_shape=jax.ShapeDtypeStruct((batch, n_mem, kvw), F32),
        compiler_params=_params("arbitrary"),
    )(mem, g_mem, w_kv)


def _kv_bwd(mem, dkv, g_mem, w_kv):
    batch, n_mem, d = mem.shape
    kvw = w_kv.shape[1]

    def body(mem_ref, dkv_ref, g_ref, w_ref, gw_ref, gg_ref):
        @pl.when(pl.program_id(0) == 0)
        def _():
            gw_ref[...] = jnp.zeros_like(gw_ref)
            gg_ref[...] = jnp.zeros_like(gg_ref)

        m = mem_ref[...]
        mn = m * _rms(m)
        dkv_b = dkv_ref[...].astype(BF16)
        gw_ref[...] += _mm_tn((mn * g_ref[...]).astype(BF16), dkv_b)
        gg_ref[...] += jnp.sum(_mm_nt(dkv_b, w_ref[...]) * mn, axis=0, keepdims=True)

    return pl.pallas_call(
        body,
        name="kv_bwd",
        grid=(batch,),
        in_specs=[
            pl.BlockSpec((None, n_mem, d), lambda b: (b, 0, 0)),
            pl.BlockSpec((None, n_mem, kvw), lambda b: (b, 0, 0)),
            pl.BlockSpec((1, d), lambda b: (0, 0)),
            pl.BlockSpec((d, kvw), lambda b: (0, 0)),
        ],
        out_specs=[pl.BlockSpec((d, kvw), lambda b: (0, 0)), pl.BlockSpec((1, d), lambda b: (0, 0))],
        out_shape=[jax.ShapeDtypeStruct((d, kvw), F32), jax.ShapeDtypeStruct((1, d), F32)],
        compiler_params=_params("arbitrary"),
    )(mem, dkv, g_mem, w_kv)


def _tail(x2, target2, gate, y_pool, y_ssm, q, kv, w_out, g_post):
    tokens, d = x2.shape
    pool_w, ssm_w, att_w, mix = y_pool.shape[1], y_ssm.shape[1], q.shape[1], gate.shape[1]
    batch, n_mem, kvw = kv.shape
    hd = att_w // MEM_HEADS
    cl = _token_tile(tokens // batch, 256)
    n_tiles = tokens // cl
    per_seq = tokens // batch // cl
    qk_scale = hd**-0.5

    def body(x_ref, tg_ref, gate_ref, yp_ref, ys_ref, q_ref, kv_ref, w_ref, g_ref,
             dres_ref, dgate_ref, dyp_ref, dys_ref, dq_ref, dkv_ref, gw_hbm, gg_ref, loss_ref, acc, sem):
        i = pl.program_id(0)

        @pl.when(i == 0)
        def _():
            acc[...] = jnp.zeros_like(acc)
            gg_ref[...] = jnp.zeros_like(gg_ref)
            loss_ref[...] = jnp.zeros_like(loss_ref)

        @pl.when(i % per_seq == 0)
        def _():
            dkv_ref[...] = jnp.zeros_like(dkv_ref)

        q = q_ref[...]
        k = kv_ref[:, 0:att_w].astype(BF16)
        v = kv_ref[:, att_w : 2 * att_w].astype(BF16)
        lane = lax.broadcasted_iota(jnp.int32, (1, att_w), 1)
        heads = [(lane >= h * hd) & (lane < (h + 1) * hd) for h in range(MEM_HEADS)]
        probs, q_heads = [], []
        att = jnp.zeros((cl, att_w), F32)
        for mask in heads:
            qh = jnp.where(mask, q, 0.0).astype(BF16)
            sc = _mm_nt(qh, k) * qk_scale
            e = jnp.exp(sc - jnp.max(sc, axis=-1, keepdims=True))
            p = e / jnp.sum(e, axis=-1, keepdims=True)
            att = att + jnp.where(mask, _mm(p.astype(BF16), v), 0.0)
            probs.append(p)
            q_heads.append(qh)

        ycat = jnp.concatenate([yp_ref[...], ys_ref[...], att], axis=1)
        gate = gate_ref[...]
        sig = jax.nn.sigmoid(gate)
        silu = gate * sig
        yg = (ycat * silu).astype(BF16)
        out = _mm(yg, w_ref[...])
        r = _rms(out)
        on = out * r
        g = g_ref[...]
        x = x_ref[...]
        err = x + on * g - tg_ref[...]
        loss_ref[...] += 0.5 * jnp.sum(jnp.mean(err * err, axis=-1, keepdims=True), axis=0, keepdims=True)
        dres = err * (1.0 / d)
        dres_ref[...] = dres
        gg_ref[...] += jnp.sum(dres * on, axis=0, keepdims=True)
        don = dres * g
        dout = (r * (don - on * jnp.mean(don * on, axis=-1, keepdims=True))).astype(BF16)
        acc[...] += _mm_tn(yg, dout)
        dyg = _mm_nt(dout, w_ref[...])
        dgate_ref[...] = dyg * ycat * (sig * (1.0 + gate * (1.0 - sig)))
        dycat = dyg * silu
        dyp_ref[...] = dycat[:, 0:pool_w]
        dys_ref[...] = dycat[:, pool_w : pool_w + ssm_w]
        datt = dycat[:, pool_w + ssm_w : mix]

        dq = jnp.zeros((cl, att_w), F32)
        dk = jnp.zeros((n_mem, att_w), F32)
        dv = jnp.zeros((n_mem, att_w), F32)
        for mask, p, qh in zip(heads, probs, q_heads):
            doh = jnp.where(mask, datt, 0.0).astype(BF16)
            dp = _mm_nt(doh, v)
            ds = (p * (dp - jnp.sum(p * dp, axis=-1, keepdims=True)) * qk_scale).astype(BF16)
            dq = dq + jnp.where(mask, _mm(ds, k), 0.0)
            dk = dk + _mm_tn(ds, qh)
            dv = dv + _mm_tn(p.astype(BF16), doh)
        dq_ref[...] = dq
        dkv_ref[:, 0:att_w] += dk
        dkv_ref[:, att_w : 2 * att_w] += dv

        @pl.when(i == n_tiles - 1)
        def _():
            cp = pltpu.make_async_copy(acc, gw_hbm, sem)
            cp.start()
            cp.wait()

    tok = lambda w: pl.BlockSpec((cl, w), lambda i: (i, 0))
    chunked = tok(ssm_w)
    per_batch = pl.BlockSpec((None, n_mem, kvw), lambda i: (i // per_seq, 0, 0))
    return pl.pallas_call(
        body,
        name="tail",
        grid=(n_tiles,),
        in_specs=[
            tok(d), tok(d), tok(mix), tok(pool_w), chunked, tok(att_w), per_batch,
            pl.BlockSpec((mix, d), lambda i: (0, 0)),
            pl.BlockSpec((1, d), lambda i: (0, 0)),
        ],
        out_specs=[
            tok(d), tok(mix), tok(pool_w), chunked, tok(att_w), per_batch,
            pl.BlockSpec(memory_space=pl.ANY),
            pl.BlockSpec((1, d), lambda i: (0, 0)),
            pl.BlockSpec((1, 1), lambda i: (0, 0)),
        ],
        out_shape=[
            jax.ShapeDtypeStruct((tokens, d), F32),
            jax.ShapeDtypeStruct((tokens, mix), F32),
            jax.ShapeDtypeStruct((tokens, pool_w), F32),
            jax.ShapeDtypeStruct((tokens, ssm_w), F32),
            jax.ShapeDtypeStruct((tokens, att_w), F32),
            jax.ShapeDtypeStruct(kv.shape, F32),
            jax.ShapeDtypeStruct((mix, d), F32),
            jax.ShapeDtypeStruct((1, d), F32),
            jax.ShapeDtypeStruct((1, 1), F32),
        ],
        scratch_shapes=[pltpu.VMEM((mix, d), F32), pltpu.SemaphoreType.DMA],
        compiler_params=_params("arbitrary"),
    )(x2, target2, gate, y_pool, y_ssm, q, kv, w_out, g_post)


def _pack(arrays):
    flat = jnp.concatenate([a.reshape(-1) for a in arrays])
    rows = -(-flat.size // (SUBLANES * LANES)) * SUBLANES
    return jnp.pad(flat, (0, rows * LANES - flat.size)).reshape(rows, LANES)


def _unpack(packed, like):
    flat, out, at = packed.reshape(-1), [], 0
    for a in like:
        out.append(flat[at : at + a.size].reshape(a.shape))
        at += a.size
    return out


def kernel(x, mem, g_pre, w_in, w_pool, pool_scale, a_re, a_im, log_dt, b_re, b_im, c_re, c_im, d_skip, w_glu, g_mem, w_kv, w_out, g_post, loss_target, m_g_pre, m_w_in, m_w_pool, m_pool_scale, m_a_re, m_a_im, m_log_dt, m_b_re, m_b_im, m_c_re, m_c_im, m_d_skip, m_w_glu, m_g_mem, m_w_kv, m_w_out, m_g_post, v_g_pre, v_w_in, v_w_pool, v_pool_scale, v_a_re, v_a_im, v_log_dt, v_b_re, v_b_im, v_c_re, v_c_im, v_d_skip, v_w_glu, v_g_mem, v_w_kv, v_w_out, v_g_post):
    batch, seq, d = x.shape
    cl = seq // NCH
    pool_w, ssm_w = pool_scale.shape[1], d_skip.shape[1]
    att_w = w_kv.shape[2] // 2
    tokens = batch * seq
    x2 = x.reshape(tokens, d)
    target2 = loss_target.reshape(tokens, d)

    w_in_g, w_out_g, w_kv_g, w_glu_g = _gather_weights([w_in[0], w_out[0], w_kv[0], w_glu[0]])
    w_out_f = w_out_g.reshape(N_DEV * w_out_g.shape[1], w_out_g.shape[2])
    w_kv_f = w_kv_g.reshape(N_DEV * w_kv_g.shape[1], w_kv_g.shape[2])
    w_glu_f = w_glu_g.transpose(1, 0, 2).reshape(w_glu_g.shape[1], N_DEV * w_glu_g.shape[2])

    wp_blk = jax.scipy.linalg.block_diag(*w_pool[0]).astype(BF16)
    ssm_params = (a_re[0], a_im[0], log_dt[0], b_re[0], b_im[0], c_re[0], c_im[0])
    (lam_row, b_tab, c_tab), tables_vjp = jax.vjp(_ssm_tables, *ssm_params)
    nb = b_tab.shape[0]
    lam8 = jnp.broadcast_to(lam_row, (SUBLANES, lam_row.shape[1]))
    lcl = _lam_power(a_re[0], a_im[0], log_dt[0], float(cl), 1.0, nb)
    lcl1 = _lam_power(a_re[0], a_im[0], log_dt[0], float(cl - 1), float(cl), nb)
    b_bf, c_bf = b_tab.astype(BF16), c_tab.astype(BF16)

    u_pool, u_ssm, q, gate = _in_proj(x2, g_pre, w_in_g, pool_w, ssm_w, att_w)
    y_pool = _pool_fwd(u_pool, wp_blk, pool_scale, batch, seq)
    u_ssm = u_ssm.reshape(batch, seq, ssm_w)
    y_ssm, pre_ssm, z_ssm = _ssm_fwd(u_ssm, b_bf, c_bf, lam8, lcl, d_skip, w_glu_f)
    kv = _kv_fwd(mem, g_mem, w_kv_f)

    dres, dgate, dy_pool, dy_ssm, dq, dkv, gw_out, gg_post, loss_part = _tail(
        x2, target2, gate, y_pool, y_ssm.reshape(tokens, ssm_w), q, kv, w_out_f, g_post)

    gw_kv, gg_mem = _kv_bwd(mem, dkv, g_mem, w_kv_f)
    du_pool, gwp_dense, g_scale = _pool_bwd(u_pool, dy_pool, wp_blk, pool_scale, batch, seq)
    du_ssm, gb_tab, gc_tab, gw_glu, gd_skip, glam = _ssm_bwd(
        u_ssm, pre_ssm, z_ssm, dy_ssm.reshape(batch, seq, ssm_w), b_bf, b_bf.transpose(0, 2, 1),
        c_bf.transpose(0, 2, 1), lam8, lcl, lcl1, d_skip, w_glu_f)
    grad_x2, gw_in, gg_pre = _in_proj_bwd(
        x2, dres, du_pool, du_ssm.reshape(tokens, ssm_w), dq, dgate, g_pre, w_in_g)

    gw = pool_w // len(POOL_WINDOWS)
    gw_pool = jnp.stack([gwp_dense[i * gw : (i + 1) * gw, i * gw : (i + 1) * gw] for i in range(len(POOL_WINDOWS))])
    g_ssm = tables_vjp((glam, gb_tab, gc_tab))

    small_w = [g_pre, w_pool, pool_scale, a_re, a_im, log_dt, b_re, b_im, c_re, c_im, d_skip, g_mem, g_post]
    small_m = [m_g_pre, m_w_pool, m_pool_scale, m_a_re, m_a_im, m_log_dt, m_b_re, m_b_im, m_c_re, m_c_im, m_d_skip, m_g_mem, m_g_post]
    small_v = [v_g_pre, v_w_pool, v_pool_scale, v_a_re, v_a_im, v_log_dt, v_b_re, v_b_im, v_c_re, v_c_im, v_d_skip, v_g_mem, v_g_post]
    small_g = [gg_pre, gw_pool, g_scale, *g_ssm, gd_skip, gg_mem, gg_post]
    scalar = jnp.zeros((1,), F32)
    sg, sd, sm, sv = _reduce_small(
        _pack(small_g + [loss_part]), _pack(small_w + [scalar]), _pack(small_m + [scalar]), _pack(small_v + [scalar]))
    like = small_w + [scalar]
    sg, sd, sm, sv = _unpack(sg, like), _unpack(sd, like), _unpack(sm, like), _unpack(sv, like)
    loss = sg[-1].reshape(())

    big = _reduce_big(
        [gw_in, gw_glu.reshape(ssm_w, N_DEV, -1).transpose(1, 0, 2), gw_kv.reshape(N_DEV, -1, gw_kv.shape[1]),
         gw_out.reshape(N_DEV, -1, gw_out.shape[1])],
        [w_in[0], w_glu[0], w_kv[0], w_out[0]],
        [m_w_in[0], m_w_glu[0], m_w_kv[0], m_w_out[0]],
        [v_w_in[0], v_w_glu[0], v_w_kv[0], v_w_out[0]])
    big = {name: tuple(t[None] for t in res) for name, res in zip(["w_in", "w_glu", "w_kv", "w_out"], big)}

    order = ["g_pre", "w_in", "w_pool", "pool_scale", "a_re", "a_im", "log_dt", "b_re", "b_im", "c_re", "c_im",
             "d_skip", "w_glu", "g_mem", "w_kv", "w_out", "g_post"]
    small_names = ["g_pre", "w_pool", "pool_scale", "a_re", "a_im", "log_dt", "b_re", "b_im", "c_re", "c_im",
                   "d_skip", "g_mem", "g_post"]
    outs = [[], [], [], []]
    for name in order:
        if name in big:
            parts = big[name]
        else:
            j = small_names.index(name)
            parts = (sg[j], sd[j], sm[j], sv[j])
        for kind in range(4):
            outs[kind].append(parts[kind])
    return (loss, grad_x2.reshape(batch, seq, d), *outs[0], *outs[1], *outs[2], *outs[3])
```

```python
import functools
import math

import jax
import jax.numpy as jnp
from jax import lax
from jax.experimental import pallas as pl
from jax.experimental.pallas import tpu as pltpu

F32 = jnp.float32
BF16 = jnp.bfloat16
MESH = pl.DeviceIdType.MESH

N_DEV = 8
NCH = 8
SUBLANES = 8
LANES = 128
VMEM_LIMIT = 56 * 1024 * 1024

EPS = 1e-6
POOL_WINDOWS = (2, 4, 8, 16)
MEM_HEADS = 4
SSM_GROUP = 16
SSM_N = 64
ADAM_LR, ADAM_B1, ADAM_B2, ADAM_EPS, ADAM_WD, ADAM_STEP = 0.001, 0.9, 0.999, 1e-08, 0.01, 10


def _mm(a, b):
    return jnp.dot(a, b, preferred_element_type=F32)


def _mm_nt(a, b):
    return lax.dot_general(a, b, (((1,), (1,)), ((), ())), preferred_element_type=F32)


def _mm_tn(a, b):
    return lax.dot_general(a, b, (((0,), (0,)), ((), ())), preferred_element_type=F32)


def _params(*sem):
    return pltpu.CompilerParams(dimension_semantics=sem or None, vmem_limit_bytes=VMEM_LIMIT)


def _adamw(w, g, m, v):
    m = ADAM_B1 * m + (1.0 - ADAM_B1) * g
    v = ADAM_B2 * v + (1.0 - ADAM_B2) * (g * g)
    m_hat = m / (1.0 - ADAM_B1**ADAM_STEP)
    v_hat = v / (1.0 - ADAM_B2**ADAM_STEP)
    delta = -ADAM_LR * (m_hat / (jnp.sqrt(v_hat) + ADAM_EPS) + ADAM_WD * w)
    return delta, m, v


def _gelu(x):
    k = math.sqrt(2.0 / math.pi)
    return 0.5 * x * (1.0 + jnp.tanh(k * (x + 0.044715 * x * x * x)))


def _gelu_grad(x):
    k = math.sqrt(2.0 / math.pi)
    th = jnp.tanh(k * (x + 0.044715 * x * x * x))
    return 0.5 * (1.0 + th) + 0.5 * x * (1.0 - th * th) * (k * (1.0 + 3.0 * 0.044715 * x * x))


def _place():
    return lax.axis_index("x"), lax.axis_index("y"), lax.axis_index("c")


def _gather_body(n, out_dtype, epilogue, n_extra_in, n_extra_out):
    def body(*refs):
        ins = refs[:n]
        extra_in = refs[n : n + n_extra_in]
        outs = refs[n + n_extra_in : 2 * n + n_extra_in]
        extra_out = refs[2 * n + n_extra_in : 2 * n + n_extra_in + n_extra_out]
        send_sems, recv_sems = refs[2 * n + n_extra_in + n_extra_out :]
        x, y, c = _place()
        me, sibling = (x, y, c), (x, y, 1 - c)
        chips = [(1 - x, y), (x, 1 - y), (1 - x, 1 - y)]

        def slot(px, py, pc):
            return 4 * px + 2 * py + pc

        def copy(a, k, block, to):
            ref = outs[a].at[slot(*block)]
            return pltpu.make_async_remote_copy(
                src_ref=ref, dst_ref=ref, send_sem=send_sems.at[a, k], recv_sem=recv_sems.at[a, k],
                device_id=to, device_id_type=MESH)

        for a in range(n):
            outs[a][slot(*me)] = ins[a][...].astype(out_dtype)
        first = []
        for a in range(n):
            first.append(copy(a, 0, me, sibling))
            first += [copy(a, 1 + j, me, (*chip, c)) for j, chip in enumerate(chips)]
        for cp in first:
            cp.start()
        passed = []
        for j, chip in enumerate(chips):
            for a in range(n):
                copy(a, 1 + j, (*chip, c), me).wait_recv()
                cp = copy(a, 4 + j, (*chip, c), sibling)
                cp.start()
                passed.append(cp)
        for a in range(n):
            copy(a, 0, sibling, me).wait_recv()
            for j, chip in enumerate(chips):
                copy(a, 4 + j, (*chip, 1 - c), me).wait_recv()
        for cp in first + passed:
            cp.wait_send()
        if epilogue is not None:
            epilogue(outs, extra_in, extra_out)

    return body


def _gather_weights(blocks):
    n = len(blocks)
    vmem = pl.BlockSpec(memory_space=pltpu.VMEM)
    return pl.pallas_call(
        _gather_body(n, BF16, None, 0, 0),
        name="gather_weights",
        out_shape=[jax.ShapeDtypeStruct((N_DEV, *b.shape), BF16) for b in blocks],
        in_specs=[vmem] * n,
        out_specs=[vmem] * n,
        scratch_shapes=[pltpu.SemaphoreType.DMA((n, 7)), pltpu.SemaphoreType.DMA((n, 7))],
        compiler_params=_params(),
    )(*blocks)


def _reduce_small(part, w, m, v):
    rows = part.shape[0]

    def epilogue(outs, extra_in, extra_out):
        gathered = outs[0]
        w_ref, m_ref, v_ref = extra_in
        g_ref, d_ref, nm_ref, nv_ref = extra_out
        chunk = SUBLANES

        def step(i, _):
            r = pl.ds(pl.multiple_of(i * chunk, chunk), chunk)
            g = gathered[0, r, :]
            for d in range(1, N_DEV):
                g = g + gathered[d, r, :]
            delta, nm, nv = _adamw(w_ref[r, :], g, m_ref[r, :], v_ref[r, :])
            g_ref[r, :] = g
            d_ref[r, :] = delta
            nm_ref[r, :] = nm
            nv_ref[r, :] = nv
            return 0

        lax.fori_loop(0, rows // chunk, step, 0)

    vmem = pl.BlockSpec(memory_space=pltpu.VMEM)
    flat = jax.ShapeDtypeStruct((rows, LANES), F32)
    res = pl.pallas_call(
        _gather_body(1, F32, epilogue, 3, 4),
        name="reduce_small",
        out_shape=[jax.ShapeDtypeStruct((N_DEV, rows, LANES), F32), flat, flat, flat, flat],
        in_specs=[vmem] * 4,
        out_specs=[vmem] * 5,
        scratch_shapes=[pltpu.SemaphoreType.DMA((1, 7)), pltpu.SemaphoreType.DMA((1, 7))],
        compiler_params=_params(),
    )(part, w, m, v)
    return res[1:]


def _reduce_big(parts, ws, ms, vs):
    n = len(parts)
    parts4 = [p.reshape(4, 2, *p.shape[1:]) for p in parts]
    blks = [p.shape[1:] for p in parts]

    def body(*refs):
        part = refs[:n]
        w_in, m_in, v_in = refs[n : 2 * n], refs[2 * n : 3 * n], refs[3 * n : 4 * n]
        outs = refs[4 * n : 8 * n]
        own, r1, r2 = refs[8 * n : 9 * n], refs[9 * n : 10 * n], refs[10 * n : 11 * n]
        s1_send, s1_recv, s2_send, s2_recv, loc = refs[11 * n :]
        x, y, c = _place()
        sibling = (x, y, 1 - c)
        chips = [(1 - x, y), (x, 1 - y), (1 - x, 1 - y)]

        def rowwise(rows, fn):
            chunk = math.gcd(rows, 128)

            def step(i, _):
                fn(pl.ds(pl.multiple_of(i * chunk, chunk), chunk))
                return 0

            lax.fori_loop(0, rows // chunk, step, 0)

        stage1, local = [], []
        for a in range(n):
            cp = pltpu.make_async_remote_copy(
                src_ref=part[a].at[:, 1 - c], dst_ref=r1[a], send_sem=s1_send.at[a], recv_sem=s1_recv.at[a],
                device_id=sibling, device_id_type=MESH)
            cp.start()
            stage1.append(cp)
            lc = pltpu.make_async_copy(part[a].at[:, c], own[a], loc.at[a])
            lc.start()
            local.append(lc)
        stage2 = []
        for a in range(n):
            local[a].wait()
            stage1[a].wait_recv()
            for chip in range(4):

                def add(r, a=a, chip=chip):
                    own[a][chip, r, :] = own[a][chip, r, :] + r1[a][chip, r, :]

                rowwise(blks[a][0], add)
            for k, chip in enumerate(chips):
                cp = pltpu.make_async_remote_copy(
                    src_ref=own[a].at[2 * chip[0] + chip[1]], dst_ref=r2[a].at[k],
                    send_sem=s2_send.at[a, k], recv_sem=s2_recv.at[a, k],
                    device_id=(*chip, c), device_id_type=MESH)
                cp.start()
                stage2.append(cp)
        for a in range(n):
            for k, chip in enumerate(chips):
                stage2[3 * a + k].wait_recv()
            g_ref, d_ref, nm_ref, nv_ref = outs[4 * a : 4 * a + 4]

            def update(r, a=a, g_ref=g_ref, d_ref=d_ref, nm_ref=nm_ref, nv_ref=nv_ref):
                g = own[a][2 * x + y, r, :] + r2[a][0, r, :] + r2[a][1, r, :] + r2[a][2, r, :]
                delta, nm, nv = _adamw(w_in[a][r, :], g, m_in[a][r, :], v_in[a][r, :])
                g_ref[r, :] = g
                d_ref[r, :] = delta
                nm_ref[r, :] = nm
                nv_ref[r, :] = nv

            rowwise(blks[a][0], update)
        for cp in stage1 + stage2:
            cp.wait_send()

    vmem = pl.BlockSpec(memory_space=pltpu.VMEM)
    hbm = pl.BlockSpec(memory_space=pl.ANY)
    out_shape = []
    for b in blks:
        out_shape += [jax.ShapeDtypeStruct(b, F32)] * 4
    scratch = (
        [pltpu.VMEM((4, *b), F32) for b in blks]
        + [pltpu.VMEM((4, *b), F32) for b in blks]
        + [pltpu.VMEM((3, *b), F32) for b in blks]
        + [pltpu.SemaphoreType.DMA((n,)), pltpu.SemaphoreType.DMA((n,)), pltpu.SemaphoreType.DMA((n, 3)),
           pltpu.SemaphoreType.DMA((n, 3)), pltpu.SemaphoreType.DMA((n,))]
    )
    res = pl.pallas_call(
        body,
        name="reduce_big",
        out_shape=out_shape,
        in_specs=[hbm] * n + [vmem] * (3 * n),
        out_specs=[vmem] * (4 * n),
        scratch_shapes=scratch,
        compiler_params=_params(),
    )(*parts4, *ws, *ms, *vs)
    return [tuple(res[4 * a : 4 * a + 4]) for a in range(n)]


def _rms(x):
    return lax.rsqrt(jnp.mean(x * x, axis=-1, keepdims=True) + EPS)


def _token_tile(tokens, want):
    tile = min(want, tokens // 2)
    assert tokens % tile == 0 and tile % 16 == 0
    return tile


def _in_proj(x2, g_pre, w_in_g, pool_w, ssm_w, att_w):
    tokens, d = x2.shape
    nb = w_in_g.shape[2]
    mix = pool_w + ssm_w + att_w
    half = N_DEV // 2
    cl = _token_tile(tokens, 512)
    assert half * nb == mix and nb == 256 and pool_w == 384 and ssm_w == 384 and att_w == 256

    def body(x_ref, g_ref, w_ref, up_ref, us_ref, q_ref, gate_ref):
        x = x_ref[...]
        h = (x * _rms(x) * g_ref[...]).astype(BF16)
        p = [_mm(h, w_ref[j]) for j in range(half)]
        up_ref[:, 0:256] = p[0]
        up_ref[:, 256:384] = p[1][:, 0:128]
        us_ref[:, 0:128] = p[1][:, 128:256]
        us_ref[:, 128:384] = p[2]
        q_ref[...] = p[3]
        for j in range(half):
            gate_ref[:, j * nb : (j + 1) * nb] = _mm(h, w_ref[half + j])

    return pl.pallas_call(
        body,
        name="in_proj",
        grid=(tokens // cl,),
        in_specs=[
            pl.BlockSpec((cl, d), lambda i: (i, 0)),
            pl.BlockSpec((1, d), lambda i: (0, 0)),
            pl.BlockSpec((N_DEV, d, nb), lambda i: (0, 0, 0)),
        ],
        out_specs=[
            pl.BlockSpec((cl, pool_w), lambda i: (i, 0)),
            pl.BlockSpec((cl, ssm_w), lambda i: (i, 0)),
            pl.BlockSpec((cl, att_w), lambda i: (i, 0)),
            pl.BlockSpec((cl, mix), lambda i: (i, 0)),
        ],
        out_shape=[
            jax.ShapeDtypeStruct((tokens, pool_w), F32),
            jax.ShapeDtypeStruct((tokens, ssm_w), F32),
            jax.ShapeDtypeStruct((tokens, att_w), F32),
            jax.ShapeDtypeStruct((tokens, mix), F32),
        ],
        compiler_params=_params("arbitrary"),
    )(x2, g_pre, w_in_g)


def _in_proj_bwd(x2, dres, du_pool, du_ssm, dq, dgate, g_pre, w_in_g):
    tokens, d = x2.shape
    nb = w_in_g.shape[2]
    pool_w, ssm_w, att_w, mix = du_pool.shape[1], du_ssm.shape[1], dq.shape[1], dgate.shape[1]
    half = N_DEV // 2
    cl = _token_tile(tokens, 512)
    n_tiles = tokens // cl

    def body(x_ref, dres_ref, dup_ref, dus_ref, dq_ref, dgate_ref, g_ref, w_ref, gx_ref, gw_hbm, gg_ref, acc, sem):
        i = pl.program_id(0)

        @pl.when(i == 0)
        def _():
            acc[...] = jnp.zeros_like(acc)
            gg_ref[...] = jnp.zeros_like(gg_ref)

        x = x_ref[...]
        r = _rms(x)
        xn = x * r
        g = g_ref[...]
        h = (xn * g).astype(BF16)
        dval = jnp.concatenate([dup_ref[...], dus_ref[...], dq_ref[...]], axis=1)
        dh = jnp.zeros((cl, d), F32)
        for j in range(N_DEV):
            src = dval if j < half else dgate_ref[...]
            jj = j % half
            dp = src[:, jj * nb : (jj + 1) * nb].astype(BF16)
            dh = dh + _mm_nt(dp, w_ref[j])
            acc[j] += _mm_tn(h, dp)
        gg_ref[...] += jnp.sum(dh * xn, axis=0, keepdims=True)
        dxn = dh * g
        gx_ref[...] = dres_ref[...] + r * (dxn - xn * jnp.mean(dxn * xn, axis=-1, keepdims=True))

        @pl.when(i == n_tiles - 1)
        def _():
            cp = pltpu.make_async_copy(acc, gw_hbm, sem)
            cp.start()
            cp.wait()

    return pl.pallas_call(
        body,
        name="in_proj_bwd",
        grid=(n_tiles,),
        in_specs=[
            pl.BlockSpec((cl, d), lambda i: (i, 0)),
            pl.BlockSpec((cl, d), lambda i: (i, 0)),
            pl.BlockSpec((cl, pool_w), lambda i: (i, 0)),
            pl.BlockSpec((cl, ssm_w), lambda i: (i, 0)),
            pl.BlockSpec((cl, att_w), lambda i: (i, 0)),
            pl.BlockSpec((cl, mix), lambda i: (i, 0)),
            pl.BlockSpec((1, d), lambda i: (0, 0)),
            pl.BlockSpec((N_DEV, d, nb), lambda i: (0, 0, 0)),
        ],
        out_specs=[
            pl.BlockSpec((cl, d), lambda i: (i, 0)),
            pl.BlockSpec(memory_space=pl.ANY),
            pl.BlockSpec((1, d), lambda i: (0, 0)),
        ],
        out_shape=[
            jax.ShapeDtypeStruct((tokens, d), F32),
            jax.ShapeDtypeStruct((N_DEV, d, nb), F32),
            jax.ShapeDtypeStruct((1, d), F32),
        ],
        scratch_shapes=[pltpu.VMEM((N_DEV, d, nb), F32), pltpu.SemaphoreType.DMA],
        compiler_params=_params("arbitrary"),
    )(x2, dres, du_pool, du_ssm, dq, dgate, g_pre, w_in_g)


def _pool_geometry(seq, width):
    gw = width // len(POOL_WINDOWS)
    col = lax.broadcasted_iota(jnp.int32, (1, width), 1)
    win = jnp.full((1, width), float(POOL_WINDOWS[-1]), F32)
    for gi in range(len(POOL_WINDOWS) - 2, -1, -1):
        win = jnp.where(col < (gi + 1) * gw, float(POOL_WINDOWS[gi]), win)
    row = lax.broadcasted_iota(jnp.int32, (seq, width), 0)
    cnt = jnp.minimum((row + 1).astype(F32), win)
    return win, row, cnt


def _window_sums(a, row, win, seq, back):
    sums = []
    s, k = a, 1
    while k < POOL_WINDOWS[-1]:
        if back:
            shifted = jnp.where(row < seq - k, pltpu.roll(s, seq - k, 0), 0.0)
        else:
            shifted = jnp.where(row >= k, pltpu.roll(s, k, 0), 0.0)
        s = s + shifted
        k *= 2
        sums.append((k, s))
    out = sums[-1][1]
    for k, s in reversed(sums[:-1]):
        out = jnp.where(win <= float(k), s, out)
    return out


def _pool_fwd(u2, wp_blk, scale, batch, seq):
    width = u2.shape[1]

    def body(u_ref, w_ref, s_ref, y_ref):
        u = u_ref[...]
        win, row, cnt = _pool_geometry(seq, width)
        diff = _window_sums(u, row, win, seq, False) / cnt - u
        y_ref[...] = _mm(diff.astype(BF16), w_ref[...]) * s_ref[...]

    return pl.pallas_call(
        body,
        name="pool_fwd",
        grid=(batch,),
        in_specs=[
            pl.BlockSpec((seq, width), lambda b: (b, 0)),
            pl.BlockSpec((width, width), lambda b: (0, 0)),
            pl.BlockSpec((1, width), lambda b: (0, 0)),
        ],
        out_specs=pl.BlockSpec((seq, width), lambda b: (b, 0)),
        out_shape=jax.ShapeDtypeStruct(u2.shape, F32),
        compiler_params=_params("arbitrary"),
    )(u2, wp_blk, scale)


def _pool_bwd(u2, dy2, wp_blk, scale, batch, seq):
    width = u2.shape[1]

    def body(u_ref, dy_ref, w_ref, s_ref, du_ref, gw_ref, gs_ref):
        @pl.when(pl.program_id(0) == 0)
        def _():
            gw_ref[...] = jnp.zeros_like(gw_ref)
            gs_ref[...] = jnp.zeros_like(gs_ref)

        u = u_ref[...]
        dy = dy_ref[...]
        win, row, cnt = _pool_geometry(seq, width)
        diff = (_window_sums(u, row, win, seq, False) / cnt - u).astype(BF16)
        gs_ref[...] += jnp.sum(dy * _mm(diff, w_ref[...]), axis=0, keepdims=True)
        dys = (dy * s_ref[...]).astype(BF16)
        gw_ref[...] += _mm_tn(diff, dys)
        dd = _mm_nt(dys, w_ref[...])
        du_ref[...] = _window_sums(dd / cnt, row, win, seq, True) - dd

    return pl.pallas_call(
        body,
        name="pool_bwd",
        grid=(batch,),
        in_specs=[
            pl.BlockSpec((seq, width), lambda b: (b, 0)),
            pl.BlockSpec((seq, width), lambda b: (b, 0)),
            pl.BlockSpec((width, width), lambda b: (0, 0)),
            pl.BlockSpec((1, width), lambda b: (0, 0)),
        ],
        out_specs=[
            pl.BlockSpec((seq, width), lambda b: (b, 0)),
            pl.BlockSpec((width, width), lambda b: (0, 0)),
            pl.BlockSpec((1, width), lambda b: (0, 0)),
        ],
        out_shape=[
            jax.ShapeDtypeStruct(u2.shape, F32),
            jax.ShapeDtypeStruct((width, width), F32),
            jax.ShapeDtypeStruct((1, width), F32),
        ],
        compiler_params=_params("arbitrary"),
    )(u2, dy2, wp_blk, scale)


def _state_row(z, n_blocks):
    re = jnp.real(z).reshape(n_blocks, -1)
    im = jnp.imag(z).reshape(n_blocks, -1)
    return jnp.concatenate([re, im], axis=1).reshape(1, -1)


def _ssm_tables(a_re, a_im, log_dt, b_re, b_im, c_re, c_im):
    groups, n_state = a_re.shape
    ch = b_re.shape[2]
    nb = groups * ch // LANES
    gl = groups // nb
    lam = lax.complex(a_re, a_im)
    lam_bar = jnp.exp(lam * jnp.exp(log_dt)[:, None])
    b_bar = ((lam_bar - 1.0) / lam)[..., None] * lax.complex(b_re, b_im)
    eye = jnp.eye(gl, dtype=F32)

    def rows_to_state(t):
        return jnp.einsum("sgnc,gh->sgchn", t.reshape(nb, gl, n_state, ch), eye).reshape(nb, gl * ch, gl * n_state)

    def state_to_rows(t):
        return jnp.einsum("sgcn,gh->shngc", t.reshape(nb, gl, ch, n_state), eye).reshape(nb, gl * n_state, gl * ch)

    b_tab = jnp.concatenate([rows_to_state(jnp.real(b_bar)), rows_to_state(jnp.imag(b_bar))], axis=2)
    c_tab = jnp.concatenate([state_to_rows(c_re), -state_to_rows(c_im)], axis=1)
    return _state_row(lam_bar, nb), b_tab, c_tab


def _lam_power(a_re, a_im, log_dt, power, scale, n_blocks):
    return _state_row(scale * jnp.exp(lax.complex(a_re, a_im) * jnp.exp(log_dt)[:, None] * power), n_blocks)


def _state_blocks(s2, n_blocks, width):
    half = s2 // n_blocks // 2
    assert half % width == 0
    return [(b * 2 * half + o, b * 2 * half + half + o) for b in range(n_blocks) for o in range(0, half, width)]


def _scan(src_ref, dst_ref, st_ref, lam8_ref, n_groups, s, n_blocks, reverse, store):
    lb = 512
    for re0, im0 in _state_blocks(2 * s, n_blocks, lb):
        cr, ci = pl.ds(re0, lb), pl.ds(im0, lb)
        lr = lam8_ref[:, cr]
        li = -lam8_ref[:, ci] if reverse else lam8_ref[:, ci]

        def step(i, carry, cr=cr, ci=ci, lr=lr, li=li):
            hr, hi = carry
            grp = n_groups - 1 - i if reverse else i
            rows = pl.ds(pl.multiple_of(grp * SUBLANES, SUBLANES), SUBLANES)
            nr = lr * hr - li * hi + src_ref[rows, cr]
            ni = lr * hi + li * hr + src_ref[rows, ci]
            if store:
                dst_ref[rows, cr] = nr
                dst_ref[rows, ci] = ni
            return nr, ni

        hr, hi = lax.fori_loop(0, n_groups, step, (st_ref[:, cr], st_ref[:, ci]), unroll=2)
        st_ref[:, cr] = hr
        st_ref[:, ci] = hi


def _scan_adjoint(dh_ref, bu_ref, stg_ref, ste_ref, acc_ref, lam8_ref, n_groups, s, n_blocks, final):
    lb = 256
    for re0, im0 in _state_blocks(2 * s, n_blocks, lb):
        cr, ci = pl.ds(re0, lb), pl.ds(im0, lb)
        lr, li = lam8_ref[:, cr], -lam8_ref[:, ci]

        def step(i, carry, cr=cr, ci=ci, lr=lr, li=li):
            gr, gi, er, ei, ar, ai = carry
            grp = n_groups - 1 - i
            rows = pl.ds(pl.multiple_of(grp * SUBLANES, SUBLANES), SUBLANES)
            ner = gr + lr * er - li * ei
            nei = gi + lr * ei + li * er
            ngr = lr * gr - li * gi + dh_ref[rows, cr]
            ngi = lr * gi + li * gr + dh_ref[rows, ci]
            if final:
                br, bi = bu_ref[rows, cr], bu_ref[rows, ci]
                ar = ar + br * ner + bi * nei
                ai = ai + br * nei - bi * ner
                dh_ref[rows, cr] = ngr
                dh_ref[rows, ci] = ngi
            return ngr, ngi, ner, nei, ar, ai

        init = (stg_ref[:, cr], stg_ref[:, ci], ste_ref[:, cr], ste_ref[:, ci], acc_ref[:, cr], acc_ref[:, ci])
        gr, gi, er, ei, ar, ai = lax.fori_loop(0, n_groups, step, init, unroll=2)
        stg_ref[:, cr] = gr
        stg_ref[:, ci] = gi
        ste_ref[:, cr] = er
        ste_ref[:, ci] = ei
        if final:
            acc_ref[:, cr] = ar
            acc_ref[:, ci] = ai


def _chunk_starts(st_ref, init_ref, lcl_ref, s, n_blocks):
    w = s // n_blocks
    init_ref[0:1, :] = jnp.zeros((1, 2 * s), F32)
    for re0, im0 in _state_blocks(2 * s, n_blocks, w):
        re, im = pl.ds(re0, w), pl.ds(im0, w)
        ar, ai = lcl_ref[:, re], lcl_ref[:, im]
        cr = jnp.zeros((1, w), F32)
        ci = jnp.zeros((1, w), F32)
        for k in range(1, NCH):
            cr, ci = (ar * cr - ai * ci + st_ref[k - 1 : k, re], ar * ci + ai * cr + st_ref[k - 1 : k, im])
            init_ref[k : k + 1, re] = cr
            init_ref[k : k + 1, im] = ci


def _chunk_starts_adjoint(stg_ref, ste_ref, initg_ref, inite_ref, lcl_ref, lcl1_ref, s, n_blocks):
    w = s // n_blocks
    initg_ref[NCH - 1 : NCH, :] = jnp.zeros((1, 2 * s), F32)
    inite_ref[NCH - 1 : NCH, :] = jnp.zeros((1, 2 * s), F32)
    for re0, im0 in _state_blocks(2 * s, n_blocks, w):
        re, im = pl.ds(re0, w), pl.ds(im0, w)
        ar, ai = lcl_ref[:, re], -lcl_ref[:, im]
        pr, pi = lcl1_ref[:, re], -lcl1_ref[:, im]
        zero = jnp.zeros((1, w), F32)
        gr, gi, er, ei = zero, zero, zero, zero
        for k in range(NCH - 2, -1, -1):
            ner = ste_ref[k + 1 : k + 2, re] + (ar * er - ai * ei) + (pr * gr - pi * gi)
            nei = ste_ref[k + 1 : k + 2, im] + (ar * ei + ai * er) + (pr * gi + pi * gr)
            ngr = stg_ref[k + 1 : k + 2, re] + ar * gr - ai * gi
            ngi = stg_ref[k + 1 : k + 2, im] + ar * gi + ai * gr
            gr, gi, er, ei = ngr, ngi, ner, nei
            initg_ref[k : k + 1, re] = gr
            initg_ref[k : k + 1, im] = gi
            inite_ref[k : k + 1, re] = er
            inite_ref[k : k + 1, im] = ei


def _ssm_rows(seq):
    rows = min(512, seq // 2)
    assert seq % rows == 0 and rows % SUBLANES == 0
    return rows


def _chunk_copies(hbm_ref, b, cm_ref, sems, to_cm):
    cl = cm_ref.shape[0]
    copies = []
    for k in range(NCH):
        nat, cm = hbm_ref.at[b, pl.ds(k * cl, cl), :], cm_ref.at[:, k, :]
        src, dst = (nat, cm) if to_cm else (cm, nat)
        copies.append(pltpu.make_async_copy(src, dst, sems.at[k]))
    return copies


def _blockwise(fn, n_blocks):
    return jnp.concatenate([fn(b) for b in range(n_blocks)], axis=1)


def _ssm_fwd(u, b_tab, c_tab, lam8, lcl, d_skip, w_glu):
    batch, seq, width = u.shape
    s = lam8.shape[1] // 2
    nb = b_tab.shape[0]
    sb = 2 * s // nb
    cl = seq // NCH
    rows = _ssm_rows(seq)
    n_tiles = seq // rows
    n_groups = rows // SUBLANES

    def body(u_hbm, b_ref, c_ref, lam_ref, lcl_ref, d_ref, wg_ref, y_hbm, pre_ref, z_ref, init_ref,
             u_cm, y_cm, bu, st, sems):
        b, ph, t = pl.program_id(0), pl.program_id(1), pl.program_id(2)
        tile_groups = pl.ds(pl.multiple_of(t * n_groups, n_groups), n_groups)

        @pl.when((ph == 0) & (t == 0))
        def _():
            loads = _chunk_copies(u_hbm, b, u_cm, sems, True)
            for cp in loads:
                cp.start()
            st[...] = jnp.zeros_like(st)
            for cp in loads:
                cp.wait()

        @pl.when((ph == 1) & (t == 0))
        def _():
            st[...] = init_ref[...]

        u_t = u_cm[tile_groups].reshape(rows, width)
        u_b = u_t.astype(BF16)
        for blk in range(nb):
            bu[:, blk * sb : (blk + 1) * sb] = _mm(u_b[:, blk * LANES : (blk + 1) * LANES], b_ref[blk])

        @pl.when(ph == 0)
        def _():
            _scan(bu, bu, st, lam_ref, n_groups, s, nb, False, False)

        @pl.when((ph == 0) & (t == n_tiles - 1))
        def _():
            _chunk_starts(st, init_ref, lcl_ref, s, nb)

        @pl.when(ph == 1)
        def _():
            _scan(bu, bu, st, lam_ref, n_groups, s, nb, False, True)
            hs = lambda blk: _mm(bu[:, blk * sb : (blk + 1) * sb].astype(BF16), c_ref[blk])
            pre = _blockwise(hs, nb) + d_ref[...] * u_t
            z = _mm(_gelu(pre).astype(BF16), wg_ref[...])
            pre_ref[...] = pre
            z_ref[...] = z
            y = z[:, 0:width] * jax.nn.sigmoid(z[:, width : 2 * width])
            y_cm[tile_groups] = y.reshape(n_groups, SUBLANES, width)

        @pl.when((ph == 1) & (t == n_tiles - 1))
        def _():
            stores = _chunk_copies(y_hbm, b, y_cm, sems, False)
            for cp in stores:
                cp.start()
            for cp in stores:
                cp.wait()

    out_tile = lambda b, ph, t: (b, t * ph, 0)
    full = lambda a: pl.BlockSpec(a.shape, lambda b, ph, t: (0,) * a.ndim)
    hbm = pl.BlockSpec(memory_space=pl.ANY)
    return pl.pallas_call(
        body,
        name="ssm_fwd",
        grid=(batch, 2, n_tiles),
        in_specs=[hbm, full(b_tab), full(c_tab), full(lam8), full(lcl), full(d_skip), full(w_glu)],
        out_specs=[
            hbm,
            pl.BlockSpec((None, rows, width), out_tile),
            pl.BlockSpec((None, rows, 2 * width), out_tile),
            pl.BlockSpec((None, SUBLANES, 2 * s), lambda b, ph, t: (b, 0, 0)),
        ],
        out_shape=[
            jax.ShapeDtypeStruct((batch, seq, width), F32),
            jax.ShapeDtypeStruct((batch, seq, width), F32),
            jax.ShapeDtypeStruct((batch, seq, 2 * width), F32),
            jax.ShapeDtypeStruct((batch, SUBLANES, 2 * s), F32),
        ],
        scratch_shapes=[
            pltpu.VMEM((cl, NCH, width), F32),
            pltpu.VMEM((cl, NCH, width), F32),
            pltpu.VMEM((rows, 2 * s), F32),
            pltpu.VMEM((SUBLANES, 2 * s), F32),
            pltpu.SemaphoreType.DMA((NCH,)),
        ],
        compiler_params=_params("arbitrary", "arbitrary", "arbitrary"),
    )(u, b_tab, c_tab, lam8, lcl, d_skip, w_glu)


def _ssm_bwd(u, pre_p, z_p, dy, init, b_tab, b_tab_t, c_tab_t, lam8, lcl, lcl1, d_skip, w_glu):
    batch, seq, width = u.shape
    s = lam8.shape[1] // 2
    nb = b_tab.shape[0]
    sb = 2 * s // nb
    cl = seq // NCH
    rows = _ssm_rows(seq)
    n_tiles = seq // rows
    n_groups = rows // SUBLANES

    def body(u_hbm, pre_ref, z_ref, dy_hbm, init_ref, b_ref, bt_ref, ct_ref, lam_ref, lcl_ref, lcl1_ref, d_ref, wg_ref,
             du_hbm, gb_ref, gc_ref, gwg_ref, gd_ref, glam_ref,
             u_cm, dy_cm, bu, dh, dpre_all, st, stg, ste, initg, inite, acc, sems):
        b, ph, t = pl.program_id(0), pl.program_id(1), pl.program_id(2)
        first = (b == 0) & (ph == 0) & (t == 0)
        last = (b == batch - 1) & (ph == 2) & (t == n_tiles - 1)
        tile = jnp.where(ph == 0, t, n_tiles - 1 - t)
        tile_rows = pl.ds(pl.multiple_of(tile * rows, rows), rows)
        tile_groups = pl.ds(pl.multiple_of(tile * n_groups, n_groups), n_groups)
        lanes = lambda blk: slice(blk * LANES, (blk + 1) * LANES)
        states = lambda blk: slice(blk * sb, (blk + 1) * sb)

        @pl.when(first)
        def _():
            acc[...] = jnp.zeros_like(acc)
            gb_ref[...] = jnp.zeros_like(gb_ref)
            gc_ref[...] = jnp.zeros_like(gc_ref)
            gwg_ref[...] = jnp.zeros_like(gwg_ref)
            gd_ref[...] = jnp.zeros_like(gd_ref)

        @pl.when((ph == 0) & (t == 0))
        def _():
            loads = _chunk_copies(u_hbm, b, u_cm, sems.at[0], True) + _chunk_copies(dy_hbm, b, dy_cm, sems.at[1], True)
            for cp in loads:
                cp.start()
            st[...] = init_ref[...]
            for cp in loads:
                cp.wait()

        u_t = u_cm[tile_groups].reshape(rows, width)
        u_b = u_t.astype(BF16)

        @pl.when((ph == 0) | (ph == 2))
        def _():
            for blk in range(nb):
                bu[:, states(blk)] = _mm(u_b[:, lanes(blk)], b_ref[blk])

        @pl.when(ph == 0)
        def _():
            _scan(bu, bu, st, lam_ref, n_groups, s, nb, False, True)
            z = z_ref[...]
            dy_t = dy_cm[tile_groups].reshape(rows, width)
            pre = pre_ref[...]
            z1, sig = z[:, 0:width], jax.nn.sigmoid(z[:, width : 2 * width])
            dz = jnp.concatenate([dy_t * sig, dy_t * z1 * sig * (1.0 - sig)], axis=1).astype(BF16)
            gwg_ref[...] += _mm_tn(_gelu(pre).astype(BF16), dz)
            dpre = _mm_nt(dz, wg_ref[...]) * _gelu_grad(pre)
            dpre_all[tile_rows, :] = dpre
            gd_ref[...] += jnp.sum(dpre * u_t, axis=0, keepdims=True)
            dpre_b = dpre.astype(BF16)
            for blk in range(nb):
                gc_ref[blk] += _mm_tn(bu[:, states(blk)].astype(BF16), dpre_b[:, lanes(blk)])

        @pl.when(ph >= 1)
        def _():
            dpre_b = dpre_all[tile_rows, :].astype(BF16)
            for blk in range(nb):
                dh[:, states(blk)] = _mm(dpre_b[:, lanes(blk)], ct_ref[blk])

        @pl.when(ph == 1)
        def _():
            @pl.when(t == 0)
            def _():
                stg[...] = jnp.zeros_like(stg)
                ste[...] = jnp.zeros_like(ste)

            _scan_adjoint(dh, bu, stg, ste, acc, lam_ref, n_groups, s, nb, False)

            @pl.when(t == n_tiles - 1)
            def _():
                _chunk_starts_adjoint(stg, ste, initg, inite, lcl_ref, lcl1_ref, s, nb)

        @pl.when(ph == 2)
        def _():
            @pl.when(t == 0)
            def _():
                stg[...] = initg[...]
                ste[...] = inite[...]

            _scan_adjoint(dh, bu, stg, ste, acc, lam_ref, n_groups, s, nb, True)
            du = lambda blk: _mm(dh[:, states(blk)].astype(BF16), bt_ref[blk])
            du_t = _blockwise(du, nb) + dpre_all[tile_rows, :] * d_ref[...]
            dy_cm[tile_groups] = du_t.reshape(n_groups, SUBLANES, width)
            for blk in range(nb):
                gb_ref[blk] += _mm_tn(u_b[:, lanes(blk)], dh[:, states(blk)].astype(BF16))

            @pl.when(t == n_tiles - 1)
            def _():
                stores = _chunk_copies(du_hbm, b, dy_cm, sems.at[0], False)
                for cp in stores:
                    cp.start()
                for cp in stores:
                    cp.wait()

        @pl.when(last)
        def _():
            glam_ref[...] = jnp.sum(acc[...], axis=0, keepdims=True)

    def tile(b, ph, t):
        return (b, jnp.where(ph == 0, t, n_tiles - 1 - t), 0)

    full = lambda a: pl.BlockSpec(a.shape, lambda b, ph, t: (0,) * a.ndim)
    hbm = pl.BlockSpec(memory_space=pl.ANY)
    return pl.pallas_call(
        body,
        name="ssm_bwd",
        grid=(batch, 3, n_tiles),
        in_specs=[
            hbm,
            pl.BlockSpec((None, rows, width), tile),
            pl.BlockSpec((None, rows, 2 * width), tile),
            hbm,
            pl.BlockSpec((None, SUBLANES, 2 * s), lambda b, ph, t: (b, 0, 0)),
            full(b_tab), full(b_tab_t), full(c_tab_t), full(lam8), full(lcl), full(lcl1), full(d_skip), full(w_glu),
        ],
        out_specs=[
            hbm,
            full(b_tab), full(b_tab_t), full(w_glu), full(d_skip),
            pl.BlockSpec((1, 2 * s), lambda b, ph, t: (0, 0)),
        ],
        out_shape=[
            jax.ShapeDtypeStruct((batch, seq, width), F32),
            jax.ShapeDtypeStruct(b_tab.shape, F32),
            jax.ShapeDtypeStruct(b_tab_t.shape, F32),
            jax.ShapeDtypeStruct(w_glu.shape, F32),
            jax.ShapeDtypeStruct(d_skip.shape, F32),
            jax.ShapeDtypeStruct((1, 2 * s), F32),
        ],
        scratch_shapes=[
            pltpu.VMEM((cl, NCH, width), F32),
            pltpu.VMEM((cl, NCH, width), F32),
            pltpu.VMEM((rows, 2 * s), F32),
            pltpu.VMEM((rows, 2 * s), F32),
            pltpu.VMEM((seq, width), F32),
        ]
        + [pltpu.VMEM((SUBLANES, 2 * s), F32)] * 6
        + [pltpu.SemaphoreType.DMA((2, NCH))],
        compiler_params=_params("arbitrary", "arbitrary", "arbitrary"),
    )(u, pre_p, z_p, dy, init, b_tab, b_tab_t, c_tab_t, lam8, lcl, lcl1, d_skip, w_glu)


def _kv_fwd(mem, g_mem, w_kv):
    batch, n_mem, d = mem.shape
    kvw = w_kv.shape[1]

    def body(mem_ref, g_ref, w_ref, kv_ref):
        m = mem_ref[...]
        kv_ref[...] = _mm((m * _rms(m) * g_ref[...]).astype(BF16), w_ref[...])

    return pl.pallas_call(
        body,
        name="kv_fwd",
        grid=(batch,),
        in_specs=[
            pl.BlockSpec((None, n_mem, d), lambda b: (b, 0, 0)),
            pl.BlockSpec((1, d), lambda b: (0, 0)),
            pl.BlockSpec((d, kvw), lambda b: (0, 0)),
        ],
        out_specs=pl.BlockSpec((None, n_mem, kvw), lambda b: (b, 0, 0)),
        out_shape=jax.ShapeDtypeStruct((batch, n_mem, kvw), F32),
        compiler_params=_params("arbitrary"),
    )(mem, g_mem, w_kv)


def _kv_bwd(mem, dkv, g_mem, w_kv):
    batch, n_mem, d = mem.shape
    kvw = w_kv.shape[1]

    def body(mem_ref, dkv_ref, g_ref, w_ref, gw_ref, gg_ref):
        @pl.when(pl.program_id(0) == 0)
        def _():
            gw_ref[...] = jnp.zeros_like(gw_ref)
            gg_ref[...] = jnp.zeros_like(gg_ref)

        m = mem_ref[...]
        mn = m * _rms(m)
        dkv_b = dkv_ref[...].astype(BF16)
        gw_ref[...] += _mm_tn((mn * g_ref[...]).astype(BF16), dkv_b)
        gg_ref[...] += jnp.sum(_mm_nt(dkv_b, w_ref[...]) * mn, axis=0, keepdims=True)

    return pl.pallas_call(
        body,
        name="kv_bwd",
        grid=(batch,),
        in_specs=[
            pl.BlockSpec((None, n_mem, d), lambda b: (b, 0, 0)),
            pl.BlockSpec((None, n_mem, kvw), lambda b: (b, 0, 0)),
            pl.BlockSpec((1, d), lambda b: (0, 0)),
            pl.BlockSpec((d, kvw), lambda b: (0, 0)),
        ],
        out_specs=[pl.BlockSpec((d, kvw), lambda b: (0, 0)), pl.BlockSpec((1, d), lambda b: (0, 0))],
        out_shape=[jax.ShapeDtypeStruct((d, kvw), F32), jax.ShapeDtypeStruct((1, d), F32)],
        compiler_params=_params("arbitrary"),
    )(mem, dkv, g_mem, w_kv)


def _tail(x2, target2, gate, y_pool, y_ssm, q, kv, w_out, g_post):
    tokens, d = x2.shape
    pool_w, ssm_w, att_w, mix = y_pool.shape[1], y_ssm.shape[1], q.shape[1], gate.shape[1]
    batch, n_mem, kvw = kv.shape
    hd = att_w // MEM_HEADS
    cl = _token_tile(tokens // batch, 256)
    n_tiles = tokens // cl
    per_seq = tokens // batch // cl
    qk_scale = hd**-0.5

    def body(x_ref, tg_ref, gate_ref, yp_ref, ys_ref, q_ref, kv_ref, w_ref, g_ref,
             dres_ref, dgate_ref, dyp_ref, dys_ref, dq_ref, dkv_ref, gw_hbm, gg_ref, loss_ref, acc, sem):
        i = pl.program_id(0)

        @pl.when(i == 0)
        def _():
            acc[...] = jnp.zeros_like(acc)
            gg_ref[...] = jnp.zeros_like(gg_ref)
            loss_ref[...] = jnp.zeros_like(loss_ref)

        @pl.when(i % per_seq == 0)
        def _():
            dkv_ref[...] = jnp.zeros_like(dkv_ref)

        q = q_ref[...]
        k = kv_ref[:, 0:att_w].astype(BF16)
        v = kv_ref[:, att_w : 2 * att_w].astype(BF16)
        lane = lax.broadcasted_iota(jnp.int32, (1, att_w), 1)
        heads = [(lane >= h * hd) & (lane < (h + 1) * hd) for h in range(MEM_HEADS)]
        probs, q_heads = [], []
        att = jnp.zeros((cl, att_w), F32)
        for mask in heads:
            qh = jnp.where(mask, q, 0.0).astype(BF16)
            sc = _mm_nt(qh, k) * qk_scale
            e = jnp.exp(sc - jnp.max(sc, axis=-1, keepdims=True))
            p = e / jnp.sum(e, axis=-1, keepdims=True)
            att = att + jnp.where(mask, _mm(p.astype(BF16), v), 0.0)
            probs.append(p)
            q_heads.append(qh)

        ycat = jnp.concatenate([yp_ref[...], ys_ref[...], att], axis=1)
        gate = gate_ref[...]
        sig = jax.nn.sigmoid(gate)
        silu = gate * sig
        yg = (ycat * silu).astype(BF16)
        out = _mm(yg, w_ref[...])
        r = _rms(out)
        on = out * r
        g = g_ref[...]
        x = x_ref[...]
        err = x + on * g - tg_ref[...]
        loss_ref[...] += 0.5 * jnp.sum(jnp.mean(err * err, axis=-1, keepdims=True), axis=0, keepdims=True)
        dres = err * (1.0 / d)
        dres_ref[...] = dres
        gg_ref[...] += jnp.sum(dres * on, axis=0, keepdims=True)
        don = dres * g
        dout = (r * (don - on * jnp.mean(don * on, axis=-1, keepdims=True))).astype(BF16)
        acc[...] += _mm_tn(yg, dout)
        dyg = _mm_nt(dout, w_ref[...])
        dgate_ref[...] = dyg * ycat * (sig * (1.0 + gate * (1.0 - sig)))
        dycat = dyg * silu
        dyp_ref[...] = dycat[:, 0:pool_w]
        dys_ref[...] = dycat[:, pool_w : pool_w + ssm_w]
        datt = dycat[:, pool_w + ssm_w : mix]

        dq = jnp.zeros((cl, att_w), F32)
        dk = jnp.zeros((n_mem, att_w), F32)
        dv = jnp.zeros((n_mem, att_w), F32)
        for mask, p, qh in zip(heads, probs, q_heads):
            doh = jnp.where(mask, datt, 0.0).astype(BF16)
            dp = _mm_nt(doh, v)
            ds = (p * (dp - jnp.sum(p * dp, axis=-1, keepdims=True)) * qk_scale).astype(BF16)
            dq = dq + jnp.where(mask, _mm(ds, k), 0.0)
            dk = dk + _mm_tn(ds, qh)
            dv = dv + _mm_tn(p.astype(BF16), doh)
        dq_ref[...] = dq
        dkv_ref[:, 0:att_w] += dk
        dkv_ref[:, att_w : 2 * att_w] += dv

        @pl.when(i == n_tiles - 1)
        def _():
            cp = pltpu.make_async_copy(acc, gw_hbm, sem)
            cp.start()
            cp.wait()

    tok = lambda w: pl.BlockSpec((cl, w), lambda i: (i, 0))
    chunked = tok(ssm_w)
    per_batch = pl.BlockSpec((None, n_mem, kvw), lambda i: (i // per_seq, 0, 0))
    return pl.pallas_call(
        body,
        name="tail",
        grid=(n_tiles,),
        in_specs=[
            tok(d), tok(d), tok(mix), tok(pool_w), chunked, tok(att_w), per_batch,
            pl.BlockSpec((mix, d), lambda i: (0, 0)),
            pl.BlockSpec((1, d), lambda i: (0, 0)),
        ],
        out_specs=[
            tok(d), tok(mix), tok(pool_w), chunked, tok(att_w), per_batch,
            pl.BlockSpec(memory_space=pl.ANY),
            pl.BlockSpec((1, d), lambda i: (0, 0)),
            pl.BlockSpec((1, 1), lambda i: (0, 0)),
        ],
        out_shape=[
            jax.ShapeDtypeStruct((tokens, d), F32),
            jax.ShapeDtypeStruct((tokens, mix), F32),
            jax.ShapeDtypeStruct((tokens, pool_w), F32),
            jax.ShapeDtypeStruct((tokens, ssm_w), F32),
            jax.ShapeDtypeStruct((tokens, att_w), F32),
            jax.ShapeDtypeStruct(kv.shape, F32),
            jax.ShapeDtypeStruct((mix, d), F32),
            jax.ShapeDtypeStruct((1, d), F32),
            jax.ShapeDtypeStruct((1, 1), F32),
        ],
        scratch_shapes=[pltpu.VMEM((mix, d), F32), pltpu.SemaphoreType.DMA],
        compiler_params=_params("arbitrary"),
    )(x2, target2, gate, y_pool, y_ssm, q, kv, w_out, g_post)


def _pack(arrays):
    flat = jnp.concatenate([a.reshape(-1) for a in arrays])
    rows = -(-flat.size // (SUBLANES * LANES)) * SUBLANES
    return jnp.pad(flat, (0, rows * LANES - flat.size)).reshape(rows, LANES)


def _unpack(packed, like):
    flat, out, at = packed.reshape(-1), [], 0
    for a in like:
        out.append(flat[at : at + a.size].reshape(a.shape))
        at += a.size
    return out


def kernel(x, mem, g_pre, w_in, w_pool, pool_scale, a_re, a_im, log_dt, b_re, b_im, c_re, c_im, d_skip, w_glu, g_mem, w_kv, w_out, g_post, loss_target, m_g_pre, m_w_in, m_w_pool, m_pool_scale, m_a_re, m_a_im, m_log_dt, m_b_re, m_b_im, m_c_re, m_c_im, m_d_skip, m_w_glu, m_g_mem, m_w_kv, m_w_out, m_g_post, v_g_pre, v_w_in, v_w_pool, v_pool_scale, v_a_re, v_a_im, v_log_dt, v_b_re, v_b_im, v_c_re, v_c_im, v_d_skip, v_w_glu, v_g_mem, v_w_kv, v_w_out, v_g_post):
    batch, seq, d = x.shape
    cl = seq // NCH
    pool_w, ssm_w = pool_scale.shape[1], d_skip.shape[1]
    att_w = w_kv.shape[2] // 2
    tokens = batch * seq
    x2 = x.reshape(tokens, d)
    target2 = loss_target.reshape(tokens, d)

    w_in_g, w_out_g, w_kv_g, w_glu_g = _gather_weights([w_in[0], w_out[0], w_kv[0], w_glu[0]])
    w_out_f = w_out_g.reshape(N_DEV * w_out_g.shape[1], w_out_g.shape[2])
    w_kv_f = w_kv_g.reshape(N_DEV * w_kv_g.shape[1], w_kv_g.shape[2])
    w_glu_f = w_glu_g.transpose(1, 0, 2).reshape(w_glu_g.shape[1], N_DEV * w_glu_g.shape[2])

    wp_blk = jax.scipy.linalg.block_diag(*w_pool[0]).astype(BF16)
    ssm_params = (a_re[0], a_im[0], log_dt[0], b_re[0], b_im[0], c_re[0], c_im[0])
    (lam_row, b_tab, c_tab), tables_vjp = jax.vjp(_ssm_tables, *ssm_params)
    nb = b_tab.shape[0]
    lam8 = jnp.broadcast_to(lam_row, (SUBLANES, lam_row.shape[1]))
    lcl = _lam_power(a_re[0], a_im[0], log_dt[0], float(cl), 1.0, nb)
    lcl1 = _lam_power(a_re[0], a_im[0], log_dt[0], float(cl - 1), float(cl), nb)
    b_bf, c_bf = b_tab.astype(BF16), c_tab.astype(BF16)

    u_pool, u_ssm, q, gate = _in_proj(x2, g_pre, w_in_g, pool_w, ssm_w, att_w)
    y_pool = _pool_fwd(u_pool, wp_blk, pool_scale, batch, seq)
    u_ssm = u_ssm.reshape(batch, seq, ssm_w)
    y_ssm, pre_ssm, z_ssm, init_ssm = _ssm_fwd(u_ssm, b_bf, c_bf, lam8, lcl, d_skip, w_glu_f)
    kv = _kv_fwd(mem, g_mem, w_kv_f)

    dres, dgate, dy_pool, dy_ssm, dq, dkv, gw_out, gg_post, loss_part = _tail(
        x2, target2, gate, y_pool, y_ssm.reshape(tokens, ssm_w), q, kv, w_out_f, g_post)

    gw_kv, gg_mem = _kv_bwd(mem, dkv, g_mem, w_kv_f)
    du_pool, gwp_dense, g_scale = _pool_bwd(u_pool, dy_pool, wp_blk, pool_scale, batch, seq)
    du_ssm, gb_tab, gc_tab, gw_glu, gd_skip, glam = _ssm_bwd(
        u_ssm, pre_ssm, z_ssm, dy_ssm.reshape(batch, seq, ssm_w), init_ssm, b_bf, b_bf.transpose(0, 2, 1),
        c_bf.transpose(0, 2, 1), lam8, lcl, lcl1, d_skip, w_glu_f)
    grad_x2, gw_in, gg_pre = _in_proj_bwd(
        x2, dres, du_pool, du_ssm.reshape(tokens, ssm_w), dq, dgate, g_pre, w_in_g)

    gw = pool_w // len(POOL_WINDOWS)
    gw_pool = jnp.stack([gwp_dense[i * gw : (i + 1) * gw, i * gw : (i + 1) * gw] for i in range(len(POOL_WINDOWS))])
    g_ssm = tables_vjp((glam, gb_tab, gc_tab))

    small_w = [g_pre, w_pool, pool_scale, a_re, a_im, log_dt, b_re, b_im, c_re, c_im, d_skip, g_mem, g_post]
    small_m = [m_g_pre, m_w_pool, m_pool_scale, m_a_re, m_a_im, m_log_dt, m_b_re, m_b_im, m_c_re, m_c_im, m_d_skip, m_g_mem, m_g_post]
    small_v = [v_g_pre, v_w_pool, v_pool_scale, v_a_re, v_a_im, v_log_dt, v_b_re, v_b_im, v_c_re, v_c_im, v_d_skip, v_g_mem, v_g_post]
    small_g = [gg_pre, gw_pool, g_scale, *g_ssm, gd_skip, gg_mem, gg_post]
    scalar = jnp.zeros((1,), F32)
    sg, sd, sm, sv = _reduce_small(
        _pack(small_g + [loss_part]), _pack(small_w + [scalar]), _pack(small_m + [scalar]), _pack(small_v + [scalar]))
    like = small_w + [scalar]
    sg, sd, sm, sv = _unpack(sg, like), _unpack(sd, like), _unpack(sm, like), _unpack(sv, like)
    loss = sg[-1].reshape(())

    big = _reduce_big(
        [gw_in, gw_glu.reshape(ssm_w, N_DEV, -1).transpose(1, 0, 2), gw_kv.reshape(N_DEV, -1, gw_kv.shape[1]),
         gw_out.reshape(N_DEV, -1, gw_out.shape[1])],
        [w_in[0], w_glu[0], w_kv[0], w_out[0]],
        [m_w_in[0], m_w_glu[0], m_w_kv[0], m_w_out[0]],
        [v_w_in[0], v_w_glu[0], v_w_kv[0], v_w_out[0]])
    big = {name: tuple(t[None] for t in res) for name, res in zip(["w_in", "w_glu", "w_kv", "w_out"], big)}

    order = ["g_pre", "w_in", "w_pool", "pool_scale", "a_re", "a_im", "log_dt", "b_re", "b_im", "c_re", "c_im",
             "d_skip", "w_glu", "g_mem", "w_kv", "w_out", "g_post"]
    small_names = ["g_pre", "w_pool", "pool_scale", "a_re", "a_im", "log_dt", "b_re", "b_im", "c_re", "c_im",
                   "d_skip", "g_mem", "g_post"]
    outs = [[], [], [], []]
    for name in order:
        if name in big:
            parts = big[name]
        else:
            j = small_names.index(name)
            parts = (sg[j], sd[j], sm[j], sv[j])
        for kind in range(4):
            outs[kind].append(parts[kind])
    return (loss, grad_x2.reshape(batch, seq, d), *outs[0], *outs[1], *outs[2], *outs[3])
```

```python
import functools
import math

import jax
import jax.numpy as jnp
from jax import lax
from jax.experimental import pallas as pl
from jax.experimental.pallas import tpu as pltpu

F32 = jnp.float32
BF16 = jnp.bfloat16
MESH = pl.DeviceIdType.MESH

N_DEV = 8
NCH = 8
SUBLANES = 8
LANES = 128
VMEM_LIMIT = 56 * 1024 * 1024

EPS = 1e-6
POOL_WINDOWS = (2, 4, 8, 16)
MEM_HEADS = 4
SSM_GROUP = 16
SSM_N = 64
ADAM_LR, ADAM_B1, ADAM_B2, ADAM_EPS, ADAM_WD, ADAM_STEP = 0.001, 0.9, 0.999, 1e-08, 0.01, 10


def _mm(a, b):
    return jnp.dot(a, b, preferred_element_type=F32)


def _mm_nt(a, b):
    return lax.dot_general(a, b, (((1,), (1,)), ((), ())), preferred_element_type=F32)


def _mm_tn(a, b):
    return lax.dot_general(a, b, (((0,), (0,)), ((), ())), preferred_element_type=F32)


def _params(*sem):
    return pltpu.CompilerParams(dimension_semantics=sem or None, vmem_limit_bytes=VMEM_LIMIT)


def _adamw(w, g, m, v):
    m = ADAM_B1 * m + (1.0 - ADAM_B1) * g
    v = ADAM_B2 * v + (1.0 - ADAM_B2) * (g * g)
    m_hat = m / (1.0 - ADAM_B1**ADAM_STEP)
    v_hat = v / (1.0 - ADAM_B2**ADAM_STEP)
    delta = -ADAM_LR * (m_hat / (jnp.sqrt(v_hat) + ADAM_EPS) + ADAM_WD * w)
    return delta, m, v


def _gelu(x):
    k = math.sqrt(2.0 / math.pi)
    return 0.5 * x * (1.0 + jnp.tanh(k * (x + 0.044715 * x * x * x)))


def _gelu_grad(x):
    k = math.sqrt(2.0 / math.pi)
    th = jnp.tanh(k * (x + 0.044715 * x * x * x))
    return 0.5 * (1.0 + th) + 0.5 * x * (1.0 - th * th) * (k * (1.0 + 3.0 * 0.044715 * x * x))


def _place():
    return lax.axis_index("x"), lax.axis_index("y"), lax.axis_index("c")


def _gather_body(n, out_dtype, epilogue, n_extra_in, n_extra_out):
    def body(*refs):
        ins = refs[:n]
        extra_in = refs[n : n + n_extra_in]
        outs = refs[n + n_extra_in : 2 * n + n_extra_in]
        extra_out = refs[2 * n + n_extra_in : 2 * n + n_extra_in + n_extra_out]
        send_sems, recv_sems = refs[2 * n + n_extra_in + n_extra_out :]
        x, y, c = _place()
        me, sibling = (x, y, c), (x, y, 1 - c)
        chips = [(1 - x, y), (x, 1 - y), (1 - x, 1 - y)]

        def slot(px, py, pc):
            return 4 * px + 2 * py + pc

        def copy(a, k, block, to):
            ref = outs[a].at[slot(*block)]
            return pltpu.make_async_remote_copy(
                src_ref=ref, dst_ref=ref, send_sem=send_sems.at[a, k], recv_sem=recv_sems.at[a, k],
                device_id=to, device_id_type=MESH)

        for a in range(n):
            outs[a][slot(*me)] = ins[a][...].astype(out_dtype)
        first = []
        for a in range(n):
            first.append(copy(a, 0, me, sibling))
            first += [copy(a, 1 + j, me, (*chip, c)) for j, chip in enumerate(chips)]
        for cp in first:
            cp.start()
        passed = []
        for j, chip in enumerate(chips):
            for a in range(n):
                copy(a, 1 + j, (*chip, c), me).wait_recv()
                cp = copy(a, 4 + j, (*chip, c), sibling)
                cp.start()
                passed.append(cp)
        for a in range(n):
            copy(a, 0, sibling, me).wait_recv()
            for j, chip in enumerate(chips):
                copy(a, 4 + j, (*chip, 1 - c), me).wait_recv()
        for cp in first + passed:
            cp.wait_send()
        if epilogue is not None:
            epilogue(outs, extra_in, extra_out)

    return body


def _gather_weights(blocks):
    n = len(blocks)
    vmem = pl.BlockSpec(memory_space=pltpu.VMEM)
    return pl.pallas_call(
        _gather_body(n, BF16, None, 0, 0),
        name="gather_weights",
        out_shape=[jax.ShapeDtypeStruct((N_DEV, *b.shape), BF16) for b in blocks],
        in_specs=[vmem] * n,
        out_specs=[vmem] * n,
        scratch_shapes=[pltpu.SemaphoreType.DMA((n, 7)), pltpu.SemaphoreType.DMA((n, 7))],
        compiler_params=_params(),
    )(*blocks)


def _reduce_small(part):
    rows = part.shape[0]

    def epilogue(outs, extra_in, extra_out):
        gathered = outs[0]
        (g_ref,) = extra_out
        chunk = SUBLANES

        def step(i, _):
            r = pl.ds(pl.multiple_of(i * chunk, chunk), chunk)
            g = gathered[0, r, :]
            for d in range(1, N_DEV):
                g = g + gathered[d, r, :]
            g_ref[r, :] = g
            return 0

        lax.fori_loop(0, rows // chunk, step, 0)

    vmem = pl.BlockSpec(memory_space=pltpu.VMEM)
    res = pl.pallas_call(
        _gather_body(1, F32, epilogue, 0, 1),
        name="reduce_small",
        out_shape=[jax.ShapeDtypeStruct((N_DEV, rows, LANES), F32), jax.ShapeDtypeStruct((rows, LANES), F32)],
        in_specs=[vmem],
        out_specs=[vmem] * 2,
        scratch_shapes=[pltpu.SemaphoreType.DMA((1, 7)), pltpu.SemaphoreType.DMA((1, 7))],
        compiler_params=_params(),
    )(part)
    return res[1]


def _adamw_small(gs, ws, ms, vs):
    n = len(gs)

    def body(*refs):
        g, w, m, v = refs[:n], refs[n : 2 * n], refs[2 * n : 3 * n], refs[3 * n : 4 * n]
        outs = refs[4 * n :]
        for a in range(n):
            delta, nm, nv = _adamw(w[a][...], g[a][...], m[a][...], v[a][...])
            outs[3 * a][...] = delta
            outs[3 * a + 1][...] = nm
            outs[3 * a + 2][...] = nv

    vmem = pl.BlockSpec(memory_space=pltpu.VMEM)
    out_shape = []
    for wa in ws:
        out_shape += [jax.ShapeDtypeStruct(wa.shape, F32)] * 3
    res = pl.pallas_call(
        body,
        name="adamw_small",
        out_shape=out_shape,
        in_specs=[vmem] * (4 * n),
        out_specs=[vmem] * (3 * n),
        compiler_params=_params(),
    )(*gs, *ws, *ms, *vs)
    return [tuple(res[3 * a : 3 * a + 3]) for a in range(n)]


def _reduce_big(parts, ws, ms, vs):
    n = len(parts)
    parts4 = [p.reshape(4, 2, *p.shape[1:]) for p in parts]
    blks = [p.shape[1:] for p in parts]

    def body(*refs):
        part = refs[:n]
        w_in, m_in, v_in = refs[n : 2 * n], refs[2 * n : 3 * n], refs[3 * n : 4 * n]
        outs = refs[4 * n : 8 * n]
        own, r1, r2 = refs[8 * n : 9 * n], refs[9 * n : 10 * n], refs[10 * n : 11 * n]
        s1_send, s1_recv, s2_send, s2_recv, loc = refs[11 * n :]
        x, y, c = _place()
        sibling = (x, y, 1 - c)
        chips = [(1 - x, y), (x, 1 - y), (1 - x, 1 - y)]

        def rowwise(rows, fn):
            chunk = math.gcd(rows, 128)

            def step(i, _):
                fn(pl.ds(pl.multiple_of(i * chunk, chunk), chunk))
                return 0

            lax.fori_loop(0, rows // chunk, step, 0)

        stage1, local = [], []
        for a in range(n):
            cp = pltpu.make_async_remote_copy(
                src_ref=part[a].at[:, 1 - c], dst_ref=r1[a], send_sem=s1_send.at[a], recv_sem=s1_recv.at[a],
                device_id=sibling, device_id_type=MESH)
            cp.start()
            stage1.append(cp)
            lc = pltpu.make_async_copy(part[a].at[:, c], own[a], loc.at[a])
            lc.start()
            local.append(lc)
        stage2 = []
        for a in range(n):
            local[a].wait()
            stage1[a].wait_recv()
            for chip in range(4):

                def add(r, a=a, chip=chip):
                    own[a][chip, r, :] = own[a][chip, r, :] + r1[a][chip, r, :]

                rowwise(blks[a][0], add)
            for k, chip in enumerate(chips):
                cp = pltpu.make_async_remote_copy(
                    src_ref=own[a].at[2 * chip[0] + chip[1]], dst_ref=r2[a].at[k],
                    send_sem=s2_send.at[a, k], recv_sem=s2_recv.at[a, k],
                    device_id=(*chip, c), device_id_type=MESH)
                cp.start()
                stage2.append(cp)
        for a in range(n):
            for k, chip in enumerate(chips):
                stage2[3 * a + k].wait_recv()
            g_ref, d_ref, nm_ref, nv_ref = outs[4 * a : 4 * a + 4]

            def update(r, a=a, g_ref=g_ref, d_ref=d_ref, nm_ref=nm_ref, nv_ref=nv_ref):
                g = own[a][2 * x + y, r, :] + r2[a][0, r, :] + r2[a][1, r, :] + r2[a][2, r, :]
                delta, nm, nv = _adamw(w_in[a][r, :], g, m_in[a][r, :], v_in[a][r, :])
                g_ref[r, :] = g
                d_ref[r, :] = delta
                nm_ref[r, :] = nm
                nv_ref[r, :] = nv

            rowwise(blks[a][0], update)
        for cp in stage1 + stage2:
            cp.wait_send()

    vmem = pl.BlockSpec(memory_space=pltpu.VMEM)
    hbm = pl.BlockSpec(memory_space=pl.ANY)
    out_shape = []
    for b in blks:
        out_shape += [jax.ShapeDtypeStruct(b, F32)] * 4
    scratch = (
        [pltpu.VMEM((4, *b), F32) for b in blks]
        + [pltpu.VMEM((4, *b), F32) for b in blks]
        + [pltpu.VMEM((3, *b), F32) for b in blks]
        + [pltpu.SemaphoreType.DMA((n,)), pltpu.SemaphoreType.DMA((n,)), pltpu.SemaphoreType.DMA((n, 3)),
           pltpu.SemaphoreType.DMA((n, 3)), pltpu.SemaphoreType.DMA((n,))]
    )
    res = pl.pallas_call(
        body,
        name="reduce_big",
        out_shape=out_shape,
        in_specs=[hbm] * n + [vmem] * (3 * n),
        out_specs=[vmem] * (4 * n),
        scratch_shapes=scratch,
        compiler_params=_params(),
    )(*parts4, *ws, *ms, *vs)
    return [tuple(res[4 * a : 4 * a + 4]) for a in range(n)]


def _rms(x):
    return lax.rsqrt(jnp.mean(x * x, axis=-1, keepdims=True) + EPS)


def _token_tile(tokens, want):
    tile = min(want, tokens // 2)
    assert tokens % tile == 0 and tile % 16 == 0
    return tile


def _in_proj(x2, g_pre, w_in_g, pool_w, ssm_w, att_w):
    tokens, d = x2.shape
    nb = w_in_g.shape[2]
    mix = pool_w + ssm_w + att_w
    half = N_DEV // 2
    cl = _token_tile(tokens, 512)
    assert half * nb == mix and nb == 256 and pool_w == 384 and ssm_w == 384 and att_w == 256

    def body(x_ref, g_ref, w_ref, up_ref, us_ref, q_ref, gate_ref):
        x = x_ref[...]
        h = (x * _rms(x) * g_ref[...]).astype(BF16)
        p = [_mm(h, w_ref[j]) for j in range(half)]
        up_ref[:, 0:256] = p[0]
        up_ref[:, 256:384] = p[1][:, 0:128]
        us_ref[:, 0:128] = p[1][:, 128:256]
        us_ref[:, 128:384] = p[2]
        q_ref[...] = p[3]
        for j in range(half):
            gate_ref[:, j * nb : (j + 1) * nb] = _mm(h, w_ref[half + j])

    return pl.pallas_call(
        body,
        name="in_proj",
        grid=(tokens // cl,),
        in_specs=[
            pl.BlockSpec((cl, d), lambda i: (i, 0)),
            pl.BlockSpec((1, d), lambda i: (0, 0)),
            pl.BlockSpec((N_DEV, d, nb), lambda i: (0, 0, 0)),
        ],
        out_specs=[
            pl.BlockSpec((cl, pool_w), lambda i: (i, 0)),
            pl.BlockSpec((cl, ssm_w), lambda i: (i, 0)),
            pl.BlockSpec((cl, att_w), lambda i: (i, 0)),
            pl.BlockSpec((cl, mix), lambda i: (i, 0)),
        ],
        out_shape=[
            jax.ShapeDtypeStruct((tokens, pool_w), F32),
            jax.ShapeDtypeStruct((tokens, ssm_w), F32),
            jax.ShapeDtypeStruct((tokens, att_w), F32),
            jax.ShapeDtypeStruct((tokens, mix), F32),
        ],
        compiler_params=_params("arbitrary"),
    )(x2, g_pre, w_in_g)


def _in_proj_bwd(x2, dres, du_pool, du_ssm, dq, dgate, g_pre, w_in_g):
    tokens, d = x2.shape
    nb = w_in_g.shape[2]
    pool_w, ssm_w, att_w, mix = du_pool.shape[1], du_ssm.shape[1], dq.shape[1], dgate.shape[1]
    half = N_DEV // 2
    cl = _token_tile(tokens, 512)
    n_tiles = tokens // cl

    def body(x_ref, dres_ref, dup_ref, dus_ref, dq_ref, dgate_ref, g_ref, w_ref, gx_ref, gw_hbm, gg_ref, acc, sem):
        i = pl.program_id(0)

        @pl.when(i == 0)
        def _():
            acc[...] = jnp.zeros_like(acc)
            gg_ref[...] = jnp.zeros_like(gg_ref)

        x = x_ref[...]
        r = _rms(x)
        xn = x * r
        g = g_ref[...]
        h = (xn * g).astype(BF16)
        dval = jnp.concatenate([dup_ref[...], dus_ref[...], dq_ref[...]], axis=1)
        dh = jnp.zeros((cl, d), F32)
        for j in range(N_DEV):
            src = dval if j < half else dgate_ref[...]
            jj = j % half
            dp = src[:, jj * nb : (jj + 1) * nb].astype(BF16)
            dh = dh + _mm_nt(dp, w_ref[j])
            acc[j] += _mm_tn(h, dp)
        gg_ref[...] += jnp.sum(dh * xn, axis=0, keepdims=True)
        dxn = dh * g
        gx_ref[...] = dres_ref[...] + r * (dxn - xn * jnp.mean(dxn * xn, axis=-1, keepdims=True))

        @pl.when(i == n_tiles - 1)
        def _():
            cp = pltpu.make_async_copy(acc, gw_hbm, sem)
            cp.start()
            cp.wait()

    return pl.pallas_call(
        body,
        name="in_proj_bwd",
        grid=(n_tiles,),
        in_specs=[
            pl.BlockSpec((cl, d), lambda i: (i, 0)),
            pl.BlockSpec((cl, d), lambda i: (i, 0)),
            pl.BlockSpec((cl, pool_w), lambda i: (i, 0)),
            pl.BlockSpec((cl, ssm_w), lambda i: (i, 0)),
            pl.BlockSpec((cl, att_w), lambda i: (i, 0)),
            pl.BlockSpec((cl, mix), lambda i: (i, 0)),
            pl.BlockSpec((1, d), lambda i: (0, 0)),
            pl.BlockSpec((N_DEV, d, nb), lambda i: (0, 0, 0)),
        ],
        out_specs=[
            pl.BlockSpec((cl, d), lambda i: (i, 0)),
            pl.BlockSpec(memory_space=pl.ANY),
            pl.BlockSpec((1, d), lambda i: (0, 0)),
        ],
        out_shape=[
            jax.ShapeDtypeStruct((tokens, d), F32),
            jax.ShapeDtypeStruct((N_DEV, d, nb), F32),
            jax.ShapeDtypeStruct((1, d), F32),
        ],
        scratch_shapes=[pltpu.VMEM((N_DEV, d, nb), F32), pltpu.SemaphoreType.DMA],
        compiler_params=_params("arbitrary"),
    )(x2, dres, du_pool, du_ssm, dq, dgate, g_pre, w_in_g)


def _pool_geometry(seq, width):
    gw = width // len(POOL_WINDOWS)
    col = lax.broadcasted_iota(jnp.int32, (1, width), 1)
    win = jnp.full((1, width), float(POOL_WINDOWS[-1]), F32)
    for gi in range(len(POOL_WINDOWS) - 2, -1, -1):
        win = jnp.where(col < (gi + 1) * gw, float(POOL_WINDOWS[gi]), win)
    row = lax.broadcasted_iota(jnp.int32, (seq, width), 0)
    cnt = jnp.minimum((row + 1).astype(F32), win)
    return win, row, cnt


def _window_sums(a, row, win, seq, back):
    sums = []
    s, k = a, 1
    while k < POOL_WINDOWS[-1]:
        if back:
            shifted = jnp.where(row < seq - k, pltpu.roll(s, seq - k, 0), 0.0)
        else:
            shifted = jnp.where(row >= k, pltpu.roll(s, k, 0), 0.0)
        s = s + shifted
        k *= 2
        sums.append((k, s))
    out = sums[-1][1]
    for k, s in reversed(sums[:-1]):
        out = jnp.where(win <= float(k), s, out)
    return out


def _pool_fwd(u2, wp_blk, scale, batch, seq):
    width = u2.shape[1]

    def body(u_ref, w_ref, s_ref, y_ref):
        u = u_ref[...]
        win, row, cnt = _pool_geometry(seq, width)
        diff = _window_sums(u, row, win, seq, False) / cnt - u
        y_ref[...] = _mm(diff.astype(BF16), w_ref[...]) * s_ref[...]

    return pl.pallas_call(
        body,
        name="pool_fwd",
        grid=(batch,),
        in_specs=[
            pl.BlockSpec((seq, width), lambda b: (b, 0)),
            pl.BlockSpec((width, width), lambda b: (0, 0)),
            pl.BlockSpec((1, width), lambda b: (0, 0)),
        ],
        out_specs=pl.BlockSpec((seq, width), lambda b: (b, 0)),
        out_shape=jax.ShapeDtypeStruct(u2.shape, F32),
        compiler_params=_params("arbitrary"),
    )(u2, wp_blk, scale)


def _pool_bwd(u2, dy2, wp_blk, scale, batch, seq):
    width = u2.shape[1]

    def body(u_ref, dy_ref, w_ref, s_ref, du_ref, gw_ref, gs_ref):
        @pl.when(pl.program_id(0) == 0)
        def _():
            gw_ref[...] = jnp.zeros_like(gw_ref)
            gs_ref[...] = jnp.zeros_like(gs_ref)

        u = u_ref[...]
        dy = dy_ref[...]
        win, row, cnt = _pool_geometry(seq, width)
        diff = (_window_sums(u, row, win, seq, False) / cnt - u).astype(BF16)
        gs_ref[...] += jnp.sum(dy * _mm(diff, w_ref[...]), axis=0, keepdims=True)
        dys = (dy * s_ref[...]).astype(BF16)
        gw_ref[...] += _mm_tn(diff, dys)
        dd = _mm_nt(dys, w_ref[...])
        du_ref[...] = _window_sums(dd / cnt, row, win, seq, True) - dd

    return pl.pallas_call(
        body,
        name="pool_bwd",
        grid=(batch,),
        in_specs=[
            pl.BlockSpec((seq, width), lambda b: (b, 0)),
            pl.BlockSpec((seq, width), lambda b: (b, 0)),
            pl.BlockSpec((width, width), lambda b: (0, 0)),
            pl.BlockSpec((1, width), lambda b: (0, 0)),
        ],
        out_specs=[
            pl.BlockSpec((seq, width), lambda b: (b, 0)),
            pl.BlockSpec((width, width), lambda b: (0, 0)),
            pl.BlockSpec((1, width), lambda b: (0, 0)),
        ],
        out_shape=[
            jax.ShapeDtypeStruct(u2.shape, F32),
            jax.ShapeDtypeStruct((width, width), F32),
            jax.ShapeDtypeStruct((1, width), F32),
        ],
        compiler_params=_params("arbitrary"),
    )(u2, dy2, wp_blk, scale)


def _state_row(z, n_blocks):
    re = jnp.real(z).reshape(n_blocks, -1)
    im = jnp.imag(z).reshape(n_blocks, -1)
    return jnp.concatenate([re, im], axis=1).reshape(1, -1)


def _ssm_tables(a_re, a_im, log_dt, b_re, b_im, c_re, c_im):
    groups, n_state = a_re.shape
    ch = b_re.shape[2]
    nb = groups * ch // LANES
    gl = groups // nb
    lam = lax.complex(a_re, a_im)
    lam_bar = jnp.exp(lam * jnp.exp(log_dt)[:, None])
    b_bar = ((lam_bar - 1.0) / lam)[..., None] * lax.complex(b_re, b_im)
    eye = jnp.eye(gl, dtype=F32)

    def rows_to_state(t):
        return jnp.einsum("sgnc,gh->sgchn", t.reshape(nb, gl, n_state, ch), eye).reshape(nb, gl * ch, gl * n_state)

    def state_to_rows(t):
        return jnp.einsum("sgcn,gh->shngc", t.reshape(nb, gl, ch, n_state), eye).reshape(nb, gl * n_state, gl * ch)

    b_tab = jnp.concatenate([rows_to_state(jnp.real(b_bar)), rows_to_state(jnp.imag(b_bar))], axis=2)
    c_tab = jnp.concatenate([state_to_rows(c_re), -state_to_rows(c_im)], axis=1)
    return _state_row(lam_bar, nb), b_tab, c_tab


def _lam_power(a_re, a_im, log_dt, power, scale, n_blocks):
    return _state_row(scale * jnp.exp(lax.complex(a_re, a_im) * jnp.exp(log_dt)[:, None] * power), n_blocks)


def _state_blocks(s2, n_blocks, width):
    half = s2 // n_blocks // 2
    assert half % width == 0
    return [(b * 2 * half + o, b * 2 * half + half + o) for b in range(n_blocks) for o in range(0, half, width)]


def _scan(src_ref, dst_ref, st_ref, lam8_ref, n_groups, s, n_blocks, reverse, store):
    lb = 512
    for re0, im0 in _state_blocks(2 * s, n_blocks, lb):
        cr, ci = pl.ds(re0, lb), pl.ds(im0, lb)
        lr = lam8_ref[:, cr]
        li = -lam8_ref[:, ci] if reverse else lam8_ref[:, ci]

        def step(i, carry, cr=cr, ci=ci, lr=lr, li=li):
            hr, hi = carry
            grp = n_groups - 1 - i if reverse else i
            rows = pl.ds(pl.multiple_of(grp * SUBLANES, SUBLANES), SUBLANES)
            nr = lr * hr - li * hi + src_ref[rows, cr]
            ni = lr * hi + li * hr + src_ref[rows, ci]
            if store:
                dst_ref[rows, cr] = nr
                dst_ref[rows, ci] = ni
            return nr, ni

        hr, hi = lax.fori_loop(0, n_groups, step, (st_ref[:, cr], st_ref[:, ci]), unroll=2)
        st_ref[:, cr] = hr
        st_ref[:, ci] = hi


def _pack_state(re, im):
    hi = lax.bitcast_convert_type(re.astype(BF16).astype(F32), jnp.uint32)
    lo = lax.bitcast_convert_type(im.astype(BF16).astype(F32), jnp.uint32)
    return hi | (lo >> 16)


def _unpack_state(word):
    re = lax.bitcast_convert_type(word & jnp.uint32(0xFFFF0000), F32)
    im = lax.bitcast_convert_type(word << 16, F32)
    return re, im


def _scan_adjoint(dh_ref, hprev_ref, group0, stg_ref, acc_ref, lam8_ref, n_groups, s, n_blocks):
    lb = 512
    half = s // n_blocks
    for re0, im0 in _state_blocks(2 * s, n_blocks, lb):
        cr, ci = pl.ds(re0, lb), pl.ds(im0, lb)
        ch = pl.ds(re0 // (2 * half) * half + re0 % (2 * half), lb)
        lr, li = lam8_ref[:, cr], -lam8_ref[:, ci]

        def step(i, carry, cr=cr, ci=ci, ch=ch, lr=lr, li=li):
            gr, gi, ar, ai = carry
            grp = n_groups - 1 - i
            rows = pl.ds(pl.multiple_of(grp * SUBLANES, SUBLANES), SUBLANES)
            ngr = lr * gr - li * gi + dh_ref[rows, cr]
            ngi = lr * gi + li * gr + dh_ref[rows, ci]
            hr, hi = _unpack_state(hprev_ref[pl.ds(pl.multiple_of((group0 + grp) * SUBLANES, SUBLANES), SUBLANES), ch])
            ar = ar + hr * ngr + hi * ngi
            ai = ai + hr * ngi - hi * ngr
            dh_ref[rows, cr] = ngr
            dh_ref[rows, ci] = ngi
            return ngr, ngi, ar, ai

        init = (stg_ref[:, cr], stg_ref[:, ci], acc_ref[:, cr], acc_ref[:, ci])
        gr, gi, ar, ai = lax.fori_loop(0, n_groups, step, init)
        stg_ref[:, cr] = gr
        stg_ref[:, ci] = gi
        acc_ref[:, cr] = ar
        acc_ref[:, ci] = ai


def _chunk_starts(st_ref, init_ref, lcl_ref, s, n_blocks):
    w = s // n_blocks
    init_ref[0:1, :] = jnp.zeros((1, 2 * s), F32)
    for re0, im0 in _state_blocks(2 * s, n_blocks, w):
        re, im = pl.ds(re0, w), pl.ds(im0, w)
        ar, ai = lcl_ref[:, re], lcl_ref[:, im]
        cr = jnp.zeros((1, w), F32)
        ci = jnp.zeros((1, w), F32)
        for k in range(1, NCH):
            cr, ci = (ar * cr - ai * ci + st_ref[k - 1 : k, re], ar * ci + ai * cr + st_ref[k - 1 : k, im])
            init_ref[k : k + 1, re] = cr
            init_ref[k : k + 1, im] = ci


def _chunk_starts_adjoint(stg_ref, initg_ref, lcl_ref, s, n_blocks):
    w = s // n_blocks
    initg_ref[NCH - 1 : NCH, :] = jnp.zeros((1, 2 * s), F32)
    for re0, im0 in _state_blocks(2 * s, n_blocks, w):
        re, im = pl.ds(re0, w), pl.ds(im0, w)
        ar, ai = lcl_ref[:, re], -lcl_ref[:, im]
        gr = jnp.zeros((1, w), F32)
        gi = jnp.zeros((1, w), F32)
        for k in range(NCH - 2, -1, -1):
            gr, gi = (stg_ref[k + 1 : k + 2, re] + ar * gr - ai * gi, stg_ref[k + 1 : k + 2, im] + ar * gi + ai * gr)
            initg_ref[k : k + 1, re] = gr
            initg_ref[k : k + 1, im] = gi


def _ssm_rows(seq):
    rows = min(512, seq // 2)
    assert seq % rows == 0 and rows % SUBLANES == 0
    return rows


def _chunk_copies(hbm_ref, b, cm_ref, sems, to_cm):
    cl = cm_ref.shape[0]
    copies = []
    for k in range(NCH):
        nat, cm = hbm_ref.at[b, pl.ds(k * cl, cl), :], cm_ref.at[:, k, :]
        src, dst = (nat, cm) if to_cm else (cm, nat)
        copies.append(pltpu.make_async_copy(src, dst, sems.at[k]))
    return copies


def _blockwise(fn, n_blocks):
    return jnp.concatenate([fn(b) for b in range(n_blocks)], axis=1)


def _ssm_fwd(u, b_tab, c_tab, lam8, lcl, d_skip, w_glu):
    batch, seq, width = u.shape
    s = lam8.shape[1] // 2
    nb = b_tab.shape[0]
    sb = 2 * s // nb
    cl = seq // NCH
    rows = _ssm_rows(seq)
    n_tiles = seq // rows
    n_groups = rows // SUBLANES

    def body(u_hbm, b_ref, c_ref, lam_ref, lcl_ref, d_ref, wg_ref, y_hbm, pre_ref, z_ref, init_ref,
             u_cm, y_cm, bu, st, sems):
        b, ph, t = pl.program_id(0), pl.program_id(1), pl.program_id(2)
        tile_groups = pl.ds(pl.multiple_of(t * n_groups, n_groups), n_groups)

        @pl.when((ph == 0) & (t == 0))
        def _():
            loads = _chunk_copies(u_hbm, b, u_cm, sems, True)
            for cp in loads:
                cp.start()
            st[...] = jnp.zeros_like(st)
            for cp in loads:
                cp.wait()

        @pl.when((ph == 1) & (t == 0))
        def _():
            st[...] = init_ref[...]

        u_t = u_cm[tile_groups].reshape(rows, width)
        u_b = u_t.astype(BF16)
        for blk in range(nb):
            bu[:, blk * sb : (blk + 1) * sb] = _mm(u_b[:, blk * LANES : (blk + 1) * LANES], b_ref[blk])

        @pl.when(ph == 0)
        def _():
            _scan(bu, bu, st, lam_ref, n_groups, s, nb, False, False)

        @pl.when((ph == 0) & (t == n_tiles - 1))
        def _():
            _chunk_starts(st, init_ref, lcl_ref, s, nb)

        @pl.when(ph == 1)
        def _():
            _scan(bu, bu, st, lam_ref, n_groups, s, nb, False, True)
            hs = lambda blk: _mm(bu[:, blk * sb : (blk + 1) * sb].astype(BF16), c_ref[blk])
            pre = _blockwise(hs, nb) + d_ref[...] * u_t
            z = _mm(_gelu(pre).astype(BF16), wg_ref[...])
            pre_ref[...] = pre
            z_ref[...] = z
            y = z[:, 0:width] * jax.nn.sigmoid(z[:, width : 2 * width])
            y_cm[tile_groups] = y.reshape(n_groups, SUBLANES, width)

        @pl.when((ph == 1) & (t == n_tiles - 1))
        def _():
            stores = _chunk_copies(y_hbm, b, y_cm, sems, False)
            for cp in stores:
                cp.start()
            for cp in stores:
                cp.wait()

    out_tile = lambda b, ph, t: (b, t * ph, 0)
    full = lambda a: pl.BlockSpec(a.shape, lambda b, ph, t: (0,) * a.ndim)
    hbm = pl.BlockSpec(memory_space=pl.ANY)
    return pl.pallas_call(
        body,
        name="ssm_fwd",
        grid=(batch, 2, n_tiles),
        in_specs=[hbm, full(b_tab), full(c_tab), full(lam8), full(lcl), full(d_skip), full(w_glu)],
        out_specs=[
            hbm,
            pl.BlockSpec((None, rows, width), out_tile),
            pl.BlockSpec((None, rows, 2 * width), out_tile),
            pl.BlockSpec((None, SUBLANES, 2 * s), lambda b, ph, t: (b, 0, 0)),
        ],
        out_shape=[
            jax.ShapeDtypeStruct((batch, seq, width), F32),
            jax.ShapeDtypeStruct((batch, seq, width), F32),
            jax.ShapeDtypeStruct((batch, seq, 2 * width), F32),
            jax.ShapeDtypeStruct((batch, SUBLANES, 2 * s), F32),
        ],
        scratch_shapes=[
            pltpu.VMEM((cl, NCH, width), F32),
            pltpu.VMEM((cl, NCH, width), F32),
            pltpu.VMEM((rows, 2 * s), F32),
            pltpu.VMEM((SUBLANES, 2 * s), F32),
            pltpu.SemaphoreType.DMA((NCH,)),
        ],
        compiler_params=_params("arbitrary", "arbitrary", "arbitrary"),
    )(u, b_tab, c_tab, lam8, lcl, d_skip, w_glu)


def _ssm_bwd(u, pre_p, z_p, dy, init, b_tab, b_tab_t, c_tab_t, lam8, lcl, d_skip, w_glu):
    batch, seq, width = u.shape
    s = lam8.shape[1] // 2
    nb = b_tab.shape[0]
    sb = 2 * s // nb
    cl = seq // NCH
    rows = _ssm_rows(seq)
    n_tiles = seq // rows
    n_groups = rows // SUBLANES

    def body(u_hbm, pre_ref, z_ref, dy_hbm, init_ref, b_ref, bt_ref, ct_ref, lam_ref, lcl_ref, d_ref, wg_ref,
             du_hbm, gb_ref, gc_ref, gwg_ref, gd_ref, glam_ref,
             u_cm, dy_cm, work, hs_all, dpre_all, st, stg, initg, acc, sems):
        b, ph, t = pl.program_id(0), pl.program_id(1), pl.program_id(2)
        half = s // nb
        first = (b == 0) & (ph == 0) & (t == 0)
        last = (b == batch - 1) & (ph == 2) & (t == n_tiles - 1)
        tile = jnp.where(ph == 0, t, n_tiles - 1 - t)
        tile_rows = pl.ds(pl.multiple_of(tile * rows, rows), rows)
        tile_groups = pl.ds(pl.multiple_of(tile * n_groups, n_groups), n_groups)
        lanes = lambda blk: slice(blk * LANES, (blk + 1) * LANES)
        states = lambda blk: slice(blk * sb, (blk + 1) * sb)

        @pl.when(first)
        def _():
            acc[...] = jnp.zeros_like(acc)
            gb_ref[...] = jnp.zeros_like(gb_ref)
            gc_ref[...] = jnp.zeros_like(gc_ref)
            gwg_ref[...] = jnp.zeros_like(gwg_ref)
            gd_ref[...] = jnp.zeros_like(gd_ref)

        @pl.when((ph == 0) & (t == 0))
        def _():
            loads = _chunk_copies(u_hbm, b, u_cm, sems.at[0], True) + _chunk_copies(dy_hbm, b, dy_cm, sems.at[1], True)
            for cp in loads:
                cp.start()
            st[...] = init_ref[...]
            for blk in range(nb):
                entry = init_ref[:, states(blk)]
                hs_all[0:SUBLANES, blk * half : (blk + 1) * half] = _pack_state(entry[:, 0:half], entry[:, half : 2 * half])
            for cp in loads:
                cp.wait()

        u_t = u_cm[tile_groups].reshape(rows, width)
        u_b = u_t.astype(BF16)

        @pl.when(ph == 0)
        def _():
            for blk in range(nb):
                work[:, states(blk)] = _mm(u_b[:, lanes(blk)], b_ref[blk])
            _scan(work, work, st, lam_ref, n_groups, s, nb, False, True)
            z = z_ref[...]
            dy_t = dy_cm[tile_groups].reshape(rows, width)
            pre = pre_ref[...]
            z1, sig = z[:, 0:width], jax.nn.sigmoid(z[:, width : 2 * width])
            dz = jnp.concatenate([dy_t * sig, dy_t * z1 * sig * (1.0 - sig)], axis=1).astype(BF16)
            gwg_ref[...] += _mm_tn(_gelu(pre).astype(BF16), dz)
            dpre = _mm_nt(dz, wg_ref[...]) * _gelu_grad(pre)
            dpre_all[tile_rows, :] = dpre
            gd_ref[...] += jnp.sum(dpre * u_t, axis=0, keepdims=True)
            dpre_b = dpre.astype(BF16)
            kept = pl.ds(pl.multiple_of(tile * rows + SUBLANES, SUBLANES), rows)
            for blk in range(nb):
                hs = work[:, states(blk)]
                gc_ref[blk] += _mm_tn(hs.astype(BF16), dpre_b[:, lanes(blk)])
                hs_all[kept, blk * half : (blk + 1) * half] = _pack_state(hs[:, 0:half], hs[:, half : 2 * half])

        @pl.when(ph >= 1)
        def _():
            dpre_b = dpre_all[tile_rows, :].astype(BF16)
            for blk in range(nb):
                work[:, states(blk)] = _mm(dpre_b[:, lanes(blk)], ct_ref[blk])

        @pl.when(ph == 1)
        def _():
            @pl.when(t == 0)
            def _():
                stg[...] = jnp.zeros_like(stg)

            _scan(work, work, stg, lam_ref, n_groups, s, nb, True, False)

            @pl.when(t == n_tiles - 1)
            def _():
                _chunk_starts_adjoint(stg, initg, lcl_ref, s, nb)

        @pl.when(ph == 2)
        def _():
            @pl.when(t == 0)
            def _():
                stg[...] = initg[...]

            _scan_adjoint(work, hs_all, tile * n_groups, stg, acc, lam_ref, n_groups, s, nb)
            du = lambda blk: _mm(work[:, states(blk)].astype(BF16), bt_ref[blk])
            du_t = _blockwise(du, nb) + dpre_all[tile_rows, :] * d_ref[...]
            dy_cm[tile_groups] = du_t.reshape(n_groups, SUBLANES, width)
            for blk in range(nb):
                gb_ref[blk] += _mm_tn(u_b[:, lanes(blk)], work[:, states(blk)].astype(BF16))

            @pl.when(t == n_tiles - 1)
            def _():
                stores = _chunk_copies(du_hbm, b, dy_cm, sems.at[0], False)
                for cp in stores:
                    cp.start()
                for cp in stores:
                    cp.wait()

        @pl.when(last)
        def _():
            glam_ref[...] = jnp.sum(acc[...], axis=0, keepdims=True)

    def tile(b, ph, t):
        return (b, jnp.where(ph == 0, t, n_tiles - 1 - t), 0)

    full = lambda a: pl.BlockSpec(a.shape, lambda b, ph, t: (0,) * a.ndim)
    hbm = pl.BlockSpec(memory_space=pl.ANY)
    return pl.pallas_call(
        body,
        name="ssm_bwd",
        grid=(batch, 3, n_tiles),
        in_specs=[
            hbm,
            pl.BlockSpec((None, rows, width), tile),
            pl.BlockSpec((None, rows, 2 * width), tile),
            hbm,
            pl.BlockSpec((None, SUBLANES, 2 * s), lambda b, ph, t: (b, 0, 0)),
            full(b_tab), full(b_tab_t), full(c_tab_t), full(lam8), full(lcl), full(d_skip), full(w_glu),
        ],
        out_specs=[
            hbm,
            full(b_tab), full(b_tab_t), full(w_glu), full(d_skip),
            pl.BlockSpec((1, 2 * s), lambda b, ph, t: (0, 0)),
        ],
        out_shape=[
            jax.ShapeDtypeStruct((batch, seq, width), F32),
            jax.ShapeDtypeStruct(b_tab.shape, F32),
            jax.ShapeDtypeStruct(b_tab_t.shape, F32),
            jax.ShapeDtypeStruct(w_glu.shape, F32),
            jax.ShapeDtypeStruct(d_skip.shape, F32),
            jax.ShapeDtypeStruct((1, 2 * s), F32),
        ],
        scratch_shapes=[
            pltpu.VMEM((cl, NCH, width), F32),
            pltpu.VMEM((cl, NCH, width), F32),
            pltpu.VMEM((rows, 2 * s), F32),
            pltpu.VMEM((seq + SUBLANES, s), jnp.uint32),
            pltpu.VMEM((seq, width), F32),
        ]
        + [pltpu.VMEM((SUBLANES, 2 * s), F32)] * 4
        + [pltpu.SemaphoreType.DMA((2, NCH))],
        compiler_params=_params("arbitrary", "arbitrary", "arbitrary"),
    )(u, pre_p, z_p, dy, init, b_tab, b_tab_t, c_tab_t, lam8, lcl, d_skip, w_glu)


def _kv_fwd(mem, g_mem, w_kv):
    batch, n_mem, d = mem.shape
    kvw = w_kv.shape[1]

    def body(mem_ref, g_ref, w_ref, kv_ref):
        m = mem_ref[...]
        kv_ref[...] = _mm((m * _rms(m) * g_ref[...]).astype(BF16), w_ref[...])

    return pl.pallas_call(
        body,
        name="kv_fwd",
        grid=(batch,),
        in_specs=[
            pl.BlockSpec((None, n_mem, d), lambda b: (b, 0, 0)),
            pl.BlockSpec((1, d), lambda b: (0, 0)),
            pl.BlockSpec((d, kvw), lambda b: (0, 0)),
        ],
        out_specs=pl.BlockSpec((None, n_mem, kvw), lambda b: (b, 0, 0)),
        out_shape=jax.ShapeDtypeStruct((batch, n_mem, kvw), F32),
        compiler_params=_params("arbitrary"),
    )(mem, g_mem, w_kv)


def _kv_bwd(mem, dkv, g_mem, w_kv):
    batch, n_mem, d = mem.shape
    kvw = w_kv.shape[1]

    def body(mem_ref, dkv_ref, g_ref, w_ref, gw_ref, gg_ref):
        @pl.when(pl.program_id(0) == 0)
        def _():
            gw_ref[...] = jnp.zeros_like(gw_ref)
            gg_ref[...] = jnp.zeros_like(gg_ref)

        m = mem_ref[...]
        mn = m * _rms(m)
        dkv_b = dkv_ref[...].astype(BF16)
        gw_ref[...] += _mm_tn((mn * g_ref[...]).astype(BF16), dkv_b)
        gg_ref[...] += jnp.sum(_mm_nt(dkv_b, w_ref[...]) * mn, axis=0, keepdims=True)

    return pl.pallas_call(
        body,
        name="kv_bwd",
        grid=(batch,),
        in_specs=[
            pl.BlockSpec((None, n_mem, d), lambda b: (b, 0, 0)),
            pl.BlockSpec((None, n_mem, kvw), lambda b: (b, 0, 0)),
            pl.BlockSpec((1, d), lambda b: (0, 0)),
            pl.BlockSpec((d, kvw), lambda b: (0, 0)),
        ],
        out_specs=[pl.BlockSpec((d, kvw), lambda b: (0, 0)), pl.BlockSpec((1, d), lambda b: (0, 0))],
        out_shape=[jax.ShapeDtypeStruct((d, kvw), F32), jax.ShapeDtypeStruct((1, d), F32)],
        compiler_params=_params("arbitrary"),
    )(mem, dkv, g_mem, w_kv)


def _tail(x2, target2, gate, y_pool, y_ssm, q, kv, w_out, g_post):
    tokens, d = x2.shape
    pool_w, ssm_w, att_w, mix = y_pool.shape[1], y_ssm.shape[1], q.shape[1], gate.shape[1]
    batch, n_mem, kvw = kv.shape
    hd = att_w // MEM_HEADS
    cl = _token_tile(tokens // batch, 256)
    n_tiles = tokens // cl
    per_seq = tokens // batch // cl
    qk_scale = hd**-0.5

    def body(x_ref, tg_ref, gate_ref, yp_ref, ys_ref, q_ref, kv_ref, w_ref, g_ref,
             dres_ref, dgate_ref, dyp_ref, dys_ref, dq_ref, dkv_ref, gw_hbm, gg_ref, loss_ref, acc, sem):
        i = pl.program_id(0)

        @pl.when(i == 0)
        def _():
            acc[...] = jnp.zeros_like(acc)
            gg_ref[...] = jnp.zeros_like(gg_ref)
            loss_ref[...] = jnp.zeros_like(loss_ref)

        @pl.when(i % per_seq == 0)
        def _():
            dkv_ref[...] = jnp.zeros_like(dkv_ref)

        q = q_ref[...]
        k = kv_ref[:, 0:att_w].astype(BF16)
        v = kv_ref[:, att_w : 2 * att_w].astype(BF16)
        lane = lax.broadcasted_iota(jnp.int32, (1, att_w), 1)
        heads = [(lane >= h * hd) & (lane < (h + 1) * hd) for h in range(MEM_HEADS)]
        probs, q_heads = [], []
        att = jnp.zeros((cl, att_w), F32)
        for mask in heads:
            qh = jnp.where(mask, q, 0.0).astype(BF16)
            sc = _mm_nt(qh, k) * qk_scale
            e = jnp.exp(sc - jnp.max(sc, axis=-1, keepdims=True))
            p = e * (1.0 / jnp.sum(e, axis=-1, keepdims=True))
            att = att + jnp.where(mask, _mm(p.astype(BF16), v), 0.0)
            probs.append(p)
            q_heads.append(qh)

        ycat = jnp.concatenate([yp_ref[...], ys_ref[...], att], axis=1)
        gate = gate_ref[...]
        sig = jax.nn.sigmoid(gate)
        silu = gate * sig
        yg = (ycat * silu).astype(BF16)
        out = _mm(yg, w_ref[...])
        r = _rms(out)
        on = out * r
        g = g_ref[...]
        x = x_ref[...]
        err = x + on * g - tg_ref[...]
        loss_ref[...] += 0.5 * jnp.sum(jnp.mean(err * err, axis=-1, keepdims=True), axis=0, keepdims=True)
        dres = err * (1.0 / d)
        dres_ref[...] = dres
        gg_ref[...] += jnp.sum(dres * on, axis=0, keepdims=True)
        don = dres * g
        dout = (r * (don - on * jnp.mean(don * on, axis=-1, keepdims=True))).astype(BF16)
        acc[...] += _mm_tn(yg, dout)
        dyg = _mm_nt(dout, w_ref[...])
        dgate_ref[...] = dyg * ycat * (sig * (1.0 + gate * (1.0 - sig)))
        dycat = dyg * silu
        dyp_ref[...] = dycat[:, 0:pool_w]
        dys_ref[...] = dycat[:, pool_w : pool_w + ssm_w]
        datt = dycat[:, pool_w + ssm_w : mix]

        dq = jnp.zeros((cl, att_w), F32)
        dk = jnp.zeros((n_mem, att_w), F32)
        dv = jnp.zeros((n_mem, att_w), F32)
        for mask, p, qh in zip(heads, probs, q_heads):
            doh = jnp.where(mask, datt, 0.0).astype(BF16)
            dp = _mm_nt(doh, v)
            ds = (p * (dp - jnp.sum(p * dp, axis=-1, keepdims=True)) * qk_scale).astype(BF16)
            dq = dq + jnp.where(mask, _mm(ds, k), 0.0)
            dk = dk + _mm_tn(ds, qh)
            dv = dv + _mm_tn(p.astype(BF16), doh)
        dq_ref[...] = dq
        dkv_ref[:, 0:att_w] += dk
        dkv_ref[:, att_w : 2 * att_w] += dv

        @pl.when(i == n_tiles - 1)
        def _():
            cp = pltpu.make_async_copy(acc, gw_hbm, sem)
            cp.start()
            cp.wait()

    tok = lambda w: pl.BlockSpec((cl, w), lambda i: (i, 0))
    chunked = tok(ssm_w)
    per_batch = pl.BlockSpec((None, n_mem, kvw), lambda i: (i // per_seq, 0, 0))
    return pl.pallas_call(
        body,
        name="tail",
        grid=(n_tiles,),
        in_specs=[
            tok(d), tok(d), tok(mix), tok(pool_w), chunked, tok(att_w), per_batch,
            pl.BlockSpec((mix, d), lambda i: (0, 0)),
            pl.BlockSpec((1, d), lambda i: (0, 0)),
        ],
        out_specs=[
            tok(d), tok(mix), tok(pool_w), chunked, tok(att_w), per_batch,
            pl.BlockSpec(memory_space=pl.ANY),
            pl.BlockSpec((1, d), lambda i: (0, 0)),
            pl.BlockSpec((1, 1), lambda i: (0, 0)),
        ],
        out_shape=[
            jax.ShapeDtypeStruct((tokens, d), F32),
            jax.ShapeDtypeStruct((tokens, mix), F32),
            jax.ShapeDtypeStruct((tokens, pool_w), F32),
            jax.ShapeDtypeStruct((tokens, ssm_w), F32),
            jax.ShapeDtypeStruct((tokens, att_w), F32),
            jax.ShapeDtypeStruct(kv.shape, F32),
            jax.ShapeDtypeStruct((mix, d), F32),
            jax.ShapeDtypeStruct((1, d), F32),
            jax.ShapeDtypeStruct((1, 1), F32),
        ],
        scratch_shapes=[pltpu.VMEM((mix, d), F32), pltpu.SemaphoreType.DMA],
        compiler_params=_params("arbitrary"),
    )(x2, target2, gate, y_pool, y_ssm, q, kv, w_out, g_post)


def _pack(arrays):
    flat = jnp.concatenate([a.reshape(-1) for a in arrays])
    rows = -(-flat.size // (SUBLANES * LANES)) * SUBLANES
    return jnp.pad(flat, (0, rows * LANES - flat.size)).reshape(rows, LANES)


def _unpack(packed, like):
    flat, out, at = packed.reshape(-1), [], 0
    for a in like:
        out.append(flat[at : at + a.size].reshape(a.shape))
        at += a.size
    return out


def kernel(x, mem, g_pre, w_in, w_pool, pool_scale, a_re, a_im, log_dt, b_re, b_im, c_re, c_im, d_skip, w_glu, g_mem, w_kv, w_out, g_post, loss_target, m_g_pre, m_w_in, m_w_pool, m_pool_scale, m_a_re, m_a_im, m_log_dt, m_b_re, m_b_im, m_c_re, m_c_im, m_d_skip, m_w_glu, m_g_mem, m_w_kv, m_w_out, m_g_post, v_g_pre, v_w_in, v_w_pool, v_pool_scale, v_a_re, v_a_im, v_log_dt, v_b_re, v_b_im, v_c_re, v_c_im, v_d_skip, v_w_glu, v_g_mem, v_w_kv, v_w_out, v_g_post):
    batch, seq, d = x.shape
    cl = seq // NCH
    pool_w, ssm_w = pool_scale.shape[1], d_skip.shape[1]
    att_w = w_kv.shape[2] // 2
    tokens = batch * seq
    x2 = x.reshape(tokens, d)
    target2 = loss_target.reshape(tokens, d)

    w_in_g, w_out_g, w_kv_g, w_glu_g = _gather_weights([w_in[0], w_out[0], w_kv[0], w_glu[0]])
    w_out_f = w_out_g.reshape(N_DEV * w_out_g.shape[1], w_out_g.shape[2])
    w_kv_f = w_kv_g.reshape(N_DEV * w_kv_g.shape[1], w_kv_g.shape[2])
    w_glu_f = w_glu_g.transpose(1, 0, 2).reshape(w_glu_g.shape[1], N_DEV * w_glu_g.shape[2])

    wp_blk = jax.scipy.linalg.block_diag(*w_pool[0]).astype(BF16)
    ssm_params = (a_re[0], a_im[0], log_dt[0], b_re[0], b_im[0], c_re[0], c_im[0])
    (lam_row, b_tab, c_tab), tables_vjp = jax.vjp(_ssm_tables, *ssm_params)
    nb = b_tab.shape[0]
    lam8 = jnp.broadcast_to(lam_row, (SUBLANES, lam_row.shape[1]))
    lcl = _lam_power(a_re[0], a_im[0], log_dt[0], float(cl), 1.0, nb)
    b_bf, c_bf = b_tab.astype(BF16), c_tab.astype(BF16)

    u_pool, u_ssm, q, gate = _in_proj(x2, g_pre, w_in_g, pool_w, ssm_w, att_w)
    y_pool = _pool_fwd(u_pool, wp_blk, pool_scale, batch, seq)
    u_ssm = u_ssm.reshape(batch, seq, ssm_w)
    y_ssm, pre_ssm, z_ssm, init_ssm = _ssm_fwd(u_ssm, b_bf, c_bf, lam8, lcl, d_skip, w_glu_f)
    kv = _kv_fwd(mem, g_mem, w_kv_f)

    dres, dgate, dy_pool, dy_ssm, dq, dkv, gw_out, gg_post, loss_part = _tail(
        x2, target2, gate, y_pool, y_ssm.reshape(tokens, ssm_w), q, kv, w_out_f, g_post)

    gw_kv, gg_mem = _kv_bwd(mem, dkv, g_mem, w_kv_f)
    du_pool, gwp_dense, g_scale = _pool_bwd(u_pool, dy_pool, wp_blk, pool_scale, batch, seq)
    du_ssm, gb_tab, gc_tab, gw_glu, gd_skip, glam = _ssm_bwd(
        u_ssm, pre_ssm, z_ssm, dy_ssm.reshape(batch, seq, ssm_w), init_ssm, b_bf, b_bf.transpose(0, 2, 1),
        c_bf.transpose(0, 2, 1), lam8, lcl, d_skip, w_glu_f)
    grad_x2, gw_in, gg_pre = _in_proj_bwd(
        x2, dres, du_pool, du_ssm.reshape(tokens, ssm_w), dq, dgate, g_pre, w_in_g)

    gw = pool_w // len(POOL_WINDOWS)
    gw_pool = jnp.stack([gwp_dense[i * gw : (i + 1) * gw, i * gw : (i + 1) * gw] for i in range(len(POOL_WINDOWS))])
    g_ssm = tables_vjp((glam, gb_tab, gc_tab))

    small_w = [g_pre, w_pool, pool_scale, a_re, a_im, log_dt, b_re, b_im, c_re, c_im, d_skip, g_mem, g_post]
    small_m = [m_g_pre, m_w_pool, m_pool_scale, m_a_re, m_a_im, m_log_dt, m_b_re, m_b_im, m_c_re, m_c_im, m_d_skip, m_g_mem, m_g_post]
    small_v = [v_g_pre, v_w_pool, v_pool_scale, v_a_re, v_a_im, v_log_dt, v_b_re, v_b_im, v_c_re, v_c_im, v_d_skip, v_g_mem, v_g_post]
    small_g = [gg_pre, gw_pool, g_scale, *g_ssm, gd_skip, gg_mem, gg_post]
    flat2 = lambda a: a.reshape(-1, a.shape[-1])
    sg = _unpack(_reduce_small(_pack(small_g + [loss_part])), [flat2(a) for a in small_w] + [loss_part])
    loss = sg[-1].reshape(())
    updates = _adamw_small(sg[:-1], [flat2(a) for a in small_w], [flat2(a) for a in small_m], [flat2(a) for a in small_v])
    sg = [g.reshape(a.shape) for g, a in zip(sg[:-1], small_w)]
    sd, sm, sv = ([u[kind].reshape(a.shape) for u, a in zip(updates, small_w)] for kind in range(3))

    big = _reduce_big(
        [gw_in, gw_glu.reshape(ssm_w, N_DEV, -1).transpose(1, 0, 2), gw_kv.reshape(N_DEV, -1, gw_kv.shape[1]),
         gw_out.reshape(N_DEV, -1, gw_out.shape[1])],
        [w_in[0], w_glu[0], w_kv[0], w_out[0]],
        [m_w_in[0], m_w_glu[0], m_w_kv[0], m_w_out[0]],
        [v_w_in[0], v_w_glu[0], v_w_kv[0], v_w_out[0]])
    big = {name: tuple(t[None] for t in res) for name, res in zip(["w_in", "w_glu", "w_kv", "w_out"], big)}

    order = ["g_pre", "w_in", "w_pool", "pool_scale", "a_re", "a_im", "log_dt", "b_re", "b_im", "c_re", "c_im",
             "d_skip", "w_glu", "g_mem", "w_kv", "w_out", "g_post"]
    small_names = ["g_pre", "w_pool", "pool_scale", "a_re", "a_im", "log_dt", "b_re", "b_im", "c_re", "c_im",
                   "d_skip", "g_mem", "g_post"]
    outs = [[], [], [], []]
    for name in order:
        if name in big:
            parts = big[name]
        else:
            j = small_names.index(name)
            parts = (sg[j], sd[j], sm[j], sv[j])
        for kind in range(4):
            outs[kind].append(parts[kind])
    return (loss, grad_x2.reshape(batch, seq, d), *outs[0], *outs[1], *outs[2], *outs[3])
```

```python
import functools
import math

import jax
import jax.numpy as jnp
from jax import lax
from jax.experimental import pallas as pl
from jax.experimental.pallas import tpu as pltpu

F32 = jnp.float32
BF16 = jnp.bfloat16
MESH = pl.DeviceIdType.MESH

N_DEV = 8
NCH = 8
SUBLANES = 8
LANES = 128
VMEM_LIMIT = 56 * 1024 * 1024

EPS = 1e-6
POOL_WINDOWS = (2, 4, 8, 16)
MEM_HEADS = 4
SSM_GROUP = 16
SSM_N = 64
ADAM_LR, ADAM_B1, ADAM_B2, ADAM_EPS, ADAM_WD, ADAM_STEP = 0.001, 0.9, 0.999, 1e-08, 0.01, 10


def _mm(a, b):
    return jnp.dot(a, b, preferred_element_type=F32)


def _mm_nt(a, b):
    return lax.dot_general(a, b, (((1,), (1,)), ((), ())), preferred_element_type=F32)


def _mm_tn(a, b):
    return lax.dot_general(a, b, (((0,), (0,)), ((), ())), preferred_element_type=F32)


def _params(*sem):
    return pltpu.CompilerParams(dimension_semantics=sem or None, vmem_limit_bytes=VMEM_LIMIT)


def _adamw(w, g, m, v):
    m = ADAM_B1 * m + (1.0 - ADAM_B1) * g
    v = ADAM_B2 * v + (1.0 - ADAM_B2) * (g * g)
    m_hat = m / (1.0 - ADAM_B1**ADAM_STEP)
    v_hat = v / (1.0 - ADAM_B2**ADAM_STEP)
    delta = -ADAM_LR * (m_hat / (jnp.sqrt(v_hat) + ADAM_EPS) + ADAM_WD * w)
    return delta, m, v


def _gelu(x):
    k = math.sqrt(2.0 / math.pi)
    return 0.5 * x * (1.0 + jnp.tanh(k * (x + 0.044715 * x * x * x)))


def _gelu_grad(x):
    k = math.sqrt(2.0 / math.pi)
    th = jnp.tanh(k * (x + 0.044715 * x * x * x))
    return 0.5 * (1.0 + th) + 0.5 * x * (1.0 - th * th) * (k * (1.0 + 3.0 * 0.044715 * x * x))


def _place():
    return lax.axis_index("x"), lax.axis_index("y"), lax.axis_index("c")


def _gather_steps(ins, outs, send_sems, recv_sems):
    n = len(ins)
    x, y, c = _place()
    me, sibling = (x, y, c), (x, y, 1 - c)
    chips = [(1 - x, y), (x, 1 - y), (1 - x, 1 - y)]
    sent = []

    def slot(px, py, pc):
        return 4 * px + 2 * py + pc

    def copy(a, k, block, to):
        ref = outs[a].at[slot(*block)]
        return pltpu.make_async_remote_copy(
            src_ref=ref, dst_ref=ref, send_sem=send_sems.at[a, k], recv_sem=recv_sems.at[a, k],
            device_id=to, device_id_type=MESH)

    def start():
        for a in range(n):
            outs[a][slot(*me)] = ins[a][...].astype(outs[a].dtype)
        for a in range(n):
            sent.append(copy(a, 0, me, sibling))
            sent.extend(copy(a, 1 + j, me, (*chip, c)) for j, chip in enumerate(chips))
        for cp in sent:
            cp.start()

    def forward():
        for j, chip in enumerate(chips):
            for a in range(n):
                copy(a, 1 + j, (*chip, c), me).wait_recv()
                cp = copy(a, 4 + j, (*chip, c), sibling)
                cp.start()
                sent.append(cp)

    def finish():
        for a in range(n):
            copy(a, 0, sibling, me).wait_recv()
            for j, chip in enumerate(chips):
                copy(a, 4 + j, (*chip, 1 - c), me).wait_recv()
        for cp in sent:
            cp.wait_send()

    return start, forward, finish


def _exchange_steps(n, src_of, landing, send_sems, recv_sems):
    x, y, c = _place()
    me = 4 * x + 2 * y + c
    peers = []
    for j in range(1, N_DEV):
        px = 1 - x if j & 4 else x
        py = 1 - y if j & 2 else y
        pc = 1 - c if j & 1 else c
        peers.append((px, py, pc))

    def copy(a, j, from_slot, to_slot, peer):
        return pltpu.make_async_remote_copy(
            src_ref=src_of(a, to_slot), dst_ref=landing[a].at[from_slot],
            send_sem=send_sems.at[a, j], recv_sem=recv_sems.at[a, j], device_id=peer, device_id_type=MESH)

    def start():
        for a in range(n):
            for j, p in enumerate(peers):
                copy(a, j, me, 4 * p[0] + 2 * p[1] + p[2], p).start()

    def finish():
        for a in range(n):
            for j, p in enumerate(peers):
                slot = 4 * p[0] + 2 * p[1] + p[2]
                copy(a, j, slot, slot, p).wait_recv()
        for a in range(n):
            for j, p in enumerate(peers):
                copy(a, j, me, 4 * p[0] + 2 * p[1] + p[2], p).wait_send()

    return start, finish


def _ordered_sum(gathered, out_ref):
    rows = out_ref.shape[0]

    def step(i, _):
        r = pl.ds(pl.multiple_of(i * SUBLANES, SUBLANES), SUBLANES)
        g = gathered[0, r, :]
        for d in range(1, N_DEV):
            g = g + gathered[d, r, :]
        out_ref[r, :] = g
        return 0

    lax.fori_loop(0, rows // SUBLANES, step, 0)


def _gather_weights(blocks, later):
    n, k = len(blocks), len(later)

    def body(*refs):
        refs = list(refs)
        take = lambda cnt: [refs.pop(0) for _ in range(cnt)]
        ins, later_in, outs, later_out = take(n), take(k), take(n), take(k)
        start, forward, finish = _gather_steps(ins, outs, *refs)
        start()
        for src, dst in zip(later_in, later_out):
            dst[...] = src[...].astype(BF16)
        forward()
        finish()

    vmem = pl.BlockSpec(memory_space=pltpu.VMEM)
    res = pl.pallas_call(
        body,
        name="gather_weights",
        out_shape=[jax.ShapeDtypeStruct((N_DEV, *b.shape), BF16) for b in blocks]
        + [jax.ShapeDtypeStruct(b.shape, BF16) for b in later],
        in_specs=[vmem] * (n + k),
        out_specs=[vmem] * (n + k),
        scratch_shapes=[pltpu.SemaphoreType.DMA((n, 7)), pltpu.SemaphoreType.DMA((n, 7))],
        compiler_params=_params(),
    )(*blocks, *later)
    return res[:n], res[n:]


def _adamw_small(gs, ws, ms, vs):
    n = len(gs)

    def body(*refs):
        g, w, m, v = refs[:n], refs[n : 2 * n], refs[2 * n : 3 * n], refs[3 * n : 4 * n]
        outs = refs[4 * n :]
        for a in range(n):
            delta, nm, nv = _adamw(w[a][...], g[a][...], m[a][...], v[a][...])
            outs[3 * a][...] = delta
            outs[3 * a + 1][...] = nm
            outs[3 * a + 2][...] = nv

    vmem = pl.BlockSpec(memory_space=pltpu.VMEM)
    out_shape = []
    for wa in ws:
        out_shape += [jax.ShapeDtypeStruct(wa.shape, F32)] * 3
    res = pl.pallas_call(
        body,
        name="adamw_small",
        out_shape=out_shape,
        in_specs=[vmem] * (4 * n),
        out_specs=[vmem] * (3 * n),
        compiler_params=_params(),
    )(*gs, *ws, *ms, *vs)
    return [tuple(res[3 * a : 3 * a + 3]) for a in range(n)]


def _reduce_all(parts, ws, ms, vs, small):
    n = len(parts)
    parts4 = [p.reshape(4, 2, *p.shape[1:]) for p in parts]
    blks = [p.shape[1:] for p in parts]

    def body(*refs):
        refs = list(refs)
        take = lambda k: [refs.pop(0) for _ in range(k)]
        part, w_in, m_in, v_in = take(n), take(n), take(n), take(n)
        (small_ref,) = take(1)
        outs = take(4 * n)
        small_all, small_sum = take(2)
        own, r1, r2 = take(n), take(n), take(n)
        s1_send, s1_recv, s2_send, s2_recv, loc, small_send, small_recv = refs
        small_start, small_forward, small_finish = _gather_steps([small_ref], [small_all], small_send, small_recv)
        x, y, c = _place()
        sibling = (x, y, 1 - c)
        chips = [(1 - x, y), (x, 1 - y), (1 - x, 1 - y)]

        def rowwise(rows, fn):
            chunk = math.gcd(rows, 128)

            def step(i, _):
                fn(pl.ds(pl.multiple_of(i * chunk, chunk), chunk))
                return 0

            lax.fori_loop(0, rows // chunk, step, 0)

        stage1, local = [], []
        for a in range(n):
            cp = pltpu.make_async_remote_copy(
                src_ref=part[a].at[:, 1 - c], dst_ref=r1[a], send_sem=s1_send.at[a], recv_sem=s1_recv.at[a],
                device_id=sibling, device_id_type=MESH)
            cp.start()
            stage1.append(cp)
            lc = pltpu.make_async_copy(part[a].at[:, c], own[a], loc.at[a])
            lc.start()
            local.append(lc)
        small_start()
        stage2 = []
        for a in range(n):
            local[a].wait()
            stage1[a].wait_recv()
            for chip in range(4):

                def add(r, a=a, chip=chip):
                    own[a][chip, r, :] = own[a][chip, r, :] + r1[a][chip, r, :]

                rowwise(blks[a][0], add)
            for k, chip in enumerate(chips):
                cp = pltpu.make_async_remote_copy(
                    src_ref=own[a].at[2 * chip[0] + chip[1]], dst_ref=r2[a].at[k],
                    send_sem=s2_send.at[a, k], recv_sem=s2_recv.at[a, k],
                    device_id=(*chip, c), device_id_type=MESH)
                cp.start()
                stage2.append(cp)
        small_forward()
        for a in range(n):
            for k, chip in enumerate(chips):
                stage2[3 * a + k].wait_recv()
            g_ref, d_ref, nm_ref, nv_ref = outs[4 * a : 4 * a + 4]

            def update(r, a=a, g_ref=g_ref, d_ref=d_ref, nm_ref=nm_ref, nv_ref=nv_ref):
                g = own[a][2 * x + y, r, :] + r2[a][0, r, :] + r2[a][1, r, :] + r2[a][2, r, :]
                delta, nm, nv = _adamw(w_in[a][r, :], g, m_in[a][r, :], v_in[a][r, :])
                g_ref[r, :] = g
                d_ref[r, :] = delta
                nm_ref[r, :] = nm
                nv_ref[r, :] = nv

            rowwise(blks[a][0], update)
        small_finish()
        _ordered_sum(small_all, small_sum)
        for cp in stage1 + stage2:
            cp.wait_send()

    vmem = pl.BlockSpec(memory_space=pltpu.VMEM)
    hbm = pl.BlockSpec(memory_space=pl.ANY)
    out_shape = []
    for b in blks:
        out_shape += [jax.ShapeDtypeStruct(b, F32)] * 4
    out_shape += [jax.ShapeDtypeStruct((N_DEV, *small.shape), F32), jax.ShapeDtypeStruct(small.shape, F32)]
    scratch = (
        [pltpu.VMEM((4, *b), F32) for b in blks]
        + [pltpu.VMEM((4, *b), F32) for b in blks]
        + [pltpu.VMEM((3, *b), F32) for b in blks]
        + [pltpu.SemaphoreType.DMA((n,)), pltpu.SemaphoreType.DMA((n,)), pltpu.SemaphoreType.DMA((n, 3)),
           pltpu.SemaphoreType.DMA((n, 3)), pltpu.SemaphoreType.DMA((n,)),
           pltpu.SemaphoreType.DMA((1, 7)), pltpu.SemaphoreType.DMA((1, 7))]
    )
    res = pl.pallas_call(
        body,
        name="reduce_all",
        out_shape=out_shape,
        in_specs=[hbm] * n + [vmem] * (3 * n + 1),
        out_specs=[vmem] * (4 * n + 2),
        scratch_shapes=scratch,
        compiler_params=_params(),
    )(*parts4, *ws, *ms, *vs, small)
    return [tuple(res[4 * a : 4 * a + 4]) for a in range(n)], res[4 * n + 1]


def _rms(x):
    return lax.rsqrt(jnp.mean(x * x, axis=-1, keepdims=True) + EPS)


def _token_tile(tokens, want):
    tile = min(want, tokens // 2)
    assert tokens % tile == 0 and tile % 16 == 0
    return tile


def _in_proj(x2, g_pre, w_in_g, shards, pool_w, ssm_w, att_w):
    tokens, d = x2.shape
    nb = w_in_g.shape[2]
    mix = pool_w + ssm_w + att_w
    half = N_DEV // 2
    cl = _token_tile(tokens, 512)
    n_tiles = tokens // cl
    ns = len(shards)
    assert half * nb == mix and nb == 256 and pool_w == 384 and ssm_w == 384 and att_w == 256

    def body(x_ref, g_ref, w_ref, *rest):
        shard_hbm, (up_ref, us_ref, q_ref, gate_ref) = rest[:ns], rest[ns : ns + 4]
        gathered = rest[ns + 4 : 2 * ns + 4]
        send_sems, recv_sems, own_sems = rest[2 * ns + 4 :]
        i = pl.program_id(0)
        x_pos, y_pos, c_pos = _place()
        own = [pltpu.make_async_copy(shard_hbm[a], gathered[a].at[4 * x_pos + 2 * y_pos + c_pos], own_sems.at[a])
               for a in range(ns)]
        start, finish = _exchange_steps(ns, lambda a, slot: shard_hbm[a], gathered, send_sems, recv_sems)

        @pl.when(i == 0)
        def _():
            start()
            for cp in own:
                cp.start()

        @pl.when(i == n_tiles - 1)
        def _():
            finish()
            for cp in own:
                cp.wait()

        x = x_ref[...]
        h = (x * _rms(x) * g_ref[...]).astype(BF16)
        p = [_mm(h, w_ref[j]) for j in range(half)]
        up_ref[:, 0:256] = p[0]
        up_ref[:, 256:384] = p[1][:, 0:128]
        us_ref[:, 0:128] = p[1][:, 128:256]
        us_ref[:, 128:384] = p[2]
        q_ref[...] = p[3]
        for j in range(half):
            gate_ref[:, j * nb : (j + 1) * nb] = _mm(h, w_ref[half + j])

    hbm = pl.BlockSpec(memory_space=pl.ANY)
    res = pl.pallas_call(
        body,
        name="in_proj",
        grid=(n_tiles,),
        in_specs=[
            pl.BlockSpec((cl, d), lambda i: (i, 0)),
            pl.BlockSpec((1, d), lambda i: (0, 0)),
            pl.BlockSpec((N_DEV, d, nb), lambda i: (0, 0, 0)),
        ]
        + [hbm] * ns,
        out_specs=[
            pl.BlockSpec((cl, pool_w), lambda i: (i, 0)),
            pl.BlockSpec((cl, ssm_w), lambda i: (i, 0)),
            pl.BlockSpec((cl, att_w), lambda i: (i, 0)),
            pl.BlockSpec((cl, mix), lambda i: (i, 0)),
        ]
        + [hbm] * ns,
        out_shape=[
            jax.ShapeDtypeStruct((tokens, pool_w), F32),
            jax.ShapeDtypeStruct((tokens, ssm_w), F32),
            jax.ShapeDtypeStruct((tokens, att_w), F32),
            jax.ShapeDtypeStruct((tokens, mix), F32),
        ]
        + [jax.ShapeDtypeStruct((N_DEV, *a.shape), a.dtype) for a in shards],
        scratch_shapes=[pltpu.SemaphoreType.DMA((ns, 7)), pltpu.SemaphoreType.DMA((ns, 7)),
                        pltpu.SemaphoreType.DMA((ns,))],
        compiler_params=_params("arbitrary"),
    )(x2, g_pre, w_in_g, *shards)
    return res[:4], res[4:]


def _in_proj_bwd(x2, dres, du_pool, du_ssm, dq, dgate, g_pre, w_in_g):
    tokens, d = x2.shape
    nb = w_in_g.shape[2]
    pool_w, ssm_w, att_w, mix = du_pool.shape[1], du_ssm.shape[1], dq.shape[1], dgate.shape[1]
    half = N_DEV // 2
    cl = _token_tile(tokens, 512)
    n_tiles = tokens // cl

    def body(x_ref, dres_ref, dup_ref, dus_ref, dq_ref, dgate_ref, g_ref, w_ref, gx_ref, gw_hbm, gg_ref, acc, sem):
        i = pl.program_id(0)

        @pl.when(i == 0)
        def _():
            acc[...] = jnp.zeros_like(acc)
            gg_ref[...] = jnp.zeros_like(gg_ref)

        x = x_ref[...]
        r = _rms(x)
        xn = x * r
        g = g_ref[...]
        h = (xn * g).astype(BF16)
        dval = jnp.concatenate([dup_ref[...], dus_ref[...], dq_ref[...]], axis=1)
        dh = jnp.zeros((cl, d), F32)
        for j in range(N_DEV):
            src = dval if j < half else dgate_ref[...]
            jj = j % half
            dp = src[:, jj * nb : (jj + 1) * nb].astype(BF16)
            dh = dh + _mm_nt(dp, w_ref[j])
            acc[j] += _mm_tn(h, dp)
        gg_ref[...] += jnp.sum(dh * xn, axis=0, keepdims=True)
        dxn = dh * g
        gx_ref[...] = dres_ref[...] + r * (dxn - xn * jnp.mean(dxn * xn, axis=-1, keepdims=True))

        @pl.when(i == n_tiles - 1)
        def _():
            cp = pltpu.make_async_copy(acc, gw_hbm, sem)
            cp.start()
            cp.wait()

    return pl.pallas_call(
        body,
        name="in_proj_bwd",
        grid=(n_tiles,),
        in_specs=[
            pl.BlockSpec((cl, d), lambda i: (i, 0)),
            pl.BlockSpec((cl, d), lambda i: (i, 0)),
            pl.BlockSpec((cl, pool_w), lambda i: (i, 0)),
            pl.BlockSpec((cl, ssm_w), lambda i: (i, 0)),
            pl.BlockSpec((cl, att_w), lambda i: (i, 0)),
            pl.BlockSpec((cl, mix), lambda i: (i, 0)),
            pl.BlockSpec((1, d), lambda i: (0, 0)),
            pl.BlockSpec((N_DEV, d, nb), lambda i: (0, 0, 0)),
        ],
        out_specs=[
            pl.BlockSpec((cl, d), lambda i: (i, 0)),
            pl.BlockSpec(memory_space=pl.ANY),
            pl.BlockSpec((1, d), lambda i: (0, 0)),
        ],
        out_shape=[
            jax.ShapeDtypeStruct((tokens, d), F32),
            jax.ShapeDtypeStruct((N_DEV, d, nb), F32),
            jax.ShapeDtypeStruct((1, d), F32),
        ],
        scratch_shapes=[pltpu.VMEM((N_DEV, d, nb), F32), pltpu.SemaphoreType.DMA],
        compiler_params=_params("arbitrary"),
    )(x2, dres, du_pool, du_ssm, dq, dgate, g_pre, w_in_g)


def _pool_geometry(seq, width):
    gw = width // len(POOL_WINDOWS)
    col = lax.broadcasted_iota(jnp.int32, (1, width), 1)
    win = jnp.full((1, width), float(POOL_WINDOWS[-1]), F32)
    for gi in range(len(POOL_WINDOWS) - 2, -1, -1):
        win = jnp.where(col < (gi + 1) * gw, float(POOL_WINDOWS[gi]), win)
    row = lax.broadcasted_iota(jnp.int32, (seq, width), 0)
    cnt = jnp.minimum((row + 1).astype(F32), win)
    return win, row, cnt


def _window_sums(a, row, win, seq, back):
    sums = []
    s, k = a, 1
    while k < POOL_WINDOWS[-1]:
        if back:
            shifted = jnp.where(row < seq - k, pltpu.roll(s, seq - k, 0), 0.0)
        else:
            shifted = jnp.where(row >= k, pltpu.roll(s, k, 0), 0.0)
        s = s + shifted
        k *= 2
        sums.append((k, s))
    out = sums[-1][1]
    for k, s in reversed(sums[:-1]):
        out = jnp.where(win <= float(k), s, out)
    return out


def _pool_fwd(u2, wp_blk, scale, batch, seq):
    width = u2.shape[1]

    def body(u_ref, w_ref, s_ref, y_ref):
        u = u_ref[...]
        win, row, cnt = _pool_geometry(seq, width)
        diff = _window_sums(u, row, win, seq, False) / cnt - u
        y_ref[...] = _mm(diff.astype(BF16), w_ref[...]) * s_ref[...]

    return pl.pallas_call(
        body,
        name="pool_fwd",
        grid=(batch,),
        in_specs=[
            pl.BlockSpec((seq, width), lambda b: (b, 0)),
            pl.BlockSpec((width, width), lambda b: (0, 0)),
            pl.BlockSpec((1, width), lambda b: (0, 0)),
        ],
        out_specs=pl.BlockSpec((seq, width), lambda b: (b, 0)),
        out_shape=jax.ShapeDtypeStruct(u2.shape, F32),
        compiler_params=_params("arbitrary"),
    )(u2, wp_blk, scale)


def _pool_bwd(u2, dy2, wp_blk, scale, batch, seq):
    width = u2.shape[1]

    def body(u_ref, dy_ref, w_ref, s_ref, du_ref, gw_ref, gs_ref):
        @pl.when(pl.program_id(0) == 0)
        def _():
            gw_ref[...] = jnp.zeros_like(gw_ref)
            gs_ref[...] = jnp.zeros_like(gs_ref)

        u = u_ref[...]
        dy = dy_ref[...]
        win, row, cnt = _pool_geometry(seq, width)
        diff = (_window_sums(u, row, win, seq, False) / cnt - u).astype(BF16)
        gs_ref[...] += jnp.sum(dy * _mm(diff, w_ref[...]), axis=0, keepdims=True)
        dys = (dy * s_ref[...]).astype(BF16)
        gw_ref[...] += _mm_tn(diff, dys)
        dd = _mm_nt(dys, w_ref[...])
        du_ref[...] = _window_sums(dd / cnt, row, win, seq, True) - dd

    return pl.pallas_call(
        body,
        name="pool_bwd",
        grid=(batch,),
        in_specs=[
            pl.BlockSpec((seq, width), lambda b: (b, 0)),
            pl.BlockSpec((seq, width), lambda b: (b, 0)),
            pl.BlockSpec((width, width), lambda b: (0, 0)),
            pl.BlockSpec((1, width), lambda b: (0, 0)),
        ],
        out_specs=[
            pl.BlockSpec((seq, width), lambda b: (b, 0)),
            pl.BlockSpec((width, width), lambda b: (0, 0)),
            pl.BlockSpec((1, width), lambda b: (0, 0)),
        ],
        out_shape=[
            jax.ShapeDtypeStruct(u2.shape, F32),
            jax.ShapeDtypeStruct((width, width), F32),
            jax.ShapeDtypeStruct((1, width), F32),
        ],
        compiler_params=_params("arbitrary"),
    )(u2, dy2, wp_blk, scale)


def _state_row(z, n_blocks):
    re = jnp.real(z).reshape(n_blocks, -1)
    im = jnp.imag(z).reshape(n_blocks, -1)
    return jnp.concatenate([re, im], axis=1).reshape(1, -1)


def _ssm_tables(a_re, a_im, log_dt, b_re, b_im, c_re, c_im):
    groups, n_state = a_re.shape
    ch = b_re.shape[2]
    nb = groups * ch // LANES
    gl = groups // nb
    lam = lax.complex(a_re, a_im)
    lam_bar = jnp.exp(lam * jnp.exp(log_dt)[:, None])
    b_bar = ((lam_bar - 1.0) / lam)[..., None] * lax.complex(b_re, b_im)
    eye = jnp.eye(gl, dtype=F32)

    def rows_to_state(t):
        return jnp.einsum("sgnc,gh->sgchn", t.reshape(nb, gl, n_state, ch), eye).reshape(nb, gl * ch, gl * n_state)

    def state_to_rows(t):
        return jnp.einsum("sgcn,gh->shngc", t.reshape(nb, gl, ch, n_state), eye).reshape(nb, gl * n_state, gl * ch)

    b_tab = jnp.concatenate([rows_to_state(jnp.real(b_bar)), rows_to_state(jnp.imag(b_bar))], axis=2)
    c_tab = jnp.concatenate([state_to_rows(c_re), -state_to_rows(c_im)], axis=1)
    return _state_row(lam_bar, nb), b_tab, c_tab


def _lam_power(a_re, a_im, log_dt, power, scale, n_blocks):
    return _state_row(scale * jnp.exp(lax.complex(a_re, a_im) * jnp.exp(log_dt)[:, None] * power), n_blocks)


def _state_blocks(s2, n_blocks, width):
    half = s2 // n_blocks // 2
    assert half % width == 0
    return [(b * 2 * half + o, b * 2 * half + half + o) for b in range(n_blocks) for o in range(0, half, width)]


def _scan(src_ref, dst_ref, st_ref, lam8_ref, n_groups, s, n_blocks, reverse, store):
    lb = 512
    for re0, im0 in _state_blocks(2 * s, n_blocks, lb):
        cr, ci = pl.ds(re0, lb), pl.ds(im0, lb)
        lr = lam8_ref[:, cr]
        li = -lam8_ref[:, ci] if reverse else lam8_ref[:, ci]

        def step(i, carry, cr=cr, ci=ci, lr=lr, li=li):
            hr, hi = carry
            grp = n_groups - 1 - i if reverse else i
            rows = pl.ds(pl.multiple_of(grp * SUBLANES, SUBLANES), SUBLANES)
            nr = lr * hr - li * hi + src_ref[rows, cr]
            ni = lr * hi + li * hr + src_ref[rows, ci]
            if store:
                dst_ref[rows, cr] = nr
                dst_ref[rows, ci] = ni
            return nr, ni

        hr, hi = lax.fori_loop(0, n_groups, step, (st_ref[:, cr], st_ref[:, ci]), unroll=2)
        st_ref[:, cr] = hr
        st_ref[:, ci] = hi


def _pack_state(re, im):
    hi = lax.bitcast_convert_type(re.astype(BF16).astype(F32), jnp.uint32)
    lo = lax.bitcast_convert_type(im.astype(BF16).astype(F32), jnp.uint32)
    return hi | (lo >> 16)


def _unpack_state(word):
    re = lax.bitcast_convert_type(word & jnp.uint32(0xFFFF0000), F32)
    im = lax.bitcast_convert_type(word << 16, F32)
    return re, im


def _scan_adjoint(dh_ref, hprev_ref, group0, stg_ref, acc_ref, lam8_ref, n_groups, s, n_blocks):
    lb = 512
    half = s // n_blocks
    for re0, im0 in _state_blocks(2 * s, n_blocks, lb):
        cr, ci = pl.ds(re0, lb), pl.ds(im0, lb)
        ch = pl.ds(re0 // (2 * half) * half + re0 % (2 * half), lb)
        lr, li = lam8_ref[:, cr], -lam8_ref[:, ci]

        def step(i, carry, cr=cr, ci=ci, ch=ch, lr=lr, li=li):
            gr, gi, ar, ai = carry
            grp = n_groups - 1 - i
            rows = pl.ds(pl.multiple_of(grp * SUBLANES, SUBLANES), SUBLANES)
            ngr = lr * gr - li * gi + dh_ref[rows, cr]
            ngi = lr * gi + li * gr + dh_ref[rows, ci]
            hr, hi = _unpack_state(hprev_ref[pl.ds(pl.multiple_of((group0 + grp) * SUBLANES, SUBLANES), SUBLANES), ch])
            ar = ar + hr * ngr + hi * ngi
            ai = ai + hr * ngi - hi * ngr
            dh_ref[rows, cr] = ngr
            dh_ref[rows, ci] = ngi
            return ngr, ngi, ar, ai

        init = (stg_ref[:, cr], stg_ref[:, ci], acc_ref[:, cr], acc_ref[:, ci])
        gr, gi, ar, ai = lax.fori_loop(0, n_groups, step, init)
        stg_ref[:, cr] = gr
        stg_ref[:, ci] = gi
        acc_ref[:, cr] = ar
        acc_ref[:, ci] = ai


def _chunk_starts(st_ref, init_ref, lcl_ref, s, n_blocks):
    w = s // n_blocks
    init_ref[0:1, :] = jnp.zeros((1, 2 * s), F32)
    for re0, im0 in _state_blocks(2 * s, n_blocks, w):
        re, im = pl.ds(re0, w), pl.ds(im0, w)
        ar, ai = lcl_ref[:, re], lcl_ref[:, im]
        cr = jnp.zeros((1, w), F32)
        ci = jnp.zeros((1, w), F32)
        for k in range(1, NCH):
            cr, ci = (ar * cr - ai * ci + st_ref[k - 1 : k, re], ar * ci + ai * cr + st_ref[k - 1 : k, im])
            init_ref[k : k + 1, re] = cr
            init_ref[k : k + 1, im] = ci


def _chunk_starts_adjoint(stg_ref, initg_ref, lcl_ref, s, n_blocks):
    w = s // n_blocks
    initg_ref[NCH - 1 : NCH, :] = jnp.zeros((1, 2 * s), F32)
    for re0, im0 in _state_blocks(2 * s, n_blocks, w):
        re, im = pl.ds(re0, w), pl.ds(im0, w)
        ar, ai = lcl_ref[:, re], -lcl_ref[:, im]
        gr = jnp.zeros((1, w), F32)
        gi = jnp.zeros((1, w), F32)
        for k in range(NCH - 2, -1, -1):
            gr, gi = (stg_ref[k + 1 : k + 2, re] + ar * gr - ai * gi, stg_ref[k + 1 : k + 2, im] + ar * gi + ai * gr)
            initg_ref[k : k + 1, re] = gr
            initg_ref[k : k + 1, im] = gi


def _ssm_rows(seq):
    rows = min(512, seq // 2)
    assert seq % rows == 0 and rows % SUBLANES == 0
    return rows


def _chunk_copies(hbm_ref, b, cm_ref, sems, to_cm):
    cl = cm_ref.shape[0]
    copies = []
    for k in range(NCH):
        nat, cm = hbm_ref.at[b, pl.ds(k * cl, cl), :], cm_ref.at[:, k, :]
        src, dst = (nat, cm) if to_cm else (cm, nat)
        copies.append(pltpu.make_async_copy(src, dst, sems.at[k]))
    return copies


def _blockwise(fn, n_blocks):
    return jnp.concatenate([fn(b) for b in range(n_blocks)], axis=1)


def _ssm_fwd(u, b_tab, c_tab, lam8, lcl, d_skip, w_glu):
    batch, seq, width = u.shape
    s = lam8.shape[1] // 2
    nb = b_tab.shape[0]
    sb = 2 * s // nb
    cl = seq // NCH
    rows = _ssm_rows(seq)
    n_tiles = seq // rows
    n_groups = rows // SUBLANES

    def body(u_hbm, b_ref, c_ref, lam_ref, lcl_ref, d_ref, wg_ref, y_hbm, pre_ref, z_ref, init_ref,
             u_cm, y_cm, bu, st, sems):
        b, ph, t = pl.program_id(0), pl.program_id(1), pl.program_id(2)
        tile_groups = pl.ds(pl.multiple_of(t * n_groups, n_groups), n_groups)

        @pl.when((ph == 0) & (t == 0))
        def _():
            loads = _chunk_copies(u_hbm, b, u_cm, sems, True)
            for cp in loads:
                cp.start()
            st[...] = jnp.zeros_like(st)
            for cp in loads:
                cp.wait()

        @pl.when((ph == 1) & (t == 0))
        def _():
            st[...] = init_ref[...]

        u_t = u_cm[tile_groups].reshape(rows, width)
        u_b = u_t.astype(BF16)
        for blk in range(nb):
            bu[:, blk * sb : (blk + 1) * sb] = _mm(u_b[:, blk * LANES : (blk + 1) * LANES], b_ref[blk])

        @pl.when(ph == 0)
        def _():
            _scan(bu, bu, st, lam_ref, n_groups, s, nb, False, False)

        @pl.when((ph == 0) & (t == n_tiles - 1))
        def _():
            _chunk_starts(st, init_ref, lcl_ref, s, nb)

        @pl.when(ph == 1)
        def _():
            _scan(bu, bu, st, lam_ref, n_groups, s, nb, False, True)
            hs = lambda blk: _mm(bu[:, blk * sb : (blk + 1) * sb].astype(BF16), c_ref[blk])
            pre = _blockwise(hs, nb) + d_ref[...] * u_t
            z = _mm(_gelu(pre).astype(BF16), wg_ref[...])
            pre_ref[...] = pre
            z_ref[...] = z
            y = z[:, 0:width] * jax.nn.sigmoid(z[:, width : 2 * width])
            y_cm[tile_groups] = y.reshape(n_groups, SUBLANES, width)

        @pl.when((ph == 1) & (t == n_tiles - 1))
        def _():
            stores = _chunk_copies(y_hbm, b, y_cm, sems, False)
            for cp in stores:
                cp.start()
            for cp in stores:
                cp.wait()

    out_tile = lambda b, ph, t: (b, t * ph, 0)
    full = lambda a: pl.BlockSpec(a.shape, lambda b, ph, t: (0,) * a.ndim)
    hbm = pl.BlockSpec(memory_space=pl.ANY)
    return pl.pallas_call(
        body,
        name="ssm_fwd",
        grid=(batch, 2, n_tiles),
        in_specs=[hbm, full(b_tab), full(c_tab), full(lam8), full(lcl), full(d_skip), full(w_glu)],
        out_specs=[
            hbm,
            pl.BlockSpec((None, rows, width), out_tile),
            pl.BlockSpec((None, rows, 2 * width), out_tile),
            pl.BlockSpec((None, SUBLANES, 2 * s), lambda b, ph, t: (b, 0, 0)),
        ],
        out_shape=[
            jax.ShapeDtypeStruct((batch, seq, width), F32),
            jax.ShapeDtypeStruct((batch, seq, width), F32),
            jax.ShapeDtypeStruct((batch, seq, 2 * width), F32),
            jax.ShapeDtypeStruct((batch, SUBLANES, 2 * s), F32),
        ],
        scratch_shapes=[
            pltpu.VMEM((cl, NCH, width), F32),
            pltpu.VMEM((cl, NCH, width), F32),
            pltpu.VMEM((rows, 2 * s), F32),
            pltpu.VMEM((SUBLANES, 2 * s), F32),
            pltpu.SemaphoreType.DMA((NCH,)),
        ],
        compiler_params=_params("arbitrary", "arbitrary", "arbitrary"),
    )(u, b_tab, c_tab, lam8, lcl, d_skip, w_glu)


def _ssm_bwd(u, pre_p, z_p, dy, init, b_tab, b_tab_t, c_tab_t, lam8, lcl, d_skip, w_glu):
    batch, seq, width = u.shape
    s = lam8.shape[1] // 2
    nb = b_tab.shape[0]
    sb = 2 * s // nb
    cl = seq // NCH
    rows = _ssm_rows(seq)
    n_tiles = seq // rows
    n_groups = rows // SUBLANES

    def body(u_hbm, pre_ref, z_ref, dy_hbm, init_ref, b_ref, bt_ref, ct_ref, lam_ref, lcl_ref, d_ref, wg_ref,
             du_hbm, gb_ref, gc_ref, gwg_ref, gd_ref, glam_ref,
             u_cm, dy_cm, work, hs_all, dpre_all, st, stg, initg, acc, sems):
        b, ph, t = pl.program_id(0), pl.program_id(1), pl.program_id(2)
        half = s // nb
        first = (b == 0) & (ph == 0) & (t == 0)
        last = (b == batch - 1) & (ph == 2) & (t == n_tiles - 1)
        tile = jnp.where(ph == 0, t, n_tiles - 1 - t)
        tile_rows = pl.ds(pl.multiple_of(tile * rows, rows), rows)
        tile_groups = pl.ds(pl.multiple_of(tile * n_groups, n_groups), n_groups)
        lanes = lambda blk: slice(blk * LANES, (blk + 1) * LANES)
        states = lambda blk: slice(blk * sb, (blk + 1) * sb)

        @pl.when(first)
        def _():
            acc[...] = jnp.zeros_like(acc)
            gb_ref[...] = jnp.zeros_like(gb_ref)
            gc_ref[...] = jnp.zeros_like(gc_ref)
            gwg_ref[...] = jnp.zeros_like(gwg_ref)
            gd_ref[...] = jnp.zeros_like(gd_ref)

        @pl.when((ph == 0) & (t == 0))
        def _():
            loads = _chunk_copies(u_hbm, b, u_cm, sems.at[0], True) + _chunk_copies(dy_hbm, b, dy_cm, sems.at[1], True)
            for cp in loads:
                cp.start()
            st[...] = init_ref[...]
            for blk in range(nb):
                entry = init_ref[:, states(blk)]
                hs_all[0:SUBLANES, blk * half : (blk + 1) * half] = _pack_state(entry[:, 0:half], entry[:, half : 2 * half])
            for cp in loads:
                cp.wait()

        u_t = u_cm[tile_groups].reshape(rows, width)
        u_b = u_t.astype(BF16)

        @pl.when(ph == 0)
        def _():
            for blk in range(nb):
                work[:, states(blk)] = _mm(u_b[:, lanes(blk)], b_ref[blk])
            _scan(work, work, st, lam_ref, n_groups, s, nb, False, True)
            z = z_ref[...]
            dy_t = dy_cm[tile_groups].reshape(rows, width)
            pre = pre_ref[...]
            z1, sig = z[:, 0:width], jax.nn.sigmoid(z[:, width : 2 * width])
            dz = jnp.concatenate([dy_t * sig, dy_t * z1 * sig * (1.0 - sig)], axis=1).astype(BF16)
            gwg_ref[...] += _mm_tn(_gelu(pre).astype(BF16), dz)
            dpre = _mm_nt(dz, wg_ref[...]) * _gelu_grad(pre)
            dpre_all[tile_rows, :] = dpre
            gd_ref[...] += jnp.sum(dpre * u_t, axis=0, keepdims=True)
            dpre_b = dpre.astype(BF16)
            kept = pl.ds(pl.multiple_of(tile * rows + SUBLANES, SUBLANES), rows)
            for blk in range(nb):
                hs = work[:, states(blk)]
                gc_ref[blk] += _mm_tn(hs.astype(BF16), dpre_b[:, lanes(blk)])
                hs_all[kept, blk * half : (blk + 1) * half] = _pack_state(hs[:, 0:half], hs[:, half : 2 * half])

        @pl.when(ph >= 1)
        def _():
            dpre_b = dpre_all[tile_rows, :].astype(BF16)
            for blk in range(nb):
                work[:, states(blk)] = _mm(dpre_b[:, lanes(blk)], ct_ref[blk])

        @pl.when(ph == 1)
        def _():
            @pl.when(t == 0)
            def _():
                stg[...] = jnp.zeros_like(stg)

            _scan(work, work, stg, lam_ref, n_groups, s, nb, True, False)

            @pl.when(t == n_tiles - 1)
            def _():
                _chunk_starts_adjoint(stg, initg, lcl_ref, s, nb)

        @pl.when(ph == 2)
        def _():
            @pl.when(t == 0)
            def _():
                stg[...] = initg[...]

            _scan_adjoint(work, hs_all, tile * n_groups, stg, acc, lam_ref, n_groups, s, nb)
            du = lambda blk: _mm(work[:, states(blk)].astype(BF16), bt_ref[blk])
            du_t = _blockwise(du, nb) + dpre_all[tile_rows, :] * d_ref[...]
            dy_cm[tile_groups] = du_t.reshape(n_groups, SUBLANES, width)
            for blk in range(nb):
                gb_ref[blk] += _mm_tn(u_b[:, lanes(blk)], work[:, states(blk)].astype(BF16))

            @pl.when(t == n_tiles - 1)
            def _():
                stores = _chunk_copies(du_hbm, b, dy_cm, sems.at[0], False)
                for cp in stores:
                    cp.start()
                for cp in stores:
                    cp.wait()

        @pl.when(last)
        def _():
            glam_ref[...] = jnp.sum(acc[...], axis=0, keepdims=True)

    def tile(b, ph, t):
        return (b, jnp.where(ph == 0, t, n_tiles - 1 - t), 0)

    full = lambda a: pl.BlockSpec(a.shape, lambda b, ph, t: (0,) * a.ndim)
    hbm = pl.BlockSpec(memory_space=pl.ANY)
    return pl.pallas_call(
        body,
        name="ssm_bwd",
        grid=(batch, 3, n_tiles),
        in_specs=[
            hbm,
            pl.BlockSpec((None, rows, width), tile),
            pl.BlockSpec((None, rows, 2 * width), tile),
            hbm,
            pl.BlockSpec((None, SUBLANES, 2 * s), lambda b, ph, t: (b, 0, 0)),
            full(b_tab), full(b_tab_t), full(c_tab_t), full(lam8), full(lcl), full(d_skip), full(w_glu),
        ],
        out_specs=[
            hbm,
            full(b_tab), full(b_tab_t), full(w_glu), full(d_skip),
            pl.BlockSpec((1, 2 * s), lambda b, ph, t: (0, 0)),
        ],
        out_shape=[
            jax.ShapeDtypeStruct((batch, seq, width), F32),
            jax.ShapeDtypeStruct(b_tab.shape, F32),
            jax.ShapeDtypeStruct(b_tab_t.shape, F32),
            jax.ShapeDtypeStruct(w_glu.shape, F32),
            jax.ShapeDtypeStruct(d_skip.shape, F32),
            jax.ShapeDtypeStruct((1, 2 * s), F32),
        ],
        scratch_shapes=[
            pltpu.VMEM((cl, NCH, width), F32),
            pltpu.VMEM((cl, NCH, width), F32),
            pltpu.VMEM((rows, 2 * s), F32),
            pltpu.VMEM((seq + SUBLANES, s), jnp.uint32),
            pltpu.VMEM((seq, width), F32),
        ]
        + [pltpu.VMEM((SUBLANES, 2 * s), F32)] * 4
        + [pltpu.SemaphoreType.DMA((2, NCH))],
        compiler_params=_params("arbitrary", "arbitrary", "arbitrary"),
    )(u, pre_p, z_p, dy, init, b_tab, b_tab_t, c_tab_t, lam8, lcl, d_skip, w_glu)


def _kv_fwd(mem, g_mem, w_kv):
    batch, n_mem, d = mem.shape
    kvw = w_kv.shape[1]

    def body(mem_ref, g_ref, w_ref, kv_ref):
        m = mem_ref[...]
        kv_ref[...] = _mm((m * _rms(m) * g_ref[...]).astype(BF16), w_ref[...])

    return pl.pallas_call(
        body,
        name="kv_fwd",
        grid=(batch,),
        in_specs=[
            pl.BlockSpec((None, n_mem, d), lambda b: (b, 0, 0)),
            pl.BlockSpec((1, d), lambda b: (0, 0)),
            pl.BlockSpec((d, kvw), lambda b: (0, 0)),
        ],
        out_specs=pl.BlockSpec((None, n_mem, kvw), lambda b: (b, 0, 0)),
        out_shape=jax.ShapeDtypeStruct((batch, n_mem, kvw), F32),
        compiler_params=_params("arbitrary"),
    )(mem, g_mem, w_kv)


def _kv_bwd(mem, dkv, g_mem, w_kv):
    batch, n_mem, d = mem.shape
    kvw = w_kv.shape[1]

    def body(mem_ref, dkv_ref, g_ref, w_ref, gw_ref, gg_ref):
        @pl.when(pl.program_id(0) == 0)
        def _():
            gw_ref[...] = jnp.zeros_like(gw_ref)
            gg_ref[...] = jnp.zeros_like(gg_ref)

        m = mem_ref[...]
        mn = m * _rms(m)
        dkv_b = dkv_ref[...].astype(BF16)
        gw_ref[...] += _mm_tn((mn * g_ref[...]).astype(BF16), dkv_b)
        gg_ref[...] += jnp.sum(_mm_nt(dkv_b, w_ref[...]) * mn, axis=0, keepdims=True)

    return pl.pallas_call(
        body,
        name="kv_bwd",
        grid=(batch,),
        in_specs=[
            pl.BlockSpec((None, n_mem, d), lambda b: (b, 0, 0)),
            pl.BlockSpec((None, n_mem, kvw), lambda b: (b, 0, 0)),
            pl.BlockSpec((1, d), lambda b: (0, 0)),
            pl.BlockSpec((d, kvw), lambda b: (0, 0)),
        ],
        out_specs=[pl.BlockSpec((d, kvw), lambda b: (0, 0)), pl.BlockSpec((1, d), lambda b: (0, 0))],
        out_shape=[jax.ShapeDtypeStruct((d, kvw), F32), jax.ShapeDtypeStruct((1, d), F32)],
        compiler_params=_params("arbitrary"),
    )(mem, dkv, g_mem, w_kv)


def _tail(x2, target2, gate, y_pool, y_ssm, q, kv, w_out, g_post):
    tokens, d = x2.shape
    pool_w, ssm_w, att_w, mix = y_pool.shape[1], y_ssm.shape[1], q.shape[1], gate.shape[1]
    batch, n_mem, kvw = kv.shape
    hd = att_w // MEM_HEADS
    cl = _token_tile(tokens // batch, 256)
    n_tiles = tokens // cl
    per_seq = tokens // batch // cl
    qk_scale = hd**-0.5

    def body(x_ref, tg_ref, gate_ref, yp_ref, ys_ref, q_ref, kv_ref, w_ref, g_ref,
             dres_ref, dgate_ref, dyp_ref, dys_ref, dq_ref, dkv_ref, gw_hbm, gg_ref, loss_ref, acc, sem):
        i = pl.program_id(0)

        @pl.when(i == 0)
        def _():
            acc[...] = jnp.zeros_like(acc)
            gg_ref[...] = jnp.zeros_like(gg_ref)
            loss_ref[...] = jnp.zeros_like(loss_ref)

        @pl.when(i % per_seq == 0)
        def _():
            dkv_ref[...] = jnp.zeros_like(dkv_ref)

        q = q_ref[...]
        k = kv_ref[:, 0:att_w].astype(BF16)
        v = kv_ref[:, att_w : 2 * att_w].astype(BF16)
        lane = lax.broadcasted_iota(jnp.int32, (1, att_w), 1)
        heads = [(lane >= h * hd) & (lane < (h + 1) * hd) for h in range(MEM_HEADS)]
        probs, q_heads = [], []
        att = jnp.zeros((cl, att_w), F32)
        for mask in heads:
            qh = jnp.where(mask, q, 0.0).astype(BF16)
            sc = _mm_nt(qh, k) * qk_scale
            e = jnp.exp(sc - jnp.max(sc, axis=-1, keepdims=True))
            p = e * (1.0 / jnp.sum(e, axis=-1, keepdims=True))
            att = att + jnp.where(mask, _mm(p.astype(BF16), v), 0.0)
            probs.append(p)
            q_heads.append(qh)

        ycat = jnp.concatenate([yp_ref[...], ys_ref[...], att], axis=1)
        gate = gate_ref[...]
        sig = jax.nn.sigmoid(gate)
        silu = gate * sig
        yg = (ycat * silu).astype(BF16)
        out = _mm(yg, w_ref[...])
        r = _rms(out)
        on = out * r
        g = g_ref[...]
        x = x_ref[...]
        err = x + on * g - tg_ref[...]
        loss_ref[...] += 0.5 * jnp.sum(jnp.mean(err * err, axis=-1, keepdims=True), axis=0, keepdims=True)
        dres = err * (1.0 / d)
        dres_ref[...] = dres
        gg_ref[...] += jnp.sum(dres * on, axis=0, keepdims=True)
        don = dres * g
        dout = (r * (don - on * jnp.mean(don * on, axis=-1, keepdims=True))).astype(BF16)
        acc[...] += _mm_tn(yg, dout)
        dyg = _mm_nt(dout, w_ref[...])
        dgate_ref[...] = dyg * ycat * (sig * (1.0 + gate * (1.0 - sig)))
        dycat = dyg * silu
        dyp_ref[...] = dycat[:, 0:pool_w]
        dys_ref[...] = dycat[:, pool_w : pool_w + ssm_w]
        datt = dycat[:, pool_w + ssm_w : mix]

        dq = jnp.zeros((cl, att_w), F32)
        dk = jnp.zeros((n_mem, att_w), F32)
        dv = jnp.zeros((n_mem, att_w), F32)
        for mask, p, qh in zip(heads, probs, q_heads):
            doh = jnp.where(mask, datt, 0.0).astype(BF16)
            dp = _mm_nt(doh, v)
            ds = (p * (dp - jnp.sum(p * dp, axis=-1, keepdims=True)) * qk_scale).astype(BF16)
            dq = dq + jnp.where(mask, _mm(ds, k), 0.0)
            dk = dk + _mm_tn(ds, qh)
            dv = dv + _mm_tn(p.astype(BF16), doh)
        dq_ref[...] = dq
        dkv_ref[:, 0:att_w] += dk
        dkv_ref[:, att_w : 2 * att_w] += dv

        @pl.when(i == n_tiles - 1)
        def _():
            cp = pltpu.make_async_copy(acc, gw_hbm, sem)
            cp.start()
            cp.wait()

    tok = lambda w: pl.BlockSpec((cl, w), lambda i: (i, 0))
    chunked = tok(ssm_w)
    per_batch = pl.BlockSpec((None, n_mem, kvw), lambda i: (i // per_seq, 0, 0))
    return pl.pallas_call(
        body,
        name="tail",
        grid=(n_tiles,),
        in_specs=[
            tok(d), tok(d), tok(mix), tok(pool_w), chunked, tok(att_w), per_batch,
            pl.BlockSpec((mix, d), lambda i: (0, 0)),
            pl.BlockSpec((1, d), lambda i: (0, 0)),
        ],
        out_specs=[
            tok(d), tok(mix), tok(pool_w), chunked, tok(att_w), per_batch,
            pl.BlockSpec(memory_space=pl.ANY),
            pl.BlockSpec((1, d), lambda i: (0, 0)),
            pl.BlockSpec((1, 1), lambda i: (0, 0)),
        ],
        out_shape=[
            jax.ShapeDtypeStruct((tokens, d), F32),
            jax.ShapeDtypeStruct((tokens, mix), F32),
            jax.ShapeDtypeStruct((tokens, pool_w), F32),
            jax.ShapeDtypeStruct((tokens, ssm_w), F32),
            jax.ShapeDtypeStruct((tokens, att_w), F32),
            jax.ShapeDtypeStruct(kv.shape, F32),
            jax.ShapeDtypeStruct((mix, d), F32),
            jax.ShapeDtypeStruct((1, d), F32),
            jax.ShapeDtypeStruct((1, 1), F32),
        ],
        scratch_shapes=[pltpu.VMEM((mix, d), F32), pltpu.SemaphoreType.DMA],
        compiler_params=_params("arbitrary"),
    )(x2, target2, gate, y_pool, y_ssm, q, kv, w_out, g_post)


def _pack(arrays):
    flat = jnp.concatenate([a.reshape(-1) for a in arrays])
    rows = -(-flat.size // (SUBLANES * LANES)) * SUBLANES
    return jnp.pad(flat, (0, rows * LANES - flat.size)).reshape(rows, LANES)


def _unpack(packed, like):
    flat, out, at = packed.reshape(-1), [], 0
    for a in like:
        out.append(flat[at : at + a.size].reshape(a.shape))
        at += a.size
    return out


def kernel(x, mem, g_pre, w_in, w_pool, pool_scale, a_re, a_im, log_dt, b_re, b_im, c_re, c_im, d_skip, w_glu, g_mem, w_kv, w_out, g_post, loss_target, m_g_pre, m_w_in, m_w_pool, m_pool_scale, m_a_re, m_a_im, m_log_dt, m_b_re, m_b_im, m_c_re, m_c_im, m_d_skip, m_w_glu, m_g_mem, m_w_kv, m_w_out, m_g_post, v_g_pre, v_w_in, v_w_pool, v_pool_scale, v_a_re, v_a_im, v_log_dt, v_b_re, v_b_im, v_c_re, v_c_im, v_d_skip, v_w_glu, v_g_mem, v_w_kv, v_w_out, v_g_post):
    batch, seq, d = x.shape
    cl = seq // NCH
    pool_w, ssm_w = pool_scale.shape[1], d_skip.shape[1]
    att_w = w_kv.shape[2] // 2
    tokens = batch * seq
    x2 = x.reshape(tokens, d)
    target2 = loss_target.reshape(tokens, d)

    (w_in_g,), shards = _gather_weights([w_in[0]], [w_out[0], w_kv[0], w_glu[0]])

    wp_blk = jax.scipy.linalg.block_diag(*w_pool[0]).astype(BF16)
    ssm_params = (a_re[0], a_im[0], log_dt[0], b_re[0], b_im[0], c_re[0], c_im[0])
    (lam_row, b_tab, c_tab), tables_vjp = jax.vjp(_ssm_tables, *ssm_params)
    nb = b_tab.shape[0]
    lam8 = jnp.broadcast_to(lam_row, (SUBLANES, lam_row.shape[1]))
    lcl = _lam_power(a_re[0], a_im[0], log_dt[0], float(cl), 1.0, nb)
    b_bf, c_bf = b_tab.astype(BF16), c_tab.astype(BF16)

    (u_pool, u_ssm, q, gate), (w_out_g, w_kv_g, w_glu_g) = _in_proj(x2, g_pre, w_in_g, shards, pool_w, ssm_w, att_w)
    w_out_f = w_out_g.reshape(N_DEV * w_out_g.shape[1], w_out_g.shape[2])
    w_kv_f = w_kv_g.reshape(N_DEV * w_kv_g.shape[1], w_kv_g.shape[2])
    w_glu_f = w_glu_g.transpose(1, 0, 2).reshape(w_glu_g.shape[1], N_DEV * w_glu_g.shape[2])
    y_pool = _pool_fwd(u_pool, wp_blk, pool_scale, batch, seq)
    u_ssm = u_ssm.reshape(batch, seq, ssm_w)
    y_ssm, pre_ssm, z_ssm, init_ssm = _ssm_fwd(u_ssm, b_bf, c_bf, lam8, lcl, d_skip, w_glu_f)
    kv = _kv_fwd(mem, g_mem, w_kv_f)

    dres, dgate, dy_pool, dy_ssm, dq, dkv, gw_out, gg_post, loss_part = _tail(
        x2, target2, gate, y_pool, y_ssm.reshape(tokens, ssm_w), q, kv, w_out_f, g_post)

    gw_kv, gg_mem = _kv_bwd(mem, dkv, g_mem, w_kv_f)
    du_pool, gwp_dense, g_scale = _pool_bwd(u_pool, dy_pool, wp_blk, pool_scale, batch, seq)
    du_ssm, gb_tab, gc_tab, gw_glu, gd_skip, glam = _ssm_bwd(
        u_ssm, pre_ssm, z_ssm, dy_ssm.reshape(batch, seq, ssm_w), init_ssm, b_bf, b_bf.transpose(0, 2, 1),
        c_bf.transpose(0, 2, 1), lam8, lcl, d_skip, w_glu_f)
    grad_x2, gw_in, gg_pre = _in_proj_bwd(
        x2, dres, du_pool, du_ssm.reshape(tokens, ssm_w), dq, dgate, g_pre, w_in_g)

    gw = pool_w // len(POOL_WINDOWS)
    gw_pool = jnp.stack([gwp_dense[i * gw : (i + 1) * gw, i * gw : (i + 1) * gw] for i in range(len(POOL_WINDOWS))])
    g_ssm = tables_vjp((glam, gb_tab, gc_tab))

    small_w = [g_pre, w_pool, pool_scale, a_re, a_im, log_dt, b_re, b_im, c_re, c_im, d_skip, g_mem, g_post]
    small_m = [m_g_pre, m_w_pool, m_pool_scale, m_a_re, m_a_im, m_log_dt, m_b_re, m_b_im, m_c_re, m_c_im, m_d_skip, m_g_mem, m_g_post]
    small_v = [v_g_pre, v_w_pool, v_pool_scale, v_a_re, v_a_im, v_log_dt, v_b_re, v_b_im, v_c_re, v_c_im, v_d_skip, v_g_mem, v_g_post]
    small_g = [gg_pre, gw_pool, g_scale, *g_ssm, gd_skip, gg_mem, gg_post]
    big, small_sum = _reduce_all(
        [gw_in, gw_glu.reshape(ssm_w, N_DEV, -1).transpose(1, 0, 2), gw_kv.reshape(N_DEV, -1, gw_kv.shape[1]),
         gw_out.reshape(N_DEV, -1, gw_out.shape[1])],
        [w_in[0], w_glu[0], w_kv[0], w_out[0]],
        [m_w_in[0], m_w_glu[0], m_w_kv[0], m_w_out[0]],
        [v_w_in[0], v_w_glu[0], v_w_kv[0], v_w_out[0]],
        _pack(small_g + [loss_part]))
    big = {name: tuple(t[None] for t in res) for name, res in zip(["w_in", "w_glu", "w_kv", "w_out"], big)}

    flat2 = lambda a: a.reshape(-1, a.shape[-1])
    sg = _unpack(small_sum, [flat2(a) for a in small_w] + [loss_part])
    loss = sg[-1].reshape(())
    updates = _adamw_small(sg[:-1], [flat2(a) for a in small_w], [flat2(a) for a in small_m], [flat2(a) for a in small_v])
    sg = [g.reshape(a.shape) for g, a in zip(sg[:-1], small_w)]
    sd, sm, sv = ([u[kind].reshape(a.shape) for u, a in zip(updates, small_w)] for kind in range(3))

    order = ["g_pre", "w_in", "w_pool", "pool_scale", "a_re", "a_im", "log_dt", "b_re", "b_im", "c_re", "c_im",
             "d_skip", "w_glu", "g_mem", "w_kv", "w_out", "g_post"]
    small_names = ["g_pre", "w_pool", "pool_scale", "a_re", "a_im", "log_dt", "b_re", "b_im", "c_re", "c_im",
                   "d_skip", "g_mem", "g_post"]
    outs = [[], [], [], []]
    for name in order:
        if name in big:
            parts = big[name]
        else:
            j = small_names.index(name)
            parts = (sg[j], sd[j], sm[j], sv[j])
        for kind in range(4):
            outs[kind].append(parts[kind])
    return (loss, grad_x2.reshape(batch, seq, d), *outs[0], *outs[1], *outs[2], *outs[3])
```

```python
import functools
import math

import jax
import jax.numpy as jnp
from jax import lax
from jax.experimental import pallas as pl
from jax.experimental.pallas import tpu as pltpu

F32 = jnp.float32
BF16 = jnp.bfloat16
MESH = pl.DeviceIdType.MESH

N_DEV = 8
NCH = 8
SUBLANES = 8
LANES = 128
VMEM_LIMIT = 56 * 1024 * 1024

EPS = 1e-6
POOL_WINDOWS = (2, 4, 8, 16)
MEM_HEADS = 4
SSM_GROUP = 16
SSM_N = 64
ADAM_LR, ADAM_B1, ADAM_B2, ADAM_EPS, ADAM_WD, ADAM_STEP = 0.001, 0.9, 0.999, 1e-08, 0.01, 10


def _mm(a, b):
    return jnp.dot(a, b, preferred_element_type=F32)


def _mm_nt(a, b):
    return lax.dot_general(a, b, (((1,), (1,)), ((), ())), preferred_element_type=F32)


def _mm_tn(a, b):
    return lax.dot_general(a, b, (((0,), (0,)), ((), ())), preferred_element_type=F32)


def _params(*sem):
    return pltpu.CompilerParams(dimension_semantics=sem or None, vmem_limit_bytes=VMEM_LIMIT)


def _adamw(w, g, m, v):
    m = ADAM_B1 * m + (1.0 - ADAM_B1) * g
    v = ADAM_B2 * v + (1.0 - ADAM_B2) * (g * g)
    m_hat = m / (1.0 - ADAM_B1**ADAM_STEP)
    v_hat = v / (1.0 - ADAM_B2**ADAM_STEP)
    delta = -ADAM_LR * (m_hat / (jnp.sqrt(v_hat) + ADAM_EPS) + ADAM_WD * w)
    return delta, m, v


def _gelu(x):
    k = math.sqrt(2.0 / math.pi)
    return 0.5 * x * (1.0 + jnp.tanh(k * (x + 0.044715 * x * x * x)))


def _gelu_grad(x):
    k = math.sqrt(2.0 / math.pi)
    th = jnp.tanh(k * (x + 0.044715 * x * x * x))
    return 0.5 * (1.0 + th) + 0.5 * x * (1.0 - th * th) * (k * (1.0 + 3.0 * 0.044715 * x * x))


def _place():
    return lax.axis_index("x"), lax.axis_index("y"), lax.axis_index("c")


def _gather_steps(ins, outs, send_sems, recv_sems):
    n = len(ins)
    x, y, c = _place()
    me, sibling = (x, y, c), (x, y, 1 - c)
    chips = [(1 - x, y), (x, 1 - y), (1 - x, 1 - y)]
    sent = []

    def slot(px, py, pc):
        return 4 * px + 2 * py + pc

    def copy(a, k, block, to):
        ref = outs[a].at[slot(*block)]
        return pltpu.make_async_remote_copy(
            src_ref=ref, dst_ref=ref, send_sem=send_sems.at[a, k], recv_sem=recv_sems.at[a, k],
            device_id=to, device_id_type=MESH)

    def start():
        for a in range(n):
            outs[a][slot(*me)] = ins[a][...].astype(outs[a].dtype)
        for a in range(n):
            sent.append(copy(a, 0, me, sibling))
            sent.extend(copy(a, 1 + j, me, (*chip, c)) for j, chip in enumerate(chips))
        for cp in sent:
            cp.start()

    def forward():
        for j, chip in enumerate(chips):
            for a in range(n):
                copy(a, 1 + j, (*chip, c), me).wait_recv()
                cp = copy(a, 4 + j, (*chip, c), sibling)
                cp.start()
                sent.append(cp)

    def finish():
        for a in range(n):
            copy(a, 0, sibling, me).wait_recv()
            for j, chip in enumerate(chips):
                copy(a, 4 + j, (*chip, 1 - c), me).wait_recv()
        for cp in sent:
            cp.wait_send()

    return start, forward, finish


def _exchange_steps(n, src_of, landing, send_sems, recv_sems):
    x, y, c = _place()
    me = 4 * x + 2 * y + c
    peers = []
    for j in range(1, N_DEV):
        px = 1 - x if j & 4 else x
        py = 1 - y if j & 2 else y
        pc = 1 - c if j & 1 else c
        peers.append((px, py, pc))

    def copy(a, j, from_slot, to_slot, peer):
        return pltpu.make_async_remote_copy(
            src_ref=src_of(a, to_slot), dst_ref=landing[a].at[from_slot],
            send_sem=send_sems.at[a, j], recv_sem=recv_sems.at[a, j], device_id=peer, device_id_type=MESH)

    def start():
        for a in range(n):
            for j, p in enumerate(peers):
                copy(a, j, me, 4 * p[0] + 2 * p[1] + p[2], p).start()

    def finish():
        for a in range(n):
            for j, p in enumerate(peers):
                slot = 4 * p[0] + 2 * p[1] + p[2]
                copy(a, j, slot, slot, p).wait_recv()
        for a in range(n):
            for j, p in enumerate(peers):
                copy(a, j, me, 4 * p[0] + 2 * p[1] + p[2], p).wait_send()

    return start, finish


def _ordered_sum(gathered, out_ref):
    rows = out_ref.shape[0]

    def step(i, _):
        r = pl.ds(pl.multiple_of(i * SUBLANES, SUBLANES), SUBLANES)
        g = gathered[0, r, :]
        for d in range(1, N_DEV):
            g = g + gathered[d, r, :]
        out_ref[r, :] = g
        return 0

    lax.fori_loop(0, rows // SUBLANES, step, 0)


def _gather_weights(blocks, later):
    n, k = len(blocks), len(later)

    def body(*refs):
        refs = list(refs)
        take = lambda cnt: [refs.pop(0) for _ in range(cnt)]
        ins, later_in, outs, later_out = take(n), take(k), take(n), take(k)
        start, forward, finish = _gather_steps(ins, outs, *refs)
        start()
        for src, dst in zip(later_in, later_out):
            dst[...] = src[...].astype(BF16)
        forward()
        finish()

    vmem = pl.BlockSpec(memory_space=pltpu.VMEM)
    res = pl.pallas_call(
        body,
        name="gather_weights",
        out_shape=[jax.ShapeDtypeStruct((N_DEV, *b.shape), BF16) for b in blocks]
        + [jax.ShapeDtypeStruct(b.shape, BF16) for b in later],
        in_specs=[vmem] * (n + k),
        out_specs=[vmem] * (n + k),
        scratch_shapes=[pltpu.SemaphoreType.DMA((n, 7)), pltpu.SemaphoreType.DMA((n, 7))],
        compiler_params=_params(),
    )(*blocks, *later)
    return res[:n], res[n:]


def _adamw_small(gs, ws, ms, vs):
    n = len(gs)

    def body(*refs):
        g, w, m, v = refs[:n], refs[n : 2 * n], refs[2 * n : 3 * n], refs[3 * n : 4 * n]
        outs = refs[4 * n :]
        for a in range(n):
            delta, nm, nv = _adamw(w[a][...], g[a][...], m[a][...], v[a][...])
            outs[3 * a][...] = delta
            outs[3 * a + 1][...] = nm
            outs[3 * a + 2][...] = nv

    vmem = pl.BlockSpec(memory_space=pltpu.VMEM)
    out_shape = []
    for wa in ws:
        out_shape += [jax.ShapeDtypeStruct(wa.shape, F32)] * 3
    res = pl.pallas_call(
        body,
        name="adamw_small",
        out_shape=out_shape,
        in_specs=[vmem] * (4 * n),
        out_specs=[vmem] * (3 * n),
        compiler_params=_params(),
    )(*gs, *ws, *ms, *vs)
    return [tuple(res[3 * a : 3 * a + 3]) for a in range(n)]


def _reduce_all(parts, ws, ms, vs, small, early):
    n, ne = len(parts), len(early)
    parts4 = [p.reshape(4, 2, *p.shape[1:]) for p in parts]
    blks = [p.shape[1:] for p in parts]

    def body(*refs):
        refs = list(refs)
        take = lambda k: [refs.pop(0) for _ in range(k)]
        part, w_in, m_in, v_in = take(n), take(n), take(n), take(n)
        (small_ref,) = take(1)
        early_in = [take(5) for _ in range(ne)]
        outs = take(4 * n)
        small_all, small_sum = take(2)
        early_out = [take(4) for _ in range(ne)]
        own, r1, r2 = take(n), take(n), take(n)
        early_buf = take(ne)
        s1_send, s1_recv, s2_send, s2_recv, loc, small_send, small_recv, early_sems = refs
        small_start, small_forward, small_finish = _gather_steps([small_ref], [small_all], small_send, small_recv)
        x, y, c = _place()
        me = 4 * x + 2 * y + c
        landed = [pltpu.make_async_copy(early_in[e][1], early_buf[e], early_sems.at[e, 0]) for e in range(ne)]
        for cp in landed:
            cp.start()
        sibling = (x, y, 1 - c)
        chips = [(1 - x, y), (x, 1 - y), (1 - x, 1 - y)]

        def rowwise(rows, fn):
            chunk = math.gcd(rows, 128)

            def step(i, _):
                fn(pl.ds(pl.multiple_of(i * chunk, chunk), chunk))
                return 0

            lax.fori_loop(0, rows // chunk, step, 0)

        stage1, local = [], []
        for a in range(n):
            cp = pltpu.make_async_remote_copy(
                src_ref=part[a].at[:, 1 - c], dst_ref=r1[a], send_sem=s1_send.at[a], recv_sem=s1_recv.at[a],
                device_id=sibling, device_id_type=MESH)
            cp.start()
            stage1.append(cp)
            lc = pltpu.make_async_copy(part[a].at[:, c], own[a], loc.at[a])
            lc.start()
            local.append(lc)
        small_start()
        stage2 = []
        for a in range(n):
            local[a].wait()
            stage1[a].wait_recv()
            for chip in range(4):

                def add(r, a=a, chip=chip):
                    own[a][chip, r, :] = own[a][chip, r, :] + r1[a][chip, r, :]

                rowwise(blks[a][0], add)
            for k, chip in enumerate(chips):
                cp = pltpu.make_async_remote_copy(
                    src_ref=own[a].at[2 * chip[0] + chip[1]], dst_ref=r2[a].at[k],
                    send_sem=s2_send.at[a, k], recv_sem=s2_recv.at[a, k],
                    device_id=(*chip, c), device_id_type=MESH)
                cp.start()
                stage2.append(cp)
        small_forward()
        for e in range(ne):
            part_e, _, w_e, m_e, v_e = early_in[e]
            g_ref, d_ref, nm_ref, nv_ref = early_out[e]
            landed[e].wait()
            mine = pltpu.make_async_copy(part_e.at[me], early_buf[e].at[me], early_sems.at[e, 1])
            mine.start()
            mine.wait()

            def update_early(r, e=e, w_e=w_e, m_e=m_e, v_e=v_e, g_ref=g_ref, d_ref=d_ref, nm_ref=nm_ref, nv_ref=nv_ref):
                g = early_buf[e][0, r, :]
                for dev in range(1, N_DEV):
                    g = g + early_buf[e][dev, r, :]
                delta, nm, nv = _adamw(w_e[r, :], g, m_e[r, :], v_e[r, :])
                g_ref[r, :] = g
                d_ref[r, :] = delta
                nm_ref[r, :] = nm
                nv_ref[r, :] = nv

            rowwise(early_buf[e].shape[1], update_early)
        for a in range(n):
            for k, chip in enumerate(chips):
                stage2[3 * a + k].wait_recv()
            g_ref, d_ref, nm_ref, nv_ref = outs[4 * a : 4 * a + 4]

            def update(r, a=a, g_ref=g_ref, d_ref=d_ref, nm_ref=nm_ref, nv_ref=nv_ref):
                g = own[a][2 * x + y, r, :] + r2[a][0, r, :] + r2[a][1, r, :] + r2[a][2, r, :]
                delta, nm, nv = _adamw(w_in[a][r, :], g, m_in[a][r, :], v_in[a][r, :])
                g_ref[r, :] = g
                d_ref[r, :] = delta
                nm_ref[r, :] = nm
                nv_ref[r, :] = nv

            rowwise(blks[a][0], update)
        small_finish()
        _ordered_sum(small_all, small_sum)
        for cp in stage1 + stage2:
            cp.wait_send()

    vmem = pl.BlockSpec(memory_space=pltpu.VMEM)
    hbm = pl.BlockSpec(memory_space=pl.ANY)
    out_shape = []
    for b in blks:
        out_shape += [jax.ShapeDtypeStruct(b, F32)] * 4
    out_shape += [jax.ShapeDtypeStruct((N_DEV, *small.shape), F32), jax.ShapeDtypeStruct(small.shape, F32)]
    for e in early:
        out_shape += [jax.ShapeDtypeStruct(e[2].shape, F32)] * 4
    scratch = (
        [pltpu.VMEM((4, *b), F32) for b in blks]
        + [pltpu.VMEM((4, *b), F32) for b in blks]
        + [pltpu.VMEM((3, *b), F32) for b in blks]
        + [pltpu.VMEM(e[0].shape, F32) for e in early]
        + [pltpu.SemaphoreType.DMA((n,)), pltpu.SemaphoreType.DMA((n,)), pltpu.SemaphoreType.DMA((n, 3)),
           pltpu.SemaphoreType.DMA((n, 3)), pltpu.SemaphoreType.DMA((n,)),
           pltpu.SemaphoreType.DMA((1, 7)), pltpu.SemaphoreType.DMA((1, 7)), pltpu.SemaphoreType.DMA((ne, 2))]
    )
    res = pl.pallas_call(
        body,
        name="reduce_all",
        out_shape=out_shape,
        in_specs=[hbm] * n + [vmem] * (3 * n + 1) + [hbm, hbm, vmem, vmem, vmem] * ne,
        out_specs=[vmem] * (4 * n + 2 + 4 * ne),
        scratch_shapes=scratch,
        compiler_params=_params(),
    )(*parts4, *ws, *ms, *vs, small, *[t for e in early for t in e])
    late = [tuple(res[4 * a : 4 * a + 4]) for a in range(n)]
    at = 4 * n + 2
    return late + [tuple(res[at + 4 * e : at + 4 * e + 4]) for e in range(ne)], res[4 * n + 1]


def _rms(x):
    return lax.rsqrt(jnp.mean(x * x, axis=-1, keepdims=True) + EPS)


def _token_tile(tokens, want):
    tile = min(want, tokens // 2)
    assert tokens % tile == 0 and tile % 16 == 0
    return tile


def _in_proj(x2, g_pre, w_in_g, shards, pool_w, ssm_w, att_w):
    tokens, d = x2.shape
    nb = w_in_g.shape[2]
    mix = pool_w + ssm_w + att_w
    half = N_DEV // 2
    cl = _token_tile(tokens, 512)
    n_tiles = tokens // cl
    ns = len(shards)
    assert half * nb == mix and nb == 256 and pool_w == 384 and ssm_w == 384 and att_w == 256

    def body(x_ref, g_ref, w_ref, *rest):
        shard_hbm, (up_ref, us_ref, q_ref, gate_ref) = rest[:ns], rest[ns : ns + 4]
        gathered = rest[ns + 4 : 2 * ns + 4]
        send_sems, recv_sems, own_sems = rest[2 * ns + 4 :]
        i = pl.program_id(0)
        x_pos, y_pos, c_pos = _place()
        own = [pltpu.make_async_copy(shard_hbm[a], gathered[a].at[4 * x_pos + 2 * y_pos + c_pos], own_sems.at[a])
               for a in range(ns)]
        start, finish = _exchange_steps(ns, lambda a, slot: shard_hbm[a], gathered, send_sems, recv_sems)

        @pl.when(i == 0)
        def _():
            start()
            for cp in own:
                cp.start()

        @pl.when(i == n_tiles - 1)
        def _():
            finish()
            for cp in own:
                cp.wait()

        x = x_ref[...]
        h = (x * _rms(x) * g_ref[...]).astype(BF16)
        p = [_mm(h, w_ref[j]) for j in range(half)]
        up_ref[:, 0:256] = p[0]
        up_ref[:, 256:384] = p[1][:, 0:128]
        us_ref[:, 0:128] = p[1][:, 128:256]
        us_ref[:, 128:384] = p[2]
        q_ref[...] = p[3]
        for j in range(half):
            gate_ref[:, j * nb : (j + 1) * nb] = _mm(h, w_ref[half + j])

    hbm = pl.BlockSpec(memory_space=pl.ANY)
    res = pl.pallas_call(
        body,
        name="in_proj",
        grid=(n_tiles,),
        in_specs=[
            pl.BlockSpec((cl, d), lambda i: (i, 0)),
            pl.BlockSpec((1, d), lambda i: (0, 0)),
            pl.BlockSpec((N_DEV, d, nb), lambda i: (0, 0, 0)),
        ]
        + [hbm] * ns,
        out_specs=[
            pl.BlockSpec((cl, pool_w), lambda i: (i, 0)),
            pl.BlockSpec((cl, ssm_w), lambda i: (i, 0)),
            pl.BlockSpec((cl, att_w), lambda i: (i, 0)),
            pl.BlockSpec((cl, mix), lambda i: (i, 0)),
        ]
        + [hbm] * ns,
        out_shape=[
            jax.ShapeDtypeStruct((tokens, pool_w), F32),
            jax.ShapeDtypeStruct((tokens, ssm_w), F32),
            jax.ShapeDtypeStruct((tokens, att_w), F32),
            jax.ShapeDtypeStruct((tokens, mix), F32),
        ]
        + [jax.ShapeDtypeStruct((N_DEV, *a.shape), a.dtype) for a in shards],
        scratch_shapes=[pltpu.SemaphoreType.DMA((ns, 7)), pltpu.SemaphoreType.DMA((ns, 7)),
                        pltpu.SemaphoreType.DMA((ns,))],
        compiler_params=_params("arbitrary"),
    )(x2, g_pre, w_in_g, *shards)
    return res[:4], res[4:]


def _in_proj_bwd(x2, dres, du_pool, du_ssm, dq, dgate, g_pre, w_in_g):
    tokens, d = x2.shape
    nb = w_in_g.shape[2]
    pool_w, ssm_w, att_w, mix = du_pool.shape[1], du_ssm.shape[1], dq.shape[1], dgate.shape[1]
    half = N_DEV // 2
    cl = _token_tile(tokens, 512)
    n_tiles = tokens // cl

    def body(x_ref, dres_ref, dup_ref, dus_ref, dq_ref, dgate_ref, g_ref, w_ref, gx_ref, gw_hbm, gg_ref, acc, sem):
        i = pl.program_id(0)

        @pl.when(i == 0)
        def _():
            acc[...] = jnp.zeros_like(acc)
            gg_ref[...] = jnp.zeros_like(gg_ref)

        x = x_ref[...]
        r = _rms(x)
        xn = x * r
        g = g_ref[...]
        h = (xn * g).astype(BF16)
        dval = jnp.concatenate([dup_ref[...], dus_ref[...], dq_ref[...]], axis=1)
        dh = jnp.zeros((cl, d), F32)
        for j in range(N_DEV):
            src = dval if j < half else dgate_ref[...]
            jj = j % half
            dp = src[:, jj * nb : (jj + 1) * nb].astype(BF16)
            dh = dh + _mm_nt(dp, w_ref[j])
            acc[j] += _mm_tn(h, dp)
        gg_ref[...] += jnp.sum(dh * xn, axis=0, keepdims=True)
        dxn = dh * g
        gx_ref[...] = dres_ref[...] + r * (dxn - xn * jnp.mean(dxn * xn, axis=-1, keepdims=True))

        @pl.when(i == n_tiles - 1)
        def _():
            cp = pltpu.make_async_copy(acc, gw_hbm, sem)
            cp.start()
            cp.wait()

    return pl.pallas_call(
        body,
        name="in_proj_bwd",
        grid=(n_tiles,),
        in_specs=[
            pl.BlockSpec((cl, d), lambda i: (i, 0)),
            pl.BlockSpec((cl, d), lambda i: (i, 0)),
            pl.BlockSpec((cl, pool_w), lambda i: (i, 0)),
            pl.BlockSpec((cl, ssm_w), lambda i: (i, 0)),
            pl.BlockSpec((cl, att_w), lambda i: (i, 0)),
            pl.BlockSpec((cl, mix), lambda i: (i, 0)),
            pl.BlockSpec((1, d), lambda i: (0, 0)),
            pl.BlockSpec((N_DEV, d, nb), lambda i: (0, 0, 0)),
        ],
        out_specs=[
            pl.BlockSpec((cl, d), lambda i: (i, 0)),
            pl.BlockSpec(memory_space=pl.ANY),
            pl.BlockSpec((1, d), lambda i: (0, 0)),
        ],
        out_shape=[
            jax.ShapeDtypeStruct((tokens, d), F32),
            jax.ShapeDtypeStruct((N_DEV, d, nb), F32),
            jax.ShapeDtypeStruct((1, d), F32),
        ],
        scratch_shapes=[pltpu.VMEM((N_DEV, d, nb), F32), pltpu.SemaphoreType.DMA],
        compiler_params=_params("arbitrary"),
    )(x2, dres, du_pool, du_ssm, dq, dgate, g_pre, w_in_g)


def _pool_geometry(seq, width):
    gw = width // len(POOL_WINDOWS)
    col = lax.broadcasted_iota(jnp.int32, (1, width), 1)
    win = jnp.full((1, width), float(POOL_WINDOWS[-1]), F32)
    for gi in range(len(POOL_WINDOWS) - 2, -1, -1):
        win = jnp.where(col < (gi + 1) * gw, float(POOL_WINDOWS[gi]), win)
    row = lax.broadcasted_iota(jnp.int32, (seq, width), 0)
    cnt = jnp.minimum((row + 1).astype(F32), win)
    return win, row, cnt


def _window_sums(a, row, win, seq, back):
    sums = []
    s, k = a, 1
    while k < POOL_WINDOWS[-1]:
        if back:
            shifted = jnp.where(row < seq - k, pltpu.roll(s, seq - k, 0), 0.0)
        else:
            shifted = jnp.where(row >= k, pltpu.roll(s, k, 0), 0.0)
        s = s + shifted
        k *= 2
        sums.append((k, s))
    out = sums[-1][1]
    for k, s in reversed(sums[:-1]):
        out = jnp.where(win <= float(k), s, out)
    return out


def _pool_fwd(u2, wp_blk, scale, batch, seq):
    width = u2.shape[1]

    def body(u_ref, w_ref, s_ref, y_ref):
        u = u_ref[...]
        win, row, cnt = _pool_geometry(seq, width)
        diff = _window_sums(u, row, win, seq, False) / cnt - u
        y_ref[...] = _mm(diff.astype(BF16), w_ref[...]) * s_ref[...]

    return pl.pallas_call(
        body,
        name="pool_fwd",
        grid=(batch,),
        in_specs=[
            pl.BlockSpec((seq, width), lambda b: (b, 0)),
            pl.BlockSpec((width, width), lambda b: (0, 0)),
            pl.BlockSpec((1, width), lambda b: (0, 0)),
        ],
        out_specs=pl.BlockSpec((seq, width), lambda b: (b, 0)),
        out_shape=jax.ShapeDtypeStruct(u2.shape, F32),
        compiler_params=_params("arbitrary"),
    )(u2, wp_blk, scale)


def _pool_bwd(u2, dy2, wp_blk, scale, batch, seq):
    width = u2.shape[1]

    def body(u_ref, dy_ref, w_ref, s_ref, du_ref, gw_ref, gs_ref):
        @pl.when(pl.program_id(0) == 0)
        def _():
            gw_ref[...] = jnp.zeros_like(gw_ref)
            gs_ref[...] = jnp.zeros_like(gs_ref)

        u = u_ref[...]
        dy = dy_ref[...]
        win, row, cnt = _pool_geometry(seq, width)
        diff = (_window_sums(u, row, win, seq, False) / cnt - u).astype(BF16)
        gs_ref[...] += jnp.sum(dy * _mm(diff, w_ref[...]), axis=0, keepdims=True)
        dys = (dy * s_ref[...]).astype(BF16)
        gw_ref[...] += _mm_tn(diff, dys)
        dd = _mm_nt(dys, w_ref[...])
        du_ref[...] = _window_sums(dd / cnt, row, win, seq, True) - dd

    return pl.pallas_call(
        body,
        name="pool_bwd",
        grid=(batch,),
        in_specs=[
            pl.BlockSpec((seq, width), lambda b: (b, 0)),
            pl.BlockSpec((seq, width), lambda b: (b, 0)),
            pl.BlockSpec((width, width), lambda b: (0, 0)),
            pl.BlockSpec((1, width), lambda b: (0, 0)),
        ],
        out_specs=[
            pl.BlockSpec((seq, width), lambda b: (b, 0)),
            pl.BlockSpec((width, width), lambda b: (0, 0)),
            pl.BlockSpec((1, width), lambda b: (0, 0)),
        ],
        out_shape=[
            jax.ShapeDtypeStruct(u2.shape, F32),
            jax.ShapeDtypeStruct((width, width), F32),
            jax.ShapeDtypeStruct((1, width), F32),
        ],
        compiler_params=_params("arbitrary"),
    )(u2, dy2, wp_blk, scale)


def _state_row(z, n_blocks):
    re = jnp.real(z).reshape(n_blocks, -1)
    im = jnp.imag(z).reshape(n_blocks, -1)
    return jnp.concatenate([re, im], axis=1).reshape(1, -1)


def _ssm_tables(a_re, a_im, log_dt, b_re, b_im, c_re, c_im):
    groups, n_state = a_re.shape
    ch = b_re.shape[2]
    nb = groups * ch // LANES
    gl = groups // nb
    lam = lax.complex(a_re, a_im)
    lam_bar = jnp.exp(lam * jnp.exp(log_dt)[:, None])
    b_bar = ((lam_bar - 1.0) / lam)[..., None] * lax.complex(b_re, b_im)
    eye = jnp.eye(gl, dtype=F32)

    def rows_to_state(t):
        return jnp.einsum("sgnc,gh->sgchn", t.reshape(nb, gl, n_state, ch), eye).reshape(nb, gl * ch, gl * n_state)

    def state_to_rows(t):
        return jnp.einsum("sgcn,gh->shngc", t.reshape(nb, gl, ch, n_state), eye).reshape(nb, gl * n_state, gl * ch)

    b_tab = jnp.concatenate([rows_to_state(jnp.real(b_bar)), rows_to_state(jnp.imag(b_bar))], axis=2)
    c_tab = jnp.concatenate([state_to_rows(c_re), -state_to_rows(c_im)], axis=1)
    return _state_row(lam_bar, nb), b_tab, c_tab


def _lam_power(a_re, a_im, log_dt, power, scale, n_blocks):
    return _state_row(scale * jnp.exp(lax.complex(a_re, a_im) * jnp.exp(log_dt)[:, None] * power), n_blocks)


def _state_blocks(s2, n_blocks, width):
    half = s2 // n_blocks // 2
    assert half % width == 0
    return [(b * 2 * half + o, b * 2 * half + half + o) for b in range(n_blocks) for o in range(0, half, width)]


def _scan(src_ref, dst_ref, st_ref, lam8_ref, n_groups, s, n_blocks, reverse, store):
    lb = 512
    for re0, im0 in _state_blocks(2 * s, n_blocks, lb):
        cr, ci = pl.ds(re0, lb), pl.ds(im0, lb)
        lr = lam8_ref[:, cr]
        li = -lam8_ref[:, ci] if reverse else lam8_ref[:, ci]

        def step(i, carry, cr=cr, ci=ci, lr=lr, li=li):
            hr, hi = carry
            grp = n_groups - 1 - i if reverse else i
            rows = pl.ds(pl.multiple_of(grp * SUBLANES, SUBLANES), SUBLANES)
            nr = lr * hr - li * hi + src_ref[rows, cr]
            ni = lr * hi + li * hr + src_ref[rows, ci]
            if store:
                dst_ref[rows, cr] = nr
                dst_ref[rows, ci] = ni
            return nr, ni

        hr, hi = lax.fori_loop(0, n_groups, step, (st_ref[:, cr], st_ref[:, ci]), unroll=2)
        st_ref[:, cr] = hr
        st_ref[:, ci] = hi


def _pack_state(re, im):
    hi = lax.bitcast_convert_type(re.astype(BF16).astype(F32), jnp.uint32)
    lo = lax.bitcast_convert_type(im.astype(BF16).astype(F32), jnp.uint32)
    return hi | (lo >> 16)


def _unpack_state(word):
    re = lax.bitcast_convert_type(word & jnp.uint32(0xFFFF0000), F32)
    im = lax.bitcast_convert_type(word << 16, F32)
    return re, im


def _scan_adjoint(dh_ref, hprev_ref, group0, stg_ref, acc_ref, lam8_ref, n_groups, s, n_blocks):
    lb = 512
    half = s // n_blocks
    for re0, im0 in _state_blocks(2 * s, n_blocks, lb):
        cr, ci = pl.ds(re0, lb), pl.ds(im0, lb)
        ch = pl.ds(re0 // (2 * half) * half + re0 % (2 * half), lb)
        lr, li = lam8_ref[:, cr], -lam8_ref[:, ci]

        def step(i, carry, cr=cr, ci=ci, ch=ch, lr=lr, li=li):
            gr, gi, ar, ai = carry
            grp = n_groups - 1 - i
            rows = pl.ds(pl.multiple_of(grp * SUBLANES, SUBLANES), SUBLANES)
            ngr = lr * gr - li * gi + dh_ref[rows, cr]
            ngi = lr * gi + li * gr + dh_ref[rows, ci]
            hr, hi = _unpack_state(hprev_ref[pl.ds(pl.multiple_of((group0 + grp) * SUBLANES, SUBLANES), SUBLANES), ch])
            ar = ar + hr * ngr + hi * ngi
            ai = ai + hr * ngi - hi * ngr
            dh_ref[rows, cr] = ngr
            dh_ref[rows, ci] = ngi
            return ngr, ngi, ar, ai

        init = (stg_ref[:, cr], stg_ref[:, ci], acc_ref[:, cr], acc_ref[:, ci])
        gr, gi, ar, ai = lax.fori_loop(0, n_groups, step, init)
        stg_ref[:, cr] = gr
        stg_ref[:, ci] = gi
        acc_ref[:, cr] = ar
        acc_ref[:, ci] = ai


def _chunk_starts(st_ref, init_ref, lcl_ref, s, n_blocks):
    w = s // n_blocks
    init_ref[0:1, :] = jnp.zeros((1, 2 * s), F32)
    for re0, im0 in _state_blocks(2 * s, n_blocks, w):
        re, im = pl.ds(re0, w), pl.ds(im0, w)
        ar, ai = lcl_ref[:, re], lcl_ref[:, im]
        cr = jnp.zeros((1, w), F32)
        ci = jnp.zeros((1, w), F32)
        for k in range(1, NCH):
            cr, ci = (ar * cr - ai * ci + st_ref[k - 1 : k, re], ar * ci + ai * cr + st_ref[k - 1 : k, im])
            init_ref[k : k + 1, re] = cr
            init_ref[k : k + 1, im] = ci


def _chunk_starts_adjoint(stg_ref, initg_ref, lcl_ref, s, n_blocks):
    w = s // n_blocks
    initg_ref[NCH - 1 : NCH, :] = jnp.zeros((1, 2 * s), F32)
    for re0, im0 in _state_blocks(2 * s, n_blocks, w):
        re, im = pl.ds(re0, w), pl.ds(im0, w)
        ar, ai = lcl_ref[:, re], -lcl_ref[:, im]
        gr = jnp.zeros((1, w), F32)
        gi = jnp.zeros((1, w), F32)
        for k in range(NCH - 2, -1, -1):
            gr, gi = (stg_ref[k + 1 : k + 2, re] + ar * gr - ai * gi, stg_ref[k + 1 : k + 2, im] + ar * gi + ai * gr)
            initg_ref[k : k + 1, re] = gr
            initg_ref[k : k + 1, im] = gi


def _ssm_rows(seq):
    rows = min(512, seq // 2)
    assert seq % rows == 0 and rows % SUBLANES == 0
    return rows


def _chunk_copies(hbm_ref, b, cm_ref, sems, to_cm):
    cl = cm_ref.shape[0]
    copies = []
    for k in range(NCH):
        nat, cm = hbm_ref.at[b, pl.ds(k * cl, cl), :], cm_ref.at[:, k, :]
        src, dst = (nat, cm) if to_cm else (cm, nat)
        copies.append(pltpu.make_async_copy(src, dst, sems.at[k]))
    return copies


def _blockwise(fn, n_blocks):
    return jnp.concatenate([fn(b) for b in range(n_blocks)], axis=1)


def _ssm_fwd(u, b_tab, c_tab, lam8, lcl, d_skip, w_glu):
    batch, seq, width = u.shape
    s = lam8.shape[1] // 2
    nb = b_tab.shape[0]
    sb = 2 * s // nb
    cl = seq // NCH
    rows = _ssm_rows(seq)
    n_tiles = seq // rows
    n_groups = rows // SUBLANES

    def body(u_hbm, b_ref, c_ref, lam_ref, lcl_ref, d_ref, wg_ref, y_hbm, pre_ref, z_ref, init_ref,
             u_cm, y_cm, bu, st, sems):
        b, ph, t = pl.program_id(0), pl.program_id(1), pl.program_id(2)
        tile_groups = pl.ds(pl.multiple_of(t * n_groups, n_groups), n_groups)

        @pl.when((ph == 0) & (t == 0))
        def _():
            loads = _chunk_copies(u_hbm, b, u_cm, sems, True)
            for cp in loads:
                cp.start()
            st[...] = jnp.zeros_like(st)
            for cp in loads:
                cp.wait()

        @pl.when((ph == 1) & (t == 0))
        def _():
            st[...] = init_ref[...]

        u_t = u_cm[tile_groups].reshape(rows, width)
        u_b = u_t.astype(BF16)
        for blk in range(nb):
            bu[:, blk * sb : (blk + 1) * sb] = _mm(u_b[:, blk * LANES : (blk + 1) * LANES], b_ref[blk])

        @pl.when(ph == 0)
        def _():
            _scan(bu, bu, st, lam_ref, n_groups, s, nb, False, False)

        @pl.when((ph == 0) & (t == n_tiles - 1))
        def _():
            _chunk_starts(st, init_ref, lcl_ref, s, nb)

        @pl.when(ph == 1)
        def _():
            _scan(bu, bu, st, lam_ref, n_groups, s, nb, False, True)
            hs = lambda blk: _mm(bu[:, blk * sb : (blk + 1) * sb].astype(BF16), c_ref[blk])
            pre = _blockwise(hs, nb) + d_ref[...] * u_t
            z = _mm(_gelu(pre).astype(BF16), wg_ref[...])
            pre_ref[...] = pre
            z_ref[...] = z
            y = z[:, 0:width] * jax.nn.sigmoid(z[:, width : 2 * width])
            y_cm[tile_groups] = y.reshape(n_groups, SUBLANES, width)

        @pl.when((ph == 1) & (t == n_tiles - 1))
        def _():
            stores = _chunk_copies(y_hbm, b, y_cm, sems, False)
            for cp in stores:
                cp.start()
            for cp in stores:
                cp.wait()

    out_tile = lambda b, ph, t: (b, t * ph, 0)
    full = lambda a: pl.BlockSpec(a.shape, lambda b, ph, t: (0,) * a.ndim)
    hbm = pl.BlockSpec(memory_space=pl.ANY)
    return pl.pallas_call(
        body,
        name="ssm_fwd",
        grid=(batch, 2, n_tiles),
        in_specs=[hbm, full(b_tab), full(c_tab), full(lam8), full(lcl), full(d_skip), full(w_glu)],
        out_specs=[
            hbm,
            pl.BlockSpec((None, rows, width), out_tile),
            pl.BlockSpec((None, rows, 2 * width), out_tile),
            pl.BlockSpec((None, SUBLANES, 2 * s), lambda b, ph, t: (b, 0, 0)),
        ],
        out_shape=[
            jax.ShapeDtypeStruct((batch, seq, width), F32),
            jax.ShapeDtypeStruct((batch, seq, width), F32),
            jax.ShapeDtypeStruct((batch, seq, 2 * width), F32),
            jax.ShapeDtypeStruct((batch, SUBLANES, 2 * s), F32),
        ],
        scratch_shapes=[
            pltpu.VMEM((cl, NCH, width), F32),
            pltpu.VMEM((cl, NCH, width), F32),
            pltpu.VMEM((rows, 2 * s), F32),
            pltpu.VMEM((SUBLANES, 2 * s), F32),
            pltpu.SemaphoreType.DMA((NCH,)),
        ],
        compiler_params=_params("arbitrary", "arbitrary", "arbitrary"),
    )(u, b_tab, c_tab, lam8, lcl, d_skip, w_glu)


def _ssm_bwd(u, pre_p, z_p, dy, init, b_tab, b_tab_t, c_tab_t, lam8, lcl, d_skip, w_glu, ready):
    batch, seq, width = u.shape
    nr = len(ready)
    s = lam8.shape[1] // 2
    nb = b_tab.shape[0]
    sb = 2 * s // nb
    cl = seq // NCH
    rows = _ssm_rows(seq)
    n_tiles = seq // rows
    n_groups = rows // SUBLANES

    def body(u_hbm, pre_ref, z_ref, dy_hbm, init_ref, b_ref, bt_ref, ct_ref, lam_ref, lcl_ref, d_ref, wg_ref, *rest):
        ready_hbm, rest = rest[:nr], rest[nr:]
        du_hbm, gb_ref, gc_ref, gwg_ref, gd_ref, glam_ref = rest[:6]
        landed_hbm, rest = rest[6 : 6 + nr], rest[6 + nr :]
        u_cm, dy_cm, work, hs_all, dpre_all, st, stg, initg, acc, sems, send_sems, recv_sems = rest
        b, ph, t = pl.program_id(0), pl.program_id(1), pl.program_id(2)
        half = s // nb
        exchange_start, exchange_finish = _exchange_steps(
            nr, lambda a, slot: ready_hbm[a].at[slot], landed_hbm, send_sems, recv_sems)
        first = (b == 0) & (ph == 0) & (t == 0)
        last = (b == batch - 1) & (ph == 2) & (t == n_tiles - 1)
        tile = jnp.where(ph == 0, t, n_tiles - 1 - t)
        tile_rows = pl.ds(pl.multiple_of(tile * rows, rows), rows)
        tile_groups = pl.ds(pl.multiple_of(tile * n_groups, n_groups), n_groups)
        lanes = lambda blk: slice(blk * LANES, (blk + 1) * LANES)
        states = lambda blk: slice(blk * sb, (blk + 1) * sb)

        @pl.when(first)
        def _():
            exchange_start()
            acc[...] = jnp.zeros_like(acc)
            gb_ref[...] = jnp.zeros_like(gb_ref)
            gc_ref[...] = jnp.zeros_like(gc_ref)
            gwg_ref[...] = jnp.zeros_like(gwg_ref)
            gd_ref[...] = jnp.zeros_like(gd_ref)

        @pl.when((ph == 0) & (t == 0))
        def _():
            loads = _chunk_copies(u_hbm, b, u_cm, sems.at[0], True) + _chunk_copies(dy_hbm, b, dy_cm, sems.at[1], True)
            for cp in loads:
                cp.start()
            st[...] = init_ref[...]
            for blk in range(nb):
                entry = init_ref[:, states(blk)]
                hs_all[0:SUBLANES, blk * half : (blk + 1) * half] = _pack_state(entry[:, 0:half], entry[:, half : 2 * half])
            for cp in loads:
                cp.wait()

        u_t = u_cm[tile_groups].reshape(rows, width)
        u_b = u_t.astype(BF16)

        @pl.when(ph == 0)
        def _():
            for blk in range(nb):
                work[:, states(blk)] = _mm(u_b[:, lanes(blk)], b_ref[blk])
            _scan(work, work, st, lam_ref, n_groups, s, nb, False, True)
            z = z_ref[...]
            dy_t = dy_cm[tile_groups].reshape(rows, width)
            pre = pre_ref[...]
            z1, sig = z[:, 0:width], jax.nn.sigmoid(z[:, width : 2 * width])
            dz = jnp.concatenate([dy_t * sig, dy_t * z1 * sig * (1.0 - sig)], axis=1).astype(BF16)
            gwg_ref[...] += _mm_tn(_gelu(pre).astype(BF16), dz)
            dpre = _mm_nt(dz, wg_ref[...]) * _gelu_grad(pre)
            dpre_all[tile_rows, :] = dpre
            gd_ref[...] += jnp.sum(dpre * u_t, axis=0, keepdims=True)
            dpre_b = dpre.astype(BF16)
            kept = pl.ds(pl.multiple_of(tile * rows + SUBLANES, SUBLANES), rows)
            for blk in range(nb):
                hs = work[:, states(blk)]
                gc_ref[blk] += _mm_tn(hs.astype(BF16), dpre_b[:, lanes(blk)])
                hs_all[kept, blk * half : (blk + 1) * half] = _pack_state(hs[:, 0:half], hs[:, half : 2 * half])

        @pl.when(ph >= 1)
        def _():
            dpre_b = dpre_all[tile_rows, :].astype(BF16)
            for blk in range(nb):
                work[:, states(blk)] = _mm(dpre_b[:, lanes(blk)], ct_ref[blk])

        @pl.when(ph == 1)
        def _():
            @pl.when(t == 0)
            def _():
                stg[...] = jnp.zeros_like(stg)

            _scan(work, work, stg, lam_ref, n_groups, s, nb, True, False)

            @pl.when(t == n_tiles - 1)
            def _():
                _chunk_starts_adjoint(stg, initg, lcl_ref, s, nb)

        @pl.when(ph == 2)
        def _():
            @pl.when(t == 0)
            def _():
                stg[...] = initg[...]

            _scan_adjoint(work, hs_all, tile * n_groups, stg, acc, lam_ref, n_groups, s, nb)
            du = lambda blk: _mm(work[:, states(blk)].astype(BF16), bt_ref[blk])
            du_t = _blockwise(du, nb) + dpre_all[tile_rows, :] * d_ref[...]
            dy_cm[tile_groups] = du_t.reshape(n_groups, SUBLANES, width)
            for blk in range(nb):
                gb_ref[blk] += _mm_tn(u_b[:, lanes(blk)], work[:, states(blk)].astype(BF16))

            @pl.when(t == n_tiles - 1)
            def _():
                stores = _chunk_copies(du_hbm, b, dy_cm, sems.at[0], False)
                for cp in stores:
                    cp.start()
                for cp in stores:
                    cp.wait()

        @pl.when(last)
        def _():
            glam_ref[...] = jnp.sum(acc[...], axis=0, keepdims=True)
            exchange_finish()

    def tile(b, ph, t):
        return (b, jnp.where(ph == 0, t, n_tiles - 1 - t), 0)

    full = lambda a: pl.BlockSpec(a.shape, lambda b, ph, t: (0,) * a.ndim)
    hbm = pl.BlockSpec(memory_space=pl.ANY)
    res = pl.pallas_call(
        body,
        name="ssm_bwd",
        grid=(batch, 3, n_tiles),
        in_specs=[
            hbm,
            pl.BlockSpec((None, rows, width), tile),
            pl.BlockSpec((None, rows, 2 * width), tile),
            hbm,
            pl.BlockSpec((None, SUBLANES, 2 * s), lambda b, ph, t: (b, 0, 0)),
            full(b_tab), full(b_tab_t), full(c_tab_t), full(lam8), full(lcl), full(d_skip), full(w_glu),
        ]
        + [hbm] * nr,
        out_specs=[
            hbm,
            full(b_tab), full(b_tab_t), full(w_glu), full(d_skip),
            pl.BlockSpec((1, 2 * s), lambda b, ph, t: (0, 0)),
        ]
        + [hbm] * nr,
        out_shape=[
            jax.ShapeDtypeStruct((batch, seq, width), F32),
            jax.ShapeDtypeStruct(b_tab.shape, F32),
            jax.ShapeDtypeStruct(b_tab_t.shape, F32),
            jax.ShapeDtypeStruct(w_glu.shape, F32),
            jax.ShapeDtypeStruct(d_skip.shape, F32),
            jax.ShapeDtypeStruct((1, 2 * s), F32),
        ]
        + [jax.ShapeDtypeStruct(a.shape, F32) for a in ready],
        scratch_shapes=[
            pltpu.VMEM((cl, NCH, width), F32),
            pltpu.VMEM((cl, NCH, width), F32),
            pltpu.VMEM((rows, 2 * s), F32),
            pltpu.VMEM((seq + SUBLANES, s), jnp.uint32),
            pltpu.VMEM((seq, width), F32),
        ]
        + [pltpu.VMEM((SUBLANES, 2 * s), F32)] * 4
        + [pltpu.SemaphoreType.DMA((2, NCH)), pltpu.SemaphoreType.DMA((nr, 7)), pltpu.SemaphoreType.DMA((nr, 7))],
        compiler_params=_params("arbitrary", "arbitrary", "arbitrary"),
    )(u, pre_p, z_p, dy, init, b_tab, b_tab_t, c_tab_t, lam8, lcl, d_skip, w_glu, *ready)
    return res[:6], res[6:]


def _kv_fwd(mem, g_mem, w_kv):
    batch, n_mem, d = mem.shape
    kvw = w_kv.shape[1]

    def body(mem_ref, g_ref, w_ref, kv_ref):
        m = mem_ref[...]
        kv_ref[...] = _mm((m * _rms(m) * g_ref[...]).astype(BF16), w_ref[...])

    return pl.pallas_call(
        body,
        name="kv_fwd",
        grid=(batch,),
        in_specs=[
            pl.BlockSpec((None, n_mem, d), lambda b: (b, 0, 0)),
            pl.BlockSpec((1, d), lambda b: (0, 0)),
            pl.BlockSpec((d, kvw), lambda b: (0, 0)),
        ],
        out_specs=pl.BlockSpec((None, n_mem, kvw), lambda b: (b, 0, 0)),
        out_shape=jax.ShapeDtypeStruct((batch, n_mem, kvw), F32),
        compiler_params=_params("arbitrary"),
    )(mem, g_mem, w_kv)


def _kv_bwd(mem, dkv, g_mem, w_kv):
    batch, n_mem, d = mem.shape
    kvw = w_kv.shape[1]

    def body(mem_ref, dkv_ref, g_ref, w_ref, gw_ref, gg_ref):
        @pl.when(pl.program_id(0) == 0)
        def _():
            gw_ref[...] = jnp.zeros_like(gw_ref)
            gg_ref[...] = jnp.zeros_like(gg_ref)

        m = mem_ref[...]
        mn = m * _rms(m)
        dkv_b = dkv_ref[...].astype(BF16)
        gw_ref[...] += _mm_tn((mn * g_ref[...]).astype(BF16), dkv_b)
        gg_ref[...] += jnp.sum(_mm_nt(dkv_b, w_ref[...]) * mn, axis=0, keepdims=True)

    return pl.pallas_call(
        body,
        name="kv_bwd",
        grid=(batch,),
        in_specs=[
            pl.BlockSpec((None, n_mem, d), lambda b: (b, 0, 0)),
            pl.BlockSpec((None, n_mem, kvw), lambda b: (b, 0, 0)),
            pl.BlockSpec((1, d), lambda b: (0, 0)),
            pl.BlockSpec((d, kvw), lambda b: (0, 0)),
        ],
        out_specs=[pl.BlockSpec((d, kvw), lambda b: (0, 0)), pl.BlockSpec((1, d), lambda b: (0, 0))],
        out_shape=[jax.ShapeDtypeStruct((d, kvw), F32), jax.ShapeDtypeStruct((1, d), F32)],
        compiler_params=_params("arbitrary"),
    )(mem, dkv, g_mem, w_kv)


def _tail(x2, target2, gate, y_pool, y_ssm, q, kv, w_out, g_post):
    tokens, d = x2.shape
    pool_w, ssm_w, att_w, mix = y_pool.shape[1], y_ssm.shape[1], q.shape[1], gate.shape[1]
    batch, n_mem, kvw = kv.shape
    hd = att_w // MEM_HEADS
    cl = _token_tile(tokens // batch, 256)
    n_tiles = tokens // cl
    per_seq = tokens // batch // cl
    qk_scale = hd**-0.5

    def body(x_ref, tg_ref, gate_ref, yp_ref, ys_ref, q_ref, kv_ref, w_ref, g_ref,
             dres_ref, dgate_ref, dyp_ref, dys_ref, dq_ref, dkv_ref, gw_hbm, gg_ref, loss_ref, acc, sem):
        i = pl.program_id(0)

        @pl.when(i == 0)
        def _():
            acc[...] = jnp.zeros_like(acc)
            gg_ref[...] = jnp.zeros_like(gg_ref)
            loss_ref[...] = jnp.zeros_like(loss_ref)

        @pl.when(i % per_seq == 0)
        def _():
            dkv_ref[...] = jnp.zeros_like(dkv_ref)

        q = q_ref[...]
        k = kv_ref[:, 0:att_w].astype(BF16)
        v = kv_ref[:, att_w : 2 * att_w].astype(BF16)
        lane = lax.broadcasted_iota(jnp.int32, (1, att_w), 1)
        heads = [(lane >= h * hd) & (lane < (h + 1) * hd) for h in range(MEM_HEADS)]
        probs, q_heads = [], []
        att = jnp.zeros((cl, att_w), F32)
        for mask in heads:
            qh = jnp.where(mask, q, 0.0).astype(BF16)
            sc = _mm_nt(qh, k) * qk_scale
            e = jnp.exp(sc - jnp.max(sc, axis=-1, keepdims=True))
            p = e * (1.0 / jnp.sum(e, axis=-1, keepdims=True))
            att = att + jnp.where(mask, _mm(p.astype(BF16), v), 0.0)
            probs.append(p)
            q_heads.append(qh)

        ycat = jnp.concatenate([yp_ref[...], ys_ref[...], att], axis=1)
        gate = gate_ref[...]
        sig = jax.nn.sigmoid(gate)
        silu = gate * sig
        yg = (ycat * silu).astype(BF16)
        out = _mm(yg, w_ref[...])
        r = _rms(out)
        on = out * r
        g = g_ref[...]
        x = x_ref[...]
        err = x + on * g - tg_ref[...]
        loss_ref[...] += 0.5 * jnp.sum(jnp.mean(err * err, axis=-1, keepdims=True), axis=0, keepdims=True)
        dres = err * (1.0 / d)
        dres_ref[...] = dres
        gg_ref[...] += jnp.sum(dres * on, axis=0, keepdims=True)
        don = dres * g
        dout = (r * (don - on * jnp.mean(don * on, axis=-1, keepdims=True))).astype(BF16)
        acc[...] += _mm_tn(yg, dout)
        dyg = _mm_nt(dout, w_ref[...])
        dgate_ref[...] = dyg * ycat * (sig * (1.0 + gate * (1.0 - sig)))
        dycat = dyg * silu
        dyp_ref[...] = dycat[:, 0:pool_w]
        dys_ref[...] = dycat[:, pool_w : pool_w + ssm_w]
        datt = dycat[:, pool_w + ssm_w : mix]

        dq = jnp.zeros((cl, att_w), F32)
        dk = jnp.zeros((n_mem, att_w), F32)
        dv = jnp.zeros((n_mem, att_w), F32)
        for mask, p, qh in zip(heads, probs, q_heads):
            doh = jnp.where(mask, datt, 0.0).astype(BF16)
            dp = _mm_nt(doh, v)
            ds = (p * (dp - jnp.sum(p * dp, axis=-1, keepdims=True)) * qk_scale).astype(BF16)
            dq = dq + jnp.where(mask, _mm(ds, k), 0.0)
            dk = dk + _mm_tn(ds, qh)
            dv = dv + _mm_tn(p.astype(BF16), doh)
        dq_ref[...] = dq
        dkv_ref[:, 0:att_w] += dk
        dkv_ref[:, att_w : 2 * att_w] += dv

        @pl.when(i == n_tiles - 1)
        def _():
            cp = pltpu.make_async_copy(acc, gw_hbm, sem)
            cp.start()
            cp.wait()

    tok = lambda w: pl.BlockSpec((cl, w), lambda i: (i, 0))
    chunked = tok(ssm_w)
    per_batch = pl.BlockSpec((None, n_mem, kvw), lambda i: (i // per_seq, 0, 0))
    return pl.pallas_call(
        body,
        name="tail",
        grid=(n_tiles,),
        in_specs=[
            tok(d), tok(d), tok(mix), tok(pool_w), chunked, tok(att_w), per_batch,
            pl.BlockSpec((mix, d), lambda i: (0, 0)),
            pl.BlockSpec((1, d), lambda i: (0, 0)),
        ],
        out_specs=[
            tok(d), tok(mix), tok(pool_w), chunked, tok(att_w), per_batch,
            pl.BlockSpec(memory_space=pl.ANY),
            pl.BlockSpec((1, d), lambda i: (0, 0)),
            pl.BlockSpec((1, 1), lambda i: (0, 0)),
        ],
        out_shape=[
            jax.ShapeDtypeStruct((tokens, d), F32),
            jax.ShapeDtypeStruct((tokens, mix), F32),
            jax.ShapeDtypeStruct((tokens, pool_w), F32),
            jax.ShapeDtypeStruct((tokens, ssm_w), F32),
            jax.ShapeDtypeStruct((tokens, att_w), F32),
            jax.ShapeDtypeStruct(kv.shape, F32),
            jax.ShapeDtypeStruct((mix, d), F32),
            jax.ShapeDtypeStruct((1, d), F32),
            jax.ShapeDtypeStruct((1, 1), F32),
        ],
        scratch_shapes=[pltpu.VMEM((mix, d), F32), pltpu.SemaphoreType.DMA],
        compiler_params=_params("arbitrary"),
    )(x2, target2, gate, y_pool, y_ssm, q, kv, w_out, g_post)


def _pack(arrays):
    flat = jnp.concatenate([a.reshape(-1) for a in arrays])
    rows = -(-flat.size // (SUBLANES * LANES)) * SUBLANES
    return jnp.pad(flat, (0, rows * LANES - flat.size)).reshape(rows, LANES)


def _unpack(packed, like):
    flat, out, at = packed.reshape(-1), [], 0
    for a in like:
        out.append(flat[at : at + a.size].reshape(a.shape))
        at += a.size
    return out


def kernel(x, mem, g_pre, w_in, w_pool, pool_scale, a_re, a_im, log_dt, b_re, b_im, c_re, c_im, d_skip, w_glu, g_mem, w_kv, w_out, g_post, loss_target, m_g_pre, m_w_in, m_w_pool, m_pool_scale, m_a_re, m_a_im, m_log_dt, m_b_re, m_b_im, m_c_re, m_c_im, m_d_skip, m_w_glu, m_g_mem, m_w_kv, m_w_out, m_g_post, v_g_pre, v_w_in, v_w_pool, v_pool_scale, v_a_re, v_a_im, v_log_dt, v_b_re, v_b_im, v_c_re, v_c_im, v_d_skip, v_w_glu, v_g_mem, v_w_kv, v_w_out, v_g_post):
    batch, seq, d = x.shape
    cl = seq // NCH
    pool_w, ssm_w = pool_scale.shape[1], d_skip.shape[1]
    att_w = w_kv.shape[2] // 2
    tokens = batch * seq
    x2 = x.reshape(tokens, d)
    target2 = loss_target.reshape(tokens, d)

    (w_in_g,), shards = _gather_weights([w_in[0]], [w_out[0], w_kv[0], w_glu[0]])

    wp_blk = jax.scipy.linalg.block_diag(*w_pool[0]).astype(BF16)
    ssm_params = (a_re[0], a_im[0], log_dt[0], b_re[0], b_im[0], c_re[0], c_im[0])
    (lam_row, b_tab, c_tab), tables_vjp = jax.vjp(_ssm_tables, *ssm_params)
    nb = b_tab.shape[0]
    lam8 = jnp.broadcast_to(lam_row, (SUBLANES, lam_row.shape[1]))
    lcl = _lam_power(a_re[0], a_im[0], log_dt[0], float(cl), 1.0, nb)
    b_bf, c_bf = b_tab.astype(BF16), c_tab.astype(BF16)

    (u_pool, u_ssm, q, gate), (w_out_g, w_kv_g, w_glu_g) = _in_proj(x2, g_pre, w_in_g, shards, pool_w, ssm_w, att_w)
    w_out_f = w_out_g.reshape(N_DEV * w_out_g.shape[1], w_out_g.shape[2])
    w_kv_f = w_kv_g.reshape(N_DEV * w_kv_g.shape[1], w_kv_g.shape[2])
    w_glu_f = w_glu_g.transpose(1, 0, 2).reshape(w_glu_g.shape[1], N_DEV * w_glu_g.shape[2])
    y_pool = _pool_fwd(u_pool, wp_blk, pool_scale, batch, seq)
    u_ssm = u_ssm.reshape(batch, seq, ssm_w)
    y_ssm, pre_ssm, z_ssm, init_ssm = _ssm_fwd(u_ssm, b_bf, c_bf, lam8, lcl, d_skip, w_glu_f)
    kv = _kv_fwd(mem, g_mem, w_kv_f)

    dres, dgate, dy_pool, dy_ssm, dq, dkv, gw_out, gg_post, loss_part = _tail(
        x2, target2, gate, y_pool, y_ssm.reshape(tokens, ssm_w), q, kv, w_out_f, g_post)

    gw_kv, gg_mem = _kv_bwd(mem, dkv, g_mem, w_kv_f)
    du_pool, gwp_dense, g_scale = _pool_bwd(u_pool, dy_pool, wp_blk, pool_scale, batch, seq)
    gw_kv8 = gw_kv.reshape(N_DEV, -1, gw_kv.shape[1])
    gw_out8 = gw_out.reshape(N_DEV, -1, gw_out.shape[1])
    (du_ssm, gb_tab, gc_tab, gw_glu, gd_skip, glam), (kv_landed, out_landed) = _ssm_bwd(
        u_ssm, pre_ssm, z_ssm, dy_ssm.reshape(batch, seq, ssm_w), init_ssm, b_bf, b_bf.transpose(0, 2, 1),
        c_bf.transpose(0, 2, 1), lam8, lcl, d_skip, w_glu_f, [gw_kv8, gw_out8])
    grad_x2, gw_in, gg_pre = _in_proj_bwd(
        x2, dres, du_pool, du_ssm.reshape(tokens, ssm_w), dq, dgate, g_pre, w_in_g)

    gw = pool_w // len(POOL_WINDOWS)
    gw_pool = jnp.stack([gwp_dense[i * gw : (i + 1) * gw, i * gw : (i + 1) * gw] for i in range(len(POOL_WINDOWS))])
    g_ssm = tables_vjp((glam, gb_tab, gc_tab))

    small_w = [g_pre, w_pool, pool_scale, a_re, a_im, log_dt, b_re, b_im, c_re, c_im, d_skip, g_mem, g_post]
    small_m = [m_g_pre, m_w_pool, m_pool_scale, m_a_re, m_a_im, m_log_dt, m_b_re, m_b_im, m_c_re, m_c_im, m_d_skip, m_g_mem, m_g_post]
    small_v = [v_g_pre, v_w_pool, v_pool_scale, v_a_re, v_a_im, v_log_dt, v_b_re, v_b_im, v_c_re, v_c_im, v_d_skip, v_g_mem, v_g_post]
    small_g = [gg_pre, gw_pool, g_scale, *g_ssm, gd_skip, gg_mem, gg_post]
    big, small_sum = _reduce_all(
        [gw_in, gw_glu.reshape(ssm_w, N_DEV, -1).transpose(1, 0, 2)],
        [w_in[0], w_glu[0]], [m_w_in[0], m_w_glu[0]], [v_w_in[0], v_w_glu[0]],
        _pack(small_g + [loss_part]),
        [(gw_kv8, kv_landed, w_kv[0], m_w_kv[0], v_w_kv[0]), (gw_out8, out_landed, w_out[0], m_w_out[0], v_w_out[0])])
    big = {name: tuple(t[None] for t in res) for name, res in zip(["w_in", "w_glu", "w_kv", "w_out"], big)}

    flat2 = lambda a: a.reshape(-1, a.shape[-1])
    sg = _unpack(small_sum, [flat2(a) for a in small_w] + [loss_part])
    loss = sg[-1].reshape(())
    updates = _adamw_small(sg[:-1], [flat2(a) for a in small_w], [flat2(a) for a in small_m], [flat2(a) for a in small_v])
    sg = [g.reshape(a.shape) for g, a in zip(sg[:-1], small_w)]
    sd, sm, sv = ([u[kind].reshape(a.shape) for u, a in zip(updates, small_w)] for kind in range(3))

    order = ["g_pre", "w_in", "w_pool", "pool_scale", "a_re", "a_im", "log_dt", "b_re", "b_im", "c_re", "c_im",
             "d_skip", "w_glu", "g_mem", "w_kv", "w_out", "g_post"]
    small_names = ["g_pre", "w_pool", "pool_scale", "a_re", "a_im", "log_dt", "b_re", "b_im", "c_re", "c_im",
                   "d_skip", "g_mem", "g_post"]
    outs = [[], [], [], []]
    for name in order:
        if name in big:
            parts = big[name]
        else:
            j = small_names.index(name)
            parts = (sg[j], sd[j], sm[j], sv[j])
        for kind in range(4):
            outs[kind].append(parts[kind])
    return (loss, grad_x2.reshape(batch, seq, d), *outs[0], *outs[1], *outs[2], *outs[3])
```

```python
import functools
import math

import jax
import jax.numpy as jnp
from jax import lax
from jax.experimental import pallas as pl
from jax.experimental.pallas import tpu as pltpu

F32 = jnp.float32
BF16 = jnp.bfloat16
MESH = pl.DeviceIdType.MESH

N_DEV = 8
NCH = 8
SUBLANES = 8
LANES = 128
VMEM_LIMIT = 56 * 1024 * 1024

EPS = 1e-6
POOL_WINDOWS = (2, 4, 8, 16)
MEM_HEADS = 4
SSM_GROUP = 16
SSM_N = 64
ADAM_LR, ADAM_B1, ADAM_B2, ADAM_EPS, ADAM_WD, ADAM_STEP = 0.001, 0.9, 0.999, 1e-08, 0.01, 10


def _mm(a, b):
    return jnp.dot(a, b, preferred_element_type=F32)


def _mm_nt(a, b):
    return lax.dot_general(a, b, (((1,), (1,)), ((), ())), preferred_element_type=F32)


def _mm_tn(a, b):
    return lax.dot_general(a, b, (((0,), (0,)), ((), ())), preferred_element_type=F32)


def _params(*sem):
    return pltpu.CompilerParams(dimension_semantics=sem or None, vmem_limit_bytes=VMEM_LIMIT)


def _adamw(w, g, m, v):
    m = ADAM_B1 * m + (1.0 - ADAM_B1) * g
    v = ADAM_B2 * v + (1.0 - ADAM_B2) * (g * g)
    m_hat = m / (1.0 - ADAM_B1**ADAM_STEP)
    v_hat = v / (1.0 - ADAM_B2**ADAM_STEP)
    delta = -ADAM_LR * (m_hat / (jnp.sqrt(v_hat) + ADAM_EPS) + ADAM_WD * w)
    return delta, m, v


def _gelu(x):
    k = math.sqrt(2.0 / math.pi)
    return 0.5 * x * (1.0 + jnp.tanh(k * (x + 0.044715 * x * x * x)))


def _gelu_grad(x):
    k = math.sqrt(2.0 / math.pi)
    th = jnp.tanh(k * (x + 0.044715 * x * x * x))
    return 0.5 * (1.0 + th) + 0.5 * x * (1.0 - th * th) * (k * (1.0 + 3.0 * 0.044715 * x * x))


def _place():
    return lax.axis_index("x"), lax.axis_index("y"), lax.axis_index("c")


def _gather_steps(ins, outs, send_sems, recv_sems):
    n = len(ins)
    x, y, c = _place()
    me, sibling = (x, y, c), (x, y, 1 - c)
    chips = [(1 - x, y), (x, 1 - y), (1 - x, 1 - y)]
    sent = []

    def slot(px, py, pc):
        return 4 * px + 2 * py + pc

    def copy(a, k, block, to):
        ref = outs[a].at[slot(*block)]
        return pltpu.make_async_remote_copy(
            src_ref=ref, dst_ref=ref, send_sem=send_sems.at[a, k], recv_sem=recv_sems.at[a, k],
            device_id=to, device_id_type=MESH)

    def start():
        for a in range(n):
            outs[a][slot(*me)] = ins[a][...].astype(outs[a].dtype)
        for a in range(n):
            sent.append(copy(a, 0, me, sibling))
            sent.extend(copy(a, 1 + j, me, (*chip, c)) for j, chip in enumerate(chips))
        for cp in sent:
            cp.start()

    def forward():
        for j, chip in enumerate(chips):
            for a in range(n):
                copy(a, 1 + j, (*chip, c), me).wait_recv()
                cp = copy(a, 4 + j, (*chip, c), sibling)
                cp.start()
                sent.append(cp)

    def finish():
        for a in range(n):
            copy(a, 0, sibling, me).wait_recv()
            for j, chip in enumerate(chips):
                copy(a, 4 + j, (*chip, 1 - c), me).wait_recv()
        for cp in sent:
            cp.wait_send()

    return start, forward, finish


def _exchange_steps(n, src_of, landing, send_sems, recv_sems):
    x, y, c = _place()
    me = 4 * x + 2 * y + c
    peers = []
    for j in range(1, N_DEV):
        px = 1 - x if j & 4 else x
        py = 1 - y if j & 2 else y
        pc = 1 - c if j & 1 else c
        peers.append((px, py, pc))

    def copy(a, j, from_slot, to_slot, peer):
        return pltpu.make_async_remote_copy(
            src_ref=src_of(a, to_slot), dst_ref=landing[a].at[from_slot],
            send_sem=send_sems.at[a, j], recv_sem=recv_sems.at[a, j], device_id=peer, device_id_type=MESH)

    def start():
        for a in range(n):
            for j, p in enumerate(peers):
                copy(a, j, me, 4 * p[0] + 2 * p[1] + p[2], p).start()

    def finish():
        for a in range(n):
            for j, p in enumerate(peers):
                slot = 4 * p[0] + 2 * p[1] + p[2]
                copy(a, j, slot, slot, p).wait_recv()
        for a in range(n):
            for j, p in enumerate(peers):
                copy(a, j, me, 4 * p[0] + 2 * p[1] + p[2], p).wait_send()

    return start, finish


def _ordered_sum(gathered, out_ref):
    rows = out_ref.shape[0]

    def step(i, _):
        r = pl.ds(pl.multiple_of(i * SUBLANES, SUBLANES), SUBLANES)
        g = gathered[0, r, :]
        for d in range(1, N_DEV):
            g = g + gathered[d, r, :]
        out_ref[r, :] = g
        return 0

    lax.fori_loop(0, rows // SUBLANES, step, 0)


def _gather_weights(blocks, later):
    n, k = len(blocks), len(later)

    def body(*refs):
        refs = list(refs)
        take = lambda cnt: [refs.pop(0) for _ in range(cnt)]
        ins, later_in, outs, later_out = take(n), take(k), take(n), take(k)
        start, forward, finish = _gather_steps(ins, outs, *refs)
        start()
        for src, dst in zip(later_in, later_out):
            dst[...] = src[...].astype(BF16)
        forward()
        finish()

    vmem = pl.BlockSpec(memory_space=pltpu.VMEM)
    res = pl.pallas_call(
        body,
        name="gather_weights",
        out_shape=[jax.ShapeDtypeStruct((N_DEV, *b.shape), BF16) for b in blocks]
        + [jax.ShapeDtypeStruct(b.shape, BF16) for b in later],
        in_specs=[vmem] * (n + k),
        out_specs=[vmem] * (n + k),
        scratch_shapes=[pltpu.SemaphoreType.DMA((n, 7)), pltpu.SemaphoreType.DMA((n, 7))],
        compiler_params=_params(),
    )(*blocks, *later)
    return res[:n], res[n:]


def _adamw_small(gs, ws, ms, vs):
    n = len(gs)

    def body(*refs):
        g, w, m, v = refs[:n], refs[n : 2 * n], refs[2 * n : 3 * n], refs[3 * n : 4 * n]
        outs = refs[4 * n :]
        for a in range(n):
            delta, nm, nv = _adamw(w[a][...], g[a][...], m[a][...], v[a][...])
            outs[3 * a][...] = delta
            outs[3 * a + 1][...] = nm
            outs[3 * a + 2][...] = nv

    vmem = pl.BlockSpec(memory_space=pltpu.VMEM)
    out_shape = []
    for wa in ws:
        out_shape += [jax.ShapeDtypeStruct(wa.shape, F32)] * 3
    res = pl.pallas_call(
        body,
        name="adamw_small",
        out_shape=out_shape,
        in_specs=[vmem] * (4 * n),
        out_specs=[vmem] * (3 * n),
        compiler_params=_params(),
    )(*gs, *ws, *ms, *vs)
    return [tuple(res[3 * a : 3 * a + 3]) for a in range(n)]


def _reduce_all(parts, ws, ms, vs, small, early):
    n, ne = len(parts), len(early)
    parts4 = [p.reshape(4, 2, *p.shape[1:]) for p in parts]
    blks = [p.shape[1:] for p in parts]

    def body(*refs):
        refs = list(refs)
        take = lambda k: [refs.pop(0) for _ in range(k)]
        part, w_in, m_in, v_in = take(n), take(n), take(n), take(n)
        (small_ref,) = take(1)
        early_in = [take(5) for _ in range(ne)]
        outs = take(4 * n)
        small_all, small_sum = take(2)
        early_out = [take(4) for _ in range(ne)]
        own, r1, r2 = take(n), take(n), take(n)
        early_buf = take(ne)
        s1_send, s1_recv, s2_send, s2_recv, loc, small_send, small_recv, early_sems = refs
        small_start, small_forward, small_finish = _gather_steps([small_ref], [small_all], small_send, small_recv)
        x, y, c = _place()
        me = 4 * x + 2 * y + c
        landed = [pltpu.make_async_copy(early_in[e][1], early_buf[e], early_sems.at[e, 0]) for e in range(ne)]
        for cp in landed:
            cp.start()
        sibling = (x, y, 1 - c)
        chips = [(1 - x, y), (x, 1 - y), (1 - x, 1 - y)]

        def rowwise(rows, fn):
            chunk = math.gcd(rows, 128)

            def step(i, _):
                fn(pl.ds(pl.multiple_of(i * chunk, chunk), chunk))
                return 0

            lax.fori_loop(0, rows // chunk, step, 0)

        stage1, local = [], []
        for a in range(n):
            cp = pltpu.make_async_remote_copy(
                src_ref=part[a].at[:, 1 - c], dst_ref=r1[a], send_sem=s1_send.at[a], recv_sem=s1_recv.at[a],
                device_id=sibling, device_id_type=MESH)
            cp.start()
            stage1.append(cp)
            lc = pltpu.make_async_copy(part[a].at[:, c], own[a], loc.at[a])
            lc.start()
            local.append(lc)
        small_start()
        stage2 = []
        for a in range(n):
            local[a].wait()
            stage1[a].wait_recv()
            for chip in range(4):

                def add(r, a=a, chip=chip):
                    own[a][chip, r, :] = own[a][chip, r, :] + r1[a][chip, r, :]

                rowwise(blks[a][0], add)
            for k, chip in enumerate(chips):
                cp = pltpu.make_async_remote_copy(
                    src_ref=own[a].at[2 * chip[0] + chip[1]], dst_ref=r2[a].at[k],
                    send_sem=s2_send.at[a, k], recv_sem=s2_recv.at[a, k],
                    device_id=(*chip, c), device_id_type=MESH)
                cp.start()
                stage2.append(cp)
        small_forward()
        for e in range(ne):
            part_e, _, w_e, m_e, v_e = early_in[e]
            g_ref, d_ref, nm_ref, nv_ref = early_out[e]
            landed[e].wait()
            mine = pltpu.make_async_copy(part_e.at[me], early_buf[e].at[me], early_sems.at[e, 1])
            mine.start()
            mine.wait()

            def update_early(r, e=e, w_e=w_e, m_e=m_e, v_e=v_e, g_ref=g_ref, d_ref=d_ref, nm_ref=nm_ref, nv_ref=nv_ref):
                g = early_buf[e][0, r, :]
                for dev in range(1, N_DEV):
                    g = g + early_buf[e][dev, r, :]
                delta, nm, nv = _adamw(w_e[r, :], g, m_e[r, :], v_e[r, :])
                g_ref[r, :] = g
                d_ref[r, :] = delta
                nm_ref[r, :] = nm
                nv_ref[r, :] = nv

            rowwise(early_buf[e].shape[1], update_early)
        for a in range(n):
            for k, chip in enumerate(chips):
                stage2[3 * a + k].wait_recv()
            g_ref, d_ref, nm_ref, nv_ref = outs[4 * a : 4 * a + 4]

            def update(r, a=a, g_ref=g_ref, d_ref=d_ref, nm_ref=nm_ref, nv_ref=nv_ref):
                g = own[a][2 * x + y, r, :] + r2[a][0, r, :] + r2[a][1, r, :] + r2[a][2, r, :]
                delta, nm, nv = _adamw(w_in[a][r, :], g, m_in[a][r, :], v_in[a][r, :])
                g_ref[r, :] = g
                d_ref[r, :] = delta
                nm_ref[r, :] = nm
                nv_ref[r, :] = nv

            rowwise(blks[a][0], update)
        small_finish()
        _ordered_sum(small_all, small_sum)
        for cp in stage1 + stage2:
            cp.wait_send()

    vmem = pl.BlockSpec(memory_space=pltpu.VMEM)
    hbm = pl.BlockSpec(memory_space=pl.ANY)
    out_shape = []
    for b in blks:
        out_shape += [jax.ShapeDtypeStruct(b, F32)] * 4
    out_shape += [jax.ShapeDtypeStruct((N_DEV, *small.shape), F32), jax.ShapeDtypeStruct(small.shape, F32)]
    for e in early:
        out_shape += [jax.ShapeDtypeStruct(e[2].shape, F32)] * 4
    scratch = (
        [pltpu.VMEM((4, *b), F32) for b in blks]
        + [pltpu.VMEM((4, *b), F32) for b in blks]
        + [pltpu.VMEM((3, *b), F32) for b in blks]
        + [pltpu.VMEM(e[0].shape, F32) for e in early]
        + [pltpu.SemaphoreType.DMA((n,)), pltpu.SemaphoreType.DMA((n,)), pltpu.SemaphoreType.DMA((n, 3)),
           pltpu.SemaphoreType.DMA((n, 3)), pltpu.SemaphoreType.DMA((n,)),
           pltpu.SemaphoreType.DMA((1, 7)), pltpu.SemaphoreType.DMA((1, 7)), pltpu.SemaphoreType.DMA((ne, 2))]
    )
    res = pl.pallas_call(
        body,
        name="reduce_all",
        out_shape=out_shape,
        in_specs=[hbm] * n + [vmem] * (3 * n + 1) + [hbm, hbm, vmem, vmem, vmem] * ne,
        out_specs=[vmem] * (4 * n + 2 + 4 * ne),
        scratch_shapes=scratch,
        compiler_params=_params(),
    )(*parts4, *ws, *ms, *vs, small, *[t for e in early for t in e])
    late = [tuple(res[4 * a : 4 * a + 4]) for a in range(n)]
    at = 4 * n + 2
    return late + [tuple(res[at + 4 * e : at + 4 * e + 4]) for e in range(ne)], res[4 * n + 1]


def _rms(x):
    return lax.rsqrt(jnp.mean(x * x, axis=-1, keepdims=True) + EPS)


def _token_tile(tokens, want):
    tile = min(want, tokens // 2)
    assert tokens % tile == 0 and tile % 16 == 0
    return tile


def _in_proj(x2, g_pre, w_in_g, shards, pool_w, ssm_w, att_w):
    tokens, d = x2.shape
    nb = w_in_g.shape[2]
    mix = pool_w + ssm_w + att_w
    half = N_DEV // 2
    cl = _token_tile(tokens, 512)
    n_tiles = tokens // cl
    ns = len(shards)
    assert half * nb == mix and nb == 256 and pool_w == 384 and ssm_w == 384 and att_w == 256

    def body(x_ref, g_ref, w_ref, *rest):
        shard_hbm, (up_ref, us_ref, q_ref, gate_ref) = rest[:ns], rest[ns : ns + 4]
        gathered = rest[ns + 4 : 2 * ns + 4]
        send_sems, recv_sems, own_sems = rest[2 * ns + 4 :]
        i = pl.program_id(0)
        x_pos, y_pos, c_pos = _place()
        own = [pltpu.make_async_copy(shard_hbm[a], gathered[a].at[4 * x_pos + 2 * y_pos + c_pos], own_sems.at[a])
               for a in range(ns)]
        start, finish = _exchange_steps(ns, lambda a, slot: shard_hbm[a], gathered, send_sems, recv_sems)

        @pl.when(i == 0)
        def _():
            start()
            for cp in own:
                cp.start()

        @pl.when(i == n_tiles - 1)
        def _():
            finish()
            for cp in own:
                cp.wait()

        x = x_ref[...]
        h = (x * _rms(x) * g_ref[...]).astype(BF16)
        p = [_mm(h, w_ref[j]) for j in range(half)]
        up_ref[:, 0:256] = p[0]
        up_ref[:, 256:384] = p[1][:, 0:128]
        us_ref[:, 0:128] = p[1][:, 128:256]
        us_ref[:, 128:384] = p[2]
        q_ref[...] = p[3]
        for j in range(half):
            gate_ref[:, j * nb : (j + 1) * nb] = _mm(h, w_ref[half + j])

    hbm = pl.BlockSpec(memory_space=pl.ANY)
    res = pl.pallas_call(
        body,
        name="in_proj",
        grid=(n_tiles,),
        in_specs=[
            pl.BlockSpec((cl, d), lambda i: (i, 0)),
            pl.BlockSpec((1, d), lambda i: (0, 0)),
            pl.BlockSpec((N_DEV, d, nb), lambda i: (0, 0, 0)),
        ]
        + [hbm] * ns,
        out_specs=[
            pl.BlockSpec((cl, pool_w), lambda i: (i, 0)),
            pl.BlockSpec((cl, ssm_w), lambda i: (i, 0)),
            pl.BlockSpec((cl, att_w), lambda i: (i, 0)),
            pl.BlockSpec((cl, mix), lambda i: (i, 0)),
        ]
        + [hbm] * ns,
        out_shape=[
            jax.ShapeDtypeStruct((tokens, pool_w), F32),
            jax.ShapeDtypeStruct((tokens, ssm_w), F32),
            jax.ShapeDtypeStruct((tokens, att_w), F32),
            jax.ShapeDtypeStruct((tokens, mix), F32),
        ]
        + [jax.ShapeDtypeStruct((N_DEV, *a.shape), a.dtype) for a in shards],
        scratch_shapes=[pltpu.SemaphoreType.DMA((ns, 7)), pltpu.SemaphoreType.DMA((ns, 7)),
                        pltpu.SemaphoreType.DMA((ns,))],
        compiler_params=_params("arbitrary"),
    )(x2, g_pre, w_in_g, *shards)
    return res[:4], res[4:]


def _in_proj_bwd(x2, dres, du_pool, du_ssm, dq, dgate, g_pre, w_in_g):
    tokens, d = x2.shape
    nb = w_in_g.shape[2]
    pool_w, ssm_w, att_w, mix = du_pool.shape[1], du_ssm.shape[1], dq.shape[1], dgate.shape[1]
    half = N_DEV // 2
    cl = _token_tile(tokens, 512)
    n_tiles = tokens // cl

    def body(x_ref, dres_ref, dup_ref, dus_ref, dq_ref, dgate_ref, g_ref, w_ref, gx_ref, gw_hbm, gg_ref, acc, sem):
        i = pl.program_id(0)

        @pl.when(i == 0)
        def _():
            acc[...] = jnp.zeros_like(acc)
            gg_ref[...] = jnp.zeros_like(gg_ref)

        x = x_ref[...]
        r = _rms(x)
        xn = x * r
        g = g_ref[...]
        h = (xn * g).astype(BF16)
        dval = jnp.concatenate([dup_ref[...], dus_ref[...], dq_ref[...]], axis=1)
        dh = jnp.zeros((cl, d), F32)
        for j in range(N_DEV):
            src = dval if j < half else dgate_ref[...]
            jj = j % half
            dp = src[:, jj * nb : (jj + 1) * nb].astype(BF16)
            dh = dh + _mm_nt(dp, w_ref[j])
            acc[j] += _mm_tn(h, dp)
        gg_ref[...] += jnp.sum(dh * xn, axis=0, keepdims=True)
        dxn = dh * g
        gx_ref[...] = dres_ref[...] + r * (dxn - xn * jnp.mean(dxn * xn, axis=-1, keepdims=True))

        @pl.when(i == n_tiles - 1)
        def _():
            cp = pltpu.make_async_copy(acc, gw_hbm, sem)
            cp.start()
            cp.wait()

    return pl.pallas_call(
        body,
        name="in_proj_bwd",
        grid=(n_tiles,),
        in_specs=[
            pl.BlockSpec((cl, d), lambda i: (i, 0)),
            pl.BlockSpec((cl, d), lambda i: (i, 0)),
            pl.BlockSpec((cl, pool_w), lambda i: (i, 0)),
            pl.BlockSpec((cl, ssm_w), lambda i: (i, 0)),
            pl.BlockSpec((cl, att_w), lambda i: (i, 0)),
            pl.BlockSpec((cl, mix), lambda i: (i, 0)),
            pl.BlockSpec((1, d), lambda i: (0, 0)),
            pl.BlockSpec((N_DEV, d, nb), lambda i: (0, 0, 0)),
        ],
        out_specs=[
            pl.BlockSpec((cl, d), lambda i: (i, 0)),
            pl.BlockSpec(memory_space=pl.ANY),
            pl.BlockSpec((1, d), lambda i: (0, 0)),
        ],
        out_shape=[
            jax.ShapeDtypeStruct((tokens, d), F32),
            jax.ShapeDtypeStruct((N_DEV, d, nb), F32),
            jax.ShapeDtypeStruct((1, d), F32),
        ],
        scratch_shapes=[pltpu.VMEM((N_DEV, d, nb), F32), pltpu.SemaphoreType.DMA],
        compiler_params=_params("arbitrary"),
    )(x2, dres, du_pool, du_ssm, dq, dgate, g_pre, w_in_g)


def _pool_geometry(seq, width):
    gw = width // len(POOL_WINDOWS)
    col = lax.broadcasted_iota(jnp.int32, (1, width), 1)
    win = jnp.full((1, width), float(POOL_WINDOWS[-1]), F32)
    for gi in range(len(POOL_WINDOWS) - 2, -1, -1):
        win = jnp.where(col < (gi + 1) * gw, float(POOL_WINDOWS[gi]), win)
    row = lax.broadcasted_iota(jnp.int32, (seq, width), 0)
    cnt = jnp.minimum((row + 1).astype(F32), win)
    return win, row, cnt


def _window_sums(a, row, win, seq, back):
    sums = []
    s, k = a, 1
    while k < POOL_WINDOWS[-1]:
        if back:
            shifted = jnp.where(row < seq - k, pltpu.roll(s, seq - k, 0), 0.0)
        else:
            shifted = jnp.where(row >= k, pltpu.roll(s, k, 0), 0.0)
        s = s + shifted
        k *= 2
        sums.append((k, s))
    out = sums[-1][1]
    for k, s in reversed(sums[:-1]):
        out = jnp.where(win <= float(k), s, out)
    return out


def _pool_fwd(u2, wp_blk, scale, batch, seq):
    width = u2.shape[1]

    def body(u_ref, w_ref, s_ref, y_ref):
        u = u_ref[...]
        win, row, cnt = _pool_geometry(seq, width)
        diff = _window_sums(u, row, win, seq, False) / cnt - u
        y_ref[...] = _mm(diff.astype(BF16), w_ref[...]) * s_ref[...]

    return pl.pallas_call(
        body,
        name="pool_fwd",
        grid=(batch,),
        in_specs=[
            pl.BlockSpec((seq, width), lambda b: (b, 0)),
            pl.BlockSpec((width, width), lambda b: (0, 0)),
            pl.BlockSpec((1, width), lambda b: (0, 0)),
        ],
        out_specs=pl.BlockSpec((seq, width), lambda b: (b, 0)),
        out_shape=jax.ShapeDtypeStruct(u2.shape, F32),
        compiler_params=_params("arbitrary"),
    )(u2, wp_blk, scale)


def _pool_bwd(u2, dy2, wp_blk, scale, batch, seq):
    width = u2.shape[1]

    def body(u_ref, dy_ref, w_ref, s_ref, du_ref, gw_ref, gs_ref):
        @pl.when(pl.program_id(0) == 0)
        def _():
            gw_ref[...] = jnp.zeros_like(gw_ref)
            gs_ref[...] = jnp.zeros_like(gs_ref)

        u = u_ref[...]
        dy = dy_ref[...]
        win, row, cnt = _pool_geometry(seq, width)
        diff = (_window_sums(u, row, win, seq, False) / cnt - u).astype(BF16)
        gs_ref[...] += jnp.sum(dy * _mm(diff, w_ref[...]), axis=0, keepdims=True)
        dys = (dy * s_ref[...]).astype(BF16)
        gw_ref[...] += _mm_tn(diff, dys)
        dd = _mm_nt(dys, w_ref[...])
        du_ref[...] = _window_sums(dd / cnt, row, win, seq, True) - dd

    return pl.pallas_call(
        body,
        name="pool_bwd",
        grid=(batch,),
        in_specs=[
            pl.BlockSpec((seq, width), lambda b: (b, 0)),
            pl.BlockSpec((seq, width), lambda b: (b, 0)),
            pl.BlockSpec((width, width), lambda b: (0, 0)),
            pl.BlockSpec((1, width), lambda b: (0, 0)),
        ],
        out_specs=[
            pl.BlockSpec((seq, width), lambda b: (b, 0)),
            pl.BlockSpec((width, width), lambda b: (0, 0)),
            pl.BlockSpec((1, width), lambda b: (0, 0)),
        ],
        out_shape=[
            jax.ShapeDtypeStruct(u2.shape, F32),
            jax.ShapeDtypeStruct((width, width), F32),
            jax.ShapeDtypeStruct((1, width), F32),
        ],
        compiler_params=_params("arbitrary"),
    )(u2, dy2, wp_blk, scale)


def _state_row(z, n_blocks):
    re = jnp.real(z).reshape(n_blocks, -1)
    im = jnp.imag(z).reshape(n_blocks, -1)
    return jnp.concatenate([re, im], axis=1).reshape(1, -1)


def _ssm_tables(a_re, a_im, log_dt, b_re, b_im, c_re, c_im):
    groups, n_state = a_re.shape
    ch = b_re.shape[2]
    nb = groups * ch // LANES
    gl = groups // nb
    lam = lax.complex(a_re, a_im)
    lam_bar = jnp.exp(lam * jnp.exp(log_dt)[:, None])
    b_bar = ((lam_bar - 1.0) / lam)[..., None] * lax.complex(b_re, b_im)
    eye = jnp.eye(gl, dtype=F32)

    def rows_to_state(t):
        return jnp.einsum("sgnc,gh->sgchn", t.reshape(nb, gl, n_state, ch), eye).reshape(nb, gl * ch, gl * n_state)

    def state_to_rows(t):
        return jnp.einsum("sgcn,gh->shngc", t.reshape(nb, gl, ch, n_state), eye).reshape(nb, gl * n_state, gl * ch)

    b_tab = jnp.concatenate([rows_to_state(jnp.real(b_bar)), rows_to_state(jnp.imag(b_bar))], axis=2)
    c_tab = jnp.concatenate([state_to_rows(c_re), -state_to_rows(c_im)], axis=1)
    return _state_row(lam_bar, nb), b_tab, c_tab


def _lam_power(a_re, a_im, log_dt, power, scale, n_blocks):
    return _state_row(scale * jnp.exp(lax.complex(a_re, a_im) * jnp.exp(log_dt)[:, None] * power), n_blocks)


def _state_blocks(s2, n_blocks, width):
    half = s2 // n_blocks // 2
    assert half % width == 0
    return [(b * 2 * half + o, b * 2 * half + half + o) for b in range(n_blocks) for o in range(0, half, width)]


def _scan(src_ref, dst_ref, st_ref, lam8_ref, n_groups, s, n_blocks, reverse, store):
    lb = 512
    for re0, im0 in _state_blocks(2 * s, n_blocks, lb):
        cr, ci = pl.ds(re0, lb), pl.ds(im0, lb)
        lr = lam8_ref[:, cr]
        li = -lam8_ref[:, ci] if reverse else lam8_ref[:, ci]

        def step(i, carry, cr=cr, ci=ci, lr=lr, li=li):
            hr, hi = carry
            grp = n_groups - 1 - i if reverse else i
            rows = pl.ds(pl.multiple_of(grp * SUBLANES, SUBLANES), SUBLANES)
            nr = lr * hr - li * hi + src_ref[rows, cr]
            ni = lr * hi + li * hr + src_ref[rows, ci]
            if store:
                dst_ref[rows, cr] = nr
                dst_ref[rows, ci] = ni
            return nr, ni

        hr, hi = lax.fori_loop(0, n_groups, step, (st_ref[:, cr], st_ref[:, ci]), unroll=2)
        st_ref[:, cr] = hr
        st_ref[:, ci] = hi


def _pack_state(re, im):
    hi = lax.bitcast_convert_type(re.astype(BF16).astype(F32), jnp.uint32)
    lo = lax.bitcast_convert_type(im.astype(BF16).astype(F32), jnp.uint32)
    return hi | (lo >> 16)


def _unpack_state(word):
    re = lax.bitcast_convert_type(word & jnp.uint32(0xFFFF0000), F32)
    im = lax.bitcast_convert_type(word << 16, F32)
    return re, im


def _scan_adjoint(dh_ref, hprev_ref, group0, stg_ref, acc_ref, lam8_ref, n_groups, s, n_blocks):
    lb = 512
    half = s // n_blocks
    for re0, im0 in _state_blocks(2 * s, n_blocks, lb):
        cr, ci = pl.ds(re0, lb), pl.ds(im0, lb)
        ch = pl.ds(re0 // (2 * half) * half + re0 % (2 * half), lb)
        lr, li = lam8_ref[:, cr], -lam8_ref[:, ci]

        def step(i, carry, cr=cr, ci=ci, ch=ch, lr=lr, li=li):
            gr, gi, ar, ai = carry
            grp = n_groups - 1 - i
            rows = pl.ds(pl.multiple_of(grp * SUBLANES, SUBLANES), SUBLANES)
            ngr = lr * gr - li * gi + dh_ref[rows, cr]
            ngi = lr * gi + li * gr + dh_ref[rows, ci]
            hr, hi = _unpack_state(hprev_ref[pl.ds(pl.multiple_of((group0 + grp) * SUBLANES, SUBLANES), SUBLANES), ch])
            ar = ar + hr * ngr + hi * ngi
            ai = ai + hr * ngi - hi * ngr
            dh_ref[rows, cr] = ngr
            dh_ref[rows, ci] = ngi
            return ngr, ngi, ar, ai

        init = (stg_ref[:, cr], stg_ref[:, ci], acc_ref[:, cr], acc_ref[:, ci])
        gr, gi, ar, ai = lax.fori_loop(0, n_groups, step, init)
        stg_ref[:, cr] = gr
        stg_ref[:, ci] = gi
        acc_ref[:, cr] = ar
        acc_ref[:, ci] = ai


def _chunk_starts(st_ref, init_ref, lcl_ref, s, n_blocks):
    w = s // n_blocks
    init_ref[0:1, :] = jnp.zeros((1, 2 * s), F32)
    for re0, im0 in _state_blocks(2 * s, n_blocks, w):
        re, im = pl.ds(re0, w), pl.ds(im0, w)
        ar, ai = lcl_ref[:, re], lcl_ref[:, im]
        cr = jnp.zeros((1, w), F32)
        ci = jnp.zeros((1, w), F32)
        for k in range(1, NCH):
            cr, ci = (ar * cr - ai * ci + st_ref[k - 1 : k, re], ar * ci + ai * cr + st_ref[k - 1 : k, im])
            init_ref[k : k + 1, re] = cr
            init_ref[k : k + 1, im] = ci


def _chunk_starts_adjoint(stg_ref, initg_ref, lcl_ref, s, n_blocks):
    w = s // n_blocks
    initg_ref[NCH - 1 : NCH, :] = jnp.zeros((1, 2 * s), F32)
    for re0, im0 in _state_blocks(2 * s, n_blocks, w):
        re, im = pl.ds(re0, w), pl.ds(im0, w)
        ar, ai = lcl_ref[:, re], -lcl_ref[:, im]
        gr = jnp.zeros((1, w), F32)
        gi = jnp.zeros((1, w), F32)
        for k in range(NCH - 2, -1, -1):
            gr, gi = (stg_ref[k + 1 : k + 2, re] + ar * gr - ai * gi, stg_ref[k + 1 : k + 2, im] + ar * gi + ai * gr)
            initg_ref[k : k + 1, re] = gr
            initg_ref[k : k + 1, im] = gi


def _ssm_rows(seq):
    rows = min(512, seq // 2)
    assert seq % rows == 0 and rows % SUBLANES == 0
    return rows


def _chunk_copies(hbm_ref, b, cm_ref, sems, to_cm):
    cl = cm_ref.shape[0]
    copies = []
    for k in range(NCH):
        nat, cm = hbm_ref.at[b, pl.ds(k * cl, cl), :], cm_ref.at[:, k, :]
        src, dst = (nat, cm) if to_cm else (cm, nat)
        copies.append(pltpu.make_async_copy(src, dst, sems.at[k]))
    return copies


def _blockwise(fn, n_blocks):
    return jnp.concatenate([fn(b) for b in range(n_blocks)], axis=1)


def _ssm_fwd(u, b_tab, c_tab, lam8, lcl, d_skip, w_glu):
    batch, seq, width = u.shape
    s = lam8.shape[1] // 2
    nb = b_tab.shape[0]
    sb = 2 * s // nb
    cl = seq // NCH
    rows = _ssm_rows(seq)
    n_tiles = seq // rows
    n_groups = rows // SUBLANES

    def body(u_hbm, b_ref, c_ref, lam_ref, lcl_ref, d_ref, wg_ref, y_hbm, pre_ref, z_ref, init_ref,
             u_cm, y_cm, bu_all, st, sems):
        b, ph, t = pl.program_id(0), pl.program_id(1), pl.program_id(2)
        tile_groups = pl.ds(pl.multiple_of(t * n_groups, n_groups), n_groups)

        @pl.when((ph == 0) & (t == 0))
        def _():
            loads = _chunk_copies(u_hbm, b, u_cm, sems, True)
            for cp in loads:
                cp.start()
            st[...] = jnp.zeros_like(st)
            for cp in loads:
                cp.wait()

        @pl.when((ph == 1) & (t == 0))
        def _():
            st[...] = init_ref[...]

        u_t = u_cm[tile_groups].reshape(rows, width)
        bu = bu_all.at[pl.ds(pl.multiple_of(t * rows, rows), rows)]

        @pl.when(ph == 0)
        def _():
            u_b = u_t.astype(BF16)
            for blk in range(nb):
                bu[:, blk * sb : (blk + 1) * sb] = _mm(u_b[:, blk * LANES : (blk + 1) * LANES], b_ref[blk])
            _scan(bu, bu, st, lam_ref, n_groups, s, nb, False, False)

        @pl.when((ph == 0) & (t == n_tiles - 1))
        def _():
            _chunk_starts(st, init_ref, lcl_ref, s, nb)

        @pl.when(ph == 1)
        def _():
            _scan(bu, bu, st, lam_ref, n_groups, s, nb, False, True)
            hs = lambda blk: _mm(bu[:, blk * sb : (blk + 1) * sb].astype(BF16), c_ref[blk])
            pre = _blockwise(hs, nb) + d_ref[...] * u_t
            z = _mm(_gelu(pre).astype(BF16), wg_ref[...])
            pre_ref[...] = pre
            z_ref[...] = z
            y = z[:, 0:width] * jax.nn.sigmoid(z[:, width : 2 * width])
            y_cm[tile_groups] = y.reshape(n_groups, SUBLANES, width)

        @pl.when((ph == 1) & (t == n_tiles - 1))
        def _():
            stores = _chunk_copies(y_hbm, b, y_cm, sems, False)
            for cp in stores:
                cp.start()
            for cp in stores:
                cp.wait()

    out_tile = lambda b, ph, t: (b, t * ph, 0)
    full = lambda a: pl.BlockSpec(a.shape, lambda b, ph, t: (0,) * a.ndim)
    hbm = pl.BlockSpec(memory_space=pl.ANY)
    return pl.pallas_call(
        body,
        name="ssm_fwd",
        grid=(batch, 2, n_tiles),
        in_specs=[hbm, full(b_tab), full(c_tab), full(lam8), full(lcl), full(d_skip), full(w_glu)],
        out_specs=[
            hbm,
            pl.BlockSpec((None, rows, width), out_tile),
            pl.BlockSpec((None, rows, 2 * width), out_tile),
            pl.BlockSpec((None, SUBLANES, 2 * s), lambda b, ph, t: (b, 0, 0)),
        ],
        out_shape=[
            jax.ShapeDtypeStruct((batch, seq, width), F32),
            jax.ShapeDtypeStruct((batch, seq, width), F32),
            jax.ShapeDtypeStruct((batch, seq, 2 * width), F32),
            jax.ShapeDtypeStruct((batch, SUBLANES, 2 * s), F32),
        ],
        scratch_shapes=[
            pltpu.VMEM((cl, NCH, width), F32),
            pltpu.VMEM((cl, NCH, width), F32),
            pltpu.VMEM((seq, 2 * s), F32),
            pltpu.VMEM((SUBLANES, 2 * s), F32),
            pltpu.SemaphoreType.DMA((NCH,)),
        ],
        compiler_params=_params("arbitrary", "arbitrary", "arbitrary"),
    )(u, b_tab, c_tab, lam8, lcl, d_skip, w_glu)


def _ssm_bwd(u, pre_p, z_p, dy, init, b_tab, b_tab_t, c_tab_t, lam8, lcl, d_skip, w_glu, ready):
    batch, seq, width = u.shape
    nr = len(ready)
    s = lam8.shape[1] // 2
    nb = b_tab.shape[0]
    sb = 2 * s // nb
    cl = seq // NCH
    rows = _ssm_rows(seq)
    n_tiles = seq // rows
    n_groups = rows // SUBLANES

    def body(u_hbm, pre_ref, z_ref, dy_hbm, init_ref, b_ref, bt_ref, ct_ref, lam_ref, lcl_ref, d_ref, wg_ref, *rest):
        ready_hbm, rest = rest[:nr], rest[nr:]
        du_hbm, gb_ref, gc_ref, gwg_ref, gd_ref, glam_ref = rest[:6]
        landed_hbm, rest = rest[6 : 6 + nr], rest[6 + nr :]
        u_cm, dy_cm, work, hs_all, dpre_all, st, stg, initg, acc, sems, send_sems, recv_sems = rest
        b, ph, t = pl.program_id(0), pl.program_id(1), pl.program_id(2)
        half = s // nb
        exchange_start, exchange_finish = _exchange_steps(
            nr, lambda a, slot: ready_hbm[a].at[slot], landed_hbm, send_sems, recv_sems)
        first = (b == 0) & (ph == 0) & (t == 0)
        last = (b == batch - 1) & (ph == 2) & (t == n_tiles - 1)
        tile = jnp.where(ph == 0, t, n_tiles - 1 - t)
        tile_rows = pl.ds(pl.multiple_of(tile * rows, rows), rows)
        tile_groups = pl.ds(pl.multiple_of(tile * n_groups, n_groups), n_groups)
        lanes = lambda blk: slice(blk * LANES, (blk + 1) * LANES)
        states = lambda blk: slice(blk * sb, (blk + 1) * sb)

        @pl.when(first)
        def _():
            exchange_start()
            acc[...] = jnp.zeros_like(acc)
            gb_ref[...] = jnp.zeros_like(gb_ref)
            gc_ref[...] = jnp.zeros_like(gc_ref)
            gwg_ref[...] = jnp.zeros_like(gwg_ref)
            gd_ref[...] = jnp.zeros_like(gd_ref)

        @pl.when((ph == 0) & (t == 0))
        def _():
            loads = _chunk_copies(u_hbm, b, u_cm, sems.at[0], True) + _chunk_copies(dy_hbm, b, dy_cm, sems.at[1], True)
            for cp in loads:
                cp.start()
            st[...] = init_ref[...]
            for blk in range(nb):
                entry = init_ref[:, states(blk)]
                hs_all[0:SUBLANES, blk * half : (blk + 1) * half] = _pack_state(entry[:, 0:half], entry[:, half : 2 * half])
            for cp in loads:
                cp.wait()

        u_t = u_cm[tile_groups].reshape(rows, width)
        u_b = u_t.astype(BF16)

        @pl.when(ph == 0)
        def _():
            for blk in range(nb):
                work[:, states(blk)] = _mm(u_b[:, lanes(blk)], b_ref[blk])
            _scan(work, work, st, lam_ref, n_groups, s, nb, False, True)
            z = z_ref[...]
            dy_t = dy_cm[tile_groups].reshape(rows, width)
            pre = pre_ref[...]
            z1, sig = z[:, 0:width], jax.nn.sigmoid(z[:, width : 2 * width])
            dz = jnp.concatenate([dy_t * sig, dy_t * z1 * sig * (1.0 - sig)], axis=1).astype(BF16)
            gwg_ref[...] += _mm_tn(_gelu(pre).astype(BF16), dz)
            dpre = _mm_nt(dz, wg_ref[...]) * _gelu_grad(pre)
            dpre_all[tile_rows, :] = dpre
            gd_ref[...] += jnp.sum(dpre * u_t, axis=0, keepdims=True)
            dpre_b = dpre.astype(BF16)
            kept = pl.ds(pl.multiple_of(tile * rows + SUBLANES, SUBLANES), rows)
            for blk in range(nb):
                hs = work[:, states(blk)]
                gc_ref[blk] += _mm_tn(hs.astype(BF16), dpre_b[:, lanes(blk)])
                hs_all[kept, blk * half : (blk + 1) * half] = _pack_state(hs[:, 0:half], hs[:, half : 2 * half])

        @pl.when(ph >= 1)
        def _():
            dpre_b = dpre_all[tile_rows, :].astype(BF16)
            for blk in range(nb):
                work[:, states(blk)] = _mm(dpre_b[:, lanes(blk)], ct_ref[blk])

        @pl.when(ph == 1)
        def _():
            @pl.when(t == 0)
            def _():
                stg[...] = jnp.zeros_like(stg)

            _scan(work, work, stg, lam_ref, n_groups, s, nb, True, False)

            @pl.when(t == n_tiles - 1)
            def _():
                _chunk_starts_adjoint(stg, initg, lcl_ref, s, nb)

        @pl.when(ph == 2)
        def _():
            @pl.when(t == 0)
            def _():
                stg[...] = initg[...]

            _scan_adjoint(work, hs_all, tile * n_groups, stg, acc, lam_ref, n_groups, s, nb)
            du = lambda blk: _mm(work[:, states(blk)].astype(BF16), bt_ref[blk])
            du_t = _blockwise(du, nb) + dpre_all[tile_rows, :] * d_ref[...]
            dy_cm[tile_groups] = du_t.reshape(n_groups, SUBLANES, width)
            for blk in range(nb):
                gb_ref[blk] += _mm_tn(u_b[:, lanes(blk)], work[:, states(blk)].astype(BF16))

            @pl.when(t == n_tiles - 1)
            def _():
                stores = _chunk_copies(du_hbm, b, dy_cm, sems.at[0], False)
                for cp in stores:
                    cp.start()
                for cp in stores:
                    cp.wait()

        @pl.when(last)
        def _():
            glam_ref[...] = jnp.sum(acc[...], axis=0, keepdims=True)
            exchange_finish()

    def tile(b, ph, t):
        return (b, jnp.where(ph == 0, t, n_tiles - 1 - t), 0)

    full = lambda a: pl.BlockSpec(a.shape, lambda b, ph, t: (0,) * a.ndim)
    hbm = pl.BlockSpec(memory_space=pl.ANY)
    res = pl.pallas_call(
        body,
        name="ssm_bwd",
        grid=(batch, 3, n_tiles),
        in_specs=[
            hbm,
            pl.BlockSpec((None, rows, width), tile),
            pl.BlockSpec((None, rows, 2 * width), tile),
            hbm,
            pl.BlockSpec((None, SUBLANES, 2 * s), lambda b, ph, t: (b, 0, 0)),
            full(b_tab), full(b_tab_t), full(c_tab_t), full(lam8), full(lcl), full(d_skip), full(w_glu),
        ]
        + [hbm] * nr,
        out_specs=[
            hbm,
            full(b_tab), full(b_tab_t), full(w_glu), full(d_skip),
            pl.BlockSpec((1, 2 * s), lambda b, ph, t: (0, 0)),
        ]
        + [hbm] * nr,
        out_shape=[
            jax.ShapeDtypeStruct((batch, seq, width), F32),
            jax.ShapeDtypeStruct(b_tab.shape, F32),
            jax.ShapeDtypeStruct(b_tab_t.shape, F32),
            jax.ShapeDtypeStruct(w_glu.shape, F32),
            jax.ShapeDtypeStruct(d_skip.shape, F32),
            jax.ShapeDtypeStruct((1, 2 * s), F32),
        ]
        + [jax.ShapeDtypeStruct(a.shape, F32) for a in ready],
        scratch_shapes=[
            pltpu.VMEM((cl, NCH, width), F32),
            pltpu.VMEM((cl, NCH, width), F32),
            pltpu.VMEM((rows, 2 * s), F32),
            pltpu.VMEM((seq + SUBLANES, s), jnp.uint32),
            pltpu.VMEM((seq, width), F32),
        ]
        + [pltpu.VMEM((SUBLANES, 2 * s), F32)] * 4
        + [pltpu.SemaphoreType.DMA((2, NCH)), pltpu.SemaphoreType.DMA((nr, 7)), pltpu.SemaphoreType.DMA((nr, 7))],
        compiler_params=_params("arbitrary", "arbitrary", "arbitrary"),
    )(u, pre_p, z_p, dy, init, b_tab, b_tab_t, c_tab_t, lam8, lcl, d_skip, w_glu, *ready)
    return res[:6], res[6:]


def _kv_fwd(mem, g_mem, w_kv):
    batch, n_mem, d = mem.shape
    kvw = w_kv.shape[1]

    def body(mem_ref, g_ref, w_ref, kv_ref):
        m = mem_ref[...]
        kv_ref[...] = _mm((m * _rms(m) * g_ref[...]).astype(BF16), w_ref[...])

    return pl.pallas_call(
        body,
        name="kv_fwd",
        grid=(batch,),
        in_specs=[
            pl.BlockSpec((None, n_mem, d), lambda b: (b, 0, 0)),
            pl.BlockSpec((1, d), lambda b: (0, 0)),
            pl.BlockSpec((d, kvw), lambda b: (0, 0)),
        ],
        out_specs=pl.BlockSpec((None, n_mem, kvw), lambda b: (b, 0, 0)),
        out_shape=jax.ShapeDtypeStruct((batch, n_mem, kvw), F32),
        compiler_params=_params("arbitrary"),
    )(mem, g_mem, w_kv)


def _kv_bwd(mem, dkv, g_mem, w_kv):
    batch, n_mem, d = mem.shape
    kvw = w_kv.shape[1]

    def body(mem_ref, dkv_ref, g_ref, w_ref, gw_ref, gg_ref):
        @pl.when(pl.program_id(0) == 0)
        def _():
            gw_ref[...] = jnp.zeros_like(gw_ref)
            gg_ref[...] = jnp.zeros_like(gg_ref)

        m = mem_ref[...]
        mn = m * _rms(m)
        dkv_b = dkv_ref[...].astype(BF16)
        gw_ref[...] += _mm_tn((mn * g_ref[...]).astype(BF16), dkv_b)
        gg_ref[...] += jnp.sum(_mm_nt(dkv_b, w_ref[...]) * mn, axis=0, keepdims=True)

    return pl.pallas_call(
        body,
        name="kv_bwd",
        grid=(batch,),
        in_specs=[
            pl.BlockSpec((None, n_mem, d), lambda b: (b, 0, 0)),
            pl.BlockSpec((None, n_mem, kvw), lambda b: (b, 0, 0)),
            pl.BlockSpec((1, d), lambda b: (0, 0)),
            pl.BlockSpec((d, kvw), lambda b: (0, 0)),
        ],
        out_specs=[pl.BlockSpec((d, kvw), lambda b: (0, 0)), pl.BlockSpec((1, d), lambda b: (0, 0))],
        out_shape=[jax.ShapeDtypeStruct((d, kvw), F32), jax.ShapeDtypeStruct((1, d), F32)],
        compiler_params=_params("arbitrary"),
    )(mem, dkv, g_mem, w_kv)


def _tail(x2, target2, gate, y_pool, y_ssm, q, kv, w_out, g_post):
    tokens, d = x2.shape
    pool_w, ssm_w, att_w, mix = y_pool.shape[1], y_ssm.shape[1], q.shape[1], gate.shape[1]
    batch, n_mem, kvw = kv.shape
    hd = att_w // MEM_HEADS
    cl = _token_tile(tokens // batch, 512)
    n_tiles = tokens // cl
    per_seq = tokens // batch // cl
    n_parts = 1
    qk_scale = hd**-0.5

    def body(x_ref, tg_ref, gate_ref, yp_ref, ys_ref, q_ref, kv_ref, w_ref, g_ref,
             dres_ref, dgate_ref, dyp_ref, dys_ref, dq_ref, dkv_ref, gw_hbm, gg_ref, loss_ref, acc, sem):
        i = pl.program_id(0)

        @pl.when(i == 0)
        def _():
            acc[...] = jnp.zeros_like(acc)
            gg_ref[...] = jnp.zeros_like(gg_ref)
            loss_ref[...] = jnp.zeros_like(loss_ref)

        @pl.when(i % per_seq == 0)
        def _():
            dkv_ref[...] = jnp.zeros_like(dkv_ref)

        k = kv_ref[:, 0:att_w].astype(BF16)
        v = kv_ref[:, att_w : 2 * att_w].astype(BF16)
        lane = lax.broadcasted_iota(jnp.int32, (1, att_w), 1)
        heads = [(lane >= h * hd) & (lane < (h + 1) * hd) for h in range(MEM_HEADS)]
        g = g_ref[...]

        def part(rows):
            n_rows = rows.stop - rows.start
            q = q_ref[rows, :]
            probs, q_heads = [], []
            att = jnp.zeros((n_rows, att_w), F32)
            for mask in heads:
                qh = jnp.where(mask, q, 0.0).astype(BF16)
                sc = _mm_nt(qh, k) * qk_scale
                e = jnp.exp(sc - jnp.max(sc, axis=-1, keepdims=True))
                p = e * (1.0 / jnp.sum(e, axis=-1, keepdims=True))
                att = att + jnp.where(mask, _mm(p.astype(BF16), v), 0.0)
                probs.append(p)
                q_heads.append(qh)

            ycat = jnp.concatenate([yp_ref[rows, :], ys_ref[rows, :], att], axis=1)
            gate = gate_ref[rows, :]
            sig = jax.nn.sigmoid(gate)
            silu = gate * sig
            yg = (ycat * silu).astype(BF16)
            out = _mm(yg, w_ref[...])
            r = _rms(out)
            on = out * r
            err = x_ref[rows, :] + on * g - tg_ref[rows, :]
            loss_ref[...] += 0.5 * jnp.sum(jnp.mean(err * err, axis=-1, keepdims=True), axis=0, keepdims=True)
            dres = err * (1.0 / d)
            dres_ref[rows, :] = dres
            gg_ref[...] += jnp.sum(dres * on, axis=0, keepdims=True)
            don = dres * g
            dout = (r * (don - on * jnp.mean(don * on, axis=-1, keepdims=True))).astype(BF16)
            acc[...] += _mm_tn(yg, dout)
            dyg = _mm_nt(dout, w_ref[...])
            dgate_ref[rows, :] = dyg * ycat * (sig * (1.0 + gate * (1.0 - sig)))
            dycat = dyg * silu
            dyp_ref[rows, :] = dycat[:, 0:pool_w]
            dys_ref[rows, :] = dycat[:, pool_w : pool_w + ssm_w]
            datt = dycat[:, pool_w + ssm_w : mix]

            dq = jnp.zeros((n_rows, att_w), F32)
            dk = jnp.zeros((n_mem, att_w), F32)
            dv = jnp.zeros((n_mem, att_w), F32)
            for mask, p, qh in zip(heads, probs, q_heads):
                doh = jnp.where(mask, datt, 0.0).astype(BF16)
                dp = _mm_nt(doh, v)
                ds = (p * (dp - jnp.sum(p * dp, axis=-1, keepdims=True)) * qk_scale).astype(BF16)
                dq = dq + jnp.where(mask, _mm(ds, k), 0.0)
                dk = dk + _mm_tn(ds, qh)
                dv = dv + _mm_tn(p.astype(BF16), doh)
            dq_ref[rows, :] = dq
            dkv_ref[:, 0:att_w] += dk
            dkv_ref[:, att_w : 2 * att_w] += dv

        for first_row in range(0, cl, cl // n_parts):
            part(slice(first_row, first_row + cl // n_parts))

        @pl.when(i == n_tiles - 1)
        def _():
            cp = pltpu.make_async_copy(acc, gw_hbm, sem)
            cp.start()
            cp.wait()

    tok = lambda w: pl.BlockSpec((cl, w), lambda i: (i, 0))
    chunked = tok(ssm_w)
    per_batch = pl.BlockSpec((None, n_mem, kvw), lambda i: (i // per_seq, 0, 0))
    return pl.pallas_call(
        body,
        name="tail",
        grid=(n_tiles,),
        in_specs=[
            tok(d), tok(d), tok(mix), tok(pool_w), chunked, tok(att_w), per_batch,
            pl.BlockSpec((mix, d), lambda i: (0, 0)),
            pl.BlockSpec((1, d), lambda i: (0, 0)),
        ],
        out_specs=[
            tok(d), tok(mix), tok(pool_w), chunked, tok(att_w), per_batch,
            pl.BlockSpec(memory_space=pl.ANY),
            pl.BlockSpec((1, d), lambda i: (0, 0)),
            pl.BlockSpec((1, 1), lambda i: (0, 0)),
        ],
        out_shape=[
            jax.ShapeDtypeStruct((tokens, d), F32),
            jax.ShapeDtypeStruct((tokens, mix), F32),
            jax.ShapeDtypeStruct((tokens, pool_w), F32),
            jax.ShapeDtypeStruct((tokens, ssm_w), F32),
            jax.ShapeDtypeStruct((tokens, att_w), F32),
            jax.ShapeDtypeStruct(kv.shape, F32),
            jax.ShapeDtypeStruct((mix, d), F32),
            jax.ShapeDtypeStruct((1, d), F32),
            jax.ShapeDtypeStruct((1, 1), F32),
        ],
        scratch_shapes=[pltpu.VMEM((mix, d), F32), pltpu.SemaphoreType.DMA],
        compiler_params=_params("arbitrary"),
    )(x2, target2, gate, y_pool, y_ssm, q, kv, w_out, g_post)


def _pack(arrays):
    flat = jnp.concatenate([a.reshape(-1) for a in arrays])
    rows = -(-flat.size // (SUBLANES * LANES)) * SUBLANES
    return jnp.pad(flat, (0, rows * LANES - flat.size)).reshape(rows, LANES)


def _unpack(packed, like):
    flat, out, at = packed.reshape(-1), [], 0
    for a in like:
        out.append(flat[at : at + a.size].reshape(a.shape))
        at += a.size
    return out


def kernel(x, mem, g_pre, w_in, w_pool, pool_scale, a_re, a_im, log_dt, b_re, b_im, c_re, c_im, d_skip, w_glu, g_mem, w_kv, w_out, g_post, loss_target, m_g_pre, m_w_in, m_w_pool, m_pool_scale, m_a_re, m_a_im, m_log_dt, m_b_re, m_b_im, m_c_re, m_c_im, m_d_skip, m_w_glu, m_g_mem, m_w_kv, m_w_out, m_g_post, v_g_pre, v_w_in, v_w_pool, v_pool_scale, v_a_re, v_a_im, v_log_dt, v_b_re, v_b_im, v_c_re, v_c_im, v_d_skip, v_w_glu, v_g_mem, v_w_kv, v_w_out, v_g_post):
    batch, seq, d = x.shape
    cl = seq // NCH
    pool_w, ssm_w = pool_scale.shape[1], d_skip.shape[1]
    att_w = w_kv.shape[2] // 2
    tokens = batch * seq
    x2 = x.reshape(tokens, d)
    target2 = loss_target.reshape(tokens, d)

    (w_in_g,), shards = _gather_weights([w_in[0]], [w_out[0], w_kv[0], w_glu[0]])

    wp_blk = jax.scipy.linalg.block_diag(*w_pool[0]).astype(BF16)
    ssm_params = (a_re[0], a_im[0], log_dt[0], b_re[0], b_im[0], c_re[0], c_im[0])
    (lam_row, b_tab, c_tab), tables_vjp = jax.vjp(_ssm_tables, *ssm_params)
    nb = b_tab.shape[0]
    lam8 = jnp.broadcast_to(lam_row, (SUBLANES, lam_row.shape[1]))
    lcl = _lam_power(a_re[0], a_im[0], log_dt[0], float(cl), 1.0, nb)
    b_bf, c_bf = b_tab.astype(BF16), c_tab.astype(BF16)

    (u_pool, u_ssm, q, gate), (w_out_g, w_kv_g, w_glu_g) = _in_proj(x2, g_pre, w_in_g, shards, pool_w, ssm_w, att_w)
    w_out_f = w_out_g.reshape(N_DEV * w_out_g.shape[1], w_out_g.shape[2])
    w_kv_f = w_kv_g.reshape(N_DEV * w_kv_g.shape[1], w_kv_g.shape[2])
    w_glu_f = w_glu_g.transpose(1, 0, 2).reshape(w_glu_g.shape[1], N_DEV * w_glu_g.shape[2])
    y_pool = _pool_fwd(u_pool, wp_blk, pool_scale, batch, seq)
    u_ssm = u_ssm.reshape(batch, seq, ssm_w)
    y_ssm, pre_ssm, z_ssm, init_ssm = _ssm_fwd(u_ssm, b_bf, c_bf, lam8, lcl, d_skip, w_glu_f)
    kv = _kv_fwd(mem, g_mem, w_kv_f)

    dres, dgate, dy_pool, dy_ssm, dq, dkv, gw_out, gg_post, loss_part = _tail(
        x2, target2, gate, y_pool, y_ssm.reshape(tokens, ssm_w), q, kv, w_out_f, g_post)

    gw_kv, gg_mem = _kv_bwd(mem, dkv, g_mem, w_kv_f)
    du_pool, gwp_dense, g_scale = _pool_bwd(u_pool, dy_pool, wp_blk, pool_scale, batch, seq)
    gw_kv8 = gw_kv.reshape(N_DEV, -1, gw_kv.shape[1])
    gw_out8 = gw_out.reshape(N_DEV, -1, gw_out.shape[1])
    (du_ssm, gb_tab, gc_tab, gw_glu, gd_skip, glam), (kv_landed, out_landed) = _ssm_bwd(
        u_ssm, pre_ssm, z_ssm, dy_ssm.reshape(batch, seq, ssm_w), init_ssm, b_bf, b_bf.transpose(0, 2, 1),
        c_bf.transpose(0, 2, 1), lam8, lcl, d_skip, w_glu_f, [gw_kv8, gw_out8])
    grad_x2, gw_in, gg_pre = _in_proj_bwd(
        x2, dres, du_pool, du_ssm.reshape(tokens, ssm_w), dq, dgate, g_pre, w_in_g)

    gw = pool_w // len(POOL_WINDOWS)
    gw_pool = jnp.stack([gwp_dense[i * gw : (i + 1) * gw, i * gw : (i + 1) * gw] for i in range(len(POOL_WINDOWS))])
    g_ssm = tables_vjp((glam, gb_tab, gc_tab))

    small_w = [g_pre, w_pool, pool_scale, a_re, a_im, log_dt, b_re, b_im, c_re, c_im, d_skip, g_mem, g_post]
    small_m = [m_g_pre, m_w_pool, m_pool_scale, m_a_re, m_a_im, m_log_dt, m_b_re, m_b_im, m_c_re, m_c_im, m_d_skip, m_g_mem, m_g_post]
    small_v = [v_g_pre, v_w_pool, v_pool_scale, v_a_re, v_a_im, v_log_dt, v_b_re, v_b_im, v_c_re, v_c_im, v_d_skip, v_g_mem, v_g_post]
    small_g = [gg_pre, gw_pool, g_scale, *g_ssm, gd_skip, gg_mem, gg_post]
    big, small_sum = _reduce_all(
        [gw_in, gw_glu.reshape(ssm_w, N_DEV, -1).transpose(1, 0, 2)],
        [w_in[0], w_glu[0]], [m_w_in[0], m_w_glu[0]], [v_w_in[0], v_w_glu[0]],
        _pack(small_g + [loss_part]),
        [(gw_kv8, kv_landed, w_kv[0], m_w_kv[0], v_w_kv[0]), (gw_out8, out_landed, w_out[0], m_w_out[0], v_w_out[0])])
    big = {name: tuple(t[None] for t in res) for name, res in zip(["w_in", "w_glu", "w_kv", "w_out"], big)}

    flat2 = lambda a: a.reshape(-1, a.shape[-1])
    sg = _unpack(small_sum, [flat2(a) for a in small_w] + [loss_part])
    loss = sg[-1].reshape(())
    updates = _adamw_small(sg[:-1], [flat2(a) for a in small_w], [flat2(a) for a in small_m], [flat2(a) for a in small_v])
    sg = [g.reshape(a.shape) for g, a in zip(sg[:-1], small_w)]
    sd, sm, sv = ([u[kind].reshape(a.shape) for u, a in zip(updates, small_w)] for kind in range(3))

    order = ["g_pre", "w_in", "w_pool", "pool_scale", "a_re", "a_im", "log_dt", "b_re", "b_im", "c_re", "c_im",
             "d_skip", "w_glu", "g_mem", "w_kv", "w_out", "g_post"]
    small_names = ["g_pre", "w_pool", "pool_scale", "a_re", "a_im", "log_dt", "b_re", "b_im", "c_re", "c_im",
                   "d_skip", "g_mem", "g_post"]
    outs = [[], [], [], []]
    for name in order:
        if name in big:
            parts = big[name]
        else:
            j = small_names.index(name)
            parts = (sg[j], sd[j], sm[j], sv[j])
        for kind in range(4):
            outs[kind].append(parts[kind])
    return (loss, grad_x2.reshape(batch, seq, d), *outs[0], *outs[1], *outs[2], *outs[3])
```

```python
import functools
import math

import jax
import jax.numpy as jnp
from jax import lax
from jax.experimental import pallas as pl
from jax.experimental.pallas import tpu as pltpu

F32 = jnp.float32
BF16 = jnp.bfloat16
MESH = pl.DeviceIdType.MESH

N_DEV = 8
NCH = 8
SUBLANES = 8
LANES = 128
VMEM_LIMIT = 56 * 1024 * 1024

EPS = 1e-6
POOL_WINDOWS = (2, 4, 8, 16)
MEM_HEADS = 4
SSM_GROUP = 16
SSM_N = 64
ADAM_LR, ADAM_B1, ADAM_B2, ADAM_EPS, ADAM_WD, ADAM_STEP = 0.001, 0.9, 0.999, 1e-08, 0.01, 10


def _mm(a, b):
    return jnp.dot(a, b, preferred_element_type=F32)


def _mm_nt(a, b):
    return lax.dot_general(a, b, (((1,), (1,)), ((), ())), preferred_element_type=F32)


def _mm_tn(a, b):
    return lax.dot_general(a, b, (((0,), (0,)), ((), ())), preferred_element_type=F32)


def _params(*sem):
    return pltpu.CompilerParams(dimension_semantics=sem or None, vmem_limit_bytes=VMEM_LIMIT)


def _adamw(w, g, m, v):
    m = ADAM_B1 * m + (1.0 - ADAM_B1) * g
    v = ADAM_B2 * v + (1.0 - ADAM_B2) * (g * g)
    m_hat = m / (1.0 - ADAM_B1**ADAM_STEP)
    v_hat = v / (1.0 - ADAM_B2**ADAM_STEP)
    delta = -ADAM_LR * (m_hat / (jnp.sqrt(v_hat) + ADAM_EPS) + ADAM_WD * w)
    return delta, m, v


def _gelu(x):
    k = math.sqrt(2.0 / math.pi)
    return 0.5 * x * (1.0 + jnp.tanh(k * (x + 0.044715 * x * x * x)))


def _gelu_grad(x):
    k = math.sqrt(2.0 / math.pi)
    th = jnp.tanh(k * (x + 0.044715 * x * x * x))
    return 0.5 * (1.0 + th) + 0.5 * x * (1.0 - th * th) * (k * (1.0 + 3.0 * 0.044715 * x * x))


def _place():
    return lax.axis_index("x"), lax.axis_index("y"), lax.axis_index("c")


def _gather_steps(ins, outs, send_sems, recv_sems):
    n = len(ins)
    x, y, c = _place()
    me, sibling = (x, y, c), (x, y, 1 - c)
    chips = [(1 - x, y), (x, 1 - y), (1 - x, 1 - y)]
    sent = []

    def slot(px, py, pc):
        return 4 * px + 2 * py + pc

    def copy(a, k, block, to):
        ref = outs[a].at[slot(*block)]
        return pltpu.make_async_remote_copy(
            src_ref=ref, dst_ref=ref, send_sem=send_sems.at[a, k], recv_sem=recv_sems.at[a, k],
            device_id=to, device_id_type=MESH)

    def start():
        for a in range(n):
            outs[a][slot(*me)] = ins[a][...].astype(outs[a].dtype)
        for a in range(n):
            sent.append(copy(a, 0, me, sibling))
            sent.extend(copy(a, 1 + j, me, (*chip, c)) for j, chip in enumerate(chips))
        for cp in sent:
            cp.start()

    def forward():
        for j, chip in enumerate(chips):
            for a in range(n):
                copy(a, 1 + j, (*chip, c), me).wait_recv()
                cp = copy(a, 4 + j, (*chip, c), sibling)
                cp.start()
                sent.append(cp)

    def finish():
        for a in range(n):
            copy(a, 0, sibling, me).wait_recv()
            for j, chip in enumerate(chips):
                copy(a, 4 + j, (*chip, 1 - c), me).wait_recv()
        for cp in sent:
            cp.wait_send()

    return start, forward, finish


def _exchange_steps(n, src_of, landing, send_sems, recv_sems):
    x, y, c = _place()
    me = 4 * x + 2 * y + c
    peers = []
    for j in range(1, N_DEV):
        px = 1 - x if j & 4 else x
        py = 1 - y if j & 2 else y
        pc = 1 - c if j & 1 else c
        peers.append((px, py, pc))

    def copy(a, j, from_slot, to_slot, peer):
        return pltpu.make_async_remote_copy(
            src_ref=src_of(a, to_slot), dst_ref=landing[a].at[from_slot],
            send_sem=send_sems.at[a, j], recv_sem=recv_sems.at[a, j], device_id=peer, device_id_type=MESH)

    def start():
        for a in range(n):
            for j, p in enumerate(peers):
                copy(a, j, me, 4 * p[0] + 2 * p[1] + p[2], p).start()

    def finish():
        for a in range(n):
            for j, p in enumerate(peers):
                slot = 4 * p[0] + 2 * p[1] + p[2]
                copy(a, j, slot, slot, p).wait_recv()
        for a in range(n):
            for j, p in enumerate(peers):
                copy(a, j, me, 4 * p[0] + 2 * p[1] + p[2], p).wait_send()

    return start, finish


def _ordered_sum(gathered, out_ref):
    rows = out_ref.shape[0]

    def step(i, _):
        r = pl.ds(pl.multiple_of(i * SUBLANES, SUBLANES), SUBLANES)
        g = gathered[0, r, :]
        for d in range(1, N_DEV):
            g = g + gathered[d, r, :]
        out_ref[r, :] = g
        return 0

    lax.fori_loop(0, rows // SUBLANES, step, 0)


def _gather_weights(blocks, later):
    n, k = len(blocks), len(later)

    def body(*refs):
        refs = list(refs)
        take = lambda cnt: [refs.pop(0) for _ in range(cnt)]
        ins, later_in, outs, later_out = take(n), take(k), take(n), take(k)
        start, forward, finish = _gather_steps(ins, outs, *refs)
        start()
        for src, dst in zip(later_in, later_out):
            dst[...] = src[...].astype(BF16)
        forward()
        finish()

    vmem = pl.BlockSpec(memory_space=pltpu.VMEM)
    res = pl.pallas_call(
        body,
        name="gather_weights",
        out_shape=[jax.ShapeDtypeStruct((N_DEV, *b.shape), BF16) for b in blocks]
        + [jax.ShapeDtypeStruct(b.shape, BF16) for b in later],
        in_specs=[vmem] * (n + k),
        out_specs=[vmem] * (n + k),
        scratch_shapes=[pltpu.SemaphoreType.DMA((n, 7)), pltpu.SemaphoreType.DMA((n, 7))],
        compiler_params=_params(),
    )(*blocks, *later)
    return res[:n], res[n:]


def _adamw_small(gs, ws, ms, vs):
    n = len(gs)

    def body(*refs):
        g, w, m, v = refs[:n], refs[n : 2 * n], refs[2 * n : 3 * n], refs[3 * n : 4 * n]
        outs = refs[4 * n :]
        for a in range(n):
            delta, nm, nv = _adamw(w[a][...], g[a][...], m[a][...], v[a][...])
            outs[3 * a][...] = delta
            outs[3 * a + 1][...] = nm
            outs[3 * a + 2][...] = nv

    vmem = pl.BlockSpec(memory_space=pltpu.VMEM)
    out_shape = []
    for wa in ws:
        out_shape += [jax.ShapeDtypeStruct(wa.shape, F32)] * 3
    res = pl.pallas_call(
        body,
        name="adamw_small",
        out_shape=out_shape,
        in_specs=[vmem] * (4 * n),
        out_specs=[vmem] * (3 * n),
        compiler_params=_params(),
    )(*gs, *ws, *ms, *vs)
    return [tuple(res[3 * a : 3 * a + 3]) for a in range(n)]


def _reduce_all(parts, ws, ms, vs, small, early):
    n, ne = len(parts), len(early)
    parts4 = [p.reshape(4, 2, *p.shape[1:]) for p in parts]
    blks = [p.shape[1:] for p in parts]

    def body(*refs):
        refs = list(refs)
        take = lambda k: [refs.pop(0) for _ in range(k)]
        part, w_in, m_in, v_in = take(n), take(n), take(n), take(n)
        (small_ref,) = take(1)
        early_in = [take(5) for _ in range(ne)]
        outs = take(4 * n)
        small_all, small_sum = take(2)
        early_out = [take(4) for _ in range(ne)]
        own, r1, r2 = take(n), take(n), take(n)
        early_buf = take(ne)
        s1_send, s1_recv, s2_send, s2_recv, loc, small_send, small_recv, early_sems = refs
        small_start, small_forward, small_finish = _gather_steps([small_ref], [small_all], small_send, small_recv)
        x, y, c = _place()
        me = 4 * x + 2 * y + c
        landed = [pltpu.make_async_copy(early_in[e][1], early_buf[e], early_sems.at[e, 0]) for e in range(ne)]
        for cp in landed:
            cp.start()
        sibling = (x, y, 1 - c)
        chips = [(1 - x, y), (x, 1 - y), (1 - x, 1 - y)]

        def rowwise(rows, fn):
            chunk = math.gcd(rows, 128)

            def step(i, _):
                fn(pl.ds(pl.multiple_of(i * chunk, chunk), chunk))
                return 0

            lax.fori_loop(0, rows // chunk, step, 0)

        stage1, local = [], []
        for a in range(n):
            cp = pltpu.make_async_remote_copy(
                src_ref=part[a].at[:, 1 - c], dst_ref=r1[a], send_sem=s1_send.at[a], recv_sem=s1_recv.at[a],
                device_id=sibling, device_id_type=MESH)
            cp.start()
            stage1.append(cp)
            lc = pltpu.make_async_copy(part[a].at[:, c], own[a], loc.at[a])
            lc.start()
            local.append(lc)
        small_start()
        stage2 = []
        for a in range(n):
            local[a].wait()
            stage1[a].wait_recv()
            for chip in range(4):

                def add(r, a=a, chip=chip):
                    own[a][chip, r, :] = own[a][chip, r, :] + r1[a][chip, r, :]

                rowwise(blks[a][0], add)
            for k, chip in enumerate(chips):
                cp = pltpu.make_async_remote_copy(
                    src_ref=own[a].at[2 * chip[0] + chip[1]], dst_ref=r2[a].at[k],
                    send_sem=s2_send.at[a, k], recv_sem=s2_recv.at[a, k],
                    device_id=(*chip, c), device_id_type=MESH)
                cp.start()
                stage2.append(cp)
        small_forward()
        for e in range(ne):
            part_e, _, w_e, m_e, v_e = early_in[e]
            g_ref, d_ref, nm_ref, nv_ref = early_out[e]
            landed[e].wait()
            mine = pltpu.make_async_copy(part_e.at[me], early_buf[e].at[me], early_sems.at[e, 1])
            mine.start()
            mine.wait()

            def update_early(r, e=e, w_e=w_e, m_e=m_e, v_e=v_e, g_ref=g_ref, d_ref=d_ref, nm_ref=nm_ref, nv_ref=nv_ref):
                g = early_buf[e][0, r, :]
                for dev in range(1, N_DEV):
                    g = g + early_buf[e][dev, r, :]
                delta, nm, nv = _adamw(w_e[r, :], g, m_e[r, :], v_e[r, :])
                g_ref[r, :] = g
                d_ref[r, :] = delta
                nm_ref[r, :] = nm
                nv_ref[r, :] = nv

            rowwise(early_buf[e].shape[1], update_early)
        for a in range(n):
            for k, chip in enumerate(chips):
                stage2[3 * a + k].wait_recv()
            g_ref, d_ref, nm_ref, nv_ref = outs[4 * a : 4 * a + 4]

            def update(r, a=a, g_ref=g_ref, d_ref=d_ref, nm_ref=nm_ref, nv_ref=nv_ref):
                g = own[a][2 * x + y, r, :] + r2[a][0, r, :] + r2[a][1, r, :] + r2[a][2, r, :]
                delta, nm, nv = _adamw(w_in[a][r, :], g, m_in[a][r, :], v_in[a][r, :])
                g_ref[r, :] = g
                d_ref[r, :] = delta
                nm_ref[r, :] = nm
                nv_ref[r, :] = nv

            rowwise(blks[a][0], update)
        small_finish()
        _ordered_sum(small_all, small_sum)
        for cp in stage1 + stage2:
            cp.wait_send()

    vmem = pl.BlockSpec(memory_space=pltpu.VMEM)
    hbm = pl.BlockSpec(memory_space=pl.ANY)
    out_shape = []
    for b in blks:
        out_shape += [jax.ShapeDtypeStruct(b, F32)] * 4
    out_shape += [jax.ShapeDtypeStruct((N_DEV, *small.shape), F32), jax.ShapeDtypeStruct(small.shape, F32)]
    for e in early:
        out_shape += [jax.ShapeDtypeStruct(e[2].shape, F32)] * 4
    scratch = (
        [pltpu.VMEM((4, *b), F32) for b in blks]
        + [pltpu.VMEM((4, *b), F32) for b in blks]
        + [pltpu.VMEM((3, *b), F32) for b in blks]
        + [pltpu.VMEM(e[0].shape, F32) for e in early]
        + [pltpu.SemaphoreType.DMA((n,)), pltpu.SemaphoreType.DMA((n,)), pltpu.SemaphoreType.DMA((n, 3)),
           pltpu.SemaphoreType.DMA((n, 3)), pltpu.SemaphoreType.DMA((n,)),
           pltpu.SemaphoreType.DMA((1, 7)), pltpu.SemaphoreType.DMA((1, 7)), pltpu.SemaphoreType.DMA((ne, 2))]
    )
    res = pl.pallas_call(
        body,
        name="reduce_all",
        out_shape=out_shape,
        in_specs=[hbm] * n + [vmem] * (3 * n + 1) + [hbm, hbm, vmem, vmem, vmem] * ne,
        out_specs=[vmem] * (4 * n + 2 + 4 * ne),
        scratch_shapes=scratch,
        compiler_params=_params(),
    )(*parts4, *ws, *ms, *vs, small, *[t for e in early for t in e])
    late = [tuple(res[4 * a : 4 * a + 4]) for a in range(n)]
    at = 4 * n + 2
    return late + [tuple(res[at + 4 * e : at + 4 * e + 4]) for e in range(ne)], res[4 * n + 1]


def _rms(x):
    return lax.rsqrt(jnp.mean(x * x, axis=-1, keepdims=True) + EPS)


def _token_tile(tokens, want):
    tile = min(want, tokens // 2)
    assert tokens % tile == 0 and tile % 16 == 0
    return tile


def _in_proj(x2, g_pre, w_in_g, shards, pool_w, ssm_w, att_w):
    tokens, d = x2.shape
    nb = w_in_g.shape[2]
    mix = pool_w + ssm_w + att_w
    half = N_DEV // 2
    cl = _token_tile(tokens, 512)
    n_tiles = tokens // cl
    ns = len(shards)
    assert half * nb == mix and nb == 256 and pool_w == 384 and ssm_w == 384 and att_w == 256

    def body(x_ref, g_ref, w_ref, *rest):
        shard_hbm, (up_ref, us_ref, q_ref, gate_ref) = rest[:ns], rest[ns : ns + 4]
        gathered = rest[ns + 4 : 2 * ns + 4]
        send_sems, recv_sems, own_sems = rest[2 * ns + 4 :]
        i = pl.program_id(0)
        x_pos, y_pos, c_pos = _place()
        own = [pltpu.make_async_copy(shard_hbm[a], gathered[a].at[4 * x_pos + 2 * y_pos + c_pos], own_sems.at[a])
               for a in range(ns)]
        start, finish = _exchange_steps(ns, lambda a, slot: shard_hbm[a], gathered, send_sems, recv_sems)

        @pl.when(i == 0)
        def _():
            start()
            for cp in own:
                cp.start()

        @pl.when(i == n_tiles - 1)
        def _():
            finish()
            for cp in own:
                cp.wait()

        x = x_ref[...]
        h = (x * _rms(x) * g_ref[...]).astype(BF16)
        p = [_mm(h, w_ref[j]) for j in range(half)]
        up_ref[:, 0:256] = p[0]
        up_ref[:, 256:384] = p[1][:, 0:128]
        us_ref[:, 0:128] = p[1][:, 128:256]
        us_ref[:, 128:384] = p[2]
        q_ref[...] = p[3]
        for j in range(half):
            gate_ref[:, j * nb : (j + 1) * nb] = _mm(h, w_ref[half + j])

    hbm = pl.BlockSpec(memory_space=pl.ANY)
    res = pl.pallas_call(
        body,
        name="in_proj",
        grid=(n_tiles,),
        in_specs=[
            pl.BlockSpec((cl, d), lambda i: (i, 0)),
            pl.BlockSpec((1, d), lambda i: (0, 0)),
            pl.BlockSpec((N_DEV, d, nb), lambda i: (0, 0, 0)),
        ]
        + [hbm] * ns,
        out_specs=[
            pl.BlockSpec((cl, pool_w), lambda i: (i, 0)),
            pl.BlockSpec((cl, ssm_w), lambda i: (i, 0)),
            pl.BlockSpec((cl, att_w), lambda i: (i, 0)),
            pl.BlockSpec((cl, mix), lambda i: (i, 0)),
        ]
        + [hbm] * ns,
        out_shape=[
            jax.ShapeDtypeStruct((tokens, pool_w), F32),
            jax.ShapeDtypeStruct((tokens, ssm_w), F32),
            jax.ShapeDtypeStruct((tokens, att_w), F32),
            jax.ShapeDtypeStruct((tokens, mix), F32),
        ]
        + [jax.ShapeDtypeStruct((N_DEV, *a.shape), a.dtype) for a in shards],
        scratch_shapes=[pltpu.SemaphoreType.DMA((ns, 7)), pltpu.SemaphoreType.DMA((ns, 7)),
                        pltpu.SemaphoreType.DMA((ns,))],
        compiler_params=_params("arbitrary"),
    )(x2, g_pre, w_in_g, *shards)
    return res[:4], res[4:]


def _in_proj_bwd(x2, dres, du_pool, du_ssm, dq, dgate, g_pre, w_in_t):
    tokens, d = x2.shape
    nb = w_in_t.shape[0] // N_DEV
    pool_w, ssm_w, att_w, mix = du_pool.shape[1], du_ssm.shape[1], dq.shape[1], dgate.shape[1]
    cl = _token_tile(tokens, 512)
    n_tiles = tokens // cl

    def body(x_ref, dres_ref, dup_ref, dus_ref, dq_ref, dgate_ref, g_ref, w_ref, gx_ref, gw_hbm, gg_ref, acc, sem):
        i = pl.program_id(0)

        @pl.when(i == 0)
        def _():
            acc[...] = jnp.zeros_like(acc)
            gg_ref[...] = jnp.zeros_like(gg_ref)

        x = x_ref[...]
        r = _rms(x)
        xn = x * r
        g = g_ref[...]
        h = (xn * g).astype(BF16)
        dproj = jnp.concatenate([dup_ref[...], dus_ref[...], dq_ref[...], dgate_ref[...]], axis=1).astype(BF16)
        dh = _mm(dproj, w_ref[...])
        for j in range(N_DEV):
            acc[j] += _mm_tn(h, dproj[:, j * nb : (j + 1) * nb])
        gg_ref[...] += jnp.sum(dh * xn, axis=0, keepdims=True)
        dxn = dh * g
        gx_ref[...] = dres_ref[...] + r * (dxn - xn * jnp.mean(dxn * xn, axis=-1, keepdims=True))

        @pl.when(i == n_tiles - 1)
        def _():
            cp = pltpu.make_async_copy(acc, gw_hbm, sem)
            cp.start()
            cp.wait()

    return pl.pallas_call(
        body,
        name="in_proj_bwd",
        grid=(n_tiles,),
        in_specs=[
            pl.BlockSpec((cl, d), lambda i: (i, 0)),
            pl.BlockSpec((cl, d), lambda i: (i, 0)),
            pl.BlockSpec((cl, pool_w), lambda i: (i, 0)),
            pl.BlockSpec((cl, ssm_w), lambda i: (i, 0)),
            pl.BlockSpec((cl, att_w), lambda i: (i, 0)),
            pl.BlockSpec((cl, mix), lambda i: (i, 0)),
            pl.BlockSpec((1, d), lambda i: (0, 0)),
            pl.BlockSpec(w_in_t.shape, lambda i: (0, 0)),
        ],
        out_specs=[
            pl.BlockSpec((cl, d), lambda i: (i, 0)),
            pl.BlockSpec(memory_space=pl.ANY),
            pl.BlockSpec((1, d), lambda i: (0, 0)),
        ],
        out_shape=[
            jax.ShapeDtypeStruct((tokens, d), F32),
            jax.ShapeDtypeStruct((N_DEV, d, nb), F32),
            jax.ShapeDtypeStruct((1, d), F32),
        ],
        scratch_shapes=[pltpu.VMEM((N_DEV, d, nb), F32), pltpu.SemaphoreType.DMA],
        compiler_params=_params("arbitrary"),
    )(x2, dres, du_pool, du_ssm, dq, dgate, g_pre, w_in_t)


def _pool_geometry(seq, width):
    gw = width // len(POOL_WINDOWS)
    col = lax.broadcasted_iota(jnp.int32, (1, width), 1)
    win = jnp.full((1, width), float(POOL_WINDOWS[-1]), F32)
    for gi in range(len(POOL_WINDOWS) - 2, -1, -1):
        win = jnp.where(col < (gi + 1) * gw, float(POOL_WINDOWS[gi]), win)
    row = lax.broadcasted_iota(jnp.int32, (seq, width), 0)
    filling = 1.0 / (lax.broadcasted_iota(jnp.int32, (seq, 1), 0) + 1).astype(F32)
    inv_cnt = jnp.where(row + 1 < win.astype(jnp.int32), filling, 1.0 / win)
    return win, row, inv_cnt


def _window_sums(a, win, seq, back):
    pad = 2 * POOL_WINDOWS[-1]
    zeros = jnp.zeros((pad, a.shape[1]), F32)
    s = jnp.concatenate([a, zeros] if back else [zeros, a], axis=0)
    sums = []
    k = 1
    while k < POOL_WINDOWS[-1]:
        s = s + pltpu.roll(s, seq + pad - k if back else k, 0)
        k *= 2
        sums.append((k, s))
    out = sums[-1][1]
    for k, s in reversed(sums[:-1]):
        out = jnp.where(win <= float(k), s, out)
    return out[0:seq] if back else out[pad : pad + seq]


def _pool_fwd(u2, wp_blk, scale, batch, seq):
    width = u2.shape[1]

    def body(u_ref, w_ref, s_ref, y_ref):
        u = u_ref[...]
        win, row, inv_cnt = _pool_geometry(seq, width)
        diff = _window_sums(u, win, seq, False) * inv_cnt - u
        y_ref[...] =_mm(diff.astype(BF16), w_ref[...]) * s_ref[...]

    return pl.pallas_call(
        body,
        name="pool_fwd",
        grid=(batch,),
        in_specs=[
            pl.BlockSpec((seq, width), lambda b: (b, 0)),
            pl.BlockSpec((width, width), lambda b: (0, 0)),
            pl.BlockSpec((1, width), lambda b: (0, 0)),
        ],
        out_specs=pl.BlockSpec((seq, width), lambda b: (b, 0)),
        out_shape=jax.ShapeDtypeStruct(u2.shape, F32),
        compiler_params=_params("arbitrary"),
    )(u2, wp_blk, scale)


def _pool_bwd(u2, dy2, wp_blk, scale, batch, seq):
    width = u2.shape[1]

    def body(u_ref, dy_ref, w_ref, s_ref, du_ref, gw_ref, gs_ref):
        @pl.when(pl.program_id(0) == 0)
        def _():
            gw_ref[...] = jnp.zeros_like(gw_ref)
            gs_ref[...] = jnp.zeros_like(gs_ref)

        u = u_ref[...]
        dy = dy_ref[...]
        win, row, inv_cnt = _pool_geometry(seq, width)
        diff = (_window_sums(u, win, seq, False) * inv_cnt - u).astype(BF16)
        gs_ref[...] += jnp.sum(dy * _mm(diff, w_ref[...]), axis=0, keepdims=True)
        dys = (dy * s_ref[...]).astype(BF16)
        gw_ref[...] += _mm_tn(diff, dys)
        dd = _mm_nt(dys, w_ref[...])
        du_ref[...] = _window_sums(dd * inv_cnt, win, seq, True) - dd

    return pl.pallas_call(
        body,
        name="pool_bwd",
        grid=(batch,),
        in_specs=[
            pl.BlockSpec((seq, width), lambda b: (b, 0)),
            pl.BlockSpec((seq, width), lambda b: (b, 0)),
            pl.BlockSpec((width, width), lambda b: (0, 0)),
            pl.BlockSpec((1, width), lambda b: (0, 0)),
        ],
        out_specs=[
            pl.BlockSpec((seq, width), lambda b: (b, 0)),
            pl.BlockSpec((width, width), lambda b: (0, 0)),
            pl.BlockSpec((1, width), lambda b: (0, 0)),
        ],
        out_shape=[
            jax.ShapeDtypeStruct(u2.shape, F32),
            jax.ShapeDtypeStruct((width, width), F32),
            jax.ShapeDtypeStruct((1, width), F32),
        ],
        compiler_params=_params("arbitrary"),
    )(u2, dy2, wp_blk, scale)


def _state_row(z, n_blocks):
    re = jnp.real(z).reshape(n_blocks, -1)
    im = jnp.imag(z).reshape(n_blocks, -1)
    return jnp.concatenate([re, im], axis=1).reshape(1, -1)


def _ssm_tables(a_re, a_im, log_dt, b_re, b_im, c_re, c_im):
    groups, n_state = a_re.shape
    ch = b_re.shape[2]
    nb = groups * ch // LANES
    gl = groups // nb
    lam = lax.complex(a_re, a_im)
    lam_bar = jnp.exp(lam * jnp.exp(log_dt)[:, None])
    b_bar = ((lam_bar - 1.0) / lam)[..., None] * lax.complex(b_re, b_im)
    eye = jnp.eye(gl, dtype=F32)

    def rows_to_state(t):
        return jnp.einsum("sgnc,gh->sgchn", t.reshape(nb, gl, n_state, ch), eye).reshape(nb, gl * ch, gl * n_state)

    def state_to_rows(t):
        return jnp.einsum("sgcn,gh->shngc", t.reshape(nb, gl, ch, n_state), eye).reshape(nb, gl * n_state, gl * ch)

    b_tab = jnp.concatenate([rows_to_state(jnp.real(b_bar)), rows_to_state(jnp.imag(b_bar))], axis=2)
    c_tab = jnp.concatenate([state_to_rows(c_re), -state_to_rows(c_im)], axis=1)
    return _state_row(lam_bar, nb), b_tab, c_tab


def _lam_power(a_re, a_im, log_dt, power, scale, n_blocks):
    return _state_row(scale * jnp.exp(lax.complex(a_re, a_im) * jnp.exp(log_dt)[:, None] * power), n_blocks)


def _state_blocks(s2, n_blocks, width):
    half = s2 // n_blocks // 2
    assert half % width == 0
    return [(b * 2 * half + o, b * 2 * half + half + o) for b in range(n_blocks) for o in range(0, half, width)]


def _scan(src_ref, dst_ref, st_ref, lam8_ref, n_groups, s, n_blocks, reverse, store):
    lb = 512
    for re0, im0 in _state_blocks(2 * s, n_blocks, lb):
        cr, ci = pl.ds(re0, lb), pl.ds(im0, lb)
        lr = lam8_ref[:, cr]
        li = -lam8_ref[:, ci] if reverse else lam8_ref[:, ci]

        def step(i, carry, cr=cr, ci=ci, lr=lr, li=li):
            hr, hi = carry
            grp = n_groups - 1 - i if reverse else i
            rows = pl.ds(pl.multiple_of(grp * SUBLANES, SUBLANES), SUBLANES)
            nr = lr * hr - li * hi + src_ref[rows, cr]
            ni = lr * hi + li * hr + src_ref[rows, ci]
            if store:
                dst_ref[rows, cr] = nr
                dst_ref[rows, ci] = ni
            return nr, ni

        hr, hi = lax.fori_loop(0, n_groups, step, (st_ref[:, cr], st_ref[:, ci]), unroll=2)
        st_ref[:, cr] = hr
        st_ref[:, ci] = hi


def _pack_state(re, im):
    hi = lax.bitcast_convert_type(re.astype(BF16).astype(F32), jnp.uint32)
    lo = lax.bitcast_convert_type(im.astype(BF16).astype(F32), jnp.uint32)
    return hi | (lo >> 16)


def _unpack_state(word):
    re = lax.bitcast_convert_type(word & jnp.uint32(0xFFFF0000), F32)
    im = lax.bitcast_convert_type(word << 16, F32)
    return re, im


def _scan_adjoint(dh_ref, hprev_ref, group0, stg_ref, acc_ref, lam8_ref, n_groups, s, n_blocks):
    lb = 512
    half = s // n_blocks
    for re0, im0 in _state_blocks(2 * s, n_blocks, lb):
        cr, ci = pl.ds(re0, lb), pl.ds(im0, lb)
        ch = pl.ds(re0 // (2 * half) * half + re0 % (2 * half), lb)
        lr, li = lam8_ref[:, cr], -lam8_ref[:, ci]

        def step(i, carry, cr=cr, ci=ci, ch=ch, lr=lr, li=li):
            gr, gi, ar, ai = carry
            grp = n_groups - 1 - i
            rows = pl.ds(pl.multiple_of(grp * SUBLANES, SUBLANES), SUBLANES)
            ngr = lr * gr - li * gi + dh_ref[rows, cr]
            ngi = lr * gi + li * gr + dh_ref[rows, ci]
            hr, hi = _unpack_state(hprev_ref[pl.ds(pl.multiple_of((group0 + grp) * SUBLANES, SUBLANES), SUBLANES), ch])
            ar = ar + hr * ngr + hi * ngi
            ai = ai + hr * ngi - hi * ngr
            dh_ref[rows, cr] = ngr
            dh_ref[rows, ci] = ngi
            return ngr, ngi, ar, ai

        init = (stg_ref[:, cr], stg_ref[:, ci], acc_ref[:, cr], acc_ref[:, ci])
        gr, gi, ar, ai = lax.fori_loop(0, n_groups, step, init)
        stg_ref[:, cr] = gr
        stg_ref[:, ci] = gi
        acc_ref[:, cr] = ar
        acc_ref[:, ci] = ai


def _chunk_starts(st_ref, init_ref, lcl_ref, s, n_blocks):
    w = s // n_blocks
    init_ref[0:1, :] = jnp.zeros((1, 2 * s), F32)
    for re0, im0 in _state_blocks(2 * s, n_blocks, w):
        re, im = pl.ds(re0, w), pl.ds(im0, w)
        ar, ai = lcl_ref[:, re], lcl_ref[:, im]
        cr = jnp.zeros((1, w), F32)
        ci = jnp.zeros((1, w), F32)
        for k in range(1, NCH):
            cr, ci = (ar * cr - ai * ci + st_ref[k - 1 : k, re], ar * ci + ai * cr + st_ref[k - 1 : k, im])
            init_ref[k : k + 1, re] = cr
            init_ref[k : k + 1, im] = ci


def _chunk_starts_adjoint(stg_ref, initg_ref, lcl_ref, s, n_blocks):
    w = s // n_blocks
    initg_ref[NCH - 1 : NCH, :] = jnp.zeros((1, 2 * s), F32)
    for re0, im0 in _state_blocks(2 * s, n_blocks, w):
        re, im = pl.ds(re0, w), pl.ds(im0, w)
        ar, ai = lcl_ref[:, re], -lcl_ref[:, im]
        gr = jnp.zeros((1, w), F32)
        gi = jnp.zeros((1, w), F32)
        for k in range(NCH - 2, -1, -1):
            gr, gi = (stg_ref[k + 1 : k + 2, re] + ar * gr - ai * gi, stg_ref[k + 1 : k + 2, im] + ar * gi + ai * gr)
            initg_ref[k : k + 1, re] = gr
            initg_ref[k : k + 1, im] = gi


def _ssm_rows(seq, want):
    rows = min(want, seq // 2)
    assert seq % rows == 0 and rows % SUBLANES == 0
    return rows


def _chunk_copies(hbm_ref, b, cm_ref, sems, to_cm):
    cl = cm_ref.shape[0]
    copies = []
    for k in range(NCH):
        nat, cm = hbm_ref.at[b, pl.ds(k * cl, cl), :], cm_ref.at[:, k, :]
        src, dst = (nat, cm) if to_cm else (cm, nat)
        copies.append(pltpu.make_async_copy(src, dst, sems.at[k]))
    return copies


def _blockwise(fn, n_blocks):
    return jnp.concatenate([fn(b) for b in range(n_blocks)], axis=1)


def _ssm_fwd(u, b_tab, c_tab, lam8, lcl, d_skip, w_glu):
    batch, seq, width = u.shape
    s = lam8.shape[1] // 2
    nb = b_tab.shape[0]
    sb = 2 * s // nb
    cl = seq // NCH
    rows = _ssm_rows(seq, 1024)
    n_tiles = seq // rows
    n_groups = rows // SUBLANES

    def body(u_hbm, b_ref, c_ref, lam_ref, lcl_ref, d_ref, wg_ref, y_hbm, pre_ref, z_ref, init_ref,
             u_cm, y_cm, bu_all, st, sems):
        b, ph, t = pl.program_id(0), pl.program_id(1), pl.program_id(2)
        tile_groups = pl.ds(pl.multiple_of(t * n_groups, n_groups), n_groups)

        @pl.when((ph == 0) & (t == 0))
        def _():
            loads = _chunk_copies(u_hbm, b, u_cm, sems, True)
            for cp in loads:
                cp.start()
            st[...] = jnp.zeros_like(st)
            for cp in loads:
                cp.wait()

        @pl.when((ph == 1) & (t == 0))
        def _():
            st[...] = init_ref[...]

        u_t = u_cm[tile_groups].reshape(rows, width)
        bu = bu_all.at[pl.ds(pl.multiple_of(t * rows, rows), rows)]

        @pl.when(ph == 0)
        def _():
            u_b = u_t.astype(BF16)
            for blk in range(nb):
                bu[:, blk * sb : (blk + 1) * sb] = _mm(u_b[:, blk * LANES : (blk + 1) * LANES], b_ref[blk])
            _scan(bu, bu, st, lam_ref, n_groups, s, nb, False, False)

        @pl.when((ph == 0) & (t == n_tiles - 1))
        def _():
            _chunk_starts(st, init_ref, lcl_ref, s, nb)

        @pl.when(ph == 1)
        def _():
            _scan(bu, bu, st, lam_ref, n_groups, s, nb, False, True)
            hs = lambda blk: _mm(bu[:, blk * sb : (blk + 1) * sb].astype(BF16), c_ref[blk])
            pre = _blockwise(hs, nb) + d_ref[...] * u_t
            z = _mm(_gelu(pre).astype(BF16), wg_ref[...])
            pre_ref[...] = pre
            z_ref[...] = z
            y = z[:, 0:width] * jax.nn.sigmoid(z[:, width : 2 * width])
            y_cm[tile_groups] = y.reshape(n_groups, SUBLANES, width)

        @pl.when((ph == 1) & (t == n_tiles - 1))
        def _():
            stores = _chunk_copies(y_hbm, b, y_cm, sems, False)
            for cp in stores:
                cp.start()
            for cp in stores:
                cp.wait()

    out_tile = lambda b, ph, t: (b, t * ph, 0)
    full = lambda a: pl.BlockSpec(a.shape, lambda b, ph, t: (0,) * a.ndim)
    hbm = pl.BlockSpec(memory_space=pl.ANY)
    return pl.pallas_call(
        body,
        name="ssm_fwd",
        grid=(batch, 2, n_tiles),
        in_specs=[hbm, full(b_tab), full(c_tab), full(lam8), full(lcl), full(d_skip), full(w_glu)],
        out_specs=[
            hbm,
            pl.BlockSpec((None, rows, width), out_tile),
            pl.BlockSpec((None, rows, 2 * width), out_tile),
            pl.BlockSpec((None, SUBLANES, 2 * s), lambda b, ph, t: (b, 0, 0)),
        ],
        out_shape=[
            jax.ShapeDtypeStruct((batch, seq, width), F32),
            jax.ShapeDtypeStruct((batch, seq, width), F32),
            jax.ShapeDtypeStruct((batch, seq, 2 * width), F32),
            jax.ShapeDtypeStruct((batch, SUBLANES, 2 * s), F32),
        ],
        scratch_shapes=[
            pltpu.VMEM((cl, NCH, width), F32),
            pltpu.VMEM((cl, NCH, width), F32),
            pltpu.VMEM((seq, 2 * s), F32),
            pltpu.VMEM((SUBLANES, 2 * s), F32),
            pltpu.SemaphoreType.DMA((NCH,)),
        ],
        compiler_params=_params("arbitrary", "arbitrary", "arbitrary"),
    )(u, b_tab, c_tab, lam8, lcl, d_skip, w_glu)


def _ssm_bwd(u, pre_p, z_p, dy, init, b_tab, b_tab_t, c_tab_t, lam8, lcl, d_skip, w_glu, ready):
    batch, seq, width = u.shape
    nr = len(ready)
    s = lam8.shape[1] // 2
    nb = b_tab.shape[0]
    sb = 2 * s // nb
    cl = seq // NCH
    rows = _ssm_rows(seq, 512)
    n_tiles = seq // rows
    n_groups = rows // SUBLANES

    def body(u_hbm, pre_ref, z_ref, dy_hbm, init_ref, b_ref, bt_ref, ct_ref, lam_ref, lcl_ref, d_ref, wg_ref, *rest):
        ready_hbm, rest = rest[:nr], rest[nr:]
        du_hbm, gb_ref, gc_ref, gwg_ref, gd_ref, glam_ref = rest[:6]
        landed_hbm, rest = rest[6 : 6 + nr], rest[6 + nr :]
        u_cm, dy_cm, work, hs_all, dpre_all, st, stg, initg, acc, sems, send_sems, recv_sems = rest
        b, ph, t = pl.program_id(0), pl.program_id(1), pl.program_id(2)
        half = s // nb
        exchange_start, exchange_finish = _exchange_steps(
            nr, lambda a, slot: ready_hbm[a].at[slot], landed_hbm, send_sems, recv_sems)
        first = (b == 0) & (ph == 0) & (t == 0)
        last = (b == batch - 1) & (ph == 2) & (t == n_tiles - 1)
        tile = jnp.where(ph == 0, t, n_tiles - 1 - t)
        tile_rows = pl.ds(pl.multiple_of(tile * rows, rows), rows)
        tile_groups = pl.ds(pl.multiple_of(tile * n_groups, n_groups), n_groups)
        lanes = lambda blk: slice(blk * LANES, (blk + 1) * LANES)
        states = lambda blk: slice(blk * sb, (blk + 1) * sb)

        @pl.when(first)
        def _():
            exchange_start()
            acc[...] = jnp.zeros_like(acc)
            gb_ref[...] = jnp.zeros_like(gb_ref)
            gc_ref[...] = jnp.zeros_like(gc_ref)
            gwg_ref[...] = jnp.zeros_like(gwg_ref)
            gd_ref[...] = jnp.zeros_like(gd_ref)

        @pl.when((ph == 0) & (t == 0))
        def _():
            loads = _chunk_copies(u_hbm, b, u_cm, sems.at[0], True) + _chunk_copies(dy_hbm, b, dy_cm, sems.at[1], True)
            for cp in loads:
                cp.start()
            st[...] = init_ref[...]
            for blk in range(nb):
                entry = init_ref[:, states(blk)]
                hs_all[0:SUBLANES, blk * half : (blk + 1) * half] = _pack_state(entry[:, 0:half], entry[:, half : 2 * half])
            for cp in loads:
                cp.wait()

        u_t = u_cm[tile_groups].reshape(rows, width)
        u_b = u_t.astype(BF16)

        @pl.when(ph == 0)
        def _():
            for blk in range(nb):
                work[:, states(blk)] = _mm(u_b[:, lanes(blk)], b_ref[blk])
            _scan(work, work, st, lam_ref, n_groups, s, nb, False, True)
            z = z_ref[...]
            dy_t = dy_cm[tile_groups].reshape(rows, width)
            pre = pre_ref[...]
            z1, sig = z[:, 0:width], jax.nn.sigmoid(z[:, width : 2 * width])
            dz = jnp.concatenate([dy_t * sig, dy_t * z1 * sig * (1.0 - sig)], axis=1).astype(BF16)
            gwg_ref[...] += _mm_tn(_gelu(pre).astype(BF16), dz)
            dpre = _mm_nt(dz, wg_ref[...]) * _gelu_grad(pre)
            dpre_all[tile_rows, :] = dpre
            gd_ref[...] += jnp.sum(dpre * u_t, axis=0, keepdims=True)
            dpre_b = dpre.astype(BF16)
            kept = pl.ds(pl.multiple_of(tile * rows + SUBLANES, SUBLANES), rows)
            for blk in range(nb):
                hs = work[:, states(blk)]
                gc_ref[blk] += _mm_tn(hs.astype(BF16), dpre_b[:, lanes(blk)])
                hs_all[kept, blk * half : (blk + 1) * half] = _pack_state(hs[:, 0:half], hs[:, half : 2 * half])

        @pl.when(ph >= 1)
        def _():
            dpre_b = dpre_all[tile_rows, :].astype(BF16)
            for blk in range(nb):
                work[:, states(blk)] = _mm(dpre_b[:, lanes(blk)], ct_ref[blk])

        @pl.when(ph == 1)
        def _():
            @pl.when(t == 0)
            def _():
                stg[...] = jnp.zeros_like(stg)

            _scan(work, work, stg, lam_ref, n_groups, s, nb, True, False)

            @pl.when(t == n_tiles - 1)
            def _():
                _chunk_starts_adjoint(stg, initg, lcl_ref, s, nb)

        @pl.when(ph == 2)
        def _():
            @pl.when(t == 0)
            def _():
                stg[...] = initg[...]

            _scan_adjoint(work, hs_all, tile * n_groups, stg, acc, lam_ref, n_groups, s, nb)
            du = lambda blk: _mm(work[:, states(blk)].astype(BF16), bt_ref[blk])
            du_t = _blockwise(du, nb) + dpre_all[tile_rows, :] * d_ref[...]
            dy_cm[tile_groups] = du_t.reshape(n_groups, SUBLANES, width)
            for blk in range(nb):
                gb_ref[blk] += _mm_tn(u_b[:, lanes(blk)], work[:, states(blk)].astype(BF16))

            @pl.when(t == n_tiles - 1)
            def _():
                stores = _chunk_copies(du_hbm, b, dy_cm, sems.at[0], False)
                for cp in stores:
                    cp.start()
                for cp in stores:
                    cp.wait()

        @pl.when(last)
        def _():
            glam_ref[...] = jnp.sum(acc[...], axis=0, keepdims=True)
            exchange_finish()

    def tile(b, ph, t):
        return (b, jnp.where(ph == 0, t, n_tiles - 1 - t), 0)

    full = lambda a: pl.BlockSpec(a.shape, lambda b, ph, t: (0,) * a.ndim)
    hbm = pl.BlockSpec(memory_space=pl.ANY)
    res = pl.pallas_call(
        body,
        name="ssm_bwd",
        grid=(batch, 3, n_tiles),
        in_specs=[
            hbm,
            pl.BlockSpec((None, rows, width), tile),
            pl.BlockSpec((None, rows, 2 * width), tile),
            hbm,
            pl.BlockSpec((None, SUBLANES, 2 * s), lambda b, ph, t: (b, 0, 0)),
            full(b_tab), full(b_tab_t), full(c_tab_t), full(lam8), full(lcl), full(d_skip), full(w_glu),
        ]
        + [hbm] * nr,
        out_specs=[
            hbm,
            full(b_tab), full(b_tab_t), full(w_glu), full(d_skip),
            pl.BlockSpec((1, 2 * s), lambda b, ph, t: (0, 0)),
        ]
        + [hbm] * nr,
        out_shape=[
            jax.ShapeDtypeStruct((batch, seq, width), F32),
            jax.ShapeDtypeStruct(b_tab.shape, F32),
            jax.ShapeDtypeStruct(b_tab_t.shape, F32),
            jax.ShapeDtypeStruct(w_glu.shape, F32),
            jax.ShapeDtypeStruct(d_skip.shape, F32),
            jax.ShapeDtypeStruct((1, 2 * s), F32),
        ]
        + [jax.ShapeDtypeStruct(a.shape, F32) for a in ready],
        scratch_shapes=[
            pltpu.VMEM((cl, NCH, width), F32),
            pltpu.VMEM((cl, NCH, width), F32),
            pltpu.VMEM((rows, 2 * s), F32),
            pltpu.VMEM((seq + SUBLANES, s), jnp.uint32),
            pltpu.VMEM((seq, width), F32),
        ]
        + [pltpu.VMEM((SUBLANES, 2 * s), F32)] * 4
        + [pltpu.SemaphoreType.DMA((2, NCH)), pltpu.SemaphoreType.DMA((nr, 7)), pltpu.SemaphoreType.DMA((nr, 7))],
        compiler_params=_params("arbitrary", "arbitrary", "arbitrary"),
    )(u, pre_p, z_p, dy, init, b_tab, b_tab_t, c_tab_t, lam8, lcl, d_skip, w_glu, *ready)
    return res[:6], res[6:]


def _kv_fwd(mem, g_mem, w_kv):
    batch, n_mem, d = mem.shape
    kvw = w_kv.shape[1]

    def body(mem_ref, g_ref, w_ref, kv_ref):
        m = mem_ref[...]
        kv_ref[...] = _mm((m * _rms(m) * g_ref[...]).astype(BF16), w_ref[...])

    return pl.pallas_call(
        body,
        name="kv_fwd",
        grid=(batch,),
        in_specs=[
            pl.BlockSpec((None, n_mem, d), lambda b: (b, 0, 0)),
            pl.BlockSpec((1, d), lambda b: (0, 0)),
            pl.BlockSpec((d, kvw), lambda b: (0, 0)),
        ],
        out_specs=pl.BlockSpec((None, n_mem, kvw), lambda b: (b, 0, 0)),
        out_shape=jax.ShapeDtypeStruct((batch, n_mem, kvw), F32),
        compiler_params=_params("arbitrary"),
    )(mem, g_mem, w_kv)


def _kv_bwd(mem, dkv, g_mem, w_kv):
    batch, n_mem, d = mem.shape
    kvw = w_kv.shape[1]

    def body(mem_ref, dkv_ref, g_ref, w_ref, gw_ref, gg_ref):
        @pl.when(pl.program_id(0) == 0)
        def _():
            gw_ref[...] = jnp.zeros_like(gw_ref)
            gg_ref[...] = jnp.zeros_like(gg_ref)

        m = mem_ref[...]
        mn = m * _rms(m)
        dkv_b = dkv_ref[...].astype(BF16)
        gw_ref[...] += _mm_tn((mn * g_ref[...]).astype(BF16), dkv_b)
        gg_ref[...] += jnp.sum(_mm_nt(dkv_b, w_ref[...]) * mn, axis=0, keepdims=True)

    return pl.pallas_call(
        body,
        name="kv_bwd",
        grid=(batch,),
        in_specs=[
            pl.BlockSpec((None, n_mem, d), lambda b: (b, 0, 0)),
            pl.BlockSpec((None, n_mem, kvw), lambda b: (b, 0, 0)),
            pl.BlockSpec((1, d), lambda b: (0, 0)),
            pl.BlockSpec((d, kvw), lambda b: (0, 0)),
        ],
        out_specs=[pl.BlockSpec((d, kvw), lambda b: (0, 0)), pl.BlockSpec((1, d), lambda b: (0, 0))],
        out_shape=[jax.ShapeDtypeStruct((d, kvw), F32), jax.ShapeDtypeStruct((1, d), F32)],
        compiler_params=_params("arbitrary"),
    )(mem, dkv, g_mem, w_kv)


def _tail(x2, target2, gate, y_pool, y_ssm, q, kv, w_out, g_post):
    tokens, d = x2.shape
    pool_w, ssm_w, att_w, mix = y_pool.shape[1], y_ssm.shape[1], q.shape[1], gate.shape[1]
    batch, n_mem, kvw = kv.shape
    hd = att_w // MEM_HEADS
    cl = _token_tile(tokens // batch, 512)
    n_tiles = tokens // cl
    per_seq = tokens // batch // cl
    qk_scale = hd**-0.5

    def body(x_ref, tg_ref, gate_ref, yp_ref, ys_ref, q_ref, kv_ref, w_ref, g_ref,
             dres_ref, dgate_ref, dyp_ref, dys_ref, dq_ref, dkv_ref, gw_hbm, gg_ref, loss_ref, acc, sem):
        i = pl.program_id(0)

        @pl.when(i == 0)
        def _():
            acc[...] = jnp.zeros_like(acc)
            gg_ref[...] = jnp.zeros_like(gg_ref)
            loss_ref[...] = jnp.zeros_like(loss_ref)

        @pl.when(i % per_seq == 0)
        def _():
            dkv_ref[...] = jnp.zeros_like(dkv_ref)

        k = kv_ref[:, 0:att_w].astype(BF16)
        v = kv_ref[:, att_w : 2 * att_w].astype(BF16)
        lane = lax.broadcasted_iota(jnp.int32, (1, att_w), 1)
        heads = [(lane >= h * hd) & (lane < (h + 1) * hd) for h in range(MEM_HEADS)]
        g = g_ref[...]

        def part(rows):
            n_rows = rows.stop - rows.start
            q = q_ref[rows, :]
            probs, q_heads = [], []
            att = jnp.zeros((n_rows, att_w), F32)
            for mask in heads:
                qh = jnp.where(mask, q, 0.0).astype(BF16)
                sc = _mm_nt(qh, k) * qk_scale
                e = jnp.exp(sc - jnp.max(sc, axis=-1, keepdims=True))
                p = e * (1.0 / jnp.sum(e, axis=-1, keepdims=True))
                att = att + jnp.where(mask, _mm(p.astype(BF16), v), 0.0)
                probs.append(p)
                q_heads.append(qh)

            ycat = jnp.concatenate([yp_ref[rows, :], ys_ref[rows, :], att], axis=1)
            gate = gate_ref[rows, :]
            sig = jax.nn.sigmoid(gate)
            silu = gate * sig
            yg = (ycat * silu).astype(BF16)
            out = _mm(yg, w_ref[...])
            r = _rms(out)
            on = out * r
            err = x_ref[rows, :] + on * g - tg_ref[rows, :]
            loss_ref[...] += 0.5 * jnp.sum(jnp.mean(err * err, axis=-1, keepdims=True), axis=0, keepdims=True)
            dres = err * (1.0 / d)
            dres_ref[rows, :] = dres
            gg_ref[...] += jnp.sum(dres * on, axis=0, keepdims=True)
            don = dres * g
            dout = (r * (don - on * jnp.mean(don * on, axis=-1, keepdims=True))).astype(BF16)
            acc[...] += _mm_tn(yg, dout)
            dyg = _mm_nt(dout, w_ref[...])
            dgate_ref[rows, :] = dyg * ycat * (sig * (1.0 + gate * (1.0 - sig)))
            dycat = dyg * silu
            dyp_ref[rows, :] = dycat[:, 0:pool_w]
            dys_ref[rows, :] = dycat[:, pool_w : pool_w + ssm_w]
            datt = dycat[:, pool_w + ssm_w : mix]

            dq = jnp.zeros((n_rows, att_w), F32)
            dk = jnp.zeros((n_mem, att_w), F32)
            dv = jnp.zeros((n_mem, att_w), F32)
            for mask, p, qh in zip(heads, probs, q_heads):
                doh = jnp.where(mask, datt, 0.0).astype(BF16)
                dp = _mm_nt(doh, v)
                ds = (p * (dp - jnp.sum(p * dp, axis=-1, keepdims=True)) * qk_scale).astype(BF16)
                dq = dq + jnp.where(mask, _mm(ds, k), 0.0)
                dk = dk + _mm_tn(ds, qh)
                dv = dv + _mm_tn(p.astype(BF16), doh)
            dq_ref[rows, :] = dq
            dkv_ref[:, 0:att_w] += dk
            dkv_ref[:, att_w : 2 * att_w] += dv

        part(slice(0, cl))

        @pl.when(i == n_tiles - 1)
        def _():
            cp = pltpu.make_async_copy(acc, gw_hbm, sem)
            cp.start()
            cp.wait()

    tok = lambda w: pl.BlockSpec((cl, w), lambda i: (i, 0))
    chunked = tok(ssm_w)
    per_batch = pl.BlockSpec((None, n_mem, kvw), lambda i: (i // per_seq, 0, 0))
    return pl.pallas_call(
        body,
        name="tail",
        grid=(n_tiles,),
        in_specs=[
            tok(d), tok(d), tok(mix), tok(pool_w), chunked, tok(att_w), per_batch,
            pl.BlockSpec((mix, d), lambda i: (0, 0)),
            pl.BlockSpec((1, d), lambda i: (0, 0)),
        ],
        out_specs=[
            tok(d), tok(mix), tok(pool_w), chunked, tok(att_w), per_batch,
            pl.BlockSpec(memory_space=pl.ANY),
            pl.BlockSpec((1, d), lambda i: (0, 0)),
            pl.BlockSpec((1, 1), lambda i: (0, 0)),
        ],
        out_shape=[
            jax.ShapeDtypeStruct((tokens, d), F32),
            jax.ShapeDtypeStruct((tokens, mix), F32),
            jax.ShapeDtypeStruct((tokens, pool_w), F32),
            jax.ShapeDtypeStruct((tokens, ssm_w), F32),
            jax.ShapeDtypeStruct((tokens, att_w), F32),
            jax.ShapeDtypeStruct(kv.shape, F32),
            jax.ShapeDtypeStruct((mix, d), F32),
            jax.ShapeDtypeStruct((1, d), F32),
            jax.ShapeDtypeStruct((1, 1), F32),
        ],
        scratch_shapes=[pltpu.VMEM((mix, d), F32), pltpu.SemaphoreType.DMA],
        compiler_params=_params("arbitrary"),
    )(x2, target2, gate, y_pool, y_ssm, q, kv, w_out, g_post)


def _pack(arrays):
    flat = jnp.concatenate([a.reshape(-1) for a in arrays])
    rows = -(-flat.size // (SUBLANES * LANES)) * SUBLANES
    return jnp.pad(flat, (0, rows * LANES - flat.size)).reshape(rows, LANES)


def _unpack(packed, like):
    flat, out, at = packed.reshape(-1), [], 0
    for a in like:
        out.append(flat[at : at + a.size].reshape(a.shape))
        at += a.size
    return out


def kernel(x, mem, g_pre, w_in, w_pool, pool_scale, a_re, a_im, log_dt, b_re, b_im, c_re, c_im, d_skip, w_glu, g_mem, w_kv, w_out, g_post, loss_target, m_g_pre, m_w_in, m_w_pool, m_pool_scale, m_a_re, m_a_im, m_log_dt, m_b_re, m_b_im, m_c_re, m_c_im, m_d_skip, m_w_glu, m_g_mem, m_w_kv, m_w_out, m_g_post, v_g_pre, v_w_in, v_w_pool, v_pool_scale, v_a_re, v_a_im, v_log_dt, v_b_re, v_b_im, v_c_re, v_c_im, v_d_skip, v_w_glu, v_g_mem, v_w_kv, v_w_out, v_g_post):
    batch, seq, d = x.shape
    cl = seq // NCH
    pool_w, ssm_w = pool_scale.shape[1], d_skip.shape[1]
    att_w = w_kv.shape[2] // 2
    tokens = batch * seq
    x2 = x.reshape(tokens, d)
    target2 = loss_target.reshape(tokens, d)

    (w_in_g,), shards = _gather_weights([w_in[0]], [w_out[0], w_kv[0], w_glu[0]])

    wp_blk = jax.scipy.linalg.block_diag(*w_pool[0]).astype(BF16)
    ssm_params = (a_re[0], a_im[0], log_dt[0], b_re[0], b_im[0], c_re[0], c_im[0])
    (lam_row, b_tab, c_tab), tables_vjp = jax.vjp(_ssm_tables, *ssm_params)
    nb = b_tab.shape[0]
    lam8 = jnp.broadcast_to(lam_row, (SUBLANES, lam_row.shape[1]))
    lcl = _lam_power(a_re[0], a_im[0], log_dt[0], float(cl), 1.0, nb)
    b_bf, c_bf = b_tab.astype(BF16), c_tab.astype(BF16)

    (u_pool, u_ssm, q, gate), (w_out_g, w_kv_g, w_glu_g) = _in_proj(x2, g_pre, w_in_g, shards, pool_w, ssm_w, att_w)
    w_out_f = w_out_g.reshape(N_DEV * w_out_g.shape[1], w_out_g.shape[2])
    w_kv_f = w_kv_g.reshape(N_DEV * w_kv_g.shape[1], w_kv_g.shape[2])
    w_glu_f = w_glu_g.transpose(1, 0, 2).reshape(w_glu_g.shape[1], N_DEV * w_glu_g.shape[2])
    y_pool = _pool_fwd(u_pool, wp_blk, pool_scale, batch, seq)
    u_ssm = u_ssm.reshape(batch, seq, ssm_w)
    y_ssm, pre_ssm, z_ssm, init_ssm = _ssm_fwd(u_ssm, b_bf, c_bf, lam8, lcl, d_skip, w_glu_f)
    kv = _kv_fwd(mem, g_mem, w_kv_f)

    dres, dgate, dy_pool, dy_ssm, dq, dkv, gw_out, gg_post, loss_part = _tail(
        x2, target2, gate, y_pool, y_ssm.reshape(tokens, ssm_w), q, kv, w_out_f, g_post)

    gw_kv, gg_mem = _kv_bwd(mem, dkv, g_mem, w_kv_f)
    du_pool, gwp_dense, g_scale = _pool_bwd(u_pool, dy_pool, wp_blk, pool_scale, batch, seq)
    gw_kv8 = gw_kv.reshape(N_DEV, -1, gw_kv.shape[1])
    gw_out8 = gw_out.reshape(N_DEV, -1, gw_out.shape[1])
    (du_ssm, gb_tab, gc_tab, gw_glu, gd_skip, glam), (kv_landed, out_landed) = _ssm_bwd(
        u_ssm, pre_ssm, z_ssm, dy_ssm.reshape(batch, seq, ssm_w), init_ssm, b_bf, b_bf.transpose(0, 2, 1),
        c_bf.transpose(0, 2, 1), lam8, lcl, d_skip, w_glu_f, [gw_kv8, gw_out8])
    grad_x2, gw_in, gg_pre = _in_proj_bwd(
        x2, dres, du_pool, du_ssm.reshape(tokens, ssm_w), dq, dgate, g_pre,
        w_in_g.transpose(0, 2, 1).reshape(-1, d))

    gw = pool_w // len(POOL_WINDOWS)
    gw_pool = jnp.stack([gwp_dense[i * gw : (i + 1) * gw, i * gw : (i + 1) * gw] for i in range(len(POOL_WINDOWS))])
    g_ssm = tables_vjp((glam, gb_tab, gc_tab))

    small_w = [g_pre, w_pool, pool_scale, a_re, a_im, log_dt, b_re, b_im, c_re, c_im, d_skip, g_mem, g_post]
    small_m = [m_g_pre, m_w_pool, m_pool_scale, m_a_re, m_a_im, m_log_dt, m_b_re, m_b_im, m_c_re, m_c_im, m_d_skip, m_g_mem, m_g_post]
    small_v = [v_g_pre, v_w_pool, v_pool_scale, v_a_re, v_a_im, v_log_dt, v_b_re, v_b_im, v_c_re, v_c_im, v_d_skip, v_g_mem, v_g_post]
    small_g = [gg_pre, gw_pool, g_scale, *g_ssm, gd_skip, gg_mem, gg_post]
    big, small_sum = _reduce_all(
        [gw_in, gw_glu.reshape(ssm_w, N_DEV, -1).transpose(1, 0, 2)],
        [w_in[0], w_glu[0]], [m_w_in[0], m_w_glu[0]], [v_w_in[0], v_w_glu[0]],
        _pack(small_g + [loss_part]),
        [(gw_kv8, kv_landed, w_kv[0], m_w_kv[0], v_w_kv[0]), (gw_out8, out_landed, w_out[0], m_w_out[0], v_w_out[0])])
    big = {name: tuple(t[None] for t in res) for name, res in zip(["w_in", "w_glu", "w_kv", "w_out"], big)}

    flat2 = lambda a: a.reshape(-1, a.shape[-1])
    sg = _unpack(small_sum, [flat2(a) for a in small_w] + [loss_part])
    loss = sg[-1].reshape(())
    updates = _adamw_small(sg[:-1], [flat2(a) for a in small_w], [flat2(a) for a in small_m], [flat2(a) for a in small_v])
    sg = [g.reshape(a.shape) for g, a in zip(sg[:-1], small_w)]
    sd, sm, sv = ([u[kind].reshape(a.shape) for u, a in zip(updates, small_w)] for kind in range(3))

    order = ["g_pre", "w_in", "w_pool", "pool_scale", "a_re", "a_im", "log_dt", "b_re", "b_im", "c_re", "c_im",
             "d_skip", "w_glu", "g_mem", "w_kv", "w_out", "g_post"]
    small_names = ["g_pre", "w_pool", "pool_scale", "a_re", "a_im", "log_dt", "b_re", "b_im", "c_re", "c_im",
                   "d_skip", "g_mem", "g_post"]
    outs = [[], [], [], []]
    for name in order:
        if name in big:
            parts = big[name]
        else:
            j = small_names.index(name)
            parts = (sg[j], sd[j], sm[j], sv[j])
        for kind in range(4):
            outs[kind].append(parts[kind])
    return (loss, grad_x2.reshape(batch, seq, d), *outs[0], *outs[1], *outs[2], *outs[3])
```

```python
import functools
import math

import jax
import jax.numpy as jnp
from jax import lax
from jax.experimental import pallas as pl
from jax.experimental.pallas import tpu as pltpu

F32 = jnp.float32
BF16 = jnp.bfloat16
MESH = pl.DeviceIdType.MESH

N_DEV = 8
NCH = 8
SUBLANES = 8
LANES = 128
VMEM_LIMIT = 56 * 1024 * 1024

EPS = 1e-6
POOL_WINDOWS = (2, 4, 8, 16)
MEM_HEADS = 4
SSM_GROUP = 16
SSM_N = 64
ADAM_LR, ADAM_B1, ADAM_B2, ADAM_EPS, ADAM_WD, ADAM_STEP = 0.001, 0.9, 0.999, 1e-08, 0.01, 10


def _mm(a, b):
    return jnp.dot(a, b, preferred_element_type=F32)


def _mm_nt(a, b):
    return lax.dot_general(a, b, (((1,), (1,)), ((), ())), preferred_element_type=F32)


def _mm_tn(a, b):
    return lax.dot_general(a, b, (((0,), (0,)), ((), ())), preferred_element_type=F32)


def _params(*sem):
    return pltpu.CompilerParams(dimension_semantics=sem or None, vmem_limit_bytes=VMEM_LIMIT)


def _adamw(w, g, m, v):
    m = ADAM_B1 * m + (1.0 - ADAM_B1) * g
    v = ADAM_B2 * v + (1.0 - ADAM_B2) * (g * g)
    m_hat = m / (1.0 - ADAM_B1**ADAM_STEP)
    v_hat = v / (1.0 - ADAM_B2**ADAM_STEP)
    delta = -ADAM_LR * (m_hat / (jnp.sqrt(v_hat) + ADAM_EPS) + ADAM_WD * w)
    return delta, m, v


def _gelu(x):
    k = math.sqrt(2.0 / math.pi)
    return 0.5 * x * (1.0 + jnp.tanh(k * (x + 0.044715 * x * x * x)))


def _gelu_grad(x):
    k = math.sqrt(2.0 / math.pi)
    th = jnp.tanh(k * (x + 0.044715 * x * x * x))
    return 0.5 * (1.0 + th) + 0.5 * x * (1.0 - th * th) * (k * (1.0 + 3.0 * 0.044715 * x * x))


def _place():
    return lax.axis_index("x"), lax.axis_index("y"), lax.axis_index("c")


def _gather_steps(ins, outs, send_sems, recv_sems):
    n = len(ins)
    x, y, c = _place()
    me, sibling = (x, y, c), (x, y, 1 - c)
    chips = [(1 - x, y), (x, 1 - y), (1 - x, 1 - y)]
    sent = []

    def slot(px, py, pc):
        return 4 * px + 2 * py + pc

    def copy(a, k, block, to):
        ref = outs[a].at[slot(*block)]
        return pltpu.make_async_remote_copy(
            src_ref=ref, dst_ref=ref, send_sem=send_sems.at[a, k], recv_sem=recv_sems.at[a, k],
            device_id=to, device_id_type=MESH)

    def start():
        for a in range(n):
            outs[a][slot(*me)] = ins[a][...].astype(outs[a].dtype)
        for a in range(n):
            sent.append(copy(a, 0, me, sibling))
            sent.extend(copy(a, 1 + j, me, (*chip, c)) for j, chip in enumerate(chips))
        for cp in sent:
            cp.start()

    def forward():
        for j, chip in enumerate(chips):
            for a in range(n):
                copy(a, 1 + j, (*chip, c), me).wait_recv()
                cp = copy(a, 4 + j, (*chip, c), sibling)
                cp.start()
                sent.append(cp)

    def finish():
        for a in range(n):
            copy(a, 0, sibling, me).wait_recv()
            for j, chip in enumerate(chips):
                copy(a, 4 + j, (*chip, 1 - c), me).wait_recv()
        for cp in sent:
            cp.wait_send()

    return start, forward, finish


def _exchange_steps(n, src_of, landing, send_sems, recv_sems):
    x, y, c = _place()
    me = 4 * x + 2 * y + c
    peers = []
    for j in range(1, N_DEV):
        px = 1 - x if j & 4 else x
        py = 1 - y if j & 2 else y
        pc = 1 - c if j & 1 else c
        peers.append((px, py, pc))

    def copy(a, j, from_slot, to_slot, peer):
        return pltpu.make_async_remote_copy(
            src_ref=src_of(a, to_slot), dst_ref=landing[a].at[from_slot],
            send_sem=send_sems.at[a, j], recv_sem=recv_sems.at[a, j], device_id=peer, device_id_type=MESH)

    def start():
        for a in range(n):
            for j, p in enumerate(peers):
                copy(a, j, me, 4 * p[0] + 2 * p[1] + p[2], p).start()

    def finish():
        for a in range(n):
            for j, p in enumerate(peers):
                slot = 4 * p[0] + 2 * p[1] + p[2]
                copy(a, j, slot, slot, p).wait_recv()
        for a in range(n):
            for j, p in enumerate(peers):
                copy(a, j, me, 4 * p[0] + 2 * p[1] + p[2], p).wait_send()

    return start, finish


def _ordered_sum(gathered, out_ref):
    rows = out_ref.shape[0]

    def step(i, _):
        r = pl.ds(pl.multiple_of(i * SUBLANES, SUBLANES), SUBLANES)
        g = gathered[0, r, :]
        for d in range(1, N_DEV):
            g = g + gathered[d, r, :]
        out_ref[r, :] = g
        return 0

    lax.fori_loop(0, rows // SUBLANES, step, 0)


def _adamw_small(gs, ws, ms, vs):
    n = len(gs)

    def body(*refs):
        g, w, m, v = refs[:n], refs[n : 2 * n], refs[2 * n : 3 * n], refs[3 * n : 4 * n]
        outs = refs[4 * n :]
        for a in range(n):
            delta, nm, nv = _adamw(w[a][...], g[a][...], m[a][...], v[a][...])
            outs[3 * a][...] = delta
            outs[3 * a + 1][...] = nm
            outs[3 * a + 2][...] = nv

    vmem = pl.BlockSpec(memory_space=pltpu.VMEM)
    out_shape = []
    for wa in ws:
        out_shape += [jax.ShapeDtypeStruct(wa.shape, F32)] * 3
    res = pl.pallas_call(
        body,
        name="adamw_small",
        out_shape=out_shape,
        in_specs=[vmem] * (4 * n),
        out_specs=[vmem] * (3 * n),
        compiler_params=_params(),
    )(*gs, *ws, *ms, *vs)
    return [tuple(res[3 * a : 3 * a + 3]) for a in range(n)]


def _reduce_all(parts, ws, ms, vs, small, early):
    n, ne = len(parts), len(early)
    parts4 = [p.reshape(4, 2, *p.shape[1:]) for p in parts]
    blks = [p.shape[1:] for p in parts]

    def body(*refs):
        refs = list(refs)
        take = lambda k: [refs.pop(0) for _ in range(k)]
        part, w_in, m_in, v_in = take(n), take(n), take(n), take(n)
        (small_ref,) = take(1)
        early_in = [take(5) for _ in range(ne)]
        outs = take(4 * n)
        small_all, small_sum = take(2)
        early_out = [take(4) for _ in range(ne)]
        own, r1, r2 = take(n), take(n), take(n)
        early_buf = take(ne)
        s1_send, s1_recv, s2_send, s2_recv, loc, small_send, small_recv, early_sems = refs
        small_start, small_forward, small_finish = _gather_steps([small_ref], [small_all], small_send, small_recv)
        x, y, c = _place()
        me = 4 * x + 2 * y + c
        landed = [pltpu.make_async_copy(early_in[e][1], early_buf[e], early_sems.at[e, 0]) for e in range(ne)]
        for cp in landed:
            cp.start()
        sibling = (x, y, 1 - c)
        chips = [(1 - x, y), (x, 1 - y), (1 - x, 1 - y)]

        def rowwise(rows, fn):
            chunk = math.gcd(rows, 128)

            def step(i, _):
                fn(pl.ds(pl.multiple_of(i * chunk, chunk), chunk))
                return 0

            lax.fori_loop(0, rows // chunk, step, 0)

        stage1, local = [], []
        for a in range(n):
            cp = pltpu.make_async_remote_copy(
                src_ref=part[a].at[:, 1 - c], dst_ref=r1[a], send_sem=s1_send.at[a], recv_sem=s1_recv.at[a],
                device_id=sibling, device_id_type=MESH)
            cp.start()
            stage1.append(cp)
            lc = pltpu.make_async_copy(part[a].at[:, c], own[a], loc.at[a])
            lc.start()
            local.append(lc)
        small_start()
        stage2 = []
        for a in range(n):
            local[a].wait()
            stage1[a].wait_recv()
            for chip in range(4):

                def add(r, a=a, chip=chip):
                    own[a][chip, r, :] = own[a][chip, r, :] + r1[a][chip, r, :]

                rowwise(blks[a][0], add)
            for k, chip in enumerate(chips):
                cp = pltpu.make_async_remote_copy(
                    src_ref=own[a].at[2 * chip[0] + chip[1]], dst_ref=r2[a].at[k],
                    send_sem=s2_send.at[a, k], recv_sem=s2_recv.at[a, k],
                    device_id=(*chip, c), device_id_type=MESH)
                cp.start()
                stage2.append(cp)
        small_forward()
        for e in range(ne):
            part_e, _, w_e, m_e, v_e = early_in[e]
            g_ref, d_ref, nm_ref, nv_ref = early_out[e]
            landed[e].wait()
            mine = pltpu.make_async_copy(part_e.at[me], early_buf[e].at[me], early_sems.at[e, 1])
            mine.start()
            mine.wait()

            def update_early(r, e=e, w_e=w_e, m_e=m_e, v_e=v_e, g_ref=g_ref, d_ref=d_ref, nm_ref=nm_ref, nv_ref=nv_ref):
                g = early_buf[e][0, r, :]
                for dev in range(1, N_DEV):
                    g = g + early_buf[e][dev, r, :]
                delta, nm, nv = _adamw(w_e[r, :], g, m_e[r, :], v_e[r, :])
                g_ref[r, :] = g
                d_ref[r, :] = delta
                nm_ref[r, :] = nm
                nv_ref[r, :] = nv

            rowwise(early_buf[e].shape[1], update_early)
        for a in range(n):
            for k, chip in enumerate(chips):
                stage2[3 * a + k].wait_recv()
            g_ref, d_ref, nm_ref, nv_ref = outs[4 * a : 4 * a + 4]

            def update(r, a=a, g_ref=g_ref, d_ref=d_ref, nm_ref=nm_ref, nv_ref=nv_ref):
                g = own[a][2 * x + y, r, :] + r2[a][0, r, :] + r2[a][1, r, :] + r2[a][2, r, :]
                delta, nm, nv = _adamw(w_in[a][r, :], g, m_in[a][r, :], v_in[a][r, :])
                g_ref[r, :] = g
                d_ref[r, :] = delta
                nm_ref[r, :] = nm
                nv_ref[r, :] = nv

            rowwise(blks[a][0], update)
        small_finish()
        _ordered_sum(small_all, small_sum)
        for cp in stage1 + stage2:
            cp.wait_send()

    vmem = pl.BlockSpec(memory_space=pltpu.VMEM)
    hbm = pl.BlockSpec(memory_space=pl.ANY)
    out_shape = []
    for b in blks:
        out_shape += [jax.ShapeDtypeStruct(b, F32)] * 4
    out_shape += [jax.ShapeDtypeStruct((N_DEV, *small.shape), F32), jax.ShapeDtypeStruct(small.shape, F32)]
    for e in early:
        out_shape += [jax.ShapeDtypeStruct(e[2].shape, F32)] * 4
    scratch = (
        [pltpu.VMEM((4, *b), F32) for b in blks]
        + [pltpu.VMEM((4, *b), F32) for b in blks]
        + [pltpu.VMEM((3, *b), F32) for b in blks]
        + [pltpu.VMEM(e[0].shape, F32) for e in early]
        + [pltpu.SemaphoreType.DMA((n,)), pltpu.SemaphoreType.DMA((n,)), pltpu.SemaphoreType.DMA((n, 3)),
           pltpu.SemaphoreType.DMA((n, 3)), pltpu.SemaphoreType.DMA((n,)),
           pltpu.SemaphoreType.DMA((1, 7)), pltpu.SemaphoreType.DMA((1, 7)), pltpu.SemaphoreType.DMA((ne, 2))]
    )
    res = pl.pallas_call(
        body,
        name="reduce_all",
        out_shape=out_shape,
        in_specs=[hbm] * n + [vmem] * (3 * n + 1) + [hbm, hbm, vmem, vmem, vmem] * ne,
        out_specs=[vmem] * (4 * n + 2 + 4 * ne),
        scratch_shapes=scratch,
        compiler_params=_params(),
    )(*parts4, *ws, *ms, *vs, small, *[t for e in early for t in e])
    late = [tuple(res[4 * a : 4 * a + 4]) for a in range(n)]
    at = 4 * n + 2
    return late + [tuple(res[at + 4 * e : at + 4 * e + 4]) for e in range(ne)], res[4 * n + 1]


def _rms(x):
    return lax.rsqrt(jnp.mean(x * x, axis=-1, keepdims=True) + EPS)


def _token_tile(tokens, want):
    tile = min(want, tokens // 2)
    assert tokens % tile == 0 and tile % 16 == 0
    return tile


def _in_proj(x2, g_pre, w_in_blk, shards):
    tokens, d = x2.shape
    nb = w_in_blk.shape[1]
    tm = _token_tile(tokens, 2048)
    n_t = tokens // tm
    ns = len(shards)
    x_pos, y_pos, c_pos = _place()
    slot = lambda px, py, pc: 4 * px + 2 * py + pc
    chip_order = [(x_pos, y_pos), (1 - x_pos, y_pos), (x_pos, 1 - y_pos), (1 - x_pos, 1 - y_pos)]
    order = jnp.stack([slot(px, py, pc) for px, py in chip_order for pc in (c_pos, 1 - c_pos)]).astype(jnp.int32)

    def body(order_ref, x_ref, g_ref, w_ref, *rest):
        shard_hbm, proj_ref, w_hbm = rest[:ns], rest[ns], rest[ns + 1]
        gathered = rest[ns + 2 : 2 * ns + 2]
        h_all, land, w_send, w_recv, out_sem, send_sems, recv_sems, own_sems = rest[2 * ns + 2 :]
        j, i = pl.program_id(0), pl.program_id(1)
        x, y, c = _place()
        me, sibling = (x, y, c), (x, y, 1 - c)
        chips = [(1 - x, y), (x, 1 - y), (1 - x, 1 - y)]
        own = [pltpu.make_async_copy(shard_hbm[a], gathered[a].at[slot(*me)], own_sems.at[a]) for a in range(ns)]
        start, finish = _exchange_steps(ns, lambda a, s: shard_hbm[a], gathered, send_sems, recv_sems)

        def copy(k, block, to):
            ref = land.at[slot(*block)]
            return pltpu.make_async_remote_copy(
                src_ref=ref, dst_ref=ref, send_sem=w_send.at[k], recv_sem=w_recv.at[k], device_id=to, device_id_type=MESH)

        first_sends = [copy(0, me, sibling)] + [copy(1 + k, me, (*chip, c)) for k, chip in enumerate(chips)]
        forwards = [copy(4 + k, (*chip, c), sibling) for k, chip in enumerate(chips)]

        @pl.when((j == 0) & (i == 0))
        def _():
            land[slot(*me)] = w_ref[...].astype(BF16)
            for cp in first_sends:
                cp.start()
            start()
            for cp in own:
                cp.start()

        @pl.when((j == 1) & (i == 0))
        def _():
            copy(0, sibling, me).wait_recv()

        for k, chip in enumerate(chips):

            @pl.when((j == 2 + 2 * k) & (i == 0))
            def _(k=k, chip=chip):
                copy(1 + k, (*chip, c), me).wait_recv()
                forwards[k].start()

            @pl.when((j == 3 + 2 * k) & (i == 0))
            def _(k=k, chip=chip):
                copy(4 + k, (*chip, 1 - c), me).wait_recv()

        rows = pl.ds(pl.multiple_of(i * tm, tm), tm)

        @pl.when(j == 0)
        def _():
            x_t = x_ref[...]
            h_all[rows, :] = (x_t * _rms(x_t) * g_ref[...]).astype(BF16)

        proj_ref[...] = _mm(h_all[rows, :], land[order_ref[j]])

        @pl.when((j == N_DEV - 1) & (i == n_t - 1))
        def _():
            for cp in first_sends + forwards:
                cp.wait_send()
            finish()
            for cp in own:
                cp.wait()
            out = pltpu.make_async_copy(land, w_hbm, out_sem)
            out.start()
            out.wait()

    hbm = pl.BlockSpec(memory_space=pl.ANY)
    res = pl.pallas_call(
        body,
        name="in_proj",
        grid_spec=pltpu.PrefetchScalarGridSpec(
            num_scalar_prefetch=1,
            grid=(N_DEV, n_t),
            in_specs=[
                pl.BlockSpec((tm, d), lambda j, i, order: (jnp.where(j == 0, i, n_t - 1), 0)),
                pl.BlockSpec((1, d), lambda j, i, order: (0, 0)),
                pl.BlockSpec((d, nb), lambda j, i, order: (0, 0)),
            ]
            + [hbm] * ns,
            out_specs=[pl.BlockSpec((tm, nb), lambda j, i, order: (i, order[j])), hbm] + [hbm] * ns,
            scratch_shapes=[
                pltpu.VMEM((tokens, d), BF16),
                pltpu.VMEM((N_DEV, d, nb), BF16),
                pltpu.SemaphoreType.DMA((7,)),
                pltpu.SemaphoreType.DMA((7,)),
                pltpu.SemaphoreType.DMA,
                pltpu.SemaphoreType.DMA((ns, 7)),
                pltpu.SemaphoreType.DMA((ns, 7)),
                pltpu.SemaphoreType.DMA((ns,)),
            ],
        ),
        out_shape=[
            jax.ShapeDtypeStruct((tokens, N_DEV * nb), F32),
            jax.ShapeDtypeStruct((N_DEV, d, nb), BF16),
        ]
        + [jax.ShapeDtypeStruct((N_DEV, *a.shape), a.dtype) for a in shards],
        compiler_params=_params("arbitrary", "arbitrary"),
    )(order, x2, g_pre, w_in_blk, *shards)
    return res[0], res[1], res[2:]


def _in_proj_bwd(x2, dres, du_pool, du_ssm, dq, dgate, g_pre, w_in_t):
    tokens, d = x2.shape
    nb = w_in_t.shape[0] // N_DEV
    pool_w, ssm_w, att_w, mix = du_pool.shape[1], du_ssm.shape[1], dq.shape[1], dgate.shape[1]
    cl = _token_tile(tokens, 512)
    n_tiles = tokens // cl

    def body(x_ref, dres_ref, dup_ref, dus_ref, dq_ref, dgate_ref, g_ref, w_ref, gx_ref, gw_hbm, gg_ref, acc, sem):
        i = pl.program_id(0)

        @pl.when(i == 0)
        def _():
            acc[...] = jnp.zeros_like(acc)
            gg_ref[...] = jnp.zeros_like(gg_ref)

        x = x_ref[...]
        r = _rms(x)
        xn = x * r
        g = g_ref[...]
        h = (xn * g).astype(BF16)
        dproj = jnp.concatenate([dup_ref[...], dus_ref[...], dq_ref[...], dgate_ref[...]], axis=1).astype(BF16)
        dh = _mm(dproj, w_ref[...])
        for j in range(N_DEV):
            acc[j] += _mm_tn(h, dproj[:, j * nb : (j + 1) * nb])
        gg_ref[...] += jnp.sum(dh * xn, axis=0, keepdims=True)
        dxn = dh * g
        gx_ref[...] = dres_ref[...] + r * (dxn - xn * jnp.mean(dxn * xn, axis=-1, keepdims=True))

        @pl.when(i == n_tiles - 1)
        def _():
            cp = pltpu.make_async_copy(acc, gw_hbm, sem)
            cp.start()
            cp.wait()

    return pl.pallas_call(
        body,
        name="in_proj_bwd",
        grid=(n_tiles,),
        in_specs=[
            pl.BlockSpec((cl, d), lambda i: (i, 0)),
            pl.BlockSpec((cl, d), lambda i: (i, 0)),
            pl.BlockSpec((cl, pool_w), lambda i: (i, 0)),
            pl.BlockSpec((cl, ssm_w), lambda i: (i, 0)),
            pl.BlockSpec((cl, att_w), lambda i: (i, 0)),
            pl.BlockSpec((cl, mix), lambda i: (i, 0)),
            pl.BlockSpec((1, d), lambda i: (0, 0)),
            pl.BlockSpec(w_in_t.shape, lambda i: (0, 0)),
        ],
        out_specs=[
            pl.BlockSpec((cl, d), lambda i: (i, 0)),
            pl.BlockSpec(memory_space=pl.ANY),
            pl.BlockSpec((1, d), lambda i: (0, 0)),
        ],
        out_shape=[
            jax.ShapeDtypeStruct((tokens, d), F32),
            jax.ShapeDtypeStruct((N_DEV, d, nb), F32),
            jax.ShapeDtypeStruct((1, d), F32),
        ],
        scratch_shapes=[pltpu.VMEM((N_DEV, d, nb), F32), pltpu.SemaphoreType.DMA],
        compiler_params=_params("arbitrary"),
    )(x2, dres, du_pool, du_ssm, dq, dgate, g_pre, w_in_t)


def _pool_geometry(seq, width):
    gw = width // len(POOL_WINDOWS)
    col = lax.broadcasted_iota(jnp.int32, (1, width), 1)
    win = jnp.full((1, width), float(POOL_WINDOWS[-1]), F32)
    for gi in range(len(POOL_WINDOWS) - 2, -1, -1):
        win = jnp.where(col < (gi + 1) * gw, float(POOL_WINDOWS[gi]), win)
    row = lax.broadcasted_iota(jnp.int32, (seq, width), 0)
    filling = 1.0 / (lax.broadcasted_iota(jnp.int32, (seq, 1), 0) + 1).astype(F32)
    inv_cnt = jnp.where(row + 1 < win.astype(jnp.int32), filling, 1.0 / win)
    return win, row, inv_cnt


def _window_sums(a, win, seq, back):
    pad = 2 * POOL_WINDOWS[-1]
    zeros = jnp.zeros((pad, a.shape[1]), F32)
    s = jnp.concatenate([a, zeros] if back else [zeros, a], axis=0)
    sums = []
    k = 1
    while k < POOL_WINDOWS[-1]:
        s = s + pltpu.roll(s, seq + pad - k if back else k, 0)
        k *= 2
        sums.append((k, s))
    out = sums[-1][1]
    for k, s in reversed(sums[:-1]):
        out = jnp.where(win <= float(k), s, out)
    return out[0:seq] if back else out[pad : pad + seq]


def _pool_fwd(u2, wp_blk, scale, batch, seq):
    width = scale.shape[1]

    def body(u_ref, w_ref, s_ref, y_ref):
        u = u_ref[...]
        win, row, inv_cnt = _pool_geometry(seq, width)
        diff = _window_sums(u, win, seq, False) * inv_cnt - u
        y_ref[...] =_mm(diff.astype(BF16), w_ref[...]) * s_ref[...]

    return pl.pallas_call(
        body,
        name="pool_fwd",
        grid=(batch,),
        in_specs=[
            pl.BlockSpec((seq, width), lambda b: (b, 0)),
            pl.BlockSpec((width, width), lambda b: (0, 0)),
            pl.BlockSpec((1, width), lambda b: (0, 0)),
        ],
        out_specs=pl.BlockSpec((seq, width), lambda b: (b, 0)),
        out_shape=jax.ShapeDtypeStruct((u2.shape[0], width), F32),
        compiler_params=_params("arbitrary"),
    )(u2, wp_blk, scale)


def _pool_bwd(u2, dy2, wp_blk, scale, batch, seq):
    width = scale.shape[1]

    def body(u_ref, dy_ref, w_ref, s_ref, du_ref, gw_ref, gs_ref):
        @pl.when(pl.program_id(0) == 0)
        def _():
            gw_ref[...] = jnp.zeros_like(gw_ref)
            gs_ref[...] = jnp.zeros_like(gs_ref)

        u = u_ref[...]
        dy = dy_ref[...]
        win, row, inv_cnt = _pool_geometry(seq, width)
        diff = (_window_sums(u, win, seq, False) * inv_cnt - u).astype(BF16)
        gs_ref[...] += jnp.sum(dy * _mm(diff, w_ref[...]), axis=0, keepdims=True)
        dys = (dy * s_ref[...]).astype(BF16)
        gw_ref[...] += _mm_tn(diff, dys)
        dd = _mm_nt(dys, w_ref[...])
        du_ref[...] = _window_sums(dd * inv_cnt, win, seq, True) - dd

    return pl.pallas_call(
        body,
        name="pool_bwd",
        grid=(batch,),
        in_specs=[
            pl.BlockSpec((seq, width), lambda b: (b, 0)),
            pl.BlockSpec((seq, width), lambda b: (b, 0)),
            pl.BlockSpec((width, width), lambda b: (0, 0)),
            pl.BlockSpec((1, width), lambda b: (0, 0)),
        ],
        out_specs=[
            pl.BlockSpec((seq, width), lambda b: (b, 0)),
            pl.BlockSpec((width, width), lambda b: (0, 0)),
            pl.BlockSpec((1, width), lambda b: (0, 0)),
        ],
        out_shape=[
            jax.ShapeDtypeStruct((u2.shape[0], width), F32),
            jax.ShapeDtypeStruct((width, width), F32),
            jax.ShapeDtypeStruct((1, width), F32),
        ],
        compiler_params=_params("arbitrary"),
    )(u2, dy2, wp_blk, scale)


def _state_row(z, n_blocks):
    re = jnp.real(z).reshape(n_blocks, -1)
    im = jnp.imag(z).reshape(n_blocks, -1)
    return jnp.concatenate([re, im], axis=1).reshape(1, -1)


def _ssm_tables(a_re, a_im, log_dt, b_re, b_im, c_re, c_im):
    groups, n_state = a_re.shape
    ch = b_re.shape[2]
    nb = groups * ch // LANES
    gl = groups // nb
    lam = lax.complex(a_re, a_im)
    lam_bar = jnp.exp(lam * jnp.exp(log_dt)[:, None])
    b_bar = ((lam_bar - 1.0) / lam)[..., None] * lax.complex(b_re, b_im)
    eye = jnp.eye(gl, dtype=F32)

    def rows_to_state(t):
        return jnp.einsum("sgnc,gh->sgchn", t.reshape(nb, gl, n_state, ch), eye).reshape(nb, gl * ch, gl * n_state)

    def state_to_rows(t):
        return jnp.einsum("sgcn,gh->shngc", t.reshape(nb, gl, ch, n_state), eye).reshape(nb, gl * n_state, gl * ch)

    b_tab = jnp.concatenate([rows_to_state(jnp.real(b_bar)), rows_to_state(jnp.imag(b_bar))], axis=2)
    c_tab = jnp.concatenate([state_to_rows(c_re), -state_to_rows(c_im)], axis=1)
    return _state_row(lam_bar, nb), b_tab, c_tab


def _lam_power(a_re, a_im, log_dt, power, scale, n_blocks):
    return _state_row(scale * jnp.exp(lax.complex(a_re, a_im) * jnp.exp(log_dt)[:, None] * power), n_blocks)


def _state_blocks(s2, n_blocks, width):
    half = s2 // n_blocks // 2
    assert half % width == 0
    return [(b * 2 * half + o, b * 2 * half + half + o) for b in range(n_blocks) for o in range(0, half, width)]


def _scan(src_ref, dst_ref, st_ref, lam8_ref, n_groups, s, n_blocks, reverse, store):
    lb = 512
    for re0, im0 in _state_blocks(2 * s, n_blocks, lb):
        cr, ci = pl.ds(re0, lb), pl.ds(im0, lb)
        lr = lam8_ref[:, cr]
        li = -lam8_ref[:, ci] if reverse else lam8_ref[:, ci]

        def step(i, carry, cr=cr, ci=ci, lr=lr, li=li):
            hr, hi = carry
            grp = n_groups - 1 - i if reverse else i
            rows = pl.ds(pl.multiple_of(grp * SUBLANES, SUBLANES), SUBLANES)
            nr = lr * hr - li * hi + src_ref[rows, cr]
            ni = lr * hi + li * hr + src_ref[rows, ci]
            if store:
                dst_ref[rows, cr] = nr
                dst_ref[rows, ci] = ni
            return nr, ni

        hr, hi = lax.fori_loop(0, n_groups, step, (st_ref[:, cr], st_ref[:, ci]), unroll=2)
        st_ref[:, cr] = hr
        st_ref[:, ci] = hi


def _pack_state(re, im):
    hi = lax.bitcast_convert_type(re.astype(BF16).astype(F32), jnp.uint32)
    lo = lax.bitcast_convert_type(im.astype(BF16).astype(F32), jnp.uint32)
    return hi | (lo >> 16)


def _unpack_state(word):
    re = lax.bitcast_convert_type(word & jnp.uint32(0xFFFF0000), F32)
    im = lax.bitcast_convert_type(word << 16, F32)
    return re, im


def _scan_adjoint(dh_ref, hprev_ref, group0, stg_ref, acc_ref, lam8_ref, n_groups, s, n_blocks):
    lb = 512
    half = s // n_blocks
    for re0, im0 in _state_blocks(2 * s, n_blocks, lb):
        cr, ci = pl.ds(re0, lb), pl.ds(im0, lb)
        ch = pl.ds(re0 // (2 * half) * half + re0 % (2 * half), lb)
        lr, li = lam8_ref[:, cr], -lam8_ref[:, ci]

        def step(i, carry, cr=cr, ci=ci, ch=ch, lr=lr, li=li):
            gr, gi, ar, ai = carry
            grp = n_groups - 1 - i
            rows = pl.ds(pl.multiple_of(grp * SUBLANES, SUBLANES), SUBLANES)
            ngr = lr * gr - li * gi + dh_ref[rows, cr]
            ngi = lr * gi + li * gr + dh_ref[rows, ci]
            hr, hi = _unpack_state(hprev_ref[pl.ds(pl.multiple_of((group0 + grp) * SUBLANES, SUBLANES), SUBLANES), ch])
            ar = ar + hr * ngr + hi * ngi
            ai = ai + hr * ngi - hi * ngr
            dh_ref[rows, cr] = ngr
            dh_ref[rows, ci] = ngi
            return ngr, ngi, ar, ai

        init = (stg_ref[:, cr], stg_ref[:, ci], acc_ref[:, cr], acc_ref[:, ci])
        gr, gi, ar, ai = lax.fori_loop(0, n_groups, step, init)
        stg_ref[:, cr] = gr
        stg_ref[:, ci] = gi
        acc_ref[:, cr] = ar
        acc_ref[:, ci] = ai


def _chunk_starts(st_ref, init_ref, lcl_ref, s, n_blocks):
    w = s // n_blocks
    init_ref[0:1, :] = jnp.zeros((1, 2 * s), F32)
    for re0, im0 in _state_blocks(2 * s, n_blocks, w):
        re, im = pl.ds(re0, w), pl.ds(im0, w)
        ar, ai = lcl_ref[:, re], lcl_ref[:, im]
        cr = jnp.zeros((1, w), F32)
        ci = jnp.zeros((1, w), F32)
        for k in range(1, NCH):
            cr, ci = (ar * cr - ai * ci + st_ref[k - 1 : k, re], ar * ci + ai * cr + st_ref[k - 1 : k, im])
            init_ref[k : k + 1, re] = cr
            init_ref[k : k + 1, im] = ci


def _chunk_starts_adjoint(stg_ref, initg_ref, lcl_ref, s, n_blocks):
    w = s // n_blocks
    initg_ref[NCH - 1 : NCH, :] = jnp.zeros((1, 2 * s), F32)
    for re0, im0 in _state_blocks(2 * s, n_blocks, w):
        re, im = pl.ds(re0, w), pl.ds(im0, w)
        ar, ai = lcl_ref[:, re], -lcl_ref[:, im]
        gr = jnp.zeros((1, w), F32)
        gi = jnp.zeros((1, w), F32)
        for k in range(NCH - 2, -1, -1):
            gr, gi = (stg_ref[k + 1 : k + 2, re] + ar * gr - ai * gi, stg_ref[k + 1 : k + 2, im] + ar * gi + ai * gr)
            initg_ref[k : k + 1, re] = gr
            initg_ref[k : k + 1, im] = gi


def _ssm_rows(seq, want):
    rows = min(want, seq // 2)
    assert seq % rows == 0 and rows % SUBLANES == 0
    return rows


def _chunk_copies(hbm_ref, b, cm_ref, sems, to_cm, col0=0):
    cl, _, width = cm_ref.shape
    copies = []
    for k in range(NCH):
        nat, cm = hbm_ref.at[b, pl.ds(k * cl, cl), pl.ds(col0, width)], cm_ref.at[:, k, :]
        src, dst = (nat, cm) if to_cm else (cm, nat)
        copies.append(pltpu.make_async_copy(src, dst, sems.at[k]))
    return copies


def _blockwise(fn, n_blocks):
    return jnp.concatenate([fn(b) for b in range(n_blocks)], axis=1)


def _ssm_fwd(u, u_col, b_tab, c_tab, lam8, lcl, d_skip, w_glu):
    batch, seq, _ = u.shape
    width = d_skip.shape[1]
    s = lam8.shape[1] // 2
    nb = b_tab.shape[0]
    sb = 2 * s // nb
    cl = seq // NCH
    rows = _ssm_rows(seq, 1024)
    n_tiles = seq // rows
    n_groups = rows // SUBLANES

    def body(u_hbm, b_ref, c_ref, lam_ref, lcl_ref, d_ref, wg_ref, y_hbm, pre_ref, z_ref, init_ref,
             u_cm, y_cm, bu_all, st, sems):
        b, ph, t = pl.program_id(0), pl.program_id(1), pl.program_id(2)
        tile_groups = pl.ds(pl.multiple_of(t * n_groups, n_groups), n_groups)

        @pl.when((ph == 0) & (t == 0))
        def _():
            loads = _chunk_copies(u_hbm, b, u_cm, sems, True, u_col)
            for cp in loads:
                cp.start()
            st[...] = jnp.zeros_like(st)
            for cp in loads:
                cp.wait()

        @pl.when((ph == 1) & (t == 0))
        def _():
            st[...] = init_ref[...]

        u_t = u_cm[tile_groups].reshape(rows, width)
        bu = bu_all.at[pl.ds(pl.multiple_of(t * rows, rows), rows)]

        @pl.when(ph == 0)
        def _():
            u_b = u_t.astype(BF16)
            for blk in range(nb):
                bu[:, blk * sb : (blk + 1) * sb] = _mm(u_b[:, blk * LANES : (blk + 1) * LANES], b_ref[blk])
            _scan(bu, bu, st, lam_ref, n_groups, s, nb, False, False)

        @pl.when((ph == 0) & (t == n_tiles - 1))
        def _():
            _chunk_starts(st, init_ref, lcl_ref, s, nb)

        @pl.when(ph == 1)
        def _():
            _scan(bu, bu, st, lam_ref, n_groups, s, nb, False, True)
            hs = lambda blk: _mm(bu[:, blk * sb : (blk + 1) * sb].astype(BF16), c_ref[blk])
            pre = _blockwise(hs, nb) + d_ref[...] * u_t
            z = _mm(_gelu(pre).astype(BF16), wg_ref[...])
            pre_ref[...] = pre
            z_ref[...] = z
            y = z[:, 0:width] * jax.nn.sigmoid(z[:, width : 2 * width])
            y_cm[tile_groups] = y.reshape(n_groups, SUBLANES, width)

        @pl.when((ph == 1) & (t == n_tiles - 1))
        def _():
            stores = _chunk_copies(y_hbm, b, y_cm, sems, False)
            for cp in stores:
                cp.start()
            for cp in stores:
                cp.wait()

    out_tile = lambda b, ph, t: (b, t * ph, 0)
    full = lambda a: pl.BlockSpec(a.shape, lambda b, ph, t: (0,) * a.ndim)
    hbm = pl.BlockSpec(memory_space=pl.ANY)
    return pl.pallas_call(
        body,
        name="ssm_fwd",
        grid=(batch, 2, n_tiles),
        in_specs=[hbm, full(b_tab), full(c_tab), full(lam8), full(lcl), full(d_skip), full(w_glu)],
        out_specs=[
            hbm,
            pl.BlockSpec((None, rows, width), out_tile),
            pl.BlockSpec((None, rows, 2 * width), out_tile),
            pl.BlockSpec((None, SUBLANES, 2 * s), lambda b, ph, t: (b, 0, 0)),
        ],
        out_shape=[
            jax.ShapeDtypeStruct((batch, seq, width), F32),
            jax.ShapeDtypeStruct((batch, seq, width), F32),
            jax.ShapeDtypeStruct((batch, seq, 2 * width), F32),
            jax.ShapeDtypeStruct((batch, SUBLANES, 2 * s), F32),
        ],
        scratch_shapes=[
            pltpu.VMEM((cl, NCH, width), F32),
            pltpu.VMEM((cl, NCH, width), F32),
            pltpu.VMEM((seq, 2 * s), F32),
            pltpu.VMEM((SUBLANES, 2 * s), F32),
            pltpu.SemaphoreType.DMA((NCH,)),
        ],
        compiler_params=_params("arbitrary", "arbitrary", "arbitrary"),
    )(u, b_tab, c_tab, lam8, lcl, d_skip, w_glu)


def _ssm_bwd(u, u_col, pre_p, z_p, dy, init, b_tab, b_tab_t, c_tab_t, lam8, lcl, d_skip, w_glu, ready):
    batch, seq, _ = u.shape
    width = d_skip.shape[1]
    nr = len(ready)
    s = lam8.shape[1] // 2
    nb = b_tab.shape[0]
    sb = 2 * s // nb
    cl = seq // NCH
    rows = _ssm_rows(seq, 512)
    n_tiles = seq // rows
    n_groups = rows // SUBLANES

    def body(u_hbm, pre_ref, z_ref, dy_hbm, init_ref, b_ref, bt_ref, ct_ref, lam_ref, lcl_ref, d_ref, wg_ref, *rest):
        ready_hbm, rest = rest[:nr], rest[nr:]
        du_hbm, gb_ref, gc_ref, gwg_ref, gd_ref, glam_ref = rest[:6]
        landed_hbm, rest = rest[6 : 6 + nr], rest[6 + nr :]
        u_cm, dy_cm, work, hs_all, dpre_all, st, stg, initg, acc, sems, send_sems, recv_sems = rest
        b, ph, t = pl.program_id(0), pl.program_id(1), pl.program_id(2)
        half = s // nb
        exchange_start, exchange_finish = _exchange_steps(
            nr, lambda a, slot: ready_hbm[a].at[slot], landed_hbm, send_sems, recv_sems)
        first = (b == 0) & (ph == 0) & (t == 0)
        last = (b == batch - 1) & (ph == 2) & (t == n_tiles - 1)
        tile = jnp.where(ph == 0, t, n_tiles - 1 - t)
        tile_rows = pl.ds(pl.multiple_of(tile * rows, rows), rows)
        tile_groups = pl.ds(pl.multiple_of(tile * n_groups, n_groups), n_groups)
        lanes = lambda blk: slice(blk * LANES, (blk + 1) * LANES)
        states = lambda blk: slice(blk * sb, (blk + 1) * sb)

        @pl.when(first)
        def _():
            exchange_start()
            acc[...] = jnp.zeros_like(acc)
            gb_ref[...] = jnp.zeros_like(gb_ref)
            gc_ref[...] = jnp.zeros_like(gc_ref)
            gwg_ref[...] = jnp.zeros_like(gwg_ref)
            gd_ref[...] = jnp.zeros_like(gd_ref)

        @pl.when((ph == 0) & (t == 0))
        def _():
            loads = (_chunk_copies(u_hbm, b, u_cm, sems.at[0], True, u_col)
                     + _chunk_copies(dy_hbm, b, dy_cm, sems.at[1], True))
            for cp in loads:
                cp.start()
            st[...] = init_ref[...]
            for blk in range(nb):
                entry = init_ref[:, states(blk)]
                hs_all[0:SUBLANES, blk * half : (blk + 1) * half] = _pack_state(entry[:, 0:half], entry[:, half : 2 * half])
            for cp in loads:
                cp.wait()

        u_t = u_cm[tile_groups].reshape(rows, width)
        u_b = u_t.astype(BF16)

        @pl.when(ph == 0)
        def _():
            for blk in range(nb):
                work[:, states(blk)] = _mm(u_b[:, lanes(blk)], b_ref[blk])
            _scan(work, work, st, lam_ref, n_groups, s, nb, False, True)
            z = z_ref[...]
            dy_t = dy_cm[tile_groups].reshape(rows, width)
            pre = pre_ref[...]
            z1, sig = z[:, 0:width], jax.nn.sigmoid(z[:, width : 2 * width])
            dz = jnp.concatenate([dy_t * sig, dy_t * z1 * sig * (1.0 - sig)], axis=1).astype(BF16)
            gwg_ref[...] += _mm_tn(_gelu(pre).astype(BF16), dz)
            dpre = _mm_nt(dz, wg_ref[...]) * _gelu_grad(pre)
            dpre_all[tile_rows, :] = dpre
            gd_ref[...] += jnp.sum(dpre * u_t, axis=0, keepdims=True)
            dpre_b = dpre.astype(BF16)
            kept = pl.ds(pl.multiple_of(tile * rows + SUBLANES, SUBLANES), rows)
            for blk in range(nb):
                hs = work[:, states(blk)]
                gc_ref[blk] += _mm_tn(hs.astype(BF16), dpre_b[:, lanes(blk)])
                hs_all[kept, blk * half : (blk + 1) * half] = _pack_state(hs[:, 0:half], hs[:, half : 2 * half])

        @pl.when(ph >= 1)
        def _():
            dpre_b = dpre_all[tile_rows, :].astype(BF16)
            for blk in range(nb):
                work[:, states(blk)] = _mm(dpre_b[:, lanes(blk)], ct_ref[blk])

        @pl.when(ph == 1)
        def _():
            @pl.when(t == 0)
            def _():
                stg[...] = jnp.zeros_like(stg)

            _scan(work, work, stg, lam_ref, n_groups, s, nb, True, False)

            @pl.when(t == n_tiles - 1)
            def _():
                _chunk_starts_adjoint(stg, initg, lcl_ref, s, nb)

        @pl.when(ph == 2)
        def _():
            @pl.when(t == 0)
            def _():
                stg[...] = initg[...]

            _scan_adjoint(work, hs_all, tile * n_groups, stg, acc, lam_ref, n_groups, s, nb)
            du = lambda blk: _mm(work[:, states(blk)].astype(BF16), bt_ref[blk])
            du_t = _blockwise(du, nb) + dpre_all[tile_rows, :] * d_ref[...]
            dy_cm[tile_groups] = du_t.reshape(n_groups, SUBLANES, width)
            for blk in range(nb):
                gb_ref[blk] += _mm_tn(u_b[:, lanes(blk)], work[:, states(blk)].astype(BF16))

            @pl.when(t == n_tiles - 1)
            def _():
                stores = _chunk_copies(du_hbm, b, dy_cm, sems.at[0], False)
                for cp in stores:
                    cp.start()
                for cp in stores:
                    cp.wait()

        @pl.when(last)
        def _():
            glam_ref[...] = jnp.sum(acc[...], axis=0, keepdims=True)
            exchange_finish()

    def tile(b, ph, t):
        return (b, jnp.where(ph == 0, t, n_tiles - 1 - t), 0)

    full = lambda a: pl.BlockSpec(a.shape, lambda b, ph, t: (0,) * a.ndim)
    hbm = pl.BlockSpec(memory_space=pl.ANY)
    res = pl.pallas_call(
        body,
        name="ssm_bwd",
        grid=(batch, 3, n_tiles),
        in_specs=[
            hbm,
            pl.BlockSpec((None, rows, width), tile),
            pl.BlockSpec((None, rows, 2 * width), tile),
            hbm,
            pl.BlockSpec((None, SUBLANES, 2 * s), lambda b, ph, t: (b, 0, 0)),
            full(b_tab), full(b_tab_t), full(c_tab_t), full(lam8), full(lcl), full(d_skip), full(w_glu),
        ]
        + [hbm] * nr,
        out_specs=[
            hbm,
            full(b_tab), full(b_tab_t), full(w_glu), full(d_skip),
            pl.BlockSpec((1, 2 * s), lambda b, ph, t: (0, 0)),
        ]
        + [hbm] * nr,
        out_shape=[
            jax.ShapeDtypeStruct((batch, seq, width), F32),
            jax.ShapeDtypeStruct(b_tab.shape, F32),
            jax.ShapeDtypeStruct(b_tab_t.shape, F32),
            jax.ShapeDtypeStruct(w_glu.shape, F32),
            jax.ShapeDtypeStruct(d_skip.shape, F32),
            jax.ShapeDtypeStruct((1, 2 * s), F32),
        ]
        + [jax.ShapeDtypeStruct(a.shape, F32) for a in ready],
        scratch_shapes=[
            pltpu.VMEM((cl, NCH, width), F32),
            pltpu.VMEM((cl, NCH, width), F32),
            pltpu.VMEM((rows, 2 * s), F32),
            pltpu.VMEM((seq + SUBLANES, s), jnp.uint32),
            pltpu.VMEM((seq, width), F32),
        ]
        + [pltpu.VMEM((SUBLANES, 2 * s), F32)] * 4
        + [pltpu.SemaphoreType.DMA((2, NCH)), pltpu.SemaphoreType.DMA((nr, 7)), pltpu.SemaphoreType.DMA((nr, 7))],
        compiler_params=_params("arbitrary", "arbitrary", "arbitrary"),
    )(u, pre_p, z_p, dy, init, b_tab, b_tab_t, c_tab_t, lam8, lcl, d_skip, w_glu, *ready)
    return res[:6], res[6:]


def _kv_fwd(mem, g_mem, w_kv):
    batch, n_mem, d = mem.shape
    kvw = w_kv.shape[1]

    def body(mem_ref, g_ref, w_ref, kv_ref):
        m = mem_ref[...]
        kv_ref[...] = _mm((m * _rms(m) * g_ref[...]).astype(BF16), w_ref[...])

    return pl.pallas_call(
        body,
        name="kv_fwd",
        grid=(batch,),
        in_specs=[
            pl.BlockSpec((None, n_mem, d), lambda b: (b, 0, 0)),
            pl.BlockSpec((1, d), lambda b: (0, 0)),
            pl.BlockSpec((d, kvw), lambda b: (0, 0)),
        ],
        out_specs=pl.BlockSpec((None, n_mem, kvw), lambda b: (b, 0, 0)),
        out_shape=jax.ShapeDtypeStruct((batch, n_mem, kvw), F32),
        compiler_params=_params("arbitrary"),
    )(mem, g_mem, w_kv)


def _kv_bwd(mem, dkv, g_mem, w_kv):
    batch, n_mem, d = mem.shape
    kvw = w_kv.shape[1]

    def body(mem_ref, dkv_ref, g_ref, w_ref, gw_ref, gg_ref):
        @pl.when(pl.program_id(0) == 0)
        def _():
            gw_ref[...] = jnp.zeros_like(gw_ref)
            gg_ref[...] = jnp.zeros_like(gg_ref)

        m = mem_ref[...]
        mn = m * _rms(m)
        dkv_b = dkv_ref[...].astype(BF16)
        gw_ref[...] += _mm_tn((mn * g_ref[...]).astype(BF16), dkv_b)
        gg_ref[...] += jnp.sum(_mm_nt(dkv_b, w_ref[...]) * mn, axis=0, keepdims=True)

    return pl.pallas_call(
        body,
        name="kv_bwd",
        grid=(batch,),
        in_specs=[
            pl.BlockSpec((None, n_mem, d), lambda b: (b, 0, 0)),
            pl.BlockSpec((None, n_mem, kvw), lambda b: (b, 0, 0)),
            pl.BlockSpec((1, d), lambda b: (0, 0)),
            pl.BlockSpec((d, kvw), lambda b: (0, 0)),
        ],
        out_specs=[pl.BlockSpec((d, kvw), lambda b: (0, 0)), pl.BlockSpec((1, d), lambda b: (0, 0))],
        out_shape=[jax.ShapeDtypeStruct((d, kvw), F32), jax.ShapeDtypeStruct((1, d), F32)],
        compiler_params=_params("arbitrary"),
    )(mem, dkv, g_mem, w_kv)


def _tail(x2, target2, proj, y_pool, y_ssm, kv, w_out, g_post):
    tokens, d = x2.shape
    batch, n_mem, kvw = kv.shape
    pool_w, ssm_w, att_w, mix = y_pool.shape[1], y_ssm.shape[1], kvw // 2, w_out.shape[0]
    assert (mix - att_w) % att_w == 0 and proj.shape[1] == 2 * mix
    hd = att_w // MEM_HEADS
    cl = _token_tile(tokens // batch, 512)
    n_tiles = tokens // cl
    per_seq = tokens // batch // cl
    qk_scale = hd**-0.5

    def body(x_ref, tg_ref, gate_ref, yp_ref, ys_ref, q_ref, kv_ref, w_ref, g_ref,
             dres_ref, dgate_ref, dyp_ref, dys_ref, dq_ref, dkv_ref, gw_hbm, gg_ref, loss_ref, acc, sem):
        i = pl.program_id(0)

        @pl.when(i == 0)
        def _():
            acc[...] = jnp.zeros_like(acc)
            gg_ref[...] = jnp.zeros_like(gg_ref)
            loss_ref[...] = jnp.zeros_like(loss_ref)

        @pl.when(i % per_seq == 0)
        def _():
            dkv_ref[...] = jnp.zeros_like(dkv_ref)

        k = kv_ref[:, 0:att_w].astype(BF16)
        v = kv_ref[:, att_w : 2 * att_w].astype(BF16)
        lane = lax.broadcasted_iota(jnp.int32, (1, att_w), 1)
        heads = [(lane >= h * hd) & (lane < (h + 1) * hd) for h in range(MEM_HEADS)]
        g = g_ref[...]

        def part(rows):
            n_rows = rows.stop - rows.start
            q = q_ref[rows, :]
            probs, q_heads = [], []
            att = jnp.zeros((n_rows, att_w), F32)
            for mask in heads:
                qh = jnp.where(mask, q, 0.0).astype(BF16)
                sc = _mm_nt(qh, k) * qk_scale
                e = jnp.exp(sc - jnp.max(sc, axis=-1, keepdims=True))
                p = e * (1.0 / jnp.sum(e, axis=-1, keepdims=True))
                att = att + jnp.where(mask, _mm(p.astype(BF16), v), 0.0)
                probs.append(p)
                q_heads.append(qh)

            ycat = jnp.concatenate([yp_ref[rows, :], ys_ref[rows, :], att], axis=1)
            gate = gate_ref[rows, :]
            sig = jax.nn.sigmoid(gate)
            silu = gate * sig
            yg = (ycat * silu).astype(BF16)
            out = _mm(yg, w_ref[...])
            r = _rms(out)
            on = out * r
            err = x_ref[rows, :] + on * g - tg_ref[rows, :]
            loss_ref[...] += 0.5 * jnp.sum(jnp.mean(err * err, axis=-1, keepdims=True), axis=0, keepdims=True)
            dres = err * (1.0 / d)
            dres_ref[rows, :] = dres
            gg_ref[...] += jnp.sum(dres * on, axis=0, keepdims=True)
            don = dres * g
            dout = (r * (don - on * jnp.mean(don * on, axis=-1, keepdims=True))).astype(BF16)
            acc[...] += _mm_tn(yg, dout)
            dyg = _mm_nt(dout, w_ref[...])
            dgate_ref[rows, :] = dyg * ycat * (sig * (1.0 + gate * (1.0 - sig)))
            dycat = dyg * silu
            dyp_ref[rows, :] = dycat[:, 0:pool_w]
            dys_ref[rows, :] = dycat[:, pool_w : pool_w + ssm_w]
            datt = dycat[:, pool_w + ssm_w : mix]

            dq = jnp.zeros((n_rows, att_w), F32)
            dk = jnp.zeros((n_mem, att_w), F32)
            dv = jnp.zeros((n_mem, att_w), F32)
            for mask, p, qh in zip(heads, probs, q_heads):
                doh = jnp.where(mask, datt, 0.0).astype(BF16)
                dp = _mm_nt(doh, v)
                ds = (p * (dp - jnp.sum(p * dp, axis=-1, keepdims=True)) * qk_scale).astype(BF16)
                dq = dq + jnp.where(mask, _mm(ds, k), 0.0)
                dk = dk + _mm_tn(ds, qh)
                dv = dv + _mm_tn(p.astype(BF16), doh)
            dq_ref[rows, :] = dq
            dkv_ref[:, 0:att_w] += dk
            dkv_ref[:, att_w : 2 * att_w] += dv

        part(slice(0, cl))

        @pl.when(i == n_tiles - 1)
        def _():
            cp = pltpu.make_async_copy(acc, gw_hbm, sem)
            cp.start()
            cp.wait()

    tok = lambda w: pl.BlockSpec((cl, w), lambda i: (i, 0))
    chunked = tok(ssm_w)
    per_batch = pl.BlockSpec((None, n_mem, kvw), lambda i: (i // per_seq, 0, 0))
    return pl.pallas_call(
        body,
        name="tail",
        grid=(n_tiles,),
        in_specs=[
            tok(d), tok(d), pl.BlockSpec((cl, mix), lambda i: (i, 1)), tok(pool_w), chunked,
            pl.BlockSpec((cl, att_w), lambda i: (i, (mix - att_w) // att_w)), per_batch,
            pl.BlockSpec((mix, d), lambda i: (0, 0)),
            pl.BlockSpec((1, d), lambda i: (0, 0)),
        ],
        out_specs=[
            tok(d), tok(mix), tok(pool_w), chunked, tok(att_w), per_batch,
            pl.BlockSpec(memory_space=pl.ANY),
            pl.BlockSpec((1, d), lambda i: (0, 0)),
            pl.BlockSpec((1, 1), lambda i: (0, 0)),
        ],
        out_shape=[
            jax.ShapeDtypeStruct((tokens, d), F32),
            jax.ShapeDtypeStruct((tokens, mix), F32),
            jax.ShapeDtypeStruct((tokens, pool_w), F32),
            jax.ShapeDtypeStruct((tokens, ssm_w), F32),
            jax.ShapeDtypeStruct((tokens, att_w), F32),
            jax.ShapeDtypeStruct(kv.shape, F32),
            jax.ShapeDtypeStruct((mix, d), F32),
            jax.ShapeDtypeStruct((1, d), F32),
            jax.ShapeDtypeStruct((1, 1), F32),
        ],
        scratch_shapes=[pltpu.VMEM((mix, d), F32), pltpu.SemaphoreType.DMA],
        compiler_params=_params("arbitrary"),
    )(x2, target2, proj, y_pool, y_ssm, proj, kv, w_out, g_post)


def _pack(arrays):
    flat = jnp.concatenate([a.reshape(-1) for a in arrays])
    rows = -(-flat.size // (SUBLANES * LANES)) * SUBLANES
    return jnp.pad(flat, (0, rows * LANES - flat.size)).reshape(rows, LANES)


def _unpack(packed, like):
    flat, out, at = packed.reshape(-1), [], 0
    for a in like:
        out.append(flat[at : at + a.size].reshape(a.shape))
        at += a.size
    return out


def kernel(x, mem, g_pre, w_in, w_pool, pool_scale, a_re, a_im, log_dt, b_re, b_im, c_re, c_im, d_skip, w_glu, g_mem, w_kv, w_out, g_post, loss_target, m_g_pre, m_w_in, m_w_pool, m_pool_scale, m_a_re, m_a_im, m_log_dt, m_b_re, m_b_im, m_c_re, m_c_im, m_d_skip, m_w_glu, m_g_mem, m_w_kv, m_w_out, m_g_post, v_g_pre, v_w_in, v_w_pool, v_pool_scale, v_a_re, v_a_im, v_log_dt, v_b_re, v_b_im, v_c_re, v_c_im, v_d_skip, v_w_glu, v_g_mem, v_w_kv, v_w_out, v_g_post):
    batch, seq, d = x.shape
    cl = seq // NCH
    pool_w, ssm_w = pool_scale.shape[1], d_skip.shape[1]
    att_w = w_kv.shape[2] // 2
    tokens = batch * seq
    x2 = x.reshape(tokens, d)
    target2 = loss_target.reshape(tokens, d)

    wp_blk = jax.scipy.linalg.block_diag(*w_pool[0]).astype(BF16)
    ssm_params = (a_re[0], a_im[0], log_dt[0], b_re[0], b_im[0], c_re[0], c_im[0])
    (lam_row, b_tab, c_tab), tables_vjp = jax.vjp(_ssm_tables, *ssm_params)
    nb = b_tab.shape[0]
    lam8 = jnp.broadcast_to(lam_row, (SUBLANES, lam_row.shape[1]))
    lcl = _lam_power(a_re[0], a_im[0], log_dt[0], float(cl), 1.0, nb)
    b_bf, c_bf = b_tab.astype(BF16), c_tab.astype(BF16)

    proj, w_in_g, (w_out_g, w_kv_g, w_glu_g) = _in_proj(
        x2, g_pre, w_in[0], [w_out[0].astype(BF16), w_kv[0].astype(BF16), w_glu[0].astype(BF16)])
    proj3 = proj.reshape(batch, seq, proj.shape[1])
    w_out_f = w_out_g.reshape(N_DEV * w_out_g.shape[1], w_out_g.shape[2])
    w_kv_f = w_kv_g.reshape(N_DEV * w_kv_g.shape[1], w_kv_g.shape[2])
    w_glu_f = w_glu_g.transpose(1, 0, 2).reshape(w_glu_g.shape[1], N_DEV * w_glu_g.shape[2])
    y_pool = _pool_fwd(proj, wp_blk, pool_scale, batch, seq)
    y_ssm, pre_ssm, z_ssm, init_ssm = _ssm_fwd(proj3, pool_w, b_bf, c_bf, lam8, lcl, d_skip, w_glu_f)
    kv = _kv_fwd(mem, g_mem, w_kv_f)

    dres, dgate, dy_pool, dy_ssm, dq, dkv, gw_out, gg_post, loss_part = _tail(
        x2, target2, proj, y_pool, y_ssm.reshape(tokens, ssm_w), kv, w_out_f, g_post)

    gw_kv, gg_mem = _kv_bwd(mem, dkv, g_mem, w_kv_f)
    du_pool, gwp_dense, g_scale = _pool_bwd(proj, dy_pool, wp_blk, pool_scale, batch, seq)
    gw_kv8 = gw_kv.reshape(N_DEV, -1, gw_kv.shape[1])
    gw_out8 = gw_out.reshape(N_DEV, -1, gw_out.shape[1])
    (du_ssm, gb_tab, gc_tab, gw_glu, gd_skip, glam), (kv_landed, out_landed) = _ssm_bwd(
        proj3, pool_w, pre_ssm, z_ssm, dy_ssm.reshape(batch, seq, ssm_w), init_ssm, b_bf, b_bf.transpose(0, 2, 1),
        c_bf.transpose(0, 2, 1), lam8, lcl, d_skip, w_glu_f, [gw_kv8, gw_out8])
    grad_x2, gw_in, gg_pre = _in_proj_bwd(
        x2, dres, du_pool, du_ssm.reshape(tokens, ssm_w), dq, dgate, g_pre,
        w_in_g.transpose(0, 2, 1).reshape(-1, d))

    gw = pool_w // len(POOL_WINDOWS)
    gw_pool = jnp.stack([gwp_dense[i * gw : (i + 1) * gw, i * gw : (i + 1) * gw] for i in range(len(POOL_WINDOWS))])
    g_ssm = tables_vjp((glam, gb_tab, gc_tab))

    small_w = [g_pre, w_pool, pool_scale, a_re, a_im, log_dt, b_re, b_im, c_re, c_im, d_skip, g_mem, g_post]
    small_m = [m_g_pre, m_w_pool, m_pool_scale, m_a_re, m_a_im, m_log_dt, m_b_re, m_b_im, m_c_re, m_c_im, m_d_skip, m_g_mem, m_g_post]
    small_v = [v_g_pre, v_w_pool, v_pool_scale, v_a_re, v_a_im, v_log_dt, v_b_re, v_b_im, v_c_re, v_c_im, v_d_skip, v_g_mem, v_g_post]
    small_g = [gg_pre, gw_pool, g_scale, *g_ssm, gd_skip, gg_mem, gg_post]
    big, small_sum = _reduce_all(
        [gw_in, gw_glu.reshape(ssm_w, N_DEV, -1).transpose(1, 0, 2)],
        [w_in[0], w_glu[0]], [m_w_in[0], m_w_glu[0]], [v_w_in[0], v_w_glu[0]],
        _pack(small_g + [loss_part]),
        [(gw_kv8, kv_landed, w_kv[0], m_w_kv[0], v_w_kv[0]), (gw_out8, out_landed, w_out[0], m_w_out[0], v_w_out[0])])
    big = {name: tuple(t[None] for t in res) for name, res in zip(["w_in", "w_glu", "w_kv", "w_out"], big)}

    flat2 = lambda a: a.reshape(-1, a.shape[-1])
    sg = _unpack(small_sum, [flat2(a) for a in small_w] + [loss_part])
    loss = sg[-1].reshape(())
    updates = _adamw_small(sg[:-1], [flat2(a) for a in small_w], [flat2(a) for a in small_m], [flat2(a) for a in small_v])
    sg = [g.reshape(a.shape) for g, a in zip(sg[:-1], small_w)]
    sd, sm, sv = ([u[kind].reshape(a.shape) for u, a in zip(updates, small_w)] for kind in range(3))

    order = ["g_pre", "w_in", "w_pool", "pool_scale", "a_re", "a_im", "log_dt", "b_re", "b_im", "c_re", "c_im",
             "d_skip", "w_glu", "g_mem", "w_kv", "w_out", "g_post"]
    small_names = ["g_pre", "w_pool", "pool_scale", "a_re", "a_im", "log_dt", "b_re", "b_im", "c_re", "c_im",
                   "d_skip", "g_mem", "g_post"]
    outs = [[], [], [], []]
    for name in order:
        if name in big:
            parts = big[name]
        else:
            j = small_names.index(name)
            parts = (sg[j], sd[j], sm[j], sv[j])
        for kind in range(4):
            outs[kind].append(parts[kind])
    return (loss, grad_x2.reshape(batch, seq, d), *outs[0], *outs[1], *outs[2], *outs[3])
```

```python
import functools
import math

import jax
import jax.numpy as jnp
from jax import lax
from jax.experimental import pallas as pl
from jax.experimental.pallas import tpu as pltpu

F32 = jnp.float32
BF16 = jnp.bfloat16
MESH = pl.DeviceIdType.MESH

N_DEV = 8
NCH = 8
SUBLANES = 8
LANES = 128
VMEM_LIMIT = 56 * 1024 * 1024

EPS = 1e-6
POOL_WINDOWS = (2, 4, 8, 16)
MEM_HEADS = 4
SSM_GROUP = 16
SSM_N = 64
ADAM_LR, ADAM_B1, ADAM_B2, ADAM_EPS, ADAM_WD, ADAM_STEP = 0.001, 0.9, 0.999, 1e-08, 0.01, 10


def _mm(a, b):
    return jnp.dot(a, b, preferred_element_type=F32)


def _mm_nt(a, b):
    return lax.dot_general(a, b, (((1,), (1,)), ((), ())), preferred_element_type=F32)


def _mm_tn(a, b):
    return lax.dot_general(a, b, (((0,), (0,)), ((), ())), preferred_element_type=F32)


def _params(*sem):
    return pltpu.CompilerParams(dimension_semantics=sem or None, vmem_limit_bytes=VMEM_LIMIT)


def _adamw(w, g, m, v):
    m = ADAM_B1 * m + (1.0 - ADAM_B1) * g
    v = ADAM_B2 * v + (1.0 - ADAM_B2) * (g * g)
    m_hat = m / (1.0 - ADAM_B1**ADAM_STEP)
    v_hat = v / (1.0 - ADAM_B2**ADAM_STEP)
    delta = -ADAM_LR * (m_hat / (jnp.sqrt(v_hat) + ADAM_EPS) + ADAM_WD * w)
    return delta, m, v


def _gelu(x):
    k = math.sqrt(2.0 / math.pi)
    return 0.5 * x * (1.0 + jnp.tanh(k * (x + 0.044715 * x * x * x)))


def _gelu_grad(x):
    k = math.sqrt(2.0 / math.pi)
    th = jnp.tanh(k * (x + 0.044715 * x * x * x))
    return 0.5 * (1.0 + th) + 0.5 * x * (1.0 - th * th) * (k * (1.0 + 3.0 * 0.044715 * x * x))


def _place():
    return lax.axis_index("x"), lax.axis_index("y"), lax.axis_index("c")


def _gather_steps(ins, outs, send_sems, recv_sems):
    n = len(ins)
    x, y, c = _place()
    me, sibling = (x, y, c), (x, y, 1 - c)
    chips = [(1 - x, y), (x, 1 - y), (1 - x, 1 - y)]
    sent = []

    def slot(px, py, pc):
        return 4 * px + 2 * py + pc

    def copy(a, k, block, to):
        ref = outs[a].at[slot(*block)]
        return pltpu.make_async_remote_copy(
            src_ref=ref, dst_ref=ref, send_sem=send_sems.at[a, k], recv_sem=recv_sems.at[a, k],
            device_id=to, device_id_type=MESH)

    def start():
        for a in range(n):
            outs[a][slot(*me)] = ins[a][...].astype(outs[a].dtype)
        for a in range(n):
            sent.append(copy(a, 0, me, sibling))
            sent.extend(copy(a, 1 + j, me, (*chip, c)) for j, chip in enumerate(chips))
        for cp in sent:
            cp.start()

    def forward():
        for j, chip in enumerate(chips):
            for a in range(n):
                copy(a, 1 + j, (*chip, c), me).wait_recv()
                cp = copy(a, 4 + j, (*chip, c), sibling)
                cp.start()
                sent.append(cp)

    def finish():
        for a in range(n):
            copy(a, 0, sibling, me).wait_recv()
            for j, chip in enumerate(chips):
                copy(a, 4 + j, (*chip, 1 - c), me).wait_recv()
        for cp in sent:
            cp.wait_send()

    return start, forward, finish


def _exchange_steps(n, src_of, landing, send_sems, recv_sems):
    x, y, c = _place()
    me = 4 * x + 2 * y + c
    peers = []
    for j in range(1, N_DEV):
        px = 1 - x if j & 4 else x
        py = 1 - y if j & 2 else y
        pc = 1 - c if j & 1 else c
        peers.append((px, py, pc))

    def copy(a, j, from_slot, to_slot, peer):
        return pltpu.make_async_remote_copy(
            src_ref=src_of(a, to_slot), dst_ref=landing[a].at[from_slot],
            send_sem=send_sems.at[a, j], recv_sem=recv_sems.at[a, j], device_id=peer, device_id_type=MESH)

    def start():
        for a in range(n):
            for j, p in enumerate(peers):
                copy(a, j, me, 4 * p[0] + 2 * p[1] + p[2], p).start()

    def finish():
        for a in range(n):
            for j, p in enumerate(peers):
                slot = 4 * p[0] + 2 * p[1] + p[2]
                copy(a, j, slot, slot, p).wait_recv()
        for a in range(n):
            for j, p in enumerate(peers):
                copy(a, j, me, 4 * p[0] + 2 * p[1] + p[2], p).wait_send()

    return start, finish


def _ordered_sum(gathered, out_ref):
    rows = out_ref.shape[0]

    def step(i, _):
        r = pl.ds(pl.multiple_of(i * SUBLANES, SUBLANES), SUBLANES)
        g = gathered[0, r, :]
        for d in range(1, N_DEV):
            g = g + gathered[d, r, :]
        out_ref[r, :] = g
        return 0

    lax.fori_loop(0, rows // SUBLANES, step, 0)


def _adamw_small(gs, ws, ms, vs):
    n = len(gs)

    def body(*refs):
        g, w, m, v = refs[:n], refs[n : 2 * n], refs[2 * n : 3 * n], refs[3 * n : 4 * n]
        outs = refs[4 * n :]
        for a in range(n):
            delta, nm, nv = _adamw(w[a][...], g[a][...], m[a][...], v[a][...])
            outs[3 * a][...] = delta
            outs[3 * a + 1][...] = nm
            outs[3 * a + 2][...] = nv

    vmem = pl.BlockSpec(memory_space=pltpu.VMEM)
    out_shape = []
    for wa in ws:
        out_shape += [jax.ShapeDtypeStruct(wa.shape, F32)] * 3
    res = pl.pallas_call(
        body,
        name="adamw_small",
        out_shape=out_shape,
        in_specs=[vmem] * (4 * n),
        out_specs=[vmem] * (3 * n),
        compiler_params=_params(),
    )(*gs, *ws, *ms, *vs)
    return [tuple(res[3 * a : 3 * a + 3]) for a in range(n)]


def _reduce_all(parts, ws, ms, vs, small, early):
    n, ne = len(parts), len(early)
    parts4 = [p.reshape(4, 2, *p.shape[1:]) for p in parts]
    blks = [p.shape[1:] for p in parts]

    def body(*refs):
        refs = list(refs)
        take = lambda k: [refs.pop(0) for _ in range(k)]
        part, w_in, m_in, v_in = take(n), take(n), take(n), take(n)
        (small_ref,) = take(1)
        early_in = [take(5) for _ in range(ne)]
        outs = take(4 * n)
        small_all, small_sum = take(2)
        early_out = [take(4) for _ in range(ne)]
        own, r1, r2 = take(n), take(n), take(n)
        early_buf = take(ne)
        s1_send, s1_recv, s2_send, s2_recv, loc, small_send, small_recv, early_sems = refs
        small_start, small_forward, small_finish = _gather_steps([small_ref], [small_all], small_send, small_recv)
        x, y, c = _place()
        me = 4 * x + 2 * y + c
        landed = [pltpu.make_async_copy(early_in[e][1], early_buf[e], early_sems.at[e, 0]) for e in range(ne)]
        for cp in landed:
            cp.start()
        sibling = (x, y, 1 - c)
        chips = [(1 - x, y), (x, 1 - y), (1 - x, 1 - y)]

        def rowwise(rows, fn):
            chunk = math.gcd(rows, 128)

            def step(i, _):
                fn(pl.ds(pl.multiple_of(i * chunk, chunk), chunk))
                return 0

            lax.fori_loop(0, rows // chunk, step, 0)

        stage1, local = [], []
        for a in range(n):
            cp = pltpu.make_async_remote_copy(
                src_ref=part[a].at[:, 1 - c], dst_ref=r1[a], send_sem=s1_send.at[a], recv_sem=s1_recv.at[a],
                device_id=sibling, device_id_type=MESH)
            cp.start()
            stage1.append(cp)
            lc = pltpu.make_async_copy(part[a].at[:, c], own[a], loc.at[a])
            lc.start()
            local.append(lc)
        small_start()
        stage2 = []
        for a in range(n):
            local[a].wait()
            stage1[a].wait_recv()
            for chip in range(4):

                def add(r, a=a, chip=chip):
                    own[a][chip, r, :] = own[a][chip, r, :] + r1[a][chip, r, :]

                rowwise(blks[a][0], add)
            for k, chip in enumerate(chips):
                cp = pltpu.make_async_remote_copy(
                    src_ref=own[a].at[2 * chip[0] + chip[1]], dst_ref=r2[a].at[k],
                    send_sem=s2_send.at[a, k], recv_sem=s2_recv.at[a, k],
                    device_id=(*chip, c), device_id_type=MESH)
                cp.start()
                stage2.append(cp)
        small_forward()
        for e in range(ne):
            part_e, _, w_e, m_e, v_e = early_in[e]
            g_ref, d_ref, nm_ref, nv_ref = early_out[e]
            landed[e].wait()
            mine = pltpu.make_async_copy(part_e.at[me], early_buf[e].at[me], early_sems.at[e, 1])
            mine.start()
            mine.wait()

            def update_early(r, e=e, w_e=w_e, m_e=m_e, v_e=v_e, g_ref=g_ref, d_ref=d_ref, nm_ref=nm_ref, nv_ref=nv_ref):
                g = early_buf[e][0, r, :]
                for dev in range(1, N_DEV):
                    g = g + early_buf[e][dev, r, :]
                delta, nm, nv = _adamw(w_e[r, :], g, m_e[r, :], v_e[r, :])
                g_ref[r, :] = g
                d_ref[r, :] = delta
                nm_ref[r, :] = nm
                nv_ref[r, :] = nv

            rowwise(early_buf[e].shape[1], update_early)
        for a in range(n):
            for k, chip in enumerate(chips):
                stage2[3 * a + k].wait_recv()
            g_ref, d_ref, nm_ref, nv_ref = outs[4 * a : 4 * a + 4]

            def update(r, a=a, g_ref=g_ref, d_ref=d_ref, nm_ref=nm_ref, nv_ref=nv_ref):
                g = own[a][2 * x + y, r, :] + r2[a][0, r, :] + r2[a][1, r, :] + r2[a][2, r, :]
                delta, nm, nv = _adamw(w_in[a][r, :], g, m_in[a][r, :], v_in[a][r, :])
                g_ref[r, :] = g
                d_ref[r, :] = delta
                nm_ref[r, :] = nm
                nv_ref[r, :] = nv

            rowwise(blks[a][0], update)
        small_finish()
        _ordered_sum(small_all, small_sum)
        for cp in stage1 + stage2:
            cp.wait_send()

    vmem = pl.BlockSpec(memory_space=pltpu.VMEM)
    hbm = pl.BlockSpec(memory_space=pl.ANY)
    out_shape = []
    for b in blks:
        out_shape += [jax.ShapeDtypeStruct(b, F32)] * 4
    out_shape += [jax.ShapeDtypeStruct((N_DEV, *small.shape), F32), jax.ShapeDtypeStruct(small.shape, F32)]
    for e in early:
        out_shape += [jax.ShapeDtypeStruct(e[2].shape, F32)] * 4
    scratch = (
        [pltpu.VMEM((4, *b), F32) for b in blks]
        + [pltpu.VMEM((4, *b), F32) for b in blks]
        + [pltpu.VMEM((3, *b), F32) for b in blks]
        + [pltpu.VMEM(e[0].shape, F32) for e in early]
        + [pltpu.SemaphoreType.DMA((n,)), pltpu.SemaphoreType.DMA((n,)), pltpu.SemaphoreType.DMA((n, 3)),
           pltpu.SemaphoreType.DMA((n, 3)), pltpu.SemaphoreType.DMA((n,)),
           pltpu.SemaphoreType.DMA((1, 7)), pltpu.SemaphoreType.DMA((1, 7)), pltpu.SemaphoreType.DMA((ne, 2))]
    )
    res = pl.pallas_call(
        body,
        name="reduce_all",
        out_shape=out_shape,
        in_specs=[hbm] * n + [vmem] * (3 * n + 1) + [hbm, hbm, vmem, vmem, vmem] * ne,
        out_specs=[vmem] * (4 * n + 2 + 4 * ne),
        scratch_shapes=scratch,
        compiler_params=_params(),
    )(*parts4, *ws, *ms, *vs, small, *[t for e in early for t in e])
    late = [tuple(res[4 * a : 4 * a + 4]) for a in range(n)]
    at = 4 * n + 2
    return late + [tuple(res[at + 4 * e : at + 4 * e + 4]) for e in range(ne)], res[4 * n + 1]


def _rms(x):
    return lax.rsqrt(jnp.mean(x * x, axis=-1, keepdims=True) + EPS)


def _token_tile(tokens, want):
    tile = min(want, tokens // 2)
    assert tokens % tile == 0 and tile % 16 == 0
    return tile


def _in_proj(x2, g_pre, w_in_blk, shards):
    tokens, d = x2.shape
    nb = w_in_blk.shape[1]
    tm = _token_tile(tokens, 2048)
    n_t = tokens // tm
    ns = len(shards)
    x_pos, y_pos, c_pos = _place()
    slot = lambda px, py, pc: 4 * px + 2 * py + pc
    chip_order = [(x_pos, y_pos), (1 - x_pos, y_pos), (x_pos, 1 - y_pos), (1 - x_pos, 1 - y_pos)]
    order = jnp.stack([slot(px, py, pc) for px, py in chip_order for pc in (c_pos, 1 - c_pos)]).astype(jnp.int32)

    def body(order_ref, x_ref, g_ref, w_ref, *rest):
        shard_hbm, proj_ref, w_hbm = rest[:ns], rest[ns], rest[ns + 1]
        gathered = rest[ns + 2 : 2 * ns + 2]
        h_all, land, w_send, w_recv, out_sem, send_sems, recv_sems, own_sems = rest[2 * ns + 2 :]
        j, i = pl.program_id(0), pl.program_id(1)
        x, y, c = _place()
        me, sibling = (x, y, c), (x, y, 1 - c)
        chips = [(1 - x, y), (x, 1 - y), (1 - x, 1 - y)]
        own = [pltpu.make_async_copy(shard_hbm[a], gathered[a].at[slot(*me)], own_sems.at[a]) for a in range(ns)]
        start, finish = _exchange_steps(ns, lambda a, s: shard_hbm[a], gathered, send_sems, recv_sems)

        def copy(k, block, to):
            ref = land.at[slot(*block)]
            return pltpu.make_async_remote_copy(
                src_ref=ref, dst_ref=ref, send_sem=w_send.at[k], recv_sem=w_recv.at[k], device_id=to, device_id_type=MESH)

        first_sends = [copy(0, me, sibling)] + [copy(1 + k, me, (*chip, c)) for k, chip in enumerate(chips)]
        forwards = [copy(4 + k, (*chip, c), sibling) for k, chip in enumerate(chips)]

        @pl.when((j == 0) & (i == 0))
        def _():
            land[slot(*me)] = w_ref[...].astype(BF16)
            for cp in first_sends:
                cp.start()
            start()
            for cp in own:
                cp.start()

        @pl.when((j == 1) & (i == 0))
        def _():
            copy(0, sibling, me).wait_recv()

        for k, chip in enumerate(chips):

            @pl.when((j == 2 + 2 * k) & (i == 0))
            def _(k=k, chip=chip):
                copy(1 + k, (*chip, c), me).wait_recv()
                forwards[k].start()

            @pl.when((j == 3 + 2 * k) & (i == 0))
            def _(k=k, chip=chip):
                copy(4 + k, (*chip, 1 - c), me).wait_recv()

        rows = pl.ds(pl.multiple_of(i * tm, tm), tm)

        @pl.when(j == 0)
        def _():
            x_t = x_ref[...]
            h_all[rows, :] = (x_t * _rms(x_t) * g_ref[...]).astype(BF16)

        proj_ref[...] = _mm(h_all[rows, :], land[order_ref[j]])

        @pl.when((j == N_DEV - 1) & (i == n_t - 1))
        def _():
            for cp in first_sends + forwards:
                cp.wait_send()
            finish()
            for cp in own:
                cp.wait()
            out = pltpu.make_async_copy(land, w_hbm, out_sem)
            out.start()
            out.wait()

    hbm = pl.BlockSpec(memory_space=pl.ANY)
    res = pl.pallas_call(
        body,
        name="in_proj",
        grid_spec=pltpu.PrefetchScalarGridSpec(
            num_scalar_prefetch=1,
            grid=(N_DEV, n_t),
            in_specs=[
                pl.BlockSpec((tm, d), lambda j, i, order: (jnp.where(j == 0, i, n_t - 1), 0)),
                pl.BlockSpec((1, d), lambda j, i, order: (0, 0)),
                pl.BlockSpec((d, nb), lambda j, i, order: (0, 0)),
            ]
            + [hbm] * ns,
            out_specs=[pl.BlockSpec((tm, nb), lambda j, i, order: (i, order[j])), hbm] + [hbm] * ns,
            scratch_shapes=[
                pltpu.VMEM((tokens, d), BF16),
                pltpu.VMEM((N_DEV, d, nb), BF16),
                pltpu.SemaphoreType.DMA((7,)),
                pltpu.SemaphoreType.DMA((7,)),
                pltpu.SemaphoreType.DMA,
                pltpu.SemaphoreType.DMA((ns, 7)),
                pltpu.SemaphoreType.DMA((ns, 7)),
                pltpu.SemaphoreType.DMA((ns,)),
            ],
        ),
        out_shape=[
            jax.ShapeDtypeStruct((tokens, N_DEV * nb), F32),
            jax.ShapeDtypeStruct((N_DEV, d, nb), BF16),
        ]
        + [jax.ShapeDtypeStruct((N_DEV, *a.shape), a.dtype) for a in shards],
        compiler_params=_params("arbitrary", "arbitrary"),
    )(order, x2, g_pre, w_in_blk, *shards)
    return res[0], res[1], res[2:]


def _in_proj_bwd(x2, dres, du_pool, du_ssm, dq, dgate, g_pre, w_in_t):
    tokens, d = x2.shape
    nb = w_in_t.shape[0] // N_DEV
    pool_w, ssm_w, att_w, mix = du_pool.shape[1], du_ssm.shape[1], dq.shape[1], dgate.shape[1]
    cl = _token_tile(tokens, 512)
    n_tiles = tokens // cl

    def body(x_ref, dres_ref, dup_ref, dus_ref, dq_ref, dgate_ref, g_ref, w_ref, gx_ref, gw_hbm, gg_ref, acc, sem):
        i = pl.program_id(0)

        @pl.when(i == 0)
        def _():
            acc[...] = jnp.zeros_like(acc)
            gg_ref[...] = jnp.zeros_like(gg_ref)

        x = x_ref[...]
        r = _rms(x)
        xn = x * r
        g = g_ref[...]
        h = (xn * g).astype(BF16)
        dproj = jnp.concatenate([dup_ref[...], dus_ref[...], dq_ref[...], dgate_ref[...]], axis=1).astype(BF16)
        dh = _mm(dproj, w_ref[...])
        for j in range(N_DEV):
            acc[j] += _mm_tn(h, dproj[:, j * nb : (j + 1) * nb])
        gg_ref[...] += jnp.sum(dh * xn, axis=0, keepdims=True)
        dxn = dh * g
        gx_ref[...] = dres_ref[...] + r * (dxn - xn * jnp.mean(dxn * xn, axis=-1, keepdims=True))

        @pl.when(i == n_tiles - 1)
        def _():
            cp = pltpu.make_async_copy(acc, gw_hbm, sem)
            cp.start()
            cp.wait()

    return pl.pallas_call(
        body,
        name="in_proj_bwd",
        grid=(n_tiles,),
        in_specs=[
            pl.BlockSpec((cl, d), lambda i: (i, 0)),
            pl.BlockSpec((cl, d), lambda i: (i, 0)),
            pl.BlockSpec((cl, pool_w), lambda i: (i, 0)),
            pl.BlockSpec((cl, ssm_w), lambda i: (i, 0)),
            pl.BlockSpec((cl, att_w), lambda i: (i, 0)),
            pl.BlockSpec((cl, mix), lambda i: (i, 0)),
            pl.BlockSpec((1, d), lambda i: (0, 0)),
            pl.BlockSpec(w_in_t.shape, lambda i: (0, 0)),
        ],
        out_specs=[
            pl.BlockSpec((cl, d), lambda i: (i, 0)),
            pl.BlockSpec(memory_space=pl.ANY),
            pl.BlockSpec((1, d), lambda i: (0, 0)),
        ],
        out_shape=[
            jax.ShapeDtypeStruct((tokens, d), F32),
            jax.ShapeDtypeStruct((N_DEV, d, nb), F32),
            jax.ShapeDtypeStruct((1, d), F32),
        ],
        scratch_shapes=[pltpu.VMEM((N_DEV, d, nb), F32), pltpu.SemaphoreType.DMA],
        compiler_params=_params("arbitrary"),
    )(x2, dres, du_pool, du_ssm, dq, dgate, g_pre, w_in_t)


def _pool_geometry(seq, width):
    gw = width // len(POOL_WINDOWS)
    col = lax.broadcasted_iota(jnp.int32, (1, width), 1)
    win = jnp.full((1, width), float(POOL_WINDOWS[-1]), F32)
    for gi in range(len(POOL_WINDOWS) - 2, -1, -1):
        win = jnp.where(col < (gi + 1) * gw, float(POOL_WINDOWS[gi]), win)
    row = lax.broadcasted_iota(jnp.int32, (seq, width), 0)
    filling = 1.0 / (lax.broadcasted_iota(jnp.int32, (seq, 1), 0) + 1).astype(F32)
    inv_cnt = jnp.where(row + 1 < win.astype(jnp.int32), filling, 1.0 / win)
    return win, row, inv_cnt


def _window_sums(a, win, seq, back):
    pad = 2 * POOL_WINDOWS[-1]
    zeros = jnp.zeros((pad, a.shape[1]), F32)
    s = jnp.concatenate([a, zeros] if back else [zeros, a], axis=0)
    sums = []
    k = 1
    while k < POOL_WINDOWS[-1]:
        s = s + pltpu.roll(s, seq + pad - k if back else k, 0)
        k *= 2
        sums.append((k, s))
    out = sums[-1][1]
    for k, s in reversed(sums[:-1]):
        out = jnp.where(win <= float(k), s, out)
    return out[0:seq] if back else out[pad : pad + seq]


def _pool_fwd(u2, wp_blk, scale, batch, seq):
    width = scale.shape[1]

    def body(u_ref, w_ref, s_ref, y_ref):
        u = u_ref[...]
        win, row, inv_cnt = _pool_geometry(seq, width)
        diff = _window_sums(u, win, seq, False) * inv_cnt - u
        y_ref[...] =_mm(diff.astype(BF16), w_ref[...]) * s_ref[...]

    return pl.pallas_call(
        body,
        name="pool_fwd",
        grid=(batch,),
        in_specs=[
            pl.BlockSpec((seq, width), lambda b: (b, 0)),
            pl.BlockSpec((width, width), lambda b: (0, 0)),
            pl.BlockSpec((1, width), lambda b: (0, 0)),
        ],
        out_specs=pl.BlockSpec((seq, width), lambda b: (b, 0)),
        out_shape=jax.ShapeDtypeStruct((u2.shape[0], width), F32),
        compiler_params=_params("arbitrary"),
    )(u2, wp_blk, scale)


def _pool_bwd(u2, dy2, wp_blk, scale, batch, seq):
    width = scale.shape[1]

    def body(u_ref, dy_ref, w_ref, s_ref, du_ref, gw_ref, gs_ref):
        @pl.when(pl.program_id(0) == 0)
        def _():
            gw_ref[...] = jnp.zeros_like(gw_ref)
            gs_ref[...] = jnp.zeros_like(gs_ref)

        u = u_ref[...]
        dy = dy_ref[...]
        win, row, inv_cnt = _pool_geometry(seq, width)
        diff = (_window_sums(u, win, seq, False) * inv_cnt - u).astype(BF16)
        gs_ref[...] += jnp.sum(dy * _mm(diff, w_ref[...]), axis=0, keepdims=True)
        dys = (dy * s_ref[...]).astype(BF16)
        gw_ref[...] += _mm_tn(diff, dys)
        dd = _mm_nt(dys, w_ref[...])
        du_ref[...] = _window_sums(dd * inv_cnt, win, seq, True) - dd

    return pl.pallas_call(
        body,
        name="pool_bwd",
        grid=(batch,),
        in_specs=[
            pl.BlockSpec((seq, width), lambda b: (b, 0)),
            pl.BlockSpec((seq, width), lambda b: (b, 0)),
            pl.BlockSpec((width, width), lambda b: (0, 0)),
            pl.BlockSpec((1, width), lambda b: (0, 0)),
        ],
        out_specs=[
            pl.BlockSpec((seq, width), lambda b: (b, 0)),
            pl.BlockSpec((width, width), lambda b: (0, 0)),
            pl.BlockSpec((1, width), lambda b: (0, 0)),
        ],
        out_shape=[
            jax.ShapeDtypeStruct((u2.shape[0], width), F32),
            jax.ShapeDtypeStruct((width, width), F32),
            jax.ShapeDtypeStruct((1, width), F32),
        ],
        compiler_params=_params("arbitrary"),
    )(u2, dy2, wp_blk, scale)


def _state_row(z, n_blocks):
    re = jnp.real(z).reshape(n_blocks, -1)
    im = jnp.imag(z).reshape(n_blocks, -1)
    return jnp.concatenate([re, im], axis=1).reshape(1, -1)


def _ssm_tables(a_re, a_im, log_dt, b_re, b_im, c_re, c_im):
    groups, n_state = a_re.shape
    ch = b_re.shape[2]
    nb = groups * ch // LANES
    gl = groups // nb
    lam = lax.complex(a_re, a_im)
    lam_bar = jnp.exp(lam * jnp.exp(log_dt)[:, None])
    b_bar = ((lam_bar - 1.0) / lam)[..., None] * lax.complex(b_re, b_im)
    eye = jnp.eye(gl, dtype=F32)

    def rows_to_state(t):
        return jnp.einsum("sgnc,gh->sgchn", t.reshape(nb, gl, n_state, ch), eye).reshape(nb, gl * ch, gl * n_state)

    def state_to_rows(t):
        return jnp.einsum("sgcn,gh->shngc", t.reshape(nb, gl, ch, n_state), eye).reshape(nb, gl * n_state, gl * ch)

    b_tab = jnp.concatenate([rows_to_state(jnp.real(b_bar)), rows_to_state(jnp.imag(b_bar))], axis=2)
    c_tab = jnp.concatenate([state_to_rows(c_re), -state_to_rows(c_im)], axis=1)
    return _state_row(lam_bar, nb), b_tab, c_tab


def _lam_power(a_re, a_im, log_dt, power, scale, n_blocks):
    return _state_row(scale * jnp.exp(lax.complex(a_re, a_im) * jnp.exp(log_dt)[:, None] * power), n_blocks)


def _state_blocks(s2, n_blocks, width):
    half = s2 // n_blocks // 2
    assert half % width == 0
    return [(b * 2 * half + o, b * 2 * half + half + o) for b in range(n_blocks) for o in range(0, half, width)]


def _scan(src_ref, dst_ref, st_ref, lam8_ref, n_groups, s, n_blocks, reverse, store):
    lb = 512
    for re0, im0 in _state_blocks(2 * s, n_blocks, lb):
        cr, ci = pl.ds(re0, lb), pl.ds(im0, lb)
        lr = lam8_ref[:, cr]
        li = -lam8_ref[:, ci] if reverse else lam8_ref[:, ci]

        def step(i, carry, cr=cr, ci=ci, lr=lr, li=li):
            hr, hi = carry
            grp = n_groups - 1 - i if reverse else i
            rows = pl.ds(pl.multiple_of(grp * SUBLANES, SUBLANES), SUBLANES)
            nr = lr * hr - li * hi + src_ref[rows, cr]
            ni = lr * hi + li * hr + src_ref[rows, ci]
            if store:
                dst_ref[rows, cr] = nr
                dst_ref[rows, ci] = ni
            return nr, ni

        hr, hi = lax.fori_loop(0, n_groups, step, (st_ref[:, cr], st_ref[:, ci]), unroll=2)
        st_ref[:, cr] = hr
        st_ref[:, ci] = hi


def _pack_state(re, im):
    hi = lax.bitcast_convert_type(re.astype(BF16).astype(F32), jnp.uint32)
    lo = lax.bitcast_convert_type(im.astype(BF16).astype(F32), jnp.uint32)
    return hi | (lo >> 16)


def _unpack_state(word):
    re = lax.bitcast_convert_type(word & jnp.uint32(0xFFFF0000), F32)
    im = lax.bitcast_convert_type(word << 16, F32)
    return re, im


def _scan_adjoint(dh_ref, hprev_ref, group0, stg_ref, acc_ref, lam8_ref, n_groups, s, n_blocks):
    lb = 512
    half = s // n_blocks
    for re0, im0 in _state_blocks(2 * s, n_blocks, lb):
        cr, ci = pl.ds(re0, lb), pl.ds(im0, lb)
        ch = pl.ds(re0 // (2 * half) * half + re0 % (2 * half), lb)
        lr, li = lam8_ref[:, cr], -lam8_ref[:, ci]

        def step(i, carry, cr=cr, ci=ci, ch=ch, lr=lr, li=li):
            gr, gi, ar, ai = carry
            grp = n_groups - 1 - i
            rows = pl.ds(pl.multiple_of(grp * SUBLANES, SUBLANES), SUBLANES)
            ngr = lr * gr - li * gi + dh_ref[rows, cr]
            ngi = lr * gi + li * gr + dh_ref[rows, ci]
            hr, hi = _unpack_state(hprev_ref[pl.ds(pl.multiple_of((group0 + grp) * SUBLANES, SUBLANES), SUBLANES), ch])
            ar = ar + hr * ngr + hi * ngi
            ai = ai + hr * ngi - hi * ngr
            dh_ref[rows, cr] = ngr
            dh_ref[rows, ci] = ngi
            return ngr, ngi, ar, ai

        init = (stg_ref[:, cr], stg_ref[:, ci], acc_ref[:, cr], acc_ref[:, ci])
        gr, gi, ar, ai = lax.fori_loop(0, n_groups, step, init)
        stg_ref[:, cr] = gr
        stg_ref[:, ci] = gi
        acc_ref[:, cr] = ar
        acc_ref[:, ci] = ai


def _chunk_starts(st_ref, init_ref, lcl_ref, s, n_blocks):
    w = s // n_blocks
    init_ref[0:1, :] = jnp.zeros((1, 2 * s), F32)
    for re0, im0 in _state_blocks(2 * s, n_blocks, w):
        re, im = pl.ds(re0, w), pl.ds(im0, w)
        ar, ai = lcl_ref[:, re], lcl_ref[:, im]
        cr = jnp.zeros((1, w), F32)
        ci = jnp.zeros((1, w), F32)
        for k in range(1, NCH):
            cr, ci = (ar * cr - ai * ci + st_ref[k - 1 : k, re], ar * ci + ai * cr + st_ref[k - 1 : k, im])
            init_ref[k : k + 1, re] = cr
            init_ref[k : k + 1, im] = ci


def _chunk_starts_adjoint(stg_ref, initg_ref, lcl_ref, s, n_blocks):
    w = s // n_blocks
    initg_ref[NCH - 1 : NCH, :] = jnp.zeros((1, 2 * s), F32)
    for re0, im0 in _state_blocks(2 * s, n_blocks, w):
        re, im = pl.ds(re0, w), pl.ds(im0, w)
        ar, ai = lcl_ref[:, re], -lcl_ref[:, im]
        gr = jnp.zeros((1, w), F32)
        gi = jnp.zeros((1, w), F32)
        for k in range(NCH - 2, -1, -1):
            gr, gi = (stg_ref[k + 1 : k + 2, re] + ar * gr - ai * gi, stg_ref[k + 1 : k + 2, im] + ar * gi + ai * gr)
            initg_ref[k : k + 1, re] = gr
            initg_ref[k : k + 1, im] = gi


def _ssm_rows(seq, want):
    rows = min(want, seq // 2)
    assert seq % rows == 0 and rows % SUBLANES == 0
    return rows


def _chunk_copies(hbm_ref, b, cm_ref, sems, to_cm, col0=0):
    cl, _, width = cm_ref.shape
    copies = []
    for k in range(NCH):
        nat, cm = hbm_ref.at[b, pl.ds(k * cl, cl), pl.ds(col0, width)], cm_ref.at[:, k, :]
        src, dst = (nat, cm) if to_cm else (cm, nat)
        copies.append(pltpu.make_async_copy(src, dst, sems.at[k]))
    return copies


def _blockwise(fn, n_blocks):
    return jnp.concatenate([fn(b) for b in range(n_blocks)], axis=1)


def _ssm_fwd(u, u_col, b_tab, c_tab, lam8, lcl, d_skip, w_glu, shards):
    batch, seq, _ = u.shape
    ns = len(shards)
    width = d_skip.shape[1]
    s = lam8.shape[1] // 2
    nb = b_tab.shape[0]
    sb = 2 * s // nb
    cl = seq // NCH
    rows = _ssm_rows(seq, 1024)
    n_tiles = seq // rows
    n_groups = rows // SUBLANES

    def body(u_hbm, b_ref, c_ref, lam_ref, lcl_ref, d_ref, wg_ref, *rest):
        shard_hbm, (y_hbm, pre_ref, z_ref, init_ref) = rest[:ns], rest[ns : ns + 4]
        gathered = rest[ns + 4 : 2 * ns + 4]
        u_cm, y_cm, bu_all, st, sems, send_sems, recv_sems, own_sems = rest[2 * ns + 4 :]
        b, ph, t = pl.program_id(0), pl.program_id(1), pl.program_id(2)
        tile_groups = pl.ds(pl.multiple_of(t * n_groups, n_groups), n_groups)
        x_pos, y_pos, c_pos = _place()
        own = [pltpu.make_async_copy(shard_hbm[a], gathered[a].at[4 * x_pos + 2 * y_pos + c_pos], own_sems.at[a])
               for a in range(ns)]
        exchange_start, exchange_finish = _exchange_steps(
            ns, lambda a, slot: shard_hbm[a], gathered, send_sems, recv_sems)

        @pl.when((b == 0) & (ph == 0) & (t == 0))
        def _():
            exchange_start()
            for cp in own:
                cp.start()

        @pl.when((b == batch - 1) & (ph == 1) & (t == n_tiles - 1))
        def _():
            exchange_finish()
            for cp in own:
                cp.wait()

        @pl.when((ph == 0) & (t == 0))
        def _():
            loads = _chunk_copies(u_hbm, b, u_cm, sems, True, u_col)
            for cp in loads:
                cp.start()
            st[...] = jnp.zeros_like(st)
            for cp in loads:
                cp.wait()

        @pl.when((ph == 1) & (t == 0))
        def _():
            st[...] = init_ref[...]

        u_t = u_cm[tile_groups].reshape(rows, width)
        bu = bu_all.at[pl.ds(pl.multiple_of(t * rows, rows), rows)]

        @pl.when(ph == 0)
        def _():
            u_b = u_t.astype(BF16)
            for blk in range(nb):
                bu[:, blk * sb : (blk + 1) * sb] = _mm(u_b[:, blk * LANES : (blk + 1) * LANES], b_ref[blk])
            _scan(bu, bu, st, lam_ref, n_groups, s, nb, False, False)

        @pl.when((ph == 0) & (t == n_tiles - 1))
        def _():
            _chunk_starts(st, init_ref, lcl_ref, s, nb)

        @pl.when(ph == 1)
        def _():
            _scan(bu, bu, st, lam_ref, n_groups, s, nb, False, True)
            hs = lambda blk: _mm(bu[:, blk * sb : (blk + 1) * sb].astype(BF16), c_ref[blk])
            pre = _blockwise(hs, nb) + d_ref[...] * u_t
            z = _mm(_gelu(pre).astype(BF16), wg_ref[...])
            pre_ref[...] = pre
            z_ref[...] = z
            y = z[:, 0:width] * jax.nn.sigmoid(z[:, width : 2 * width])
            y_cm[tile_groups] = y.reshape(n_groups, SUBLANES, width)

        @pl.when((ph == 1) & (t == n_tiles - 1))
        def _():
            stores = _chunk_copies(y_hbm, b, y_cm, sems, False)
            for cp in stores:
                cp.start()
            for cp in stores:
                cp.wait()

    out_tile = lambda b, ph, t: (b, t * ph, 0)
    full = lambda a: pl.BlockSpec(a.shape, lambda b, ph, t: (0,) * a.ndim)
    hbm = pl.BlockSpec(memory_space=pl.ANY)
    res = pl.pallas_call(
        body,
        name="ssm_fwd",
        grid=(batch, 2, n_tiles),
        in_specs=[hbm, full(b_tab), full(c_tab), full(lam8), full(lcl), full(d_skip), full(w_glu)] + [hbm] * ns,
        out_specs=[
            hbm,
            pl.BlockSpec((None, rows, width), out_tile),
            pl.BlockSpec((None, rows, 2 * width), out_tile),
            pl.BlockSpec((None, SUBLANES, 2 * s), lambda b, ph, t: (b, 0, 0)),
        ]
        + [hbm] * ns,
        out_shape=[
            jax.ShapeDtypeStruct((batch, seq, width), F32),
            jax.ShapeDtypeStruct((batch, seq, width), F32),
            jax.ShapeDtypeStruct((batch, seq, 2 * width), F32),
            jax.ShapeDtypeStruct((batch, SUBLANES, 2 * s), F32),
        ]
        + [jax.ShapeDtypeStruct((N_DEV, *a.shape), a.dtype) for a in shards],
        scratch_shapes=[
            pltpu.VMEM((cl, NCH, width), F32),
            pltpu.VMEM((cl, NCH, width), F32),
            pltpu.VMEM((seq, 2 * s), F32),
            pltpu.VMEM((SUBLANES, 2 * s), F32),
            pltpu.SemaphoreType.DMA((NCH,)),
            pltpu.SemaphoreType.DMA((ns, 7)),
            pltpu.SemaphoreType.DMA((ns, 7)),
            pltpu.SemaphoreType.DMA((ns,)),
        ],
        compiler_params=_params("arbitrary", "arbitrary", "arbitrary"),
    )(u, b_tab, c_tab, lam8, lcl, d_skip, w_glu, *shards)
    return res[:4], res[4:]


def _ssm_bwd(u, u_col, pre_p, z_p, dy, init, b_tab, b_tab_t, c_tab_t, lam8, lcl, d_skip, w_glu, ready):
    batch, seq, _ = u.shape
    width = d_skip.shape[1]
    nr = len(ready)
    s = lam8.shape[1] // 2
    nb = b_tab.shape[0]
    sb = 2 * s // nb
    cl = seq // NCH
    rows = _ssm_rows(seq, 512)
    n_tiles = seq // rows
    n_groups = rows // SUBLANES

    def body(u_hbm, pre_ref, z_ref, dy_hbm, init_ref, b_ref, bt_ref, ct_ref, lam_ref, lcl_ref, d_ref, wg_ref, *rest):
        ready_hbm, rest = rest[:nr], rest[nr:]
        du_hbm, gb_ref, gc_ref, gwg_ref, gd_ref, glam_ref = rest[:6]
        landed_hbm, rest = rest[6 : 6 + nr], rest[6 + nr :]
        u_cm, dy_cm, work, hs_all, dpre_all, st, stg, initg, acc, sems, send_sems, recv_sems = rest
        b, ph, t = pl.program_id(0), pl.program_id(1), pl.program_id(2)
        half = s // nb
        exchange_start, exchange_finish = _exchange_steps(
            nr, lambda a, slot: ready_hbm[a].at[slot], landed_hbm, send_sems, recv_sems)
        first = (b == 0) & (ph == 0) & (t == 0)
        last = (b == batch - 1) & (ph == 2) & (t == n_tiles - 1)
        tile = jnp.where(ph == 0, t, n_tiles - 1 - t)
        tile_rows = pl.ds(pl.multiple_of(tile * rows, rows), rows)
        tile_groups = pl.ds(pl.multiple_of(tile * n_groups, n_groups), n_groups)
        lanes = lambda blk: slice(blk * LANES, (blk + 1) * LANES)
        states = lambda blk: slice(blk * sb, (blk + 1) * sb)

        @pl.when(first)
        def _():
            exchange_start()
            acc[...] = jnp.zeros_like(acc)
            gb_ref[...] = jnp.zeros_like(gb_ref)
            gc_ref[...] = jnp.zeros_like(gc_ref)
            gwg_ref[...] = jnp.zeros_like(gwg_ref)
            gd_ref[...] = jnp.zeros_like(gd_ref)

        @pl.when((ph == 0) & (t == 0))
        def _():
            loads = (_chunk_copies(u_hbm, b, u_cm, sems.at[0], True, u_col)
                     + _chunk_copies(dy_hbm, b, dy_cm, sems.at[1], True))
            for cp in loads:
                cp.start()
            st[...] = init_ref[...]
            for blk in range(nb):
                entry = init_ref[:, states(blk)]
                hs_all[0:SUBLANES, blk * half : (blk + 1) * half] = _pack_state(entry[:, 0:half], entry[:, half : 2 * half])
            for cp in loads:
                cp.wait()

        u_t = u_cm[tile_groups].reshape(rows, width)
        u_b = u_t.astype(BF16)

        @pl.when(ph == 0)
        def _():
            for blk in range(nb):
                work[:, states(blk)] = _mm(u_b[:, lanes(blk)], b_ref[blk])
            _scan(work, work, st, lam_ref, n_groups, s, nb, False, True)
            z = z_ref[...]
            dy_t = dy_cm[tile_groups].reshape(rows, width)
            pre = pre_ref[...]
            z1, sig = z[:, 0:width], jax.nn.sigmoid(z[:, width : 2 * width])
            dz = jnp.concatenate([dy_t * sig, dy_t * z1 * sig * (1.0 - sig)], axis=1).astype(BF16)
            gwg_ref[...] += _mm_tn(_gelu(pre).astype(BF16), dz)
            dpre = _mm_nt(dz, wg_ref[...]) * _gelu_grad(pre)
            dpre_all[tile_rows, :] = dpre
            gd_ref[...] += jnp.sum(dpre * u_t, axis=0, keepdims=True)
            dpre_b = dpre.astype(BF16)
            kept = pl.ds(pl.multiple_of(tile * rows + SUBLANES, SUBLANES), rows)
            for blk in range(nb):
                hs = work[:, states(blk)]
                gc_ref[blk] += _mm_tn(hs.astype(BF16), dpre_b[:, lanes(blk)])
                hs_all[kept, blk * half : (blk + 1) * half] = _pack_state(hs[:, 0:half], hs[:, half : 2 * half])

        @pl.when(ph >= 1)
        def _():
            dpre_b = dpre_all[tile_rows, :].astype(BF16)
            for blk in range(nb):
                work[:, states(blk)] = _mm(dpre_b[:, lanes(blk)], ct_ref[blk])

        @pl.when(ph == 1)
        def _():
            @pl.when(t == 0)
            def _():
                stg[...] = jnp.zeros_like(stg)

            _scan(work, work, stg, lam_ref, n_groups, s, nb, True, False)

            @pl.when(t == n_tiles - 1)
            def _():
                _chunk_starts_adjoint(stg, initg, lcl_ref, s, nb)

        @pl.when(ph == 2)
        def _():
            @pl.when(t == 0)
            def _():
                stg[...] = initg[...]

            _scan_adjoint(work, hs_all, tile * n_groups, stg, acc, lam_ref, n_groups, s, nb)
            du = lambda blk: _mm(work[:, states(blk)].astype(BF16), bt_ref[blk])
            du_t = _blockwise(du, nb) + dpre_all[tile_rows, :] * d_ref[...]
            dy_cm[tile_groups] = du_t.reshape(n_groups, SUBLANES, width)
            for blk in range(nb):
                gb_ref[blk] += _mm_tn(u_b[:, lanes(blk)], work[:, states(blk)].astype(BF16))

            @pl.when(t == n_tiles - 1)
            def _():
                stores = _chunk_copies(du_hbm, b, dy_cm, sems.at[0], False)
                for cp in stores:
                    cp.start()
                for cp in stores:
                    cp.wait()

        @pl.when(last)
        def _():
            glam_ref[...] = jnp.sum(acc[...], axis=0, keepdims=True)
            exchange_finish()

    def tile(b, ph, t):
        return (b, jnp.where(ph == 0, t, n_tiles - 1 - t), 0)

    full = lambda a: pl.BlockSpec(a.shape, lambda b, ph, t: (0,) * a.ndim)
    hbm = pl.BlockSpec(memory_space=pl.ANY)
    res = pl.pallas_call(
        body,
        name="ssm_bwd",
        grid=(batch, 3, n_tiles),
        in_specs=[
            hbm,
            pl.BlockSpec((None, rows, width), tile),
            pl.BlockSpec((None, rows, 2 * width), tile),
            hbm,
            pl.BlockSpec((None, SUBLANES, 2 * s), lambda b, ph, t: (b, 0, 0)),
            full(b_tab), full(b_tab_t), full(c_tab_t), full(lam8), full(lcl), full(d_skip), full(w_glu),
        ]
        + [hbm] * nr,
        out_specs=[
            hbm,
            full(b_tab), full(b_tab_t), full(w_glu), full(d_skip),
            pl.BlockSpec((1, 2 * s), lambda b, ph, t: (0, 0)),
        ]
        + [hbm] * nr,
        out_shape=[
            jax.ShapeDtypeStruct((batch, seq, width), F32),
            jax.ShapeDtypeStruct(b_tab.shape, F32),
            jax.ShapeDtypeStruct(b_tab_t.shape, F32),
            jax.ShapeDtypeStruct(w_glu.shape, F32),
            jax.ShapeDtypeStruct(d_skip.shape, F32),
            jax.ShapeDtypeStruct((1, 2 * s), F32),
        ]
        + [jax.ShapeDtypeStruct(a.shape, F32) for a in ready],
        scratch_shapes=[
            pltpu.VMEM((cl, NCH, width), F32),
            pltpu.VMEM((cl, NCH, width), F32),
            pltpu.VMEM((rows, 2 * s), F32),
            pltpu.VMEM((seq + SUBLANES, s), jnp.uint32),
            pltpu.VMEM((seq, width), F32),
        ]
        + [pltpu.VMEM((SUBLANES, 2 * s), F32)] * 4
        + [pltpu.SemaphoreType.DMA((2, NCH)), pltpu.SemaphoreType.DMA((nr, 7)), pltpu.SemaphoreType.DMA((nr, 7))],
        compiler_params=_params("arbitrary", "arbitrary", "arbitrary"),
    )(u, pre_p, z_p, dy, init, b_tab, b_tab_t, c_tab_t, lam8, lcl, d_skip, w_glu, *ready)
    return res[:6], res[6:]


def _kv_fwd(mem, g_mem, w_kv):
    batch, n_mem, d = mem.shape
    kvw = w_kv.shape[1]

    def body(mem_ref, g_ref, w_ref, kv_ref):
        m = mem_ref[...]
        kv_ref[...] = _mm((m * _rms(m) * g_ref[...]).astype(BF16), w_ref[...])

    return pl.pallas_call(
        body,
        name="kv_fwd",
        grid=(batch,),
        in_specs=[
            pl.BlockSpec((None, n_mem, d), lambda b: (b, 0, 0)),
            pl.BlockSpec((1, d), lambda b: (0, 0)),
            pl.BlockSpec((d, kvw), lambda b: (0, 0)),
        ],
        out_specs=pl.BlockSpec((None, n_mem, kvw), lambda b: (b, 0, 0)),
        out_shape=jax.ShapeDtypeStruct((batch, n_mem, kvw), F32),
        compiler_params=_params("arbitrary"),
    )(mem, g_mem, w_kv)


def _kv_bwd(mem, dkv, g_mem, w_kv):
    batch, n_mem, d = mem.shape
    kvw = w_kv.shape[1]

    def body(mem_ref, dkv_ref, g_ref, w_ref, gw_ref, gg_ref):
        @pl.when(pl.program_id(0) == 0)
        def _():
            gw_ref[...] = jnp.zeros_like(gw_ref)
            gg_ref[...] = jnp.zeros_like(gg_ref)

        m = mem_ref[...]
        mn = m * _rms(m)
        dkv_b = dkv_ref[...].astype(BF16)
        gw_ref[...] += _mm_tn((mn * g_ref[...]).astype(BF16), dkv_b)
        gg_ref[...] += jnp.sum(_mm_nt(dkv_b, w_ref[...]) * mn, axis=0, keepdims=True)

    return pl.pallas_call(
        body,
        name="kv_bwd",
        grid=(batch,),
        in_specs=[
            pl.BlockSpec((None, n_mem, d), lambda b: (b, 0, 0)),
            pl.BlockSpec((None, n_mem, kvw), lambda b: (b, 0, 0)),
            pl.BlockSpec((1, d), lambda b: (0, 0)),
            pl.BlockSpec((d, kvw), lambda b: (0, 0)),
        ],
        out_specs=[pl.BlockSpec((d, kvw), lambda b: (0, 0)), pl.BlockSpec((1, d), lambda b: (0, 0))],
        out_shape=[jax.ShapeDtypeStruct((d, kvw), F32), jax.ShapeDtypeStruct((1, d), F32)],
        compiler_params=_params("arbitrary"),
    )(mem, dkv, g_mem, w_kv)


def _tail(x2, target2, proj, y_pool, y_ssm, kv, w_out, g_post):
    tokens, d = x2.shape
    batch, n_mem, kvw = kv.shape
    pool_w, ssm_w, att_w, mix = y_pool.shape[1], y_ssm.shape[1], kvw // 2, w_out.shape[0]
    assert (mix - att_w) % att_w == 0 and proj.shape[1] == 2 * mix
    hd = att_w // MEM_HEADS
    cl = _token_tile(tokens // batch, 512)
    n_tiles = tokens // cl
    per_seq = tokens // batch // cl
    qk_scale = hd**-0.5

    def body(x_ref, tg_ref, gate_ref, yp_ref, ys_ref, q_ref, kv_ref, w_ref, g_ref,
             dres_ref, dgate_ref, dyp_ref, dys_ref, dq_ref, dkv_ref, gw_hbm, gg_ref, loss_ref, acc, sem):
        i = pl.program_id(0)

        @pl.when(i == 0)
        def _():
            acc[...] = jnp.zeros_like(acc)
            gg_ref[...] = jnp.zeros_like(gg_ref)
            loss_ref[...] = jnp.zeros_like(loss_ref)

        @pl.when(i % per_seq == 0)
        def _():
            dkv_ref[...] = jnp.zeros_like(dkv_ref)

        k = kv_ref[:, 0:att_w].astype(BF16)
        v = kv_ref[:, att_w : 2 * att_w].astype(BF16)
        lane = lax.broadcasted_iota(jnp.int32, (1, att_w), 1)
        heads = [(lane >= h * hd) & (lane < (h + 1) * hd) for h in range(MEM_HEADS)]
        g = g_ref[...]

        def part(rows):
            n_rows = rows.stop - rows.start
            q = q_ref[rows, :]
            probs, q_heads = [], []
            att = jnp.zeros((n_rows, att_w), F32)
            for mask in heads:
                qh = jnp.where(mask, q, 0.0).astype(BF16)
                sc = _mm_nt(qh, k) * qk_scale
                e = jnp.exp(sc - jnp.max(sc, axis=-1, keepdims=True))
                p = e * (1.0 / jnp.sum(e, axis=-1, keepdims=True))
                att = att + jnp.where(mask, _mm(p.astype(BF16), v), 0.0)
                probs.append(p)
                q_heads.append(qh)

            ycat = jnp.concatenate([yp_ref[rows, :], ys_ref[rows, :], att], axis=1)
            gate = gate_ref[rows, :]
            sig = jax.nn.sigmoid(gate)
            silu = gate * sig
            yg = (ycat * silu).astype(BF16)
            out = _mm(yg, w_ref[...])
            r = _rms(out)
            on = out * r
            err = x_ref[rows, :] + on * g - tg_ref[rows, :]
            loss_ref[...] += 0.5 * jnp.sum(jnp.mean(err * err, axis=-1, keepdims=True), axis=0, keepdims=True)
            dres = err * (1.0 / d)
            dres_ref[rows, :] = dres
            gg_ref[...] += jnp.sum(dres * on, axis=0, keepdims=True)
            don = dres * g
            dout = (r * (don - on * jnp.mean(don * on, axis=-1, keepdims=True))).astype(BF16)
            acc[...] += _mm_tn(yg, dout)
            dyg = _mm_nt(dout, w_ref[...])
            dgate_ref[rows, :] = dyg * ycat * (sig * (1.0 + gate * (1.0 - sig)))
            dycat = dyg * silu
            dyp_ref[rows, :] = dycat[:, 0:pool_w]
            dys_ref[rows, :] = dycat[:, pool_w : pool_w + ssm_w]
            datt = dycat[:, pool_w + ssm_w : mix]

            dq = jnp.zeros((n_rows, att_w), F32)
            dk = jnp.zeros((n_mem, att_w), F32)
            dv = jnp.zeros((n_mem, att_w), F32)
            for mask, p, qh in zip(heads, probs, q_heads):
                doh = jnp.where(mask, datt, 0.0).astype(BF16)
                dp = _mm_nt(doh, v)
                ds = (p * (dp - jnp.sum(p * dp, axis=-1, keepdims=True)) * qk_scale).astype(BF16)
                dq = dq + jnp.where(mask, _mm(ds, k), 0.0)
                dk = dk + _mm_tn(ds, qh)
                dv = dv + _mm_tn(p.astype(BF16), doh)
            dq_ref[rows, :] = dq
            dkv_ref[:, 0:att_w] += dk
            dkv_ref[:, att_w : 2 * att_w] += dv

        part(slice(0, cl))

        @pl.when(i == n_tiles - 1)
        def _():
            cp = pltpu.make_async_copy(acc, gw_hbm, sem)
            cp.start()
            cp.wait()

    tok = lambda w: pl.BlockSpec((cl, w), lambda i: (i, 0))
    chunked = tok(ssm_w)
    per_batch = pl.BlockSpec((None, n_mem, kvw), lambda i: (i // per_seq, 0, 0))
    return pl.pallas_call(
        body,
        name="tail",
        grid=(n_tiles,),
        in_specs=[
            tok(d), tok(d), pl.BlockSpec((cl, mix), lambda i: (i, 1)), tok(pool_w), chunked,
            pl.BlockSpec((cl, att_w), lambda i: (i, (mix - att_w) // att_w)), per_batch,
            pl.BlockSpec((mix, d), lambda i: (0, 0)),
            pl.BlockSpec((1, d), lambda i: (0, 0)),
        ],
        out_specs=[
            tok(d), tok(mix), tok(pool_w), chunked, tok(att_w), per_batch,
            pl.BlockSpec(memory_space=pl.ANY),
            pl.BlockSpec((1, d), lambda i: (0, 0)),
            pl.BlockSpec((1, 1), lambda i: (0, 0)),
        ],
        out_shape=[
            jax.ShapeDtypeStruct((tokens, d), F32),
            jax.ShapeDtypeStruct((tokens, mix), F32),
            jax.ShapeDtypeStruct((tokens, pool_w), F32),
            jax.ShapeDtypeStruct((tokens, ssm_w), F32),
            jax.ShapeDtypeStruct((tokens, att_w), F32),
            jax.ShapeDtypeStruct(kv.shape, F32),
            jax.ShapeDtypeStruct((mix, d), F32),
            jax.ShapeDtypeStruct((1, d), F32),
            jax.ShapeDtypeStruct((1, 1), F32),
        ],
        scratch_shapes=[pltpu.VMEM((mix, d), F32), pltpu.SemaphoreType.DMA],
        compiler_params=_params("arbitrary"),
    )(x2, target2, proj, y_pool, y_ssm, proj, kv, w_out, g_post)


def _pack(arrays):
    flat = jnp.concatenate([a.reshape(-1) for a in arrays])
    rows = -(-flat.size // (SUBLANES * LANES)) * SUBLANES
    return jnp.pad(flat, (0, rows * LANES - flat.size)).reshape(rows, LANES)


def _unpack(packed, like):
    flat, out, at = packed.reshape(-1), [], 0
    for a in like:
        out.append(flat[at : at + a.size].reshape(a.shape))
        at += a.size
    return out


def kernel(x, mem, g_pre, w_in, w_pool, pool_scale, a_re, a_im, log_dt, b_re, b_im, c_re, c_im, d_skip, w_glu, g_mem, w_kv, w_out, g_post, loss_target, m_g_pre, m_w_in, m_w_pool, m_pool_scale, m_a_re, m_a_im, m_log_dt, m_b_re, m_b_im, m_c_re, m_c_im, m_d_skip, m_w_glu, m_g_mem, m_w_kv, m_w_out, m_g_post, v_g_pre, v_w_in, v_w_pool, v_pool_scale, v_a_re, v_a_im, v_log_dt, v_b_re, v_b_im, v_c_re, v_c_im, v_d_skip, v_w_glu, v_g_mem, v_w_kv, v_w_out, v_g_post):
    batch, seq, d = x.shape
    cl = seq // NCH
    pool_w, ssm_w = pool_scale.shape[1], d_skip.shape[1]
    att_w = w_kv.shape[2] // 2
    tokens = batch * seq
    x2 = x.reshape(tokens, d)
    target2 = loss_target.reshape(tokens, d)

    wp_blk = jax.scipy.linalg.block_diag(*w_pool[0]).astype(BF16)
    ssm_params = (a_re[0], a_im[0], log_dt[0], b_re[0], b_im[0], c_re[0], c_im[0])
    (lam_row, b_tab, c_tab), tables_vjp = jax.vjp(_ssm_tables, *ssm_params)
    nb = b_tab.shape[0]
    lam8 = jnp.broadcast_to(lam_row, (SUBLANES, lam_row.shape[1]))
    lcl = _lam_power(a_re[0], a_im[0], log_dt[0], float(cl), 1.0, nb)
    b_bf, c_bf = b_tab.astype(BF16), c_tab.astype(BF16)

    proj, w_in_g, (w_glu_g,) = _in_proj(x2, g_pre, w_in[0], [w_glu[0].astype(BF16)])
    proj3 = proj.reshape(batch, seq, proj.shape[1])
    w_glu_f = w_glu_g.transpose(1, 0, 2).reshape(w_glu_g.shape[1], N_DEV * w_glu_g.shape[2])
    y_pool = _pool_fwd(proj, wp_blk, pool_scale, batch, seq)
    (y_ssm, pre_ssm, z_ssm, init_ssm), (w_out_g, w_kv_g) = _ssm_fwd(
        proj3, pool_w, b_bf, c_bf, lam8, lcl, d_skip, w_glu_f, [w_out[0].astype(BF16), w_kv[0].astype(BF16)])
    w_out_f = w_out_g.reshape(N_DEV * w_out_g.shape[1], w_out_g.shape[2])
    w_kv_f = w_kv_g.reshape(N_DEV * w_kv_g.shape[1], w_kv_g.shape[2])
    kv = _kv_fwd(mem, g_mem, w_kv_f)

    dres, dgate, dy_pool, dy_ssm, dq, dkv, gw_out, gg_post, loss_part = _tail(
        x2, target2, proj, y_pool, y_ssm.reshape(tokens, ssm_w), kv, w_out_f, g_post)

    gw_kv, gg_mem = _kv_bwd(mem, dkv, g_mem, w_kv_f)
    du_pool, gwp_dense, g_scale = _pool_bwd(proj, dy_pool, wp_blk, pool_scale, batch, seq)
    gw_kv8 = gw_kv.reshape(N_DEV, -1, gw_kv.shape[1])
    gw_out8 = gw_out.reshape(N_DEV, -1, gw_out.shape[1])
    (du_ssm, gb_tab, gc_tab, gw_glu, gd_skip, glam), (kv_landed, out_landed) = _ssm_bwd(
        proj3, pool_w, pre_ssm, z_ssm, dy_ssm.reshape(batch, seq, ssm_w), init_ssm, b_bf, b_bf.transpose(0, 2, 1),
        c_bf.transpose(0, 2, 1), lam8, lcl, d_skip, w_glu_f, [gw_kv8, gw_out8])
    grad_x2, gw_in, gg_pre = _in_proj_bwd(
        x2, dres, du_pool, du_ssm.reshape(tokens, ssm_w), dq, dgate, g_pre,
        w_in_g.transpose(0, 2, 1).reshape(-1, d))

    gw = pool_w // len(POOL_WINDOWS)
    gw_pool = jnp.stack([gwp_dense[i * gw : (i + 1) * gw, i * gw : (i + 1) * gw] for i in range(len(POOL_WINDOWS))])
    g_ssm = tables_vjp((glam, gb_tab, gc_tab))

    small_w = [g_pre, w_pool, pool_scale, a_re, a_im, log_dt, b_re, b_im, c_re, c_im, d_skip, g_mem, g_post]
    small_m = [m_g_pre, m_w_pool, m_pool_scale, m_a_re, m_a_im, m_log_dt, m_b_re, m_b_im, m_c_re, m_c_im, m_d_skip, m_g_mem, m_g_post]
    small_v = [v_g_pre, v_w_pool, v_pool_scale, v_a_re, v_a_im, v_log_dt, v_b_re, v_b_im, v_c_re, v_c_im, v_d_skip, v_g_mem, v_g_post]
    small_g = [gg_pre, gw_pool, g_scale, *g_ssm, gd_skip, gg_mem, gg_post]
    big, small_sum = _reduce_all(
        [gw_in, gw_glu.reshape(ssm_w, N_DEV, -1).transpose(1, 0, 2)],
        [w_in[0], w_glu[0]], [m_w_in[0], m_w_glu[0]], [v_w_in[0], v_w_glu[0]],
        _pack(small_g + [loss_part]),
        [(gw_kv8, kv_landed, w_kv[0], m_w_kv[0], v_w_kv[0]), (gw_out8, out_landed, w_out[0], m_w_out[0], v_w_out[0])])
    big = {name: tuple(t[None] for t in res) for name, res in zip(["w_in", "w_glu", "w_kv", "w_out"], big)}

    flat2 = lambda a: a.reshape(-1, a.shape[-1])
    sg = _unpack(small_sum, [flat2(a) for a in small_w] + [loss_part])
    loss = sg[-1].reshape(())
    updates = _adamw_small(sg[:-1], [flat2(a) for a in small_w], [flat2(a) for a in small_m], [flat2(a) for a in small_v])
    sg = [g.reshape(a.shape) for g, a in zip(sg[:-1], small_w)]
    sd, sm, sv = ([u[kind].reshape(a.shape) for u, a in zip(updates, small_w)] for kind in range(3))

    order = ["g_pre", "w_in", "w_pool", "pool_scale", "a_re", "a_im", "log_dt", "b_re", "b_im", "c_re", "c_im",
             "d_skip", "w_glu", "g_mem", "w_kv", "w_out", "g_post"]
    small_names = ["g_pre", "w_pool", "pool_scale", "a_re", "a_im", "log_dt", "b_re", "b_im", "c_re", "c_im",
                   "d_skip", "g_mem", "g_post"]
    outs = [[], [], [], []]
    for name in order:
        if name in big:
            parts = big[name]
        else:
            j = small_names.index(name)
            parts = (sg[j], sd[j], sm[j], sv[j])
        for kind in range(4):
            outs[kind].append(parts[kind])
    return (loss, grad_x2.reshape(batch, seq, d), *outs[0], *outs[1], *outs[2], *outs[3])
```

```python
import functools
import math

import jax
import jax.numpy as jnp
from jax import lax
from jax.experimental import pallas as pl
from jax.experimental.pallas import tpu as pltpu

F32 = jnp.float32
BF16 = jnp.bfloat16
MESH = pl.DeviceIdType.MESH

N_DEV = 8
NCH = 8
SUBLANES = 8
LANES = 128
VMEM_LIMIT = 56 * 1024 * 1024

EPS = 1e-6
POOL_WINDOWS = (2, 4, 8, 16)
MEM_HEADS = 4
SSM_GROUP = 16
SSM_N = 64
ADAM_LR, ADAM_B1, ADAM_B2, ADAM_EPS, ADAM_WD, ADAM_STEP = 0.001, 0.9, 0.999, 1e-08, 0.01, 10


def _mm(a, b):
    return jnp.dot(a, b, preferred_element_type=F32)


def _mm_nt(a, b):
    return lax.dot_general(a, b, (((1,), (1,)), ((), ())), preferred_element_type=F32)


def _mm_tn(a, b):
    return lax.dot_general(a, b, (((0,), (0,)), ((), ())), preferred_element_type=F32)


def _params(*sem):
    return pltpu.CompilerParams(dimension_semantics=sem or None, vmem_limit_bytes=VMEM_LIMIT)


def _adamw(w, g, m, v):
    m = ADAM_B1 * m + (1.0 - ADAM_B1) * g
    v = ADAM_B2 * v + (1.0 - ADAM_B2) * (g * g)
    m_hat = m / (1.0 - ADAM_B1**ADAM_STEP)
    v_hat = v / (1.0 - ADAM_B2**ADAM_STEP)
    delta = -ADAM_LR * (m_hat / (jnp.sqrt(v_hat) + ADAM_EPS) + ADAM_WD * w)
    return delta, m, v


def _gelu(x):
    k = math.sqrt(2.0 / math.pi)
    return 0.5 * x * (1.0 + jnp.tanh(k * (x + 0.044715 * x * x * x)))


def _gelu_grad(x):
    k = math.sqrt(2.0 / math.pi)
    th = jnp.tanh(k * (x + 0.044715 * x * x * x))
    return 0.5 * (1.0 + th) + 0.5 * x * (1.0 - th * th) * (k * (1.0 + 3.0 * 0.044715 * x * x))


def _place():
    return lax.axis_index("x"), lax.axis_index("y"), lax.axis_index("c")


def _gather_steps(ins, outs, send_sems, recv_sems):
    n = len(ins)
    x, y, c = _place()
    me, sibling = (x, y, c), (x, y, 1 - c)
    chips = [(1 - x, y), (x, 1 - y), (1 - x, 1 - y)]
    sent = []

    def slot(px, py, pc):
        return 4 * px + 2 * py + pc

    def copy(a, k, block, to):
        ref = outs[a].at[slot(*block)]
        return pltpu.make_async_remote_copy(
            src_ref=ref, dst_ref=ref, send_sem=send_sems.at[a, k], recv_sem=recv_sems.at[a, k],
            device_id=to, device_id_type=MESH)

    def start():
        for a in range(n):
            outs[a][slot(*me)] = ins[a][...].astype(outs[a].dtype)
        for a in range(n):
            sent.append(copy(a, 0, me, sibling))
            sent.extend(copy(a, 1 + j, me, (*chip, c)) for j, chip in enumerate(chips))
        for cp in sent:
            cp.start()

    def forward():
        for j, chip in enumerate(chips):
            for a in range(n):
                copy(a, 1 + j, (*chip, c), me).wait_recv()
                cp = copy(a, 4 + j, (*chip, c), sibling)
                cp.start()
                sent.append(cp)

    def finish():
        for a in range(n):
            copy(a, 0, sibling, me).wait_recv()
            for j, chip in enumerate(chips):
                copy(a, 4 + j, (*chip, 1 - c), me).wait_recv()
        for cp in sent:
            cp.wait_send()

    return start, forward, finish


def _exchange_steps(n, src_of, landing, send_sems, recv_sems):
    x, y, c = _place()
    me = 4 * x + 2 * y + c
    peers = []
    for j in range(1, N_DEV):
        px = 1 - x if j & 4 else x
        py = 1 - y if j & 2 else y
        pc = 1 - c if j & 1 else c
        peers.append((px, py, pc))

    def copy(a, j, from_slot, to_slot, peer):
        return pltpu.make_async_remote_copy(
            src_ref=src_of(a, to_slot), dst_ref=landing[a].at[from_slot],
            send_sem=send_sems.at[a, j], recv_sem=recv_sems.at[a, j], device_id=peer, device_id_type=MESH)

    def start():
        for a in range(n):
            for j, p in enumerate(peers):
                copy(a, j, me, 4 * p[0] + 2 * p[1] + p[2], p).start()

    def finish():
        for a in range(n):
            for j, p in enumerate(peers):
                slot = 4 * p[0] + 2 * p[1] + p[2]
                copy(a, j, slot, slot, p).wait_recv()
        for a in range(n):
            for j, p in enumerate(peers):
                copy(a, j, me, 4 * p[0] + 2 * p[1] + p[2], p).wait_send()

    return start, finish


def _ordered_sum(gathered, out_ref):
    rows = out_ref.shape[0]

    def step(i, _):
        r = pl.ds(pl.multiple_of(i * SUBLANES, SUBLANES), SUBLANES)
        g = gathered[0, r, :]
        for d in range(1, N_DEV):
            g = g + gathered[d, r, :]
        out_ref[r, :] = g
        return 0

    lax.fori_loop(0, rows // SUBLANES, step, 0)


def _adamw_all(gs, ws, ms, vs):
    n = len(gs)

    def body(*refs):
        g, w, m, v = refs[:n], refs[n : 2 * n], refs[2 * n : 3 * n], refs[3 * n : 4 * n]
        outs = refs[4 * n :]
        for a in range(n):
            rows = g[a].shape[0]
            chunk = math.gcd(rows, 128)

            def step(i, _, a=a, chunk=chunk):
                r = pl.ds(pl.multiple_of(i * chunk, chunk), chunk)
                grad = g[a][r, :]
                delta, nm, nv = _adamw(w[a][r, :], grad, m[a][r, :], v[a][r, :])
                outs[4 * a][r, :] = grad
                outs[4 * a + 1][r, :] = delta
                outs[4 * a + 2][r, :] = nm
                outs[4 * a + 3][r, :] = nv
                return 0

            lax.fori_loop(0, rows // chunk, step, 0)

    vmem = pl.BlockSpec(memory_space=pltpu.VMEM)
    out_shape = []
    for wa in ws:
        out_shape += [jax.ShapeDtypeStruct(wa.shape, F32)] * 4
    res = pl.pallas_call(
        body,
        name="adamw_all",
        out_shape=out_shape,
        in_specs=[vmem] * (4 * n),
        out_specs=[vmem] * (4 * n),
        compiler_params=_params(),
    )(*gs, *ws, *ms, *vs)
    return [tuple(res[4 * a : 4 * a + 4]) for a in range(n)]


def _reduce_all(parts, small, early):
    n, ne = len(parts), len(early)
    parts4 = [p.reshape(4, 2, *p.shape[1:]) for p in parts]
    blks = [p.shape[1:] for p in parts]

    def body(*refs):
        refs = list(refs)
        take = lambda k: [refs.pop(0) for _ in range(k)]
        part = take(n)
        (small_ref,) = take(1)
        early_in = [take(2) for _ in range(ne)]
        outs = take(n)
        small_all, small_sum = take(2)
        early_out = take(ne)
        own, r1, got_a1, got_a2, got_b1, got_b2, pass_a, pass_b = (take(n) for _ in range(8))
        early_buf = take(ne)
        s1_send, s1_recv, h_send, h_recv, loc, small_send, small_recv, early_sems = refs
        small_start, small_forward, small_finish = _gather_steps([small_ref], [small_all], small_send, small_recv)
        x, y, c = _place()
        me = 4 * x + 2 * y + c
        landed = [pltpu.make_async_copy(early_in[e][1], early_buf[e], early_sems.at[e, 0]) for e in range(ne)]
        for cp in landed:
            cp.start()
        sibling = (x, y, 1 - c)
        chips = [(1 - x, y), (x, 1 - y), (1 - x, 1 - y)]

        def rowwise(rows, fn):
            chunk = math.gcd(rows, 128)

            def step(i, _):
                fn(pl.ds(pl.multiple_of(i * chunk, chunk), chunk))
                return 0

            lax.fori_loop(0, rows // chunk, step, 0)

        stage1, local = [], []
        for a in range(n):
            cp = pltpu.make_async_remote_copy(
                src_ref=part[a].at[:, 1 - c], dst_ref=r1[a], send_sem=s1_send.at[a], recv_sem=s1_recv.at[a],
                device_id=sibling, device_id_type=MESH)
            cp.start()
            stage1.append(cp)
            lc = pltpu.make_async_copy(part[a].at[:, c], own[a], loc.at[a])
            lc.start()
            local.append(lc)
        small_start()
        x_nbr, y_nbr = (1 - x, y, c), (x, 1 - y, c)
        mine, mine_x, mine_y = 2 * x + y, 2 * (1 - x) + y, 2 * x + (1 - y)

        def hop(a, k, src, dst, to):
            return pltpu.make_async_remote_copy(
                src_ref=src, dst_ref=dst, send_sem=h_send.at[a, k], recv_sem=h_recv.at[a, k],
                device_id=to, device_id_type=MESH)

        first, second = [], []
        for a in range(n):
            half = blks[a][0] // 2
            up, low = pl.ds(0, half), pl.ds(half, half)
            local[a].wait()
            stage1[a].wait_recv()
            for chip in range(4):

                def add(r, a=a, chip=chip):
                    own[a][chip, r, :] = own[a][chip, r, :] + r1[a][chip, r, :]

                rowwise(blks[a][0], add)
            first.append([
                hop(a, 0, own[a].at[pl.ds(2 * (1 - x), 2), up], got_a1[a], x_nbr),
                hop(a, 2, own[a].at[2 * x + (1 - y), low], got_b1[a].at[x], y_nbr),
                hop(a, 3, own[a].at[2 * (1 - x) + (1 - y), low], got_b1[a].at[1 - x], y_nbr),
            ])
            for cp in first[a]:
                cp.start()
        small_forward()
        for a in range(n):
            half = blks[a][0] // 2
            first[a][0].wait_recv()

            def fold_upper(r, a=a):
                own[a][mine, r, :] = own[a][mine, r, :] + got_a1[a][y, r, :]
                pass_a[a][r, :] = own[a][mine_y, r, :] + got_a1[a][1 - y, r, :]

            rowwise(half, fold_upper)
            first[a][1].wait_recv()
            first[a][2].wait_recv()

            def fold_lower(r, a=a, half=half):
                rl = pl.ds(pl.multiple_of(r.start + half, SUBLANES), r.size)
                own[a][mine, rl, :] = own[a][mine, rl, :] + got_b1[a][x, r, :]
                pass_b[a][r, :] = own[a][mine_x, rl, :] + got_b1[a][1 - x, r, :]

            rowwise(half, fold_lower)
            second.append([hop(a, 1, pass_a[a], got_a2[a], y_nbr), hop(a, 4, pass_b[a], got_b2[a], x_nbr)])
            for cp in second[a]:
                cp.start()
        for e in range(ne):
            part_e, _ = early_in[e]
            landed[e].wait()
            own_block = pltpu.make_async_copy(part_e.at[me], early_buf[e].at[me], early_sems.at[e, 1])
            own_block.start()
            own_block.wait()

            def sum_early(r, e=e):
                g = early_buf[e][0, r, :]
                for dev in range(1, N_DEV):
                    g = g + early_buf[e][dev, r, :]
                early_out[e][r, :] = g

            rowwise(early_buf[e].shape[1], sum_early)
        for a in range(n):
            half = blks[a][0] // 2
            second[a][0].wait_recv()
            second[a][1].wait_recv()

            def finish_rows(r, a=a, half=half):
                rl = pl.ds(pl.multiple_of(r.start + half, SUBLANES), r.size)
                outs[a][r, :] = own[a][mine, r, :] + got_a2[a][r, :]
                outs[a][rl, :] = own[a][mine, rl, :] + got_b2[a][r, :]

            rowwise(half, finish_rows)
        small_finish()
        _ordered_sum(small_all, small_sum)
        for cp in stage1 + [cp for group in first + second for cp in group]:
            cp.wait_send()

    vmem = pl.BlockSpec(memory_space=pltpu.VMEM)
    hbm = pl.BlockSpec(memory_space=pl.ANY)
    out_shape = [jax.ShapeDtypeStruct(b, F32) for b in blks]
    out_shape += [jax.ShapeDtypeStruct((N_DEV, *small.shape), F32), jax.ShapeDtypeStruct(small.shape, F32)]
    out_shape += [jax.ShapeDtypeStruct(e[0].shape[1:], F32) for e in early]
    halves = [(b[0] // 2, b[1]) for b in blks]
    scratch = (
        [pltpu.VMEM((4, *b), F32) for b in blks]
        + [pltpu.VMEM((4, *b), F32) for b in blks]
        + [pltpu.VMEM((2, *h), F32) for h in halves]
        + [pltpu.VMEM(h, F32) for h in halves]
        + [pltpu.VMEM((2, *h), F32) for h in halves]
        + [pltpu.VMEM(h, F32) for h in halves] * 3
        + [pltpu.VMEM(e[0].shape, F32) for e in early]
        + [pltpu.SemaphoreType.DMA((n,)), pltpu.SemaphoreType.DMA((n,)), pltpu.SemaphoreType.DMA((n, 5)),
           pltpu.SemaphoreType.DMA((n, 5)), pltpu.SemaphoreType.DMA((n,)),
           pltpu.SemaphoreType.DMA((1, 7)), pltpu.SemaphoreType.DMA((1, 7)), pltpu.SemaphoreType.DMA((ne, 2))]
    )
    res = pl.pallas_call(
        body,
        name="reduce_all",
        out_shape=out_shape,
        in_specs=[hbm] * n + [vmem] + [hbm, hbm] * ne,
        out_specs=[vmem] * (n + 2 + ne),
        scratch_shapes=scratch,
        compiler_params=_params(),
    )(*parts4, small, *[t for e in early for t in e])
    return list(res[:n]) + list(res[n + 2 :]), res[n + 1]


def _rms(x):
    return lax.rsqrt(jnp.mean(x * x, axis=-1, keepdims=True) + EPS)


def _token_tile(tokens, want):
    tile = min(want, tokens // 2)
    assert tokens % tile == 0 and tile % 16 == 0
    return tile


def _in_proj(x2, g_pre, w_in_blk, shards):
    tokens, d = x2.shape
    nb = w_in_blk.shape[1]
    tm = _token_tile(tokens, 2048)
    n_t = tokens // tm
    ns = len(shards)
    x_pos, y_pos, c_pos = _place()
    slot = lambda px, py, pc: 4 * px + 2 * py + pc
    chip_order = [(x_pos, y_pos), (1 - x_pos, y_pos), (x_pos, 1 - y_pos), (1 - x_pos, 1 - y_pos)]
    order = jnp.stack([slot(px, py, pc) for px, py in chip_order for pc in (c_pos, 1 - c_pos)]).astype(jnp.int32)

    def body(order_ref, x_ref, g_ref, w_ref, *rest):
        shard_hbm, proj_ref, w_hbm = rest[:ns], rest[ns], rest[ns + 1]
        gathered = rest[ns + 2 : 2 * ns + 2]
        h_all, land, w_send, w_recv, out_sem, send_sems, recv_sems, own_sems = rest[2 * ns + 2 :]
        j, i = pl.program_id(0), pl.program_id(1)
        x, y, c = _place()
        me, sibling = (x, y, c), (x, y, 1 - c)
        chips = [(1 - x, y), (x, 1 - y), (1 - x, 1 - y)]
        own = [pltpu.make_async_copy(shard_hbm[a], gathered[a].at[slot(*me)], own_sems.at[a]) for a in range(ns)]
        start, finish = _exchange_steps(ns, lambda a, s: shard_hbm[a], gathered, send_sems, recv_sems)

        def copy(k, block, to):
            ref = land.at[slot(*block)]
            return pltpu.make_async_remote_copy(
                src_ref=ref, dst_ref=ref, send_sem=w_send.at[k], recv_sem=w_recv.at[k], device_id=to, device_id_type=MESH)

        first_sends = [copy(0, me, sibling)] + [copy(1 + k, me, (*chip, c)) for k, chip in enumerate(chips)]
        forwards = [copy(4 + k, (*chip, c), sibling) for k, chip in enumerate(chips)]

        @pl.when((j == 0) & (i == 0))
        def _():
            land[slot(*me)] = w_ref[...].astype(BF16)
            for cp in first_sends:
                cp.start()
            start()
            for cp in own:
                cp.start()

        @pl.when((j == 1) & (i == 0))
        def _():
            copy(0, sibling, me).wait_recv()

        for k, chip in enumerate(chips):

            @pl.when((j == 2 + 2 * k) & (i == 0))
            def _(k=k, chip=chip):
                copy(1 + k, (*chip, c), me).wait_recv()
                forwards[k].start()

            @pl.when((j == 3 + 2 * k) & (i == 0))
            def _(k=k, chip=chip):
                copy(4 + k, (*chip, 1 - c), me).wait_recv()

        rows = pl.ds(pl.multiple_of(i * tm, tm), tm)

        @pl.when(j == 0)
        def _():
            x_t = x_ref[...]
            h_all[rows, :] = (x_t * _rms(x_t) * g_ref[...]).astype(BF16)

        proj_ref[...] = _mm(h_all[rows, :], land[order_ref[j]])

        @pl.when((j == N_DEV - 1) & (i == n_t - 1))
        def _():
            for cp in first_sends + forwards:
                cp.wait_send()
            finish()
            for cp in own:
                cp.wait()
            out = pltpu.make_async_copy(land, w_hbm, out_sem)
            out.start()
            out.wait()

    hbm = pl.BlockSpec(memory_space=pl.ANY)
    res = pl.pallas_call(
        body,
        name="in_proj",
        grid_spec=pltpu.PrefetchScalarGridSpec(
            num_scalar_prefetch=1,
            grid=(N_DEV, n_t),
            in_specs=[
                pl.BlockSpec((tm, d), lambda j, i, order: (jnp.where(j == 0, i, n_t - 1), 0)),
                pl.BlockSpec((1, d), lambda j, i, order: (0, 0)),
                pl.BlockSpec((d, nb), lambda j, i, order: (0, 0)),
            ]
            + [hbm] * ns,
            out_specs=[pl.BlockSpec((tm, nb), lambda j, i, order: (i, order[j])), hbm] + [hbm] * ns,
            scratch_shapes=[
                pltpu.VMEM((tokens, d), BF16),
                pltpu.VMEM((N_DEV, d, nb), BF16),
                pltpu.SemaphoreType.DMA((7,)),
                pltpu.SemaphoreType.DMA((7,)),
                pltpu.SemaphoreType.DMA,
                pltpu.SemaphoreType.DMA((ns, 7)),
                pltpu.SemaphoreType.DMA((ns, 7)),
                pltpu.SemaphoreType.DMA((ns,)),
            ],
        ),
        out_shape=[
            jax.ShapeDtypeStruct((tokens, N_DEV * nb), F32),
            jax.ShapeDtypeStruct((N_DEV, d, nb), BF16),
        ]
        + [jax.ShapeDtypeStruct((N_DEV, *a.shape), a.dtype) for a in shards],
        compiler_params=_params("arbitrary", "arbitrary"),
    )(order, x2, g_pre, w_in_blk, *shards)
    return res[0], res[1], res[2:]


def _in_proj_bwd(x2, dres, du_pool, du_ssm, dq, dgate, g_pre, w_in_t):
    tokens, d = x2.shape
    nb = w_in_t.shape[0] // N_DEV
    pool_w, ssm_w, att_w, mix = du_pool.shape[1], du_ssm.shape[1], dq.shape[1], dgate.shape[1]
    cl = _token_tile(tokens, 512)
    n_tiles = tokens // cl

    def body(x_ref, dres_ref, dup_ref, dus_ref, dq_ref, dgate_ref, g_ref, w_ref, gx_ref, gw_hbm, gg_ref, acc, sem):
        i = pl.program_id(0)

        @pl.when(i == 0)
        def _():
            acc[...] = jnp.zeros_like(acc)
            gg_ref[...] = jnp.zeros_like(gg_ref)

        x = x_ref[...]
        r = _rms(x)
        xn = x * r
        g = g_ref[...]
        h = (xn * g).astype(BF16)
        dproj = jnp.concatenate([dup_ref[...], dus_ref[...], dq_ref[...], dgate_ref[...]], axis=1).astype(BF16)
        dh = _mm(dproj, w_ref[...])
        for j in range(N_DEV):
            acc[j] += _mm_tn(h, dproj[:, j * nb : (j + 1) * nb])
        gg_ref[...] += jnp.sum(dh * xn, axis=0, keepdims=True)
        dxn = dh * g
        gx_ref[...] = dres_ref[...] + r * (dxn - xn * jnp.mean(dxn * xn, axis=-1, keepdims=True))

        @pl.when(i == n_tiles - 1)
        def _():
            cp = pltpu.make_async_copy(acc, gw_hbm, sem)
            cp.start()
            cp.wait()

    return pl.pallas_call(
        body,
        name="in_proj_bwd",
        grid=(n_tiles,),
        in_specs=[
            pl.BlockSpec((cl, d), lambda i: (i, 0)),
            pl.BlockSpec((cl, d), lambda i: (i, 0)),
            pl.BlockSpec((cl, pool_w), lambda i: (i, 0)),
            pl.BlockSpec((cl, ssm_w), lambda i: (i, 0)),
            pl.BlockSpec((cl, att_w), lambda i: (i, 0)),
            pl.BlockSpec((cl, mix), lambda i: (i, 0)),
            pl.BlockSpec((1, d), lambda i: (0, 0)),
            pl.BlockSpec(w_in_t.shape, lambda i: (0, 0)),
        ],
        out_specs=[
            pl.BlockSpec((cl, d), lambda i: (i, 0)),
            pl.BlockSpec(memory_space=pl.ANY),
            pl.BlockSpec((1, d), lambda i: (0, 0)),
        ],
        out_shape=[
            jax.ShapeDtypeStruct((tokens, d), F32),
            jax.ShapeDtypeStruct((N_DEV, d, nb), F32),
            jax.ShapeDtypeStruct((1, d), F32),
        ],
        scratch_shapes=[pltpu.VMEM((N_DEV, d, nb), F32), pltpu.SemaphoreType.DMA],
        compiler_params=_params("arbitrary"),
    )(x2, dres, du_pool, du_ssm, dq, dgate, g_pre, w_in_t)


def _pool_geometry(seq, width):
    gw = width // len(POOL_WINDOWS)
    col = lax.broadcasted_iota(jnp.int32, (1, width), 1)
    win = jnp.full((1, width), float(POOL_WINDOWS[-1]), F32)
    for gi in range(len(POOL_WINDOWS) - 2, -1, -1):
        win = jnp.where(col < (gi + 1) * gw, float(POOL_WINDOWS[gi]), win)
    row = lax.broadcasted_iota(jnp.int32, (seq, width), 0)
    filling = 1.0 / (lax.broadcasted_iota(jnp.int32, (seq, 1), 0) + 1).astype(F32)
    inv_cnt = jnp.where(row + 1 < win.astype(jnp.int32), filling, 1.0 / win)
    return win, row, inv_cnt


def _window_sums(a, win, seq, back):
    pad = 2 * POOL_WINDOWS[-1]
    zeros = jnp.zeros((pad, a.shape[1]), F32)
    s = jnp.concatenate([a, zeros] if back else [zeros, a], axis=0)
    sums = []
    k = 1
    while k < POOL_WINDOWS[-1]:
        s = s + pltpu.roll(s, seq + pad - k if back else k, 0)
        k *= 2
        sums.append((k, s))
    out = sums[-1][1]
    for k, s in reversed(sums[:-1]):
        out = jnp.where(win <= float(k), s, out)
    return out[0:seq] if back else out[pad : pad + seq]


def _pool_fwd(u2, wp_blk, scale, batch, seq):
    width = scale.shape[1]

    def body(u_ref, w_ref, s_ref, y_ref):
        u = u_ref[...]
        win, row, inv_cnt = _pool_geometry(seq, width)
        diff = _window_sums(u, win, seq, False) * inv_cnt - u
        y_ref[...] =_mm(diff.astype(BF16), w_ref[...]) * s_ref[...]

    return pl.pallas_call(
        body,
        name="pool_fwd",
        grid=(batch,),
        in_specs=[
            pl.BlockSpec((seq, width), lambda b: (b, 0)),
            pl.BlockSpec((width, width), lambda b: (0, 0)),
            pl.BlockSpec((1, width), lambda b: (0, 0)),
        ],
        out_specs=pl.BlockSpec((seq, width), lambda b: (b, 0)),
        out_shape=jax.ShapeDtypeStruct((u2.shape[0], width), F32),
        compiler_params=_params("arbitrary"),
    )(u2, wp_blk, scale)


def _pool_bwd(u2, dy2, wp_blk, scale, batch, seq):
    width = scale.shape[1]

    def body(u_ref, dy_ref, w_ref, s_ref, du_ref, gw_ref, gs_ref):
        @pl.when(pl.program_id(0) == 0)
        def _():
            gw_ref[...] = jnp.zeros_like(gw_ref)
            gs_ref[...] = jnp.zeros_like(gs_ref)

        u = u_ref[...]
        dy = dy_ref[...]
        win, row, inv_cnt = _pool_geometry(seq, width)
        diff = (_window_sums(u, win, seq, False) * inv_cnt - u).astype(BF16)
        gs_ref[...] += jnp.sum(dy * _mm(diff, w_ref[...]), axis=0, keepdims=True)
        dys = (dy * s_ref[...]).astype(BF16)
        gw_ref[...] += _mm_tn(diff, dys)
        dd = _mm_nt(dys, w_ref[...])
        du_ref[...] = _window_sums(dd * inv_cnt, win, seq, True) - dd

    return pl.pallas_call(
        body,
        name="pool_bwd",
        grid=(batch,),
        in_specs=[
            pl.BlockSpec((seq, width), lambda b: (b, 0)),
            pl.BlockSpec((seq, width), lambda b: (b, 0)),
            pl.BlockSpec((width, width), lambda b: (0, 0)),
            pl.BlockSpec((1, width), lambda b: (0, 0)),
        ],
        out_specs=[
            pl.BlockSpec((seq, width), lambda b: (b, 0)),
            pl.BlockSpec((width, width), lambda b: (0, 0)),
            pl.BlockSpec((1, width), lambda b: (0, 0)),
        ],
        out_shape=[
            jax.ShapeDtypeStruct((u2.shape[0], width), F32),
            jax.ShapeDtypeStruct((width, width), F32),
            jax.ShapeDtypeStruct((1, width), F32),
        ],
        compiler_params=_params("arbitrary"),
    )(u2, dy2, wp_blk, scale)


def _state_row(z, n_blocks):
    re = jnp.real(z).reshape(n_blocks, -1)
    im = jnp.imag(z).reshape(n_blocks, -1)
    return jnp.concatenate([re, im], axis=1).reshape(1, -1)


def _ssm_tables(a_re, a_im, log_dt, b_re, b_im, c_re, c_im):
    groups, n_state = a_re.shape
    ch = b_re.shape[2]
    nb = groups * ch // LANES
    gl = groups // nb
    lam = lax.complex(a_re, a_im)
    lam_bar = jnp.exp(lam * jnp.exp(log_dt)[:, None])
    b_bar = ((lam_bar - 1.0) / lam)[..., None] * lax.complex(b_re, b_im)
    eye = jnp.eye(gl, dtype=F32)

    def rows_to_state(t):
        return jnp.einsum("sgnc,gh->sgchn", t.reshape(nb, gl, n_state, ch), eye).reshape(nb, gl * ch, gl * n_state)

    def state_to_rows(t):
        return jnp.einsum("sgcn,gh->shngc", t.reshape(nb, gl, ch, n_state), eye).reshape(nb, gl * n_state, gl * ch)

    b_tab = jnp.concatenate([rows_to_state(jnp.real(b_bar)), rows_to_state(jnp.imag(b_bar))], axis=2)
    c_tab = jnp.concatenate([state_to_rows(c_re), -state_to_rows(c_im)], axis=1)
    return _state_row(lam_bar, nb), b_tab, c_tab


def _lam_power(a_re, a_im, log_dt, power, scale, n_blocks):
    return _state_row(scale * jnp.exp(lax.complex(a_re, a_im) * jnp.exp(log_dt)[:, None] * power), n_blocks)


def _state_blocks(s2, n_blocks, width):
    half = s2 // n_blocks // 2
    assert half % width == 0
    return [(b * 2 * half + o, b * 2 * half + half + o) for b in range(n_blocks) for o in range(0, half, width)]


def _scan(src_ref, dst_ref, st_ref, lam8_ref, n_groups, s, n_blocks, reverse, store):
    lb = 512
    for re0, im0 in _state_blocks(2 * s, n_blocks, lb):
        cr, ci = pl.ds(re0, lb), pl.ds(im0, lb)
        lr = lam8_ref[:, cr]
        li = -lam8_ref[:, ci] if reverse else lam8_ref[:, ci]

        def step(i, carry, cr=cr, ci=ci, lr=lr, li=li):
            hr, hi = carry
            grp = n_groups - 1 - i if reverse else i
            rows = pl.ds(pl.multiple_of(grp * SUBLANES, SUBLANES), SUBLANES)
            nr = lr * hr - li * hi + src_ref[rows, cr]
            ni = lr * hi + li * hr + src_ref[rows, ci]
            if store:
                dst_ref[rows, cr] = nr
                dst_ref[rows, ci] = ni
            return nr, ni

        hr, hi = lax.fori_loop(0, n_groups, step, (st_ref[:, cr], st_ref[:, ci]), unroll=2)
        st_ref[:, cr] = hr
        st_ref[:, ci] = hi


def _pack_state(re, im):
    hi = lax.bitcast_convert_type(re.astype(BF16).astype(F32), jnp.uint32)
    lo = lax.bitcast_convert_type(im.astype(BF16).astype(F32), jnp.uint32)
    return hi | (lo >> 16)


def _unpack_state(word):
    re = lax.bitcast_convert_type(word & jnp.uint32(0xFFFF0000), F32)
    im = lax.bitcast_convert_type(word << 16, F32)
    return re, im


def _scan_adjoint(dh_ref, hprev_ref, group0, stg_ref, acc_ref, lam8_ref, n_groups, s, n_blocks):
    lb = 512
    half = s // n_blocks
    for re0, im0 in _state_blocks(2 * s, n_blocks, lb):
        cr, ci = pl.ds(re0, lb), pl.ds(im0, lb)
        ch = pl.ds(re0 // (2 * half) * half + re0 % (2 * half), lb)
        lr, li = lam8_ref[:, cr], -lam8_ref[:, ci]

        def step(i, carry, cr=cr, ci=ci, ch=ch, lr=lr, li=li):
            gr, gi, ar, ai = carry
            grp = n_groups - 1 - i
            rows = pl.ds(pl.multiple_of(grp * SUBLANES, SUBLANES), SUBLANES)
            ngr = lr * gr - li * gi + dh_ref[rows, cr]
            ngi = lr * gi + li * gr + dh_ref[rows, ci]
            hr, hi = _unpack_state(hprev_ref[pl.ds(pl.multiple_of((group0 + grp) * SUBLANES, SUBLANES), SUBLANES), ch])
            ar = ar + hr * ngr + hi * ngi
            ai = ai + hr * ngi - hi * ngr
            dh_ref[rows, cr] = ngr
            dh_ref[rows, ci] = ngi
            return ngr, ngi, ar, ai

        init = (stg_ref[:, cr], stg_ref[:, ci], acc_ref[:, cr], acc_ref[:, ci])
        gr, gi, ar, ai = lax.fori_loop(0, n_groups, step, init)
        stg_ref[:, cr] = gr
        stg_ref[:, ci] = gi
        acc_ref[:, cr] = ar
        acc_ref[:, ci] = ai


def _chunk_starts(st_ref, init_ref, lcl_ref, s, n_blocks):
    w = s // n_blocks
    init_ref[0:1, :] = jnp.zeros((1, 2 * s), F32)
    for re0, im0 in _state_blocks(2 * s, n_blocks, w):
        re, im = pl.ds(re0, w), pl.ds(im0, w)
        ar, ai = lcl_ref[:, re], lcl_ref[:, im]
        cr = jnp.zeros((1, w), F32)
        ci = jnp.zeros((1, w), F32)
        for k in range(1, NCH):
            cr, ci = (ar * cr - ai * ci + st_ref[k - 1 : k, re], ar * ci + ai * cr + st_ref[k - 1 : k, im])
            init_ref[k : k + 1, re] = cr
            init_ref[k : k + 1, im] = ci


def _chunk_starts_adjoint(stg_ref, initg_ref, lcl_ref, s, n_blocks):
    w = s // n_blocks
    initg_ref[NCH - 1 : NCH, :] = jnp.zeros((1, 2 * s), F32)
    for re0, im0 in _state_blocks(2 * s, n_blocks, w):
        re, im = pl.ds(re0, w), pl.ds(im0, w)
        ar, ai = lcl_ref[:, re], -lcl_ref[:, im]
        gr = jnp.zeros((1, w), F32)
        gi = jnp.zeros((1, w), F32)
        for k in range(NCH - 2, -1, -1):
            gr, gi = (stg_ref[k + 1 : k + 2, re] + ar * gr - ai * gi, stg_ref[k + 1 : k + 2, im] + ar * gi + ai * gr)
            initg_ref[k : k + 1, re] = gr
            initg_ref[k : k + 1, im] = gi


def _ssm_rows(seq, want):
    rows = min(want, seq // 2)
    assert seq % rows == 0 and rows % SUBLANES == 0
    return rows


def _chunk_copies(hbm_ref, b, cm_ref, sems, to_cm, col0=0):
    cl, _, width = cm_ref.shape
    copies = []
    for k in range(NCH):
        nat, cm = hbm_ref.at[b, pl.ds(k * cl, cl), pl.ds(col0, width)], cm_ref.at[:, k, :]
        src, dst = (nat, cm) if to_cm else (cm, nat)
        copies.append(pltpu.make_async_copy(src, dst, sems.at[k]))
    return copies


def _blockwise(fn, n_blocks):
    return jnp.concatenate([fn(b) for b in range(n_blocks)], axis=1)


def _ssm_fwd(u, u_col, b_tab, c_tab, lam8, lcl, d_skip, w_glu, shards):
    batch, seq, _ = u.shape
    ns = len(shards)
    width = d_skip.shape[1]
    s = lam8.shape[1] // 2
    nb = b_tab.shape[0]
    sb = 2 * s // nb
    cl = seq // NCH
    rows = _ssm_rows(seq, 1024)
    n_tiles = seq // rows
    n_groups = rows // SUBLANES

    def body(u_hbm, b_ref, c_ref, lam_ref, lcl_ref, d_ref, wg_ref, *rest):
        shard_hbm, (y_hbm, pre_ref, z_ref, init_ref) = rest[:ns], rest[ns : ns + 4]
        gathered = rest[ns + 4 : 2 * ns + 4]
        u_cm, y_cm, bu_all, st, sems, send_sems, recv_sems, own_sems = rest[2 * ns + 4 :]
        b, ph, t = pl.program_id(0), pl.program_id(1), pl.program_id(2)
        tile_groups = pl.ds(pl.multiple_of(t * n_groups, n_groups), n_groups)
        x_pos, y_pos, c_pos = _place()
        own = [pltpu.make_async_copy(shard_hbm[a], gathered[a].at[4 * x_pos + 2 * y_pos + c_pos], own_sems.at[a])
               for a in range(ns)]
        exchange_start, exchange_finish = _exchange_steps(
            ns, lambda a, slot: shard_hbm[a], gathered, send_sems, recv_sems)

        @pl.when((b == 0) & (ph == 0) & (t == 0))
        def _():
            exchange_start()
            for cp in own:
                cp.start()

        @pl.when((b == batch - 1) & (ph == 1) & (t == n_tiles - 1))
        def _():
            exchange_finish()
            for cp in own:
                cp.wait()

        @pl.when((ph == 0) & (t == 0))
        def _():
            loads = _chunk_copies(u_hbm, b, u_cm, sems, True, u_col)
            for cp in loads:
                cp.start()
            st[...] = jnp.zeros_like(st)
            for cp in loads:
                cp.wait()

        @pl.when((ph == 1) & (t == 0))
        def _():
            st[...] = init_ref[...]

        u_t = u_cm[tile_groups].reshape(rows, width)
        bu = bu_all.at[pl.ds(pl.multiple_of(t * rows, rows), rows)]

        @pl.when(ph == 0)
        def _():
            u_b = u_t.astype(BF16)
            for blk in range(nb):
                bu[:, blk * sb : (blk + 1) * sb] = _mm(u_b[:, blk * LANES : (blk + 1) * LANES], b_ref[blk])
            _scan(bu, bu, st, lam_ref, n_groups, s, nb, False, False)

        @pl.when((ph == 0) & (t == n_tiles - 1))
        def _():
            _chunk_starts(st, init_ref, lcl_ref, s, nb)

        @pl.when(ph == 1)
        def _():
            _scan(bu, bu, st, lam_ref, n_groups, s, nb, False, True)
            hs = lambda blk: _mm(bu[:, blk * sb : (blk + 1) * sb].astype(BF16), c_ref[blk])
            pre = _blockwise(hs, nb) + d_ref[...] * u_t
            z = _mm(_gelu(pre).astype(BF16), wg_ref[...])
            pre_ref[...] = pre
            z_ref[...] = z
            y = z[:, 0:width] * jax.nn.sigmoid(z[:, width : 2 * width])
            y_cm[tile_groups] = y.reshape(n_groups, SUBLANES, width)

        @pl.when((ph == 1) & (t == n_tiles - 1))
        def _():
            stores = _chunk_copies(y_hbm, b, y_cm, sems, False)
            for cp in stores:
                cp.start()
            for cp in stores:
                cp.wait()

    out_tile = lambda b, ph, t: (b, t * ph, 0)
    full = lambda a: pl.BlockSpec(a.shape, lambda b, ph, t: (0,) * a.ndim)
    hbm = pl.BlockSpec(memory_space=pl.ANY)
    res = pl.pallas_call(
        body,
        name="ssm_fwd",
        grid=(batch, 2, n_tiles),
        in_specs=[hbm, full(b_tab), full(c_tab), full(lam8), full(lcl), full(d_skip), full(w_glu)] + [hbm] * ns,
        out_specs=[
            hbm,
            pl.BlockSpec((None, rows, width), out_tile),
            pl.BlockSpec((None, rows, 2 * width), out_tile),
            pl.BlockSpec((None, SUBLANES, 2 * s), lambda b, ph, t: (b, 0, 0)),
        ]
        + [hbm] * ns,
        out_shape=[
            jax.ShapeDtypeStruct((batch, seq, width), F32),
            jax.ShapeDtypeStruct((batch, seq, width), F32),
            jax.ShapeDtypeStruct((batch, seq, 2 * width), F32),
            jax.ShapeDtypeStruct((batch, SUBLANES, 2 * s), F32),
        ]
        + [jax.ShapeDtypeStruct((N_DEV, *a.shape), a.dtype) for a in shards],
        scratch_shapes=[
            pltpu.VMEM((cl, NCH, width), F32),
            pltpu.VMEM((cl, NCH, width), F32),
            pltpu.VMEM((seq, 2 * s), F32),
            pltpu.VMEM((SUBLANES, 2 * s), F32),
            pltpu.SemaphoreType.DMA((NCH,)),
            pltpu.SemaphoreType.DMA((ns, 7)),
            pltpu.SemaphoreType.DMA((ns, 7)),
            pltpu.SemaphoreType.DMA((ns,)),
        ],
        compiler_params=_params("arbitrary", "arbitrary", "arbitrary"),
    )(u, b_tab, c_tab, lam8, lcl, d_skip, w_glu, *shards)
    return res[:4], res[4:]


def _ssm_bwd(u, u_col, pre_p, z_p, dy, init, b_tab, b_tab_t, c_tab_t, lam8, lcl, d_skip, w_glu, ready):
    batch, seq, _ = u.shape
    width = d_skip.shape[1]
    nr = len(ready)
    s = lam8.shape[1] // 2
    nb = b_tab.shape[0]
    sb = 2 * s // nb
    cl = seq // NCH
    rows = _ssm_rows(seq, 512)
    n_tiles = seq // rows
    n_groups = rows // SUBLANES

    def body(u_hbm, pre_ref, z_ref, dy_hbm, init_ref, b_ref, bt_ref, ct_ref, lam_ref, lcl_ref, d_ref, wg_ref, *rest):
        ready_hbm, rest = rest[:nr], rest[nr:]
        du_hbm, gb_ref, gc_ref, gwg_ref, gd_ref, glam_ref = rest[:6]
        landed_hbm, rest = rest[6 : 6 + nr], rest[6 + nr :]
        u_cm, dy_cm, work, hs_all, dpre_all, st, stg, initg, acc, sems, send_sems, recv_sems = rest
        b, ph, t = pl.program_id(0), pl.program_id(1), pl.program_id(2)
        half = s // nb
        exchange_start, exchange_finish = _exchange_steps(
            nr, lambda a, slot: ready_hbm[a].at[slot], landed_hbm, send_sems, recv_sems)
        first = (b == 0) & (ph == 0) & (t == 0)
        last = (b == batch - 1) & (ph == 2) & (t == n_tiles - 1)
        tile = jnp.where(ph == 0, t, n_tiles - 1 - t)
        tile_rows = pl.ds(pl.multiple_of(tile * rows, rows), rows)
        tile_groups = pl.ds(pl.multiple_of(tile * n_groups, n_groups), n_groups)
        lanes = lambda blk: slice(blk * LANES, (blk + 1) * LANES)
        states = lambda blk: slice(blk * sb, (blk + 1) * sb)

        @pl.when(first)
        def _():
            exchange_start()
            acc[...] = jnp.zeros_like(acc)
            gb_ref[...] = jnp.zeros_like(gb_ref)
            gc_ref[...] = jnp.zeros_like(gc_ref)
            gwg_ref[...] = jnp.zeros_like(gwg_ref)
            gd_ref[...] = jnp.zeros_like(gd_ref)

        @pl.when((ph == 0) & (t == 0))
        def _():
            loads = (_chunk_copies(u_hbm, b, u_cm, sems.at[0], True, u_col)
                     + _chunk_copies(dy_hbm, b, dy_cm, sems.at[1], True))
            for cp in loads:
                cp.start()
            st[...] = init_ref[...]
            for blk in range(nb):
                entry = init_ref[:, states(blk)]
                hs_all[0:SUBLANES, blk * half : (blk + 1) * half] = _pack_state(entry[:, 0:half], entry[:, half : 2 * half])
            for cp in loads:
                cp.wait()

        u_t = u_cm[tile_groups].reshape(rows, width)
        u_b = u_t.astype(BF16)

        @pl.when(ph == 0)
        def _():
            for blk in range(nb):
                work[:, states(blk)] = _mm(u_b[:, lanes(blk)], b_ref[blk])
            _scan(work, work, st, lam_ref, n_groups, s, nb, False, True)
            z = z_ref[...]
            dy_t = dy_cm[tile_groups].reshape(rows, width)
            pre = pre_ref[...]
            z1, sig = z[:, 0:width], jax.nn.sigmoid(z[:, width : 2 * width])
            dz = jnp.concatenate([dy_t * sig, dy_t * z1 * sig * (1.0 - sig)], axis=1).astype(BF16)
            gwg_ref[...] += _mm_tn(_gelu(pre).astype(BF16), dz)
            dpre = _mm_nt(dz, wg_ref[...]) * _gelu_grad(pre)
            dpre_all[tile_rows, :] = dpre
            gd_ref[...] += jnp.sum(dpre * u_t, axis=0, keepdims=True)
            dpre_b = dpre.astype(BF16)
            kept = pl.ds(pl.multiple_of(tile * rows + SUBLANES, SUBLANES), rows)
            for blk in range(nb):
                hs = work[:, states(blk)]
                gc_ref[blk] += _mm_tn(hs.astype(BF16), dpre_b[:, lanes(blk)])
                hs_all[kept, blk * half : (blk + 1) * half] = _pack_state(hs[:, 0:half], hs[:, half : 2 * half])

        @pl.when(ph >= 1)
        def _():
            dpre_b = dpre_all[tile_rows, :].astype(BF16)
            for blk in range(nb):
                work[:, states(blk)] = _mm(dpre_b[:, lanes(blk)], ct_ref[blk])

        @pl.when(ph == 1)
        def _():
            @pl.when(t == 0)
            def _():
                stg[...] = jnp.zeros_like(stg)

            _scan(work, work, stg, lam_ref, n_groups, s, nb, True, False)

            @pl.when(t == n_tiles - 1)
            def _():
                _chunk_starts_adjoint(stg, initg, lcl_ref, s, nb)

        @pl.when(ph == 2)
        def _():
            @pl.when(t == 0)
            def _():
                stg[...] = initg[...]

            _scan_adjoint(work, hs_all, tile * n_groups, stg, acc, lam_ref, n_groups, s, nb)
            du = lambda blk: _mm(work[:, states(blk)].astype(BF16), bt_ref[blk])
            du_t = _blockwise(du, nb) + dpre_all[tile_rows, :] * d_ref[...]
            dy_cm[tile_groups] = du_t.reshape(n_groups, SUBLANES, width)
            for blk in range(nb):
                gb_ref[blk] += _mm_tn(u_b[:, lanes(blk)], work[:, states(blk)].astype(BF16))

            @pl.when(t == n_tiles - 1)
            def _():
                stores = _chunk_copies(du_hbm, b, dy_cm, sems.at[0], False)
                for cp in stores:
                    cp.start()
                for cp in stores:
                    cp.wait()

        @pl.when(last)
        def _():
            glam_ref[...] = jnp.sum(acc[...], axis=0, keepdims=True)
            exchange_finish()

    def tile(b, ph, t):
        return (b, jnp.where(ph == 0, t, n_tiles - 1 - t), 0)

    full = lambda a: pl.BlockSpec(a.shape, lambda b, ph, t: (0,) * a.ndim)
    hbm = pl.BlockSpec(memory_space=pl.ANY)
    res = pl.pallas_call(
        body,
        name="ssm_bwd",
        grid=(batch, 3, n_tiles),
        in_specs=[
            hbm,
            pl.BlockSpec((None, rows, width), tile),
            pl.BlockSpec((None, rows, 2 * width), tile),
            hbm,
            pl.BlockSpec((None, SUBLANES, 2 * s), lambda b, ph, t: (b, 0, 0)),
            full(b_tab), full(b_tab_t), full(c_tab_t), full(lam8), full(lcl), full(d_skip), full(w_glu),
        ]
        + [hbm] * nr,
        out_specs=[
            hbm,
            full(b_tab), full(b_tab_t), full(w_glu), full(d_skip),
            pl.BlockSpec((1, 2 * s), lambda b, ph, t: (0, 0)),
        ]
        + [hbm] * nr,
        out_shape=[
            jax.ShapeDtypeStruct((batch, seq, width), F32),
            jax.ShapeDtypeStruct(b_tab.shape, F32),
            jax.ShapeDtypeStruct(b_tab_t.shape, F32),
            jax.ShapeDtypeStruct(w_glu.shape, F32),
            jax.ShapeDtypeStruct(d_skip.shape, F32),
            jax.ShapeDtypeStruct((1, 2 * s), F32),
        ]
        + [jax.ShapeDtypeStruct(a.shape, F32) for a in ready],
        scratch_shapes=[
            pltpu.VMEM((cl, NCH, width), F32),
            pltpu.VMEM((cl, NCH, width), F32),
            pltpu.VMEM((rows, 2 * s), F32),
            pltpu.VMEM((seq + SUBLANES, s), jnp.uint32),
            pltpu.VMEM((seq, width), F32),
        ]
        + [pltpu.VMEM((SUBLANES, 2 * s), F32)] * 4
        + [pltpu.SemaphoreType.DMA((2, NCH)), pltpu.SemaphoreType.DMA((nr, 7)), pltpu.SemaphoreType.DMA((nr, 7))],
        compiler_params=_params("arbitrary", "arbitrary", "arbitrary"),
    )(u, pre_p, z_p, dy, init, b_tab, b_tab_t, c_tab_t, lam8, lcl, d_skip, w_glu, *ready)
    return res[:6], res[6:]


def _kv_fwd(mem, g_mem, w_kv):
    batch, n_mem, d = mem.shape
    kvw = w_kv.shape[1]

    def body(mem_ref, g_ref, w_ref, kv_ref):
        m = mem_ref[...]
        kv_ref[...] = _mm((m * _rms(m) * g_ref[...]).astype(BF16), w_ref[...])

    return pl.pallas_call(
        body,
        name="kv_fwd",
        grid=(batch,),
        in_specs=[
            pl.BlockSpec((None, n_mem, d), lambda b: (b, 0, 0)),
            pl.BlockSpec((1, d), lambda b: (0, 0)),
            pl.BlockSpec((d, kvw), lambda b: (0, 0)),
        ],
        out_specs=pl.BlockSpec((None, n_mem, kvw), lambda b: (b, 0, 0)),
        out_shape=jax.ShapeDtypeStruct((batch, n_mem, kvw), F32),
        compiler_params=_params("arbitrary"),
    )(mem, g_mem, w_kv)


def _kv_bwd(mem, dkv, g_mem, w_kv):
    batch, n_mem, d = mem.shape
    kvw = w_kv.shape[1]

    def body(mem_ref, dkv_ref, g_ref, w_ref, gw_ref, gg_ref):
        @pl.when(pl.program_id(0) == 0)
        def _():
            gw_ref[...] = jnp.zeros_like(gw_ref)
            gg_ref[...] = jnp.zeros_like(gg_ref)

        m = mem_ref[...]
        mn = m * _rms(m)
        dkv_b = dkv_ref[...].astype(BF16)
        gw_ref[...] += _mm_tn((mn * g_ref[...]).astype(BF16), dkv_b)
        gg_ref[...] += jnp.sum(_mm_nt(dkv_b, w_ref[...]) * mn, axis=0, keepdims=True)

    return pl.pallas_call(
        body,
        name="kv_bwd",
        grid=(batch,),
        in_specs=[
            pl.BlockSpec((None, n_mem, d), lambda b: (b, 0, 0)),
            pl.BlockSpec((None, n_mem, kvw), lambda b: (b, 0, 0)),
            pl.BlockSpec((1, d), lambda b: (0, 0)),
            pl.BlockSpec((d, kvw), lambda b: (0, 0)),
        ],
        out_specs=[pl.BlockSpec((d, kvw), lambda b: (0, 0)), pl.BlockSpec((1, d), lambda b: (0, 0))],
        out_shape=[jax.ShapeDtypeStruct((d, kvw), F32), jax.ShapeDtypeStruct((1, d), F32)],
        compiler_params=_params("arbitrary"),
    )(mem, dkv, g_mem, w_kv)


def _tail(x2, target2, proj, y_pool, y_ssm, kv, w_out, g_post):
    tokens, d = x2.shape
    batch, n_mem, kvw = kv.shape
    pool_w, ssm_w, att_w, mix = y_pool.shape[1], y_ssm.shape[1], kvw // 2, w_out.shape[0]
    assert (mix - att_w) % att_w == 0 and proj.shape[1] == 2 * mix
    hd = att_w // MEM_HEADS
    cl = _token_tile(tokens // batch, 512)
    n_tiles = tokens // cl
    per_seq = tokens // batch // cl
    qk_scale = hd**-0.5

    def body(x_ref, tg_ref, gate_ref, yp_ref, ys_ref, q_ref, kv_ref, w_ref, g_ref,
             dres_ref, dgate_ref, dyp_ref, dys_ref, dq_ref, dkv_ref, gw_hbm, gg_ref, loss_ref, acc, sem):
        i = pl.program_id(0)

        @pl.when(i == 0)
        def _():
            acc[...] = jnp.zeros_like(acc)
            gg_ref[...] = jnp.zeros_like(gg_ref)
            loss_ref[...] = jnp.zeros_like(loss_ref)

        @pl.when(i % per_seq == 0)
        def _():
            dkv_ref[...] = jnp.zeros_like(dkv_ref)

        k = kv_ref[:, 0:att_w].astype(BF16)
        v = kv_ref[:, att_w : 2 * att_w].astype(BF16)
        lane = lax.broadcasted_iota(jnp.int32, (1, att_w), 1)
        heads = [(lane >= h * hd) & (lane < (h + 1) * hd) for h in range(MEM_HEADS)]
        g = g_ref[...]

        def part(rows):
            n_rows = rows.stop - rows.start
            q = q_ref[rows, :]
            probs, q_heads = [], []
            att = jnp.zeros((n_rows, att_w), F32)
            for mask in heads:
                qh = jnp.where(mask, q, 0.0).astype(BF16)
                sc = _mm_nt(qh, k) * qk_scale
                e = jnp.exp(sc - jnp.max(sc, axis=-1, keepdims=True))
                p = e * (1.0 / jnp.sum(e, axis=-1, keepdims=True))
                att = att + jnp.where(mask, _mm(p.astype(BF16), v), 0.0)
                probs.append(p)
                q_heads.append(qh)

            ycat = jnp.concatenate([yp_ref[rows, :], ys_ref[rows, :], att], axis=1)
            gate = gate_ref[rows, :]
            sig = jax.nn.sigmoid(gate)
            silu = gate * sig
            yg = (ycat * silu).astype(BF16)
            out = _mm(yg, w_ref[...])
            r = _rms(out)
            on = out * r
            err = x_ref[rows, :] + on * g - tg_ref[rows, :]
            loss_ref[...] += 0.5 * jnp.sum(jnp.mean(err * err, axis=-1, keepdims=True), axis=0, keepdims=True)
            dres = err * (1.0 / d)
            dres_ref[rows, :] = dres
            gg_ref[...] += jnp.sum(dres * on, axis=0, keepdims=True)
            don = dres * g
            dout = (r * (don - on * jnp.mean(don * on, axis=-1, keepdims=True))).astype(BF16)
            acc[...] += _mm_tn(yg, dout)
            dyg = _mm_nt(dout, w_ref[...])
            dgate_ref[rows, :] = dyg * ycat * (sig * (1.0 + gate * (1.0 - sig)))
            dycat = dyg * silu
            dyp_ref[rows, :] = dycat[:, 0:pool_w]
            dys_ref[rows, :] = dycat[:, pool_w : pool_w + ssm_w]
            datt = dycat[:, pool_w + ssm_w : mix]

            dq = jnp.zeros((n_rows, att_w), F32)
            dk = jnp.zeros((n_mem, att_w), F32)
            dv = jnp.zeros((n_mem, att_w), F32)
            for mask, p, qh in zip(heads, probs, q_heads):
                doh = jnp.where(mask, datt, 0.0).astype(BF16)
                dp = _mm_nt(doh, v)
                ds = (p * (dp - jnp.sum(p * dp, axis=-1, keepdims=True)) * qk_scale).astype(BF16)
                dq = dq + jnp.where(mask, _mm(ds, k), 0.0)
                dk = dk + _mm_tn(ds, qh)
                dv = dv + _mm_tn(p.astype(BF16), doh)
            dq_ref[rows, :] = dq
            dkv_ref[:, 0:att_w] += dk
            dkv_ref[:, att_w : 2 * att_w] += dv

        part(slice(0, cl))

        @pl.when(i == n_tiles - 1)
        def _():
            cp = pltpu.make_async_copy(acc, gw_hbm, sem)
            cp.start()
            cp.wait()

    tok = lambda w: pl.BlockSpec((cl, w), lambda i: (i, 0))
    chunked = tok(ssm_w)
    per_batch = pl.BlockSpec((None, n_mem, kvw), lambda i: (i // per_seq, 0, 0))
    return pl.pallas_call(
        body,
        name="tail",
        grid=(n_tiles,),
        in_specs=[
            tok(d), tok(d), pl.BlockSpec((cl, mix), lambda i: (i, 1)), tok(pool_w), chunked,
            pl.BlockSpec((cl, att_w), lambda i: (i, (mix - att_w) // att_w)), per_batch,
            pl.BlockSpec((mix, d), lambda i: (0, 0)),
            pl.BlockSpec((1, d), lambda i: (0, 0)),
        ],
        out_specs=[
            tok(d), tok(mix), tok(pool_w), chunked, tok(att_w), per_batch,
            pl.BlockSpec(memory_space=pl.ANY),
            pl.BlockSpec((1, d), lambda i: (0, 0)),
            pl.BlockSpec((1, 1), lambda i: (0, 0)),
        ],
        out_shape=[
            jax.ShapeDtypeStruct((tokens, d), F32),
            jax.ShapeDtypeStruct((tokens, mix), F32),
            jax.ShapeDtypeStruct((tokens, pool_w), F32),
            jax.ShapeDtypeStruct((tokens, ssm_w), F32),
            jax.ShapeDtypeStruct((tokens, att_w), F32),
            jax.ShapeDtypeStruct(kv.shape, F32),
            jax.ShapeDtypeStruct((mix, d), F32),
            jax.ShapeDtypeStruct((1, d), F32),
            jax.ShapeDtypeStruct((1, 1), F32),
        ],
        scratch_shapes=[pltpu.VMEM((mix, d), F32), pltpu.SemaphoreType.DMA],
        compiler_params=_params("arbitrary"),
    )(x2, target2, proj, y_pool, y_ssm, proj, kv, w_out, g_post)


def _pack(arrays):
    flat = jnp.concatenate([a.reshape(-1) for a in arrays])
    rows = -(-flat.size // (SUBLANES * LANES)) * SUBLANES
    return jnp.pad(flat, (0, rows * LANES - flat.size)).reshape(rows, LANES)


def _unpack(packed, like):
    flat, out, at = packed.reshape(-1), [], 0
    for a in like:
        out.append(flat[at : at + a.size].reshape(a.shape))
        at += a.size
    return out


def kernel(x, mem, g_pre, w_in, w_pool, pool_scale, a_re, a_im, log_dt, b_re, b_im, c_re, c_im, d_skip, w_glu, g_mem, w_kv, w_out, g_post, loss_target, m_g_pre, m_w_in, m_w_pool, m_pool_scale, m_a_re, m_a_im, m_log_dt, m_b_re, m_b_im, m_c_re, m_c_im, m_d_skip, m_w_glu, m_g_mem, m_w_kv, m_w_out, m_g_post, v_g_pre, v_w_in, v_w_pool, v_pool_scale, v_a_re, v_a_im, v_log_dt, v_b_re, v_b_im, v_c_re, v_c_im, v_d_skip, v_w_glu, v_g_mem, v_w_kv, v_w_out, v_g_post):
    batch, seq, d = x.shape
    cl = seq // NCH
    pool_w, ssm_w = pool_scale.shape[1], d_skip.shape[1]
    att_w = w_kv.shape[2] // 2
    tokens = batch * seq
    x2 = x.reshape(tokens, d)
    target2 = loss_target.reshape(tokens, d)

    wp_blk = jax.scipy.linalg.block_diag(*w_pool[0]).astype(BF16)
    ssm_params = (a_re[0], a_im[0], log_dt[0], b_re[0], b_im[0], c_re[0], c_im[0])
    (lam_row, b_tab, c_tab), tables_vjp = jax.vjp(_ssm_tables, *ssm_params)
    nb = b_tab.shape[0]
    lam8 = jnp.broadcast_to(lam_row, (SUBLANES, lam_row.shape[1]))
    lcl = _lam_power(a_re[0], a_im[0], log_dt[0], float(cl), 1.0, nb)
    b_bf, c_bf = b_tab.astype(BF16), c_tab.astype(BF16)

    proj, w_in_g, (w_glu_g,) = _in_proj(x2, g_pre, w_in[0], [w_glu[0].astype(BF16)])
    proj3 = proj.reshape(batch, seq, proj.shape[1])
    w_glu_f = w_glu_g.transpose(1, 0, 2).reshape(w_glu_g.shape[1], N_DEV * w_glu_g.shape[2])
    y_pool = _pool_fwd(proj, wp_blk, pool_scale, batch, seq)
    (y_ssm, pre_ssm, z_ssm, init_ssm), (w_out_g, w_kv_g) = _ssm_fwd(
        proj3, pool_w, b_bf, c_bf, lam8, lcl, d_skip, w_glu_f, [w_out[0].astype(BF16), w_kv[0].astype(BF16)])
    w_out_f = w_out_g.reshape(N_DEV * w_out_g.shape[1], w_out_g.shape[2])
    w_kv_f = w_kv_g.reshape(N_DEV * w_kv_g.shape[1], w_kv_g.shape[2])
    kv = _kv_fwd(mem, g_mem, w_kv_f)

    dres, dgate, dy_pool, dy_ssm, dq, dkv, gw_out, gg_post, loss_part = _tail(
        x2, target2, proj, y_pool, y_ssm.reshape(tokens, ssm_w), kv, w_out_f, g_post)

    gw_kv, gg_mem = _kv_bwd(mem, dkv, g_mem, w_kv_f)
    du_pool, gwp_dense, g_scale = _pool_bwd(proj, dy_pool, wp_blk, pool_scale, batch, seq)
    gw_kv8 = gw_kv.reshape(N_DEV, -1, gw_kv.shape[1])
    gw_out8 = gw_out.reshape(N_DEV, -1, gw_out.shape[1])
    (du_ssm, gb_tab, gc_tab, gw_glu, gd_skip, glam), (kv_landed, out_landed) = _ssm_bwd(
        proj3, pool_w, pre_ssm, z_ssm, dy_ssm.reshape(batch, seq, ssm_w), init_ssm, b_bf, b_bf.transpose(0, 2, 1),
        c_bf.transpose(0, 2, 1), lam8, lcl, d_skip, w_glu_f, [gw_kv8, gw_out8])
    grad_x2, gw_in, gg_pre = _in_proj_bwd(
        x2, dres, du_pool, du_ssm.reshape(tokens, ssm_w), dq, dgate, g_pre,
        w_in_g.transpose(0, 2, 1).reshape(-1, d))

    gw = pool_w // len(POOL_WINDOWS)
    gw_pool = jnp.stack([gwp_dense[i * gw : (i + 1) * gw, i * gw : (i + 1) * gw] for i in range(len(POOL_WINDOWS))])
    g_ssm = tables_vjp((glam, gb_tab, gc_tab))

    small_w = [g_pre, w_pool, pool_scale, a_re, a_im, log_dt, b_re, b_im, c_re, c_im, d_skip, g_mem, g_post]
    small_m = [m_g_pre, m_w_pool, m_pool_scale, m_a_re, m_a_im, m_log_dt, m_b_re, m_b_im, m_c_re, m_c_im, m_d_skip, m_g_mem, m_g_post]
    small_v = [v_g_pre, v_w_pool, v_pool_scale, v_a_re, v_a_im, v_log_dt, v_b_re, v_b_im, v_c_re, v_c_im, v_d_skip, v_g_mem, v_g_post]
    small_g = [gg_pre, gw_pool, g_scale, *g_ssm, gd_skip, gg_mem, gg_post]
    big_g, small_sum = _reduce_all(
        [gw_in, gw_glu.reshape(ssm_w, N_DEV, -1).transpose(1, 0, 2)],
        _pack(small_g + [loss_part]),
        [(gw_kv8, kv_landed), (gw_out8, out_landed)])

    flat2 = lambda a: a.reshape(-1, a.shape[-1])
    sg = _unpack(small_sum, [flat2(a) for a in small_w] + [loss_part])
    loss = sg[-1].reshape(())
    small_names = ["g_pre", "w_pool", "pool_scale", "a_re", "a_im", "log_dt", "b_re", "b_im", "c_re", "c_im",
                   "d_skip", "g_mem", "g_post"]
    names = ["w_in", "w_glu", "w_kv", "w_out"] + small_names
    all_w = [w_in, w_glu, w_kv, w_out] + small_w
    all_m = [m_w_in, m_w_glu, m_w_kv, m_w_out] + small_m
    all_v = [v_w_in, v_w_glu, v_w_kv, v_w_out] + small_v
    updates = _adamw_all(big_g + sg[:-1], [flat2(a) for a in all_w], [flat2(a) for a in all_m], [flat2(a) for a in all_v])
    updates = {name: [t.reshape(a.shape) for t in u] for name, u, a in zip(names, updates, all_w)}

    order = ["g_pre", "w_in", "w_pool", "pool_scale", "a_re", "a_im", "log_dt", "b_re", "b_im", "c_re", "c_im",
             "d_skip", "w_glu", "g_mem", "w_kv", "w_out", "g_post"]
    outs = [[updates[name][kind] for name in order] for kind in range(4)]
    return (loss, grad_x2.reshape(batch, seq, d), *outs[0], *outs[1], *outs[2], *outs[3])
```

```python
import functools
import math

import jax
import jax.numpy as jnp
from jax import lax
from jax.experimental import pallas as pl
from jax.experimental.pallas import tpu as pltpu

F32 = jnp.float32
BF16 = jnp.bfloat16
MESH = pl.DeviceIdType.MESH

N_DEV = 8
NCH = 8
SUBLANES = 8
LANES = 128
VMEM_LIMIT = 60 * 1024 * 1024

EPS = 1e-6
POOL_WINDOWS = (2, 4, 8, 16)
MEM_HEADS = 4
SSM_GROUP = 16
SSM_N = 64
ADAM_LR, ADAM_B1, ADAM_B2, ADAM_EPS, ADAM_WD, ADAM_STEP = 0.001, 0.9, 0.999, 1e-08, 0.01, 10


def _mm(a, b):
    return jnp.dot(a, b, preferred_element_type=F32)


def _mm_nt(a, b):
    return lax.dot_general(a, b, (((1,), (1,)), ((), ())), preferred_element_type=F32)


def _mm_tn(a, b):
    return lax.dot_general(a, b, (((0,), (0,)), ((), ())), preferred_element_type=F32)


def _params(*sem):
    return pltpu.CompilerParams(dimension_semantics=sem or None, vmem_limit_bytes=VMEM_LIMIT)


def _adamw(w, g, m, v):
    m = ADAM_B1 * m + (1.0 - ADAM_B1) * g
    v = ADAM_B2 * v + (1.0 - ADAM_B2) * (g * g)
    m_hat = m / (1.0 - ADAM_B1**ADAM_STEP)
    v_hat = v / (1.0 - ADAM_B2**ADAM_STEP)
    delta = -ADAM_LR * (m_hat / (jnp.sqrt(v_hat) + ADAM_EPS) + ADAM_WD * w)
    return delta, m, v


def _gelu(x):
    k = math.sqrt(2.0 / math.pi)
    return 0.5 * x * (1.0 + jnp.tanh(k * (x + 0.044715 * x * x * x)))


def _gelu_grad(x):
    k = math.sqrt(2.0 / math.pi)
    th = jnp.tanh(k * (x + 0.044715 * x * x * x))
    return 0.5 * (1.0 + th) + 0.5 * x * (1.0 - th * th) * (k * (1.0 + 3.0 * 0.044715 * x * x))


def _place():
    return lax.axis_index("x"), lax.axis_index("y"), lax.axis_index("c")


def _gather_steps(ins, outs, send_sems, recv_sems):
    n = len(ins)
    x, y, c = _place()
    me, sibling = (x, y, c), (x, y, 1 - c)
    chips = [(1 - x, y), (x, 1 - y), (1 - x, 1 - y)]
    sent = []

    def slot(px, py, pc):
        return 4 * px + 2 * py + pc

    def copy(a, k, block, to):
        ref = outs[a].at[slot(*block)]
        return pltpu.make_async_remote_copy(
            src_ref=ref, dst_ref=ref, send_sem=send_sems.at[a, k], recv_sem=recv_sems.at[a, k],
            device_id=to, device_id_type=MESH)

    def start():
        for a in range(n):
            outs[a][slot(*me)] = ins[a][...].astype(outs[a].dtype)
        for a in range(n):
            sent.append(copy(a, 0, me, sibling))
            sent.extend(copy(a, 1 + j, me, (*chip, c)) for j, chip in enumerate(chips))
        for cp in sent:
            cp.start()

    def forward():
        for j, chip in enumerate(chips):
            for a in range(n):
                copy(a, 1 + j, (*chip, c), me).wait_recv()
                cp = copy(a, 4 + j, (*chip, c), sibling)
                cp.start()
                sent.append(cp)

    def finish():
        for a in range(n):
            copy(a, 0, sibling, me).wait_recv()
            for j, chip in enumerate(chips):
                copy(a, 4 + j, (*chip, 1 - c), me).wait_recv()
        for cp in sent:
            cp.wait_send()

    return start, forward, finish


def _exchange_steps(n, src_of, landing, send_sems, recv_sems):
    x, y, c = _place()
    me = 4 * x + 2 * y + c
    peers = []
    for j in range(1, N_DEV):
        px = 1 - x if j & 4 else x
        py = 1 - y if j & 2 else y
        pc = 1 - c if j & 1 else c
        peers.append((px, py, pc))

    def copy(a, j, from_slot, to_slot, peer):
        return pltpu.make_async_remote_copy(
            src_ref=src_of(a, to_slot), dst_ref=landing[a].at[from_slot],
            send_sem=send_sems.at[a, j], recv_sem=recv_sems.at[a, j], device_id=peer, device_id_type=MESH)

    def start():
        for a in range(n):
            for j, p in enumerate(peers):
                copy(a, j, me, 4 * p[0] + 2 * p[1] + p[2], p).start()

    def finish():
        for a in range(n):
            for j, p in enumerate(peers):
                slot = 4 * p[0] + 2 * p[1] + p[2]
                copy(a, j, slot, slot, p).wait_recv()
        for a in range(n):
            for j, p in enumerate(peers):
                copy(a, j, me, 4 * p[0] + 2 * p[1] + p[2], p).wait_send()

    return start, finish


def _ordered_sum(gathered, out_ref):
    rows = out_ref.shape[0]

    def step(i, _):
        r = pl.ds(pl.multiple_of(i * SUBLANES, SUBLANES), SUBLANES)
        g = gathered[0, r, :]
        for d in range(1, N_DEV):
            g = g + gathered[d, r, :]
        out_ref[r, :] = g
        return 0

    lax.fori_loop(0, rows // SUBLANES, step, 0)


def _adamw_all(gs, ws, ms, vs):
    n = len(gs)

    def body(*refs):
        g, w, m, v = refs[:n], refs[n : 2 * n], refs[2 * n : 3 * n], refs[3 * n : 4 * n]
        outs = refs[4 * n :]
        for a in range(n):
            rows = g[a].shape[0]
            chunk = math.gcd(rows, 128)

            def step(i, _, a=a, chunk=chunk):
                r = pl.ds(pl.multiple_of(i * chunk, chunk), chunk)
                grad = g[a][r, :]
                delta, nm, nv = _adamw(w[a][r, :], grad, m[a][r, :], v[a][r, :])
                outs[4 * a][r, :] = grad
                outs[4 * a + 1][r, :] = delta
                outs[4 * a + 2][r, :] = nm
                outs[4 * a + 3][r, :] = nv
                return 0

            lax.fori_loop(0, rows // chunk, step, 0)

    vmem = pl.BlockSpec(memory_space=pltpu.VMEM)
    out_shape = []
    for wa in ws:
        out_shape += [jax.ShapeDtypeStruct(wa.shape, F32)] * 4
    res = pl.pallas_call(
        body,
        name="adamw_all",
        out_shape=out_shape,
        in_specs=[vmem] * (4 * n),
        out_specs=[vmem] * (4 * n),
        compiler_params=_params(),
    )(*gs, *ws, *ms, *vs)
    return [tuple(res[4 * a : 4 * a + 4]) for a in range(n)]


def _reduce_all(parts, small, early):
    n, ne = len(parts), len(early)
    parts4 = [p.reshape(4, 2, *p.shape[1:]) for p in parts]
    blks = [p.shape[1:] for p in parts]

    def body(*refs):
        refs = list(refs)
        take = lambda k: [refs.pop(0) for _ in range(k)]
        part = take(n)
        (small_ref,) = take(1)
        early_in = [take(2) for _ in range(ne)]
        outs = take(n)
        small_all, small_sum = take(2)
        early_out = take(ne)
        own, r1, got_a1, got_a2, got_b1, got_b2, pass_a, pass_b = (take(n) for _ in range(8))
        early_buf = take(ne)
        s1_send, s1_recv, h_send, h_recv, loc, small_send, small_recv, early_sems = refs
        small_start, small_forward, small_finish = _gather_steps([small_ref], [small_all], small_send, small_recv)
        x, y, c = _place()
        me = 4 * x + 2 * y + c
        landed = [pltpu.make_async_copy(early_in[e][1], early_buf[e], early_sems.at[e, 0]) for e in range(ne)]
        for cp in landed:
            cp.start()
        sibling = (x, y, 1 - c)
        chips = [(1 - x, y), (x, 1 - y), (1 - x, 1 - y)]

        def rowwise(rows, fn):
            chunk = math.gcd(rows, 128)

            def step(i, _):
                fn(pl.ds(pl.multiple_of(i * chunk, chunk), chunk))
                return 0

            lax.fori_loop(0, rows // chunk, step, 0)

        stage1, local = [], []
        for a in range(n):
            cp = pltpu.make_async_remote_copy(
                src_ref=part[a].at[:, 1 - c], dst_ref=r1[a], send_sem=s1_send.at[a], recv_sem=s1_recv.at[a],
                device_id=sibling, device_id_type=MESH)
            cp.start()
            stage1.append(cp)
            lc = pltpu.make_async_copy(part[a].at[:, c], own[a], loc.at[a])
            lc.start()
            local.append(lc)
        small_start()
        x_nbr, y_nbr = (1 - x, y, c), (x, 1 - y, c)
        mine, mine_x, mine_y = 2 * x + y, 2 * (1 - x) + y, 2 * x + (1 - y)

        def hop(a, k, src, dst, to):
            return pltpu.make_async_remote_copy(
                src_ref=src, dst_ref=dst, send_sem=h_send.at[a, k], recv_sem=h_recv.at[a, k],
                device_id=to, device_id_type=MESH)

        first, second = [], []
        for a in range(n):
            half = blks[a][0] // 2
            up, low = pl.ds(0, half), pl.ds(half, half)
            local[a].wait()
            stage1[a].wait_recv()
            for chip in range(4):

                def add(r, a=a, chip=chip):
                    own[a][chip, r, :] = own[a][chip, r, :] + r1[a][chip, r, :]

                rowwise(blks[a][0], add)
            first.append([
                hop(a, 0, own[a].at[pl.ds(2 * (1 - x), 2), up], got_a1[a], x_nbr),
                hop(a, 2, own[a].at[2 * x + (1 - y), low], got_b1[a].at[x], y_nbr),
                hop(a, 3, own[a].at[2 * (1 - x) + (1 - y), low], got_b1[a].at[1 - x], y_nbr),
            ])
            for cp in first[a]:
                cp.start()
        small_forward()
        for a in range(n):
            half = blks[a][0] // 2
            first[a][0].wait_recv()

            def fold_upper(r, a=a):
                own[a][mine, r, :] = own[a][mine, r, :] + got_a1[a][y, r, :]
                pass_a[a][r, :] = own[a][mine_y, r, :] + got_a1[a][1 - y, r, :]

            rowwise(half, fold_upper)
            first[a][1].wait_recv()
            first[a][2].wait_recv()

            def fold_lower(r, a=a, half=half):
                rl = pl.ds(pl.multiple_of(r.start + half, SUBLANES), r.size)
                own[a][mine, rl, :] = own[a][mine, rl, :] + got_b1[a][x, r, :]
                pass_b[a][r, :] = own[a][mine_x, rl, :] + got_b1[a][1 - x, r, :]

            rowwise(half, fold_lower)
            second.append([hop(a, 1, pass_a[a], got_a2[a], y_nbr), hop(a, 4, pass_b[a], got_b2[a], x_nbr)])
            for cp in second[a]:
                cp.start()
        for e in range(ne):
            part_e, _ = early_in[e]
            landed[e].wait()
            own_block = pltpu.make_async_copy(part_e.at[me], early_buf[e].at[me], early_sems.at[e, 1])
            own_block.start()
            own_block.wait()

            def sum_early(r, e=e):
                g = early_buf[e][0, r, :]
                for dev in range(1, N_DEV):
                    g = g + early_buf[e][dev, r, :]
                early_out[e][r, :] = g

            rowwise(early_buf[e].shape[1], sum_early)
        for a in range(n):
            half = blks[a][0] // 2
            second[a][0].wait_recv()
            second[a][1].wait_recv()

            def finish_rows(r, a=a, half=half):
                rl = pl.ds(pl.multiple_of(r.start + half, SUBLANES), r.size)
                outs[a][r, :] = own[a][mine, r, :] + got_a2[a][r, :]
                outs[a][rl, :] = own[a][mine, rl, :] + got_b2[a][r, :]

            rowwise(half, finish_rows)
        small_finish()
        _ordered_sum(small_all, small_sum)
        for cp in stage1 + [cp for group in first + second for cp in group]:
            cp.wait_send()

    vmem = pl.BlockSpec(memory_space=pltpu.VMEM)
    hbm = pl.BlockSpec(memory_space=pl.ANY)
    out_shape = [jax.ShapeDtypeStruct(b, F32) for b in blks]
    out_shape += [jax.ShapeDtypeStruct((N_DEV, *small.shape), F32), jax.ShapeDtypeStruct(small.shape, F32)]
    out_shape += [jax.ShapeDtypeStruct(e[0].shape[1:], F32) for e in early]
    halves = [(b[0] // 2, b[1]) for b in blks]
    scratch = (
        [pltpu.VMEM((4, *b), F32) for b in blks]
        + [pltpu.VMEM((4, *b), F32) for b in blks]
        + [pltpu.VMEM((2, *h), F32) for h in halves]
        + [pltpu.VMEM(h, F32) for h in halves]
        + [pltpu.VMEM((2, *h), F32) for h in halves]
        + [pltpu.VMEM(h, F32) for h in halves] * 3
        + [pltpu.VMEM(e[0].shape, F32) for e in early]
        + [pltpu.SemaphoreType.DMA((n,)), pltpu.SemaphoreType.DMA((n,)), pltpu.SemaphoreType.DMA((n, 5)),
           pltpu.SemaphoreType.DMA((n, 5)), pltpu.SemaphoreType.DMA((n,)),
           pltpu.SemaphoreType.DMA((1, 7)), pltpu.SemaphoreType.DMA((1, 7)), pltpu.SemaphoreType.DMA((ne, 2))]
    )
    res = pl.pallas_call(
        body,
        name="reduce_all",
        out_shape=out_shape,
        in_specs=[hbm] * n + [vmem] + [hbm, hbm] * ne,
        out_specs=[vmem] * (n + 2 + ne),
        scratch_shapes=scratch,
        compiler_params=_params(),
    )(*parts4, small, *[t for e in early for t in e])
    return list(res[:n]) + list(res[n + 2 :]), res[n + 1]


def _rms(x):
    return lax.rsqrt(jnp.mean(x * x, axis=-1, keepdims=True) + EPS)


def _token_tile(tokens, want):
    tile = min(want, tokens // 2)
    assert tokens % tile == 0 and tile % 16 == 0
    return tile


def _in_proj(x2, g_pre, w_in_blk, shards):
    tokens, d = x2.shape
    nb = w_in_blk.shape[1]
    tm = _token_tile(tokens, 2048)
    n_t = tokens // tm
    ns = len(shards)
    x_pos, y_pos, c_pos = _place()
    slot = lambda px, py, pc: 4 * px + 2 * py + pc
    chip_order = [(x_pos, y_pos), (1 - x_pos, y_pos), (x_pos, 1 - y_pos), (1 - x_pos, 1 - y_pos)]
    order = jnp.stack([slot(px, py, pc) for px, py in chip_order for pc in (c_pos, 1 - c_pos)]).astype(jnp.int32)

    def body(order_ref, x_ref, g_ref, w_ref, *rest):
        shard_hbm, proj_ref, w_hbm = rest[:ns], rest[ns], rest[ns + 1]
        gathered = rest[ns + 2 : 2 * ns + 2]
        h_all, land, w_send, w_recv, out_sem, send_sems, recv_sems, own_sems = rest[2 * ns + 2 :]
        j, i = pl.program_id(0), pl.program_id(1)
        x, y, c = _place()
        me, sibling = (x, y, c), (x, y, 1 - c)
        chips = [(1 - x, y), (x, 1 - y), (1 - x, 1 - y)]
        own = [pltpu.make_async_copy(shard_hbm[a], gathered[a].at[slot(*me)], own_sems.at[a]) for a in range(ns)]
        start, finish = _exchange_steps(ns, lambda a, s: shard_hbm[a], gathered, send_sems, recv_sems)

        def copy(k, block, to):
            ref = land.at[slot(*block)]
            return pltpu.make_async_remote_copy(
                src_ref=ref, dst_ref=ref, send_sem=w_send.at[k], recv_sem=w_recv.at[k], device_id=to, device_id_type=MESH)

        first_sends = [copy(0, me, sibling)] + [copy(1 + k, me, (*chip, c)) for k, chip in enumerate(chips)]
        forwards = [copy(4 + k, (*chip, c), sibling) for k, chip in enumerate(chips)]

        @pl.when((j == 0) & (i == 0))
        def _():
            land[slot(*me)] = w_ref[...].astype(BF16)
            for cp in first_sends:
                cp.start()
            start()
            for cp in own:
                cp.start()

        @pl.when((j == 1) & (i == 0))
        def _():
            copy(0, sibling, me).wait_recv()

        for k, chip in enumerate(chips):

            @pl.when((j == 2 + 2 * k) & (i == 0))
            def _(k=k, chip=chip):
                copy(1 + k, (*chip, c), me).wait_recv()
                forwards[k].start()

            @pl.when((j == 3 + 2 * k) & (i == 0))
            def _(k=k, chip=chip):
                copy(4 + k, (*chip, 1 - c), me).wait_recv()

        rows = pl.ds(pl.multiple_of(i * tm, tm), tm)

        @pl.when(j == 0)
        def _():
            x_t = x_ref[...]
            h_all[rows, :] = (x_t * _rms(x_t) * g_ref[...]).astype(BF16)

        proj_ref[...] = _mm(h_all[rows, :], land[order_ref[j]])

        @pl.when((j == N_DEV - 1) & (i == n_t - 1))
        def _():
            for cp in first_sends + forwards:
                cp.wait_send()
            finish()
            for cp in own:
                cp.wait()
            out = pltpu.make_async_copy(land, w_hbm, out_sem)
            out.start()
            out.wait()

    hbm = pl.BlockSpec(memory_space=pl.ANY)
    res = pl.pallas_call(
        body,
        name="in_proj",
        grid_spec=pltpu.PrefetchScalarGridSpec(
            num_scalar_prefetch=1,
            grid=(N_DEV, n_t),
            in_specs=[
                pl.BlockSpec((tm, d), lambda j, i, order: (jnp.where(j == 0, i, n_t - 1), 0)),
                pl.BlockSpec((1, d), lambda j, i, order: (0, 0)),
                pl.BlockSpec((d, nb), lambda j, i, order: (0, 0)),
            ]
            + [hbm] * ns,
            out_specs=[pl.BlockSpec((tm, nb), lambda j, i, order: (i, order[j])), hbm] + [hbm] * ns,
            scratch_shapes=[
                pltpu.VMEM((tokens, d), BF16),
                pltpu.VMEM((N_DEV, d, nb), BF16),
                pltpu.SemaphoreType.DMA((7,)),
                pltpu.SemaphoreType.DMA((7,)),
                pltpu.SemaphoreType.DMA,
                pltpu.SemaphoreType.DMA((ns, 7)),
                pltpu.SemaphoreType.DMA((ns, 7)),
                pltpu.SemaphoreType.DMA((ns,)),
            ],
        ),
        out_shape=[
            jax.ShapeDtypeStruct((tokens, N_DEV * nb), F32),
            jax.ShapeDtypeStruct((N_DEV, d, nb), BF16),
        ]
        + [jax.ShapeDtypeStruct((N_DEV, *a.shape), a.dtype) for a in shards],
        compiler_params=_params("arbitrary", "arbitrary"),
    )(order, x2, g_pre, w_in_blk, *shards)
    return res[0], res[1], res[2:]


def _in_proj_bwd(x2, dres, du_pool, du_ssm, dq, dgate, g_pre, w_in_t):
    tokens, d = x2.shape
    nb = w_in_t.shape[0] // N_DEV
    pool_w, ssm_w, att_w, mix = du_pool.shape[1], du_ssm.shape[1], dq.shape[1], dgate.shape[1]
    cl = _token_tile(tokens, 512)
    n_tiles = tokens // cl

    def body(x_ref, dres_ref, dup_ref, dus_ref, dq_ref, dgate_ref, g_ref, w_ref, gx_ref, gw_hbm, gg_ref, acc, sem):
        i = pl.program_id(0)

        @pl.when(i == 0)
        def _():
            acc[...] = jnp.zeros_like(acc)
            gg_ref[...] = jnp.zeros_like(gg_ref)

        x = x_ref[...]
        r = _rms(x)
        xn = x * r
        g = g_ref[...]
        h = (xn * g).astype(BF16)
        dproj = jnp.concatenate([dup_ref[...], dus_ref[...], dq_ref[...], dgate_ref[...]], axis=1).astype(BF16)
        dh = _mm(dproj, w_ref[...])
        for j in range(N_DEV):
            acc[j] += _mm_tn(h, dproj[:, j * nb : (j + 1) * nb])
        gg_ref[...] += jnp.sum(dh * xn, axis=0, keepdims=True)
        dxn = dh * g
        gx_ref[...] = dres_ref[...] + r * (dxn - xn * jnp.mean(dxn * xn, axis=-1, keepdims=True))

        @pl.when(i == n_tiles - 1)
        def _():
            cp = pltpu.make_async_copy(acc, gw_hbm, sem)
            cp.start()
            cp.wait()

    return pl.pallas_call(
        body,
        name="in_proj_bwd",
        grid=(n_tiles,),
        in_specs=[
            pl.BlockSpec((cl, d), lambda i: (i, 0)),
            pl.BlockSpec((cl, d), lambda i: (i, 0)),
            pl.BlockSpec((cl, pool_w), lambda i: (i, 0)),
            pl.BlockSpec((cl, ssm_w), lambda i: (i, 0)),
            pl.BlockSpec((cl, att_w), lambda i: (i, 0)),
            pl.BlockSpec((cl, mix), lambda i: (i, 0)),
            pl.BlockSpec((1, d), lambda i: (0, 0)),
            pl.BlockSpec(w_in_t.shape, lambda i: (0, 0)),
        ],
        out_specs=[
            pl.BlockSpec((cl, d), lambda i: (i, 0)),
            pl.BlockSpec(memory_space=pl.ANY),
            pl.BlockSpec((1, d), lambda i: (0, 0)),
        ],
        out_shape=[
            jax.ShapeDtypeStruct((tokens, d), F32),
            jax.ShapeDtypeStruct((N_DEV, d, nb), F32),
            jax.ShapeDtypeStruct((1, d), F32),
        ],
        scratch_shapes=[pltpu.VMEM((N_DEV, d, nb), F32), pltpu.SemaphoreType.DMA],
        compiler_params=_params("arbitrary"),
    )(x2, dres, du_pool, du_ssm, dq, dgate, g_pre, w_in_t)


def _pool_geometry(seq, width):
    gw = width // len(POOL_WINDOWS)
    col = lax.broadcasted_iota(jnp.int32, (1, width), 1)
    win = jnp.full((1, width), float(POOL_WINDOWS[-1]), F32)
    for gi in range(len(POOL_WINDOWS) - 2, -1, -1):
        win = jnp.where(col < (gi + 1) * gw, float(POOL_WINDOWS[gi]), win)
    row = lax.broadcasted_iota(jnp.int32, (seq, width), 0)
    filling = 1.0 / (lax.broadcasted_iota(jnp.int32, (seq, 1), 0) + 1).astype(F32)
    inv_cnt = jnp.where(row + 1 < win.astype(jnp.int32), filling, 1.0 / win)
    return win, row, inv_cnt


def _window_sums(a, win, seq, back):
    pad = 2 * POOL_WINDOWS[-1]
    zeros = jnp.zeros((pad, a.shape[1]), F32)
    s = jnp.concatenate([a, zeros] if back else [zeros, a], axis=0)
    sums = []
    k = 1
    while k < POOL_WINDOWS[-1]:
        s = s + pltpu.roll(s, seq + pad - k if back else k, 0)
        k *= 2
        sums.append((k, s))
    out = sums[-1][1]
    for k, s in reversed(sums[:-1]):
        out = jnp.where(win <= float(k), s, out)
    return out[0:seq] if back else out[pad : pad + seq]


def _pool_fwd(u2, wp_blk, scale, batch, seq):
    width = scale.shape[1]

    def body(u_ref, w_ref, s_ref, y_ref):
        u = u_ref[...]
        win, row, inv_cnt = _pool_geometry(seq, width)
        diff = _window_sums(u, win, seq, False) * inv_cnt - u
        y_ref[...] =_mm(diff.astype(BF16), w_ref[...]) * s_ref[...]

    return pl.pallas_call(
        body,
        name="pool_fwd",
        grid=(batch,),
        in_specs=[
            pl.BlockSpec((seq, width), lambda b: (b, 0)),
            pl.BlockSpec((width, width), lambda b: (0, 0)),
            pl.BlockSpec((1, width), lambda b: (0, 0)),
        ],
        out_specs=pl.BlockSpec((seq, width), lambda b: (b, 0)),
        out_shape=jax.ShapeDtypeStruct((u2.shape[0], width), F32),
        compiler_params=_params("arbitrary"),
    )(u2, wp_blk, scale)


def _pool_bwd(u2, dy2, wp_blk, scale, batch, seq):
    width = scale.shape[1]

    def body(u_ref, dy_ref, w_ref, s_ref, du_ref, gw_ref, gs_ref):
        @pl.when(pl.program_id(0) == 0)
        def _():
            gw_ref[...] = jnp.zeros_like(gw_ref)
            gs_ref[...] = jnp.zeros_like(gs_ref)

        u = u_ref[...]
        dy = dy_ref[...]
        win, row, inv_cnt = _pool_geometry(seq, width)
        diff = (_window_sums(u, win, seq, False) * inv_cnt - u).astype(BF16)
        gs_ref[...] += jnp.sum(dy * _mm(diff, w_ref[...]), axis=0, keepdims=True)
        dys = (dy * s_ref[...]).astype(BF16)
        gw_ref[...] += _mm_tn(diff, dys)
        dd = _mm_nt(dys, w_ref[...])
        du_ref[...] = _window_sums(dd * inv_cnt, win, seq, True) - dd

    return pl.pallas_call(
        body,
        name="pool_bwd",
        grid=(batch,),
        in_specs=[
            pl.BlockSpec((seq, width), lambda b: (b, 0)),
            pl.BlockSpec((seq, width), lambda b: (b, 0)),
            pl.BlockSpec((width, width), lambda b: (0, 0)),
            pl.BlockSpec((1, width), lambda b: (0, 0)),
        ],
        out_specs=[
            pl.BlockSpec((seq, width), lambda b: (b, 0)),
            pl.BlockSpec((width, width), lambda b: (0, 0)),
            pl.BlockSpec((1, width), lambda b: (0, 0)),
        ],
        out_shape=[
            jax.ShapeDtypeStruct((u2.shape[0], width), F32),
            jax.ShapeDtypeStruct((width, width), F32),
            jax.ShapeDtypeStruct((1, width), F32),
        ],
        compiler_params=_params("arbitrary"),
    )(u2, dy2, wp_blk, scale)


def _state_row(z, n_blocks):
    re = jnp.real(z).reshape(n_blocks, -1)
    im = jnp.imag(z).reshape(n_blocks, -1)
    return jnp.concatenate([re, im], axis=1).reshape(1, -1)


def _ssm_tables(a_re, a_im, log_dt, b_re, b_im, c_re, c_im):
    groups, n_state = a_re.shape
    ch = b_re.shape[2]
    nb = groups * ch // LANES
    gl = groups // nb
    lam = lax.complex(a_re, a_im)
    lam_bar = jnp.exp(lam * jnp.exp(log_dt)[:, None])
    b_bar = ((lam_bar - 1.0) / lam)[..., None] * lax.complex(b_re, b_im)
    eye = jnp.eye(gl, dtype=F32)

    def rows_to_state(t):
        return jnp.einsum("sgnc,gh->sgchn", t.reshape(nb, gl, n_state, ch), eye).reshape(nb, gl * ch, gl * n_state)

    def state_to_rows(t):
        return jnp.einsum("sgcn,gh->shngc", t.reshape(nb, gl, ch, n_state), eye).reshape(nb, gl * n_state, gl * ch)

    b_tab = jnp.concatenate([rows_to_state(jnp.real(b_bar)), rows_to_state(jnp.imag(b_bar))], axis=2)
    c_tab = jnp.concatenate([state_to_rows(c_re), -state_to_rows(c_im)], axis=1)
    return _state_row(lam_bar, nb), b_tab, c_tab


def _lam_power(a_re, a_im, log_dt, power, scale, n_blocks):
    return _state_row(scale * jnp.exp(lax.complex(a_re, a_im) * jnp.exp(log_dt)[:, None] * power), n_blocks)


def _state_blocks(s2, n_blocks, width):
    half = s2 // n_blocks // 2
    assert half % width == 0
    return [(b * 2 * half + o, b * 2 * half + half + o) for b in range(n_blocks) for o in range(0, half, width)]


def _scan(src_ref, dst_ref, st_ref, lam8_ref, n_groups, s, n_blocks, reverse, store):
    lb = 512
    for re0, im0 in _state_blocks(2 * s, n_blocks, lb):
        cr, ci = pl.ds(re0, lb), pl.ds(im0, lb)
        lr = lam8_ref[:, cr]
        li = -lam8_ref[:, ci] if reverse else lam8_ref[:, ci]

        def step(i, carry, cr=cr, ci=ci, lr=lr, li=li):
            hr, hi = carry
            grp = n_groups - 1 - i if reverse else i
            rows = pl.ds(pl.multiple_of(grp * SUBLANES, SUBLANES), SUBLANES)
            nr = lr * hr - li * hi + src_ref[rows, cr]
            ni = lr * hi + li * hr + src_ref[rows, ci]
            if store:
                dst_ref[rows, cr] = nr
                dst_ref[rows, ci] = ni
            return nr, ni

        hr, hi = lax.fori_loop(0, n_groups, step, (st_ref[:, cr], st_ref[:, ci]), unroll=2)
        st_ref[:, cr] = hr
        st_ref[:, ci] = hi


def _pack_state(re, im):
    hi = lax.bitcast_convert_type(re.astype(BF16).astype(F32), jnp.uint32)
    lo = lax.bitcast_convert_type(im.astype(BF16).astype(F32), jnp.uint32)
    return hi | (lo >> 16)


def _unpack_state(word):
    re = lax.bitcast_convert_type(word & jnp.uint32(0xFFFF0000), F32)
    im = lax.bitcast_convert_type(word << 16, F32)
    return re, im


def _scan_adjoint(dh_ref, hprev_ref, group0, stg_ref, acc_ref, lam8_ref, n_groups, s, n_blocks):
    lb = 512
    half = s // n_blocks
    for re0, im0 in _state_blocks(2 * s, n_blocks, lb):
        cr, ci = pl.ds(re0, lb), pl.ds(im0, lb)
        ch = pl.ds(re0 // (2 * half) * half + re0 % (2 * half), lb)
        lr, li = lam8_ref[:, cr], -lam8_ref[:, ci]

        def step(i, carry, cr=cr, ci=ci, ch=ch, lr=lr, li=li):
            gr, gi, ar, ai = carry
            grp = n_groups - 1 - i
            rows = pl.ds(pl.multiple_of(grp * SUBLANES, SUBLANES), SUBLANES)
            ngr = lr * gr - li * gi + dh_ref[rows, cr]
            ngi = lr * gi + li * gr + dh_ref[rows, ci]
            hr, hi = _unpack_state(hprev_ref[pl.ds(pl.multiple_of((group0 + grp) * SUBLANES, SUBLANES), SUBLANES), ch])
            ar = ar + hr * ngr + hi * ngi
            ai = ai + hr * ngi - hi * ngr
            dh_ref[rows, cr] = ngr
            dh_ref[rows, ci] = ngi
            return ngr, ngi, ar, ai

        init = (stg_ref[:, cr], stg_ref[:, ci], acc_ref[:, cr], acc_ref[:, ci])
        gr, gi, ar, ai = lax.fori_loop(0, n_groups, step, init)
        stg_ref[:, cr] = gr
        stg_ref[:, ci] = gi
        acc_ref[:, cr] = ar
        acc_ref[:, ci] = ai


def _chunk_starts(st_ref, init_ref, lcl_ref, s, n_blocks):
    w = s // n_blocks
    init_ref[0:1, :] = jnp.zeros((1, 2 * s), F32)
    for re0, im0 in _state_blocks(2 * s, n_blocks, w):
        re, im = pl.ds(re0, w), pl.ds(im0, w)
        ar, ai = lcl_ref[:, re], lcl_ref[:, im]
        cr = jnp.zeros((1, w), F32)
        ci = jnp.zeros((1, w), F32)
        for k in range(1, NCH):
            cr, ci = (ar * cr - ai * ci + st_ref[k - 1 : k, re], ar * ci + ai * cr + st_ref[k - 1 : k, im])
            init_ref[k : k + 1, re] = cr
            init_ref[k : k + 1, im] = ci


def _chunk_starts_adjoint(stg_ref, initg_ref, lcl_ref, s, n_blocks):
    w = s // n_blocks
    initg_ref[NCH - 1 : NCH, :] = jnp.zeros((1, 2 * s), F32)
    for re0, im0 in _state_blocks(2 * s, n_blocks, w):
        re, im = pl.ds(re0, w), pl.ds(im0, w)
        ar, ai = lcl_ref[:, re], -lcl_ref[:, im]
        gr = jnp.zeros((1, w), F32)
        gi = jnp.zeros((1, w), F32)
        for k in range(NCH - 2, -1, -1):
            gr, gi = (stg_ref[k + 1 : k + 2, re] + ar * gr - ai * gi, stg_ref[k + 1 : k + 2, im] + ar * gi + ai * gr)
            initg_ref[k : k + 1, re] = gr
            initg_ref[k : k + 1, im] = gi


def _ssm_rows(seq, want):
    rows = min(want, seq // 2)
    assert seq % rows == 0 and rows % SUBLANES == 0
    return rows


def _chunk_copies(hbm_ref, b, cm_ref, sems, to_cm, col0=0):
    cl, _, width = cm_ref.shape
    copies = []
    for k in range(NCH):
        nat, cm = hbm_ref.at[b, pl.ds(k * cl, cl), pl.ds(col0, width)], cm_ref.at[:, k, :]
        src, dst = (nat, cm) if to_cm else (cm, nat)
        copies.append(pltpu.make_async_copy(src, dst, sems.at[k]))
    return copies


def _blockwise(fn, n_blocks):
    return jnp.concatenate([fn(b) for b in range(n_blocks)], axis=1)


def _ssm_fwd(u, u_col, b_tab, c_tab, lam8, lcl, d_skip, w_glu, shards):
    batch, seq, _ = u.shape
    ns = len(shards)
    width = d_skip.shape[1]
    s = lam8.shape[1] // 2
    nb = b_tab.shape[0]
    sb = 2 * s // nb
    cl = seq // NCH
    rows = _ssm_rows(seq, 1024)
    n_tiles = seq // rows
    n_groups = rows // SUBLANES

    def body(u_hbm, b_ref, c_ref, lam_ref, lcl_ref, d_ref, wg_ref, *rest):
        shard_hbm, (y_hbm, pre_ref, z_ref, init_ref) = rest[:ns], rest[ns : ns + 4]
        gathered = rest[ns + 4 : 2 * ns + 4]
        u_cm, y_cm, bu_all, st, sems, send_sems, recv_sems, own_sems = rest[2 * ns + 4 :]
        b, ph, t = pl.program_id(0), pl.program_id(1), pl.program_id(2)
        tile_groups = pl.ds(pl.multiple_of(t * n_groups, n_groups), n_groups)
        x_pos, y_pos, c_pos = _place()
        own = [pltpu.make_async_copy(shard_hbm[a], gathered[a].at[4 * x_pos + 2 * y_pos + c_pos], own_sems.at[a])
               for a in range(ns)]
        exchange_start, exchange_finish = _exchange_steps(
            ns, lambda a, slot: shard_hbm[a], gathered, send_sems, recv_sems)

        @pl.when((b == 0) & (ph == 0) & (t == 0))
        def _():
            exchange_start()
            for cp in own:
                cp.start()

        @pl.when((b == batch - 1) & (ph == 1) & (t == n_tiles - 1))
        def _():
            exchange_finish()
            for cp in own:
                cp.wait()

        @pl.when((ph == 0) & (t == 0))
        def _():
            loads = _chunk_copies(u_hbm, b, u_cm, sems, True, u_col)
            for cp in loads:
                cp.start()
            st[...] = jnp.zeros_like(st)
            for cp in loads:
                cp.wait()

        @pl.when((ph == 1) & (t == 0))
        def _():
            st[...] = init_ref[...]

        u_t = u_cm[tile_groups].reshape(rows, width)
        bu = bu_all.at[pl.ds(pl.multiple_of(t * rows, rows), rows)]

        @pl.when(ph == 0)
        def _():
            u_b = u_t.astype(BF16)
            for blk in range(nb):
                bu[:, blk * sb : (blk + 1) * sb] = _mm(u_b[:, blk * LANES : (blk + 1) * LANES], b_ref[blk])
            _scan(bu, bu, st, lam_ref, n_groups, s, nb, False, False)

        @pl.when((ph == 0) & (t == n_tiles - 1))
        def _():
            _chunk_starts(st, init_ref, lcl_ref, s, nb)

        @pl.when(ph == 1)
        def _():
            _scan(bu, bu, st, lam_ref, n_groups, s, nb, False, True)
            hs = lambda blk: _mm(bu[:, blk * sb : (blk + 1) * sb].astype(BF16), c_ref[blk])
            pre = _blockwise(hs, nb) + d_ref[...] * u_t
            z = _mm(_gelu(pre).astype(BF16), wg_ref[...])
            pre_ref[...] = pre
            z_ref[...] = z
            y = z[:, 0:width] * jax.nn.sigmoid(z[:, width : 2 * width])
            y_cm[tile_groups] = y.reshape(n_groups, SUBLANES, width)

        @pl.when((ph == 1) & (t == n_tiles - 1))
        def _():
            stores = _chunk_copies(y_hbm, b, y_cm, sems, False)
            for cp in stores:
                cp.start()
            for cp in stores:
                cp.wait()

    out_tile = lambda b, ph, t: (b, t * ph, 0)
    full = lambda a: pl.BlockSpec(a.shape, lambda b, ph, t: (0,) * a.ndim)
    hbm = pl.BlockSpec(memory_space=pl.ANY)
    res = pl.pallas_call(
        body,
        name="ssm_fwd",
        grid=(batch, 2, n_tiles),
        in_specs=[hbm, full(b_tab), full(c_tab), full(lam8), full(lcl), full(d_skip), full(w_glu)] + [hbm] * ns,
        out_specs=[
            hbm,
            pl.BlockSpec((None, rows, width), out_tile),
            pl.BlockSpec((None, rows, 2 * width), out_tile),
            pl.BlockSpec((None, SUBLANES, 2 * s), lambda b, ph, t: (b, 0, 0)),
        ]
        + [hbm] * ns,
        out_shape=[
            jax.ShapeDtypeStruct((batch, seq, width), F32),
            jax.ShapeDtypeStruct((batch, seq, width), F32),
            jax.ShapeDtypeStruct((batch, seq, 2 * width), F32),
            jax.ShapeDtypeStruct((batch, SUBLANES, 2 * s), F32),
        ]
        + [jax.ShapeDtypeStruct((N_DEV, *a.shape), a.dtype) for a in shards],
        scratch_shapes=[
            pltpu.VMEM((cl, NCH, width), F32),
            pltpu.VMEM((cl, NCH, width), F32),
            pltpu.VMEM((seq, 2 * s), F32),
            pltpu.VMEM((SUBLANES, 2 * s), F32),
            pltpu.SemaphoreType.DMA((NCH,)),
            pltpu.SemaphoreType.DMA((ns, 7)),
            pltpu.SemaphoreType.DMA((ns, 7)),
            pltpu.SemaphoreType.DMA((ns,)),
        ],
        compiler_params=_params("arbitrary", "arbitrary", "arbitrary"),
    )(u, b_tab, c_tab, lam8, lcl, d_skip, w_glu, *shards)
    return res[:4], res[4:]


def _ssm_bwd(u, u_col, pre_p, z_p, dy, init, b_tab, b_tab_t, c_tab_t, lam8, lcl, d_skip, w_glu, ready):
    batch, seq, _ = u.shape
    width = d_skip.shape[1]
    nr = len(ready)
    s = lam8.shape[1] // 2
    nb = b_tab.shape[0]
    sb = 2 * s // nb
    cl = seq // NCH
    rows = _ssm_rows(seq, 1024)
    n_tiles = seq // rows
    n_groups = rows // SUBLANES

    def body(u_hbm, pre_ref, z_ref, dy_hbm, init_ref, b_ref, bt_ref, ct_ref, lam_ref, lcl_ref, d_ref, wg_ref, *rest):
        ready_hbm, rest = rest[:nr], rest[nr:]
        du_hbm, gb_ref, gc_ref, gwg_ref, gd_ref, glam_ref = rest[:6]
        landed_hbm, rest = rest[6 : 6 + nr], rest[6 + nr :]
        u_cm, dy_cm, work, hs_all, dpre_all, st, stg, initg, acc, sems, send_sems, recv_sems = rest
        b, ph, t = pl.program_id(0), pl.program_id(1), pl.program_id(2)
        half = s // nb
        exchange_start, exchange_finish = _exchange_steps(
            nr, lambda a, slot: ready_hbm[a].at[slot], landed_hbm, send_sems, recv_sems)
        first = (b == 0) & (ph == 0) & (t == 0)
        last = (b == batch - 1) & (ph == 2) & (t == n_tiles - 1)
        tile = jnp.where(ph == 0, t, n_tiles - 1 - t)
        tile_rows = pl.ds(pl.multiple_of(tile * rows, rows), rows)
        tile_groups = pl.ds(pl.multiple_of(tile * n_groups, n_groups), n_groups)
        lanes = lambda blk: slice(blk * LANES, (blk + 1) * LANES)
        states = lambda blk: slice(blk * sb, (blk + 1) * sb)

        @pl.when(first)
        def _():
            exchange_start()
            acc[...] = jnp.zeros_like(acc)
            gb_ref[...] = jnp.zeros_like(gb_ref)
            gc_ref[...] = jnp.zeros_like(gc_ref)
            gwg_ref[...] = jnp.zeros_like(gwg_ref)
            gd_ref[...] = jnp.zeros_like(gd_ref)

        @pl.when((ph == 0) & (t == 0))
        def _():
            loads = (_chunk_copies(u_hbm, b, u_cm, sems.at[0], True, u_col)
                     + _chunk_copies(dy_hbm, b, dy_cm, sems.at[1], True))
            for cp in loads:
                cp.start()
            st[...] = init_ref[...]
            for blk in range(nb):
                entry = init_ref[:, states(blk)]
                hs_all[0:SUBLANES, blk * half : (blk + 1) * half] = _pack_state(entry[:, 0:half], entry[:, half : 2 * half])
            for cp in loads:
                cp.wait()

        u_t = u_cm[tile_groups].reshape(rows, width)
        u_b = u_t.astype(BF16)

        @pl.when(ph == 0)
        def _():
            for blk in range(nb):
                work[:, states(blk)] = _mm(u_b[:, lanes(blk)], b_ref[blk])
            _scan(work, work, st, lam_ref, n_groups, s, nb, False, True)
            z = z_ref[...]
            dy_t = dy_cm[tile_groups].reshape(rows, width)
            pre = pre_ref[...]
            z1, sig = z[:, 0:width], jax.nn.sigmoid(z[:, width : 2 * width])
            dz = jnp.concatenate([dy_t * sig, dy_t * z1 * sig * (1.0 - sig)], axis=1).astype(BF16)
            gwg_ref[...] += _mm_tn(_gelu(pre).astype(BF16), dz)
            dpre = _mm_nt(dz, wg_ref[...]) * _gelu_grad(pre)
            dpre_all[tile_rows, :] = dpre
            gd_ref[...] += jnp.sum(dpre * u_t, axis=0, keepdims=True)
            dpre_b = dpre.astype(BF16)
            kept = pl.ds(pl.multiple_of(tile * rows + SUBLANES, SUBLANES), rows)
            for blk in range(nb):
                hs = work[:, states(blk)]
                gc_ref[blk] += _mm_tn(dpre_b[:, lanes(blk)], hs.astype(BF16))
                hs_all[kept, blk * half : (blk + 1) * half] = _pack_state(hs[:, 0:half], hs[:, half : 2 * half])

        @pl.when(ph >= 1)
        def _():
            dpre_b = dpre_all[tile_rows, :].astype(BF16)
            for blk in range(nb):
                work[:, states(blk)] = _mm(dpre_b[:, lanes(blk)], ct_ref[blk])

        @pl.when(ph == 1)
        def _():
            @pl.when(t == 0)
            def _():
                stg[...] = jnp.zeros_like(stg)

            _scan(work, work, stg, lam_ref, n_groups, s, nb, True, False)

            @pl.when(t == n_tiles - 1)
            def _():
                _chunk_starts_adjoint(stg, initg, lcl_ref, s, nb)

        @pl.when(ph == 2)
        def _():
            @pl.when(t == 0)
            def _():
                stg[...] = initg[...]

            _scan_adjoint(work, hs_all, tile * n_groups, stg, acc, lam_ref, n_groups, s, nb)
            du = lambda blk: _mm(work[:, states(blk)].astype(BF16), bt_ref[blk])
            du_t = _blockwise(du, nb) + dpre_all[tile_rows, :] * d_ref[...]
            dy_cm[tile_groups] = du_t.reshape(n_groups, SUBLANES, width)
            for blk in range(nb):
                gb_ref[blk] += _mm_tn(u_b[:, lanes(blk)], work[:, states(blk)].astype(BF16))

            @pl.when(t == n_tiles - 1)
            def _():
                stores = _chunk_copies(du_hbm, b, dy_cm, sems.at[0], False)
                for cp in stores:
                    cp.start()
                for cp in stores:
                    cp.wait()

        @pl.when(last)
        def _():
            glam_ref[...] = jnp.sum(acc[...], axis=0, keepdims=True)
            exchange_finish()

    def tile(b, ph, t):
        return (b, jnp.where(ph == 0, t, n_tiles - 1 - t), 0)

    full = lambda a: pl.BlockSpec(a.shape, lambda b, ph, t: (0,) * a.ndim)
    hbm = pl.BlockSpec(memory_space=pl.ANY)
    res = pl.pallas_call(
        body,
        name="ssm_bwd",
        grid=(batch, 3, n_tiles),
        in_specs=[
            hbm,
            pl.BlockSpec((None, rows, width), tile),
            pl.BlockSpec((None, rows, 2 * width), tile),
            hbm,
            pl.BlockSpec((None, SUBLANES, 2 * s), lambda b, ph, t: (b, 0, 0)),
            full(b_tab), full(b_tab_t), full(c_tab_t), full(lam8), full(lcl), full(d_skip), full(w_glu),
        ]
        + [hbm] * nr,
        out_specs=[
            hbm,
            full(b_tab), full(b_tab), full(w_glu), full(d_skip),
            pl.BlockSpec((1, 2 * s), lambda b, ph, t: (0, 0)),
        ]
        + [hbm] * nr,
        out_shape=[
            jax.ShapeDtypeStruct((batch, seq, width), F32),
            jax.ShapeDtypeStruct(b_tab.shape, F32),
            jax.ShapeDtypeStruct(b_tab.shape, F32),
            jax.ShapeDtypeStruct(w_glu.shape, F32),
            jax.ShapeDtypeStruct(d_skip.shape, F32),
            jax.ShapeDtypeStruct((1, 2 * s), F32),
        ]
        + [jax.ShapeDtypeStruct(a.shape, F32) for a in ready],
        scratch_shapes=[
            pltpu.VMEM((cl, NCH, width), F32),
            pltpu.VMEM((cl, NCH, width), F32),
            pltpu.VMEM((rows, 2 * s), F32),
            pltpu.VMEM((seq + SUBLANES, s), jnp.uint32),
            pltpu.VMEM((seq, width), F32),
        ]
        + [pltpu.VMEM((SUBLANES, 2 * s), F32)] * 4
        + [pltpu.SemaphoreType.DMA((2, NCH)), pltpu.SemaphoreType.DMA((nr, 7)), pltpu.SemaphoreType.DMA((nr, 7))],
        compiler_params=_params("arbitrary", "arbitrary", "arbitrary"),
    )(u, pre_p, z_p, dy, init, b_tab, b_tab_t, c_tab_t, lam8, lcl, d_skip, w_glu, *ready)
    return res[:6], res[6:]


def _kv_fwd(mem, g_mem, w_kv):
    batch, n_mem, d = mem.shape
    kvw = w_kv.shape[1]

    def body(mem_ref, g_ref, w_ref, kv_ref):
        m = mem_ref[...]
        kv_ref[...] = _mm((m * _rms(m) * g_ref[...]).astype(BF16), w_ref[...])

    return pl.pallas_call(
        body,
        name="kv_fwd",
        grid=(batch,),
        in_specs=[
            pl.BlockSpec((None, n_mem, d), lambda b: (b, 0, 0)),
            pl.BlockSpec((1, d), lambda b: (0, 0)),
            pl.BlockSpec((d, kvw), lambda b: (0, 0)),
        ],
        out_specs=pl.BlockSpec((None, n_mem, kvw), lambda b: (b, 0, 0)),
        out_shape=jax.ShapeDtypeStruct((batch, n_mem, kvw), F32),
        compiler_params=_params("arbitrary"),
    )(mem, g_mem, w_kv)


def _kv_bwd(mem, dkv, g_mem, w_kv):
    batch, n_mem, d = mem.shape
    kvw = w_kv.shape[1]

    def body(mem_ref, dkv_ref, g_ref, w_ref, gw_ref, gg_ref):
        @pl.when(pl.program_id(0) == 0)
        def _():
            gw_ref[...] = jnp.zeros_like(gw_ref)
            gg_ref[...] = jnp.zeros_like(gg_ref)

        m = mem_ref[...]
        mn = m * _rms(m)
        dkv_b = dkv_ref[...].astype(BF16)
        gw_ref[...] += _mm_tn((mn * g_ref[...]).astype(BF16), dkv_b)
        gg_ref[...] += jnp.sum(_mm_nt(dkv_b, w_ref[...]) * mn, axis=0, keepdims=True)

    return pl.pallas_call(
        body,
        name="kv_bwd",
        grid=(batch,),
        in_specs=[
            pl.BlockSpec((None, n_mem, d), lambda b: (b, 0, 0)),
            pl.BlockSpec((None, n_mem, kvw), lambda b: (b, 0, 0)),
            pl.BlockSpec((1, d), lambda b: (0, 0)),
            pl.BlockSpec((d, kvw), lambda b: (0, 0)),
        ],
        out_specs=[pl.BlockSpec((d, kvw), lambda b: (0, 0)), pl.BlockSpec((1, d), lambda b: (0, 0))],
        out_shape=[jax.ShapeDtypeStruct((d, kvw), F32), jax.ShapeDtypeStruct((1, d), F32)],
        compiler_params=_params("arbitrary"),
    )(mem, dkv, g_mem, w_kv)


def _tail(x2, target2, proj, y_pool, y_ssm, kv, w_out, g_post):
    tokens, d = x2.shape
    batch, n_mem, kvw = kv.shape
    pool_w, ssm_w, att_w, mix = y_pool.shape[1], y_ssm.shape[1], kvw // 2, w_out.shape[0]
    assert (mix - att_w) % att_w == 0 and proj.shape[1] == 2 * mix
    hd = att_w // MEM_HEADS
    cl = _token_tile(tokens // batch, 512)
    n_tiles = tokens // cl
    per_seq = tokens // batch // cl
    qk_scale = hd**-0.5

    def body(x_ref, tg_ref, gate_ref, yp_ref, ys_ref, q_ref, kv_ref, w_ref, g_ref,
             dres_ref, dgate_ref, dyp_ref, dys_ref, dq_ref, dkv_ref, gw_hbm, gg_ref, loss_ref, acc, sem):
        i = pl.program_id(0)

        @pl.when(i == 0)
        def _():
            acc[...] = jnp.zeros_like(acc)
            gg_ref[...] = jnp.zeros_like(gg_ref)
            loss_ref[...] = jnp.zeros_like(loss_ref)

        @pl.when(i % per_seq == 0)
        def _():
            dkv_ref[...] = jnp.zeros_like(dkv_ref)

        k = kv_ref[:, 0:att_w].astype(BF16)
        v = kv_ref[:, att_w : 2 * att_w].astype(BF16)
        lane = lax.broadcasted_iota(jnp.int32, (1, att_w), 1)
        heads = [(lane >= h * hd) & (lane < (h + 1) * hd) for h in range(MEM_HEADS)]
        g = g_ref[...]

        def part(rows):
            n_rows = rows.stop - rows.start
            q = q_ref[rows, :]
            probs, q_heads = [], []
            att = jnp.zeros((n_rows, att_w), F32)
            for mask in heads:
                qh = jnp.where(mask, q, 0.0).astype(BF16)
                sc = _mm_nt(qh, k) * qk_scale
                e = jnp.exp(sc - jnp.max(sc, axis=-1, keepdims=True))
                p = e * (1.0 / jnp.sum(e, axis=-1, keepdims=True))
                att = att + jnp.where(mask, _mm(p.astype(BF16), v), 0.0)
                probs.append(p)
                q_heads.append(qh)

            ycat = jnp.concatenate([yp_ref[rows, :], ys_ref[rows, :], att], axis=1)
            gate = gate_ref[rows, :]
            sig = jax.nn.sigmoid(gate)
            silu = gate * sig
            yg = (ycat * silu).astype(BF16)
            out = _mm(yg, w_ref[...])
            r = _rms(out)
            on = out * r
            err = x_ref[rows, :] + on * g - tg_ref[rows, :]
            loss_ref[...] += 0.5 * jnp.sum(jnp.mean(err * err, axis=-1, keepdims=True), axis=0, keepdims=True)
            dres = err * (1.0 / d)
            dres_ref[rows, :] = dres
            gg_ref[...] += jnp.sum(dres * on, axis=0, keepdims=True)
            don = dres * g
            dout = (r * (don - on * jnp.mean(don * on, axis=-1, keepdims=True))).astype(BF16)
            acc[...] += _mm_tn(yg, dout)
            dyg = _mm_nt(dout, w_ref[...])
            dgate_ref[rows, :] = dyg * ycat * (sig * (1.0 + gate * (1.0 - sig)))
            dycat = dyg * silu
            dyp_ref[rows, :] = dycat[:, 0:pool_w]
            dys_ref[rows, :] = dycat[:, pool_w : pool_w + ssm_w]
            datt = dycat[:, pool_w + ssm_w : mix]

            dq = jnp.zeros((n_rows, att_w), F32)
            dk = jnp.zeros((n_mem, att_w), F32)
            dv = jnp.zeros((n_mem, att_w), F32)
            for mask, p, qh in zip(heads, probs, q_heads):
                doh = jnp.where(mask, datt, 0.0).astype(BF16)
                dp = _mm_nt(doh, v)
                ds = (p * (dp - jnp.sum(p * dp, axis=-1, keepdims=True)) * qk_scale).astype(BF16)
                dq = dq + jnp.where(mask, _mm(ds, k), 0.0)
                dk = dk + _mm_tn(ds, qh)
                dv = dv + _mm_tn(p.astype(BF16), doh)
            dq_ref[rows, :] = dq
            dkv_ref[:, 0:att_w] += dk
            dkv_ref[:, att_w : 2 * att_w] += dv

        part(slice(0, cl))

        @pl.when(i == n_tiles - 1)
        def _():
            cp = pltpu.make_async_copy(acc, gw_hbm, sem)
            cp.start()
            cp.wait()

    tok = lambda w: pl.BlockSpec((cl, w), lambda i: (i, 0))
    chunked = tok(ssm_w)
    per_batch = pl.BlockSpec((None, n_mem, kvw), lambda i: (i // per_seq, 0, 0))
    return pl.pallas_call(
        body,
        name="tail",
        grid=(n_tiles,),
        in_specs=[
            tok(d), tok(d), pl.BlockSpec((cl, mix), lambda i: (i, 1)), tok(pool_w), chunked,
            pl.BlockSpec((cl, att_w), lambda i: (i, (mix - att_w) // att_w)), per_batch,
            pl.BlockSpec((mix, d), lambda i: (0, 0)),
            pl.BlockSpec((1, d), lambda i: (0, 0)),
        ],
        out_specs=[
            tok(d), tok(mix), tok(pool_w), chunked, tok(att_w), per_batch,
            pl.BlockSpec(memory_space=pl.ANY),
            pl.BlockSpec((1, d), lambda i: (0, 0)),
            pl.BlockSpec((1, 1), lambda i: (0, 0)),
        ],
        out_shape=[
            jax.ShapeDtypeStruct((tokens, d), F32),
            jax.ShapeDtypeStruct((tokens, mix), F32),
            jax.ShapeDtypeStruct((tokens, pool_w), F32),
            jax.ShapeDtypeStruct((tokens, ssm_w), F32),
            jax.ShapeDtypeStruct((tokens, att_w), F32),
            jax.ShapeDtypeStruct(kv.shape, F32),
            jax.ShapeDtypeStruct((mix, d), F32),
            jax.ShapeDtypeStruct((1, d), F32),
            jax.ShapeDtypeStruct((1, 1), F32),
        ],
        scratch_shapes=[pltpu.VMEM((mix, d), F32), pltpu.SemaphoreType.DMA],
        compiler_params=_params("arbitrary"),
    )(x2, target2, proj, y_pool, y_ssm, proj, kv, w_out, g_post)


def _pack(arrays):
    flat = jnp.concatenate([a.reshape(-1) for a in arrays])
    rows = -(-flat.size // (SUBLANES * LANES)) * SUBLANES
    return jnp.pad(flat, (0, rows * LANES - flat.size)).reshape(rows, LANES)


def _unpack(packed, like):
    flat, out, at = packed.reshape(-1), [], 0
    for a in like:
        out.append(flat[at : at + a.size].reshape(a.shape))
        at += a.size
    return out


def kernel(x, mem, g_pre, w_in, w_pool, pool_scale, a_re, a_im, log_dt, b_re, b_im, c_re, c_im, d_skip, w_glu, g_mem, w_kv, w_out, g_post, loss_target, m_g_pre, m_w_in, m_w_pool, m_pool_scale, m_a_re, m_a_im, m_log_dt, m_b_re, m_b_im, m_c_re, m_c_im, m_d_skip, m_w_glu, m_g_mem, m_w_kv, m_w_out, m_g_post, v_g_pre, v_w_in, v_w_pool, v_pool_scale, v_a_re, v_a_im, v_log_dt, v_b_re, v_b_im, v_c_re, v_c_im, v_d_skip, v_w_glu, v_g_mem, v_w_kv, v_w_out, v_g_post):
    batch, seq, d = x.shape
    cl = seq // NCH
    pool_w, ssm_w = pool_scale.shape[1], d_skip.shape[1]
    att_w = w_kv.shape[2] // 2
    tokens = batch * seq
    x2 = x.reshape(tokens, d)
    target2 = loss_target.reshape(tokens, d)

    wp_blk = jax.scipy.linalg.block_diag(*w_pool[0]).astype(BF16)
    ssm_params = (a_re[0], a_im[0], log_dt[0], b_re[0], b_im[0], c_re[0], c_im[0])
    (lam_row, b_tab, c_tab), tables_vjp = jax.vjp(_ssm_tables, *ssm_params)
    nb = b_tab.shape[0]
    lam8 = jnp.broadcast_to(lam_row, (SUBLANES, lam_row.shape[1]))
    lcl = _lam_power(a_re[0], a_im[0], log_dt[0], float(cl), 1.0, nb)
    b_bf, c_bf = b_tab.astype(BF16), c_tab.astype(BF16)

    proj, w_in_g, (w_glu_g,) = _in_proj(x2, g_pre, w_in[0], [w_glu[0].astype(BF16)])
    proj3 = proj.reshape(batch, seq, proj.shape[1])
    w_glu_f = w_glu_g.transpose(1, 0, 2).reshape(w_glu_g.shape[1], N_DEV * w_glu_g.shape[2])
    y_pool = _pool_fwd(proj, wp_blk, pool_scale, batch, seq)
    (y_ssm, pre_ssm, z_ssm, init_ssm), (w_out_g, w_kv_g) = _ssm_fwd(
        proj3, pool_w, b_bf, c_bf, lam8, lcl, d_skip, w_glu_f, [w_out[0].astype(BF16), w_kv[0].astype(BF16)])
    w_out_f = w_out_g.reshape(N_DEV * w_out_g.shape[1], w_out_g.shape[2])
    w_kv_f = w_kv_g.reshape(N_DEV * w_kv_g.shape[1], w_kv_g.shape[2])
    kv = _kv_fwd(mem, g_mem, w_kv_f)

    dres, dgate, dy_pool, dy_ssm, dq, dkv, gw_out, gg_post, loss_part = _tail(
        x2, target2, proj, y_pool, y_ssm.reshape(tokens, ssm_w), kv, w_out_f, g_post)

    gw_kv, gg_mem = _kv_bwd(mem, dkv, g_mem, w_kv_f)
    du_pool, gwp_dense, g_scale = _pool_bwd(proj, dy_pool, wp_blk, pool_scale, batch, seq)
    gw_kv8 = gw_kv.reshape(N_DEV, -1, gw_kv.shape[1])
    gw_out8 = gw_out.reshape(N_DEV, -1, gw_out.shape[1])
    (du_ssm, gb_tab, gc_tab_t, gw_glu, gd_skip, glam), (kv_landed, out_landed) = _ssm_bwd(
        proj3, pool_w, pre_ssm, z_ssm, dy_ssm.reshape(batch, seq, ssm_w), init_ssm, b_bf, b_bf.transpose(0, 2, 1),
        c_bf.transpose(0, 2, 1), lam8, lcl, d_skip, w_glu_f, [gw_kv8, gw_out8])
    grad_x2, gw_in, gg_pre = _in_proj_bwd(
        x2, dres, du_pool, du_ssm.reshape(tokens, ssm_w), dq, dgate, g_pre,
        w_in_g.transpose(0, 2, 1).reshape(-1, d))

    gw = pool_w // len(POOL_WINDOWS)
    gw_pool = jnp.stack([gwp_dense[i * gw : (i + 1) * gw, i * gw : (i + 1) * gw] for i in range(len(POOL_WINDOWS))])
    g_ssm = tables_vjp((glam, gb_tab, gc_tab_t.transpose(0, 2, 1)))

    small_w = [g_pre, w_pool, pool_scale, a_re, a_im, log_dt, b_re, b_im, c_re, c_im, d_skip, g_mem, g_post]
    small_m = [m_g_pre, m_w_pool, m_pool_scale, m_a_re, m_a_im, m_log_dt, m_b_re, m_b_im, m_c_re, m_c_im, m_d_skip, m_g_mem, m_g_post]
    small_v = [v_g_pre, v_w_pool, v_pool_scale, v_a_re, v_a_im, v_log_dt, v_b_re, v_b_im, v_c_re, v_c_im, v_d_skip, v_g_mem, v_g_post]
    small_g = [gg_pre, gw_pool, g_scale, *g_ssm, gd_skip, gg_mem, gg_post]
    big_g, small_sum = _reduce_all(
        [gw_in, gw_glu.reshape(ssm_w, N_DEV, -1).transpose(1, 0, 2)],
        _pack(small_g + [loss_part]),
        [(gw_kv8, kv_landed), (gw_out8, out_landed)])

    flat2 = lambda a: a.reshape(-1, a.shape[-1])
    sg = _unpack(small_sum, [flat2(a) for a in small_w] + [loss_part])
    loss = sg[-1].reshape(())
    small_names = ["g_pre", "w_pool", "pool_scale", "a_re", "a_im", "log_dt", "b_re", "b_im", "c_re", "c_im",
                   "d_skip", "g_mem", "g_post"]
    names = ["w_in", "w_glu", "w_kv", "w_out"] + small_names
    all_w = [w_in, w_glu, w_kv, w_out] + small_w
    all_m = [m_w_in, m_w_glu, m_w_kv, m_w_out] + small_m
    all_v = [v_w_in, v_w_glu, v_w_kv, v_w_out] + small_v
    updates = _adamw_all(big_g + sg[:-1], [flat2(a) for a in all_w], [flat2(a) for a in all_m], [flat2(a) for a in all_v])
    updates = {name: [t.reshape(a.shape) for t in u] for name, u, a in zip(names, updates, all_w)}

    order = ["g_pre", "w_in", "w_pool", "pool_scale", "a_re", "a_im", "log_dt", "b_re", "b_im", "c_re", "c_im",
             "d_skip", "w_glu", "g_mem", "w_kv", "w_out", "g_post"]
    outs = [[updates[name][kind] for name in order] for kind in range(4)]
    return (loss, grad_x2.reshape(batch, seq, d), *outs[0], *outs[1], *outs[2], *outs[3])
```

```python
import math

import jax
import jax.numpy as jnp
from jax import lax
from jax.experimental import pallas as pl
from jax.experimental.pallas import tpu as pltpu

F32 = jnp.float32
BF16 = jnp.bfloat16
MESH = pl.DeviceIdType.MESH

N_DEV = 8
SUBLANES = 8
LANES = 128
NCH = SUBLANES
VMEM_LIMIT = 60 * 1024 * 1024

IN_PROJ_ROWS = 2048
TOKEN_ROWS = 512
SSM_ROWS = 1024
SCAN_LANES = 512

EPS = 1e-6
POOL_WINDOWS = (2, 4, 8, 16)
MEM_HEADS = 4
ADAM_LR, ADAM_B1, ADAM_B2, ADAM_EPS, ADAM_WD, ADAM_STEP = 0.001, 0.9, 0.999, 1e-08, 0.01, 10


def _mm(a, b):
    return jnp.dot(a, b, preferred_element_type=F32)


def _mm_nt(a, b):
    return lax.dot_general(a, b, (((1,), (1,)), ((), ())), preferred_element_type=F32)


def _mm_tn(a, b):
    return lax.dot_general(a, b, (((0,), (0,)), ((), ())), preferred_element_type=F32)


def _params(*sem):
    return pltpu.CompilerParams(dimension_semantics=sem or None, vmem_limit_bytes=VMEM_LIMIT)


def _adamw(w, g, m, v):
    m = ADAM_B1 * m + (1.0 - ADAM_B1) * g
    v = ADAM_B2 * v + (1.0 - ADAM_B2) * (g * g)
    m_hat = m / (1.0 - ADAM_B1**ADAM_STEP)
    v_hat = v / (1.0 - ADAM_B2**ADAM_STEP)
    delta = -ADAM_LR * (m_hat / (jnp.sqrt(v_hat) + ADAM_EPS) + ADAM_WD * w)
    return delta, m, v


def _gelu(x):
    k = math.sqrt(2.0 / math.pi)
    return 0.5 * x * (1.0 + jnp.tanh(k * (x + 0.044715 * x * x * x)))


def _gelu_grad(x):
    k = math.sqrt(2.0 / math.pi)
    th = jnp.tanh(k * (x + 0.044715 * x * x * x))
    return 0.5 * (1.0 + th) + 0.5 * x * (1.0 - th * th) * (k * (1.0 + 3.0 * 0.044715 * x * x))


def _place():
    return lax.axis_index("x"), lax.axis_index("y"), lax.axis_index("c")


def _exchange_steps(n, src_of, landing, send_sems, recv_sems):
    x, y, c = _place()
    me = 4 * x + 2 * y + c
    peers = []
    for j in range(1, N_DEV):
        px = 1 - x if j & 4 else x
        py = 1 - y if j & 2 else y
        pc = 1 - c if j & 1 else c
        peers.append((px, py, pc))

    def copy(a, j, from_slot, to_slot, peer):
        return pltpu.make_async_remote_copy(
            src_ref=src_of(a, to_slot), dst_ref=landing[a].at[from_slot],
            send_sem=send_sems.at[a, j], recv_sem=recv_sems.at[a, j], device_id=peer, device_id_type=MESH)

    def start():
        for a in range(n):
            for j, p in enumerate(peers):
                copy(a, j, me, 4 * p[0] + 2 * p[1] + p[2], p).start()

    def finish():
        for a in range(n):
            for j, p in enumerate(peers):
                slot = 4 * p[0] + 2 * p[1] + p[2]
                copy(a, j, slot, slot, p).wait_recv()
        for a in range(n):
            for j, p in enumerate(peers):
                copy(a, j, me, 4 * p[0] + 2 * p[1] + p[2], p).wait_send()

    return start, finish


def _adamw_all(gs, ws, ms, vs):
    n = len(gs)

    def body(*refs):
        g, w, m, v = refs[:n], refs[n : 2 * n], refs[2 * n : 3 * n], refs[3 * n : 4 * n]
        outs = refs[4 * n :]
        for a in range(n):
            rows = g[a].shape[0]
            chunk = math.gcd(rows, 128)

            def step(i, _, a=a, chunk=chunk):
                r = pl.ds(pl.multiple_of(i * chunk, chunk), chunk)
                grad = g[a][r, :]
                delta, nm, nv = _adamw(w[a][r, :], grad, m[a][r, :], v[a][r, :])
                outs[4 * a][r, :] = grad
                outs[4 * a + 1][r, :] = delta
                outs[4 * a + 2][r, :] = nm
                outs[4 * a + 3][r, :] = nv
                return 0

            lax.fori_loop(0, rows // chunk, step, 0)

    vmem = pl.BlockSpec(memory_space=pltpu.VMEM)
    out_shape = []
    for wa in ws:
        out_shape += [jax.ShapeDtypeStruct(wa.shape, F32)] * 4
    res = pl.pallas_call(
        body,
        name="adamw_all",
        out_shape=out_shape,
        in_specs=[vmem] * (4 * n),
        out_specs=[vmem] * (4 * n),
        compiler_params=_params(),
    )(*gs, *ws, *ms, *vs)
    return [tuple(res[4 * a : 4 * a + 4]) for a in range(n)]


def _reduce_all(parts, small, early):
    n, ne = len(parts), len(early)
    parts4 = [p.reshape(4, 2, *p.shape[1:]) for p in parts]
    blks = [p.shape[1:] for p in parts]

    def body(*refs):
        refs = list(refs)
        take = lambda k: [refs.pop(0) for _ in range(k)]
        part = take(n)
        (small_ref,) = take(1)
        early_in = [take(2) for _ in range(ne)]
        outs = take(n)
        (small_all,) = take(1)
        early_out = take(ne)
        own, r1, got_a1, got_a2, got_b1, got_b2, pass_a, pass_b = (take(n) for _ in range(8))
        early_buf = take(ne)
        small_land, small_mine = take(2)
        s1_send, s1_recv, h_send, h_recv, loc, rs_send, rs_recv, ag_send, ag_recv, early_sems = refs
        x, y, c = _place()
        me = 4 * x + 2 * y + c
        piece = small_mine.shape[0]
        piece_of = lambda slot: pl.ds(pl.multiple_of(slot * piece, SUBLANES), piece)
        rs_start, rs_finish = _exchange_steps(
            1, lambda a, slot: small_ref.at[piece_of(slot)], [small_land], rs_send, rs_recv)
        ag_start, ag_finish = _exchange_steps(1, lambda a, slot: small_mine, [small_all], ag_send, ag_recv)
        landed = [pltpu.make_async_copy(early_in[e][1], early_buf[e], early_sems.at[e, 0]) for e in range(ne)]
        for cp in landed:
            cp.start()
        sibling = (x, y, 1 - c)
        chips = [(1 - x, y), (x, 1 - y), (1 - x, 1 - y)]

        def rowwise(rows, fn):
            chunk = math.gcd(rows, 128)

            def step(i, _):
                fn(pl.ds(pl.multiple_of(i * chunk, chunk), chunk))
                return 0

            lax.fori_loop(0, rows // chunk, step, 0)

        stage1, local = [], []
        for a in range(n):
            cp = pltpu.make_async_remote_copy(
                src_ref=part[a].at[:, 1 - c], dst_ref=r1[a], send_sem=s1_send.at[a], recv_sem=s1_recv.at[a],
                device_id=sibling, device_id_type=MESH)
            cp.start()
            stage1.append(cp)
            lc = pltpu.make_async_copy(part[a].at[:, c], own[a], loc.at[a])
            lc.start()
            local.append(lc)
        rs_start()
        x_nbr, y_nbr = (1 - x, y, c), (x, 1 - y, c)
        mine, mine_x, mine_y = 2 * x + y, 2 * (1 - x) + y, 2 * x + (1 - y)

        def hop(a, k, src, dst, to):
            return pltpu.make_async_remote_copy(
                src_ref=src, dst_ref=dst, send_sem=h_send.at[a, k], recv_sem=h_recv.at[a, k],
                device_id=to, device_id_type=MESH)

        first, second = [], []
        for a in range(n):
            half = blks[a][0] // 2
            up, low = pl.ds(0, half), pl.ds(half, half)
            local[a].wait()
            stage1[a].wait_recv()
            for chip in range(4):

                def add(r, a=a, chip=chip):
                    own[a][chip, r, :] = own[a][chip, r, :] + r1[a][chip, r, :]

                rowwise(blks[a][0], add)
            first.append([
                hop(a, 0, own[a].at[pl.ds(2 * (1 - x), 2), up], got_a1[a], x_nbr),
                hop(a, 2, own[a].at[2 * x + (1 - y), low], got_b1[a].at[x], y_nbr),
                hop(a, 3, own[a].at[2 * (1 - x) + (1 - y), low], got_b1[a].at[1 - x], y_nbr),
            ])
            for cp in first[a]:
                cp.start()
        rs_finish()
        small_land[me] = small_ref[piece_of(me), :]

        def sum_piece(i, _):
            r = pl.ds(pl.multiple_of(i * SUBLANES, SUBLANES), SUBLANES)
            total = small_land[0, r, :]
            for dev in range(1, N_DEV):
                total = total + small_land[dev, r, :]
            small_mine[r, :] = total
            small_all[me, r, :] = total
            return 0

        lax.fori_loop(0, piece // SUBLANES, sum_piece, 0)
        ag_start()
        for a in range(n):
            half = blks[a][0] // 2
            first[a][0].wait_recv()

            def fold_upper(r, a=a):
                own[a][mine, r, :] = own[a][mine, r, :] + got_a1[a][y, r, :]
                pass_a[a][r, :] = own[a][mine_y, r, :] + got_a1[a][1 - y, r, :]

            rowwise(half, fold_upper)
            first[a][1].wait_recv()
            first[a][2].wait_recv()

            def fold_lower(r, a=a, half=half):
                rl = pl.ds(pl.multiple_of(r.start + half, SUBLANES), r.size)
                own[a][mine, rl, :] = own[a][mine, rl, :] + got_b1[a][x, r, :]
                pass_b[a][r, :] = own[a][mine_x, rl, :] + got_b1[a][1 - x, r, :]

            rowwise(half, fold_lower)
            second.append([hop(a, 1, pass_a[a], got_a2[a], y_nbr), hop(a, 4, pass_b[a], got_b2[a], x_nbr)])
            for cp in second[a]:
                cp.start()
        for e in range(ne):
            part_e, _ = early_in[e]
            landed[e].wait()
            own_block = pltpu.make_async_copy(part_e.at[me], early_buf[e].at[me], early_sems.at[e, 1])
            own_block.start()
            own_block.wait()

            def sum_early(r, e=e):
                g = early_buf[e][0, r, :]
                for dev in range(1, N_DEV):
                    g = g + early_buf[e][dev, r, :]
                early_out[e][r, :] = g

            rowwise(early_buf[e].shape[1], sum_early)
        for a in range(n):
            half = blks[a][0] // 2
            second[a][0].wait_recv()
            second[a][1].wait_recv()

            def finish_rows(r, a=a, half=half):
                rl = pl.ds(pl.multiple_of(r.start + half, SUBLANES), r.size)
                outs[a][r, :] = own[a][mine, r, :] + got_a2[a][r, :]
                outs[a][rl, :] = own[a][mine, rl, :] + got_b2[a][r, :]

            rowwise(half, finish_rows)
        ag_finish()
        for cp in stage1 + [cp for group in first + second for cp in group]:
            cp.wait_send()

    vmem = pl.BlockSpec(memory_space=pltpu.VMEM)
    hbm = pl.BlockSpec(memory_space=pl.ANY)
    piece = small.shape[0] // N_DEV
    assert piece * N_DEV == small.shape[0] and piece % SUBLANES == 0
    out_shape = [jax.ShapeDtypeStruct(b, F32) for b in blks]
    out_shape += [jax.ShapeDtypeStruct((N_DEV, piece, LANES), F32)]
    out_shape += [jax.ShapeDtypeStruct(e[0].shape[1:], F32) for e in early]
    halves = [(b[0] // 2, b[1]) for b in blks]
    scratch = (
        [pltpu.VMEM((4, *b), F32) for b in blks]
        + [pltpu.VMEM((4, *b), F32) for b in blks]
        + [pltpu.VMEM((2, *h), F32) for h in halves]
        + [pltpu.VMEM(h, F32) for h in halves]
        + [pltpu.VMEM((2, *h), F32) for h in halves]
        + [pltpu.VMEM(h, F32) for h in halves] * 3
        + [pltpu.VMEM(e[0].shape, F32) for e in early]
        + [pltpu.VMEM((N_DEV, piece, LANES), F32), pltpu.VMEM((piece, LANES), F32)]
        + [pltpu.SemaphoreType.DMA((n,)), pltpu.SemaphoreType.DMA((n,)), pltpu.SemaphoreType.DMA((n, 5)),
           pltpu.SemaphoreType.DMA((n, 5)), pltpu.SemaphoreType.DMA((n,))]
        + [pltpu.SemaphoreType.DMA((1, 7))] * 4
        + [pltpu.SemaphoreType.DMA((ne, 2))]
    )
    res = pl.pallas_call(
        body,
        name="reduce_all",
        out_shape=out_shape,
        in_specs=[hbm] * n + [vmem] + [hbm, hbm] * ne,
        out_specs=[vmem] * (n + 1 + ne),
        scratch_shapes=scratch,
        compiler_params=_params(),
    )(*parts4, small, *[t for e in early for t in e])
    return list(res[:n]) + list(res[n + 1 :]), res[n].reshape(small.shape)


def _rms(x):
    return lax.rsqrt(jnp.mean(x * x, axis=-1, keepdims=True) + EPS)


def _token_tile(tokens, want):
    tile = min(want, tokens // 2)
    assert tokens % tile == 0 and tile % 16 == 0
    return tile


def _in_proj(x2, g_pre, w_in_blk, shards):
    tokens, d = x2.shape
    nb = w_in_blk.shape[1]
    tm = _token_tile(tokens, IN_PROJ_ROWS)
    n_t = tokens // tm
    ns = len(shards)
    x_pos, y_pos, c_pos = _place()
    slot = lambda px, py, pc: 4 * px + 2 * py + pc
    chip_order = [(x_pos, y_pos), (1 - x_pos, y_pos), (x_pos, 1 - y_pos), (1 - x_pos, 1 - y_pos)]
    order = jnp.stack([slot(px, py, pc) for px, py in chip_order for pc in (c_pos, 1 - c_pos)]).astype(jnp.int32)

    def body(order_ref, x_ref, g_ref, w_ref, *rest):
        shard_hbm, proj_ref, w_hbm = rest[:ns], rest[ns], rest[ns + 1]
        gathered = rest[ns + 2 : 2 * ns + 2]
        h_all, land, w_send, w_recv, out_sem, send_sems, recv_sems, own_sems = rest[2 * ns + 2 :]
        j, i = pl.program_id(0), pl.program_id(1)
        x, y, c = _place()
        me, sibling = (x, y, c), (x, y, 1 - c)
        chips = [(1 - x, y), (x, 1 - y), (1 - x, 1 - y)]
        own = [pltpu.make_async_copy(shard_hbm[a], gathered[a].at[slot(*me)], own_sems.at[a]) for a in range(ns)]
        start, finish = _exchange_steps(ns, lambda a, s: shard_hbm[a], gathered, send_sems, recv_sems)

        def copy(k, block, to):
            ref = land.at[slot(*block)]
            return pltpu.make_async_remote_copy(
                src_ref=ref, dst_ref=ref, send_sem=w_send.at[k], recv_sem=w_recv.at[k], device_id=to, device_id_type=MESH)

        first_sends = [copy(0, me, sibling)] + [copy(1 + k, me, (*chip, c)) for k, chip in enumerate(chips)]
        forwards = [copy(4 + k, (*chip, c), sibling) for k, chip in enumerate(chips)]

        @pl.when((j == 0) & (i == 0))
        def _():
            land[slot(*me)] = w_ref[...].astype(BF16)
            for cp in first_sends:
                cp.start()
            start()
            for cp in own:
                cp.start()

        @pl.when((j == 1) & (i == 0))
        def _():
            copy(0, sibling, me).wait_recv()

        for k, chip in enumerate(chips):

            @pl.when((j == 2 + 2 * k) & (i == 0))
            def _(k=k, chip=chip):
                copy(1 + k, (*chip, c), me).wait_recv()
                forwards[k].start()

            @pl.when((j == 3 + 2 * k) & (i == 0))
            def _(k=k, chip=chip):
                copy(4 + k, (*chip, 1 - c), me).wait_recv()

        rows = pl.ds(pl.multiple_of(i * tm, tm), tm)

        @pl.when(j == 0)
        def _():
            x_t = x_ref[...]
            h_all[rows, :] = (x_t * _rms(x_t) * g_ref[...]).astype(BF16)

        proj_ref[...] = _mm(h_all[rows, :], land[order_ref[j]])

        @pl.when((j == N_DEV - 1) & (i == n_t - 1))
        def _():
            for cp in first_sends + forwards:
                cp.wait_send()
            finish()
            for cp in own:
                cp.wait()
            out = pltpu.make_async_copy(land, w_hbm, out_sem)
            out.start()
            out.wait()

    hbm = pl.BlockSpec(memory_space=pl.ANY)
    res = pl.pallas_call(
        body,
        name="in_proj",
        grid_spec=pltpu.PrefetchScalarGridSpec(
            num_scalar_prefetch=1,
            grid=(N_DEV, n_t),
            in_specs=[
                pl.BlockSpec((tm, d), lambda j, i, order: (jnp.where(j == 0, i, n_t - 1), 0)),
                pl.BlockSpec((1, d), lambda j, i, order: (0, 0)),
                pl.BlockSpec((d, nb), lambda j, i, order: (0, 0)),
            ]
            + [hbm] * ns,
            out_specs=[pl.BlockSpec((tm, nb), lambda j, i, order: (i, order[j])), hbm] + [hbm] * ns,
            scratch_shapes=[
                pltpu.VMEM((tokens, d), BF16),
                pltpu.VMEM((N_DEV, d, nb), BF16),
                pltpu.SemaphoreType.DMA((7,)),
                pltpu.SemaphoreType.DMA((7,)),
                pltpu.SemaphoreType.DMA,
                pltpu.SemaphoreType.DMA((ns, 7)),
                pltpu.SemaphoreType.DMA((ns, 7)),
                pltpu.SemaphoreType.DMA((ns,)),
            ],
        ),
        out_shape=[
            jax.ShapeDtypeStruct((tokens, N_DEV * nb), F32),
            jax.ShapeDtypeStruct((N_DEV, d, nb), BF16),
        ]
        + [jax.ShapeDtypeStruct((N_DEV, *a.shape), a.dtype) for a in shards],
        compiler_params=_params("arbitrary", "arbitrary"),
    )(order, x2, g_pre, w_in_blk, *shards)
    return res[0], res[1], res[2:]


def _in_proj_bwd(x2, dres, du_pool, du_ssm, dq, dgate, g_pre, w_in_t):
    tokens, d = x2.shape
    nb = w_in_t.shape[0] // N_DEV
    pool_w, ssm_w, att_w, mix = du_pool.shape[1], du_ssm.shape[1], dq.shape[1], dgate.shape[1]
    cl = _token_tile(tokens, TOKEN_ROWS)
    n_tiles = tokens // cl

    def body(x_ref, dres_ref, dup_ref, dus_ref, dq_ref, dgate_ref, g_ref, w_ref, gx_ref, gw_hbm, gg_ref, acc, sem):
        i = pl.program_id(0)

        @pl.when(i == 0)
        def _():
            acc[...] = jnp.zeros_like(acc)
            gg_ref[...] = jnp.zeros_like(gg_ref)

        x = x_ref[...]
        r = _rms(x)
        xn = x * r
        g = g_ref[...]
        h = (xn * g).astype(BF16)
        dproj = jnp.concatenate([dup_ref[...], dus_ref[...], dq_ref[...], dgate_ref[...]], axis=1).astype(BF16)
        dh = _mm(dproj, w_ref[...])
        for j in range(N_DEV):
            acc[j] += _mm_tn(h, dproj[:, j * nb : (j + 1) * nb])
        gg_ref[...] += jnp.sum(dh * xn, axis=0, keepdims=True)
        dxn = dh * g
        gx_ref[...] = dres_ref[...] + r * (dxn - xn * jnp.mean(dxn * xn, axis=-1, keepdims=True))

        @pl.when(i == n_tiles - 1)
        def _():
            cp = pltpu.make_async_copy(acc, gw_hbm, sem)
            cp.start()
            cp.wait()

    return pl.pallas_call(
        body,
        name="in_proj_bwd",
        grid=(n_tiles,),
        in_specs=[
            pl.BlockSpec((cl, d), lambda i: (i, 0)),
            pl.BlockSpec((cl, d), lambda i: (i, 0)),
            pl.BlockSpec((cl, pool_w), lambda i: (i, 0)),
            pl.BlockSpec((cl, ssm_w), lambda i: (i, 0)),
            pl.BlockSpec((cl, att_w), lambda i: (i, 0)),
            pl.BlockSpec((cl, mix), lambda i: (i, 0)),
            pl.BlockSpec((1, d), lambda i: (0, 0)),
            pl.BlockSpec(w_in_t.shape, lambda i: (0, 0)),
        ],
        out_specs=[
            pl.BlockSpec((cl, d), lambda i: (i, 0)),
            pl.BlockSpec(memory_space=pl.ANY),
            pl.BlockSpec((1, d), lambda i: (0, 0)),
        ],
        out_shape=[
            jax.ShapeDtypeStruct((tokens, d), F32),
            jax.ShapeDtypeStruct((N_DEV, d, nb), F32),
            jax.ShapeDtypeStruct((1, d), F32),
        ],
        scratch_shapes=[pltpu.VMEM((N_DEV, d, nb), F32), pltpu.SemaphoreType.DMA],
        compiler_params=_params("arbitrary"),
    )(x2, dres, du_pool, du_ssm, dq, dgate, g_pre, w_in_t)


def _pool_geometry(seq, width):
    gw = width // len(POOL_WINDOWS)
    col = lax.broadcasted_iota(jnp.int32, (1, width), 1)
    win = jnp.full((1, width), float(POOL_WINDOWS[-1]), F32)
    for gi in range(len(POOL_WINDOWS) - 2, -1, -1):
        win = jnp.where(col < (gi + 1) * gw, float(POOL_WINDOWS[gi]), win)
    row = lax.broadcasted_iota(jnp.int32, (seq, width), 0)
    filling = 1.0 / (lax.broadcasted_iota(jnp.int32, (seq, 1), 0) + 1).astype(F32)
    inv_cnt = jnp.where(row + 1 < win.astype(jnp.int32), filling, 1.0 / win)
    return win, row, inv_cnt


def _window_sums(a, win, seq, back):
    pad = 2 * POOL_WINDOWS[-1]
    zeros = jnp.zeros((pad, a.shape[1]), F32)
    s = jnp.concatenate([a, zeros] if back else [zeros, a], axis=0)
    sums = []
    k = 1
    while k < POOL_WINDOWS[-1]:
        s = s + pltpu.roll(s, seq + pad - k if back else k, 0)
        k *= 2
        sums.append((k, s))
    out = sums[-1][1]
    for k, s in reversed(sums[:-1]):
        out = jnp.where(win <= float(k), s, out)
    return out[0:seq] if back else out[pad : pad + seq]


def _pool_fwd(u2, wp_blk, scale, batch, seq):
    width = scale.shape[1]

    def body(u_ref, w_ref, s_ref, y_ref, diff_ref):
        u = u_ref[...]
        win, row, inv_cnt = _pool_geometry(seq, width)
        diff = (_window_sums(u, win, seq, False) * inv_cnt - u).astype(BF16)
        diff_ref[...] = diff
        y_ref[...] = _mm(diff, w_ref[...]) * s_ref[...]

    return pl.pallas_call(
        body,
        name="pool_fwd",
        grid=(batch,),
        in_specs=[
            pl.BlockSpec((seq, width), lambda b: (b, 0)),
            pl.BlockSpec((width, width), lambda b: (0, 0)),
            pl.BlockSpec((1, width), lambda b: (0, 0)),
        ],
        out_specs=[pl.BlockSpec((seq, width), lambda b: (b, 0)), pl.BlockSpec((seq, width), lambda b: (b, 0))],
        out_shape=[
            jax.ShapeDtypeStruct((u2.shape[0], width), F32),
            jax.ShapeDtypeStruct((u2.shape[0], width), BF16),
        ],
        compiler_params=_params("arbitrary"),
    )(u2, wp_blk, scale)


def _pool_bwd(diff2, dy2, wp_blk, scale, batch, seq):
    width = scale.shape[1]

    def body(diff_ref, dy_ref, w_ref, s_ref, du_ref, gw_ref, gs_ref):
        @pl.when(pl.program_id(0) == 0)
        def _():
            gw_ref[...] = jnp.zeros_like(gw_ref)
            gs_ref[...] = jnp.zeros_like(gs_ref)

        diff = diff_ref[...]
        dy = dy_ref[...]
        win, row, inv_cnt = _pool_geometry(seq, width)
        gs_ref[...] += jnp.sum(dy * _mm(diff, w_ref[...]), axis=0, keepdims=True)
        dys = (dy * s_ref[...]).astype(BF16)
        gw_ref[...] += _mm_tn(diff, dys)
        dd = _mm_nt(dys, w_ref[...])
        du_ref[...] = _window_sums(dd * inv_cnt, win, seq, True) - dd

    return pl.pallas_call(
        body,
        name="pool_bwd",
        grid=(batch,),
        in_specs=[
            pl.BlockSpec((seq, width), lambda b: (b, 0)),
            pl.BlockSpec((seq, width), lambda b: (b, 0)),
            pl.BlockSpec((width, width), lambda b: (0, 0)),
            pl.BlockSpec((1, width), lambda b: (0, 0)),
        ],
        out_specs=[
            pl.BlockSpec((seq, width), lambda b: (b, 0)),
            pl.BlockSpec((width, width), lambda b: (0, 0)),
            pl.BlockSpec((1, width), lambda b: (0, 0)),
        ],
        out_shape=[
            jax.ShapeDtypeStruct(dy2.shape, F32),
            jax.ShapeDtypeStruct((width, width), F32),
            jax.ShapeDtypeStruct((1, width), F32),
        ],
        compiler_params=_params("arbitrary"),
    )(diff2, dy2, wp_blk, scale)


def _state_row(z, n_blocks):
    re = jnp.real(z).reshape(n_blocks, -1)
    im = jnp.imag(z).reshape(n_blocks, -1)
    return jnp.concatenate([re, im], axis=1).reshape(1, -1)


def _ssm_tables(a_re, a_im, log_dt, b_re, b_im, c_re, c_im):
    groups, n_state = a_re.shape
    ch = b_re.shape[2]
    nb = groups * ch // LANES
    gl = groups // nb
    lam = lax.complex(a_re, a_im)
    lam_bar = jnp.exp(lam * jnp.exp(log_dt)[:, None])
    b_bar = ((lam_bar - 1.0) / lam)[..., None] * lax.complex(b_re, b_im)
    eye = jnp.eye(gl, dtype=F32)

    def rows_to_state(t):
        return jnp.einsum("sgnc,gh->sgchn", t.reshape(nb, gl, n_state, ch), eye).reshape(nb, gl * ch, gl * n_state)

    def state_to_rows(t):
        return jnp.einsum("sgcn,gh->shngc", t.reshape(nb, gl, ch, n_state), eye).reshape(nb, gl * n_state, gl * ch)

    b_tab = jnp.concatenate([rows_to_state(jnp.real(b_bar)), rows_to_state(jnp.imag(b_bar))], axis=2)
    c_tab = jnp.concatenate([state_to_rows(c_re), -state_to_rows(c_im)], axis=1)
    return _state_row(lam_bar, nb), b_tab, c_tab


def _lam_power(a_re, a_im, log_dt, power, scale, n_blocks):
    return _state_row(scale * jnp.exp(lax.complex(a_re, a_im) * jnp.exp(log_dt)[:, None] * power), n_blocks)


def _state_blocks(s2, n_blocks, width):
    half = s2 // n_blocks // 2
    assert half % width == 0
    return [(b * 2 * half + o, b * 2 * half + half + o) for b in range(n_blocks) for o in range(0, half, width)]


def _scan(src_ref, dst_ref, st_ref, lam8_ref, n_groups, s, n_blocks, reverse, store):
    lb = SCAN_LANES
    for re0, im0 in _state_blocks(2 * s, n_blocks, lb):
        cr, ci = pl.ds(re0, lb), pl.ds(im0, lb)
        lr = lam8_ref[:, cr]
        li = -lam8_ref[:, ci] if reverse else lam8_ref[:, ci]

        def step(i, carry, cr=cr, ci=ci, lr=lr, li=li):
            hr, hi = carry
            grp = n_groups - 1 - i if reverse else i
            rows = pl.ds(pl.multiple_of(grp * SUBLANES, SUBLANES), SUBLANES)
            nr = lr * hr - li * hi + src_ref[rows, cr]
            ni = lr * hi + li * hr + src_ref[rows, ci]
            if store:
                dst_ref[rows, cr] = nr
                dst_ref[rows, ci] = ni
            return nr, ni

        hr, hi = lax.fori_loop(0, n_groups, step, (st_ref[:, cr], st_ref[:, ci]), unroll=2)
        st_ref[:, cr] = hr
        st_ref[:, ci] = hi


def _pack_state(re, im):
    hi = lax.bitcast_convert_type(re.astype(BF16).astype(F32), jnp.uint32)
    lo = lax.bitcast_convert_type(im.astype(BF16).astype(F32), jnp.uint32)
    return hi | (lo >> 16)


def _unpack_state(word):
    re = lax.bitcast_convert_type(word & jnp.uint32(0xFFFF0000), F32)
    im = lax.bitcast_convert_type(word << 16, F32)
    return re, im


def _scan_adjoint(dh_ref, hprev_ref, group0, stg_ref, acc_ref, lam8_ref, n_groups, s, n_blocks):
    lb = SCAN_LANES
    half = s // n_blocks
    for re0, im0 in _state_blocks(2 * s, n_blocks, lb):
        cr, ci = pl.ds(re0, lb), pl.ds(im0, lb)
        ch = pl.ds(re0 // (2 * half) * half + re0 % (2 * half), lb)
        lr, li = lam8_ref[:, cr], -lam8_ref[:, ci]

        def step(i, carry, cr=cr, ci=ci, ch=ch, lr=lr, li=li):
            gr, gi, ar, ai = carry
            grp = n_groups - 1 - i
            rows = pl.ds(pl.multiple_of(grp * SUBLANES, SUBLANES), SUBLANES)
            ngr = lr * gr - li * gi + dh_ref[rows, cr]
            ngi = lr * gi + li * gr + dh_ref[rows, ci]
            hr, hi = _unpack_state(hprev_ref[pl.ds(pl.multiple_of((group0 + grp) * SUBLANES, SUBLANES), SUBLANES), ch])
            ar = ar + hr * ngr + hi * ngi
            ai = ai + hr * ngi - hi * ngr
            dh_ref[rows, cr] = ngr
            dh_ref[rows, ci] = ngi
            return ngr, ngi, ar, ai

        init = (stg_ref[:, cr], stg_ref[:, ci], acc_ref[:, cr], acc_ref[:, ci])
        gr, gi, ar, ai = lax.fori_loop(0, n_groups, step, init)
        stg_ref[:, cr] = gr
        stg_ref[:, ci] = gi
        acc_ref[:, cr] = ar
        acc_ref[:, ci] = ai


def _chunk_starts(st_ref, init_ref, lcl_ref, s, n_blocks):
    w = s // n_blocks
    init_ref[0:1, :] = jnp.zeros((1, 2 * s), F32)
    for re0, im0 in _state_blocks(2 * s, n_blocks, w):
        re, im = pl.ds(re0, w), pl.ds(im0, w)
        ar, ai = lcl_ref[:, re], lcl_ref[:, im]
        cr = jnp.zeros((1, w), F32)
        ci = jnp.zeros((1, w), F32)
        for k in range(1, NCH):
            cr, ci = (ar * cr - ai * ci + st_ref[k - 1 : k, re], ar * ci + ai * cr + st_ref[k - 1 : k, im])
            init_ref[k : k + 1, re] = cr
            init_ref[k : k + 1, im] = ci


def _chunk_starts_adjoint(stg_ref, initg_ref, lcl_ref, s, n_blocks):
    w = s // n_blocks
    initg_ref[NCH - 1 : NCH, :] = jnp.zeros((1, 2 * s), F32)
    for re0, im0 in _state_blocks(2 * s, n_blocks, w):
        re, im = pl.ds(re0, w), pl.ds(im0, w)
        ar, ai = lcl_ref[:, re], -lcl_ref[:, im]
        gr = jnp.zeros((1, w), F32)
        gi = jnp.zeros((1, w), F32)
        for k in range(NCH - 2, -1, -1):
            gr, gi = (stg_ref[k + 1 : k + 2, re] + ar * gr - ai * gi, stg_ref[k + 1 : k + 2, im] + ar * gi + ai * gr)
            initg_ref[k : k + 1, re] = gr
            initg_ref[k : k + 1, im] = gi


def _ssm_rows(seq, want):
    rows = min(want, seq // 2)
    assert seq % rows == 0 and rows % SUBLANES == 0
    return rows


def _chunk_copies(hbm_ref, b, cm_ref, sems, to_cm, col0=0):
    cl, _, width = cm_ref.shape
    copies = []
    for k in range(NCH):
        nat, cm = hbm_ref.at[b, pl.ds(k * cl, cl), pl.ds(col0, width)], cm_ref.at[:, k, :]
        src, dst = (nat, cm) if to_cm else (cm, nat)
        copies.append(pltpu.make_async_copy(src, dst, sems.at[k]))
    return copies


def _blockwise(fn, n_blocks):
    return jnp.concatenate([fn(b) for b in range(n_blocks)], axis=1)


def _ssm_fwd(u, u_col, b_tab, c_tab, lam8, lcl, d_skip, w_glu, shards):
    batch, seq, _ = u.shape
    ns = len(shards)
    width = d_skip.shape[1]
    s = lam8.shape[1] // 2
    nb = b_tab.shape[0]
    sb = 2 * s // nb
    cl = seq // NCH
    rows = _ssm_rows(seq, SSM_ROWS)
    n_tiles = seq // rows
    n_groups = rows // SUBLANES

    def body(u_hbm, b_ref, c_ref, lam_ref, lcl_ref, d_ref, wg_ref, *rest):
        shard_hbm, (y_hbm, pre_ref, z_ref, init_ref) = rest[:ns], rest[ns : ns + 4]
        gathered = rest[ns + 4 : 2 * ns + 4]
        u_cm, y_cm, bu_all, st, sems, send_sems, recv_sems, own_sems = rest[2 * ns + 4 :]
        b, ph, t = pl.program_id(0), pl.program_id(1), pl.program_id(2)
        tile_groups = pl.ds(pl.multiple_of(t * n_groups, n_groups), n_groups)
        x_pos, y_pos, c_pos = _place()
        own = [pltpu.make_async_copy(shard_hbm[a], gathered[a].at[4 * x_pos + 2 * y_pos + c_pos], own_sems.at[a])
               for a in range(ns)]
        exchange_start, exchange_finish = _exchange_steps(
            ns, lambda a, slot: shard_hbm[a], gathered, send_sems, recv_sems)

        @pl.when((b == 0) & (ph == 0) & (t == 0))
        def _():
            exchange_start()
            for cp in own:
                cp.start()

        @pl.when((b == batch - 1) & (ph == 1) & (t == n_tiles - 1))
        def _():
            exchange_finish()
            for cp in own:
                cp.wait()

        @pl.when((ph == 0) & (t == 0))
        def _():
            loads = _chunk_copies(u_hbm, b, u_cm, sems, True, u_col)
            for cp in loads:
                cp.start()
            st[...] = jnp.zeros_like(st)
            for cp in loads:
                cp.wait()

        @pl.when((ph == 1) & (t == 0))
        def _():
            st[...] = init_ref[...]

        u_t = u_cm[tile_groups].reshape(rows, width)
        bu = bu_all.at[pl.ds(pl.multiple_of(t * rows, rows), rows)]

        @pl.when(ph == 0)
        def _():
            u_b = u_t.astype(BF16)
            for blk in range(nb):
                bu[:, blk * sb : (blk + 1) * sb] = _mm(u_b[:, blk * LANES : (blk + 1) * LANES], b_ref[blk])
            _scan(bu, bu, st, lam_ref, n_groups, s, nb, False, False)

        @pl.when((ph == 0) & (t == n_tiles - 1))
        def _():
            _chunk_starts(st, init_ref, lcl_ref, s, nb)

        @pl.when(ph == 1)
        def _():
            _scan(bu, bu, st, lam_ref, n_groups, s, nb, False, True)
            hs = lambda blk: _mm(bu[:, blk * sb : (blk + 1) * sb].astype(BF16), c_ref[blk])
            pre = _blockwise(hs, nb) + d_ref[...] * u_t
            z = _mm(_gelu(pre).astype(BF16), wg_ref[...])
            pre_ref[...] = pre
            z_ref[...] = z
            y = z[:, 0:width] * jax.nn.sigmoid(z[:, width : 2 * width])
            y_cm[tile_groups] = y.reshape(n_groups, SUBLANES, width)

        @pl.when((ph == 1) & (t == n_tiles - 1))
        def _():
            stores = _chunk_copies(y_hbm, b, y_cm, sems, False)
            for cp in stores:
                cp.start()
            for cp in stores:
                cp.wait()

    out_tile = lambda b, ph, t: (b, t * ph, 0)
    full = lambda a: pl.BlockSpec(a.shape, lambda b, ph, t: (0,) * a.ndim)
    hbm = pl.BlockSpec(memory_space=pl.ANY)
    res = pl.pallas_call(
        body,
        name="ssm_fwd",
        grid=(batch, 2, n_tiles),
        in_specs=[hbm, full(b_tab), full(c_tab), full(lam8), full(lcl), full(d_skip), full(w_glu)] + [hbm] * ns,
        out_specs=[
            hbm,
            pl.BlockSpec((None, rows, width), out_tile),
            pl.BlockSpec((None, rows, 2 * width), out_tile),
            pl.BlockSpec((None, SUBLANES, 2 * s), lambda b, ph, t: (b, 0, 0)),
        ]
        + [hbm] * ns,
        out_shape=[
            jax.ShapeDtypeStruct((batch, seq, width), F32),
            jax.ShapeDtypeStruct((batch, seq, width), F32),
            jax.ShapeDtypeStruct((batch, seq, 2 * width), F32),
            jax.ShapeDtypeStruct((batch, SUBLANES, 2 * s), F32),
        ]
        + [jax.ShapeDtypeStruct((N_DEV, *a.shape), a.dtype) for a in shards],
        scratch_shapes=[
            pltpu.VMEM((cl, NCH, width), F32),
            pltpu.VMEM((cl, NCH, width), F32),
            pltpu.VMEM((seq, 2 * s), F32),
            pltpu.VMEM((SUBLANES, 2 * s), F32),
            pltpu.SemaphoreType.DMA((NCH,)),
            pltpu.SemaphoreType.DMA((ns, 7)),
            pltpu.SemaphoreType.DMA((ns, 7)),
            pltpu.SemaphoreType.DMA((ns,)),
        ],
        compiler_params=_params("arbitrary", "arbitrary", "arbitrary"),
    )(u, b_tab, c_tab, lam8, lcl, d_skip, w_glu, *shards)
    return res[:4], res[4:]


def _ssm_bwd(u, u_col, pre_p, z_p, dy, init, b_tab, b_tab_t, c_tab_t, lam8, lcl, d_skip, w_glu, ready):
    batch, seq, _ = u.shape
    width = d_skip.shape[1]
    nr = len(ready)
    s = lam8.shape[1] // 2
    nb = b_tab.shape[0]
    sb = 2 * s // nb
    cl = seq // NCH
    rows = _ssm_rows(seq, SSM_ROWS)
    n_tiles = seq // rows
    n_groups = rows // SUBLANES

    def body(u_hbm, pre_ref, z_ref, dy_hbm, init_ref, b_ref, bt_ref, ct_ref, lam_ref, lcl_ref, d_ref, wg_ref, *rest):
        ready_hbm, rest = rest[:nr], rest[nr:]
        du_hbm, gb_ref, gc_ref, gwg_ref, gd_ref, glam_ref = rest[:6]
        landed_hbm, rest = rest[6 : 6 + nr], rest[6 + nr :]
        u_cm, dy_cm, work, hs_all, dpre_all, st, stg, initg, acc, sems, send_sems, recv_sems = rest
        b, ph, t = pl.program_id(0), pl.program_id(1), pl.program_id(2)
        half = s // nb
        exchange_start, exchange_finish = _exchange_steps(
            nr, lambda a, slot: ready_hbm[a].at[slot], landed_hbm, send_sems, recv_sems)
        first = (b == 0) & (ph == 0) & (t == 0)
        last = (b == batch - 1) & (ph == 2) & (t == n_tiles - 1)
        tile = jnp.where(ph == 0, t, n_tiles - 1 - t)
        tile_rows = pl.ds(pl.multiple_of(tile * rows, rows), rows)
        tile_groups = pl.ds(pl.multiple_of(tile * n_groups, n_groups), n_groups)
        lanes = lambda blk: slice(blk * LANES, (blk + 1) * LANES)
        states = lambda blk: slice(blk * sb, (blk + 1) * sb)

        @pl.when(first)
        def _():
            exchange_start()
            acc[...] = jnp.zeros_like(acc)
            gb_ref[...] = jnp.zeros_like(gb_ref)
            gc_ref[...] = jnp.zeros_like(gc_ref)
            gwg_ref[...] = jnp.zeros_like(gwg_ref)
            gd_ref[...] = jnp.zeros_like(gd_ref)

        @pl.when((ph == 0) & (t == 0))
        def _():
            loads = (_chunk_copies(u_hbm, b, u_cm, sems.at[0], True, u_col)
                     + _chunk_copies(dy_hbm, b, dy_cm, sems.at[1], True))
            for cp in loads:
                cp.start()
            st[...] = init_ref[...]
            for blk in range(nb):
                entry = init_ref[:, states(blk)]
                hs_all[0:SUBLANES, blk * half : (blk + 1) * half] = _pack_state(entry[:, 0:half], entry[:, half : 2 * half])
            for cp in loads:
                cp.wait()

        u_t = u_cm[tile_groups].reshape(rows, width)
        u_b = u_t.astype(BF16)

        @pl.when(ph == 0)
        def _():
            for blk in range(nb):
                work[:, states(blk)] = _mm(u_b[:, lanes(blk)], b_ref[blk])
            _scan(work, work, st, lam_ref, n_groups, s, nb, False, True)
            z = z_ref[...]
            dy_t = dy_cm[tile_groups].reshape(rows, width)
            pre = pre_ref[...]
            z1, sig = z[:, 0:width], jax.nn.sigmoid(z[:, width : 2 * width])
            dz = jnp.concatenate([dy_t * sig, dy_t * z1 * sig * (1.0 - sig)], axis=1).astype(BF16)
            gwg_ref[...] += _mm_tn(_gelu(pre).astype(BF16), dz)
            dpre = _mm_nt(dz, wg_ref[...]) * _gelu_grad(pre)
            dpre_all[tile_rows, :] = dpre
            gd_ref[...] += jnp.sum(dpre * u_t, axis=0, keepdims=True)
            dpre_b = dpre.astype(BF16)
            kept = pl.ds(pl.multiple_of(tile * rows + SUBLANES, SUBLANES), rows)
            for blk in range(nb):
                hs = work[:, states(blk)]
                gc_ref[blk] += _mm_tn(dpre_b[:, lanes(blk)], hs.astype(BF16))
                hs_all[kept, blk * half : (blk + 1) * half] = _pack_state(hs[:, 0:half], hs[:, half : 2 * half])

        @pl.when(ph >= 1)
        def _():
            dpre_b = dpre_all[tile_rows, :].astype(BF16)
            for blk in range(nb):
                work[:, states(blk)] = _mm(dpre_b[:, lanes(blk)], ct_ref[blk])

        @pl.when(ph == 1)
        def _():
            @pl.when(t == 0)
            def _():
                stg[...] = jnp.zeros_like(stg)

            _scan(work, work, stg, lam_ref, n_groups, s, nb, True, False)

            @pl.when(t == n_tiles - 1)
            def _():
                _chunk_starts_adjoint(stg, initg, lcl_ref, s, nb)

        @pl.when(ph == 2)
        def _():
            @pl.when(t == 0)
            def _():
                stg[...] = initg[...]

            _scan_adjoint(work, hs_all, tile * n_groups, stg, acc, lam_ref, n_groups, s, nb)
            du = lambda blk: _mm(work[:, states(blk)].astype(BF16), bt_ref[blk])
            du_t = _blockwise(du, nb) + dpre_all[tile_rows, :] * d_ref[...]
            dy_cm[tile_groups] = du_t.reshape(n_groups, SUBLANES, width)
            for blk in range(nb):
                gb_ref[blk] += _mm_tn(u_b[:, lanes(blk)], work[:, states(blk)].astype(BF16))

            @pl.when(t == n_tiles - 1)
            def _():
                stores = _chunk_copies(du_hbm, b, dy_cm, sems.at[0], False)
                for cp in stores:
                    cp.start()
                for cp in stores:
                    cp.wait()

        @pl.when(last)
        def _():
            glam_ref[...] = jnp.sum(acc[...], axis=0, keepdims=True)
            exchange_finish()

    def tile(b, ph, t):
        return (b, jnp.where(ph == 0, t, n_tiles - 1 - t), 0)

    full = lambda a: pl.BlockSpec(a.shape, lambda b, ph, t: (0,) * a.ndim)
    hbm = pl.BlockSpec(memory_space=pl.ANY)
    res = pl.pallas_call(
        body,
        name="ssm_bwd",
        grid=(batch, 3, n_tiles),
        in_specs=[
            hbm,
            pl.BlockSpec((None, rows, width), tile),
            pl.BlockSpec((None, rows, 2 * width), tile),
            hbm,
            pl.BlockSpec((None, SUBLANES, 2 * s), lambda b, ph, t: (b, 0, 0)),
            full(b_tab), full(b_tab_t), full(c_tab_t), full(lam8), full(lcl), full(d_skip), full(w_glu),
        ]
        + [hbm] * nr,
        out_specs=[
            hbm,
            full(b_tab), full(b_tab), full(w_glu), full(d_skip),
            pl.BlockSpec((1, 2 * s), lambda b, ph, t: (0, 0)),
        ]
        + [hbm] * nr,
        out_shape=[
            jax.ShapeDtypeStruct((batch, seq, width), F32),
            jax.ShapeDtypeStruct(b_tab.shape, F32),
            jax.ShapeDtypeStruct(b_tab.shape, F32),
            jax.ShapeDtypeStruct(w_glu.shape, F32),
            jax.ShapeDtypeStruct(d_skip.shape, F32),
            jax.ShapeDtypeStruct((1, 2 * s), F32),
        ]
        + [jax.ShapeDtypeStruct(a.shape, F32) for a in ready],
        scratch_shapes=[
            pltpu.VMEM((cl, NCH, width), F32),
            pltpu.VMEM((cl, NCH, width), F32),
            pltpu.VMEM((rows, 2 * s), F32),
            pltpu.VMEM((seq + SUBLANES, s), jnp.uint32),
            pltpu.VMEM((seq, width), F32),
        ]
        + [pltpu.VMEM((SUBLANES, 2 * s), F32)] * 4
        + [pltpu.SemaphoreType.DMA((2, NCH)), pltpu.SemaphoreType.DMA((nr, 7)), pltpu.SemaphoreType.DMA((nr, 7))],
        compiler_params=_params("arbitrary", "arbitrary", "arbitrary"),
    )(u, pre_p, z_p, dy, init, b_tab, b_tab_t, c_tab_t, lam8, lcl, d_skip, w_glu, *ready)
    return res[:6], res[6:]


def _kv_fwd(mem, g_mem, w_kv):
    batch, n_mem, d = mem.shape
    kvw = w_kv.shape[1]

    def body(mem_ref, g_ref, w_ref, kv_ref):
        m = mem_ref[...]
        kv_ref[...] = _mm((m * _rms(m) * g_ref[...]).astype(BF16), w_ref[...])

    return pl.pallas_call(
        body,
        name="kv_fwd",
        grid=(batch,),
        in_specs=[
            pl.BlockSpec((None, n_mem, d), lambda b: (b, 0, 0)),
            pl.BlockSpec((1, d), lambda b: (0, 0)),
            pl.BlockSpec((d, kvw), lambda b: (0, 0)),
        ],
        out_specs=pl.BlockSpec((None, n_mem, kvw), lambda b: (b, 0, 0)),
        out_shape=jax.ShapeDtypeStruct((batch, n_mem, kvw), F32),
        compiler_params=_params("arbitrary"),
    )(mem, g_mem, w_kv)


def _kv_bwd(mem, dkv, g_mem, w_kv):
    batch, n_mem, d = mem.shape
    kvw = w_kv.shape[1]

    def body(mem_ref, dkv_ref, g_ref, w_ref, gw_ref, gg_ref):
        @pl.when(pl.program_id(0) == 0)
        def _():
            gw_ref[...] = jnp.zeros_like(gw_ref)
            gg_ref[...] = jnp.zeros_like(gg_ref)

        m = mem_ref[...]
        mn = m * _rms(m)
        dkv_b = dkv_ref[...].astype(BF16)
        gw_ref[...] += _mm_tn((mn * g_ref[...]).astype(BF16), dkv_b)
        gg_ref[...] += jnp.sum(_mm_nt(dkv_b, w_ref[...]) * mn, axis=0, keepdims=True)

    return pl.pallas_call(
        body,
        name="kv_bwd",
        grid=(batch,),
        in_specs=[
            pl.BlockSpec((None, n_mem, d), lambda b: (b, 0, 0)),
            pl.BlockSpec((None, n_mem, kvw), lambda b: (b, 0, 0)),
            pl.BlockSpec((1, d), lambda b: (0, 0)),
            pl.BlockSpec((d, kvw), lambda b: (0, 0)),
        ],
        out_specs=[pl.BlockSpec((d, kvw), lambda b: (0, 0)), pl.BlockSpec((1, d), lambda b: (0, 0))],
        out_shape=[jax.ShapeDtypeStruct((d, kvw), F32), jax.ShapeDtypeStruct((1, d), F32)],
        compiler_params=_params("arbitrary"),
    )(mem, dkv, g_mem, w_kv)


def _tail(x2, target2, proj, y_pool, y_ssm, kv, w_out, g_post):
    tokens, d = x2.shape
    batch, n_mem, kvw = kv.shape
    pool_w, ssm_w, att_w, mix = y_pool.shape[1], y_ssm.shape[1], kvw // 2, w_out.shape[0]
    assert (mix - att_w) % att_w == 0 and proj.shape[1] == 2 * mix
    hd = att_w // MEM_HEADS
    cl = _token_tile(tokens // batch, TOKEN_ROWS)
    n_tiles = tokens // cl
    per_seq = tokens // batch // cl
    qk_scale = hd**-0.5

    def body(x_ref, tg_ref, gate_ref, yp_ref, ys_ref, q_ref, kv_ref, w_ref, g_ref,
             dres_ref, dgate_ref, dyp_ref, dys_ref, dq_ref, dkv_ref, gw_hbm, gg_ref, loss_ref, acc, sem):
        i = pl.program_id(0)

        @pl.when(i == 0)
        def _():
            acc[...] = jnp.zeros_like(acc)
            gg_ref[...] = jnp.zeros_like(gg_ref)
            loss_ref[...] = jnp.zeros_like(loss_ref)

        @pl.when(i % per_seq == 0)
        def _():
            dkv_ref[...] = jnp.zeros_like(dkv_ref)

        k = kv_ref[:, 0:att_w].astype(BF16)
        v = kv_ref[:, att_w : 2 * att_w].astype(BF16)
        lane = lax.broadcasted_iota(jnp.int32, (1, att_w), 1)
        heads = [(lane >= h * hd) & (lane < (h + 1) * hd) for h in range(MEM_HEADS)]
        g = g_ref[...]

        def part(rows):
            n_rows = rows.stop - rows.start
            q = q_ref[rows, :]
            probs, q_heads = [], []
            att = jnp.zeros((n_rows, att_w), F32)
            for mask in heads:
                qh = jnp.where(mask, q, 0.0).astype(BF16)
                sc = _mm_nt(qh, k) * qk_scale
                e = jnp.exp(sc - jnp.max(sc, axis=-1, keepdims=True))
                p = e * (1.0 / jnp.sum(e, axis=-1, keepdims=True))
                att = att + jnp.where(mask, _mm(p.astype(BF16), v), 0.0)
                probs.append(p)
                q_heads.append(qh)

            ycat = jnp.concatenate([yp_ref[rows, :], ys_ref[rows, :], att], axis=1)
            gate = gate_ref[rows, :]
            sig = jax.nn.sigmoid(gate)
            silu = gate * sig
            yg = (ycat * silu).astype(BF16)
            out = _mm(yg, w_ref[...])
            r = _rms(out)
            on = out * r
            err = x_ref[rows, :] + on * g - tg_ref[rows, :]
            loss_ref[...] += 0.5 * jnp.sum(jnp.mean(err * err, axis=-1, keepdims=True), axis=0, keepdims=True)
            dres = err * (1.0 / d)
            dres_ref[rows, :] = dres
            gg_ref[...] += jnp.sum(dres * on, axis=0, keepdims=True)
            don = dres * g
            dout = (r * (don - on * jnp.mean(don * on, axis=-1, keepdims=True))).astype(BF16)
            acc[...] += _mm_tn(yg, dout)
            dyg = _mm_nt(dout, w_ref[...])
            dgate_ref[rows, :] = dyg * ycat * (sig * (1.0 + gate * (1.0 - sig)))
            dycat = dyg * silu
            dyp_ref[rows, :] = dycat[:, 0:pool_w]
            dys_ref[rows, :] = dycat[:, pool_w : pool_w + ssm_w]
            datt = dycat[:, pool_w + ssm_w : mix]

            dq = jnp.zeros((n_rows, att_w), F32)
            dk = jnp.zeros((n_mem, att_w), F32)
            dv = jnp.zeros((n_mem, att_w), F32)
            for mask, p, qh in zip(heads, probs, q_heads):
                doh = jnp.where(mask, datt, 0.0).astype(BF16)
                dp = _mm_nt(doh, v)
                ds = (p * (dp - jnp.sum(p * dp, axis=-1, keepdims=True)) * qk_scale).astype(BF16)
                dq = dq + jnp.where(mask, _mm(ds, k), 0.0)
                dk = dk + _mm_tn(ds, qh)
                dv = dv + _mm_tn(p.astype(BF16), doh)
            dq_ref[rows, :] = dq
            dkv_ref[:, 0:att_w] += dk
            dkv_ref[:, att_w : 2 * att_w] += dv

        part(slice(0, cl))

        @pl.when(i == n_tiles - 1)
        def _():
            cp = pltpu.make_async_copy(acc, gw_hbm, sem)
            cp.start()
            cp.wait()

    tok = lambda w: pl.BlockSpec((cl, w), lambda i: (i, 0))
    chunked = tok(ssm_w)
    per_batch = pl.BlockSpec((None, n_mem, kvw), lambda i: (i // per_seq, 0, 0))
    return pl.pallas_call(
        body,
        name="tail",
        grid=(n_tiles,),
        in_specs=[
            tok(d), tok(d), pl.BlockSpec((cl, mix), lambda i: (i, 1)), tok(pool_w), chunked,
            pl.BlockSpec((cl, att_w), lambda i: (i, (mix - att_w) // att_w)), per_batch,
            pl.BlockSpec((mix, d), lambda i: (0, 0)),
            pl.BlockSpec((1, d), lambda i: (0, 0)),
        ],
        out_specs=[
            tok(d), tok(mix), tok(pool_w), chunked, tok(att_w), per_batch,
            pl.BlockSpec(memory_space=pl.ANY),
            pl.BlockSpec((1, d), lambda i: (0, 0)),
            pl.BlockSpec((1, 1), lambda i: (0, 0)),
        ],
        out_shape=[
            jax.ShapeDtypeStruct((tokens, d), F32),
            jax.ShapeDtypeStruct((tokens, mix), F32),
            jax.ShapeDtypeStruct((tokens, pool_w), F32),
            jax.ShapeDtypeStruct((tokens, ssm_w), F32),
            jax.ShapeDtypeStruct((tokens, att_w), F32),
            jax.ShapeDtypeStruct(kv.shape, F32),
            jax.ShapeDtypeStruct((mix, d), F32),
            jax.ShapeDtypeStruct((1, d), F32),
            jax.ShapeDtypeStruct((1, 1), F32),
        ],
        scratch_shapes=[pltpu.VMEM((mix, d), F32), pltpu.SemaphoreType.DMA],
        compiler_params=_params("arbitrary"),
    )(x2, target2, proj, y_pool, y_ssm, proj, kv, w_out, g_post)


def _pack(arrays):
    flat = jnp.concatenate([a.reshape(-1) for a in arrays])
    rows = -(-flat.size // (N_DEV * SUBLANES * LANES)) * N_DEV * SUBLANES
    return jnp.pad(flat, (0, rows * LANES - flat.size)).reshape(rows, LANES)


def _unpack(packed, like):
    flat, out, at = packed.reshape(-1), [], 0
    for a in like:
        out.append(flat[at : at + a.size].reshape(a.shape))
        at += a.size
    return out


def kernel(x, mem, g_pre, w_in, w_pool, pool_scale, a_re, a_im, log_dt, b_re, b_im, c_re, c_im, d_skip, w_glu, g_mem, w_kv, w_out, g_post, loss_target, m_g_pre, m_w_in, m_w_pool, m_pool_scale, m_a_re, m_a_im, m_log_dt, m_b_re, m_b_im, m_c_re, m_c_im, m_d_skip, m_w_glu, m_g_mem, m_w_kv, m_w_out, m_g_post, v_g_pre, v_w_in, v_w_pool, v_pool_scale, v_a_re, v_a_im, v_log_dt, v_b_re, v_b_im, v_c_re, v_c_im, v_d_skip, v_w_glu, v_g_mem, v_w_kv, v_w_out, v_g_post):
    batch, seq, d = x.shape
    cl = seq // NCH
    pool_w, ssm_w = pool_scale.shape[1], d_skip.shape[1]
    att_w = w_kv.shape[2] // 2
    tokens = batch * seq
    x2 = x.reshape(tokens, d)
    target2 = loss_target.reshape(tokens, d)

    wp_blk = jax.scipy.linalg.block_diag(*w_pool[0]).astype(BF16)
    ssm_params = (a_re[0], a_im[0], log_dt[0], b_re[0], b_im[0], c_re[0], c_im[0])
    (lam_row, b_tab, c_tab), tables_vjp = jax.vjp(_ssm_tables, *ssm_params)
    nb = b_tab.shape[0]
    lam8 = jnp.broadcast_to(lam_row, (SUBLANES, lam_row.shape[1]))
    lcl = _lam_power(a_re[0], a_im[0], log_dt[0], float(cl), 1.0, nb)
    b_bf, c_bf = b_tab.astype(BF16), c_tab.astype(BF16)

    proj, w_in_g, (w_glu_g,) = _in_proj(x2, g_pre, w_in[0], [w_glu[0].astype(BF16)])
    proj3 = proj.reshape(batch, seq, proj.shape[1])
    w_glu_f = w_glu_g.transpose(1, 0, 2).reshape(w_glu_g.shape[1], N_DEV * w_glu_g.shape[2])
    y_pool, diff_pool = _pool_fwd(proj, wp_blk, pool_scale, batch, seq)
    (y_ssm, pre_ssm, z_ssm, init_ssm), (w_out_g, w_kv_g) = _ssm_fwd(
        proj3, pool_w, b_bf, c_bf, lam8, lcl, d_skip, w_glu_f, [w_out[0].astype(BF16), w_kv[0].astype(BF16)])
    w_out_f = w_out_g.reshape(N_DEV * w_out_g.shape[1], w_out_g.shape[2])
    w_kv_f = w_kv_g.reshape(N_DEV * w_kv_g.shape[1], w_kv_g.shape[2])
    kv = _kv_fwd(mem, g_mem, w_kv_f)

    dres, dgate, dy_pool, dy_ssm, dq, dkv, gw_out, gg_post, loss_part = _tail(
        x2, target2, proj, y_pool, y_ssm.reshape(tokens, ssm_w), kv, w_out_f, g_post)

    gw_kv, gg_mem = _kv_bwd(mem, dkv, g_mem, w_kv_f)
    du_pool, gwp_dense, g_scale = _pool_bwd(diff_pool, dy_pool, wp_blk, pool_scale, batch, seq)
    gw_kv8 = gw_kv.reshape(N_DEV, -1, gw_kv.shape[1])
    gw_out8 = gw_out.reshape(N_DEV, -1, gw_out.shape[1])
    (du_ssm, gb_tab, gc_tab_t, gw_glu, gd_skip, glam), (kv_landed, out_landed) = _ssm_bwd(
        proj3, pool_w, pre_ssm, z_ssm, dy_ssm.reshape(batch, seq, ssm_w), init_ssm, b_bf, b_bf.transpose(0, 2, 1),
        c_bf.transpose(0, 2, 1), lam8, lcl, d_skip, w_glu_f, [gw_kv8, gw_out8])
    grad_x2, gw_in, gg_pre = _in_proj_bwd(
        x2, dres, du_pool, du_ssm.reshape(tokens, ssm_w), dq, dgate, g_pre,
        w_in_g.transpose(0, 2, 1).reshape(-1, d))

    gw = pool_w // len(POOL_WINDOWS)
    gw_pool = jnp.stack([gwp_dense[i * gw : (i + 1) * gw, i * gw : (i + 1) * gw] for i in range(len(POOL_WINDOWS))])
    g_ssm = tables_vjp((glam, gb_tab, gc_tab_t.transpose(0, 2, 1)))

    small_w = [g_pre, w_pool, pool_scale, a_re, a_im, log_dt, b_re, b_im, c_re, c_im, d_skip, g_mem, g_post]
    small_m = [m_g_pre, m_w_pool, m_pool_scale, m_a_re, m_a_im, m_log_dt, m_b_re, m_b_im, m_c_re, m_c_im, m_d_skip, m_g_mem, m_g_post]
    small_v = [v_g_pre, v_w_pool, v_pool_scale, v_a_re, v_a_im, v_log_dt, v_b_re, v_b_im, v_c_re, v_c_im, v_d_skip, v_g_mem, v_g_post]
    small_g = [gg_pre, gw_pool, g_scale, *g_ssm, gd_skip, gg_mem, gg_post]
    big_g, small_sum = _reduce_all(
        [gw_in, gw_glu.reshape(ssm_w, N_DEV, -1).transpose(1, 0, 2)],
        _pack(small_g + [loss_part]),
        [(gw_kv8, kv_landed), (gw_out8, out_landed)])

    flat2 = lambda a: a.reshape(-1, a.shape[-1])
    sg = _unpack(small_sum, [flat2(a) for a in small_w] + [loss_part])
    loss = sg[-1].reshape(())
    small_names = ["g_pre", "w_pool", "pool_scale", "a_re", "a_im", "log_dt", "b_re", "b_im", "c_re", "c_im",
                   "d_skip", "g_mem", "g_post"]
    names = ["w_in", "w_glu", "w_kv", "w_out"] + small_names
    all_w = [w_in, w_glu, w_kv, w_out] + small_w
    all_m = [m_w_in, m_w_glu, m_w_kv, m_w_out] + small_m
    all_v = [v_w_in, v_w_glu, v_w_kv, v_w_out] + small_v
    updates = _adamw_all(big_g + sg[:-1], [flat2(a) for a in all_w], [flat2(a) for a in all_m], [flat2(a) for a in all_v])
    updates = {name: [t.reshape(a.shape) for t in u] for name, u, a in zip(names, updates, all_w)}

    order = ["g_pre", "w_in", "w_pool", "pool_scale", "a_re", "a_im", "log_dt", "b_re", "b_im", "c_re", "c_im",
             "d_skip", "w_glu", "g_mem", "w_kv", "w_out", "g_post"]
    outs = [[updates[name][kind] for name in order] for kind in range(4)]
    return (loss, grad_x2.reshape(batch, seq, d), *outs[0], *outs[1], *outs[2], *outs[3])
```

```python
import math

import jax
import jax.numpy as jnp
from jax import lax
from jax.experimental import pallas as pl
from jax.experimental.pallas import tpu as pltpu

F32 = jnp.float32
BF16 = jnp.bfloat16
MESH = pl.DeviceIdType.MESH

N_DEV = 8
SUBLANES = 8
LANES = 128
NCH = SUBLANES
VMEM_LIMIT = 60 * 1024 * 1024

IN_PROJ_ROWS = 2048
TOKEN_ROWS = 512
SSM_ROWS = 1024
SCAN_LANES = 512

EPS = 1e-6
POOL_WINDOWS = (2, 4, 8, 16)
MEM_HEADS = 4
ADAM_LR, ADAM_B1, ADAM_B2, ADAM_EPS, ADAM_WD, ADAM_STEP = 0.001, 0.9, 0.999, 1e-08, 0.01, 10


def _mm(a, b):
    return jnp.dot(a, b, preferred_element_type=F32)


def _mm_nt(a, b):
    return lax.dot_general(a, b, (((1,), (1,)), ((), ())), preferred_element_type=F32)


def _mm_tn(a, b):
    return lax.dot_general(a, b, (((0,), (0,)), ((), ())), preferred_element_type=F32)


def _params(*sem):
    return pltpu.CompilerParams(dimension_semantics=sem or None, vmem_limit_bytes=VMEM_LIMIT)


def _adamw(w, g, m, v):
    m = ADAM_B1 * m + (1.0 - ADAM_B1) * g
    v = ADAM_B2 * v + (1.0 - ADAM_B2) * (g * g)
    m_hat = m / (1.0 - ADAM_B1**ADAM_STEP)
    v_hat = v / (1.0 - ADAM_B2**ADAM_STEP)
    delta = -ADAM_LR * (m_hat / (jnp.sqrt(v_hat) + ADAM_EPS) + ADAM_WD * w)
    return delta, m, v


def _gelu(x):
    k = math.sqrt(2.0 / math.pi)
    return 0.5 * x * (1.0 + jnp.tanh(k * (x + 0.044715 * x * x * x)))


def _gelu_grad(x):
    k = math.sqrt(2.0 / math.pi)
    th = jnp.tanh(k * (x + 0.044715 * x * x * x))
    return 0.5 * (1.0 + th) + 0.5 * x * (1.0 - th * th) * (k * (1.0 + 3.0 * 0.044715 * x * x))


def _place():
    return lax.axis_index("x"), lax.axis_index("y"), lax.axis_index("c")


def _exchange_steps(n, src_of, landing, send_sems, recv_sems):
    x, y, c = _place()
    me = 4 * x + 2 * y + c
    peers = []
    for j in range(1, N_DEV):
        px = 1 - x if j & 4 else x
        py = 1 - y if j & 2 else y
        pc = 1 - c if j & 1 else c
        peers.append((px, py, pc))

    def copy(a, j, from_slot, to_slot, peer):
        return pltpu.make_async_remote_copy(
            src_ref=src_of(a, to_slot), dst_ref=landing[a].at[from_slot],
            send_sem=send_sems.at[a, j], recv_sem=recv_sems.at[a, j], device_id=peer, device_id_type=MESH)

    def start():
        for a in range(n):
            for j, p in enumerate(peers):
                copy(a, j, me, 4 * p[0] + 2 * p[1] + p[2], p).start()

    def finish():
        for a in range(n):
            for j, p in enumerate(peers):
                slot = 4 * p[0] + 2 * p[1] + p[2]
                copy(a, j, slot, slot, p).wait_recv()
        for a in range(n):
            for j, p in enumerate(peers):
                copy(a, j, me, 4 * p[0] + 2 * p[1] + p[2], p).wait_send()

    return start, finish


def _adamw_all(gs, ws, ms, vs):
    n = len(gs)

    def body(*refs):
        g, w, m, v = refs[:n], refs[n : 2 * n], refs[2 * n : 3 * n], refs[3 * n : 4 * n]
        outs = refs[4 * n :]
        for a in range(n):
            rows = g[a].shape[0]
            chunk = math.gcd(rows, 128)

            def step(i, _, a=a, chunk=chunk):
                r = pl.ds(pl.multiple_of(i * chunk, chunk), chunk)
                grad = g[a][r, :]
                delta, nm, nv = _adamw(w[a][r, :], grad, m[a][r, :], v[a][r, :])
                outs[4 * a][r, :] = grad
                outs[4 * a + 1][r, :] = delta
                outs[4 * a + 2][r, :] = nm
                outs[4 * a + 3][r, :] = nv
                return 0

            lax.fori_loop(0, rows // chunk, step, 0)

    vmem = pl.BlockSpec(memory_space=pltpu.VMEM)
    out_shape = []
    for wa in ws:
        out_shape += [jax.ShapeDtypeStruct(wa.shape, F32)] * 4
    res = pl.pallas_call(
        body,
        name="adamw_all",
        out_shape=out_shape,
        in_specs=[vmem] * (4 * n),
        out_specs=[vmem] * (4 * n),
        compiler_params=_params(),
    )(*gs, *ws, *ms, *vs)
    return [tuple(res[4 * a : 4 * a + 4]) for a in range(n)]


def _reduce_all(parts, small, early):
    n, ne = len(parts), len(early)
    parts4 = [p.reshape(4, 2, *p.shape[1:]) for p in parts]
    blks = [p.shape[1:] for p in parts]

    def body(*refs):
        refs = list(refs)
        take = lambda k: [refs.pop(0) for _ in range(k)]
        part = take(n)
        (small_ref,) = take(1)
        early_in = [take(2) for _ in range(ne)]
        outs = take(n)
        (small_all,) = take(1)
        early_out = take(ne)
        own, r1, got_a1, got_a2, got_b1, got_b2, pass_a, pass_b = (take(n) for _ in range(8))
        early_buf = take(ne)
        small_land, small_mine = take(2)
        s1_send, s1_recv, h_send, h_recv, loc, rs_send, rs_recv, ag_send, ag_recv, early_sems = refs
        x, y, c = _place()
        me = 4 * x + 2 * y + c
        piece = small_mine.shape[0]
        piece_of = lambda slot: pl.ds(pl.multiple_of(slot * piece, SUBLANES), piece)
        rs_start, rs_finish = _exchange_steps(
            1, lambda a, slot: small_ref.at[piece_of(slot)], [small_land], rs_send, rs_recv)
        ag_start, ag_finish = _exchange_steps(1, lambda a, slot: small_mine, [small_all], ag_send, ag_recv)
        landed = [pltpu.make_async_copy(early_in[e][1], early_buf[e], early_sems.at[e, 0]) for e in range(ne)]
        for cp in landed:
            cp.start()
        sibling = (x, y, 1 - c)
        chips = [(1 - x, y), (x, 1 - y), (1 - x, 1 - y)]

        def rowwise(rows, fn):
            chunk = math.gcd(rows, 128)

            def step(i, _):
                fn(pl.ds(pl.multiple_of(i * chunk, chunk), chunk))
                return 0

            lax.fori_loop(0, rows // chunk, step, 0)

        stage1, local = [], []
        for a in range(n):
            cp = pltpu.make_async_remote_copy(
                src_ref=part[a].at[:, 1 - c], dst_ref=r1[a], send_sem=s1_send.at[a], recv_sem=s1_recv.at[a],
                device_id=sibling, device_id_type=MESH)
            cp.start()
            stage1.append(cp)
            lc = pltpu.make_async_copy(part[a].at[:, c], own[a], loc.at[a])
            lc.start()
            local.append(lc)
        rs_start()
        x_nbr, y_nbr = (1 - x, y, c), (x, 1 - y, c)
        mine, mine_x, mine_y = 2 * x + y, 2 * (1 - x) + y, 2 * x + (1 - y)

        def hop(a, k, src, dst, to):
            return pltpu.make_async_remote_copy(
                src_ref=src, dst_ref=dst, send_sem=h_send.at[a, k], recv_sem=h_recv.at[a, k],
                device_id=to, device_id_type=MESH)

        first, second = [], []
        for a in range(n):
            half = blks[a][0] // 2
            up, low = pl.ds(0, half), pl.ds(half, half)
            local[a].wait()
            stage1[a].wait_recv()
            for chip in range(4):

                def add(r, a=a, chip=chip):
                    own[a][chip, r, :] = own[a][chip, r, :] + r1[a][chip, r, :]

                rowwise(blks[a][0], add)
            first.append([
                hop(a, 0, own[a].at[pl.ds(2 * (1 - x), 2), up], got_a1[a], x_nbr),
                hop(a, 2, own[a].at[2 * x + (1 - y), low], got_b1[a].at[x], y_nbr),
                hop(a, 3, own[a].at[2 * (1 - x) + (1 - y), low], got_b1[a].at[1 - x], y_nbr),
            ])
            for cp in first[a]:
                cp.start()
        rs_finish()
        small_land[me] = small_ref[piece_of(me), :]

        def sum_piece(i, _):
            r = pl.ds(pl.multiple_of(i * SUBLANES, SUBLANES), SUBLANES)
            total = small_land[0, r, :]
            for dev in range(1, N_DEV):
                total = total + small_land[dev, r, :]
            small_mine[r, :] = total
            small_all[me, r, :] = total
            return 0

        lax.fori_loop(0, piece // SUBLANES, sum_piece, 0)
        ag_start()
        for a in range(n):
            half = blks[a][0] // 2
            first[a][0].wait_recv()

            def fold_upper(r, a=a):
                own[a][mine, r, :] = own[a][mine, r, :] + got_a1[a][y, r, :]
                pass_a[a][r, :] = own[a][mine_y, r, :] + got_a1[a][1 - y, r, :]

            rowwise(half, fold_upper)
            first[a][1].wait_recv()
            first[a][2].wait_recv()

            def fold_lower(r, a=a, half=half):
                rl = pl.ds(pl.multiple_of(r.start + half, SUBLANES), r.size)
                own[a][mine, rl, :] = own[a][mine, rl, :] + got_b1[a][x, r, :]
                pass_b[a][r, :] = own[a][mine_x, rl, :] + got_b1[a][1 - x, r, :]

            rowwise(half, fold_lower)
            second.append([hop(a, 1, pass_a[a], got_a2[a], y_nbr), hop(a, 4, pass_b[a], got_b2[a], x_nbr)])
            for cp in second[a]:
                cp.start()
        for e in range(ne):
            part_e, _ = early_in[e]
            landed[e].wait()
            own_block = pltpu.make_async_copy(part_e.at[me], early_buf[e].at[me], early_sems.at[e, 1])
            own_block.start()
            own_block.wait()

            def sum_early(r, e=e):
                g = early_buf[e][0, r, :]
                for dev in range(1, N_DEV):
                    g = g + early_buf[e][dev, r, :]
                early_out[e][r, :] = g

            rowwise(early_buf[e].shape[1], sum_early)
        for a in range(n):
            half = blks[a][0] // 2
            second[a][0].wait_recv()
            second[a][1].wait_recv()

            def finish_rows(r, a=a, half=half):
                rl = pl.ds(pl.multiple_of(r.start + half, SUBLANES), r.size)
                outs[a][r, :] = own[a][mine, r, :] + got_a2[a][r, :]
                outs[a][rl, :] = own[a][mine, rl, :] + got_b2[a][r, :]

            rowwise(half, finish_rows)
        ag_finish()
        for cp in stage1 + [cp for group in first + second for cp in group]:
            cp.wait_send()

    vmem = pl.BlockSpec(memory_space=pltpu.VMEM)
    hbm = pl.BlockSpec(memory_space=pl.ANY)
    piece = small.shape[0] // N_DEV
    assert piece * N_DEV == small.shape[0] and piece % SUBLANES == 0
    out_shape = [jax.ShapeDtypeStruct(b, F32) for b in blks]
    out_shape += [jax.ShapeDtypeStruct((N_DEV, piece, LANES), F32)]
    out_shape += [jax.ShapeDtypeStruct(e[0].shape[1:], F32) for e in early]
    halves = [(b[0] // 2, b[1]) for b in blks]
    scratch = (
        [pltpu.VMEM((4, *b), F32) for b in blks]
        + [pltpu.VMEM((4, *b), F32) for b in blks]
        + [pltpu.VMEM((2, *h), F32) for h in halves]
        + [pltpu.VMEM(h, F32) for h in halves]
        + [pltpu.VMEM((2, *h), F32) for h in halves]
        + [pltpu.VMEM(h, F32) for h in halves] * 3
        + [pltpu.VMEM(e[0].shape, F32) for e in early]
        + [pltpu.VMEM((N_DEV, piece, LANES), F32), pltpu.VMEM((piece, LANES), F32)]
        + [pltpu.SemaphoreType.DMA((n,)), pltpu.SemaphoreType.DMA((n,)), pltpu.SemaphoreType.DMA((n, 5)),
           pltpu.SemaphoreType.DMA((n, 5)), pltpu.SemaphoreType.DMA((n,))]
        + [pltpu.SemaphoreType.DMA((1, 7))] * 4
        + [pltpu.SemaphoreType.DMA((ne, 2))]
    )
    res = pl.pallas_call(
        body,
        name="reduce_all",
        out_shape=out_shape,
        in_specs=[hbm] * n + [vmem] + [hbm, hbm] * ne,
        out_specs=[vmem] * (n + 1 + ne),
        scratch_shapes=scratch,
        compiler_params=_params(),
    )(*parts4, small, *[t for e in early for t in e])
    return list(res[:n]) + list(res[n + 1 :]), res[n].reshape(small.shape)


def _rms(x):
    return lax.rsqrt(jnp.mean(x * x, axis=-1, keepdims=True) + EPS)


def _token_tile(tokens, want):
    tile = min(want, tokens // 2)
    assert tokens % tile == 0 and tile % 16 == 0
    return tile


def _in_proj(x2, g_pre, w_in_blk, shards):
    tokens, d = x2.shape
    nb = w_in_blk.shape[1]
    tm = _token_tile(tokens, IN_PROJ_ROWS)
    n_t = tokens // tm
    ns = len(shards)
    x_pos, y_pos, c_pos = _place()
    slot = lambda px, py, pc: 4 * px + 2 * py + pc
    chip_order = [(x_pos, y_pos), (1 - x_pos, y_pos), (x_pos, 1 - y_pos), (1 - x_pos, 1 - y_pos)]
    order = jnp.stack([slot(px, py, pc) for px, py in chip_order for pc in (c_pos, 1 - c_pos)]).astype(jnp.int32)

    def body(order_ref, x_ref, g_ref, w_ref, *rest):
        shard_hbm, proj_ref, w_hbm = rest[:ns], rest[ns], rest[ns + 1]
        gathered = rest[ns + 2 : 2 * ns + 2]
        h_all, land, w_send, w_recv, out_sem, send_sems, recv_sems, own_sems = rest[2 * ns + 2 :]
        j, i = pl.program_id(0), pl.program_id(1)
        x, y, c = _place()
        me, sibling = (x, y, c), (x, y, 1 - c)
        chips = [(1 - x, y), (x, 1 - y), (1 - x, 1 - y)]
        own = [pltpu.make_async_copy(shard_hbm[a], gathered[a].at[slot(*me)], own_sems.at[a]) for a in range(ns)]
        start, finish = _exchange_steps(ns, lambda a, s: shard_hbm[a], gathered, send_sems, recv_sems)

        def copy(k, block, to):
            ref = land.at[slot(*block)]
            return pltpu.make_async_remote_copy(
                src_ref=ref, dst_ref=ref, send_sem=w_send.at[k], recv_sem=w_recv.at[k], device_id=to, device_id_type=MESH)

        first_sends = [copy(0, me, sibling)] + [copy(1 + k, me, (*chip, c)) for k, chip in enumerate(chips)]
        forwards = [copy(4 + k, (*chip, c), sibling) for k, chip in enumerate(chips)]

        @pl.when((j == 0) & (i == 0))
        def _():
            land[slot(*me)] = w_ref[...].astype(BF16)
            for cp in first_sends:
                cp.start()
            start()
            for cp in own:
                cp.start()

        @pl.when((j == 1) & (i == 0))
        def _():
            copy(0, sibling, me).wait_recv()

        for k, chip in enumerate(chips):

            @pl.when((j == 2 + 2 * k) & (i == 0))
            def _(k=k, chip=chip):
                copy(1 + k, (*chip, c), me).wait_recv()
                forwards[k].start()

            @pl.when((j == 3 + 2 * k) & (i == 0))
            def _(k=k, chip=chip):
                copy(4 + k, (*chip, 1 - c), me).wait_recv()

        rows = pl.ds(pl.multiple_of(i * tm, tm), tm)

        @pl.when(j == 0)
        def _():
            x_t = x_ref[...]
            h_all[rows, :] = (x_t * _rms(x_t) * g_ref[...]).astype(BF16)

        proj_ref[...] = _mm(h_all[rows, :], land[order_ref[j]])

        @pl.when((j == N_DEV - 1) & (i == n_t - 1))
        def _():
            for cp in first_sends + forwards:
                cp.wait_send()
            finish()
            for cp in own:
                cp.wait()
            outs = [pltpu.make_async_copy(land.at[blk], w_hbm.at[:, pl.ds(blk * nb, nb)], out_sem.at[blk])
                    for blk in range(N_DEV)]
            for cp in outs:
                cp.start()
            for cp in outs:
                cp.wait()

    hbm = pl.BlockSpec(memory_space=pl.ANY)
    res = pl.pallas_call(
        body,
        name="in_proj",
        grid_spec=pltpu.PrefetchScalarGridSpec(
            num_scalar_prefetch=1,
            grid=(N_DEV, n_t),
            in_specs=[
                pl.BlockSpec((tm, d), lambda j, i, order: (jnp.where(j == 0, i, n_t - 1), 0)),
                pl.BlockSpec((1, d), lambda j, i, order: (0, 0)),
                pl.BlockSpec((d, nb), lambda j, i, order: (0, 0)),
            ]
            + [hbm] * ns,
            out_specs=[pl.BlockSpec((tm, nb), lambda j, i, order: (i, order[j])), hbm] + [hbm] * ns,
            scratch_shapes=[
                pltpu.VMEM((tokens, d), BF16),
                pltpu.VMEM((N_DEV, d, nb), BF16),
                pltpu.SemaphoreType.DMA((7,)),
                pltpu.SemaphoreType.DMA((7,)),
                pltpu.SemaphoreType.DMA((N_DEV,)),
                pltpu.SemaphoreType.DMA((ns, 7)),
                pltpu.SemaphoreType.DMA((ns, 7)),
                pltpu.SemaphoreType.DMA((ns,)),
            ],
        ),
        out_shape=[
            jax.ShapeDtypeStruct((tokens, N_DEV * nb), F32),
            jax.ShapeDtypeStruct((d, N_DEV * nb), BF16),
        ]
        + [jax.ShapeDtypeStruct((N_DEV, *a.shape), a.dtype) for a in shards],
        compiler_params=_params("arbitrary", "arbitrary"),
    )(order, x2, g_pre, w_in_blk, *shards)
    return res[0], res[1], res[2:]


def _in_proj_bwd(x2, dres, du_pool, du_ssm, dq, dgate, g_pre, w_in_f):
    tokens, d = x2.shape
    nb = w_in_f.shape[1] // N_DEV
    pool_w, ssm_w, att_w, mix = du_pool.shape[1], du_ssm.shape[1], dq.shape[1], dgate.shape[1]
    cl = _token_tile(tokens, TOKEN_ROWS)
    n_tiles = tokens // cl

    def body(x_ref, dres_ref, dup_ref, dus_ref, dq_ref, dgate_ref, g_ref, w_ref, gx_ref, gw_hbm, gg_ref, acc, sem):
        i = pl.program_id(0)

        @pl.when(i == 0)
        def _():
            acc[...] = jnp.zeros_like(acc)
            gg_ref[...] = jnp.zeros_like(gg_ref)

        x = x_ref[...]
        r = _rms(x)
        xn = x * r
        g = g_ref[...]
        h = (xn * g).astype(BF16)
        dproj = jnp.concatenate([dup_ref[...], dus_ref[...], dq_ref[...], dgate_ref[...]], axis=1).astype(BF16)
        dh = _mm_nt(dproj, w_ref[...])
        for j in range(N_DEV):
            acc[j] += _mm_tn(h, dproj[:, j * nb : (j + 1) * nb])
        gg_ref[...] += jnp.sum(dh * xn, axis=0, keepdims=True)
        dxn = dh * g
        gx_ref[...] = dres_ref[...] + r * (dxn - xn * jnp.mean(dxn * xn, axis=-1, keepdims=True))

        @pl.when(i == n_tiles - 1)
        def _():
            cp = pltpu.make_async_copy(acc, gw_hbm, sem)
            cp.start()
            cp.wait()

    return pl.pallas_call(
        body,
        name="in_proj_bwd",
        grid=(n_tiles,),
        in_specs=[
            pl.BlockSpec((cl, d), lambda i: (i, 0)),
            pl.BlockSpec((cl, d), lambda i: (i, 0)),
            pl.BlockSpec((cl, pool_w), lambda i: (i, 0)),
            pl.BlockSpec((cl, ssm_w), lambda i: (i, 0)),
            pl.BlockSpec((cl, att_w), lambda i: (i, 0)),
            pl.BlockSpec((cl, mix), lambda i: (i, 0)),
            pl.BlockSpec((1, d), lambda i: (0, 0)),
            pl.BlockSpec(w_in_f.shape, lambda i: (0, 0)),
        ],
        out_specs=[
            pl.BlockSpec((cl, d), lambda i: (i, 0)),
            pl.BlockSpec(memory_space=pl.ANY),
            pl.BlockSpec((1, d), lambda i: (0, 0)),
        ],
        out_shape=[
            jax.ShapeDtypeStruct((tokens, d), F32),
            jax.ShapeDtypeStruct((N_DEV, d, nb), F32),
            jax.ShapeDtypeStruct((1, d), F32),
        ],
        scratch_shapes=[pltpu.VMEM((N_DEV, d, nb), F32), pltpu.SemaphoreType.DMA],
        compiler_params=_params("arbitrary"),
    )(x2, dres, du_pool, du_ssm, dq, dgate, g_pre, w_in_f)


def _pool_geometry(seq, width):
    gw = width // len(POOL_WINDOWS)
    col = lax.broadcasted_iota(jnp.int32, (1, width), 1)
    win = jnp.full((1, width), float(POOL_WINDOWS[-1]), F32)
    for gi in range(len(POOL_WINDOWS) - 2, -1, -1):
        win = jnp.where(col < (gi + 1) * gw, float(POOL_WINDOWS[gi]), win)
    row = lax.broadcasted_iota(jnp.int32, (seq, width), 0)
    filling = 1.0 / (lax.broadcasted_iota(jnp.int32, (seq, 1), 0) + 1).astype(F32)
    inv_cnt = jnp.where(row + 1 < win.astype(jnp.int32), filling, 1.0 / win)
    return win, row, inv_cnt


def _window_sums(a, win, seq, back):
    pad = 2 * POOL_WINDOWS[-1]
    zeros = jnp.zeros((pad, a.shape[1]), F32)
    s = jnp.concatenate([a, zeros] if back else [zeros, a], axis=0)
    sums = []
    k = 1
    while k < POOL_WINDOWS[-1]:
        s = s + pltpu.roll(s, seq + pad - k if back else k, 0)
        k *= 2
        sums.append((k, s))
    out = sums[-1][1]
    for k, s in reversed(sums[:-1]):
        out = jnp.where(win <= float(k), s, out)
    return out[0:seq] if back else out[pad : pad + seq]


def _pool_fwd(u2, wp_blk, scale, batch, seq):
    width = scale.shape[1]

    def body(u_ref, w_ref, s_ref, y_ref, diff_ref):
        u = u_ref[...]
        win, row, inv_cnt = _pool_geometry(seq, width)
        diff = (_window_sums(u, win, seq, False) * inv_cnt - u).astype(BF16)
        diff_ref[...] = diff
        y_ref[...] = _mm(diff, w_ref[...]) * s_ref[...]

    return pl.pallas_call(
        body,
        name="pool_fwd",
        grid=(batch,),
        in_specs=[
            pl.BlockSpec((seq, width), lambda b: (b, 0)),
            pl.BlockSpec((width, width), lambda b: (0, 0)),
            pl.BlockSpec((1, width), lambda b: (0, 0)),
        ],
        out_specs=[pl.BlockSpec((seq, width), lambda b: (b, 0)), pl.BlockSpec((seq, width), lambda b: (b, 0))],
        out_shape=[
            jax.ShapeDtypeStruct((u2.shape[0], width), F32),
            jax.ShapeDtypeStruct((u2.shape[0], width), BF16),
        ],
        compiler_params=_params("arbitrary"),
    )(u2, wp_blk, scale)


def _pool_bwd(diff2, dy2, wp_blk, scale, batch, seq):
    width = scale.shape[1]

    def body(diff_ref, dy_ref, w_ref, s_ref, du_ref, gw_ref, gs_ref):
        @pl.when(pl.program_id(0) == 0)
        def _():
            gw_ref[...] = jnp.zeros_like(gw_ref)
            gs_ref[...] = jnp.zeros_like(gs_ref)

        diff = diff_ref[...]
        dy = dy_ref[...]
        win, row, inv_cnt = _pool_geometry(seq, width)
        gs_ref[...] += jnp.sum(dy * _mm(diff, w_ref[...]), axis=0, keepdims=True)
        dys = (dy * s_ref[...]).astype(BF16)
        gw_ref[...] += _mm_tn(diff, dys)
        dd = _mm_nt(dys, w_ref[...])
        du_ref[...] = _window_sums(dd * inv_cnt, win, seq, True) - dd

    return pl.pallas_call(
        body,
        name="pool_bwd",
        grid=(batch,),
        in_specs=[
            pl.BlockSpec((seq, width), lambda b: (b, 0)),
            pl.BlockSpec((seq, width), lambda b: (b, 0)),
            pl.BlockSpec((width, width), lambda b: (0, 0)),
            pl.BlockSpec((1, width), lambda b: (0, 0)),
        ],
        out_specs=[
            pl.BlockSpec((seq, width), lambda b: (b, 0)),
            pl.BlockSpec((width, width), lambda b: (0, 0)),
            pl.BlockSpec((1, width), lambda b: (0, 0)),
        ],
        out_shape=[
            jax.ShapeDtypeStruct(dy2.shape, F32),
            jax.ShapeDtypeStruct((width, width), F32),
            jax.ShapeDtypeStruct((1, width), F32),
        ],
        compiler_params=_params("arbitrary"),
    )(diff2, dy2, wp_blk, scale)


def _state_row(z, n_blocks):
    re = jnp.real(z).reshape(n_blocks, -1)
    im = jnp.imag(z).reshape(n_blocks, -1)
    return jnp.concatenate([re, im], axis=1).reshape(1, -1)


def _ssm_tables(a_re, a_im, log_dt, b_re, b_im, c_re, c_im):
    groups, n_state = a_re.shape
    ch = b_re.shape[2]
    nb = groups * ch // LANES
    gl = groups // nb
    lam = lax.complex(a_re, a_im)
    lam_bar = jnp.exp(lam * jnp.exp(log_dt)[:, None])
    b_bar = ((lam_bar - 1.0) / lam)[..., None] * lax.complex(b_re, b_im)
    eye = jnp.eye(gl, dtype=F32)

    def rows_to_state(t):
        return jnp.einsum("sgnc,gh->sgchn", t.reshape(nb, gl, n_state, ch), eye).reshape(nb, gl * ch, gl * n_state)

    def state_to_rows(t):
        return jnp.einsum("sgcn,gh->shngc", t.reshape(nb, gl, ch, n_state), eye).reshape(nb, gl * n_state, gl * ch)

    b_tab = jnp.concatenate([rows_to_state(jnp.real(b_bar)), rows_to_state(jnp.imag(b_bar))], axis=2)
    c_tab = jnp.concatenate([state_to_rows(c_re), -state_to_rows(c_im)], axis=1)
    return _state_row(lam_bar, nb), b_tab, c_tab


def _lam_power(a_re, a_im, log_dt, power, scale, n_blocks):
    return _state_row(scale * jnp.exp(lax.complex(a_re, a_im) * jnp.exp(log_dt)[:, None] * power), n_blocks)


def _state_blocks(s2, n_blocks, width):
    half = s2 // n_blocks // 2
    assert half % width == 0
    return [(b * 2 * half + o, b * 2 * half + half + o) for b in range(n_blocks) for o in range(0, half, width)]


def _scan(src_ref, dst_ref, st_ref, lam8_ref, n_groups, s, n_blocks, reverse, store):
    lb = SCAN_LANES
    for re0, im0 in _state_blocks(2 * s, n_blocks, lb):
        cr, ci = pl.ds(re0, lb), pl.ds(im0, lb)
        lr = lam8_ref[:, cr]
        li = -lam8_ref[:, ci] if reverse else lam8_ref[:, ci]

        def step(i, carry, cr=cr, ci=ci, lr=lr, li=li):
            hr, hi = carry
            grp = n_groups - 1 - i if reverse else i
            rows = pl.ds(pl.multiple_of(grp * SUBLANES, SUBLANES), SUBLANES)
            nr = lr * hr - li * hi + src_ref[rows, cr]
            ni = lr * hi + li * hr + src_ref[rows, ci]
            if store:
                dst_ref[rows, cr] = nr
                dst_ref[rows, ci] = ni
            return nr, ni

        hr, hi = lax.fori_loop(0, n_groups, step, (st_ref[:, cr], st_ref[:, ci]), unroll=2)
        st_ref[:, cr] = hr
        st_ref[:, ci] = hi


def _pack_state(re, im):
    hi = lax.bitcast_convert_type(re.astype(BF16).astype(F32), jnp.uint32)
    lo = lax.bitcast_convert_type(im.astype(BF16).astype(F32), jnp.uint32)
    return hi | (lo >> 16)


def _unpack_state(word):
    re = lax.bitcast_convert_type(word & jnp.uint32(0xFFFF0000), F32)
    im = lax.bitcast_convert_type(word << 16, F32)
    return re, im


def _scan_adjoint(dh_ref, hprev_ref, group0, stg_ref, acc_ref, lam8_ref, n_groups, s, n_blocks):
    lb = SCAN_LANES
    half = s // n_blocks
    for re0, im0 in _state_blocks(2 * s, n_blocks, lb):
        cr, ci = pl.ds(re0, lb), pl.ds(im0, lb)
        ch = pl.ds(re0 // (2 * half) * half + re0 % (2 * half), lb)
        lr, li = lam8_ref[:, cr], -lam8_ref[:, ci]

        def step(i, carry, cr=cr, ci=ci, ch=ch, lr=lr, li=li):
            gr, gi, ar, ai = carry
            grp = n_groups - 1 - i
            rows = pl.ds(pl.multiple_of(grp * SUBLANES, SUBLANES), SUBLANES)
            ngr = lr * gr - li * gi + dh_ref[rows, cr]
            ngi = lr * gi + li * gr + dh_ref[rows, ci]
            hr, hi = _unpack_state(hprev_ref[pl.ds(pl.multiple_of((group0 + grp) * SUBLANES, SUBLANES), SUBLANES), ch])
            ar = ar + hr * ngr + hi * ngi
            ai = ai + hr * ngi - hi * ngr
            dh_ref[rows, cr] = ngr
            dh_ref[rows, ci] = ngi
            return ngr, ngi, ar, ai

        init = (stg_ref[:, cr], stg_ref[:, ci], acc_ref[:, cr], acc_ref[:, ci])
        gr, gi, ar, ai = lax.fori_loop(0, n_groups, step, init)
        stg_ref[:, cr] = gr
        stg_ref[:, ci] = gi
        acc_ref[:, cr] = ar
        acc_ref[:, ci] = ai


def _chunk_starts(st_ref, init_ref, lcl_ref, s, n_blocks):
    w = s // n_blocks
    init_ref[0:1, :] = jnp.zeros((1, 2 * s), F32)
    for re0, im0 in _state_blocks(2 * s, n_blocks, w):
        re, im = pl.ds(re0, w), pl.ds(im0, w)
        ar, ai = lcl_ref[:, re], lcl_ref[:, im]
        cr = jnp.zeros((1, w), F32)
        ci = jnp.zeros((1, w), F32)
        for k in range(1, NCH):
            cr, ci = (ar * cr - ai * ci + st_ref[k - 1 : k, re], ar * ci + ai * cr + st_ref[k - 1 : k, im])
            init_ref[k : k + 1, re] = cr
            init_ref[k : k + 1, im] = ci


def _chunk_starts_adjoint(stg_ref, initg_ref, lcl_ref, s, n_blocks):
    w = s // n_blocks
    initg_ref[NCH - 1 : NCH, :] = jnp.zeros((1, 2 * s), F32)
    for re0, im0 in _state_blocks(2 * s, n_blocks, w):
        re, im = pl.ds(re0, w), pl.ds(im0, w)
        ar, ai = lcl_ref[:, re], -lcl_ref[:, im]
        gr = jnp.zeros((1, w), F32)
        gi = jnp.zeros((1, w), F32)
        for k in range(NCH - 2, -1, -1):
            gr, gi = (stg_ref[k + 1 : k + 2, re] + ar * gr - ai * gi, stg_ref[k + 1 : k + 2, im] + ar * gi + ai * gr)
            initg_ref[k : k + 1, re] = gr
            initg_ref[k : k + 1, im] = gi


def _ssm_rows(seq, want):
    rows = min(want, seq // 2)
    assert seq % rows == 0 and rows % SUBLANES == 0
    return rows


def _chunk_copies(hbm_ref, b, cm_ref, sems, to_cm, col0=0):
    cl, _, width = cm_ref.shape
    copies = []
    for k in range(NCH):
        nat, cm = hbm_ref.at[b, pl.ds(k * cl, cl), pl.ds(col0, width)], cm_ref.at[:, k, :]
        src, dst = (nat, cm) if to_cm else (cm, nat)
        copies.append(pltpu.make_async_copy(src, dst, sems.at[k]))
    return copies


def _blockwise(fn, n_blocks):
    return jnp.concatenate([fn(b) for b in range(n_blocks)], axis=1)


def _ssm_fwd(u, u_col, b_tab, c_tab, lam8, lcl, d_skip, w_glu, shards):
    batch, seq, _ = u.shape
    ns = len(shards)
    width = d_skip.shape[1]
    s = lam8.shape[1] // 2
    nb = b_tab.shape[0]
    sb = 2 * s // nb
    cl = seq // NCH
    rows = _ssm_rows(seq, SSM_ROWS)
    n_tiles = seq // rows
    n_groups = rows // SUBLANES

    def body(u_hbm, b_ref, c_ref, lam_ref, lcl_ref, d_ref, wg_ref, *rest):
        shard_hbm, (y_hbm, pre_ref, z_ref, init_ref) = rest[:ns], rest[ns : ns + 4]
        gathered = rest[ns + 4 : 2 * ns + 4]
        u_cm, y_cm, bu_all, st, sems, send_sems, recv_sems, own_sems = rest[2 * ns + 4 :]
        b, ph, t = pl.program_id(0), pl.program_id(1), pl.program_id(2)
        tile_groups = pl.ds(pl.multiple_of(t * n_groups, n_groups), n_groups)
        x_pos, y_pos, c_pos = _place()
        own = [pltpu.make_async_copy(shard_hbm[a], gathered[a].at[4 * x_pos + 2 * y_pos + c_pos], own_sems.at[a])
               for a in range(ns)]
        exchange_start, exchange_finish = _exchange_steps(
            ns, lambda a, slot: shard_hbm[a], gathered, send_sems, recv_sems)

        @pl.when((b == 0) & (ph == 0) & (t == 0))
        def _():
            exchange_start()
            for cp in own:
                cp.start()

        @pl.when((b == batch - 1) & (ph == 1) & (t == n_tiles - 1))
        def _():
            exchange_finish()
            for cp in own:
                cp.wait()

        @pl.when((ph == 0) & (t == 0))
        def _():
            loads = _chunk_copies(u_hbm, b, u_cm, sems, True, u_col)
            for cp in loads:
                cp.start()
            st[...] = jnp.zeros_like(st)
            for cp in loads:
                cp.wait()

        @pl.when((ph == 1) & (t == 0))
        def _():
            st[...] = init_ref[...]

        u_t = u_cm[tile_groups].reshape(rows, width)
        bu = bu_all.at[pl.ds(pl.multiple_of(t * rows, rows), rows)]

        @pl.when(ph == 0)
        def _():
            u_b = u_t.astype(BF16)
            for blk in range(nb):
                bu[:, blk * sb : (blk + 1) * sb] = _mm(u_b[:, blk * LANES : (blk + 1) * LANES], b_ref[blk])
            _scan(bu, bu, st, lam_ref, n_groups, s, nb, False, False)

        @pl.when((ph == 0) & (t == n_tiles - 1))
        def _():
            _chunk_starts(st, init_ref, lcl_ref, s, nb)

        @pl.when(ph == 1)
        def _():
            _scan(bu, bu, st, lam_ref, n_groups, s, nb, False, True)
            hs = lambda blk: _mm(bu[:, blk * sb : (blk + 1) * sb].astype(BF16), c_ref[blk])
            pre = _blockwise(hs, nb) + d_ref[...] * u_t
            z = _mm(_gelu(pre).astype(BF16), wg_ref[...])
            pre_ref[...] = pre
            z_ref[...] = z
            y = z[:, 0:width] * jax.nn.sigmoid(z[:, width : 2 * width])
            y_cm[tile_groups] = y.reshape(n_groups, SUBLANES, width)

        @pl.when((ph == 1) & (t == n_tiles - 1))
        def _():
            stores = _chunk_copies(y_hbm, b, y_cm, sems, False)
            for cp in stores:
                cp.start()
            for cp in stores:
                cp.wait()

    out_tile = lambda b, ph, t: (b, t * ph, 0)
    full = lambda a: pl.BlockSpec(a.shape, lambda b, ph, t: (0,) * a.ndim)
    hbm = pl.BlockSpec(memory_space=pl.ANY)
    res = pl.pallas_call(
        body,
        name="ssm_fwd",
        grid=(batch, 2, n_tiles),
        in_specs=[hbm, full(b_tab), full(c_tab), full(lam8), full(lcl), full(d_skip), full(w_glu)] + [hbm] * ns,
        out_specs=[
            hbm,
            pl.BlockSpec((None, rows, width), out_tile),
            pl.BlockSpec((None, rows, 2 * width), out_tile),
            pl.BlockSpec((None, SUBLANES, 2 * s), lambda b, ph, t: (b, 0, 0)),
        ]
        + [hbm] * ns,
        out_shape=[
            jax.ShapeDtypeStruct((batch, seq, width), F32),
            jax.ShapeDtypeStruct((batch, seq, width), F32),
            jax.ShapeDtypeStruct((batch, seq, 2 * width), F32),
            jax.ShapeDtypeStruct((batch, SUBLANES, 2 * s), F32),
        ]
        + [jax.ShapeDtypeStruct((N_DEV, *a.shape), a.dtype) for a in shards],
        scratch_shapes=[
            pltpu.VMEM((cl, NCH, width), F32),
            pltpu.VMEM((cl, NCH, width), F32),
            pltpu.VMEM((seq, 2 * s), F32),
            pltpu.VMEM((SUBLANES, 2 * s), F32),
            pltpu.SemaphoreType.DMA((NCH,)),
            pltpu.SemaphoreType.DMA((ns, 7)),
            pltpu.SemaphoreType.DMA((ns, 7)),
            pltpu.SemaphoreType.DMA((ns,)),
        ],
        compiler_params=_params("arbitrary", "arbitrary", "arbitrary"),
    )(u, b_tab, c_tab, lam8, lcl, d_skip, w_glu, *shards)
    return res[:4], res[4:]


def _ssm_bwd(u, u_col, pre_p, z_p, dy, init, b_tab, b_tab_t, c_tab_t, lam8, lcl, d_skip, w_glu, ready):
    batch, seq, _ = u.shape
    width = d_skip.shape[1]
    nr = len(ready)
    s = lam8.shape[1] // 2
    nb = b_tab.shape[0]
    sb = 2 * s // nb
    cl = seq // NCH
    rows = _ssm_rows(seq, SSM_ROWS)
    n_tiles = seq // rows
    n_groups = rows // SUBLANES

    def body(u_hbm, pre_ref, z_ref, dy_hbm, init_ref, b_ref, bt_ref, ct_ref, lam_ref, lcl_ref, d_ref, wg_ref, *rest):
        ready_hbm, rest = rest[:nr], rest[nr:]
        du_hbm, gb_ref, gc_ref, gwg_ref, gd_ref, glam_ref = rest[:6]
        landed_hbm, rest = rest[6 : 6 + nr], rest[6 + nr :]
        u_cm, dy_cm, work, hs_all, dpre_all, st, stg, initg, acc, sems, send_sems, recv_sems = rest
        b, ph, t = pl.program_id(0), pl.program_id(1), pl.program_id(2)
        half = s // nb
        exchange_start, exchange_finish = _exchange_steps(
            nr, lambda a, slot: ready_hbm[a].at[slot], landed_hbm, send_sems, recv_sems)
        first = (b == 0) & (ph == 0) & (t == 0)
        last = (b == batch - 1) & (ph == 2) & (t == n_tiles - 1)
        tile = jnp.where(ph == 0, t, n_tiles - 1 - t)
        tile_rows = pl.ds(pl.multiple_of(tile * rows, rows), rows)
        tile_groups = pl.ds(pl.multiple_of(tile * n_groups, n_groups), n_groups)
        lanes = lambda blk: slice(blk * LANES, (blk + 1) * LANES)
        states = lambda blk: slice(blk * sb, (blk + 1) * sb)

        @pl.when(first)
        def _():
            exchange_start()
            acc[...] = jnp.zeros_like(acc)
            gb_ref[...] = jnp.zeros_like(gb_ref)
            gc_ref[...] = jnp.zeros_like(gc_ref)
            gwg_ref[...] = jnp.zeros_like(gwg_ref)
            gd_ref[...] = jnp.zeros_like(gd_ref)

        @pl.when((ph == 0) & (t == 0))
        def _():
            loads = (_chunk_copies(u_hbm, b, u_cm, sems.at[0], True, u_col)
                     + _chunk_copies(dy_hbm, b, dy_cm, sems.at[1], True))
            for cp in loads:
                cp.start()
            st[...] = init_ref[...]
            for blk in range(nb):
                entry = init_ref[:, states(blk)]
                hs_all[0:SUBLANES, blk * half : (blk + 1) * half] = _pack_state(entry[:, 0:half], entry[:, half : 2 * half])
            for cp in loads:
                cp.wait()

        u_t = u_cm[tile_groups].reshape(rows, width)
        u_b = u_t.astype(BF16)

        @pl.when(ph == 0)
        def _():
            for blk in range(nb):
                work[:, states(blk)] = _mm(u_b[:, lanes(blk)], b_ref[blk])
            _scan(work, work, st, lam_ref, n_groups, s, nb, False, True)
            z = z_ref[...]
            dy_t = dy_cm[tile_groups].reshape(rows, width)
            pre = pre_ref[...]
            z1, sig = z[:, 0:width], jax.nn.sigmoid(z[:, width : 2 * width])
            dz = jnp.concatenate([dy_t * sig, dy_t * z1 * sig * (1.0 - sig)], axis=1).astype(BF16)
            gwg_ref[...] += _mm_tn(_gelu(pre).astype(BF16), dz)
            dpre = _mm_nt(dz, wg_ref[...]) * _gelu_grad(pre)
            dpre_all[tile_rows, :] = dpre
            gd_ref[...] += jnp.sum(dpre * u_t, axis=0, keepdims=True)
            dpre_b = dpre.astype(BF16)
            kept = pl.ds(pl.multiple_of(tile * rows + SUBLANES, SUBLANES), rows)
            for blk in range(nb):
                hs = work[:, states(blk)]
                gc_ref[blk] += _mm_tn(dpre_b[:, lanes(blk)], hs.astype(BF16))
                hs_all[kept, blk * half : (blk + 1) * half] = _pack_state(hs[:, 0:half], hs[:, half : 2 * half])

        @pl.when(ph >= 1)
        def _():
            dpre_b = dpre_all[tile_rows, :].astype(BF16)
            for blk in range(nb):
                work[:, states(blk)] = _mm(dpre_b[:, lanes(blk)], ct_ref[blk])

        @pl.when(ph == 1)
        def _():
            @pl.when(t == 0)
            def _():
                stg[...] = jnp.zeros_like(stg)

            _scan(work, work, stg, lam_ref, n_groups, s, nb, True, False)

            @pl.when(t == n_tiles - 1)
            def _():
                _chunk_starts_adjoint(stg, initg, lcl_ref, s, nb)

        @pl.when(ph == 2)
        def _():
            @pl.when(t == 0)
            def _():
                stg[...] = initg[...]

            _scan_adjoint(work, hs_all, tile * n_groups, stg, acc, lam_ref, n_groups, s, nb)
            du = lambda blk: _mm(work[:, states(blk)].astype(BF16), bt_ref[blk])
            du_t = _blockwise(du, nb) + dpre_all[tile_rows, :] * d_ref[...]
            dy_cm[tile_groups] = du_t.reshape(n_groups, SUBLANES, width)
            for blk in range(nb):
                gb_ref[blk] += _mm_tn(u_b[:, lanes(blk)], work[:, states(blk)].astype(BF16))

            @pl.when(t == n_tiles - 1)
            def _():
                stores = _chunk_copies(du_hbm, b, dy_cm, sems.at[0], False)
                for cp in stores:
                    cp.start()
                for cp in stores:
                    cp.wait()

        @pl.when(last)
        def _():
            glam_ref[...] = jnp.sum(acc[...], axis=0, keepdims=True)
            exchange_finish()

    def tile(b, ph, t):
        return (b, jnp.where(ph == 0, t, n_tiles - 1 - t), 0)

    full = lambda a: pl.BlockSpec(a.shape, lambda b, ph, t: (0,) * a.ndim)
    hbm = pl.BlockSpec(memory_space=pl.ANY)
    res = pl.pallas_call(
        body,
        name="ssm_bwd",
        grid=(batch, 3, n_tiles),
        in_specs=[
            hbm,
            pl.BlockSpec((None, rows, width), tile),
            pl.BlockSpec((None, rows, 2 * width), tile),
            hbm,
            pl.BlockSpec((None, SUBLANES, 2 * s), lambda b, ph, t: (b, 0, 0)),
            full(b_tab), full(b_tab_t), full(c_tab_t), full(lam8), full(lcl), full(d_skip), full(w_glu),
        ]
        + [hbm] * nr,
        out_specs=[
            hbm,
            full(b_tab), full(b_tab), full(w_glu), full(d_skip),
            pl.BlockSpec((1, 2 * s), lambda b, ph, t: (0, 0)),
        ]
        + [hbm] * nr,
        out_shape=[
            jax.ShapeDtypeStruct((batch, seq, width), F32),
            jax.ShapeDtypeStruct(b_tab.shape, F32),
            jax.ShapeDtypeStruct(b_tab.shape, F32),
            jax.ShapeDtypeStruct(w_glu.shape, F32),
            jax.ShapeDtypeStruct(d_skip.shape, F32),
            jax.ShapeDtypeStruct((1, 2 * s), F32),
        ]
        + [jax.ShapeDtypeStruct(a.shape, F32) for a in ready],
        scratch_shapes=[
            pltpu.VMEM((cl, NCH, width), F32),
            pltpu.VMEM((cl, NCH, width), F32),
            pltpu.VMEM((rows, 2 * s), F32),
            pltpu.VMEM((seq + SUBLANES, s), jnp.uint32),
            pltpu.VMEM((seq, width), F32),
        ]
        + [pltpu.VMEM((SUBLANES, 2 * s), F32)] * 4
        + [pltpu.SemaphoreType.DMA((2, NCH)), pltpu.SemaphoreType.DMA((nr, 7)), pltpu.SemaphoreType.DMA((nr, 7))],
        compiler_params=_params("arbitrary", "arbitrary", "arbitrary"),
    )(u, pre_p, z_p, dy, init, b_tab, b_tab_t, c_tab_t, lam8, lcl, d_skip, w_glu, *ready)
    return res[:6], res[6:]


def _kv_fwd(mem, g_mem, w_kv):
    batch, n_mem, d = mem.shape
    kvw = w_kv.shape[1]

    def body(mem_ref, g_ref, w_ref, kv_ref):
        m = mem_ref[...]
        kv_ref[...] = _mm((m * _rms(m) * g_ref[...]).astype(BF16), w_ref[...])

    return pl.pallas_call(
        body,
        name="kv_fwd",
        grid=(batch,),
        in_specs=[
            pl.BlockSpec((None, n_mem, d), lambda b: (b, 0, 0)),
            pl.BlockSpec((1, d), lambda b: (0, 0)),
            pl.BlockSpec((d, kvw), lambda b: (0, 0)),
        ],
        out_specs=pl.BlockSpec((None, n_mem, kvw), lambda b: (b, 0, 0)),
        out_shape=jax.ShapeDtypeStruct((batch, n_mem, kvw), F32),
        compiler_params=_params("arbitrary"),
    )(mem, g_mem, w_kv)


def _kv_bwd(mem, dkv, g_mem, w_kv):
    batch, n_mem, d = mem.shape
    kvw = w_kv.shape[1]

    def body(mem_ref, dkv_ref, g_ref, w_ref, gw_ref, gg_ref):
        @pl.when(pl.program_id(0) == 0)
        def _():
            gw_ref[...] = jnp.zeros_like(gw_ref)
            gg_ref[...] = jnp.zeros_like(gg_ref)

        m = mem_ref[...]
        mn = m * _rms(m)
        dkv_b = dkv_ref[...].astype(BF16)
        gw_ref[...] += _mm_tn((mn * g_ref[...]).astype(BF16), dkv_b)
        gg_ref[...] += jnp.sum(_mm_nt(dkv_b, w_ref[...]) * mn, axis=0, keepdims=True)

    return pl.pallas_call(
        body,
        name="kv_bwd",
        grid=(batch,),
        in_specs=[
            pl.BlockSpec((None, n_mem, d), lambda b: (b, 0, 0)),
            pl.BlockSpec((None, n_mem, kvw), lambda b: (b, 0, 0)),
            pl.BlockSpec((1, d), lambda b: (0, 0)),
            pl.BlockSpec((d, kvw), lambda b: (0, 0)),
        ],
        out_specs=[pl.BlockSpec((d, kvw), lambda b: (0, 0)), pl.BlockSpec((1, d), lambda b: (0, 0))],
        out_shape=[jax.ShapeDtypeStruct((d, kvw), F32), jax.ShapeDtypeStruct((1, d), F32)],
        compiler_params=_params("arbitrary"),
    )(mem, dkv, g_mem, w_kv)


def _tail(x2, target2, proj, y_pool, y_ssm, kv, w_out, g_post):
    tokens, d = x2.shape
    batch, n_mem, kvw = kv.shape
    pool_w, ssm_w, att_w, mix = y_pool.shape[1], y_ssm.shape[1], kvw // 2, w_out.shape[0]
    assert (mix - att_w) % att_w == 0 and proj.shape[1] == 2 * mix
    hd = att_w // MEM_HEADS
    cl = _token_tile(tokens // batch, TOKEN_ROWS)
    n_tiles = tokens // cl
    per_seq = tokens // batch // cl
    qk_scale = hd**-0.5

    def body(x_ref, tg_ref, gate_ref, yp_ref, ys_ref, q_ref, kv_ref, w_ref, g_ref,
             dres_ref, dgate_ref, dyp_ref, dys_ref, dq_ref, dkv_ref, gw_hbm, gg_ref, loss_ref, acc, sem):
        i = pl.program_id(0)

        @pl.when(i == 0)
        def _():
            acc[...] = jnp.zeros_like(acc)
            gg_ref[...] = jnp.zeros_like(gg_ref)
            loss_ref[...] = jnp.zeros_like(loss_ref)

        @pl.when(i % per_seq == 0)
        def _():
            dkv_ref[...] = jnp.zeros_like(dkv_ref)

        k = kv_ref[:, 0:att_w].astype(BF16)
        v = kv_ref[:, att_w : 2 * att_w].astype(BF16)
        lane = lax.broadcasted_iota(jnp.int32, (1, att_w), 1)
        heads = [(lane >= h * hd) & (lane < (h + 1) * hd) for h in range(MEM_HEADS)]
        g = g_ref[...]

        def part(rows):
            n_rows = rows.stop - rows.start
            q = q_ref[rows, :]
            probs, q_heads = [], []
            att = jnp.zeros((n_rows, att_w), F32)
            for mask in heads:
                qh = jnp.where(mask, q, 0.0).astype(BF16)
                sc = _mm_nt(qh, k) * qk_scale
                e = jnp.exp(sc - jnp.max(sc, axis=-1, keepdims=True))
                p = e * (1.0 / jnp.sum(e, axis=-1, keepdims=True))
                att = att + jnp.where(mask, _mm(p.astype(BF16), v), 0.0)
                probs.append(p)
                q_heads.append(qh)

            ycat = jnp.concatenate([yp_ref[rows, :], ys_ref[rows, :], att], axis=1)
            gate = gate_ref[rows, :]
            sig = jax.nn.sigmoid(gate)
            silu = gate * sig
            yg = (ycat * silu).astype(BF16)
            out = _mm(yg, w_ref[...])
            r = _rms(out)
            on = out * r
            err = x_ref[rows, :] + on * g - tg_ref[rows, :]
            loss_ref[...] += 0.5 * jnp.sum(jnp.mean(err * err, axis=-1, keepdims=True), axis=0, keepdims=True)
            dres = err * (1.0 / d)
            dres_ref[rows, :] = dres
            gg_ref[...] += jnp.sum(dres * on, axis=0, keepdims=True)
            don = dres * g
            dout = (r * (don - on * jnp.mean(don * on, axis=-1, keepdims=True))).astype(BF16)
            acc[...] += _mm_tn(yg, dout)
            dyg = _mm_nt(dout, w_ref[...])
            dgate_ref[rows, :] = dyg * ycat * (sig * (1.0 + gate * (1.0 - sig)))
            dycat = dyg * silu
            dyp_ref[rows, :] = dycat[:, 0:pool_w]
            dys_ref[rows, :] = dycat[:, pool_w : pool_w + ssm_w]
            datt = dycat[:, pool_w + ssm_w : mix]

            dq = jnp.zeros((n_rows, att_w), F32)
            dk = jnp.zeros((n_mem, att_w), F32)
            dv = jnp.zeros((n_mem, att_w), F32)
            for mask, p, qh in zip(heads, probs, q_heads):
                doh = jnp.where(mask, datt, 0.0).astype(BF16)
                dp = _mm_nt(doh, v)
                ds = (p * (dp - jnp.sum(p * dp, axis=-1, keepdims=True)) * qk_scale).astype(BF16)
                dq = dq + jnp.where(mask, _mm(ds, k), 0.0)
                dk = dk + _mm_tn(ds, qh)
                dv = dv + _mm_tn(p.astype(BF16), doh)
            dq_ref[rows, :] = dq
            dkv_ref[:, 0:att_w] += dk
            dkv_ref[:, att_w : 2 * att_w] += dv

        part(slice(0, cl))

        @pl.when(i == n_tiles - 1)
        def _():
            cp = pltpu.make_async_copy(acc, gw_hbm, sem)
            cp.start()
            cp.wait()

    tok = lambda w: pl.BlockSpec((cl, w), lambda i: (i, 0))
    chunked = tok(ssm_w)
    per_batch = pl.BlockSpec((None, n_mem, kvw), lambda i: (i // per_seq, 0, 0))
    return pl.pallas_call(
        body,
        name="tail",
        grid=(n_tiles,),
        in_specs=[
            tok(d), tok(d), pl.BlockSpec((cl, mix), lambda i: (i, 1)), tok(pool_w), chunked,
            pl.BlockSpec((cl, att_w), lambda i: (i, (mix - att_w) // att_w)), per_batch,
            pl.BlockSpec((mix, d), lambda i: (0, 0)),
            pl.BlockSpec((1, d), lambda i: (0, 0)),
        ],
        out_specs=[
            tok(d), tok(mix), tok(pool_w), chunked, tok(att_w), per_batch,
            pl.BlockSpec(memory_space=pl.ANY),
            pl.BlockSpec((1, d), lambda i: (0, 0)),
            pl.BlockSpec((1, 1), lambda i: (0, 0)),
        ],
        out_shape=[
            jax.ShapeDtypeStruct((tokens, d), F32),
            jax.ShapeDtypeStruct((tokens, mix), F32),
            jax.ShapeDtypeStruct((tokens, pool_w), F32),
            jax.ShapeDtypeStruct((tokens, ssm_w), F32),
            jax.ShapeDtypeStruct((tokens, att_w), F32),
            jax.ShapeDtypeStruct(kv.shape, F32),
            jax.ShapeDtypeStruct((mix, d), F32),
            jax.ShapeDtypeStruct((1, d), F32),
            jax.ShapeDtypeStruct((1, 1), F32),
        ],
        scratch_shapes=[pltpu.VMEM((mix, d), F32), pltpu.SemaphoreType.DMA],
        compiler_params=_params("arbitrary"),
    )(x2, target2, proj, y_pool, y_ssm, proj, kv, w_out, g_post)


def _pack(arrays):
    flat = jnp.concatenate([a.reshape(-1) for a in arrays])
    rows = -(-flat.size // (N_DEV * SUBLANES * LANES)) * N_DEV * SUBLANES
    return jnp.pad(flat, (0, rows * LANES - flat.size)).reshape(rows, LANES)


def _unpack(packed, like):
    flat, out, at = packed.reshape(-1), [], 0
    for a in like:
        out.append(flat[at : at + a.size].reshape(a.shape))
        at += a.size
    return out


def kernel(x, mem, g_pre, w_in, w_pool, pool_scale, a_re, a_im, log_dt, b_re, b_im, c_re, c_im, d_skip, w_glu, g_mem, w_kv, w_out, g_post, loss_target, m_g_pre, m_w_in, m_w_pool, m_pool_scale, m_a_re, m_a_im, m_log_dt, m_b_re, m_b_im, m_c_re, m_c_im, m_d_skip, m_w_glu, m_g_mem, m_w_kv, m_w_out, m_g_post, v_g_pre, v_w_in, v_w_pool, v_pool_scale, v_a_re, v_a_im, v_log_dt, v_b_re, v_b_im, v_c_re, v_c_im, v_d_skip, v_w_glu, v_g_mem, v_w_kv, v_w_out, v_g_post):
    batch, seq, d = x.shape
    cl = seq // NCH
    pool_w, ssm_w = pool_scale.shape[1], d_skip.shape[1]
    att_w = w_kv.shape[2] // 2
    tokens = batch * seq
    x2 = x.reshape(tokens, d)
    target2 = loss_target.reshape(tokens, d)

    wp_blk = jax.scipy.linalg.block_diag(*w_pool[0]).astype(BF16)
    ssm_params = (a_re[0], a_im[0], log_dt[0], b_re[0], b_im[0], c_re[0], c_im[0])
    (lam_row, b_tab, c_tab), tables_vjp = jax.vjp(_ssm_tables, *ssm_params)
    nb = b_tab.shape[0]
    lam8 = jnp.broadcast_to(lam_row, (SUBLANES, lam_row.shape[1]))
    lcl = _lam_power(a_re[0], a_im[0], log_dt[0], float(cl), 1.0, nb)
    b_bf, c_bf = b_tab.astype(BF16), c_tab.astype(BF16)

    proj, w_in_f, (w_glu_g,) = _in_proj(x2, g_pre, w_in[0], [w_glu[0].astype(BF16)])
    proj3 = proj.reshape(batch, seq, proj.shape[1])
    w_glu_f = w_glu_g.transpose(1, 0, 2).reshape(w_glu_g.shape[1], N_DEV * w_glu_g.shape[2])
    y_pool, diff_pool = _pool_fwd(proj, wp_blk, pool_scale, batch, seq)
    (y_ssm, pre_ssm, z_ssm, init_ssm), (w_out_g, w_kv_g) = _ssm_fwd(
        proj3, pool_w, b_bf, c_bf, lam8, lcl, d_skip, w_glu_f, [w_out[0].astype(BF16), w_kv[0].astype(BF16)])
    w_out_f = w_out_g.reshape(N_DEV * w_out_g.shape[1], w_out_g.shape[2])
    w_kv_f = w_kv_g.reshape(N_DEV * w_kv_g.shape[1], w_kv_g.shape[2])
    kv = _kv_fwd(mem, g_mem, w_kv_f)

    dres, dgate, dy_pool, dy_ssm, dq, dkv, gw_out, gg_post, loss_part = _tail(
        x2, target2, proj, y_pool, y_ssm.reshape(tokens, ssm_w), kv, w_out_f, g_post)

    gw_kv, gg_mem = _kv_bwd(mem, dkv, g_mem, w_kv_f)
    du_pool, gwp_dense, g_scale = _pool_bwd(diff_pool, dy_pool, wp_blk, pool_scale, batch, seq)
    gw_kv8 = gw_kv.reshape(N_DEV, -1, gw_kv.shape[1])
    gw_out8 = gw_out.reshape(N_DEV, -1, gw_out.shape[1])
    (du_ssm, gb_tab, gc_tab_t, gw_glu, gd_skip, glam), (kv_landed, out_landed) = _ssm_bwd(
        proj3, pool_w, pre_ssm, z_ssm, dy_ssm.reshape(batch, seq, ssm_w), init_ssm, b_bf, b_bf.transpose(0, 2, 1),
        c_bf.transpose(0, 2, 1), lam8, lcl, d_skip, w_glu_f, [gw_kv8, gw_out8])
    grad_x2, gw_in, gg_pre = _in_proj_bwd(
        x2, dres, du_pool, du_ssm.reshape(tokens, ssm_w), dq, dgate, g_pre,
        w_in_f)

    gw = pool_w // len(POOL_WINDOWS)
    gw_pool = jnp.stack([gwp_dense[i * gw : (i + 1) * gw, i * gw : (i + 1) * gw] for i in range(len(POOL_WINDOWS))])
    g_ssm = tables_vjp((glam, gb_tab, gc_tab_t.transpose(0, 2, 1)))

    small_w = [g_pre, w_pool, pool_scale, a_re, a_im, log_dt, b_re, b_im, c_re, c_im, d_skip, g_mem, g_post]
    small_m = [m_g_pre, m_w_pool, m_pool_scale, m_a_re, m_a_im, m_log_dt, m_b_re, m_b_im, m_c_re, m_c_im, m_d_skip, m_g_mem, m_g_post]
    small_v = [v_g_pre, v_w_pool, v_pool_scale, v_a_re, v_a_im, v_log_dt, v_b_re, v_b_im, v_c_re, v_c_im, v_d_skip, v_g_mem, v_g_post]
    small_g = [gg_pre, gw_pool, g_scale, *g_ssm, gd_skip, gg_mem, gg_post]
    big_g, small_sum = _reduce_all(
        [gw_in, gw_glu.reshape(ssm_w, N_DEV, -1).transpose(1, 0, 2)],
        _pack(small_g + [loss_part]),
        [(gw_kv8, kv_landed), (gw_out8, out_landed)])

    flat2 = lambda a: a.reshape(-1, a.shape[-1])
    sg = _unpack(small_sum, [flat2(a) for a in small_w] + [loss_part])
    loss = sg[-1].reshape(())
    small_names = ["g_pre", "w_pool", "pool_scale", "a_re", "a_im", "log_dt", "b_re", "b_im", "c_re", "c_im",
                   "d_skip", "g_mem", "g_post"]
    names = ["w_in", "w_glu", "w_kv", "w_out"] + small_names
    all_w = [w_in, w_glu, w_kv, w_out] + small_w
    all_m = [m_w_in, m_w_glu, m_w_kv, m_w_out] + small_m
    all_v = [v_w_in, v_w_glu, v_w_kv, v_w_out] + small_v
    updates = _adamw_all(big_g + sg[:-1], [flat2(a) for a in all_w], [flat2(a) for a in all_m], [flat2(a) for a in all_v])
    updates = {name: [t.reshape(a.shape) for t in u] for name, u, a in zip(names, updates, all_w)}

    order = ["g_pre", "w_in", "w_pool", "pool_scale", "a_re", "a_im", "log_dt", "b_re", "b_im", "c_re", "c_im",
             "d_skip", "w_glu", "g_mem", "w_kv", "w_out", "g_post"]
    outs = [[updates[name][kind] for name in order] for kind in range(4)]
    return (loss, grad_x2.reshape(batch, seq, d), *outs[0], *outs[1], *outs[2], *outs[3])
```

```python
import math

import jax
import jax.numpy as jnp
from jax import lax
from jax.experimental import pallas as pl
from jax.experimental.pallas import tpu as pltpu

F32 = jnp.float32
BF16 = jnp.bfloat16
MESH = pl.DeviceIdType.MESH

N_DEV = 8
SUBLANES = 8
LANES = 128
NCH = SUBLANES
VMEM_LIMIT = 60 * 1024 * 1024

IN_PROJ_ROWS = 2048
TOKEN_ROWS = 512
SSM_ROWS = 1024
SCAN_LANES = 512
SCAN_UNROLL = 8

EPS = 1e-6
POOL_WINDOWS = (2, 4, 8, 16)
MEM_HEADS = 4
ADAM_LR, ADAM_B1, ADAM_B2, ADAM_EPS, ADAM_WD, ADAM_STEP = 0.001, 0.9, 0.999, 1e-08, 0.01, 10


def _mm(a, b):
    return jnp.dot(a, b, preferred_element_type=F32)


def _mm_nt(a, b):
    return lax.dot_general(a, b, (((1,), (1,)), ((), ())), preferred_element_type=F32)


def _mm_tn(a, b):
    return lax.dot_general(a, b, (((0,), (0,)), ((), ())), preferred_element_type=F32)


def _params(*sem):
    return pltpu.CompilerParams(dimension_semantics=sem or None, vmem_limit_bytes=VMEM_LIMIT)


def _adamw(w, g, m, v):
    m = ADAM_B1 * m + (1.0 - ADAM_B1) * g
    v = ADAM_B2 * v + (1.0 - ADAM_B2) * (g * g)
    m_hat = m / (1.0 - ADAM_B1**ADAM_STEP)
    v_hat = v / (1.0 - ADAM_B2**ADAM_STEP)
    delta = -ADAM_LR * (m_hat / (jnp.sqrt(v_hat) + ADAM_EPS) + ADAM_WD * w)
    return delta, m, v


def _gelu(x):
    k = math.sqrt(2.0 / math.pi)
    return 0.5 * x * (1.0 + jnp.tanh(k * (x + 0.044715 * x * x * x)))


def _gelu_grad(x):
    k = math.sqrt(2.0 / math.pi)
    th = jnp.tanh(k * (x + 0.044715 * x * x * x))
    return 0.5 * (1.0 + th) + 0.5 * x * (1.0 - th * th) * (k * (1.0 + 3.0 * 0.044715 * x * x))


def _place():
    return lax.axis_index("x"), lax.axis_index("y"), lax.axis_index("c")


def _exchange_steps(n, src_of, landing, send_sems, recv_sems):
    x, y, c = _place()
    me = 4 * x + 2 * y + c
    peers = []
    for j in range(1, N_DEV):
        px = 1 - x if j & 4 else x
        py = 1 - y if j & 2 else y
        pc = 1 - c if j & 1 else c
        peers.append((px, py, pc))

    def copy(a, j, from_slot, to_slot, peer):
        return pltpu.make_async_remote_copy(
            src_ref=src_of(a, to_slot), dst_ref=landing[a].at[from_slot],
            send_sem=send_sems.at[a, j], recv_sem=recv_sems.at[a, j], device_id=peer, device_id_type=MESH)

    def start():
        for a in range(n):
            for j, p in enumerate(peers):
                copy(a, j, me, 4 * p[0] + 2 * p[1] + p[2], p).start()

    def finish():
        for a in range(n):
            for j, p in enumerate(peers):
                slot = 4 * p[0] + 2 * p[1] + p[2]
                copy(a, j, slot, slot, p).wait_recv()
        for a in range(n):
            for j, p in enumerate(peers):
                copy(a, j, me, 4 * p[0] + 2 * p[1] + p[2], p).wait_send()

    return start, finish


def _adamw_all(gs, ws, ms, vs):
    n = len(gs)

    def body(*refs):
        g, w, m, v = refs[:n], refs[n : 2 * n], refs[2 * n : 3 * n], refs[3 * n : 4 * n]
        outs = refs[4 * n :]
        for a in range(n):
            rows = g[a].shape[0]
            chunk = math.gcd(rows, 128)

            def step(i, _, a=a, chunk=chunk):
                r = pl.ds(pl.multiple_of(i * chunk, chunk), chunk)
                grad = g[a][r, :]
                delta, nm, nv = _adamw(w[a][r, :], grad, m[a][r, :], v[a][r, :])
                outs[4 * a][r, :] = grad
                outs[4 * a + 1][r, :] = delta
                outs[4 * a + 2][r, :] = nm
                outs[4 * a + 3][r, :] = nv
                return 0

            lax.fori_loop(0, rows // chunk, step, 0)

    vmem = pl.BlockSpec(memory_space=pltpu.VMEM)
    out_shape = []
    for wa in ws:
        out_shape += [jax.ShapeDtypeStruct(wa.shape, F32)] * 4
    res = pl.pallas_call(
        body,
        name="adamw_all",
        out_shape=out_shape,
        in_specs=[vmem] * (4 * n),
        out_specs=[vmem] * (4 * n),
        compiler_params=_params(),
    )(*gs, *ws, *ms, *vs)
    return [tuple(res[4 * a : 4 * a + 4]) for a in range(n)]


def _reduce_all(parts, small, early):
    n, ne = len(parts), len(early)
    parts4 = [p.reshape(4, 2, *p.shape[1:]) for p in parts]
    blks = [p.shape[1:] for p in parts]

    def body(*refs):
        refs = list(refs)
        take = lambda k: [refs.pop(0) for _ in range(k)]
        part = take(n)
        (small_ref,) = take(1)
        early_in = [take(2) for _ in range(ne)]
        outs = take(n)
        (small_all,) = take(1)
        early_out = take(ne)
        own, r1, got_a1, got_a2, got_b1, got_b2, pass_a, pass_b = (take(n) for _ in range(8))
        early_buf = take(ne)
        small_land, small_mine = take(2)
        s1_send, s1_recv, h_send, h_recv, loc, rs_send, rs_recv, ag_send, ag_recv, early_sems = refs
        x, y, c = _place()
        me = 4 * x + 2 * y + c
        piece = small_mine.shape[0]
        piece_of = lambda slot: pl.ds(pl.multiple_of(slot * piece, SUBLANES), piece)
        rs_start, rs_finish = _exchange_steps(
            1, lambda a, slot: small_ref.at[piece_of(slot)], [small_land], rs_send, rs_recv)
        ag_start, ag_finish = _exchange_steps(1, lambda a, slot: small_mine, [small_all], ag_send, ag_recv)
        landed = [pltpu.make_async_copy(early_in[e][1], early_buf[e], early_sems.at[e, 0]) for e in range(ne)]
        for cp in landed:
            cp.start()
        sibling = (x, y, 1 - c)
        chips = [(1 - x, y), (x, 1 - y), (1 - x, 1 - y)]

        def rowwise(rows, fn):
            chunk = math.gcd(rows, 128)

            def step(i, _):
                fn(pl.ds(pl.multiple_of(i * chunk, chunk), chunk))
                return 0

            lax.fori_loop(0, rows // chunk, step, 0)

        stage1, local = [], []
        for a in range(n):
            cp = pltpu.make_async_remote_copy(
                src_ref=part[a].at[:, 1 - c], dst_ref=r1[a], send_sem=s1_send.at[a], recv_sem=s1_recv.at[a],
                device_id=sibling, device_id_type=MESH)
            cp.start()
            stage1.append(cp)
            lc = pltpu.make_async_copy(part[a].at[:, c], own[a], loc.at[a])
            lc.start()
            local.append(lc)
        rs_start()
        x_nbr, y_nbr = (1 - x, y, c), (x, 1 - y, c)
        mine, mine_x, mine_y = 2 * x + y, 2 * (1 - x) + y, 2 * x + (1 - y)

        def hop(a, k, src, dst, to):
            return pltpu.make_async_remote_copy(
                src_ref=src, dst_ref=dst, send_sem=h_send.at[a, k], recv_sem=h_recv.at[a, k],
                device_id=to, device_id_type=MESH)

        first, second = [], []
        for a in range(n):
            half = blks[a][0] // 2
            up, low = pl.ds(0, half), pl.ds(half, half)
            local[a].wait()
            stage1[a].wait_recv()
            for chip in range(4):

                def add(r, a=a, chip=chip):
                    own[a][chip, r, :] = own[a][chip, r, :] + r1[a][chip, r, :]

                rowwise(blks[a][0], add)
            first.append([
                hop(a, 0, own[a].at[pl.ds(2 * (1 - x), 2), up], got_a1[a], x_nbr),
                hop(a, 2, own[a].at[2 * x + (1 - y), low], got_b1[a].at[x], y_nbr),
                hop(a, 3, own[a].at[2 * (1 - x) + (1 - y), low], got_b1[a].at[1 - x], y_nbr),
            ])
            for cp in first[a]:
                cp.start()
        rs_finish()
        small_land[me] = small_ref[piece_of(me), :]

        def sum_piece(i, _):
            r = pl.ds(pl.multiple_of(i * SUBLANES, SUBLANES), SUBLANES)
            total = small_land[0, r, :]
            for dev in range(1, N_DEV):
                total = total + small_land[dev, r, :]
            small_mine[r, :] = total
            small_all[me, r, :] = total
            return 0

        lax.fori_loop(0, piece // SUBLANES, sum_piece, 0)
        ag_start()
        for a in range(n):
            half = blks[a][0] // 2
            first[a][0].wait_recv()

            def fold_upper(r, a=a):
                own[a][mine, r, :] = own[a][mine, r, :] + got_a1[a][y, r, :]
                pass_a[a][r, :] = own[a][mine_y, r, :] + got_a1[a][1 - y, r, :]

            rowwise(half, fold_upper)
            first[a][1].wait_recv()
            first[a][2].wait_recv()

            def fold_lower(r, a=a, half=half):
                rl = pl.ds(pl.multiple_of(r.start + half, SUBLANES), r.size)
                own[a][mine, rl, :] = own[a][mine, rl, :] + got_b1[a][x, r, :]
                pass_b[a][r, :] = own[a][mine_x, rl, :] + got_b1[a][1 - x, r, :]

            rowwise(half, fold_lower)
            second.append([hop(a, 1, pass_a[a], got_a2[a], y_nbr), hop(a, 4, pass_b[a], got_b2[a], x_nbr)])
            for cp in second[a]:
                cp.start()
        for e in range(ne):
            part_e, _ = early_in[e]
            landed[e].wait()
            own_block = pltpu.make_async_copy(part_e.at[me], early_buf[e].at[me], early_sems.at[e, 1])
            own_block.start()
            own_block.wait()

            def sum_early(r, e=e):
                g = early_buf[e][0, r, :]
                for dev in range(1, N_DEV):
                    g = g + early_buf[e][dev, r, :]
                early_out[e][r, :] = g

            rowwise(early_buf[e].shape[1], sum_early)
        for a in range(n):
            half = blks[a][0] // 2
            second[a][0].wait_recv()
            second[a][1].wait_recv()

            def finish_rows(r, a=a, half=half):
                rl = pl.ds(pl.multiple_of(r.start + half, SUBLANES), r.size)
                outs[a][r, :] = own[a][mine, r, :] + got_a2[a][r, :]
                outs[a][rl, :] = own[a][mine, rl, :] + got_b2[a][r, :]

            rowwise(half, finish_rows)
        ag_finish()
        for cp in stage1 + [cp for group in first + second for cp in group]:
            cp.wait_send()

    vmem = pl.BlockSpec(memory_space=pltpu.VMEM)
    hbm = pl.BlockSpec(memory_space=pl.ANY)
    piece = small.shape[0] // N_DEV
    assert piece * N_DEV == small.shape[0] and piece % SUBLANES == 0
    out_shape = [jax.ShapeDtypeStruct(b, F32) for b in blks]
    out_shape += [jax.ShapeDtypeStruct((N_DEV, piece, LANES), F32)]
    out_shape += [jax.ShapeDtypeStruct(e[0].shape[1:], F32) for e in early]
    halves = [(b[0] // 2, b[1]) for b in blks]
    scratch = (
        [pltpu.VMEM((4, *b), F32) for b in blks]
        + [pltpu.VMEM((4, *b), F32) for b in blks]
        + [pltpu.VMEM((2, *h), F32) for h in halves]
        + [pltpu.VMEM(h, F32) for h in halves]
        + [pltpu.VMEM((2, *h), F32) for h in halves]
        + [pltpu.VMEM(h, F32) for h in halves] * 3
        + [pltpu.VMEM(e[0].shape, F32) for e in early]
        + [pltpu.VMEM((N_DEV, piece, LANES), F32), pltpu.VMEM((piece, LANES), F32)]
        + [pltpu.SemaphoreType.DMA((n,)), pltpu.SemaphoreType.DMA((n,)), pltpu.SemaphoreType.DMA((n, 5)),
           pltpu.SemaphoreType.DMA((n, 5)), pltpu.SemaphoreType.DMA((n,))]
        + [pltpu.SemaphoreType.DMA((1, 7))] * 4
        + [pltpu.SemaphoreType.DMA((ne, 2))]
    )
    res = pl.pallas_call(
        body,
        name="reduce_all",
        out_shape=out_shape,
        in_specs=[hbm] * n + [vmem] + [hbm, hbm] * ne,
        out_specs=[vmem] * (n + 1 + ne),
        scratch_shapes=scratch,
        compiler_params=_params(),
    )(*parts4, small, *[t for e in early for t in e])
    return list(res[:n]) + list(res[n + 1 :]), res[n].reshape(small.shape)


def _rms(x):
    return lax.rsqrt(jnp.mean(x * x, axis=-1, keepdims=True) + EPS)


def _token_tile(tokens, want):
    tile = min(want, tokens // 2)
    assert tokens % tile == 0 and tile % 16 == 0
    return tile


def _in_proj(x2, g_pre, w_in_blk, shards):
    tokens, d = x2.shape
    nb = w_in_blk.shape[1]
    tm = _token_tile(tokens, IN_PROJ_ROWS)
    n_t = tokens // tm
    ns = len(shards)
    x_pos, y_pos, c_pos = _place()
    slot = lambda px, py, pc: 4 * px + 2 * py + pc
    chip_order = [(x_pos, y_pos), (1 - x_pos, y_pos), (x_pos, 1 - y_pos), (1 - x_pos, 1 - y_pos)]
    order = jnp.stack([slot(px, py, pc) for px, py in chip_order for pc in (c_pos, 1 - c_pos)]).astype(jnp.int32)

    def body(order_ref, x_ref, g_ref, w_ref, *rest):
        shard_hbm, proj_ref, w_hbm = rest[:ns], rest[ns], rest[ns + 1]
        gathered = rest[ns + 2 : 2 * ns + 2]
        h_all, land, w_send, w_recv, out_sem, send_sems, recv_sems, own_sems = rest[2 * ns + 2 :]
        j, i = pl.program_id(0), pl.program_id(1)
        x, y, c = _place()
        me, sibling = (x, y, c), (x, y, 1 - c)
        chips = [(1 - x, y), (x, 1 - y), (1 - x, 1 - y)]
        own = [pltpu.make_async_copy(shard_hbm[a], gathered[a].at[slot(*me)], own_sems.at[a]) for a in range(ns)]
        start, finish = _exchange_steps(ns, lambda a, s: shard_hbm[a], gathered, send_sems, recv_sems)

        def copy(k, block, to):
            ref = land.at[slot(*block)]
            return pltpu.make_async_remote_copy(
                src_ref=ref, dst_ref=ref, send_sem=w_send.at[k], recv_sem=w_recv.at[k], device_id=to, device_id_type=MESH)

        first_sends = [copy(0, me, sibling)] + [copy(1 + k, me, (*chip, c)) for k, chip in enumerate(chips)]
        forwards = [copy(4 + k, (*chip, c), sibling) for k, chip in enumerate(chips)]

        @pl.when((j == 0) & (i == 0))
        def _():
            land[slot(*me)] = w_ref[...].astype(BF16)
            for cp in first_sends:
                cp.start()
            start()
            for cp in own:
                cp.start()

        @pl.when((j == 1) & (i == 0))
        def _():
            copy(0, sibling, me).wait_recv()

        for k, chip in enumerate(chips):

            @pl.when((j == 2 + 2 * k) & (i == 0))
            def _(k=k, chip=chip):
                copy(1 + k, (*chip, c), me).wait_recv()
                forwards[k].start()

            @pl.when((j == 3 + 2 * k) & (i == 0))
            def _(k=k, chip=chip):
                copy(4 + k, (*chip, 1 - c), me).wait_recv()

        rows = pl.ds(pl.multiple_of(i * tm, tm), tm)

        @pl.when(j == 0)
        def _():
            x_t = x_ref[...]
            h_all[rows, :] = (x_t * _rms(x_t) * g_ref[...]).astype(BF16)

        proj_ref[...] = _mm(h_all[rows, :], land[order_ref[j]])

        @pl.when((j == N_DEV - 1) & (i == n_t - 1))
        def _():
            for cp in first_sends + forwards:
                cp.wait_send()
            finish()
            for cp in own:
                cp.wait()
            outs = [pltpu.make_async_copy(land.at[blk], w_hbm.at[:, pl.ds(blk * nb, nb)], out_sem.at[blk])
                    for blk in range(N_DEV)]
            for cp in outs:
                cp.start()
            for cp in outs:
                cp.wait()

    hbm = pl.BlockSpec(memory_space=pl.ANY)
    res = pl.pallas_call(
        body,
        name="in_proj",
        grid_spec=pltpu.PrefetchScalarGridSpec(
            num_scalar_prefetch=1,
            grid=(N_DEV, n_t),
            in_specs=[
                pl.BlockSpec((tm, d), lambda j, i, order: (jnp.where(j == 0, i, n_t - 1), 0)),
                pl.BlockSpec((1, d), lambda j, i, order: (0, 0)),
                pl.BlockSpec((d, nb), lambda j, i, order: (0, 0)),
            ]
            + [hbm] * ns,
            out_specs=[pl.BlockSpec((tm, nb), lambda j, i, order: (i, order[j])), hbm] + [hbm] * ns,
            scratch_shapes=[
                pltpu.VMEM((tokens, d), BF16),
                pltpu.VMEM((N_DEV, d, nb), BF16),
                pltpu.SemaphoreType.DMA((7,)),
                pltpu.SemaphoreType.DMA((7,)),
                pltpu.SemaphoreType.DMA((N_DEV,)),
                pltpu.SemaphoreType.DMA((ns, 7)),
                pltpu.SemaphoreType.DMA((ns, 7)),
                pltpu.SemaphoreType.DMA((ns,)),
            ],
        ),
        out_shape=[
            jax.ShapeDtypeStruct((tokens, N_DEV * nb), F32),
            jax.ShapeDtypeStruct((d, N_DEV * nb), BF16),
        ]
        + [jax.ShapeDtypeStruct((N_DEV, *a.shape), a.dtype) for a in shards],
        compiler_params=_params("arbitrary", "arbitrary"),
    )(order, x2, g_pre, w_in_blk, *shards)
    return res[0], res[1], res[2:]


def _in_proj_bwd(x2, dres, du_pool, du_ssm, dq, dgate, g_pre, w_in_f):
    tokens, d = x2.shape
    nb = w_in_f.shape[1] // N_DEV
    pool_w, ssm_w, att_w, mix = du_pool.shape[1], du_ssm.shape[1], dq.shape[1], dgate.shape[1]
    cl = _token_tile(tokens, TOKEN_ROWS)
    n_tiles = tokens // cl

    def body(x_ref, dres_ref, dup_ref, dus_ref, dq_ref, dgate_ref, g_ref, w_ref, gx_ref, gw_hbm, gg_ref, acc, sem):
        i = pl.program_id(0)

        @pl.when(i == 0)
        def _():
            acc[...] = jnp.zeros_like(acc)
            gg_ref[...] = jnp.zeros_like(gg_ref)

        x = x_ref[...]
        r = _rms(x)
        xn = x * r
        g = g_ref[...]
        h = (xn * g).astype(BF16)
        dproj = jnp.concatenate([dup_ref[...], dus_ref[...], dq_ref[...], dgate_ref[...]], axis=1).astype(BF16)
        dh = _mm_nt(dproj, w_ref[...])
        for j in range(N_DEV):
            acc[j] += _mm_tn(h, dproj[:, j * nb : (j + 1) * nb])
        gg_ref[...] += jnp.sum(dh * xn, axis=0, keepdims=True)
        dxn = dh * g
        gx_ref[...] = dres_ref[...] + r * (dxn - xn * jnp.mean(dxn * xn, axis=-1, keepdims=True))

        @pl.when(i == n_tiles - 1)
        def _():
            cp = pltpu.make_async_copy(acc, gw_hbm, sem)
            cp.start()
            cp.wait()

    return pl.pallas_call(
        body,
        name="in_proj_bwd",
        grid=(n_tiles,),
        in_specs=[
            pl.BlockSpec((cl, d), lambda i: (i, 0)),
            pl.BlockSpec((cl, d), lambda i: (i, 0)),
            pl.BlockSpec((cl, pool_w), lambda i: (i, 0)),
            pl.BlockSpec((cl, ssm_w), lambda i: (i, 0)),
            pl.BlockSpec((cl, att_w), lambda i: (i, 0)),
            pl.BlockSpec((cl, mix), lambda i: (i, 0)),
            pl.BlockSpec((1, d), lambda i: (0, 0)),
            pl.BlockSpec(w_in_f.shape, lambda i: (0, 0)),
        ],
        out_specs=[
            pl.BlockSpec((cl, d), lambda i: (i, 0)),
            pl.BlockSpec(memory_space=pl.ANY),
            pl.BlockSpec((1, d), lambda i: (0, 0)),
        ],
        out_shape=[
            jax.ShapeDtypeStruct((tokens, d), F32),
            jax.ShapeDtypeStruct((N_DEV, d, nb), F32),
            jax.ShapeDtypeStruct((1, d), F32),
        ],
        scratch_shapes=[pltpu.VMEM((N_DEV, d, nb), F32), pltpu.SemaphoreType.DMA],
        compiler_params=_params("arbitrary"),
    )(x2, dres, du_pool, du_ssm, dq, dgate, g_pre, w_in_f)


def _pool_geometry(seq, width):
    gw = width // len(POOL_WINDOWS)
    col = lax.broadcasted_iota(jnp.int32, (1, width), 1)
    win = jnp.full((1, width), float(POOL_WINDOWS[-1]), F32)
    for gi in range(len(POOL_WINDOWS) - 2, -1, -1):
        win = jnp.where(col < (gi + 1) * gw, float(POOL_WINDOWS[gi]), win)
    row = lax.broadcasted_iota(jnp.int32, (seq, width), 0)
    filling = 1.0 / (lax.broadcasted_iota(jnp.int32, (seq, 1), 0) + 1).astype(F32)
    inv_cnt = jnp.where(row + 1 < win.astype(jnp.int32), filling, 1.0 / win)
    return win, row, inv_cnt


def _window_sums(a, win, seq, back):
    pad = 2 * POOL_WINDOWS[-1]
    zeros = jnp.zeros((pad, a.shape[1]), F32)
    s = jnp.concatenate([a, zeros] if back else [zeros, a], axis=0)
    sums = []
    k = 1
    while k < POOL_WINDOWS[-1]:
        s = s + pltpu.roll(s, seq + pad - k if back else k, 0)
        k *= 2
        sums.append((k, s))
    out = sums[-1][1]
    for k, s in reversed(sums[:-1]):
        out = jnp.where(win <= float(k), s, out)
    return out[0:seq] if back else out[pad : pad + seq]


def _pool_fwd(u2, wp_blk, scale, batch, seq):
    width = scale.shape[1]

    def body(u_ref, w_ref, s_ref, y_ref, diff_ref):
        u = u_ref[...]
        win, row, inv_cnt = _pool_geometry(seq, width)
        diff = (_window_sums(u, win, seq, False) * inv_cnt - u).astype(BF16)
        diff_ref[...] = diff
        y_ref[...] = _mm(diff, w_ref[...]) * s_ref[...]

    return pl.pallas_call(
        body,
        name="pool_fwd",
        grid=(batch,),
        in_specs=[
            pl.BlockSpec((seq, width), lambda b: (b, 0)),
            pl.BlockSpec((width, width), lambda b: (0, 0)),
            pl.BlockSpec((1, width), lambda b: (0, 0)),
        ],
        out_specs=[pl.BlockSpec((seq, width), lambda b: (b, 0)), pl.BlockSpec((seq, width), lambda b: (b, 0))],
        out_shape=[
            jax.ShapeDtypeStruct((u2.shape[0], width), F32),
            jax.ShapeDtypeStruct((u2.shape[0], width), BF16),
        ],
        compiler_params=_params("arbitrary"),
    )(u2, wp_blk, scale)


def _pool_bwd(diff2, dy2, wp_blk, scale, batch, seq):
    width = scale.shape[1]

    def body(diff_ref, dy_ref, w_ref, s_ref, du_ref, gw_ref, gs_ref):
        @pl.when(pl.program_id(0) == 0)
        def _():
            gw_ref[...] = jnp.zeros_like(gw_ref)
            gs_ref[...] = jnp.zeros_like(gs_ref)

        diff = diff_ref[...]
        dy = dy_ref[...]
        win, row, inv_cnt = _pool_geometry(seq, width)
        gs_ref[...] += jnp.sum(dy * _mm(diff, w_ref[...]), axis=0, keepdims=True)
        dys = (dy * s_ref[...]).astype(BF16)
        gw_ref[...] += _mm_tn(diff, dys)
        dd = _mm_nt(dys, w_ref[...])
        du_ref[...] = _window_sums(dd * inv_cnt, win, seq, True) - dd

    return pl.pallas_call(
        body,
        name="pool_bwd",
        grid=(batch,),
        in_specs=[
            pl.BlockSpec((seq, width), lambda b: (b, 0)),
            pl.BlockSpec((seq, width), lambda b: (b, 0)),
            pl.BlockSpec((width, width), lambda b: (0, 0)),
            pl.BlockSpec((1, width), lambda b: (0, 0)),
        ],
        out_specs=[
            pl.BlockSpec((seq, width), lambda b: (b, 0)),
            pl.BlockSpec((width, width), lambda b: (0, 0)),
            pl.BlockSpec((1, width), lambda b: (0, 0)),
        ],
        out_shape=[
            jax.ShapeDtypeStruct(dy2.shape, F32),
            jax.ShapeDtypeStruct((width, width), F32),
            jax.ShapeDtypeStruct((1, width), F32),
        ],
        compiler_params=_params("arbitrary"),
    )(diff2, dy2, wp_blk, scale)


def _state_row(z, n_blocks):
    re = jnp.real(z).reshape(n_blocks, -1)
    im = jnp.imag(z).reshape(n_blocks, -1)
    return jnp.concatenate([re, im], axis=1).reshape(1, -1)


def _ssm_tables(a_re, a_im, log_dt, b_re, b_im, c_re, c_im):
    groups, n_state = a_re.shape
    ch = b_re.shape[2]
    nb = groups * ch // LANES
    gl = groups // nb
    lam = lax.complex(a_re, a_im)
    lam_bar = jnp.exp(lam * jnp.exp(log_dt)[:, None])
    b_bar = ((lam_bar - 1.0) / lam)[..., None] * lax.complex(b_re, b_im)
    eye = jnp.eye(gl, dtype=F32)

    def rows_to_state(t):
        return jnp.einsum("sgnc,gh->sgchn", t.reshape(nb, gl, n_state, ch), eye).reshape(nb, gl * ch, gl * n_state)

    def state_to_rows(t):
        return jnp.einsum("sgcn,gh->shngc", t.reshape(nb, gl, ch, n_state), eye).reshape(nb, gl * n_state, gl * ch)

    b_tab = jnp.concatenate([rows_to_state(jnp.real(b_bar)), rows_to_state(jnp.imag(b_bar))], axis=2)
    c_tab = jnp.concatenate([state_to_rows(c_re), -state_to_rows(c_im)], axis=1)
    return _state_row(lam_bar, nb), b_tab, c_tab


def _lam_power(a_re, a_im, log_dt, power, scale, n_blocks):
    return _state_row(scale * jnp.exp(lax.complex(a_re, a_im) * jnp.exp(log_dt)[:, None] * power), n_blocks)


def _state_blocks(s2, n_blocks, width):
    half = s2 // n_blocks // 2
    assert half % width == 0
    return [(b * 2 * half + o, b * 2 * half + half + o) for b in range(n_blocks) for o in range(0, half, width)]


def _scan(src_ref, dst_ref, st_ref, lam8_ref, n_groups, s, n_blocks, reverse, store):
    lb = SCAN_LANES
    for re0, im0 in _state_blocks(2 * s, n_blocks, lb):
        cr, ci = pl.ds(re0, lb), pl.ds(im0, lb)
        lr = lam8_ref[:, cr]
        li = -lam8_ref[:, ci] if reverse else lam8_ref[:, ci]

        unroll = 1 if store else SCAN_UNROLL

        def step(i, carry, cr=cr, ci=ci, lr=lr, li=li):
            hr, hi = carry
            for k in range(unroll):
                grp = i * unroll + k
                grp = n_groups - 1 - grp if reverse else grp
                rows = pl.ds(pl.multiple_of(grp * SUBLANES, SUBLANES), SUBLANES)
                hr, hi = (lr * hr - li * hi + src_ref[rows, cr], lr * hi + li * hr + src_ref[rows, ci])
                if store:
                    dst_ref[rows, cr] = hr
                    dst_ref[rows, ci] = hi
            return hr, hi

        assert n_groups % unroll == 0
        hr, hi = lax.fori_loop(0, n_groups // unroll, step, (st_ref[:, cr], st_ref[:, ci]))
        st_ref[:, cr] = hr
        st_ref[:, ci] = hi


def _pack_state(re, im):
    hi = lax.bitcast_convert_type(re.astype(BF16).astype(F32), jnp.uint32)
    lo = lax.bitcast_convert_type(im.astype(BF16).astype(F32), jnp.uint32)
    return hi | (lo >> 16)


def _unpack_state(word):
    re = lax.bitcast_convert_type(word & jnp.uint32(0xFFFF0000), F32)
    im = lax.bitcast_convert_type(word << 16, F32)
    return re, im


def _scan_adjoint(dh_ref, hprev_ref, group0, stg_ref, acc_ref, lam8_ref, n_groups, s, n_blocks):
    lb = SCAN_LANES
    half = s // n_blocks
    for re0, im0 in _state_blocks(2 * s, n_blocks, lb):
        cr, ci = pl.ds(re0, lb), pl.ds(im0, lb)
        ch = pl.ds(re0 // (2 * half) * half + re0 % (2 * half), lb)
        lr, li = lam8_ref[:, cr], -lam8_ref[:, ci]

        def step(i, carry, cr=cr, ci=ci, ch=ch, lr=lr, li=li):
            gr, gi, ar, ai = carry
            grp = n_groups - 1 - i
            rows = pl.ds(pl.multiple_of(grp * SUBLANES, SUBLANES), SUBLANES)
            ngr = lr * gr - li * gi + dh_ref[rows, cr]
            ngi = lr * gi + li * gr + dh_ref[rows, ci]
            hr, hi = _unpack_state(hprev_ref[pl.ds(pl.multiple_of((group0 + grp) * SUBLANES, SUBLANES), SUBLANES), ch])
            ar = ar + hr * ngr + hi * ngi
            ai = ai + hr * ngi - hi * ngr
            dh_ref[rows, cr] = ngr
            dh_ref[rows, ci] = ngi
            return ngr, ngi, ar, ai

        init = (stg_ref[:, cr], stg_ref[:, ci], acc_ref[:, cr], acc_ref[:, ci])
        gr, gi, ar, ai = lax.fori_loop(0, n_groups, step, init)
        stg_ref[:, cr] = gr
        stg_ref[:, ci] = gi
        acc_ref[:, cr] = ar
        acc_ref[:, ci] = ai


def _chunk_starts(st_ref, init_ref, lcl_ref, s, n_blocks):
    w = s // n_blocks
    init_ref[0:1, :] = jnp.zeros((1, 2 * s), F32)
    for re0, im0 in _state_blocks(2 * s, n_blocks, w):
        re, im = pl.ds(re0, w), pl.ds(im0, w)
        ar, ai = lcl_ref[:, re], lcl_ref[:, im]
        cr = jnp.zeros((1, w), F32)
        ci = jnp.zeros((1, w), F32)
        for k in range(1, NCH):
            cr, ci = (ar * cr - ai * ci + st_ref[k - 1 : k, re], ar * ci + ai * cr + st_ref[k - 1 : k, im])
            init_ref[k : k + 1, re] = cr
            init_ref[k : k + 1, im] = ci


def _chunk_starts_adjoint(stg_ref, initg_ref, lcl_ref, s, n_blocks):
    w = s // n_blocks
    initg_ref[NCH - 1 : NCH, :] = jnp.zeros((1, 2 * s), F32)
    for re0, im0 in _state_blocks(2 * s, n_blocks, w):
        re, im = pl.ds(re0, w), pl.ds(im0, w)
        ar, ai = lcl_ref[:, re], -lcl_ref[:, im]
        gr = jnp.zeros((1, w), F32)
        gi = jnp.zeros((1, w), F32)
        for k in range(NCH - 2, -1, -1):
            gr, gi = (stg_ref[k + 1 : k + 2, re] + ar * gr - ai * gi, stg_ref[k + 1 : k + 2, im] + ar * gi + ai * gr)
            initg_ref[k : k + 1, re] = gr
            initg_ref[k : k + 1, im] = gi


def _ssm_rows(seq, want):
    rows = min(want, seq // 2)
    assert seq % rows == 0 and rows % SUBLANES == 0
    return rows


def _chunk_copies(hbm_ref, b, cm_ref, sems, to_cm, col0=0):
    cl, _, width = cm_ref.shape
    copies = []
    for k in range(NCH):
        nat, cm = hbm_ref.at[b, pl.ds(k * cl, cl), pl.ds(col0, width)], cm_ref.at[:, k, :]
        src, dst = (nat, cm) if to_cm else (cm, nat)
        copies.append(pltpu.make_async_copy(src, dst, sems.at[k]))
    return copies


def _blockwise(fn, n_blocks):
    return jnp.concatenate([fn(b) for b in range(n_blocks)], axis=1)


def _ssm_fwd(u, u_col, b_tab, c_tab, lam8, lcl, d_skip, w_glu, shards):
    batch, seq, _ = u.shape
    ns = len(shards)
    width = d_skip.shape[1]
    s = lam8.shape[1] // 2
    nb = b_tab.shape[0]
    sb = 2 * s // nb
    cl = seq // NCH
    rows = _ssm_rows(seq, SSM_ROWS)
    n_tiles = seq // rows
    n_groups = rows // SUBLANES

    def body(u_hbm, b_ref, c_ref, lam_ref, lcl_ref, d_ref, wg_ref, *rest):
        shard_hbm, (y_hbm, pre_ref, z_ref, init_ref) = rest[:ns], rest[ns : ns + 4]
        gathered = rest[ns + 4 : 2 * ns + 4]
        u_cm, y_cm, bu_all, st, sems, send_sems, recv_sems, own_sems = rest[2 * ns + 4 :]
        b, ph, t = pl.program_id(0), pl.program_id(1), pl.program_id(2)
        tile_groups = pl.ds(pl.multiple_of(t * n_groups, n_groups), n_groups)
        x_pos, y_pos, c_pos = _place()
        own = [pltpu.make_async_copy(shard_hbm[a], gathered[a].at[4 * x_pos + 2 * y_pos + c_pos], own_sems.at[a])
               for a in range(ns)]
        exchange_start, exchange_finish = _exchange_steps(
            ns, lambda a, slot: shard_hbm[a], gathered, send_sems, recv_sems)

        @pl.when((b == 0) & (ph == 0) & (t == 0))
        def _():
            exchange_start()
            for cp in own:
                cp.start()

        @pl.when((b == batch - 1) & (ph == 1) & (t == n_tiles - 1))
        def _():
            exchange_finish()
            for cp in own:
                cp.wait()

        @pl.when((ph == 0) & (t == 0))
        def _():
            loads = _chunk_copies(u_hbm, b, u_cm, sems, True, u_col)
            for cp in loads:
                cp.start()
            st[...] = jnp.zeros_like(st)
            for cp in loads:
                cp.wait()

        @pl.when((ph == 1) & (t == 0))
        def _():
            st[...] = init_ref[...]

        u_t = u_cm[tile_groups].reshape(rows, width)
        bu = bu_all.at[pl.ds(pl.multiple_of(t * rows, rows), rows)]

        @pl.when(ph == 0)
        def _():
            u_b = u_t.astype(BF16)
            for blk in range(nb):
                bu[:, blk * sb : (blk + 1) * sb] = _mm(u_b[:, blk * LANES : (blk + 1) * LANES], b_ref[blk])
            _scan(bu, bu, st, lam_ref, n_groups, s, nb, False, False)

        @pl.when((ph == 0) & (t == n_tiles - 1))
        def _():
            _chunk_starts(st, init_ref, lcl_ref, s, nb)

        @pl.when(ph == 1)
        def _():
            _scan(bu, bu, st, lam_ref, n_groups, s, nb, False, True)
            hs = lambda blk: _mm(bu[:, blk * sb : (blk + 1) * sb].astype(BF16), c_ref[blk])
            pre = _blockwise(hs, nb) + d_ref[...] * u_t
            z = _mm(_gelu(pre).astype(BF16), wg_ref[...])
            pre_ref[...] = pre
            z_ref[...] = z
            y = z[:, 0:width] * jax.nn.sigmoid(z[:, width : 2 * width])
            y_cm[tile_groups] = y.reshape(n_groups, SUBLANES, width)

        @pl.when((ph == 1) & (t == n_tiles - 1))
        def _():
            stores = _chunk_copies(y_hbm, b, y_cm, sems, False)
            for cp in stores:
                cp.start()
            for cp in stores:
                cp.wait()

    out_tile = lambda b, ph, t: (b, t * ph, 0)
    full = lambda a: pl.BlockSpec(a.shape, lambda b, ph, t: (0,) * a.ndim)
    hbm = pl.BlockSpec(memory_space=pl.ANY)
    res = pl.pallas_call(
        body,
        name="ssm_fwd",
        grid=(batch, 2, n_tiles),
        in_specs=[hbm, full(b_tab), full(c_tab), full(lam8), full(lcl), full(d_skip), full(w_glu)] + [hbm] * ns,
        out_specs=[
            hbm,
            pl.BlockSpec((None, rows, width), out_tile),
            pl.BlockSpec((None, rows, 2 * width), out_tile),
            pl.BlockSpec((None, SUBLANES, 2 * s), lambda b, ph, t: (b, 0, 0)),
        ]
        + [hbm] * ns,
        out_shape=[
            jax.ShapeDtypeStruct((batch, seq, width), F32),
            jax.ShapeDtypeStruct((batch, seq, width), F32),
            jax.ShapeDtypeStruct((batch, seq, 2 * width), F32),
            jax.ShapeDtypeStruct((batch, SUBLANES, 2 * s), F32),
        ]
        + [jax.ShapeDtypeStruct((N_DEV, *a.shape), a.dtype) for a in shards],
        scratch_shapes=[
            pltpu.VMEM((cl, NCH, width), F32),
            pltpu.VMEM((cl, NCH, width), F32),
            pltpu.VMEM((seq, 2 * s), F32),
            pltpu.VMEM((SUBLANES, 2 * s), F32),
            pltpu.SemaphoreType.DMA((NCH,)),
            pltpu.SemaphoreType.DMA((ns, 7)),
            pltpu.SemaphoreType.DMA((ns, 7)),
            pltpu.SemaphoreType.DMA((ns,)),
        ],
        compiler_params=_params("arbitrary", "arbitrary", "arbitrary"),
    )(u, b_tab, c_tab, lam8, lcl, d_skip, w_glu, *shards)
    return res[:4], res[4:]


def _ssm_bwd(u, u_col, pre_p, z_p, dy, init, b_tab, b_tab_t, c_tab_t, lam8, lcl, d_skip, w_glu, ready):
    batch, seq, _ = u.shape
    width = d_skip.shape[1]
    nr = len(ready)
    s = lam8.shape[1] // 2
    nb = b_tab.shape[0]
    sb = 2 * s // nb
    cl = seq // NCH
    rows = _ssm_rows(seq, SSM_ROWS)
    n_tiles = seq // rows
    n_groups = rows // SUBLANES

    def body(u_hbm, pre_ref, z_ref, dy_hbm, init_ref, b_ref, bt_ref, ct_ref, lam_ref, lcl_ref, d_ref, wg_ref, *rest):
        ready_hbm, rest = rest[:nr], rest[nr:]
        du_hbm, gb_ref, gc_ref, gwg_ref, gd_ref, glam_ref = rest[:6]
        landed_hbm, rest = rest[6 : 6 + nr], rest[6 + nr :]
        u_cm, dy_cm, work, hs_all, dpre_all, st, stg, initg, acc, sems, send_sems, recv_sems = rest
        b, ph, t = pl.program_id(0), pl.program_id(1), pl.program_id(2)
        half = s // nb
        exchange_start, exchange_finish = _exchange_steps(
            nr, lambda a, slot: ready_hbm[a].at[slot], landed_hbm, send_sems, recv_sems)
        first = (b == 0) & (ph == 0) & (t == 0)
        last = (b == batch - 1) & (ph == 2) & (t == n_tiles - 1)
        tile = jnp.where(ph == 0, t, n_tiles - 1 - t)
        tile_rows = pl.ds(pl.multiple_of(tile * rows, rows), rows)
        tile_groups = pl.ds(pl.multiple_of(tile * n_groups, n_groups), n_groups)
        lanes = lambda blk: slice(blk * LANES, (blk + 1) * LANES)
        states = lambda blk: slice(blk * sb, (blk + 1) * sb)

        @pl.when(first)
        def _():
            exchange_start()
            acc[...] = jnp.zeros_like(acc)
            gb_ref[...] = jnp.zeros_like(gb_ref)
            gc_ref[...] = jnp.zeros_like(gc_ref)
            gwg_ref[...] = jnp.zeros_like(gwg_ref)
            gd_ref[...] = jnp.zeros_like(gd_ref)

        @pl.when((ph == 0) & (t == 0))
        def _():
            loads = (_chunk_copies(u_hbm, b, u_cm, sems.at[0], True, u_col)
                     + _chunk_copies(dy_hbm, b, dy_cm, sems.at[1], True))
            for cp in loads:
                cp.start()
            st[...] = init_ref[...]
            for blk in range(nb):
                entry = init_ref[:, states(blk)]
                hs_all[0:SUBLANES, blk * half : (blk + 1) * half] = _pack_state(entry[:, 0:half], entry[:, half : 2 * half])
            for cp in loads:
                cp.wait()

        u_t = u_cm[tile_groups].reshape(rows, width)
        u_b = u_t.astype(BF16)

        @pl.when(ph == 0)
        def _():
            for blk in range(nb):
                work[:, states(blk)] = _mm(u_b[:, lanes(blk)], b_ref[blk])
            _scan(work, work, st, lam_ref, n_groups, s, nb, False, True)
            z = z_ref[...]
            dy_t = dy_cm[tile_groups].reshape(rows, width)
            pre = pre_ref[...]
            z1, sig = z[:, 0:width], jax.nn.sigmoid(z[:, width : 2 * width])
            dz = jnp.concatenate([dy_t * sig, dy_t * z1 * sig * (1.0 - sig)], axis=1).astype(BF16)
            gwg_ref[...] += _mm_tn(_gelu(pre).astype(BF16), dz)
            dpre = _mm_nt(dz, wg_ref[...]) * _gelu_grad(pre)
            dpre_all[tile_rows, :] = dpre
            gd_ref[...] += jnp.sum(dpre * u_t, axis=0, keepdims=True)
            dpre_b = dpre.astype(BF16)
            kept = pl.ds(pl.multiple_of(tile * rows + SUBLANES, SUBLANES), rows)
            for blk in range(nb):
                hs = work[:, states(blk)]
                gc_ref[blk] += _mm_tn(dpre_b[:, lanes(blk)], hs.astype(BF16))
                hs_all[kept, blk * half : (blk + 1) * half] = _pack_state(hs[:, 0:half], hs[:, half : 2 * half])

        @pl.when(ph >= 1)
        def _():
            dpre_b = dpre_all[tile_rows, :].astype(BF16)
            for blk in range(nb):
                work[:, states(blk)] = _mm(dpre_b[:, lanes(blk)], ct_ref[blk])

        @pl.when(ph == 1)
        def _():
            @pl.when(t == 0)
            def _():
                stg[...] = jnp.zeros_like(stg)

            _scan(work, work, stg, lam_ref, n_groups, s, nb, True, False)

            @pl.when(t == n_tiles - 1)
            def _():
                _chunk_starts_adjoint(stg, initg, lcl_ref, s, nb)

        @pl.when(ph == 2)
        def _():
            @pl.when(t == 0)
            def _():
                stg[...] = initg[...]

            _scan_adjoint(work, hs_all, tile * n_groups, stg, acc, lam_ref, n_groups, s, nb)
            du = lambda blk: _mm(work[:, states(blk)].astype(BF16), bt_ref[blk])
            du_t = _blockwise(du, nb) + dpre_all[tile_rows, :] * d_ref[...]
            dy_cm[tile_groups] = du_t.reshape(n_groups, SUBLANES, width)
            for blk in range(nb):
                gb_ref[blk] += _mm_tn(u_b[:, lanes(blk)], work[:, states(blk)].astype(BF16))

            @pl.when(t == n_tiles - 1)
            def _():
                stores = _chunk_copies(du_hbm, b, dy_cm, sems.at[0], False)
                for cp in stores:
                    cp.start()
                for cp in stores:
                    cp.wait()

        @pl.when(last)
        def _():
            glam_ref[...] = jnp.sum(acc[...], axis=0, keepdims=True)
            exchange_finish()

    def tile(b, ph, t):
        return (b, jnp.where(ph == 0, t, n_tiles - 1 - t), 0)

    full = lambda a: pl.BlockSpec(a.shape, lambda b, ph, t: (0,) * a.ndim)
    hbm = pl.BlockSpec(memory_space=pl.ANY)
    res = pl.pallas_call(
        body,
        name="ssm_bwd",
        grid=(batch, 3, n_tiles),
        in_specs=[
            hbm,
            pl.BlockSpec((None, rows, width), tile),
            pl.BlockSpec((None, rows, 2 * width), tile),
            hbm,
            pl.BlockSpec((None, SUBLANES, 2 * s), lambda b, ph, t: (b, 0, 0)),
            full(b_tab), full(b_tab_t), full(c_tab_t), full(lam8), full(lcl), full(d_skip), full(w_glu),
        ]
        + [hbm] * nr,
        out_specs=[
            hbm,
            full(b_tab), full(b_tab), full(w_glu), full(d_skip),
            pl.BlockSpec((1, 2 * s), lambda b, ph, t: (0, 0)),
        ]
        + [hbm] * nr,
        out_shape=[
            jax.ShapeDtypeStruct((batch, seq, width), F32),
            jax.ShapeDtypeStruct(b_tab.shape, F32),
            jax.ShapeDtypeStruct(b_tab.shape, F32),
            jax.ShapeDtypeStruct(w_glu.shape, F32),
            jax.ShapeDtypeStruct(d_skip.shape, F32),
            jax.ShapeDtypeStruct((1, 2 * s), F32),
        ]
        + [jax.ShapeDtypeStruct(a.shape, F32) for a in ready],
        scratch_shapes=[
            pltpu.VMEM((cl, NCH, width), F32),
            pltpu.VMEM((cl, NCH, width), F32),
            pltpu.VMEM((rows, 2 * s), F32),
            pltpu.VMEM((seq + SUBLANES, s), jnp.uint32),
            pltpu.VMEM((seq, width), F32),
        ]
        + [pltpu.VMEM((SUBLANES, 2 * s), F32)] * 4
        + [pltpu.SemaphoreType.DMA((2, NCH)), pltpu.SemaphoreType.DMA((nr, 7)), pltpu.SemaphoreType.DMA((nr, 7))],
        compiler_params=_params("arbitrary", "arbitrary", "arbitrary"),
    )(u, pre_p, z_p, dy, init, b_tab, b_tab_t, c_tab_t, lam8, lcl, d_skip, w_glu, *ready)
    return res[:6], res[6:]


def _kv_fwd(mem, g_mem, w_kv):
    batch, n_mem, d = mem.shape
    kvw = w_kv.shape[1]

    def body(mem_ref, g_ref, w_ref, kv_ref):
        m = mem_ref[...]
        kv_ref[...] = _mm((m * _rms(m) * g_ref[...]).astype(BF16), w_ref[...])

    return pl.pallas_call(
        body,
        name="kv_fwd",
        grid=(batch,),
        in_specs=[
            pl.BlockSpec((None, n_mem, d), lambda b: (b, 0, 0)),
            pl.BlockSpec((1, d), lambda b: (0, 0)),
            pl.BlockSpec((d, kvw), lambda b: (0, 0)),
        ],
        out_specs=pl.BlockSpec((None, n_mem, kvw), lambda b: (b, 0, 0)),
        out_shape=jax.ShapeDtypeStruct((batch, n_mem, kvw), F32),
        compiler_params=_params("arbitrary"),
    )(mem, g_mem, w_kv)


def _kv_bwd(mem, dkv, g_mem, w_kv):
    batch, n_mem, d = mem.shape
    kvw = w_kv.shape[1]

    def body(mem_ref, dkv_ref, g_ref, w_ref, gw_ref, gg_ref):
        @pl.when(pl.program_id(0) == 0)
        def _():
            gw_ref[...] = jnp.zeros_like(gw_ref)
            gg_ref[...] = jnp.zeros_like(gg_ref)

        m = mem_ref[...]
        mn = m * _rms(m)
        dkv_b = dkv_ref[...].astype(BF16)
        gw_ref[...] += _mm_tn((mn * g_ref[...]).astype(BF16), dkv_b)
        gg_ref[...] += jnp.sum(_mm_nt(dkv_b, w_ref[...]) * mn, axis=0, keepdims=True)

    return pl.pallas_call(
        body,
        name="kv_bwd",
        grid=(batch,),
        in_specs=[
            pl.BlockSpec((None, n_mem, d), lambda b: (b, 0, 0)),
            pl.BlockSpec((None, n_mem, kvw), lambda b: (b, 0, 0)),
            pl.BlockSpec((1, d), lambda b: (0, 0)),
            pl.BlockSpec((d, kvw), lambda b: (0, 0)),
        ],
        out_specs=[pl.BlockSpec((d, kvw), lambda b: (0, 0)), pl.BlockSpec((1, d), lambda b: (0, 0))],
        out_shape=[jax.ShapeDtypeStruct((d, kvw), F32), jax.ShapeDtypeStruct((1, d), F32)],
        compiler_params=_params("arbitrary"),
    )(mem, dkv, g_mem, w_kv)


def _tail(x2, target2, proj, y_pool, y_ssm, kv, w_out, g_post):
    tokens, d = x2.shape
    batch, n_mem, kvw = kv.shape
    pool_w, ssm_w, att_w, mix = y_pool.shape[1], y_ssm.shape[1], kvw // 2, w_out.shape[0]
    assert (mix - att_w) % att_w == 0 and proj.shape[1] == 2 * mix
    hd = att_w // MEM_HEADS
    cl = _token_tile(tokens // batch, TOKEN_ROWS)
    n_tiles = tokens // cl
    per_seq = tokens // batch // cl
    qk_scale = hd**-0.5

    def body(x_ref, tg_ref, gate_ref, yp_ref, ys_ref, q_ref, kv_ref, w_ref, g_ref,
             dres_ref, dgate_ref, dyp_ref, dys_ref, dq_ref, dkv_ref, gw_hbm, gg_ref, loss_ref, acc, sem):
        i = pl.program_id(0)

        @pl.when(i == 0)
        def _():
            acc[...] = jnp.zeros_like(acc)
            gg_ref[...] = jnp.zeros_like(gg_ref)
            loss_ref[...] = jnp.zeros_like(loss_ref)

        @pl.when(i % per_seq == 0)
        def _():
            dkv_ref[...] = jnp.zeros_like(dkv_ref)

        k = kv_ref[:, 0:att_w].astype(BF16)
        v = kv_ref[:, att_w : 2 * att_w].astype(BF16)
        lane = lax.broadcasted_iota(jnp.int32, (1, att_w), 1)
        heads = [(lane >= h * hd) & (lane < (h + 1) * hd) for h in range(MEM_HEADS)]
        g = g_ref[...]

        def part(rows):
            n_rows = rows.stop - rows.start
            q = q_ref[rows, :]
            probs, q_heads = [], []
            att = jnp.zeros((n_rows, att_w), F32)
            for mask in heads:
                qh = jnp.where(mask, q, 0.0).astype(BF16)
                sc = _mm_nt(qh, k) * qk_scale
                e = jnp.exp(sc - jnp.max(sc, axis=-1, keepdims=True))
                p = e * (1.0 / jnp.sum(e, axis=-1, keepdims=True))
                att = att + jnp.where(mask, _mm(p.astype(BF16), v), 0.0)
                probs.append(p)
                q_heads.append(qh)

            ycat = jnp.concatenate([yp_ref[rows, :], ys_ref[rows, :], att], axis=1)
            gate = gate_ref[rows, :]
            sig = jax.nn.sigmoid(gate)
            silu = gate * sig
            yg = (ycat * silu).astype(BF16)
            out = _mm(yg, w_ref[...])
            r = _rms(out)
            on = out * r
            err = x_ref[rows, :] + on * g - tg_ref[rows, :]
            loss_ref[...] += 0.5 * jnp.sum(jnp.mean(err * err, axis=-1, keepdims=True), axis=0, keepdims=True)
            dres = err * (1.0 / d)
            dres_ref[rows, :] = dres
            gg_ref[...] += jnp.sum(dres * on, axis=0, keepdims=True)
            don = dres * g
            dout = (r * (don - on * jnp.mean(don * on, axis=-1, keepdims=True))).astype(BF16)
            acc[...] += _mm_tn(yg, dout)
            dyg = _mm_nt(dout, w_ref[...])
            dgate_ref[rows, :] = dyg * ycat * (sig * (1.0 + gate * (1.0 - sig)))
            dycat = dyg * silu
            dyp_ref[rows, :] = dycat[:, 0:pool_w]
            dys_ref[rows, :] = dycat[:, pool_w : pool_w + ssm_w]
            datt = dycat[:, pool_w + ssm_w : mix]

            dq = jnp.zeros((n_rows, att_w), F32)
            dk = jnp.zeros((n_mem, att_w), F32)
            dv = jnp.zeros((n_mem, att_w), F32)
            for mask, p, qh in zip(heads, probs, q_heads):
                doh = jnp.where(mask, datt, 0.0).astype(BF16)
                dp = _mm_nt(doh, v)
                ds = (p * (dp - jnp.sum(p * dp, axis=-1, keepdims=True)) * qk_scale).astype(BF16)
                dq = dq + jnp.where(mask, _mm(ds, k), 0.0)
                dk = dk + _mm_tn(ds, qh)
                dv = dv + _mm_tn(p.astype(BF16), doh)
            dq_ref[rows, :] = dq
            dkv_ref[:, 0:att_w] += dk
            dkv_ref[:, att_w : 2 * att_w] += dv

        part(slice(0, cl))

        @pl.when(i == n_tiles - 1)
        def _():
            cp = pltpu.make_async_copy(acc, gw_hbm, sem)
            cp.start()
            cp.wait()

    tok = lambda w: pl.BlockSpec((cl, w), lambda i: (i, 0))
    chunked = tok(ssm_w)
    per_batch = pl.BlockSpec((None, n_mem, kvw), lambda i: (i // per_seq, 0, 0))
    return pl.pallas_call(
        body,
        name="tail",
        grid=(n_tiles,),
        in_specs=[
            tok(d), tok(d), pl.BlockSpec((cl, mix), lambda i: (i, 1)), tok(pool_w), chunked,
            pl.BlockSpec((cl, att_w), lambda i: (i, (mix - att_w) // att_w)), per_batch,
            pl.BlockSpec((mix, d), lambda i: (0, 0)),
            pl.BlockSpec((1, d), lambda i: (0, 0)),
        ],
        out_specs=[
            tok(d), tok(mix), tok(pool_w), chunked, tok(att_w), per_batch,
            pl.BlockSpec(memory_space=pl.ANY),
            pl.BlockSpec((1, d), lambda i: (0, 0)),
            pl.BlockSpec((1, 1), lambda i: (0, 0)),
        ],
        out_shape=[
            jax.ShapeDtypeStruct((tokens, d), F32),
            jax.ShapeDtypeStruct((tokens, mix), F32),
            jax.ShapeDtypeStruct((tokens, pool_w), F32),
            jax.ShapeDtypeStruct((tokens, ssm_w), F32),
            jax.ShapeDtypeStruct((tokens, att_w), F32),
            jax.ShapeDtypeStruct(kv.shape, F32),
            jax.ShapeDtypeStruct((mix, d), F32),
            jax.ShapeDtypeStruct((1, d), F32),
            jax.ShapeDtypeStruct((1, 1), F32),
        ],
        scratch_shapes=[pltpu.VMEM((mix, d), F32), pltpu.SemaphoreType.DMA],
        compiler_params=_params("arbitrary"),
    )(x2, target2, proj, y_pool, y_ssm, proj, kv, w_out, g_post)


def _pack(arrays):
    flat = jnp.concatenate([a.reshape(-1) for a in arrays])
    rows = -(-flat.size // (N_DEV * SUBLANES * LANES)) * N_DEV * SUBLANES
    return jnp.pad(flat, (0, rows * LANES - flat.size)).reshape(rows, LANES)


def _unpack(packed, like):
    flat, out, at = packed.reshape(-1), [], 0
    for a in like:
        out.append(flat[at : at + a.size].reshape(a.shape))
        at += a.size
    return out


def kernel(x, mem, g_pre, w_in, w_pool, pool_scale, a_re, a_im, log_dt, b_re, b_im, c_re, c_im, d_skip, w_glu, g_mem, w_kv, w_out, g_post, loss_target, m_g_pre, m_w_in, m_w_pool, m_pool_scale, m_a_re, m_a_im, m_log_dt, m_b_re, m_b_im, m_c_re, m_c_im, m_d_skip, m_w_glu, m_g_mem, m_w_kv, m_w_out, m_g_post, v_g_pre, v_w_in, v_w_pool, v_pool_scale, v_a_re, v_a_im, v_log_dt, v_b_re, v_b_im, v_c_re, v_c_im, v_d_skip, v_w_glu, v_g_mem, v_w_kv, v_w_out, v_g_post):
    batch, seq, d = x.shape
    cl = seq // NCH
    pool_w, ssm_w = pool_scale.shape[1], d_skip.shape[1]
    att_w = w_kv.shape[2] // 2
    tokens = batch * seq
    x2 = x.reshape(tokens, d)
    target2 = loss_target.reshape(tokens, d)

    wp_blk = jax.scipy.linalg.block_diag(*w_pool[0]).astype(BF16)
    ssm_params = (a_re[0], a_im[0], log_dt[0], b_re[0], b_im[0], c_re[0], c_im[0])
    (lam_row, b_tab, c_tab), tables_vjp = jax.vjp(_ssm_tables, *ssm_params)
    nb = b_tab.shape[0]
    lam8 = jnp.broadcast_to(lam_row, (SUBLANES, lam_row.shape[1]))
    lcl = _lam_power(a_re[0], a_im[0], log_dt[0], float(cl), 1.0, nb)
    b_bf, c_bf = b_tab.astype(BF16), c_tab.astype(BF16)

    proj, w_in_f, (w_glu_g,) = _in_proj(x2, g_pre, w_in[0], [w_glu[0].astype(BF16)])
    proj3 = proj.reshape(batch, seq, proj.shape[1])
    w_glu_f = w_glu_g.transpose(1, 0, 2).reshape(w_glu_g.shape[1], N_DEV * w_glu_g.shape[2])
    y_pool, diff_pool = _pool_fwd(proj, wp_blk, pool_scale, batch, seq)
    (y_ssm, pre_ssm, z_ssm, init_ssm), (w_out_g, w_kv_g) = _ssm_fwd(
        proj3, pool_w, b_bf, c_bf, lam8, lcl, d_skip, w_glu_f, [w_out[0].astype(BF16), w_kv[0].astype(BF16)])
    w_out_f = w_out_g.reshape(N_DEV * w_out_g.shape[1], w_out_g.shape[2])
    w_kv_f = w_kv_g.reshape(N_DEV * w_kv_g.shape[1], w_kv_g.shape[2])
    kv = _kv_fwd(mem, g_mem, w_kv_f)

    dres, dgate, dy_pool, dy_ssm, dq, dkv, gw_out, gg_post, loss_part = _tail(
        x2, target2, proj, y_pool, y_ssm.reshape(tokens, ssm_w), kv, w_out_f, g_post)

    gw_kv, gg_mem = _kv_bwd(mem, dkv, g_mem, w_kv_f)
    du_pool, gwp_dense, g_scale = _pool_bwd(diff_pool, dy_pool, wp_blk, pool_scale, batch, seq)
    gw_kv8 = gw_kv.reshape(N_DEV, -1, gw_kv.shape[1])
    gw_out8 = gw_out.reshape(N_DEV, -1, gw_out.shape[1])
    (du_ssm, gb_tab, gc_tab_t, gw_glu, gd_skip, glam), (kv_landed, out_landed) = _ssm_bwd(
        proj3, pool_w, pre_ssm, z_ssm, dy_ssm.reshape(batch, seq, ssm_w), init_ssm, b_bf, b_bf.transpose(0, 2, 1),
        c_bf.transpose(0, 2, 1), lam8, lcl, d_skip, w_glu_f, [gw_kv8, gw_out8])
    grad_x2, gw_in, gg_pre = _in_proj_bwd(
        x2, dres, du_pool, du_ssm.reshape(tokens, ssm_w), dq, dgate, g_pre,
        w_in_f)

    gw = pool_w // len(POOL_WINDOWS)
    gw_pool = jnp.stack([gwp_dense[i * gw : (i + 1) * gw, i * gw : (i + 1) * gw] for i in range(len(POOL_WINDOWS))])
    g_ssm = tables_vjp((glam, gb_tab, gc_tab_t.transpose(0, 2, 1)))

    small_w = [g_pre, w_pool, pool_scale, a_re, a_im, log_dt, b_re, b_im, c_re, c_im, d_skip, g_mem, g_post]
    small_m = [m_g_pre, m_w_pool, m_pool_scale, m_a_re, m_a_im, m_log_dt, m_b_re, m_b_im, m_c_re, m_c_im, m_d_skip, m_g_mem, m_g_post]
    small_v = [v_g_pre, v_w_pool, v_pool_scale, v_a_re, v_a_im, v_log_dt, v_b_re, v_b_im, v_c_re, v_c_im, v_d_skip, v_g_mem, v_g_post]
    small_g = [gg_pre, gw_pool, g_scale, *g_ssm, gd_skip, gg_mem, gg_post]
    big_g, small_sum = _reduce_all(
        [gw_in, gw_glu.reshape(ssm_w, N_DEV, -1).transpose(1, 0, 2)],
        _pack(small_g + [loss_part]),
        [(gw_kv8, kv_landed), (gw_out8, out_landed)])

    flat2 = lambda a: a.reshape(-1, a.shape[-1])
    sg = _unpack(small_sum, [flat2(a) for a in small_w] + [loss_part])
    loss = sg[-1].reshape(())
    small_names = ["g_pre", "w_pool", "pool_scale", "a_re", "a_im", "log_dt", "b_re", "b_im", "c_re", "c_im",
                   "d_skip", "g_mem", "g_post"]
    names = ["w_in", "w_glu", "w_kv", "w_out"] + small_names
    all_w = [w_in, w_glu, w_kv, w_out] + small_w
    all_m = [m_w_in, m_w_glu, m_w_kv, m_w_out] + small_m
    all_v = [v_w_in, v_w_glu, v_w_kv, v_w_out] + small_v
    updates = _adamw_all(big_g + sg[:-1], [flat2(a) for a in all_w], [flat2(a) for a in all_m], [flat2(a) for a in all_v])
    updates = {name: [t.reshape(a.shape) for t in u] for name, u, a in zip(names, updates, all_w)}

    order = ["g_pre", "w_in", "w_pool", "pool_scale", "a_re", "a_im", "log_dt", "b_re", "b_im", "c_re", "c_im",
             "d_skip", "w_glu", "g_mem", "w_kv", "w_out", "g_post"]
    outs = [[updates[name][kind] for name in order] for kind in range(4)]
    return (loss, grad_x2.reshape(batch, seq, d), *outs[0], *outs[1], *outs[2], *outs[3])
```

```python
import math

import jax
import jax.numpy as jnp
from jax import lax
from jax.experimental import pallas as pl
from jax.experimental.pallas import tpu as pltpu

F32 = jnp.float32
BF16 = jnp.bfloat16
MESH = pl.DeviceIdType.MESH

N_DEV = 8
SUBLANES = 8
LANES = 128
NCH = SUBLANES
VMEM_LIMIT = 60 * 1024 * 1024

IN_PROJ_ROWS = 2048
TOKEN_ROWS = 512
SSM_ROWS = 1024
SCAN_LANES = 512
SCAN_UNROLL = 8

EPS = 1e-6
POOL_WINDOWS = (2, 4, 8, 16)
MEM_HEADS = 4
ADAM_LR, ADAM_B1, ADAM_B2, ADAM_EPS, ADAM_WD, ADAM_STEP = 0.001, 0.9, 0.999, 1e-08, 0.01, 10


def _mm(a, b):
    return jnp.dot(a, b, preferred_element_type=F32)


def _mm_nt(a, b):
    return lax.dot_general(a, b, (((1,), (1,)), ((), ())), preferred_element_type=F32)


def _mm_tn(a, b):
    return lax.dot_general(a, b, (((0,), (0,)), ((), ())), preferred_element_type=F32)


def _params(*sem):
    return pltpu.CompilerParams(dimension_semantics=sem or None, vmem_limit_bytes=VMEM_LIMIT)


def _adamw(w, g, m, v):
    m = ADAM_B1 * m + (1.0 - ADAM_B1) * g
    v = ADAM_B2 * v + (1.0 - ADAM_B2) * (g * g)
    m_hat = m / (1.0 - ADAM_B1**ADAM_STEP)
    v_hat = v / (1.0 - ADAM_B2**ADAM_STEP)
    delta = -ADAM_LR * (m_hat / (jnp.sqrt(v_hat) + ADAM_EPS) + ADAM_WD * w)
    return delta, m, v


def _gelu(x):
    k = math.sqrt(2.0 / math.pi)
    return 0.5 * x * (1.0 + jnp.tanh(k * (x + 0.044715 * x * x * x)))


def _gelu_grad(x):
    k = math.sqrt(2.0 / math.pi)
    th = jnp.tanh(k * (x + 0.044715 * x * x * x))
    return 0.5 * (1.0 + th) + 0.5 * x * (1.0 - th * th) * (k * (1.0 + 3.0 * 0.044715 * x * x))


def _place():
    return lax.axis_index("x"), lax.axis_index("y"), lax.axis_index("c")


def _exchange_steps(n, src_of, landing, send_sems, recv_sems):
    x, y, c = _place()
    me = 4 * x + 2 * y + c
    peers = []
    for j in range(1, N_DEV):
        px = 1 - x if j & 4 else x
        py = 1 - y if j & 2 else y
        pc = 1 - c if j & 1 else c
        peers.append((px, py, pc))

    def copy(a, j, from_slot, to_slot, peer):
        return pltpu.make_async_remote_copy(
            src_ref=src_of(a, to_slot), dst_ref=landing[a].at[from_slot],
            send_sem=send_sems.at[a, j], recv_sem=recv_sems.at[a, j], device_id=peer, device_id_type=MESH)

    def start():
        for a in range(n):
            for j, p in enumerate(peers):
                copy(a, j, me, 4 * p[0] + 2 * p[1] + p[2], p).start()

    def finish():
        for a in range(n):
            for j, p in enumerate(peers):
                slot = 4 * p[0] + 2 * p[1] + p[2]
                copy(a, j, slot, slot, p).wait_recv()
        for a in range(n):
            for j, p in enumerate(peers):
                copy(a, j, me, 4 * p[0] + 2 * p[1] + p[2], p).wait_send()

    return start, finish


def _adamw_all(gs, ws, ms, vs):
    n = len(gs)

    def body(*refs):
        g, w, m, v = refs[:n], refs[n : 2 * n], refs[2 * n : 3 * n], refs[3 * n : 4 * n]
        outs = refs[4 * n :]
        for a in range(n):
            rows = g[a].shape[0]
            chunk = math.gcd(rows, 128)

            def step(i, _, a=a, chunk=chunk):
                r = pl.ds(pl.multiple_of(i * chunk, chunk), chunk)
                grad = g[a][r, :]
                delta, nm, nv = _adamw(w[a][r, :], grad, m[a][r, :], v[a][r, :])
                outs[4 * a][r, :] = grad
                outs[4 * a + 1][r, :] = delta
                outs[4 * a + 2][r, :] = nm
                outs[4 * a + 3][r, :] = nv
                return 0

            lax.fori_loop(0, rows // chunk, step, 0)

    vmem = pl.BlockSpec(memory_space=pltpu.VMEM)
    out_shape = []
    for wa in ws:
        out_shape += [jax.ShapeDtypeStruct(wa.shape, F32)] * 4
    res = pl.pallas_call(
        body,
        name="adamw_all",
        out_shape=out_shape,
        in_specs=[vmem] * (4 * n),
        out_specs=[vmem] * (4 * n),
        compiler_params=_params(),
    )(*gs, *ws, *ms, *vs)
    return [tuple(res[4 * a : 4 * a + 4]) for a in range(n)]


def _reduce_all(parts, small, early):
    n, ne = len(parts), len(early)
    parts4 = [p.reshape(4, 2, *p.shape[1:]) for p in parts]
    blks = [p.shape[1:] for p in parts]

    def body(*refs):
        refs = list(refs)
        take = lambda k: [refs.pop(0) for _ in range(k)]
        part = take(n)
        (small_ref,) = take(1)
        early_in = [take(2) for _ in range(ne)]
        outs = take(n)
        (small_all,) = take(1)
        early_out = take(ne)
        own, r1, got_a1, got_a2, got_b1, got_b2, pass_a, pass_b = (take(n) for _ in range(8))
        early_buf = take(ne)
        small_land, small_mine = take(2)
        s1_send, s1_recv, h_send, h_recv, loc, rs_send, rs_recv, ag_send, ag_recv, early_sems = refs
        x, y, c = _place()
        me = 4 * x + 2 * y + c
        piece = small_mine.shape[0]
        piece_of = lambda slot: pl.ds(pl.multiple_of(slot * piece, SUBLANES), piece)
        rs_start, rs_finish = _exchange_steps(
            1, lambda a, slot: small_ref.at[piece_of(slot)], [small_land], rs_send, rs_recv)
        ag_start, ag_finish = _exchange_steps(1, lambda a, slot: small_mine, [small_all], ag_send, ag_recv)
        landed = [pltpu.make_async_copy(early_in[e][1], early_buf[e], early_sems.at[e, 0]) for e in range(ne)]
        for cp in landed:
            cp.start()
        sibling = (x, y, 1 - c)
        chips = [(1 - x, y), (x, 1 - y), (1 - x, 1 - y)]

        def rowwise(rows, fn):
            chunk = math.gcd(rows, 128)

            def step(i, _):
                fn(pl.ds(pl.multiple_of(i * chunk, chunk), chunk))
                return 0

            lax.fori_loop(0, rows // chunk, step, 0)

        stage1, local = [], []
        for a in range(n):
            cp = pltpu.make_async_remote_copy(
                src_ref=part[a].at[:, 1 - c], dst_ref=r1[a], send_sem=s1_send.at[a], recv_sem=s1_recv.at[a],
                device_id=sibling, device_id_type=MESH)
            cp.start()
            stage1.append(cp)
            lc = pltpu.make_async_copy(part[a].at[:, c], own[a], loc.at[a])
            lc.start()
            local.append(lc)
        rs_start()
        x_nbr, y_nbr = (1 - x, y, c), (x, 1 - y, c)
        mine, mine_x, mine_y = 2 * x + y, 2 * (1 - x) + y, 2 * x + (1 - y)

        def hop(a, k, src, dst, to):
            return pltpu.make_async_remote_copy(
                src_ref=src, dst_ref=dst, send_sem=h_send.at[a, k], recv_sem=h_recv.at[a, k],
                device_id=to, device_id_type=MESH)

        first, second = [], []
        for a in range(n):
            half = blks[a][0] // 2
            up, low = pl.ds(0, half), pl.ds(half, half)
            local[a].wait()
            stage1[a].wait_recv()
            for chip in range(4):

                def add(r, a=a, chip=chip):
                    own[a][chip, r, :] = own[a][chip, r, :] + r1[a][chip, r, :]

                rowwise(blks[a][0], add)
            first.append([
                hop(a, 0, own[a].at[pl.ds(2 * (1 - x), 2), up], got_a1[a], x_nbr),
                hop(a, 2, own[a].at[2 * x + (1 - y), low], got_b1[a].at[x], y_nbr),
                hop(a, 3, own[a].at[2 * (1 - x) + (1 - y), low], got_b1[a].at[1 - x], y_nbr),
            ])
            for cp in first[a]:
                cp.start()
        rs_finish()
        small_land[me] = small_ref[piece_of(me), :]

        def sum_piece(i, _):
            r = pl.ds(pl.multiple_of(i * SUBLANES, SUBLANES), SUBLANES)
            total = small_land[0, r, :]
            for dev in range(1, N_DEV):
                total = total + small_land[dev, r, :]
            small_mine[r, :] = total
            small_all[me, r, :] = total
            return 0

        lax.fori_loop(0, piece // SUBLANES, sum_piece, 0)
        ag_start()
        for a in range(n):
            half = blks[a][0] // 2
            first[a][0].wait_recv()

            def fold_upper(r, a=a):
                own[a][mine, r, :] = own[a][mine, r, :] + got_a1[a][y, r, :]
                pass_a[a][r, :] = own[a][mine_y, r, :] + got_a1[a][1 - y, r, :]

            rowwise(half, fold_upper)
            first[a][1].wait_recv()
            first[a][2].wait_recv()

            def fold_lower(r, a=a, half=half):
                rl = pl.ds(pl.multiple_of(r.start + half, SUBLANES), r.size)
                own[a][mine, rl, :] = own[a][mine, rl, :] + got_b1[a][x, r, :]
                pass_b[a][r, :] = own[a][mine_x, rl, :] + got_b1[a][1 - x, r, :]

            rowwise(half, fold_lower)
            second.append([hop(a, 1, pass_a[a], got_a2[a], y_nbr), hop(a, 4, pass_b[a], got_b2[a], x_nbr)])
            for cp in second[a]:
                cp.start()
        for e in range(ne):
            part_e, _ = early_in[e]
            landed[e].wait()
            own_block = pltpu.make_async_copy(part_e.at[me], early_buf[e].at[me], early_sems.at[e, 1])
            own_block.start()
            own_block.wait()

            def sum_early(r, e=e):
                g = early_buf[e][0, r, :]
                for dev in range(1, N_DEV):
                    g = g + early_buf[e][dev, r, :]
                early_out[e][r, :] = g

            rowwise(early_buf[e].shape[1], sum_early)
        for a in range(n):
            half = blks[a][0] // 2
            second[a][0].wait_recv()
            second[a][1].wait_recv()

            def finish_rows(r, a=a, half=half):
                rl = pl.ds(pl.multiple_of(r.start + half, SUBLANES), r.size)
                outs[a][r, :] = own[a][mine, r, :] + got_a2[a][r, :]
                outs[a][rl, :] = own[a][mine, rl, :] + got_b2[a][r, :]

            rowwise(half, finish_rows)
        ag_finish()
        for cp in stage1 + [cp for group in first + second for cp in group]:
            cp.wait_send()

    vmem = pl.BlockSpec(memory_space=pltpu.VMEM)
    hbm = pl.BlockSpec(memory_space=pl.ANY)
    piece = small.shape[0] // N_DEV
    assert piece * N_DEV == small.shape[0] and piece % SUBLANES == 0
    out_shape = [jax.ShapeDtypeStruct(b, F32) for b in blks]
    out_shape += [jax.ShapeDtypeStruct((N_DEV, piece, LANES), F32)]
    out_shape += [jax.ShapeDtypeStruct(e[0].shape[1:], F32) for e in early]
    halves = [(b[0] // 2, b[1]) for b in blks]
    scratch = (
        [pltpu.VMEM((4, *b), F32) for b in blks]
        + [pltpu.VMEM((4, *b), F32) for b in blks]
        + [pltpu.VMEM((2, *h), F32) for h in halves]
        + [pltpu.VMEM(h, F32) for h in halves]
        + [pltpu.VMEM((2, *h), F32) for h in halves]
        + [pltpu.VMEM(h, F32) for h in halves] * 3
        + [pltpu.VMEM(e[0].shape, F32) for e in early]
        + [pltpu.VMEM((N_DEV, piece, LANES), F32), pltpu.VMEM((piece, LANES), F32)]
        + [pltpu.SemaphoreType.DMA((n,)), pltpu.SemaphoreType.DMA((n,)), pltpu.SemaphoreType.DMA((n, 5)),
           pltpu.SemaphoreType.DMA((n, 5)), pltpu.SemaphoreType.DMA((n,))]
        + [pltpu.SemaphoreType.DMA((1, 7))] * 4
        + [pltpu.SemaphoreType.DMA((ne, 2))]
    )
    res = pl.pallas_call(
        body,
        name="reduce_all",
        out_shape=out_shape,
        in_specs=[hbm] * n + [vmem] + [hbm, hbm] * ne,
        out_specs=[vmem] * (n + 1 + ne),
        scratch_shapes=scratch,
        compiler_params=_params(),
    )(*parts4, small, *[t for e in early for t in e])
    return list(res[:n]) + list(res[n + 1 :]), res[n].reshape(small.shape)


def _rms(x):
    return lax.rsqrt(jnp.mean(x * x, axis=-1, keepdims=True) + EPS)


def _token_tile(tokens, want):
    tile = min(want, tokens // 2)
    assert tokens % tile == 0 and tile % 16 == 0
    return tile


def _in_proj(x2, g_pre, w_in_blk, shards):
    tokens, d = x2.shape
    nb = w_in_blk.shape[1]
    tm = _token_tile(tokens, IN_PROJ_ROWS)
    n_t = tokens // tm
    ns = len(shards)
    x_pos, y_pos, c_pos = _place()
    slot = lambda px, py, pc: 4 * px + 2 * py + pc
    chip_order = [(x_pos, y_pos), (1 - x_pos, y_pos), (x_pos, 1 - y_pos), (1 - x_pos, 1 - y_pos)]
    order = jnp.stack([2 * px + py for px, py in chip_order]).astype(jnp.int32)
    n_chips = len(chip_order)

    def body(order_ref, x_ref, g_ref, w_ref, *rest):
        shard_hbm, proj_ref, w_hbm = rest[:ns], rest[ns], rest[ns + 1]
        gathered = rest[ns + 2 : 2 * ns + 2]
        h_all, land, w_send, w_recv, out_sem, send_sems, recv_sems, own_sems = rest[2 * ns + 2 :]
        j, i = pl.program_id(0), pl.program_id(1)
        x, y, c = _place()
        me, sibling = (x, y, c), (x, y, 1 - c)
        chips = [(1 - x, y), (x, 1 - y), (1 - x, 1 - y)]
        own = [pltpu.make_async_copy(shard_hbm[a], gathered[a].at[slot(*me)], own_sems.at[a]) for a in range(ns)]
        start, finish = _exchange_steps(ns, lambda a, s: shard_hbm[a], gathered, send_sems, recv_sems)

        def copy(k, block, to):
            ref = land.at[slot(*block)]
            return pltpu.make_async_remote_copy(
                src_ref=ref, dst_ref=ref, send_sem=w_send.at[k], recv_sem=w_recv.at[k], device_id=to, device_id_type=MESH)

        first_sends = [copy(0, me, sibling)] + [copy(1 + k, me, (*chip, c)) for k, chip in enumerate(chips)]
        forwards = [copy(4 + k, (*chip, c), sibling) for k, chip in enumerate(chips)]

        @pl.when((j == 0) & (i == 0))
        def _():
            land[slot(*me)] = w_ref[...].astype(BF16)
            for cp in first_sends:
                cp.start()
            start()
            for cp in own:
                cp.start()
            copy(0, sibling, me).wait_recv()

        for k, chip in enumerate(chips):

            @pl.when((j == k) & (i == n_t - 1))
            def _(k=k, chip=chip):
                copy(1 + k, (*chip, c), me).wait_recv()
                forwards[k].start()

            @pl.when((j == k + 1) & (i == 0))
            def _(k=k, chip=chip):
                copy(4 + k, (*chip, 1 - c), me).wait_recv()

        rows = pl.ds(pl.multiple_of(i * tm, tm), tm)

        @pl.when(j == 0)
        def _():
            x_t = x_ref[...]
            h_all[rows, :] = (x_t * _rms(x_t) * g_ref[...]).astype(BF16)

        h = h_all[rows, :]
        for half in range(2):
            proj_ref[:, half * nb : (half + 1) * nb] = _mm(h, land[2 * order_ref[j] + half])

        @pl.when((j == n_chips - 1) & (i == n_t - 1))
        def _():
            for cp in first_sends + forwards:
                cp.wait_send()
            finish()
            for cp in own:
                cp.wait()
            outs = [pltpu.make_async_copy(land.at[blk], w_hbm.at[:, pl.ds(blk * nb, nb)], out_sem.at[blk])
                    for blk in range(N_DEV)]
            for cp in outs:
                cp.start()
            for cp in outs:
                cp.wait()

    hbm = pl.BlockSpec(memory_space=pl.ANY)
    res = pl.pallas_call(
        body,
        name="in_proj",
        grid_spec=pltpu.PrefetchScalarGridSpec(
            num_scalar_prefetch=1,
            grid=(n_chips, n_t),
            in_specs=[
                pl.BlockSpec((tm, d), lambda j, i, order: (jnp.where(j == 0, i, n_t - 1), 0)),
                pl.BlockSpec((1, d), lambda j, i, order: (0, 0)),
                pl.BlockSpec((d, nb), lambda j, i, order: (0, 0)),
            ]
            + [hbm] * ns,
            out_specs=[pl.BlockSpec((tm, 2 * nb), lambda j, i, order: (i, order[j])), hbm] + [hbm] * ns,
            scratch_shapes=[
                pltpu.VMEM((tokens, d), BF16),
                pltpu.VMEM((N_DEV, d, nb), BF16),
                pltpu.SemaphoreType.DMA((7,)),
                pltpu.SemaphoreType.DMA((7,)),
                pltpu.SemaphoreType.DMA((N_DEV,)),
                pltpu.SemaphoreType.DMA((ns, 7)),
                pltpu.SemaphoreType.DMA((ns, 7)),
                pltpu.SemaphoreType.DMA((ns,)),
            ],
        ),
        out_shape=[
            jax.ShapeDtypeStruct((tokens, N_DEV * nb), F32),
            jax.ShapeDtypeStruct((d, N_DEV * nb), BF16),
        ]
        + [jax.ShapeDtypeStruct((N_DEV, *a.shape), a.dtype) for a in shards],
        compiler_params=_params("arbitrary", "arbitrary"),
    )(order, x2, g_pre, w_in_blk, *shards)
    return res[0], res[1], res[2:]


def _in_proj_bwd(x2, dres, du_pool, du_ssm, dq, dgate, g_pre, w_in_f):
    tokens, d = x2.shape
    nb = w_in_f.shape[1] // N_DEV
    pool_w, ssm_w, att_w, mix = du_pool.shape[1], du_ssm.shape[1], dq.shape[1], dgate.shape[1]
    cl = _token_tile(tokens, TOKEN_ROWS)
    n_tiles = tokens // cl

    def body(x_ref, dres_ref, dup_ref, dus_ref, dq_ref, dgate_ref, g_ref, w_ref, gx_ref, gw_hbm, gg_ref, acc, sem):
        i = pl.program_id(0)

        @pl.when(i == 0)
        def _():
            acc[...] = jnp.zeros_like(acc)
            gg_ref[...] = jnp.zeros_like(gg_ref)

        x = x_ref[...]
        r = _rms(x)
        xn = x * r
        g = g_ref[...]
        h = (xn * g).astype(BF16)
        dproj = jnp.concatenate([dup_ref[...], dus_ref[...], dq_ref[...], dgate_ref[...]], axis=1).astype(BF16)
        dh = _mm_nt(dproj, w_ref[...])
        for j in range(N_DEV):
            acc[j] += _mm_tn(h, dproj[:, j * nb : (j + 1) * nb])
        gg_ref[...] += jnp.sum(dh * xn, axis=0, keepdims=True)
        dxn = dh * g
        gx_ref[...] = dres_ref[...] + r * (dxn - xn * jnp.mean(dxn * xn, axis=-1, keepdims=True))

        @pl.when(i == n_tiles - 1)
        def _():
            cp = pltpu.make_async_copy(acc, gw_hbm, sem)
            cp.start()
            cp.wait()

    return pl.pallas_call(
        body,
        name="in_proj_bwd",
        grid=(n_tiles,),
        in_specs=[
            pl.BlockSpec((cl, d), lambda i: (i, 0)),
            pl.BlockSpec((cl, d), lambda i: (i, 0)),
            pl.BlockSpec((cl, pool_w), lambda i: (i, 0)),
            pl.BlockSpec((cl, ssm_w), lambda i: (i, 0)),
            pl.BlockSpec((cl, att_w), lambda i: (i, 0)),
            pl.BlockSpec((cl, mix), lambda i: (i, 0)),
            pl.BlockSpec((1, d), lambda i: (0, 0)),
            pl.BlockSpec(w_in_f.shape, lambda i: (0, 0)),
        ],
        out_specs=[
            pl.BlockSpec((cl, d), lambda i: (i, 0)),
            pl.BlockSpec(memory_space=pl.ANY),
            pl.BlockSpec((1, d), lambda i: (0, 0)),
        ],
        out_shape=[
            jax.ShapeDtypeStruct((tokens, d), F32),
            jax.ShapeDtypeStruct((N_DEV, d, nb), F32),
            jax.ShapeDtypeStruct((1, d), F32),
        ],
        scratch_shapes=[pltpu.VMEM((N_DEV, d, nb), F32), pltpu.SemaphoreType.DMA],
        compiler_params=_params("arbitrary"),
    )(x2, dres, du_pool, du_ssm, dq, dgate, g_pre, w_in_f)


def _pool_geometry(seq, width):
    gw = width // len(POOL_WINDOWS)
    col = lax.broadcasted_iota(jnp.int32, (1, width), 1)
    win = jnp.full((1, width), float(POOL_WINDOWS[-1]), F32)
    for gi in range(len(POOL_WINDOWS) - 2, -1, -1):
        win = jnp.where(col < (gi + 1) * gw, float(POOL_WINDOWS[gi]), win)
    row = lax.broadcasted_iota(jnp.int32, (seq, width), 0)
    filling = 1.0 / (lax.broadcasted_iota(jnp.int32, (seq, 1), 0) + 1).astype(F32)
    inv_cnt = jnp.where(row + 1 < win.astype(jnp.int32), filling, 1.0 / win)
    return win, row, inv_cnt


def _window_sums(a, win, seq, back):
    pad = 2 * POOL_WINDOWS[-1]
    zeros = jnp.zeros((pad, a.shape[1]), F32)
    s = jnp.concatenate([a, zeros] if back else [zeros, a], axis=0)
    sums = []
    k = 1
    while k < POOL_WINDOWS[-1]:
        s = s + pltpu.roll(s, seq + pad - k if back else k, 0)
        k *= 2
        sums.append((k, s))
    out = sums[-1][1]
    for k, s in reversed(sums[:-1]):
        out = jnp.where(win <= float(k), s, out)
    return out[0:seq] if back else out[pad : pad + seq]


def _pool_fwd(u2, wp_blk, scale, batch, seq):
    width = scale.shape[1]

    def body(u_ref, w_ref, s_ref, y_ref, diff_ref):
        u = u_ref[...]
        win, row, inv_cnt = _pool_geometry(seq, width)
        diff = (_window_sums(u, win, seq, False) * inv_cnt - u).astype(BF16)
        diff_ref[...] = diff
        y_ref[...] = _mm(diff, w_ref[...]) * s_ref[...]

    return pl.pallas_call(
        body,
        name="pool_fwd",
        grid=(batch,),
        in_specs=[
            pl.BlockSpec((seq, width), lambda b: (b, 0)),
            pl.BlockSpec((width, width), lambda b: (0, 0)),
            pl.BlockSpec((1, width), lambda b: (0, 0)),
        ],
        out_specs=[pl.BlockSpec((seq, width), lambda b: (b, 0)), pl.BlockSpec((seq, width), lambda b: (b, 0))],
        out_shape=[
            jax.ShapeDtypeStruct((u2.shape[0], width), F32),
            jax.ShapeDtypeStruct((u2.shape[0], width), BF16),
        ],
        compiler_params=_params("arbitrary"),
    )(u2, wp_blk, scale)


def _pool_bwd(diff2, dy2, wp_blk, scale, batch, seq):
    width = scale.shape[1]

    def body(diff_ref, dy_ref, w_ref, s_ref, du_ref, gw_ref, gs_ref):
        @pl.when(pl.program_id(0) == 0)
        def _():
            gw_ref[...] = jnp.zeros_like(gw_ref)
            gs_ref[...] = jnp.zeros_like(gs_ref)

        diff = diff_ref[...]
        dy = dy_ref[...]
        win, row, inv_cnt = _pool_geometry(seq, width)
        gs_ref[...] += jnp.sum(dy * _mm(diff, w_ref[...]), axis=0, keepdims=True)
        dys = (dy * s_ref[...]).astype(BF16)
        gw_ref[...] += _mm_tn(diff, dys)
        dd = _mm_nt(dys, w_ref[...])
        du_ref[...] = _window_sums(dd * inv_cnt, win, seq, True) - dd

    return pl.pallas_call(
        body,
        name="pool_bwd",
        grid=(batch,),
        in_specs=[
            pl.BlockSpec((seq, width), lambda b: (b, 0)),
            pl.BlockSpec((seq, width), lambda b: (b, 0)),
            pl.BlockSpec((width, width), lambda b: (0, 0)),
            pl.BlockSpec((1, width), lambda b: (0, 0)),
        ],
        out_specs=[
            pl.BlockSpec((seq, width), lambda b: (b, 0)),
            pl.BlockSpec((width, width), lambda b: (0, 0)),
            pl.BlockSpec((1, width), lambda b: (0, 0)),
        ],
        out_shape=[
            jax.ShapeDtypeStruct(dy2.shape, F32),
            jax.ShapeDtypeStruct((width, width), F32),
            jax.ShapeDtypeStruct((1, width), F32),
        ],
        compiler_params=_params("arbitrary"),
    )(diff2, dy2, wp_blk, scale)


def _state_row(z, n_blocks):
    re = jnp.real(z).reshape(n_blocks, -1)
    im = jnp.imag(z).reshape(n_blocks, -1)
    return jnp.concatenate([re, im], axis=1).reshape(1, -1)


def _ssm_tables(a_re, a_im, log_dt, b_re, b_im, c_re, c_im):
    groups, n_state = a_re.shape
    ch = b_re.shape[2]
    nb = groups * ch // LANES
    gl = groups // nb
    lam = lax.complex(a_re, a_im)
    lam_bar = jnp.exp(lam * jnp.exp(log_dt)[:, None])
    b_bar = ((lam_bar - 1.0) / lam)[..., None] * lax.complex(b_re, b_im)
    eye = jnp.eye(gl, dtype=F32)

    def rows_to_state(t):
        return jnp.einsum("sgnc,gh->sgchn", t.reshape(nb, gl, n_state, ch), eye).reshape(nb, gl * ch, gl * n_state)

    def state_to_rows(t):
        return jnp.einsum("sgcn,gh->shngc", t.reshape(nb, gl, ch, n_state), eye).reshape(nb, gl * n_state, gl * ch)

    b_tab = jnp.concatenate([rows_to_state(jnp.real(b_bar)), rows_to_state(jnp.imag(b_bar))], axis=2)
    c_tab = jnp.concatenate([state_to_rows(c_re), -state_to_rows(c_im)], axis=1)
    return _state_row(lam_bar, nb), b_tab, c_tab


def _lam_power(a_re, a_im, log_dt, power, scale, n_blocks):
    return _state_row(scale * jnp.exp(lax.complex(a_re, a_im) * jnp.exp(log_dt)[:, None] * power), n_blocks)


def _state_blocks(s2, n_blocks, width):
    half = s2 // n_blocks // 2
    assert half % width == 0
    return [(b * 2 * half + o, b * 2 * half + half + o) for b in range(n_blocks) for o in range(0, half, width)]


def _scan(src_ref, dst_ref, st_ref, lam8_ref, n_groups, s, n_blocks, reverse, store):
    lb = SCAN_LANES
    for re0, im0 in _state_blocks(2 * s, n_blocks, lb):
        cr, ci = pl.ds(re0, lb), pl.ds(im0, lb)
        lr = lam8_ref[:, cr]
        li = -lam8_ref[:, ci] if reverse else lam8_ref[:, ci]

        unroll = 1 if store else SCAN_UNROLL

        def step(i, carry, cr=cr, ci=ci, lr=lr, li=li):
            hr, hi = carry
            for k in range(unroll):
                grp = i * unroll + k
                grp = n_groups - 1 - grp if reverse else grp
                rows = pl.ds(pl.multiple_of(grp * SUBLANES, SUBLANES), SUBLANES)
                hr, hi = (lr * hr - li * hi + src_ref[rows, cr], lr * hi + li * hr + src_ref[rows, ci])
                if store:
                    dst_ref[rows, cr] = hr
                    dst_ref[rows, ci] = hi
            return hr, hi

        assert n_groups % unroll == 0
        hr, hi = lax.fori_loop(0, n_groups // unroll, step, (st_ref[:, cr], st_ref[:, ci]))
        st_ref[:, cr] = hr
        st_ref[:, ci] = hi


def _pack_state(re, im):
    hi = lax.bitcast_convert_type(re.astype(BF16).astype(F32), jnp.uint32)
    lo = lax.bitcast_convert_type(im.astype(BF16).astype(F32), jnp.uint32)
    return hi | (lo >> 16)


def _unpack_state(word):
    re = lax.bitcast_convert_type(word & jnp.uint32(0xFFFF0000), F32)
    im = lax.bitcast_convert_type(word << 16, F32)
    return re, im


def _scan_adjoint(dh_ref, hprev_ref, group0, stg_ref, acc_ref, lam8_ref, n_groups, s, n_blocks):
    lb = SCAN_LANES
    half = s // n_blocks
    for re0, im0 in _state_blocks(2 * s, n_blocks, lb):
        cr, ci = pl.ds(re0, lb), pl.ds(im0, lb)
        ch = pl.ds(re0 // (2 * half) * half + re0 % (2 * half), lb)
        lr, li = lam8_ref[:, cr], -lam8_ref[:, ci]

        def step(i, carry, cr=cr, ci=ci, ch=ch, lr=lr, li=li):
            gr, gi, ar, ai = carry
            grp = n_groups - 1 - i
            rows = pl.ds(pl.multiple_of(grp * SUBLANES, SUBLANES), SUBLANES)
            ngr = lr * gr - li * gi + dh_ref[rows, cr]
            ngi = lr * gi + li * gr + dh_ref[rows, ci]
            hr, hi = _unpack_state(hprev_ref[pl.ds(pl.multiple_of((group0 + grp) * SUBLANES, SUBLANES), SUBLANES), ch])
            ar = ar + hr * ngr + hi * ngi
            ai = ai + hr * ngi - hi * ngr
            dh_ref[rows, cr] = ngr
            dh_ref[rows, ci] = ngi
            return ngr, ngi, ar, ai

        init = (stg_ref[:, cr], stg_ref[:, ci], acc_ref[:, cr], acc_ref[:, ci])
        gr, gi, ar, ai = lax.fori_loop(0, n_groups, step, init)
        stg_ref[:, cr] = gr
        stg_ref[:, ci] = gi
        acc_ref[:, cr] = ar
        acc_ref[:, ci] = ai


def _chunk_starts(st_ref, init_ref, lcl_ref, s, n_blocks):
    w = s // n_blocks
    init_ref[0:1, :] = jnp.zeros((1, 2 * s), F32)
    for re0, im0 in _state_blocks(2 * s, n_blocks, w):
        re, im = pl.ds(re0, w), pl.ds(im0, w)
        ar, ai = lcl_ref[:, re], lcl_ref[:, im]
        cr = jnp.zeros((1, w), F32)
        ci = jnp.zeros((1, w), F32)
        for k in range(1, NCH):
            cr, ci = (ar * cr - ai * ci + st_ref[k - 1 : k, re], ar * ci + ai * cr + st_ref[k - 1 : k, im])
            init_ref[k : k + 1, re] = cr
            init_ref[k : k + 1, im] = ci


def _chunk_starts_adjoint(stg_ref, initg_ref, lcl_ref, s, n_blocks):
    w = s // n_blocks
    initg_ref[NCH - 1 : NCH, :] = jnp.zeros((1, 2 * s), F32)
    for re0, im0 in _state_blocks(2 * s, n_blocks, w):
        re, im = pl.ds(re0, w), pl.ds(im0, w)
        ar, ai = lcl_ref[:, re], -lcl_ref[:, im]
        gr = jnp.zeros((1, w), F32)
        gi = jnp.zeros((1, w), F32)
        for k in range(NCH - 2, -1, -1):
            gr, gi = (stg_ref[k + 1 : k + 2, re] + ar * gr - ai * gi, stg_ref[k + 1 : k + 2, im] + ar * gi + ai * gr)
            initg_ref[k : k + 1, re] = gr
            initg_ref[k : k + 1, im] = gi


def _ssm_rows(seq, want):
    rows = min(want, seq // 2)
    assert seq % rows == 0 and rows % SUBLANES == 0
    return rows


def _chunk_copies(hbm_ref, b, cm_ref, sems, to_cm, col0=0):
    cl, _, width = cm_ref.shape
    copies = []
    for k in range(NCH):
        nat, cm = hbm_ref.at[b, pl.ds(k * cl, cl), pl.ds(col0, width)], cm_ref.at[:, k, :]
        src, dst = (nat, cm) if to_cm else (cm, nat)
        copies.append(pltpu.make_async_copy(src, dst, sems.at[k]))
    return copies


def _blockwise(fn, n_blocks):
    return jnp.concatenate([fn(b) for b in range(n_blocks)], axis=1)


def _ssm_fwd(u, u_col, b_tab, c_tab, lam8, lcl, d_skip, w_glu, shards):
    batch, seq, _ = u.shape
    ns = len(shards)
    width = d_skip.shape[1]
    s = lam8.shape[1] // 2
    nb = b_tab.shape[0]
    sb = 2 * s // nb
    cl = seq // NCH
    rows = _ssm_rows(seq, SSM_ROWS)
    n_tiles = seq // rows
    n_groups = rows // SUBLANES

    def body(u_hbm, b_ref, c_ref, lam_ref, lcl_ref, d_ref, wg_ref, *rest):
        shard_hbm, (y_hbm, pre_ref, z_ref, init_ref) = rest[:ns], rest[ns : ns + 4]
        gathered = rest[ns + 4 : 2 * ns + 4]
        u_cm, y_cm, bu_all, st, sems, send_sems, recv_sems, own_sems = rest[2 * ns + 4 :]
        b, ph, t = pl.program_id(0), pl.program_id(1), pl.program_id(2)
        tile_groups = pl.ds(pl.multiple_of(t * n_groups, n_groups), n_groups)
        x_pos, y_pos, c_pos = _place()
        own = [pltpu.make_async_copy(shard_hbm[a], gathered[a].at[4 * x_pos + 2 * y_pos + c_pos], own_sems.at[a])
               for a in range(ns)]
        exchange_start, exchange_finish = _exchange_steps(
            ns, lambda a, slot: shard_hbm[a], gathered, send_sems, recv_sems)

        @pl.when((b == 0) & (ph == 0) & (t == 0))
        def _():
            exchange_start()
            for cp in own:
                cp.start()

        @pl.when((b == batch - 1) & (ph == 1) & (t == n_tiles - 1))
        def _():
            exchange_finish()
            for cp in own:
                cp.wait()

        @pl.when((ph == 0) & (t == 0))
        def _():
            loads = _chunk_copies(u_hbm, b, u_cm, sems, True, u_col)
            for cp in loads:
                cp.start()
            st[...] = jnp.zeros_like(st)
            for cp in loads:
                cp.wait()

        @pl.when((ph == 1) & (t == 0))
        def _():
            st[...] = init_ref[...]

        u_t = u_cm[tile_groups].reshape(rows, width)
        bu = bu_all.at[pl.ds(pl.multiple_of(t * rows, rows), rows)]

        @pl.when(ph == 0)
        def _():
            u_b = u_t.astype(BF16)
            for blk in range(nb):
                bu[:, blk * sb : (blk + 1) * sb] = _mm(u_b[:, blk * LANES : (blk + 1) * LANES], b_ref[blk])
            _scan(bu, bu, st, lam_ref, n_groups, s, nb, False, False)

        @pl.when((ph == 0) & (t == n_tiles - 1))
        def _():
            _chunk_starts(st, init_ref, lcl_ref, s, nb)

        @pl.when(ph == 1)
        def _():
            _scan(bu, bu, st, lam_ref, n_groups, s, nb, False, True)
            hs = lambda blk: _mm(bu[:, blk * sb : (blk + 1) * sb].astype(BF16), c_ref[blk])
            pre = _blockwise(hs, nb) + d_ref[...] * u_t
            z = _mm(_gelu(pre).astype(BF16), wg_ref[...])
            pre_ref[...] = pre
            z_ref[...] = z
            y = z[:, 0:width] * jax.nn.sigmoid(z[:, width : 2 * width])
            y_cm[tile_groups] = y.reshape(n_groups, SUBLANES, width)

        @pl.when((ph == 1) & (t == n_tiles - 1))
        def _():
            stores = _chunk_copies(y_hbm, b, y_cm, sems, False)
            for cp in stores:
                cp.start()
            for cp in stores:
                cp.wait()

    out_tile = lambda b, ph, t: (b, t * ph, 0)
    full = lambda a: pl.BlockSpec(a.shape, lambda b, ph, t: (0,) * a.ndim)
    hbm = pl.BlockSpec(memory_space=pl.ANY)
    res = pl.pallas_call(
        body,
        name="ssm_fwd",
        grid=(batch, 2, n_tiles),
        in_specs=[hbm, full(b_tab), full(c_tab), full(lam8), full(lcl), full(d_skip), full(w_glu)] + [hbm] * ns,
        out_specs=[
            hbm,
            pl.BlockSpec((None, rows, width), out_tile),
            pl.BlockSpec((None, rows, 2 * width), out_tile),
            pl.BlockSpec((None, SUBLANES, 2 * s), lambda b, ph, t: (b, 0, 0)),
        ]
        + [hbm] * ns,
        out_shape=[
            jax.ShapeDtypeStruct((batch, seq, width), F32),
            jax.ShapeDtypeStruct((batch, seq, width), F32),
            jax.ShapeDtypeStruct((batch, seq, 2 * width), F32),
            jax.ShapeDtypeStruct((batch, SUBLANES, 2 * s), F32),
        ]
        + [jax.ShapeDtypeStruct((N_DEV, *a.shape), a.dtype) for a in shards],
        scratch_shapes=[
            pltpu.VMEM((cl, NCH, width), F32),
            pltpu.VMEM((cl, NCH, width), F32),
            pltpu.VMEM((seq, 2 * s), F32),
            pltpu.VMEM((SUBLANES, 2 * s), F32),
            pltpu.SemaphoreType.DMA((NCH,)),
            pltpu.SemaphoreType.DMA((ns, 7)),
            pltpu.SemaphoreType.DMA((ns, 7)),
            pltpu.SemaphoreType.DMA((ns,)),
        ],
        compiler_params=_params("arbitrary", "arbitrary", "arbitrary"),
    )(u, b_tab, c_tab, lam8, lcl, d_skip, w_glu, *shards)
    return res[:4], res[4:]


def _ssm_bwd(u, u_col, pre_p, z_p, dy, init, b_tab, b_tab_t, c_tab_t, lam8, lcl, d_skip, w_glu, ready):
    batch, seq, _ = u.shape
    width = d_skip.shape[1]
    nr = len(ready)
    s = lam8.shape[1] // 2
    nb = b_tab.shape[0]
    sb = 2 * s // nb
    cl = seq // NCH
    rows = _ssm_rows(seq, SSM_ROWS)
    n_tiles = seq // rows
    n_groups = rows // SUBLANES

    def body(u_hbm, pre_ref, z_ref, dy_hbm, init_ref, b_ref, bt_ref, ct_ref, lam_ref, lcl_ref, d_ref, wg_ref, *rest):
        ready_hbm, rest = rest[:nr], rest[nr:]
        du_hbm, gb_ref, gc_ref, gwg_ref, gd_ref, glam_ref = rest[:6]
        landed_hbm, rest = rest[6 : 6 + nr], rest[6 + nr :]
        u_cm, dy_cm, work, hs_all, dpre_all, st, stg, initg, acc, sems, send_sems, recv_sems = rest
        b, ph, t = pl.program_id(0), pl.program_id(1), pl.program_id(2)
        half = s // nb
        exchange_start, exchange_finish = _exchange_steps(
            nr, lambda a, slot: ready_hbm[a].at[slot], landed_hbm, send_sems, recv_sems)
        first = (b == 0) & (ph == 0) & (t == 0)
        last = (b == batch - 1) & (ph == 2) & (t == n_tiles - 1)
        tile = jnp.where(ph == 0, t, n_tiles - 1 - t)
        tile_rows = pl.ds(pl.multiple_of(tile * rows, rows), rows)
        tile_groups = pl.ds(pl.multiple_of(tile * n_groups, n_groups), n_groups)
        lanes = lambda blk: slice(blk * LANES, (blk + 1) * LANES)
        states = lambda blk: slice(blk * sb, (blk + 1) * sb)

        @pl.when(first)
        def _():
            exchange_start()
            acc[...] = jnp.zeros_like(acc)
            gb_ref[...] = jnp.zeros_like(gb_ref)
            gc_ref[...] = jnp.zeros_like(gc_ref)
            gwg_ref[...] = jnp.zeros_like(gwg_ref)
            gd_ref[...] = jnp.zeros_like(gd_ref)

        @pl.when((ph == 0) & (t == 0))
        def _():
            loads = (_chunk_copies(u_hbm, b, u_cm, sems.at[0], True, u_col)
                     + _chunk_copies(dy_hbm, b, dy_cm, sems.at[1], True))
            for cp in loads:
                cp.start()
            st[...] = init_ref[...]
            for blk in range(nb):
                entry = init_ref[:, states(blk)]
                hs_all[0:SUBLANES, blk * half : (blk + 1) * half] = _pack_state(entry[:, 0:half], entry[:, half : 2 * half])
            for cp in loads:
                cp.wait()

        u_t = u_cm[tile_groups].reshape(rows, width)
        u_b = u_t.astype(BF16)

        @pl.when(ph == 0)
        def _():
            for blk in range(nb):
                work[:, states(blk)] = _mm(u_b[:, lanes(blk)], b_ref[blk])
            _scan(work, work, st, lam_ref, n_groups, s, nb, False, True)
            z = z_ref[...]
            dy_t = dy_cm[tile_groups].reshape(rows, width)
            pre = pre_ref[...]
            z1, sig = z[:, 0:width], jax.nn.sigmoid(z[:, width : 2 * width])
            dz = jnp.concatenate([dy_t * sig, dy_t * z1 * sig * (1.0 - sig)], axis=1).astype(BF16)
            gwg_ref[...] += _mm_tn(_gelu(pre).astype(BF16), dz)
            dpre = _mm_nt(dz, wg_ref[...]) * _gelu_grad(pre)
            dpre_all[tile_rows, :] = dpre
            gd_ref[...] += jnp.sum(dpre * u_t, axis=0, keepdims=True)
            dpre_b = dpre.astype(BF16)
            kept = pl.ds(pl.multiple_of(tile * rows + SUBLANES, SUBLANES), rows)
            for blk in range(nb):
                hs = work[:, states(blk)]
                gc_ref[blk] += _mm_tn(dpre_b[:, lanes(blk)], hs.astype(BF16))
                hs_all[kept, blk * half : (blk + 1) * half] = _pack_state(hs[:, 0:half], hs[:, half : 2 * half])

        @pl.when(ph >= 1)
        def _():
            dpre_b = dpre_all[tile_rows, :].astype(BF16)
            for blk in range(nb):
                work[:, states(blk)] = _mm(dpre_b[:, lanes(blk)], ct_ref[blk])

        @pl.when(ph == 1)
        def _():
            @pl.when(t == 0)
            def _():
                stg[...] = jnp.zeros_like(stg)

            _scan(work, work, stg, lam_ref, n_groups, s, nb, True, False)

            @pl.when(t == n_tiles - 1)
            def _():
                _chunk_starts_adjoint(stg, initg, lcl_ref, s, nb)

        @pl.when(ph == 2)
        def _():
            @pl.when(t == 0)
            def _():
                stg[...] = initg[...]

            _scan_adjoint(work, hs_all, tile * n_groups, stg, acc, lam_ref, n_groups, s, nb)
            du = lambda blk: _mm(work[:, states(blk)].astype(BF16), bt_ref[blk])
            du_t = _blockwise(du, nb) + dpre_all[tile_rows, :] * d_ref[...]
            dy_cm[tile_groups] = du_t.reshape(n_groups, SUBLANES, width)
            for blk in range(nb):
                gb_ref[blk] += _mm_tn(u_b[:, lanes(blk)], work[:, states(blk)].astype(BF16))

            @pl.when(t == n_tiles - 1)
            def _():
                stores = _chunk_copies(du_hbm, b, dy_cm, sems.at[0], False)
                for cp in stores:
                    cp.start()
                for cp in stores:
                    cp.wait()

        @pl.when(last)
        def _():
            glam_ref[...] = jnp.sum(acc[...], axis=0, keepdims=True)
            exchange_finish()

    def tile(b, ph, t):
        return (b, jnp.where(ph == 0, t, n_tiles - 1 - t), 0)

    full = lambda a: pl.BlockSpec(a.shape, lambda b, ph, t: (0,) * a.ndim)
    hbm = pl.BlockSpec(memory_space=pl.ANY)
    res = pl.pallas_call(
        body,
        name="ssm_bwd",
        grid=(batch, 3, n_tiles),
        in_specs=[
            hbm,
            pl.BlockSpec((None, rows, width), tile),
            pl.BlockSpec((None, rows, 2 * width), tile),
            hbm,
            pl.BlockSpec((None, SUBLANES, 2 * s), lambda b, ph, t: (b, 0, 0)),
            full(b_tab), full(b_tab_t), full(c_tab_t), full(lam8), full(lcl), full(d_skip), full(w_glu),
        ]
        + [hbm] * nr,
        out_specs=[
            hbm,
            full(b_tab), full(b_tab), full(w_glu), full(d_skip),
            pl.BlockSpec((1, 2 * s), lambda b, ph, t: (0, 0)),
        ]
        + [hbm] * nr,
        out_shape=[
            jax.ShapeDtypeStruct((batch, seq, width), F32),
            jax.ShapeDtypeStruct(b_tab.shape, F32),
            jax.ShapeDtypeStruct(b_tab.shape, F32),
            jax.ShapeDtypeStruct(w_glu.shape, F32),
            jax.ShapeDtypeStruct(d_skip.shape, F32),
            jax.ShapeDtypeStruct((1, 2 * s), F32),
        ]
        + [jax.ShapeDtypeStruct(a.shape, F32) for a in ready],
        scratch_shapes=[
            pltpu.VMEM((cl, NCH, width), F32),
            pltpu.VMEM((cl, NCH, width), F32),
            pltpu.VMEM((rows, 2 * s), F32),
            pltpu.VMEM((seq + SUBLANES, s), jnp.uint32),
            pltpu.VMEM((seq, width), F32),
        ]
        + [pltpu.VMEM((SUBLANES, 2 * s), F32)] * 4
        + [pltpu.SemaphoreType.DMA((2, NCH)), pltpu.SemaphoreType.DMA((nr, 7)), pltpu.SemaphoreType.DMA((nr, 7))],
        compiler_params=_params("arbitrary", "arbitrary", "arbitrary"),
    )(u, pre_p, z_p, dy, init, b_tab, b_tab_t, c_tab_t, lam8, lcl, d_skip, w_glu, *ready)
    return res[:6], res[6:]


def _kv_fwd(mem, g_mem, w_kv):
    batch, n_mem, d = mem.shape
    kvw = w_kv.shape[1]

    def body(mem_ref, g_ref, w_ref, kv_ref):
        m = mem_ref[...]
        kv_ref[...] = _mm((m * _rms(m) * g_ref[...]).astype(BF16), w_ref[...])

    return pl.pallas_call(
        body,
        name="kv_fwd",
        grid=(batch,),
        in_specs=[
            pl.BlockSpec((None, n_mem, d), lambda b: (b, 0, 0)),
            pl.BlockSpec((1, d), lambda b: (0, 0)),
            pl.BlockSpec((d, kvw), lambda b: (0, 0)),
        ],
        out_specs=pl.BlockSpec((None, n_mem, kvw), lambda b: (b, 0, 0)),
        out_shape=jax.ShapeDtypeStruct((batch, n_mem, kvw), F32),
        compiler_params=_params("arbitrary"),
    )(mem, g_mem, w_kv)


def _kv_bwd(mem, dkv, g_mem, w_kv):
    batch, n_mem, d = mem.shape
    kvw = w_kv.shape[1]

    def body(mem_ref, dkv_ref, g_ref, w_ref, gw_ref, gg_ref):
        @pl.when(pl.program_id(0) == 0)
        def _():
            gw_ref[...] = jnp.zeros_like(gw_ref)
            gg_ref[...] = jnp.zeros_like(gg_ref)

        m = mem_ref[...]
        mn = m * _rms(m)
        dkv_b = dkv_ref[...].astype(BF16)
        gw_ref[...] += _mm_tn((mn * g_ref[...]).astype(BF16), dkv_b)
        gg_ref[...] += jnp.sum(_mm_nt(dkv_b, w_ref[...]) * mn, axis=0, keepdims=True)

    return pl.pallas_call(
        body,
        name="kv_bwd",
        grid=(batch,),
        in_specs=[
            pl.BlockSpec((None, n_mem, d), lambda b: (b, 0, 0)),
            pl.BlockSpec((None, n_mem, kvw), lambda b: (b, 0, 0)),
            pl.BlockSpec((1, d), lambda b: (0, 0)),
            pl.BlockSpec((d, kvw), lambda b: (0, 0)),
        ],
        out_specs=[pl.BlockSpec((d, kvw), lambda b: (0, 0)), pl.BlockSpec((1, d), lambda b: (0, 0))],
        out_shape=[jax.ShapeDtypeStruct((d, kvw), F32), jax.ShapeDtypeStruct((1, d), F32)],
        compiler_params=_params("arbitrary"),
    )(mem, dkv, g_mem, w_kv)


def _tail(x2, target2, proj, y_pool, y_ssm, kv, w_out, g_post):
    tokens, d = x2.shape
    batch, n_mem, kvw = kv.shape
    pool_w, ssm_w, att_w, mix = y_pool.shape[1], y_ssm.shape[1], kvw // 2, w_out.shape[0]
    assert (mix - att_w) % att_w == 0 and proj.shape[1] == 2 * mix
    hd = att_w // MEM_HEADS
    cl = _token_tile(tokens // batch, TOKEN_ROWS)
    n_tiles = tokens // cl
    per_seq = tokens // batch // cl
    qk_scale = hd**-0.5

    def body(x_ref, tg_ref, gate_ref, yp_ref, ys_ref, q_ref, kv_ref, w_ref, g_ref,
             dres_ref, dgate_ref, dyp_ref, dys_ref, dq_ref, dkv_ref, gw_hbm, gg_ref, loss_ref, acc, sem):
        i = pl.program_id(0)

        @pl.when(i == 0)
        def _():
            acc[...] = jnp.zeros_like(acc)
            gg_ref[...] = jnp.zeros_like(gg_ref)
            loss_ref[...] = jnp.zeros_like(loss_ref)

        @pl.when(i % per_seq == 0)
        def _():
            dkv_ref[...] = jnp.zeros_like(dkv_ref)

        k = kv_ref[:, 0:att_w].astype(BF16)
        v = kv_ref[:, att_w : 2 * att_w].astype(BF16)
        lane = lax.broadcasted_iota(jnp.int32, (1, att_w), 1)
        heads = [(lane >= h * hd) & (lane < (h + 1) * hd) for h in range(MEM_HEADS)]
        g = g_ref[...]

        def part(rows):
            n_rows = rows.stop - rows.start
            q = q_ref[rows, :]
            probs, q_heads = [], []
            att = jnp.zeros((n_rows, att_w), F32)
            for mask in heads:
                qh = jnp.where(mask, q, 0.0).astype(BF16)
                sc = _mm_nt(qh, k) * qk_scale
                e = jnp.exp(sc - jnp.max(sc, axis=-1, keepdims=True))
                p = e * (1.0 / jnp.sum(e, axis=-1, keepdims=True))
                att = att + jnp.where(mask, _mm(p.astype(BF16), v), 0.0)
                probs.append(p)
                q_heads.append(qh)

            ycat = jnp.concatenate([yp_ref[rows, :], ys_ref[rows, :], att], axis=1)
            gate = gate_ref[rows, :]
            sig = jax.nn.sigmoid(gate)
            silu = gate * sig
            yg = (ycat * silu).astype(BF16)
            out = _mm(yg, w_ref[...])
            r = _rms(out)
            on = out * r
            err = x_ref[rows, :] + on * g - tg_ref[rows, :]
            loss_ref[...] += 0.5 * jnp.sum(jnp.mean(err * err, axis=-1, keepdims=True), axis=0, keepdims=True)
            dres = err * (1.0 / d)
            dres_ref[rows, :] = dres
            gg_ref[...] += jnp.sum(dres * on, axis=0, keepdims=True)
            don = dres * g
            dout = (r * (don - on * jnp.mean(don * on, axis=-1, keepdims=True))).astype(BF16)
            acc[...] += _mm_tn(yg, dout)
            dyg = _mm_nt(dout, w_ref[...])
            dgate_ref[rows, :] = dyg * ycat * (sig * (1.0 + gate * (1.0 - sig)))
            dycat = dyg * silu
            dyp_ref[rows, :] = dycat[:, 0:pool_w]
            dys_ref[rows, :] = dycat[:, pool_w : pool_w + ssm_w]
            datt = dycat[:, pool_w + ssm_w : mix]

            dq = jnp.zeros((n_rows, att_w), F32)
            dk = jnp.zeros((n_mem, att_w), F32)
            dv = jnp.zeros((n_mem, att_w), F32)
            for mask, p, qh in zip(heads, probs, q_heads):
                doh = jnp.where(mask, datt, 0.0).astype(BF16)
                dp = _mm_nt(doh, v)
                ds = (p * (dp - jnp.sum(p * dp, axis=-1, keepdims=True)) * qk_scale).astype(BF16)
                dq = dq + jnp.where(mask, _mm(ds, k), 0.0)
                dk = dk + _mm_tn(ds, qh)
                dv = dv + _mm_tn(p.astype(BF16), doh)
            dq_ref[rows, :] = dq
            dkv_ref[:, 0:att_w] += dk
            dkv_ref[:, att_w : 2 * att_w] += dv

        part(slice(0, cl))

        @pl.when(i == n_tiles - 1)
        def _():
            cp = pltpu.make_async_copy(acc, gw_hbm, sem)
            cp.start()
            cp.wait()

    tok = lambda w: pl.BlockSpec((cl, w), lambda i: (i, 0))
    chunked = tok(ssm_w)
    per_batch = pl.BlockSpec((None, n_mem, kvw), lambda i: (i // per_seq, 0, 0))
    return pl.pallas_call(
        body,
        name="tail",
        grid=(n_tiles,),
        in_specs=[
            tok(d), tok(d), pl.BlockSpec((cl, mix), lambda i: (i, 1)), tok(pool_w), chunked,
            pl.BlockSpec((cl, att_w), lambda i: (i, (mix - att_w) // att_w)), per_batch,
            pl.BlockSpec((mix, d), lambda i: (0, 0)),
            pl.BlockSpec((1, d), lambda i: (0, 0)),
        ],
        out_specs=[
            tok(d), tok(mix), tok(pool_w), chunked, tok(att_w), per_batch,
            pl.BlockSpec(memory_space=pl.ANY),
            pl.BlockSpec((1, d), lambda i: (0, 0)),
            pl.BlockSpec((1, 1), lambda i: (0, 0)),
        ],
        out_shape=[
            jax.ShapeDtypeStruct((tokens, d), F32),
            jax.ShapeDtypeStruct((tokens, mix), F32),
            jax.ShapeDtypeStruct((tokens, pool_w), F32),
            jax.ShapeDtypeStruct((tokens, ssm_w), F32),
            jax.ShapeDtypeStruct((tokens, att_w), F32),
            jax.ShapeDtypeStruct(kv.shape, F32),
            jax.ShapeDtypeStruct((mix, d), F32),
            jax.ShapeDtypeStruct((1, d), F32),
            jax.ShapeDtypeStruct((1, 1), F32),
        ],
        scratch_shapes=[pltpu.VMEM((mix, d), F32), pltpu.SemaphoreType.DMA],
        compiler_params=_params("arbitrary"),
    )(x2, target2, proj, y_pool, y_ssm, proj, kv, w_out, g_post)


def _pack(arrays):
    flat = jnp.concatenate([a.reshape(-1) for a in arrays])
    rows = -(-flat.size // (N_DEV * SUBLANES * LANES)) * N_DEV * SUBLANES
    return jnp.pad(flat, (0, rows * LANES - flat.size)).reshape(rows, LANES)


def _unpack(packed, like):
    flat, out, at = packed.reshape(-1), [], 0
    for a in like:
        out.append(flat[at : at + a.size].reshape(a.shape))
        at += a.size
    return out


def kernel(x, mem, g_pre, w_in, w_pool, pool_scale, a_re, a_im, log_dt, b_re, b_im, c_re, c_im, d_skip, w_glu, g_mem, w_kv, w_out, g_post, loss_target, m_g_pre, m_w_in, m_w_pool, m_pool_scale, m_a_re, m_a_im, m_log_dt, m_b_re, m_b_im, m_c_re, m_c_im, m_d_skip, m_w_glu, m_g_mem, m_w_kv, m_w_out, m_g_post, v_g_pre, v_w_in, v_w_pool, v_pool_scale, v_a_re, v_a_im, v_log_dt, v_b_re, v_b_im, v_c_re, v_c_im, v_d_skip, v_w_glu, v_g_mem, v_w_kv, v_w_out, v_g_post):
    batch, seq, d = x.shape
    cl = seq // NCH
    pool_w, ssm_w = pool_scale.shape[1], d_skip.shape[1]
    att_w = w_kv.shape[2] // 2
    tokens = batch * seq
    x2 = x.reshape(tokens, d)
    target2 = loss_target.reshape(tokens, d)

    wp_blk = jax.scipy.linalg.block_diag(*w_pool[0]).astype(BF16)
    ssm_params = (a_re[0], a_im[0], log_dt[0], b_re[0], b_im[0], c_re[0], c_im[0])
    (lam_row, b_tab, c_tab), tables_vjp = jax.vjp(_ssm_tables, *ssm_params)
    nb = b_tab.shape[0]
    lam8 = jnp.broadcast_to(lam_row, (SUBLANES, lam_row.shape[1]))
    lcl = _lam_power(a_re[0], a_im[0], log_dt[0], float(cl), 1.0, nb)
    b_bf, c_bf = b_tab.astype(BF16), c_tab.astype(BF16)

    proj, w_in_f, (w_glu_g,) = _in_proj(x2, g_pre, w_in[0], [w_glu[0].astype(BF16)])
    proj3 = proj.reshape(batch, seq, proj.shape[1])
    w_glu_f = w_glu_g.transpose(1, 0, 2).reshape(w_glu_g.shape[1], N_DEV * w_glu_g.shape[2])
    y_pool, diff_pool = _pool_fwd(proj, wp_blk, pool_scale, batch, seq)
    (y_ssm, pre_ssm, z_ssm, init_ssm), (w_out_g, w_kv_g) = _ssm_fwd(
        proj3, pool_w, b_bf, c_bf, lam8, lcl, d_skip, w_glu_f, [w_out[0].astype(BF16), w_kv[0].astype(BF16)])
    w_out_f = w_out_g.reshape(N_DEV * w_out_g.shape[1], w_out_g.shape[2])
    w_kv_f = w_kv_g.reshape(N_DEV * w_kv_g.shape[1], w_kv_g.shape[2])
    kv = _kv_fwd(mem, g_mem, w_kv_f)

    dres, dgate, dy_pool, dy_ssm, dq, dkv, gw_out, gg_post, loss_part = _tail(
        x2, target2, proj, y_pool, y_ssm.reshape(tokens, ssm_w), kv, w_out_f, g_post)

    gw_kv, gg_mem = _kv_bwd(mem, dkv, g_mem, w_kv_f)
    du_pool, gwp_dense, g_scale = _pool_bwd(diff_pool, dy_pool, wp_blk, pool_scale, batch, seq)
    gw_kv8 = gw_kv.reshape(N_DEV, -1, gw_kv.shape[1])
    gw_out8 = gw_out.reshape(N_DEV, -1, gw_out.shape[1])
    (du_ssm, gb_tab, gc_tab_t, gw_glu, gd_skip, glam), (kv_landed, out_landed) = _ssm_bwd(
        proj3, pool_w, pre_ssm, z_ssm, dy_ssm.reshape(batch, seq, ssm_w), init_ssm, b_bf, b_bf.transpose(0, 2, 1),
        c_bf.transpose(0, 2, 1), lam8, lcl, d_skip, w_glu_f, [gw_kv8, gw_out8])
    grad_x2, gw_in, gg_pre = _in_proj_bwd(
        x2, dres, du_pool, du_ssm.reshape(tokens, ssm_w), dq, dgate, g_pre,
        w_in_f)

    gw = pool_w // len(POOL_WINDOWS)
    gw_pool = jnp.stack([gwp_dense[i * gw : (i + 1) * gw, i * gw : (i + 1) * gw] for i in range(len(POOL_WINDOWS))])
    g_ssm = tables_vjp((glam, gb_tab, gc_tab_t.transpose(0, 2, 1)))

    small_w = [g_pre, w_pool, pool_scale, a_re, a_im, log_dt, b_re, b_im, c_re, c_im, d_skip, g_mem, g_post]
    small_m = [m_g_pre, m_w_pool, m_pool_scale, m_a_re, m_a_im, m_log_dt, m_b_re, m_b_im, m_c_re, m_c_im, m_d_skip, m_g_mem, m_g_post]
    small_v = [v_g_pre, v_w_pool, v_pool_scale, v_a_re, v_a_im, v_log_dt, v_b_re, v_b_im, v_c_re, v_c_im, v_d_skip, v_g_mem, v_g_post]
    small_g = [gg_pre, gw_pool, g_scale, *g_ssm, gd_skip, gg_mem, gg_post]
    big_g, small_sum = _reduce_all(
        [gw_in, gw_glu.reshape(ssm_w, N_DEV, -1).transpose(1, 0, 2)],
        _pack(small_g + [loss_part]),
        [(gw_kv8, kv_landed), (gw_out8, out_landed)])

    flat2 = lambda a: a.reshape(-1, a.shape[-1])
    sg = _unpack(small_sum, [flat2(a) for a in small_w] + [loss_part])
    loss = sg[-1].reshape(())
    small_names = ["g_pre", "w_pool", "pool_scale", "a_re", "a_im", "log_dt", "b_re", "b_im", "c_re", "c_im",
                   "d_skip", "g_mem", "g_post"]
    names = ["w_in", "w_glu", "w_kv", "w_out"] + small_names
    all_w = [w_in, w_glu, w_kv, w_out] + small_w
    all_m = [m_w_in, m_w_glu, m_w_kv, m_w_out] + small_m
    all_v = [v_w_in, v_w_glu, v_w_kv, v_w_out] + small_v
    updates = _adamw_all(big_g + sg[:-1], [flat2(a) for a in all_w], [flat2(a) for a in all_m], [flat2(a) for a in all_v])
    updates = {name: [t.reshape(a.shape) for t in u] for name, u, a in zip(names, updates, all_w)}

    order = ["g_pre", "w_in", "w_pool", "pool_scale", "a_re", "a_im", "log_dt", "b_re", "b_im", "c_re", "c_im",
             "d_skip", "w_glu", "g_mem", "w_kv", "w_out", "g_post"]
    outs = [[updates[name][kind] for name in order] for kind in range(4)]
    return (loss, grad_x2.reshape(batch, seq, d), *outs[0], *outs[1], *outs[2], *outs[3])
```

```python
import math

import jax
import jax.numpy as jnp
from jax import lax
from jax.experimental import pallas as pl
from jax.experimental.pallas import tpu as pltpu

F32 = jnp.float32
BF16 = jnp.bfloat16
MESH = pl.DeviceIdType.MESH

N_DEV = 8
SUBLANES = 8
LANES = 128
NCH = SUBLANES
VMEM_LIMIT = 60 * 1024 * 1024

IN_PROJ_ROWS = 2048
TOKEN_ROWS = 512
SSM_ROWS = 1024
SCAN_LANES = 512
SCAN_UNROLL = 8

EPS = 1e-6
POOL_WINDOWS = (2, 4, 8, 16)
MEM_HEADS = 4
ADAM_LR, ADAM_B1, ADAM_B2, ADAM_EPS, ADAM_WD, ADAM_STEP = 0.001, 0.9, 0.999, 1e-08, 0.01, 10


def _mm(a, b):
    return jnp.dot(a, b, preferred_element_type=F32)


def _mm_nt(a, b):
    return lax.dot_general(a, b, (((1,), (1,)), ((), ())), preferred_element_type=F32)


def _mm_tn(a, b):
    return lax.dot_general(a, b, (((0,), (0,)), ((), ())), preferred_element_type=F32)


def _params(*sem):
    return pltpu.CompilerParams(dimension_semantics=sem or None, vmem_limit_bytes=VMEM_LIMIT)


def _adamw(w, g, m, v):
    m = ADAM_B1 * m + (1.0 - ADAM_B1) * g
    v = ADAM_B2 * v + (1.0 - ADAM_B2) * (g * g)
    m_hat = m / (1.0 - ADAM_B1**ADAM_STEP)
    v_hat = v / (1.0 - ADAM_B2**ADAM_STEP)
    delta = -ADAM_LR * (m_hat / (jnp.sqrt(v_hat) + ADAM_EPS) + ADAM_WD * w)
    return delta, m, v


def _gelu(x):
    k = math.sqrt(2.0 / math.pi)
    return 0.5 * x * (1.0 + jnp.tanh(k * (x + 0.044715 * x * x * x)))


def _gelu_grad(x):
    k = math.sqrt(2.0 / math.pi)
    th = jnp.tanh(k * (x + 0.044715 * x * x * x))
    return 0.5 * (1.0 + th) + 0.5 * x * (1.0 - th * th) * (k * (1.0 + 3.0 * 0.044715 * x * x))


def _place():
    return lax.axis_index("x"), lax.axis_index("y"), lax.axis_index("c")


def _exchange_steps(n, src_of, landing, send_sems, recv_sems):
    x, y, c = _place()
    me = 4 * x + 2 * y + c
    peers = []
    for j in range(1, N_DEV):
        px = 1 - x if j & 4 else x
        py = 1 - y if j & 2 else y
        pc = 1 - c if j & 1 else c
        peers.append((px, py, pc))

    def copy(a, j, from_slot, to_slot, peer):
        return pltpu.make_async_remote_copy(
            src_ref=src_of(a, to_slot), dst_ref=landing[a].at[from_slot],
            send_sem=send_sems.at[a, j], recv_sem=recv_sems.at[a, j], device_id=peer, device_id_type=MESH)

    def start():
        for a in range(n):
            for j, p in enumerate(peers):
                copy(a, j, me, 4 * p[0] + 2 * p[1] + p[2], p).start()

    def finish():
        for a in range(n):
            for j, p in enumerate(peers):
                slot = 4 * p[0] + 2 * p[1] + p[2]
                copy(a, j, slot, slot, p).wait_recv()
        for a in range(n):
            for j, p in enumerate(peers):
                copy(a, j, me, 4 * p[0] + 2 * p[1] + p[2], p).wait_send()

    return start, finish


def _adamw_all(gs, ws, ms, vs):
    n = len(gs)

    def body(*refs):
        g, w, m, v = refs[:n], refs[n : 2 * n], refs[2 * n : 3 * n], refs[3 * n : 4 * n]
        outs = refs[4 * n :]
        for a in range(n):
            rows = g[a].shape[0]
            chunk = math.gcd(rows, 128)

            def step(i, _, a=a, chunk=chunk):
                r = pl.ds(pl.multiple_of(i * chunk, chunk), chunk)
                grad = g[a][r, :]
                delta, nm, nv = _adamw(w[a][r, :], grad, m[a][r, :], v[a][r, :])
                outs[4 * a][r, :] = grad
                outs[4 * a + 1][r, :] = delta
                outs[4 * a + 2][r, :] = nm
                outs[4 * a + 3][r, :] = nv
                return 0

            lax.fori_loop(0, rows // chunk, step, 0)

    vmem = pl.BlockSpec(memory_space=pltpu.VMEM)
    out_shape = []
    for wa in ws:
        out_shape += [jax.ShapeDtypeStruct(wa.shape, F32)] * 4
    res = pl.pallas_call(
        body,
        name="adamw_all",
        out_shape=out_shape,
        in_specs=[vmem] * (4 * n),
        out_specs=[vmem] * (4 * n),
        compiler_params=_params(),
    )(*gs, *ws, *ms, *vs)
    return [tuple(res[4 * a : 4 * a + 4]) for a in range(n)]


def _reduce_all(parts, small, early):
    n, ne = len(parts), len(early)
    parts4 = [p.reshape(4, 2, *p.shape[1:]) for p in parts]
    blks = [p.shape[1:] for p in parts]

    def body(*refs):
        refs = list(refs)
        take = lambda k: [refs.pop(0) for _ in range(k)]
        part = take(n)
        (small_ref,) = take(1)
        early_in = [take(2) for _ in range(ne)]
        outs = take(n)
        (small_all,) = take(1)
        early_out = take(ne)
        own, r1, got_a1, got_a2, got_b1, got_b2, pass_a, pass_b = (take(n) for _ in range(8))
        early_buf = take(ne)
        small_land, small_mine = take(2)
        s1_send, s1_recv, h_send, h_recv, loc, rs_send, rs_recv, ag_send, ag_recv, early_sems = refs
        x, y, c = _place()
        me = 4 * x + 2 * y + c
        piece = small_mine.shape[0]
        piece_of = lambda slot: pl.ds(pl.multiple_of(slot * piece, SUBLANES), piece)
        rs_start, rs_finish = _exchange_steps(
            1, lambda a, slot: small_ref.at[piece_of(slot)], [small_land], rs_send, rs_recv)
        ag_start, ag_finish = _exchange_steps(1, lambda a, slot: small_mine, [small_all], ag_send, ag_recv)
        landed = [pltpu.make_async_copy(early_in[e][1], early_buf[e], early_sems.at[e, 0]) for e in range(ne)]
        for cp in landed:
            cp.start()
        sibling = (x, y, 1 - c)
        chips = [(1 - x, y), (x, 1 - y), (1 - x, 1 - y)]

        def rowwise(rows, fn):
            chunk = math.gcd(rows, 128)

            def step(i, _):
                fn(pl.ds(pl.multiple_of(i * chunk, chunk), chunk))
                return 0

            lax.fori_loop(0, rows // chunk, step, 0)

        stage1, local = [], []
        for a in range(n):
            cp = pltpu.make_async_remote_copy(
                src_ref=part[a].at[:, 1 - c], dst_ref=r1[a], send_sem=s1_send.at[a], recv_sem=s1_recv.at[a],
                device_id=sibling, device_id_type=MESH)
            cp.start()
            stage1.append(cp)
            lc = pltpu.make_async_copy(part[a].at[:, c], own[a], loc.at[a])
            lc.start()
            local.append(lc)
        rs_start()
        x_nbr, y_nbr = (1 - x, y, c), (x, 1 - y, c)
        mine, mine_x, mine_y = 2 * x + y, 2 * (1 - x) + y, 2 * x + (1 - y)

        def hop(a, k, src, dst, to):
            return pltpu.make_async_remote_copy(
                src_ref=src, dst_ref=dst, send_sem=h_send.at[a, k], recv_sem=h_recv.at[a, k],
                device_id=to, device_id_type=MESH)

        first, second = [], []
        for a in range(n):
            half = blks[a][0] // 2
            up, low = pl.ds(0, half), pl.ds(half, half)
            local[a].wait()
            stage1[a].wait_recv()
            for chip in range(4):

                def add(r, a=a, chip=chip):
                    own[a][chip, r, :] = own[a][chip, r, :] + r1[a][chip, r, :]

                rowwise(blks[a][0], add)
            first.append([
                hop(a, 0, own[a].at[pl.ds(2 * (1 - x), 2), up], got_a1[a], x_nbr),
                hop(a, 2, own[a].at[2 * x + (1 - y), low], got_b1[a].at[x], y_nbr),
                hop(a, 3, own[a].at[2 * (1 - x) + (1 - y), low], got_b1[a].at[1 - x], y_nbr),
            ])
            for cp in first[a]:
                cp.start()
        rs_finish()
        small_land[me] = small_ref[piece_of(me), :]

        def sum_piece(i, _):
            r = pl.ds(pl.multiple_of(i * SUBLANES, SUBLANES), SUBLANES)
            total = small_land[0, r, :]
            for dev in range(1, N_DEV):
                total = total + small_land[dev, r, :]
            small_mine[r, :] = total
            small_all[me, r, :] = total
            return 0

        lax.fori_loop(0, piece // SUBLANES, sum_piece, 0)
        ag_start()
        for a in range(n):
            half = blks[a][0] // 2
            first[a][0].wait_recv()

            def fold_upper(r, a=a):
                own[a][mine, r, :] = own[a][mine, r, :] + got_a1[a][y, r, :]
                pass_a[a][r, :] = own[a][mine_y, r, :] + got_a1[a][1 - y, r, :]

            rowwise(half, fold_upper)
            first[a][1].wait_recv()
            first[a][2].wait_recv()

            def fold_lower(r, a=a, half=half):
                rl = pl.ds(pl.multiple_of(r.start + half, SUBLANES), r.size)
                own[a][mine, rl, :] = own[a][mine, rl, :] + got_b1[a][x, r, :]
                pass_b[a][r, :] = own[a][mine_x, rl, :] + got_b1[a][1 - x, r, :]

            rowwise(half, fold_lower)
            second.append([hop(a, 1, pass_a[a], got_a2[a], y_nbr), hop(a, 4, pass_b[a], got_b2[a], x_nbr)])
            for cp in second[a]:
                cp.start()
        for e in range(ne):
            part_e, _ = early_in[e]
            landed[e].wait()
            own_block = pltpu.make_async_copy(part_e.at[me], early_buf[e].at[me], early_sems.at[e, 1])
            own_block.start()
            own_block.wait()

            def sum_early(r, e=e):
                g = early_buf[e][0, r, :]
                for dev in range(1, N_DEV):
                    g = g + early_buf[e][dev, r, :]
                early_out[e][r, :] = g

            rowwise(early_buf[e].shape[1], sum_early)
        for a in range(n):
            half = blks[a][0] // 2
            second[a][0].wait_recv()
            second[a][1].wait_recv()

            def finish_rows(r, a=a, half=half):
                rl = pl.ds(pl.multiple_of(r.start + half, SUBLANES), r.size)
                outs[a][r, :] = own[a][mine, r, :] + got_a2[a][r, :]
                outs[a][rl, :] = own[a][mine, rl, :] + got_b2[a][r, :]

            rowwise(half, finish_rows)
        ag_finish()
        for cp in stage1 + [cp for group in first + second for cp in group]:
            cp.wait_send()

    vmem = pl.BlockSpec(memory_space=pltpu.VMEM)
    hbm = pl.BlockSpec(memory_space=pl.ANY)
    piece = small.shape[0] // N_DEV
    assert piece * N_DEV == small.shape[0] and piece % SUBLANES == 0
    out_shape = [jax.ShapeDtypeStruct(b, F32) for b in blks]
    out_shape += [jax.ShapeDtypeStruct((N_DEV, piece, LANES), F32)]
    out_shape += [jax.ShapeDtypeStruct(e[0].shape[1:], F32) for e in early]
    halves = [(b[0] // 2, b[1]) for b in blks]
    scratch = (
        [pltpu.VMEM((4, *b), F32) for b in blks]
        + [pltpu.VMEM((4, *b), F32) for b in blks]
        + [pltpu.VMEM((2, *h), F32) for h in halves]
        + [pltpu.VMEM(h, F32) for h in halves]
        + [pltpu.VMEM((2, *h), F32) for h in halves]
        + [pltpu.VMEM(h, F32) for h in halves] * 3
        + [pltpu.VMEM(e[0].shape, F32) for e in early]
        + [pltpu.VMEM((N_DEV, piece, LANES), F32), pltpu.VMEM((piece, LANES), F32)]
        + [pltpu.SemaphoreType.DMA((n,)), pltpu.SemaphoreType.DMA((n,)), pltpu.SemaphoreType.DMA((n, 5)),
           pltpu.SemaphoreType.DMA((n, 5)), pltpu.SemaphoreType.DMA((n,))]
        + [pltpu.SemaphoreType.DMA((1, 7))] * 4
        + [pltpu.SemaphoreType.DMA((ne, 2))]
    )
    res = pl.pallas_call(
        body,
        name="reduce_all",
        out_shape=out_shape,
        in_specs=[hbm] * n + [vmem] + [hbm, hbm] * ne,
        out_specs=[vmem] * (n + 1 + ne),
        scratch_shapes=scratch,
        compiler_params=_params(),
    )(*parts4, small, *[t for e in early for t in e])
    return list(res[:n]) + list(res[n + 1 :]), res[n].reshape(small.shape)


def _rms(x):
    return lax.rsqrt(jnp.mean(x * x, axis=-1, keepdims=True) + EPS)


def _token_tile(tokens, want):
    tile = min(want, tokens // 2)
    assert tokens % tile == 0 and tile % 16 == 0
    return tile


def _in_proj(x2, g_pre, w_in_blk, shards):
    tokens, d = x2.shape
    nb = w_in_blk.shape[1]
    tm = _token_tile(tokens, IN_PROJ_ROWS)
    n_t = tokens // tm
    ns = len(shards)
    x_pos, y_pos, c_pos = _place()
    slot = lambda px, py, pc: 4 * px + 2 * py + pc
    chip_order = [(x_pos, y_pos), (1 - x_pos, y_pos), (x_pos, 1 - y_pos), (1 - x_pos, 1 - y_pos)]
    order = jnp.stack([2 * px + py for px, py in chip_order]).astype(jnp.int32)
    n_chips = len(chip_order)

    def body(order_ref, x_ref, g_ref, w_ref, *rest):
        shard_hbm, proj_ref, w_hbm = rest[:ns], rest[ns], rest[ns + 1]
        gathered = rest[ns + 2 : 2 * ns + 2]
        h_all, land, w_send, w_recv, out_sem, send_sems, recv_sems, own_sems = rest[2 * ns + 2 :]
        j, i = pl.program_id(0), pl.program_id(1)
        x, y, c = _place()
        me, sibling = (x, y, c), (x, y, 1 - c)
        chips = [(1 - x, y), (x, 1 - y), (1 - x, 1 - y)]
        own = [pltpu.make_async_copy(shard_hbm[a], gathered[a].at[slot(*me)], own_sems.at[a]) for a in range(ns)]
        start, finish = _exchange_steps(ns, lambda a, s: shard_hbm[a], gathered, send_sems, recv_sems)

        def copy(k, block, to):
            ref = land.at[slot(*block)]
            return pltpu.make_async_remote_copy(
                src_ref=ref, dst_ref=ref, send_sem=w_send.at[k], recv_sem=w_recv.at[k], device_id=to, device_id_type=MESH)

        first_sends = [copy(0, me, sibling)] + [copy(1 + k, me, (*chip, c)) for k, chip in enumerate(chips)]
        forwards = [copy(4 + k, (*chip, c), sibling) for k, chip in enumerate(chips)]

        @pl.when((j == 0) & (i == 0))
        def _():
            land[slot(*me)] = w_ref[...].astype(BF16)
            for cp in first_sends:
                cp.start()
            start()
            for cp in own:
                cp.start()
            copy(0, sibling, me).wait_recv()

        for k, chip in enumerate(chips):

            @pl.when((j == k) & (i == n_t - 1))
            def _(k=k, chip=chip):
                copy(1 + k, (*chip, c), me).wait_recv()
                forwards[k].start()

            @pl.when((j == k + 1) & (i == 0))
            def _(k=k, chip=chip):
                copy(4 + k, (*chip, 1 - c), me).wait_recv()

        rows = pl.ds(pl.multiple_of(i * tm, tm), tm)

        @pl.when(j == 0)
        def _():
            x_t = x_ref[...]
            h_all[rows, :] = (x_t * _rms(x_t) * g_ref[...]).astype(BF16)

        h = h_all[rows, :]
        for half in range(2):
            proj_ref[:, half * nb : (half + 1) * nb] = _mm(h, land[2 * order_ref[j] + half])

        @pl.when((j == n_chips - 1) & (i == n_t - 1))
        def _():
            for cp in first_sends + forwards:
                cp.wait_send()
            finish()
            for cp in own:
                cp.wait()
            outs = [pltpu.make_async_copy(land.at[blk], w_hbm.at[:, pl.ds(blk * nb, nb)], out_sem.at[blk])
                    for blk in range(N_DEV)]
            for cp in outs:
                cp.start()
            for cp in outs:
                cp.wait()

    hbm = pl.BlockSpec(memory_space=pl.ANY)
    res = pl.pallas_call(
        body,
        name="in_proj",
        grid_spec=pltpu.PrefetchScalarGridSpec(
            num_scalar_prefetch=1,
            grid=(n_chips, n_t),
            in_specs=[
                pl.BlockSpec((tm, d), lambda j, i, order: (jnp.where(j == 0, i, n_t - 1), 0)),
                pl.BlockSpec((1, d), lambda j, i, order: (0, 0)),
                pl.BlockSpec((d, nb), lambda j, i, order: (0, 0)),
            ]
            + [hbm] * ns,
            out_specs=[pl.BlockSpec((tm, 2 * nb), lambda j, i, order: (i, order[j])), hbm] + [hbm] * ns,
            scratch_shapes=[
                pltpu.VMEM((tokens, d), BF16),
                pltpu.VMEM((N_DEV, d, nb), BF16),
                pltpu.SemaphoreType.DMA((7,)),
                pltpu.SemaphoreType.DMA((7,)),
                pltpu.SemaphoreType.DMA((N_DEV,)),
                pltpu.SemaphoreType.DMA((ns, 7)),
                pltpu.SemaphoreType.DMA((ns, 7)),
                pltpu.SemaphoreType.DMA((ns,)),
            ],
        ),
        out_shape=[
            jax.ShapeDtypeStruct((tokens, N_DEV * nb), F32),
            jax.ShapeDtypeStruct((d, N_DEV * nb), BF16),
        ]
        + [jax.ShapeDtypeStruct((N_DEV, *a.shape), a.dtype) for a in shards],
        compiler_params=_params("arbitrary", "arbitrary"),
    )(order, x2, g_pre, w_in_blk, *shards)
    return res[0], res[1], res[2:]


def _in_proj_bwd(x2, dres, du_pool, du_ssm, dq, dgate, g_pre, w_in_f):
    tokens, d = x2.shape
    nb = w_in_f.shape[1] // N_DEV
    pool_w, ssm_w, att_w, mix = du_pool.shape[1], du_ssm.shape[1], dq.shape[1], dgate.shape[1]
    cl = _token_tile(tokens, TOKEN_ROWS)
    n_tiles = tokens // cl

    def body(x_ref, dres_ref, dup_ref, dus_ref, dq_ref, dgate_ref, g_ref, w_ref, gx_ref, gw_hbm, gg_ref, acc, sem):
        i = pl.program_id(0)

        @pl.when(i == 0)
        def _():
            acc[...] = jnp.zeros_like(acc)
            gg_ref[...] = jnp.zeros_like(gg_ref)

        x = x_ref[...]
        r = _rms(x)
        xn = x * r
        g = g_ref[...]
        h = (xn * g).astype(BF16)
        dproj = jnp.concatenate([dup_ref[...], dus_ref[...], dq_ref[...], dgate_ref[...]], axis=1).astype(BF16)
        dh = _mm_nt(dproj, w_ref[...])
        for j in range(N_DEV):
            acc[j] += _mm_tn(h, dproj[:, j * nb : (j + 1) * nb])
        gg_ref[...] += jnp.sum(dh * xn, axis=0, keepdims=True)
        dxn = dh * g
        gx_ref[...] = dres_ref[...] + r * (dxn - xn * jnp.mean(dxn * xn, axis=-1, keepdims=True))

        @pl.when(i == n_tiles - 1)
        def _():
            cp = pltpu.make_async_copy(acc, gw_hbm, sem)
            cp.start()
            cp.wait()

    return pl.pallas_call(
        body,
        name="in_proj_bwd",
        grid=(n_tiles,),
        in_specs=[
            pl.BlockSpec((cl, d), lambda i: (i, 0)),
            pl.BlockSpec((cl, d), lambda i: (i, 0)),
            pl.BlockSpec((cl, pool_w), lambda i: (i, 0)),
            pl.BlockSpec((cl, ssm_w), lambda i: (i, 0)),
            pl.BlockSpec((cl, att_w), lambda i: (i, 0)),
            pl.BlockSpec((cl, mix), lambda i: (i, 0)),
            pl.BlockSpec((1, d), lambda i: (0, 0)),
            pl.BlockSpec(w_in_f.shape, lambda i: (0, 0)),
        ],
        out_specs=[
            pl.BlockSpec((cl, d), lambda i: (i, 0)),
            pl.BlockSpec(memory_space=pl.ANY),
            pl.BlockSpec((1, d), lambda i: (0, 0)),
        ],
        out_shape=[
            jax.ShapeDtypeStruct((tokens, d), F32),
            jax.ShapeDtypeStruct((N_DEV, d, nb), F32),
            jax.ShapeDtypeStruct((1, d), F32),
        ],
        scratch_shapes=[pltpu.VMEM((N_DEV, d, nb), F32), pltpu.SemaphoreType.DMA],
        compiler_params=_params("arbitrary"),
    )(x2, dres, du_pool, du_ssm, dq, dgate, g_pre, w_in_f)


def _pool_geometry(seq, width):
    gw = width // len(POOL_WINDOWS)
    col = lax.broadcasted_iota(jnp.int32, (1, width), 1)
    win = jnp.full((1, width), float(POOL_WINDOWS[-1]), F32)
    for gi in range(len(POOL_WINDOWS) - 2, -1, -1):
        win = jnp.where(col < (gi + 1) * gw, float(POOL_WINDOWS[gi]), win)
    row = lax.broadcasted_iota(jnp.int32, (seq, width), 0)
    filling = 1.0 / (lax.broadcasted_iota(jnp.int32, (seq, 1), 0) + 1).astype(F32)
    inv_cnt = jnp.where(row + 1 < win.astype(jnp.int32), filling, 1.0 / win)
    return win, row, inv_cnt


def _window_sums(a, win, seq, back):
    pad = 2 * POOL_WINDOWS[-1]
    zeros = jnp.zeros((pad, a.shape[1]), F32)
    s = jnp.concatenate([a, zeros] if back else [zeros, a], axis=0)
    sums = []
    k = 1
    while k < POOL_WINDOWS[-1]:
        s = s + pltpu.roll(s, seq + pad - k if back else k, 0)
        k *= 2
        sums.append((k, s))
    out = sums[-1][1]
    for k, s in reversed(sums[:-1]):
        out = jnp.where(win <= float(k), s, out)
    return out[0:seq] if back else out[pad : pad + seq]


def _pool_fwd(u2, wp_blk, scale, batch, seq):
    width = scale.shape[1]

    def body(u_ref, w_ref, s_ref, y_ref, diff_ref):
        u = u_ref[...]
        win, row, inv_cnt = _pool_geometry(seq, width)
        diff = (_window_sums(u, win, seq, False) * inv_cnt - u).astype(BF16)
        diff_ref[...] = diff
        y_ref[...] = _mm(diff, w_ref[...]) * s_ref[...]

    return pl.pallas_call(
        body,
        name="pool_fwd",
        grid=(batch,),
        in_specs=[
            pl.BlockSpec((seq, width), lambda b: (b, 0)),
            pl.BlockSpec((width, width), lambda b: (0, 0)),
            pl.BlockSpec((1, width), lambda b: (0, 0)),
        ],
        out_specs=[pl.BlockSpec((seq, width), lambda b: (b, 0)), pl.BlockSpec((seq, width), lambda b: (b, 0))],
        out_shape=[
            jax.ShapeDtypeStruct((u2.shape[0], width), F32),
            jax.ShapeDtypeStruct((u2.shape[0], width), BF16),
        ],
        compiler_params=_params("arbitrary"),
    )(u2, wp_blk, scale)


def _pool_bwd(diff2, dy2, wp_blk, scale, batch, seq):
    width = scale.shape[1]

    def body(diff_ref, dy_ref, w_ref, s_ref, du_ref, gw_ref, gs_ref):
        @pl.when(pl.program_id(0) == 0)
        def _():
            gw_ref[...] = jnp.zeros_like(gw_ref)
            gs_ref[...] = jnp.zeros_like(gs_ref)

        diff = diff_ref[...]
        dy = dy_ref[...]
        win, row, inv_cnt = _pool_geometry(seq, width)
        gs_ref[...] += jnp.sum(dy * _mm(diff, w_ref[...]), axis=0, keepdims=True)
        dys = (dy * s_ref[...]).astype(BF16)
        gw_ref[...] += _mm_tn(diff, dys)
        dd = _mm_nt(dys, w_ref[...])
        du_ref[...] = _window_sums(dd * inv_cnt, win, seq, True) - dd

    return pl.pallas_call(
        body,
        name="pool_bwd",
        grid=(batch,),
        in_specs=[
            pl.BlockSpec((seq, width), lambda b: (b, 0)),
            pl.BlockSpec((seq, width), lambda b: (b, 0)),
            pl.BlockSpec((width, width), lambda b: (0, 0)),
            pl.BlockSpec((1, width), lambda b: (0, 0)),
        ],
        out_specs=[
            pl.BlockSpec((seq, width), lambda b: (b, 0)),
            pl.BlockSpec((width, width), lambda b: (0, 0)),
            pl.BlockSpec((1, width), lambda b: (0, 0)),
        ],
        out_shape=[
            jax.ShapeDtypeStruct(dy2.shape, F32),
            jax.ShapeDtypeStruct((width, width), F32),
            jax.ShapeDtypeStruct((1, width), F32),
        ],
        compiler_params=_params("arbitrary"),
    )(diff2, dy2, wp_blk, scale)


def _state_row(z, n_blocks):
    re = jnp.real(z).reshape(n_blocks, -1)
    im = jnp.imag(z).reshape(n_blocks, -1)
    return jnp.concatenate([re, im], axis=1).reshape(1, -1)


def _ssm_tables(a_re, a_im, log_dt, b_re, b_im, c_re, c_im):
    groups, n_state = a_re.shape
    ch = b_re.shape[2]
    nb = groups * ch // LANES
    gl = groups // nb
    lam = lax.complex(a_re, a_im)
    lam_bar = jnp.exp(lam * jnp.exp(log_dt)[:, None])
    b_bar = ((lam_bar - 1.0) / lam)[..., None] * lax.complex(b_re, b_im)
    eye = jnp.eye(gl, dtype=F32)

    def rows_to_state(t):
        return jnp.einsum("sgnc,gh->sgchn", t.reshape(nb, gl, n_state, ch), eye).reshape(nb, gl * ch, gl * n_state)

    def state_to_rows(t):
        return jnp.einsum("sgcn,gh->shngc", t.reshape(nb, gl, ch, n_state), eye).reshape(nb, gl * n_state, gl * ch)

    b_tab = jnp.concatenate([rows_to_state(jnp.real(b_bar)), rows_to_state(jnp.imag(b_bar))], axis=2)
    c_tab = jnp.concatenate([state_to_rows(c_re), -state_to_rows(c_im)], axis=1)
    return _state_row(lam_bar, nb), b_tab, c_tab


def _lam_power(a_re, a_im, log_dt, power, scale, n_blocks):
    return _state_row(scale * jnp.exp(lax.complex(a_re, a_im) * jnp.exp(log_dt)[:, None] * power), n_blocks)


def _state_blocks(s2, n_blocks, width):
    half = s2 // n_blocks // 2
    assert half % width == 0
    return [(b * 2 * half + o, b * 2 * half + half + o) for b in range(n_blocks) for o in range(0, half, width)]


def _scan(src_ref, dst_ref, st_ref, lam8_ref, n_groups, s, n_blocks, reverse, store):
    lb = SCAN_LANES
    for re0, im0 in _state_blocks(2 * s, n_blocks, lb):
        cr, ci = pl.ds(re0, lb), pl.ds(im0, lb)
        lr = lam8_ref[:, cr]
        li = -lam8_ref[:, ci] if reverse else lam8_ref[:, ci]

        unroll = 1 if store else SCAN_UNROLL

        def step(i, carry, cr=cr, ci=ci, lr=lr, li=li):
            hr, hi = carry
            for k in range(unroll):
                grp = i * unroll + k
                grp = n_groups - 1 - grp if reverse else grp
                rows = pl.ds(pl.multiple_of(grp * SUBLANES, SUBLANES), SUBLANES)
                hr, hi = (lr * hr - li * hi + src_ref[rows, cr], lr * hi + li * hr + src_ref[rows, ci])
                if store:
                    dst_ref[rows, cr] = hr
                    dst_ref[rows, ci] = hi
            return hr, hi

        assert n_groups % unroll == 0
        hr, hi = lax.fori_loop(0, n_groups // unroll, step, (st_ref[:, cr], st_ref[:, ci]))
        st_ref[:, cr] = hr
        st_ref[:, ci] = hi


def _pack_state(re, im):
    hi = lax.bitcast_convert_type(re.astype(BF16).astype(F32), jnp.uint32)
    lo = lax.bitcast_convert_type(im.astype(BF16).astype(F32), jnp.uint32)
    return hi | (lo >> 16)


def _unpack_state(word):
    re = lax.bitcast_convert_type(word & jnp.uint32(0xFFFF0000), F32)
    im = lax.bitcast_convert_type(word << 16, F32)
    return re, im


def _lam_adjoint(g_ref, hprev_ref, group0, acc_ref, n_groups, s, n_blocks):
    lb = SCAN_LANES
    half = s // n_blocks
    assert n_groups % SCAN_UNROLL == 0
    for re0, im0 in _state_blocks(2 * s, n_blocks, lb):
        cr, ci = pl.ds(re0, lb), pl.ds(im0, lb)
        ch = pl.ds(re0 // (2 * half) * half + re0 % (2 * half), lb)

        def step(i, carry, cr=cr, ci=ci, ch=ch):
            ar, ai = carry
            for k in range(SCAN_UNROLL):
                grp = i * SCAN_UNROLL + k
                rows = pl.ds(pl.multiple_of(grp * SUBLANES, SUBLANES), SUBLANES)
                gr, gi = g_ref[rows, cr], g_ref[rows, ci]
                hr, hi = _unpack_state(hprev_ref[pl.ds(pl.multiple_of((group0 + grp) * SUBLANES, SUBLANES), SUBLANES), ch])
                ar = ar + hr * gr + hi * gi
                ai = ai + hr * gi - hi * gr
            return ar, ai

        ar, ai = lax.fori_loop(0, n_groups // SCAN_UNROLL, step, (acc_ref[:, cr], acc_ref[:, ci]))
        acc_ref[:, cr] = ar
        acc_ref[:, ci] = ai


def _chunk_starts(st_ref, init_ref, lcl_ref, s, n_blocks):
    w = s // n_blocks
    init_ref[0:1, :] = jnp.zeros((1, 2 * s), F32)
    for re0, im0 in _state_blocks(2 * s, n_blocks, w):
        re, im = pl.ds(re0, w), pl.ds(im0, w)
        ar, ai = lcl_ref[:, re], lcl_ref[:, im]
        cr = jnp.zeros((1, w), F32)
        ci = jnp.zeros((1, w), F32)
        for k in range(1, NCH):
            cr, ci = (ar * cr - ai * ci + st_ref[k - 1 : k, re], ar * ci + ai * cr + st_ref[k - 1 : k, im])
            init_ref[k : k + 1, re] = cr
            init_ref[k : k + 1, im] = ci


def _chunk_starts_adjoint(stg_ref, initg_ref, lcl_ref, s, n_blocks):
    w = s // n_blocks
    initg_ref[NCH - 1 : NCH, :] = jnp.zeros((1, 2 * s), F32)
    for re0, im0 in _state_blocks(2 * s, n_blocks, w):
        re, im = pl.ds(re0, w), pl.ds(im0, w)
        ar, ai = lcl_ref[:, re], -lcl_ref[:, im]
        gr = jnp.zeros((1, w), F32)
        gi = jnp.zeros((1, w), F32)
        for k in range(NCH - 2, -1, -1):
            gr, gi = (stg_ref[k + 1 : k + 2, re] + ar * gr - ai * gi, stg_ref[k + 1 : k + 2, im] + ar * gi + ai * gr)
            initg_ref[k : k + 1, re] = gr
            initg_ref[k : k + 1, im] = gi


def _ssm_rows(seq, want):
    rows = min(want, seq // 2)
    assert seq % rows == 0 and rows % SUBLANES == 0
    return rows


def _chunk_copies(hbm_ref, b, cm_ref, sems, to_cm, col0=0):
    cl, _, width = cm_ref.shape
    copies = []
    for k in range(NCH):
        nat, cm = hbm_ref.at[b, pl.ds(k * cl, cl), pl.ds(col0, width)], cm_ref.at[:, k, :]
        src, dst = (nat, cm) if to_cm else (cm, nat)
        copies.append(pltpu.make_async_copy(src, dst, sems.at[k]))
    return copies


def _blockwise(fn, n_blocks):
    return jnp.concatenate([fn(b) for b in range(n_blocks)], axis=1)


def _ssm_fwd(u, u_col, b_tab, c_tab, lam8, lcl, d_skip, w_glu, shards):
    batch, seq, _ = u.shape
    ns = len(shards)
    width = d_skip.shape[1]
    s = lam8.shape[1] // 2
    nb = b_tab.shape[0]
    sb = 2 * s // nb
    cl = seq // NCH
    rows = _ssm_rows(seq, SSM_ROWS)
    n_tiles = seq // rows
    n_groups = rows // SUBLANES

    def body(u_hbm, b_ref, c_ref, lam_ref, lcl_ref, d_ref, wg_ref, *rest):
        shard_hbm, (y_hbm, pre_ref, z_ref, init_ref) = rest[:ns], rest[ns : ns + 4]
        gathered = rest[ns + 4 : 2 * ns + 4]
        u_cm, y_cm, bu_all, st, sems, send_sems, recv_sems, own_sems = rest[2 * ns + 4 :]
        b, ph, t = pl.program_id(0), pl.program_id(1), pl.program_id(2)
        tile_groups = pl.ds(pl.multiple_of(t * n_groups, n_groups), n_groups)
        x_pos, y_pos, c_pos = _place()
        own = [pltpu.make_async_copy(shard_hbm[a], gathered[a].at[4 * x_pos + 2 * y_pos + c_pos], own_sems.at[a])
               for a in range(ns)]
        exchange_start, exchange_finish = _exchange_steps(
            ns, lambda a, slot: shard_hbm[a], gathered, send_sems, recv_sems)

        @pl.when((b == 0) & (ph == 0) & (t == 0))
        def _():
            exchange_start()
            for cp in own:
                cp.start()

        @pl.when((b == batch - 1) & (ph == 1) & (t == n_tiles - 1))
        def _():
            exchange_finish()
            for cp in own:
                cp.wait()

        @pl.when((ph == 0) & (t == 0))
        def _():
            loads = _chunk_copies(u_hbm, b, u_cm, sems, True, u_col)
            for cp in loads:
                cp.start()
            st[...] = jnp.zeros_like(st)
            for cp in loads:
                cp.wait()

        @pl.when((ph == 1) & (t == 0))
        def _():
            st[...] = init_ref[...]

        u_t = u_cm[tile_groups].reshape(rows, width)
        bu = bu_all.at[pl.ds(pl.multiple_of(t * rows, rows), rows)]

        @pl.when(ph == 0)
        def _():
            u_b = u_t.astype(BF16)
            for blk in range(nb):
                bu[:, blk * sb : (blk + 1) * sb] = _mm(u_b[:, blk * LANES : (blk + 1) * LANES], b_ref[blk])
            _scan(bu, bu, st, lam_ref, n_groups, s, nb, False, False)

        @pl.when((ph == 0) & (t == n_tiles - 1))
        def _():
            _chunk_starts(st, init_ref, lcl_ref, s, nb)

        @pl.when(ph == 1)
        def _():
            _scan(bu, bu, st, lam_ref, n_groups, s, nb, False, True)
            hs = lambda blk: _mm(bu[:, blk * sb : (blk + 1) * sb].astype(BF16), c_ref[blk])
            pre = _blockwise(hs, nb) + d_ref[...] * u_t
            z = _mm(_gelu(pre).astype(BF16), wg_ref[...])
            pre_ref[...] = pre
            z_ref[...] = z
            y = z[:, 0:width] * jax.nn.sigmoid(z[:, width : 2 * width])
            y_cm[tile_groups] = y.reshape(n_groups, SUBLANES, width)

        @pl.when((ph == 1) & (t == n_tiles - 1))
        def _():
            stores = _chunk_copies(y_hbm, b, y_cm, sems, False)
            for cp in stores:
                cp.start()
            for cp in stores:
                cp.wait()

    out_tile = lambda b, ph, t: (b, t * ph, 0)
    full = lambda a: pl.BlockSpec(a.shape, lambda b, ph, t: (0,) * a.ndim)
    hbm = pl.BlockSpec(memory_space=pl.ANY)
    res = pl.pallas_call(
        body,
        name="ssm_fwd",
        grid=(batch, 2, n_tiles),
        in_specs=[hbm, full(b_tab), full(c_tab), full(lam8), full(lcl), full(d_skip), full(w_glu)] + [hbm] * ns,
        out_specs=[
            hbm,
            pl.BlockSpec((None, rows, width), out_tile),
            pl.BlockSpec((None, rows, 2 * width), out_tile),
            pl.BlockSpec((None, SUBLANES, 2 * s), lambda b, ph, t: (b, 0, 0)),
        ]
        + [hbm] * ns,
        out_shape=[
            jax.ShapeDtypeStruct((batch, seq, width), F32),
            jax.ShapeDtypeStruct((batch, seq, width), F32),
            jax.ShapeDtypeStruct((batch, seq, 2 * width), F32),
            jax.ShapeDtypeStruct((batch, SUBLANES, 2 * s), F32),
        ]
        + [jax.ShapeDtypeStruct((N_DEV, *a.shape), a.dtype) for a in shards],
        scratch_shapes=[
            pltpu.VMEM((cl, NCH, width), F32),
            pltpu.VMEM((cl, NCH, width), F32),
            pltpu.VMEM((seq, 2 * s), F32),
            pltpu.VMEM((SUBLANES, 2 * s), F32),
            pltpu.SemaphoreType.DMA((NCH,)),
            pltpu.SemaphoreType.DMA((ns, 7)),
            pltpu.SemaphoreType.DMA((ns, 7)),
            pltpu.SemaphoreType.DMA((ns,)),
        ],
        compiler_params=_params("arbitrary", "arbitrary", "arbitrary"),
    )(u, b_tab, c_tab, lam8, lcl, d_skip, w_glu, *shards)
    return res[:4], res[4:]


def _ssm_bwd(u, u_col, pre_p, z_p, dy, init, b_tab, b_tab_t, c_tab_t, lam8, lcl, d_skip, w_glu, ready):
    batch, seq, _ = u.shape
    width = d_skip.shape[1]
    nr = len(ready)
    s = lam8.shape[1] // 2
    nb = b_tab.shape[0]
    sb = 2 * s // nb
    cl = seq // NCH
    rows = _ssm_rows(seq, SSM_ROWS)
    n_tiles = seq // rows
    n_groups = rows // SUBLANES

    def body(u_hbm, pre_ref, z_ref, dy_hbm, init_ref, b_ref, bt_ref, ct_ref, lam_ref, lcl_ref, d_ref, wg_ref, *rest):
        ready_hbm, rest = rest[:nr], rest[nr:]
        du_hbm, gb_ref, gc_ref, gwg_ref, gd_ref, glam_ref = rest[:6]
        landed_hbm, rest = rest[6 : 6 + nr], rest[6 + nr :]
        u_cm, dy_cm, work, hs_all, dpre_all, st, stg, initg, acc, sems, send_sems, recv_sems = rest
        b, ph, t = pl.program_id(0), pl.program_id(1), pl.program_id(2)
        half = s // nb
        exchange_start, exchange_finish = _exchange_steps(
            nr, lambda a, slot: ready_hbm[a].at[slot], landed_hbm, send_sems, recv_sems)
        first = (b == 0) & (ph == 0) & (t == 0)
        last = (b == batch - 1) & (ph == 2) & (t == n_tiles - 1)
        tile = jnp.where(ph == 0, t, n_tiles - 1 - t)
        tile_rows = pl.ds(pl.multiple_of(tile * rows, rows), rows)
        tile_groups = pl.ds(pl.multiple_of(tile * n_groups, n_groups), n_groups)
        lanes = lambda blk: slice(blk * LANES, (blk + 1) * LANES)
        states = lambda blk: slice(blk * sb, (blk + 1) * sb)

        @pl.when(first)
        def _():
            exchange_start()
            acc[...] = jnp.zeros_like(acc)
            gb_ref[...] = jnp.zeros_like(gb_ref)
            gc_ref[...] = jnp.zeros_like(gc_ref)
            gwg_ref[...] = jnp.zeros_like(gwg_ref)
            gd_ref[...] = jnp.zeros_like(gd_ref)

        @pl.when((ph == 0) & (t == 0))
        def _():
            loads = (_chunk_copies(u_hbm, b, u_cm, sems.at[0], True, u_col)
                     + _chunk_copies(dy_hbm, b, dy_cm, sems.at[1], True))
            for cp in loads:
                cp.start()
            st[...] = init_ref[...]
            for blk in range(nb):
                entry = init_ref[:, states(blk)]
                hs_all[0:SUBLANES, blk * half : (blk + 1) * half] = _pack_state(entry[:, 0:half], entry[:, half : 2 * half])
            for cp in loads:
                cp.wait()

        u_t = u_cm[tile_groups].reshape(rows, width)
        u_b = u_t.astype(BF16)

        @pl.when(ph == 0)
        def _():
            for blk in range(nb):
                work[:, states(blk)] = _mm(u_b[:, lanes(blk)], b_ref[blk])
            _scan(work, work, st, lam_ref, n_groups, s, nb, False, True)
            z = z_ref[...]
            dy_t = dy_cm[tile_groups].reshape(rows, width)
            pre = pre_ref[...]
            z1, sig = z[:, 0:width], jax.nn.sigmoid(z[:, width : 2 * width])
            dz = jnp.concatenate([dy_t * sig, dy_t * z1 * sig * (1.0 - sig)], axis=1).astype(BF16)
            gwg_ref[...] += _mm_tn(_gelu(pre).astype(BF16), dz)
            dpre = _mm_nt(dz, wg_ref[...]) * _gelu_grad(pre)
            dpre_all[tile_rows, :] = dpre
            gd_ref[...] += jnp.sum(dpre * u_t, axis=0, keepdims=True)
            dpre_b = dpre.astype(BF16)
            kept = pl.ds(pl.multiple_of(tile * rows + SUBLANES, SUBLANES), rows)
            for blk in range(nb):
                hs = work[:, states(blk)]
                gc_ref[blk] += _mm_tn(dpre_b[:, lanes(blk)], hs.astype(BF16))
                hs_all[kept, blk * half : (blk + 1) * half] = _pack_state(hs[:, 0:half], hs[:, half : 2 * half])

        @pl.when(ph >= 1)
        def _():
            dpre_b = dpre_all[tile_rows, :].astype(BF16)
            for blk in range(nb):
                work[:, states(blk)] = _mm(dpre_b[:, lanes(blk)], ct_ref[blk])

        @pl.when(ph == 1)
        def _():
            @pl.when(t == 0)
            def _():
                stg[...] = jnp.zeros_like(stg)

            _scan(work, work, stg, lam_ref, n_groups, s, nb, True, False)

            @pl.when(t == n_tiles - 1)
            def _():
                _chunk_starts_adjoint(stg, initg, lcl_ref, s, nb)

        @pl.when(ph == 2)
        def _():
            @pl.when(t == 0)
            def _():
                stg[...] = initg[...]

            _scan(work, work, stg, lam_ref, n_groups, s, nb, True, True)
            _lam_adjoint(work, hs_all, tile * n_groups, acc, n_groups, s, nb)
            du = lambda blk: _mm(work[:, states(blk)].astype(BF16), bt_ref[blk])
            du_t = _blockwise(du, nb) + dpre_all[tile_rows, :] * d_ref[...]
            dy_cm[tile_groups] = du_t.reshape(n_groups, SUBLANES, width)
            for blk in range(nb):
                gb_ref[blk] += _mm_tn(u_b[:, lanes(blk)], work[:, states(blk)].astype(BF16))

            @pl.when(t == n_tiles - 1)
            def _():
                stores = _chunk_copies(du_hbm, b, dy_cm, sems.at[0], False)
                for cp in stores:
                    cp.start()
                for cp in stores:
                    cp.wait()

        @pl.when(last)
        def _():
            glam_ref[...] = jnp.sum(acc[...], axis=0, keepdims=True)
            exchange_finish()

    def tile(b, ph, t):
        return (b, jnp.where(ph == 0, t, n_tiles - 1 - t), 0)

    full = lambda a: pl.BlockSpec(a.shape, lambda b, ph, t: (0,) * a.ndim)
    hbm = pl.BlockSpec(memory_space=pl.ANY)
    res = pl.pallas_call(
        body,
        name="ssm_bwd",
        grid=(batch, 3, n_tiles),
        in_specs=[
            hbm,
            pl.BlockSpec((None, rows, width), tile),
            pl.BlockSpec((None, rows, 2 * width), tile),
            hbm,
            pl.BlockSpec((None, SUBLANES, 2 * s), lambda b, ph, t: (b, 0, 0)),
            full(b_tab), full(b_tab_t), full(c_tab_t), full(lam8), full(lcl), full(d_skip), full(w_glu),
        ]
        + [hbm] * nr,
        out_specs=[
            hbm,
            full(b_tab), full(b_tab), full(w_glu), full(d_skip),
            pl.BlockSpec((1, 2 * s), lambda b, ph, t: (0, 0)),
        ]
        + [hbm] * nr,
        out_shape=[
            jax.ShapeDtypeStruct((batch, seq, width), F32),
            jax.ShapeDtypeStruct(b_tab.shape, F32),
            jax.ShapeDtypeStruct(b_tab.shape, F32),
            jax.ShapeDtypeStruct(w_glu.shape, F32),
            jax.ShapeDtypeStruct(d_skip.shape, F32),
            jax.ShapeDtypeStruct((1, 2 * s), F32),
        ]
        + [jax.ShapeDtypeStruct(a.shape, F32) for a in ready],
        scratch_shapes=[
            pltpu.VMEM((cl, NCH, width), F32),
            pltpu.VMEM((cl, NCH, width), F32),
            pltpu.VMEM((rows, 2 * s), F32),
            pltpu.VMEM((seq + SUBLANES, s), jnp.uint32),
            pltpu.VMEM((seq, width), F32),
        ]
        + [pltpu.VMEM((SUBLANES, 2 * s), F32)] * 4
        + [pltpu.SemaphoreType.DMA((2, NCH)), pltpu.SemaphoreType.DMA((nr, 7)), pltpu.SemaphoreType.DMA((nr, 7))],
        compiler_params=_params("arbitrary", "arbitrary", "arbitrary"),
    )(u, pre_p, z_p, dy, init, b_tab, b_tab_t, c_tab_t, lam8, lcl, d_skip, w_glu, *ready)
    return res[:6], res[6:]


def _kv_fwd(mem, g_mem, w_kv):
    batch, n_mem, d = mem.shape
    kvw = w_kv.shape[1]

    def body(mem_ref, g_ref, w_ref, kv_ref):
        m = mem_ref[...]
        kv_ref[...] = _mm((m * _rms(m) * g_ref[...]).astype(BF16), w_ref[...])

    return pl.pallas_call(
        body,
        name="kv_fwd",
        grid=(batch,),
        in_specs=[
            pl.BlockSpec((None, n_mem, d), lambda b: (b, 0, 0)),
            pl.BlockSpec((1, d), lambda b: (0, 0)),
            pl.BlockSpec((d, kvw), lambda b: (0, 0)),
        ],
        out_specs=pl.BlockSpec((None, n_mem, kvw), lambda b: (b, 0, 0)),
        out_shape=jax.ShapeDtypeStruct((batch, n_mem, kvw), F32),
        compiler_params=_params("arbitrary"),
    )(mem, g_mem, w_kv)


def _kv_bwd(mem, dkv, g_mem, w_kv):
    batch, n_mem, d = mem.shape
    kvw = w_kv.shape[1]

    def body(mem_ref, dkv_ref, g_ref, w_ref, gw_ref, gg_ref):
        @pl.when(pl.program_id(0) == 0)
        def _():
            gw_ref[...] = jnp.zeros_like(gw_ref)
            gg_ref[...] = jnp.zeros_like(gg_ref)

        m = mem_ref[...]
        mn = m * _rms(m)
        dkv_b = dkv_ref[...].astype(BF16)
        gw_ref[...] += _mm_tn((mn * g_ref[...]).astype(BF16), dkv_b)
        gg_ref[...] += jnp.sum(_mm_nt(dkv_b, w_ref[...]) * mn, axis=0, keepdims=True)

    return pl.pallas_call(
        body,
        name="kv_bwd",
        grid=(batch,),
        in_specs=[
            pl.BlockSpec((None, n_mem, d), lambda b: (b, 0, 0)),
            pl.BlockSpec((None, n_mem, kvw), lambda b: (b, 0, 0)),
            pl.BlockSpec((1, d), lambda b: (0, 0)),
            pl.BlockSpec((d, kvw), lambda b: (0, 0)),
        ],
        out_specs=[pl.BlockSpec((d, kvw), lambda b: (0, 0)), pl.BlockSpec((1, d), lambda b: (0, 0))],
        out_shape=[jax.ShapeDtypeStruct((d, kvw), F32), jax.ShapeDtypeStruct((1, d), F32)],
        compiler_params=_params("arbitrary"),
    )(mem, dkv, g_mem, w_kv)


def _tail(x2, target2, proj, y_pool, y_ssm, kv, w_out, g_post):
    tokens, d = x2.shape
    batch, n_mem, kvw = kv.shape
    pool_w, ssm_w, att_w, mix = y_pool.shape[1], y_ssm.shape[1], kvw // 2, w_out.shape[0]
    assert (mix - att_w) % att_w == 0 and proj.shape[1] == 2 * mix
    hd = att_w // MEM_HEADS
    cl = _token_tile(tokens // batch, TOKEN_ROWS)
    n_tiles = tokens // cl
    per_seq = tokens // batch // cl
    qk_scale = hd**-0.5

    def body(x_ref, tg_ref, gate_ref, yp_ref, ys_ref, q_ref, kv_ref, w_ref, g_ref,
             dres_ref, dgate_ref, dyp_ref, dys_ref, dq_ref, dkv_ref, gw_hbm, gg_ref, loss_ref, acc, sem):
        i = pl.program_id(0)

        @pl.when(i == 0)
        def _():
            acc[...] = jnp.zeros_like(acc)
            gg_ref[...] = jnp.zeros_like(gg_ref)
            loss_ref[...] = jnp.zeros_like(loss_ref)

        @pl.when(i % per_seq == 0)
        def _():
            dkv_ref[...] = jnp.zeros_like(dkv_ref)

        k = kv_ref[:, 0:att_w].astype(BF16)
        v = kv_ref[:, att_w : 2 * att_w].astype(BF16)
        lane = lax.broadcasted_iota(jnp.int32, (1, att_w), 1)
        heads = [(lane >= h * hd) & (lane < (h + 1) * hd) for h in range(MEM_HEADS)]
        g = g_ref[...]

        def part(rows):
            n_rows = rows.stop - rows.start
            q = q_ref[rows, :]
            probs, q_heads = [], []
            att = jnp.zeros((n_rows, att_w), F32)
            for mask in heads:
                qh = jnp.where(mask, q, 0.0).astype(BF16)
                sc = _mm_nt(qh, k) * qk_scale
                e = jnp.exp(sc - jnp.max(sc, axis=-1, keepdims=True))
                p = e * (1.0 / jnp.sum(e, axis=-1, keepdims=True))
                att = att + jnp.where(mask, _mm(p.astype(BF16), v), 0.0)
                probs.append(p)
                q_heads.append(qh)

            ycat = jnp.concatenate([yp_ref[rows, :], ys_ref[rows, :], att], axis=1)
            gate = gate_ref[rows, :]
            sig = jax.nn.sigmoid(gate)
            silu = gate * sig
            yg = (ycat * silu).astype(BF16)
            out = _mm(yg, w_ref[...])
            r = _rms(out)
            on = out * r
            err = x_ref[rows, :] + on * g - tg_ref[rows, :]
            loss_ref[...] += 0.5 * jnp.sum(jnp.mean(err * err, axis=-1, keepdims=True), axis=0, keepdims=True)
            dres = err * (1.0 / d)
            dres_ref[rows, :] = dres
            gg_ref[...] += jnp.sum(dres * on, axis=0, keepdims=True)
            don = dres * g
            dout = (r * (don - on * jnp.mean(don * on, axis=-1, keepdims=True))).astype(BF16)
            acc[...] += _mm_tn(yg, dout)
            dyg = _mm_nt(dout, w_ref[...])
            dgate_ref[rows, :] = dyg * ycat * (sig * (1.0 + gate * (1.0 - sig)))
            dycat = dyg * silu
            dyp_ref[rows, :] = dycat[:, 0:pool_w]
            dys_ref[rows, :] = dycat[:, pool_w : pool_w + ssm_w]
            datt = dycat[:, pool_w + ssm_w : mix]

            dq = jnp.zeros((n_rows, att_w), F32)
            dk = jnp.zeros((n_mem, att_w), F32)
            dv = jnp.zeros((n_mem, att_w), F32)
            for mask, p, qh in zip(heads, probs, q_heads):
                doh = jnp.where(mask, datt, 0.0).astype(BF16)
                dp = _mm_nt(doh, v)
                ds = (p * (dp - jnp.sum(p * dp, axis=-1, keepdims=True)) * qk_scale).astype(BF16)
                dq = dq + jnp.where(mask, _mm(ds, k), 0.0)
                dk = dk + _mm_tn(ds, qh)
                dv = dv + _mm_tn(p.astype(BF16), doh)
            dq_ref[rows, :] = dq
            dkv_ref[:, 0:att_w] += dk
            dkv_ref[:, att_w : 2 * att_w] += dv

        part(slice(0, cl))

        @pl.when(i == n_tiles - 1)
        def _():
            cp = pltpu.make_async_copy(acc, gw_hbm, sem)
            cp.start()
            cp.wait()

    tok = lambda w: pl.BlockSpec((cl, w), lambda i: (i, 0))
    chunked = tok(ssm_w)
    per_batch = pl.BlockSpec((None, n_mem, kvw), lambda i: (i // per_seq, 0, 0))
    return pl.pallas_call(
        body,
        name="tail",
        grid=(n_tiles,),
        in_specs=[
            tok(d), tok(d), pl.BlockSpec((cl, mix), lambda i: (i, 1)), tok(pool_w), chunked,
            pl.BlockSpec((cl, att_w), lambda i: (i, (mix - att_w) // att_w)), per_batch,
            pl.BlockSpec((mix, d), lambda i: (0, 0)),
            pl.BlockSpec((1, d), lambda i: (0, 0)),
        ],
        out_specs=[
            tok(d), tok(mix), tok(pool_w), chunked, tok(att_w), per_batch,
            pl.BlockSpec(memory_space=pl.ANY),
            pl.BlockSpec((1, d), lambda i: (0, 0)),
            pl.BlockSpec((1, 1), lambda i: (0, 0)),
        ],
        out_shape=[
            jax.ShapeDtypeStruct((tokens, d), F32),
            jax.ShapeDtypeStruct((tokens, mix), F32),
            jax.ShapeDtypeStruct((tokens, pool_w), F32),
            jax.ShapeDtypeStruct((tokens, ssm_w), F32),
            jax.ShapeDtypeStruct((tokens, att_w), F32),
            jax.ShapeDtypeStruct(kv.shape, F32),
            jax.ShapeDtypeStruct((mix, d), F32),
            jax.ShapeDtypeStruct((1, d), F32),
            jax.ShapeDtypeStruct((1, 1), F32),
        ],
        scratch_shapes=[pltpu.VMEM((mix, d), F32), pltpu.SemaphoreType.DMA],
        compiler_params=_params("arbitrary"),
    )(x2, target2, proj, y_pool, y_ssm, proj, kv, w_out, g_post)


def _pack(arrays):
    flat = jnp.concatenate([a.reshape(-1) for a in arrays])
    rows = -(-flat.size // (N_DEV * SUBLANES * LANES)) * N_DEV * SUBLANES
    return jnp.pad(flat, (0, rows * LANES - flat.size)).reshape(rows, LANES)


def _unpack(packed, like):
    flat, out, at = packed.reshape(-1), [], 0
    for a in like:
        out.append(flat[at : at + a.size].reshape(a.shape))
        at += a.size
    return out


def kernel(x, mem, g_pre, w_in, w_pool, pool_scale, a_re, a_im, log_dt, b_re, b_im, c_re, c_im, d_skip, w_glu, g_mem, w_kv, w_out, g_post, loss_target, m_g_pre, m_w_in, m_w_pool, m_pool_scale, m_a_re, m_a_im, m_log_dt, m_b_re, m_b_im, m_c_re, m_c_im, m_d_skip, m_w_glu, m_g_mem, m_w_kv, m_w_out, m_g_post, v_g_pre, v_w_in, v_w_pool, v_pool_scale, v_a_re, v_a_im, v_log_dt, v_b_re, v_b_im, v_c_re, v_c_im, v_d_skip, v_w_glu, v_g_mem, v_w_kv, v_w_out, v_g_post):
    batch, seq, d = x.shape
    cl = seq // NCH
    pool_w, ssm_w = pool_scale.shape[1], d_skip.shape[1]
    att_w = w_kv.shape[2] // 2
    tokens = batch * seq
    x2 = x.reshape(tokens, d)
    target2 = loss_target.reshape(tokens, d)

    wp_blk = jax.scipy.linalg.block_diag(*w_pool[0]).astype(BF16)
    ssm_params = (a_re[0], a_im[0], log_dt[0], b_re[0], b_im[0], c_re[0], c_im[0])
    (lam_row, b_tab, c_tab), tables_vjp = jax.vjp(_ssm_tables, *ssm_params)
    nb = b_tab.shape[0]
    lam8 = jnp.broadcast_to(lam_row, (SUBLANES, lam_row.shape[1]))
    lcl = _lam_power(a_re[0], a_im[0], log_dt[0], float(cl), 1.0, nb)
    b_bf, c_bf = b_tab.astype(BF16), c_tab.astype(BF16)

    proj, w_in_f, (w_glu_g,) = _in_proj(x2, g_pre, w_in[0], [w_glu[0].astype(BF16)])
    proj3 = proj.reshape(batch, seq, proj.shape[1])
    w_glu_f = w_glu_g.transpose(1, 0, 2).reshape(w_glu_g.shape[1], N_DEV * w_glu_g.shape[2])
    y_pool, diff_pool = _pool_fwd(proj, wp_blk, pool_scale, batch, seq)
    (y_ssm, pre_ssm, z_ssm, init_ssm), (w_out_g, w_kv_g) = _ssm_fwd(
        proj3, pool_w, b_bf, c_bf, lam8, lcl, d_skip, w_glu_f, [w_out[0].astype(BF16), w_kv[0].astype(BF16)])
    w_out_f = w_out_g.reshape(N_DEV * w_out_g.shape[1], w_out_g.shape[2])
    w_kv_f = w_kv_g.reshape(N_DEV * w_kv_g.shape[1], w_kv_g.shape[2])
    kv = _kv_fwd(mem, g_mem, w_kv_f)

    dres, dgate, dy_pool, dy_ssm, dq, dkv, gw_out, gg_post, loss_part = _tail(
        x2, target2, proj, y_pool, y_ssm.reshape(tokens, ssm_w), kv, w_out_f, g_post)

    gw_kv, gg_mem = _kv_bwd(mem, dkv, g_mem, w_kv_f)
    du_pool, gwp_dense, g_scale = _pool_bwd(diff_pool, dy_pool, wp_blk, pool_scale, batch, seq)
    gw_kv8 = gw_kv.reshape(N_DEV, -1, gw_kv.shape[1])
    gw_out8 = gw_out.reshape(N_DEV, -1, gw_out.shape[1])
    (du_ssm, gb_tab, gc_tab_t, gw_glu, gd_skip, glam), (kv_landed, out_landed) = _ssm_bwd(
        proj3, pool_w, pre_ssm, z_ssm, dy_ssm.reshape(batch, seq, ssm_w), init_ssm, b_bf, b_bf.transpose(0, 2, 1),
        c_bf.transpose(0, 2, 1), lam8, lcl, d_skip, w_glu_f, [gw_kv8, gw_out8])
    grad_x2, gw_in, gg_pre = _in_proj_bwd(
        x2, dres, du_pool, du_ssm.reshape(tokens, ssm_w), dq, dgate, g_pre,
        w_in_f)

    gw = pool_w // len(POOL_WINDOWS)
    gw_pool = jnp.stack([gwp_dense[i * gw : (i + 1) * gw, i * gw : (i + 1) * gw] for i in range(len(POOL_WINDOWS))])
    g_ssm = tables_vjp((glam, gb_tab, gc_tab_t.transpose(0, 2, 1)))

    small_w = [g_pre, w_pool, pool_scale, a_re, a_im, log_dt, b_re, b_im, c_re, c_im, d_skip, g_mem, g_post]
    small_m = [m_g_pre, m_w_pool, m_pool_scale, m_a_re, m_a_im, m_log_dt, m_b_re, m_b_im, m_c_re, m_c_im, m_d_skip, m_g_mem, m_g_post]
    small_v = [v_g_pre, v_w_pool, v_pool_scale, v_a_re, v_a_im, v_log_dt, v_b_re, v_b_im, v_c_re, v_c_im, v_d_skip, v_g_mem, v_g_post]
    small_g = [gg_pre, gw_pool, g_scale, *g_ssm, gd_skip, gg_mem, gg_post]
    big_g, small_sum = _reduce_all(
        [gw_in, gw_glu.reshape(ssm_w, N_DEV, -1).transpose(1, 0, 2)],
        _pack(small_g + [loss_part]),
        [(gw_kv8, kv_landed), (gw_out8, out_landed)])

    flat2 = lambda a: a.reshape(-1, a.shape[-1])
    sg = _unpack(small_sum, [flat2(a) for a in small_w] + [loss_part])
    loss = sg[-1].reshape(())
    small_names = ["g_pre", "w_pool", "pool_scale", "a_re", "a_im", "log_dt", "b_re", "b_im", "c_re", "c_im",
                   "d_skip", "g_mem", "g_post"]
    names = ["w_in", "w_glu", "w_kv", "w_out"] + small_names
    all_w = [w_in, w_glu, w_kv, w_out] + small_w
    all_m = [m_w_in, m_w_glu, m_w_kv, m_w_out] + small_m
    all_v = [v_w_in, v_w_glu, v_w_kv, v_w_out] + small_v
    updates = _adamw_all(big_g + sg[:-1], [flat2(a) for a in all_w], [flat2(a) for a in all_m], [flat2(a) for a in all_v])
    updates = {name: [t.reshape(a.shape) for t in u] for name, u, a in zip(names, updates, all_w)}

    order = ["g_pre", "w_in", "w_pool", "pool_scale", "a_re", "a_im", "log_dt", "b_re", "b_im", "c_re", "c_im",
             "d_skip", "w_glu", "g_mem", "w_kv", "w_out", "g_post"]
    outs = [[updates[name][kind] for name in order] for kind in range(4)]
    return (loss, grad_x2.reshape(batch, seq, d), *outs[0], *outs[1], *outs[2], *outs[3])
```

```python
import math

import jax
import jax.numpy as jnp
from jax import lax
from jax.experimental import pallas as pl
from jax.experimental.pallas import tpu as pltpu

F32 = jnp.float32
BF16 = jnp.bfloat16
MESH = pl.DeviceIdType.MESH

N_DEV = 8
SUBLANES = 8
LANES = 128
NCH = SUBLANES
VMEM_LIMIT = 60 * 1024 * 1024

IN_PROJ_ROWS = 2048
TOKEN_ROWS = 512
SSM_ROWS = 1024
SCAN_LANES = 512
SCAN_UNROLL = 8

EPS = 1e-6
POOL_WINDOWS = (2, 4, 8, 16)
MEM_HEADS = 4
ADAM_LR, ADAM_B1, ADAM_B2, ADAM_EPS, ADAM_WD, ADAM_STEP = 0.001, 0.9, 0.999, 1e-08, 0.01, 10


def _mm(a, b):
    return jnp.dot(a, b, preferred_element_type=F32)


def _mm_nt(a, b):
    return lax.dot_general(a, b, (((1,), (1,)), ((), ())), preferred_element_type=F32)


def _mm_tn(a, b):
    return lax.dot_general(a, b, (((0,), (0,)), ((), ())), preferred_element_type=F32)


def _params(*sem):
    return pltpu.CompilerParams(dimension_semantics=sem or None, vmem_limit_bytes=VMEM_LIMIT)


def _adamw(w, g, m, v):
    m = ADAM_B1 * m + (1.0 - ADAM_B1) * g
    v = ADAM_B2 * v + (1.0 - ADAM_B2) * (g * g)
    m_hat = m / (1.0 - ADAM_B1**ADAM_STEP)
    v_hat = v / (1.0 - ADAM_B2**ADAM_STEP)
    delta = -ADAM_LR * (m_hat / (jnp.sqrt(v_hat) + ADAM_EPS) + ADAM_WD * w)
    return delta, m, v


def _gelu(x):
    k = math.sqrt(2.0 / math.pi)
    return 0.5 * x * (1.0 + jnp.tanh(k * (x + 0.044715 * x * x * x)))


def _gelu_grad(x):
    k = math.sqrt(2.0 / math.pi)
    th = jnp.tanh(k * (x + 0.044715 * x * x * x))
    return 0.5 * (1.0 + th) + 0.5 * x * (1.0 - th * th) * (k * (1.0 + 3.0 * 0.044715 * x * x))


def _place():
    return lax.axis_index("x"), lax.axis_index("y"), lax.axis_index("c")


def _exchange_steps(n, src_of, landing, send_sems, recv_sems):
    x, y, c = _place()
    me = 4 * x + 2 * y + c
    peers = []
    for j in range(1, N_DEV):
        px = 1 - x if j & 4 else x
        py = 1 - y if j & 2 else y
        pc = 1 - c if j & 1 else c
        peers.append((px, py, pc))

    def copy(a, j, from_slot, to_slot, peer):
        return pltpu.make_async_remote_copy(
            src_ref=src_of(a, to_slot), dst_ref=landing[a].at[from_slot],
            send_sem=send_sems.at[a, j], recv_sem=recv_sems.at[a, j], device_id=peer, device_id_type=MESH)

    def start():
        for a in range(n):
            for j, p in enumerate(peers):
                copy(a, j, me, 4 * p[0] + 2 * p[1] + p[2], p).start()

    def finish():
        for a in range(n):
            for j, p in enumerate(peers):
                slot = 4 * p[0] + 2 * p[1] + p[2]
                copy(a, j, slot, slot, p).wait_recv()
        for a in range(n):
            for j, p in enumerate(peers):
                copy(a, j, me, 4 * p[0] + 2 * p[1] + p[2], p).wait_send()

    return start, finish


def _adamw_all(gs, ws, ms, vs):
    n = len(gs)

    def body(*refs):
        g, w, m, v = refs[:n], refs[n : 2 * n], refs[2 * n : 3 * n], refs[3 * n : 4 * n]
        outs = refs[4 * n :]
        for a in range(n):
            rows = g[a].shape[0]
            chunk = math.gcd(rows, 128)

            def step(i, _, a=a, chunk=chunk):
                r = pl.ds(pl.multiple_of(i * chunk, chunk), chunk)
                grad = g[a][r, :]
                delta, nm, nv = _adamw(w[a][r, :], grad, m[a][r, :], v[a][r, :])
                outs[4 * a][r, :] = grad
                outs[4 * a + 1][r, :] = delta
                outs[4 * a + 2][r, :] = nm
                outs[4 * a + 3][r, :] = nv
                return 0

            lax.fori_loop(0, rows // chunk, step, 0)

    vmem = pl.BlockSpec(memory_space=pltpu.VMEM)
    out_shape = []
    for wa in ws:
        out_shape += [jax.ShapeDtypeStruct(wa.shape, F32)] * 4
    res = pl.pallas_call(
        body,
        name="adamw_all",
        out_shape=out_shape,
        in_specs=[vmem] * (4 * n),
        out_specs=[vmem] * (4 * n),
        compiler_params=_params(),
    )(*gs, *ws, *ms, *vs)
    return [tuple(res[4 * a : 4 * a + 4]) for a in range(n)]


def _reduce_all(parts, small, early):
    n, ne = len(parts), len(early)
    parts4 = [p.reshape(4, 2, *p.shape[1:]) for p in parts]
    blks = [p.shape[1:] for p in parts]

    def body(*refs):
        refs = list(refs)
        take = lambda k: [refs.pop(0) for _ in range(k)]
        part = take(n)
        (small_ref,) = take(1)
        early_in = [take(2) for _ in range(ne)]
        outs = take(n)
        (small_all,) = take(1)
        early_out = take(ne)
        own, r1, got_a1, got_a2, got_b1, got_b2, pass_a, pass_b = (take(n) for _ in range(8))
        early_buf = take(ne)
        small_land, small_mine = take(2)
        s1_send, s1_recv, h_send, h_recv, loc, rs_send, rs_recv, ag_send, ag_recv, early_sems = refs
        x, y, c = _place()
        me = 4 * x + 2 * y + c
        piece = small_mine.shape[0]
        piece_of = lambda slot: pl.ds(pl.multiple_of(slot * piece, SUBLANES), piece)
        rs_start, rs_finish = _exchange_steps(
            1, lambda a, slot: small_ref.at[piece_of(slot)], [small_land], rs_send, rs_recv)
        ag_start, ag_finish = _exchange_steps(1, lambda a, slot: small_mine, [small_all], ag_send, ag_recv)
        landed = [pltpu.make_async_copy(early_in[e][1], early_buf[e], early_sems.at[e, 0]) for e in range(ne)]
        for cp in landed:
            cp.start()
        sibling = (x, y, 1 - c)
        chips = [(1 - x, y), (x, 1 - y), (1 - x, 1 - y)]

        def rowwise(rows, fn):
            chunk = math.gcd(rows, 128)

            def step(i, _):
                fn(pl.ds(pl.multiple_of(i * chunk, chunk), chunk))
                return 0

            lax.fori_loop(0, rows // chunk, step, 0)

        stage1, local = [], []
        for a in range(n):
            cp = pltpu.make_async_remote_copy(
                src_ref=part[a].at[:, 1 - c], dst_ref=r1[a], send_sem=s1_send.at[a], recv_sem=s1_recv.at[a],
                device_id=sibling, device_id_type=MESH)
            cp.start()
            stage1.append(cp)
            lc = pltpu.make_async_copy(part[a].at[:, c], own[a], loc.at[a])
            lc.start()
            local.append(lc)
        rs_start()
        x_nbr, y_nbr = (1 - x, y, c), (x, 1 - y, c)
        mine, mine_x, mine_y = 2 * x + y, 2 * (1 - x) + y, 2 * x + (1 - y)

        def hop(a, k, src, dst, to):
            return pltpu.make_async_remote_copy(
                src_ref=src, dst_ref=dst, send_sem=h_send.at[a, k], recv_sem=h_recv.at[a, k],
                device_id=to, device_id_type=MESH)

        first, second = [], []
        for a in range(n):
            half = blks[a][0] // 2
            up, low = pl.ds(0, half), pl.ds(half, half)
            local[a].wait()
            stage1[a].wait_recv()
            for chip in range(4):

                def add(r, a=a, chip=chip):
                    own[a][chip, r, :] = own[a][chip, r, :] + r1[a][chip, r, :]

                rowwise(blks[a][0], add)
            first.append([
                hop(a, 0, own[a].at[pl.ds(2 * (1 - x), 2), up], got_a1[a], x_nbr),
                hop(a, 2, own[a].at[2 * x + (1 - y), low], got_b1[a].at[x], y_nbr),
                hop(a, 3, own[a].at[2 * (1 - x) + (1 - y), low], got_b1[a].at[1 - x], y_nbr),
            ])
            for cp in first[a]:
                cp.start()
        rs_finish()
        small_land[me] = small_ref[piece_of(me), :]

        def sum_piece(i, _):
            r = pl.ds(pl.multiple_of(i * SUBLANES, SUBLANES), SUBLANES)
            total = small_land[0, r, :]
            for dev in range(1, N_DEV):
                total = total + small_land[dev, r, :]
            small_mine[r, :] = total
            small_all[me, r, :] = total
            return 0

        lax.fori_loop(0, piece // SUBLANES, sum_piece, 0)
        ag_start()
        for a in range(n):
            half = blks[a][0] // 2
            first[a][0].wait_recv()

            def fold_upper(r, a=a):
                own[a][mine, r, :] = own[a][mine, r, :] + got_a1[a][y, r, :]
                pass_a[a][r, :] = own[a][mine_y, r, :] + got_a1[a][1 - y, r, :]

            rowwise(half, fold_upper)
            first[a][1].wait_recv()
            first[a][2].wait_recv()

            def fold_lower(r, a=a, half=half):
                rl = pl.ds(pl.multiple_of(r.start + half, SUBLANES), r.size)
                own[a][mine, rl, :] = own[a][mine, rl, :] + got_b1[a][x, r, :]
                pass_b[a][r, :] = own[a][mine_x, rl, :] + got_b1[a][1 - x, r, :]

            rowwise(half, fold_lower)
            second.append([hop(a, 1, pass_a[a], got_a2[a], y_nbr), hop(a, 4, pass_b[a], got_b2[a], x_nbr)])
            for cp in second[a]:
                cp.start()
        for e in range(ne):
            part_e, _ = early_in[e]
            landed[e].wait()
            own_block = pltpu.make_async_copy(part_e.at[me], early_buf[e].at[me], early_sems.at[e, 1])
            own_block.start()
            own_block.wait()

            def sum_early(r, e=e):
                g = early_buf[e][0, r, :]
                for dev in range(1, N_DEV):
                    g = g + early_buf[e][dev, r, :]
                early_out[e][r, :] = g

            rowwise(early_buf[e].shape[1], sum_early)
        for a in range(n):
            half = blks[a][0] // 2
            second[a][0].wait_recv()
            second[a][1].wait_recv()

            def finish_rows(r, a=a, half=half):
                rl = pl.ds(pl.multiple_of(r.start + half, SUBLANES), r.size)
                outs[a][r, :] = own[a][mine, r, :] + got_a2[a][r, :]
                outs[a][rl, :] = own[a][mine, rl, :] + got_b2[a][r, :]

            rowwise(half, finish_rows)
        ag_finish()
        for cp in stage1 + [cp for group in first + second for cp in group]:
            cp.wait_send()

    vmem = pl.BlockSpec(memory_space=pltpu.VMEM)
    hbm = pl.BlockSpec(memory_space=pl.ANY)
    piece = small.shape[0] // N_DEV
    assert piece * N_DEV == small.shape[0] and piece % SUBLANES == 0
    out_shape = [jax.ShapeDtypeStruct(b, F32) for b in blks]
    out_shape += [jax.ShapeDtypeStruct((N_DEV, piece, LANES), F32)]
    out_shape += [jax.ShapeDtypeStruct(e[0].shape[1:], F32) for e in early]
    halves = [(b[0] // 2, b[1]) for b in blks]
    scratch = (
        [pltpu.VMEM((4, *b), F32) for b in blks]
        + [pltpu.VMEM((4, *b), F32) for b in blks]
        + [pltpu.VMEM((2, *h), F32) for h in halves]
        + [pltpu.VMEM(h, F32) for h in halves]
        + [pltpu.VMEM((2, *h), F32) for h in halves]
        + [pltpu.VMEM(h, F32) for h in halves] * 3
        + [pltpu.VMEM(e[0].shape, F32) for e in early]
        + [pltpu.VMEM((N_DEV, piece, LANES), F32), pltpu.VMEM((piece, LANES), F32)]
        + [pltpu.SemaphoreType.DMA((n,)), pltpu.SemaphoreType.DMA((n,)), pltpu.SemaphoreType.DMA((n, 5)),
           pltpu.SemaphoreType.DMA((n, 5)), pltpu.SemaphoreType.DMA((n,))]
        + [pltpu.SemaphoreType.DMA((1, 7))] * 4
        + [pltpu.SemaphoreType.DMA((ne, 2))]
    )
    res = pl.pallas_call(
        body,
        name="reduce_all",
        out_shape=out_shape,
        in_specs=[hbm] * n + [vmem] + [hbm, hbm] * ne,
        out_specs=[vmem] * (n + 1 + ne),
        scratch_shapes=scratch,
        compiler_params=_params(),
    )(*parts4, small, *[t for e in early for t in e])
    return list(res[:n]) + list(res[n + 1 :]), res[n].reshape(small.shape)


def _rms(x):
    return lax.rsqrt(jnp.mean(x * x, axis=-1, keepdims=True) + EPS)


def _token_tile(tokens, want):
    tile = min(want, tokens // 2)
    assert tokens % tile == 0 and tile % 16 == 0
    return tile


def _in_proj(x2, g_pre, w_in_blk, shards):
    tokens, d = x2.shape
    nb = w_in_blk.shape[1]
    tm = _token_tile(tokens, IN_PROJ_ROWS)
    n_t = tokens // tm
    ns = len(shards)
    x_pos, y_pos, c_pos = _place()
    slot = lambda px, py, pc: 4 * px + 2 * py + pc
    chip_order = [(x_pos, y_pos), (1 - x_pos, y_pos), (x_pos, 1 - y_pos), (1 - x_pos, 1 - y_pos)]
    order = jnp.stack([2 * px + py for px, py in chip_order]).astype(jnp.int32)
    n_chips = len(chip_order)

    def body(order_ref, x_ref, g_ref, w_ref, *rest):
        shard_hbm, proj_ref, w_hbm = rest[:ns], rest[ns], rest[ns + 1]
        gathered = rest[ns + 2 : 2 * ns + 2]
        h_all, land, w_send, w_recv, out_sem, send_sems, recv_sems, own_sems = rest[2 * ns + 2 :]
        j, i = pl.program_id(0), pl.program_id(1)
        x, y, c = _place()
        me, sibling = (x, y, c), (x, y, 1 - c)
        chips = [(1 - x, y), (x, 1 - y), (1 - x, 1 - y)]
        own = [pltpu.make_async_copy(shard_hbm[a], gathered[a].at[slot(*me)], own_sems.at[a]) for a in range(ns)]
        start, finish = _exchange_steps(ns, lambda a, s: shard_hbm[a], gathered, send_sems, recv_sems)

        def copy(k, block, to):
            ref = land.at[slot(*block)]
            return pltpu.make_async_remote_copy(
                src_ref=ref, dst_ref=ref, send_sem=w_send.at[k], recv_sem=w_recv.at[k], device_id=to, device_id_type=MESH)

        first_sends = [copy(0, me, sibling)] + [copy(1 + k, me, (*chip, c)) for k, chip in enumerate(chips)]
        forwards = [copy(4 + k, (*chip, c), sibling) for k, chip in enumerate(chips)]

        @pl.when((j == 0) & (i == 0))
        def _():
            land[slot(*me)] = w_ref[...].astype(BF16)
            for cp in first_sends:
                cp.start()
            copy(0, sibling, me).wait_recv()

        @pl.when((j == n_chips // 2) & (i == 0))
        def _():
            start()
            for cp in own:
                cp.start()

        for k, chip in enumerate(chips):

            @pl.when((j == k) & (i == n_t - 1))
            def _(k=k, chip=chip):
                copy(1 + k, (*chip, c), me).wait_recv()
                forwards[k].start()

            @pl.when((j == k + 1) & (i == 0))
            def _(k=k, chip=chip):
                copy(4 + k, (*chip, 1 - c), me).wait_recv()

        rows = pl.ds(pl.multiple_of(i * tm, tm), tm)

        @pl.when(j == 0)
        def _():
            x_t = x_ref[...]
            h_all[rows, :] = (x_t * _rms(x_t) * g_ref[...]).astype(BF16)

        h = h_all[rows, :]
        for half in range(2):
            proj_ref[:, half * nb : (half + 1) * nb] = _mm(h, land[2 * order_ref[j] + half])

        @pl.when((j == n_chips - 1) & (i == n_t - 1))
        def _():
            for cp in first_sends + forwards:
                cp.wait_send()
            finish()
            for cp in own:
                cp.wait()
            outs = [pltpu.make_async_copy(land.at[blk], w_hbm.at[:, pl.ds(blk * nb, nb)], out_sem.at[blk])
                    for blk in range(N_DEV)]
            for cp in outs:
                cp.start()
            for cp in outs:
                cp.wait()

    hbm = pl.BlockSpec(memory_space=pl.ANY)
    res = pl.pallas_call(
        body,
        name="in_proj",
        grid_spec=pltpu.PrefetchScalarGridSpec(
            num_scalar_prefetch=1,
            grid=(n_chips, n_t),
            in_specs=[
                pl.BlockSpec((tm, d), lambda j, i, order: (jnp.where(j == 0, i, n_t - 1), 0)),
                pl.BlockSpec((1, d), lambda j, i, order: (0, 0)),
                pl.BlockSpec((d, nb), lambda j, i, order: (0, 0)),
            ]
            + [hbm] * ns,
            out_specs=[pl.BlockSpec((tm, 2 * nb), lambda j, i, order: (i, order[j])), hbm] + [hbm] * ns,
            scratch_shapes=[
                pltpu.VMEM((tokens, d), BF16),
                pltpu.VMEM((N_DEV, d, nb), BF16),
                pltpu.SemaphoreType.DMA((7,)),
                pltpu.SemaphoreType.DMA((7,)),
                pltpu.SemaphoreType.DMA((N_DEV,)),
                pltpu.SemaphoreType.DMA((ns, 7)),
                pltpu.SemaphoreType.DMA((ns, 7)),
                pltpu.SemaphoreType.DMA((ns,)),
            ],
        ),
        out_shape=[
            jax.ShapeDtypeStruct((tokens, N_DEV * nb), F32),
            jax.ShapeDtypeStruct((d, N_DEV * nb), BF16),
        ]
        + [jax.ShapeDtypeStruct((N_DEV, *a.shape), a.dtype) for a in shards],
        compiler_params=_params("arbitrary", "arbitrary"),
    )(order, x2, g_pre, w_in_blk, *shards)
    return res[0], res[1], res[2:]


def _in_proj_bwd(x2, dres, du_pool, du_ssm, dq, dgate, g_pre, w_in_f):
    tokens, d = x2.shape
    nb = w_in_f.shape[1] // N_DEV
    pool_w, ssm_w, att_w, mix = du_pool.shape[1], du_ssm.shape[1], dq.shape[1], dgate.shape[1]
    cl = _token_tile(tokens, TOKEN_ROWS)
    n_tiles = tokens // cl

    def body(x_ref, dres_ref, dup_ref, dus_ref, dq_ref, dgate_ref, g_ref, w_ref, gx_ref, gw_hbm, gg_ref, acc, sem):
        i = pl.program_id(0)

        @pl.when(i == 0)
        def _():
            acc[...] = jnp.zeros_like(acc)
            gg_ref[...] = jnp.zeros_like(gg_ref)

        x = x_ref[...]
        r = _rms(x)
        xn = x * r
        g = g_ref[...]
        h = (xn * g).astype(BF16)
        dproj = jnp.concatenate([dup_ref[...], dus_ref[...], dq_ref[...], dgate_ref[...]], axis=1).astype(BF16)
        dh = _mm_nt(dproj, w_ref[...])
        for j in range(N_DEV):
            acc[j] += _mm_tn(h, dproj[:, j * nb : (j + 1) * nb])
        gg_ref[...] += jnp.sum(dh * xn, axis=0, keepdims=True)
        dxn = dh * g
        gx_ref[...] = dres_ref[...] + r * (dxn - xn * jnp.mean(dxn * xn, axis=-1, keepdims=True))

        @pl.when(i == n_tiles - 1)
        def _():
            cp = pltpu.make_async_copy(acc, gw_hbm, sem)
            cp.start()
            cp.wait()

    return pl.pallas_call(
        body,
        name="in_proj_bwd",
        grid=(n_tiles,),
        in_specs=[
            pl.BlockSpec((cl, d), lambda i: (i, 0)),
            pl.BlockSpec((cl, d), lambda i: (i, 0)),
            pl.BlockSpec((cl, pool_w), lambda i: (i, 0)),
            pl.BlockSpec((cl, ssm_w), lambda i: (i, 0)),
            pl.BlockSpec((cl, att_w), lambda i: (i, 0)),
            pl.BlockSpec((cl, mix), lambda i: (i, 0)),
            pl.BlockSpec((1, d), lambda i: (0, 0)),
            pl.BlockSpec(w_in_f.shape, lambda i: (0, 0)),
        ],
        out_specs=[
            pl.BlockSpec((cl, d), lambda i: (i, 0)),
            pl.BlockSpec(memory_space=pl.ANY),
            pl.BlockSpec((1, d), lambda i: (0, 0)),
        ],
        out_shape=[
            jax.ShapeDtypeStruct((tokens, d), F32),
            jax.ShapeDtypeStruct((N_DEV, d, nb), F32),
            jax.ShapeDtypeStruct((1, d), F32),
        ],
        scratch_shapes=[pltpu.VMEM((N_DEV, d, nb), F32), pltpu.SemaphoreType.DMA],
        compiler_params=_params("arbitrary"),
    )(x2, dres, du_pool, du_ssm, dq, dgate, g_pre, w_in_f)


def _pool_geometry(seq, width):
    gw = width // len(POOL_WINDOWS)
    col = lax.broadcasted_iota(jnp.int32, (1, width), 1)
    win = jnp.full((1, width), float(POOL_WINDOWS[-1]), F32)
    for gi in range(len(POOL_WINDOWS) - 2, -1, -1):
        win = jnp.where(col < (gi + 1) * gw, float(POOL_WINDOWS[gi]), win)
    row = lax.broadcasted_iota(jnp.int32, (seq, width), 0)
    filling = 1.0 / (lax.broadcasted_iota(jnp.int32, (seq, 1), 0) + 1).astype(F32)
    inv_cnt = jnp.where(row + 1 < win.astype(jnp.int32), filling, 1.0 / win)
    return win, row, inv_cnt


def _window_sums(a, win, seq, back):
    pad = 2 * POOL_WINDOWS[-1]
    zeros = jnp.zeros((pad, a.shape[1]), F32)
    s = jnp.concatenate([a, zeros] if back else [zeros, a], axis=0)
    sums = []
    k = 1
    while k < POOL_WINDOWS[-1]:
        s = s + pltpu.roll(s, seq + pad - k if back else k, 0)
        k *= 2
        sums.append((k, s))
    out = sums[-1][1]
    for k, s in reversed(sums[:-1]):
        out = jnp.where(win <= float(k), s, out)
    return out[0:seq] if back else out[pad : pad + seq]


def _pool_fwd(u2, wp_blk, scale, batch, seq):
    width = scale.shape[1]

    def body(u_ref, w_ref, s_ref, y_ref, diff_ref):
        u = u_ref[...]
        win, row, inv_cnt = _pool_geometry(seq, width)
        diff = (_window_sums(u, win, seq, False) * inv_cnt - u).astype(BF16)
        diff_ref[...] = diff
        y_ref[...] = _mm(diff, w_ref[...]) * s_ref[...]

    return pl.pallas_call(
        body,
        name="pool_fwd",
        grid=(batch,),
        in_specs=[
            pl.BlockSpec((seq, width), lambda b: (b, 0)),
            pl.BlockSpec((width, width), lambda b: (0, 0)),
            pl.BlockSpec((1, width), lambda b: (0, 0)),
        ],
        out_specs=[pl.BlockSpec((seq, width), lambda b: (b, 0)), pl.BlockSpec((seq, width), lambda b: (b, 0))],
        out_shape=[
            jax.ShapeDtypeStruct((u2.shape[0], width), F32),
            jax.ShapeDtypeStruct((u2.shape[0], width), BF16),
        ],
        compiler_params=_params("arbitrary"),
    )(u2, wp_blk, scale)


def _pool_bwd(diff2, dy2, wp_blk, scale, batch, seq):
    width = scale.shape[1]

    def body(diff_ref, dy_ref, w_ref, s_ref, du_ref, gw_ref, gs_ref):
        @pl.when(pl.program_id(0) == 0)
        def _():
            gw_ref[...] = jnp.zeros_like(gw_ref)
            gs_ref[...] = jnp.zeros_like(gs_ref)

        diff = diff_ref[...]
        dy = dy_ref[...]
        win, row, inv_cnt = _pool_geometry(seq, width)
        gs_ref[...] += jnp.sum(dy * _mm(diff, w_ref[...]), axis=0, keepdims=True)
        dys = (dy * s_ref[...]).astype(BF16)
        gw_ref[...] += _mm_tn(diff, dys)
        dd = _mm_nt(dys, w_ref[...])
        du_ref[...] = _window_sums(dd * inv_cnt, win, seq, True) - dd

    return pl.pallas_call(
        body,
        name="pool_bwd",
        grid=(batch,),
        in_specs=[
            pl.BlockSpec((seq, width), lambda b: (b, 0)),
            pl.BlockSpec((seq, width), lambda b: (b, 0)),
            pl.BlockSpec((width, width), lambda b: (0, 0)),
            pl.BlockSpec((1, width), lambda b: (0, 0)),
        ],
        out_specs=[
            pl.BlockSpec((seq, width), lambda b: (b, 0)),
            pl.BlockSpec((width, width), lambda b: (0, 0)),
            pl.BlockSpec((1, width), lambda b: (0, 0)),
        ],
        out_shape=[
            jax.ShapeDtypeStruct(dy2.shape, F32),
            jax.ShapeDtypeStruct((width, width), F32),
            jax.ShapeDtypeStruct((1, width), F32),
        ],
        compiler_params=_params("arbitrary"),
    )(diff2, dy2, wp_blk, scale)


def _state_row(z, n_blocks):
    re = jnp.real(z).reshape(n_blocks, -1)
    im = jnp.imag(z).reshape(n_blocks, -1)
    return jnp.concatenate([re, im], axis=1).reshape(1, -1)


def _ssm_tables(a_re, a_im, log_dt, b_re, b_im, c_re, c_im):
    groups, n_state = a_re.shape
    ch = b_re.shape[2]
    nb = groups * ch // LANES
    gl = groups // nb
    lam = lax.complex(a_re, a_im)
    lam_bar = jnp.exp(lam * jnp.exp(log_dt)[:, None])
    b_bar = ((lam_bar - 1.0) / lam)[..., None] * lax.complex(b_re, b_im)
    eye = jnp.eye(gl, dtype=F32)

    def rows_to_state(t):
        return jnp.einsum("sgnc,gh->sgchn", t.reshape(nb, gl, n_state, ch), eye).reshape(nb, gl * ch, gl * n_state)

    def state_to_rows(t):
        return jnp.einsum("sgcn,gh->shngc", t.reshape(nb, gl, ch, n_state), eye).reshape(nb, gl * n_state, gl * ch)

    b_tab = jnp.concatenate([rows_to_state(jnp.real(b_bar)), rows_to_state(jnp.imag(b_bar))], axis=2)
    c_tab = jnp.concatenate([state_to_rows(c_re), -state_to_rows(c_im)], axis=1)
    return _state_row(lam_bar, nb), b_tab, c_tab


def _lam_power(a_re, a_im, log_dt, power, scale, n_blocks):
    return _state_row(scale * jnp.exp(lax.complex(a_re, a_im) * jnp.exp(log_dt)[:, None] * power), n_blocks)


def _state_blocks(s2, n_blocks, width):
    half = s2 // n_blocks // 2
    assert half % width == 0
    return [(b * 2 * half + o, b * 2 * half + half + o) for b in range(n_blocks) for o in range(0, half, width)]


def _scan(src_ref, dst_ref, st_ref, lam8_ref, n_groups, s, n_blocks, reverse, store):
    lb = SCAN_LANES
    for re0, im0 in _state_blocks(2 * s, n_blocks, lb):
        cr, ci = pl.ds(re0, lb), pl.ds(im0, lb)
        lr = lam8_ref[:, cr]
        li = -lam8_ref[:, ci] if reverse else lam8_ref[:, ci]

        unroll = 1 if store else SCAN_UNROLL

        def step(i, carry, cr=cr, ci=ci, lr=lr, li=li):
            hr, hi = carry
            for k in range(unroll):
                grp = i * unroll + k
                grp = n_groups - 1 - grp if reverse else grp
                rows = pl.ds(pl.multiple_of(grp * SUBLANES, SUBLANES), SUBLANES)
                hr, hi = (lr * hr - li * hi + src_ref[rows, cr], lr * hi + li * hr + src_ref[rows, ci])
                if store:
                    dst_ref[rows, cr] = hr
                    dst_ref[rows, ci] = hi
            return hr, hi

        assert n_groups % unroll == 0
        hr, hi = lax.fori_loop(0, n_groups // unroll, step, (st_ref[:, cr], st_ref[:, ci]))
        st_ref[:, cr] = hr
        st_ref[:, ci] = hi


def _pack_state(re, im):
    hi = lax.bitcast_convert_type(re.astype(BF16).astype(F32), jnp.uint32)
    lo = lax.bitcast_convert_type(im.astype(BF16).astype(F32), jnp.uint32)
    return hi | (lo >> 16)


def _unpack_state(word):
    re = lax.bitcast_convert_type(word & jnp.uint32(0xFFFF0000), F32)
    im = lax.bitcast_convert_type(word << 16, F32)
    return re, im


def _lam_adjoint(g_ref, hprev_ref, group0, acc_ref, n_groups, s, n_blocks):
    lb = SCAN_LANES
    half = s // n_blocks
    assert n_groups % SCAN_UNROLL == 0
    for re0, im0 in _state_blocks(2 * s, n_blocks, lb):
        cr, ci = pl.ds(re0, lb), pl.ds(im0, lb)
        ch = pl.ds(re0 // (2 * half) * half + re0 % (2 * half), lb)

        def step(i, carry, cr=cr, ci=ci, ch=ch):
            ar, ai = carry
            for k in range(SCAN_UNROLL):
                grp = i * SCAN_UNROLL + k
                rows = pl.ds(pl.multiple_of(grp * SUBLANES, SUBLANES), SUBLANES)
                gr, gi = g_ref[rows, cr], g_ref[rows, ci]
                hr, hi = _unpack_state(hprev_ref[pl.ds(pl.multiple_of((group0 + grp) * SUBLANES, SUBLANES), SUBLANES), ch])
                ar = ar + hr * gr + hi * gi
                ai = ai + hr * gi - hi * gr
            return ar, ai

        ar, ai = lax.fori_loop(0, n_groups // SCAN_UNROLL, step, (acc_ref[:, cr], acc_ref[:, ci]))
        acc_ref[:, cr] = ar
        acc_ref[:, ci] = ai


def _chunk_starts(st_ref, init_ref, lcl_ref, s, n_blocks):
    w = s // n_blocks
    init_ref[0:1, :] = jnp.zeros((1, 2 * s), F32)
    for re0, im0 in _state_blocks(2 * s, n_blocks, w):
        re, im = pl.ds(re0, w), pl.ds(im0, w)
        ar, ai = lcl_ref[:, re], lcl_ref[:, im]
        cr = jnp.zeros((1, w), F32)
        ci = jnp.zeros((1, w), F32)
        for k in range(1, NCH):
            cr, ci = (ar * cr - ai * ci + st_ref[k - 1 : k, re], ar * ci + ai * cr + st_ref[k - 1 : k, im])
            init_ref[k : k + 1, re] = cr
            init_ref[k : k + 1, im] = ci


def _chunk_starts_adjoint(stg_ref, initg_ref, lcl_ref, s, n_blocks):
    w = s // n_blocks
    initg_ref[NCH - 1 : NCH, :] = jnp.zeros((1, 2 * s), F32)
    for re0, im0 in _state_blocks(2 * s, n_blocks, w):
        re, im = pl.ds(re0, w), pl.ds(im0, w)
        ar, ai = lcl_ref[:, re], -lcl_ref[:, im]
        gr = jnp.zeros((1, w), F32)
        gi = jnp.zeros((1, w), F32)
        for k in range(NCH - 2, -1, -1):
            gr, gi = (stg_ref[k + 1 : k + 2, re] + ar * gr - ai * gi, stg_ref[k + 1 : k + 2, im] + ar * gi + ai * gr)
            initg_ref[k : k + 1, re] = gr
            initg_ref[k : k + 1, im] = gi


def _ssm_rows(seq, want):
    rows = min(want, seq // 2)
    assert seq % rows == 0 and rows % SUBLANES == 0
    return rows


def _chunk_copies(hbm_ref, b, cm_ref, sems, to_cm, col0=0):
    cl, _, width = cm_ref.shape
    copies = []
    for k in range(NCH):
        nat, cm = hbm_ref.at[b, pl.ds(k * cl, cl), pl.ds(col0, width)], cm_ref.at[:, k, :]
        src, dst = (nat, cm) if to_cm else (cm, nat)
        copies.append(pltpu.make_async_copy(src, dst, sems.at[k]))
    return copies


def _blockwise(fn, n_blocks):
    return jnp.concatenate([fn(b) for b in range(n_blocks)], axis=1)


def _ssm_fwd(u, u_col, b_tab, c_tab, lam8, lcl, d_skip, w_glu, shards):
    batch, seq, _ = u.shape
    ns = len(shards)
    width = d_skip.shape[1]
    s = lam8.shape[1] // 2
    nb = b_tab.shape[0]
    sb = 2 * s // nb
    cl = seq // NCH
    rows = _ssm_rows(seq, SSM_ROWS)
    n_tiles = seq // rows
    n_groups = rows // SUBLANES

    def body(u_hbm, b_ref, c_ref, lam_ref, lcl_ref, d_ref, wg_ref, *rest):
        shard_hbm, (y_hbm, pre_ref, z_ref, init_ref) = rest[:ns], rest[ns : ns + 4]
        gathered = rest[ns + 4 : 2 * ns + 4]
        u_cm, y_cm, bu_all, st, sems, send_sems, recv_sems, own_sems = rest[2 * ns + 4 :]
        b, ph, t = pl.program_id(0), pl.program_id(1), pl.program_id(2)
        tile_groups = pl.ds(pl.multiple_of(t * n_groups, n_groups), n_groups)
        x_pos, y_pos, c_pos = _place()
        own = [pltpu.make_async_copy(shard_hbm[a], gathered[a].at[4 * x_pos + 2 * y_pos + c_pos], own_sems.at[a])
               for a in range(ns)]
        exchange_start, exchange_finish = _exchange_steps(
            ns, lambda a, slot: shard_hbm[a], gathered, send_sems, recv_sems)

        @pl.when((b == 0) & (ph == 0) & (t == 0))
        def _():
            exchange_start()
            for cp in own:
                cp.start()

        @pl.when((b == batch - 1) & (ph == 1) & (t == n_tiles - 1))
        def _():
            exchange_finish()
            for cp in own:
                cp.wait()

        @pl.when((ph == 0) & (t == 0))
        def _():
            loads = _chunk_copies(u_hbm, b, u_cm, sems, True, u_col)
            for cp in loads:
                cp.start()
            st[...] = jnp.zeros_like(st)
            for cp in loads:
                cp.wait()

        @pl.when((ph == 1) & (t == 0))
        def _():
            st[...] = init_ref[...]

        u_t = u_cm[tile_groups].reshape(rows, width)
        bu = bu_all.at[pl.ds(pl.multiple_of(t * rows, rows), rows)]

        @pl.when(ph == 0)
        def _():
            u_b = u_t.astype(BF16)
            for blk in range(nb):
                bu[:, blk * sb : (blk + 1) * sb] = _mm(u_b[:, blk * LANES : (blk + 1) * LANES], b_ref[blk])
            _scan(bu, bu, st, lam_ref, n_groups, s, nb, False, False)

        @pl.when((ph == 0) & (t == n_tiles - 1))
        def _():
            _chunk_starts(st, init_ref, lcl_ref, s, nb)

        @pl.when(ph == 1)
        def _():
            _scan(bu, bu, st, lam_ref, n_groups, s, nb, False, True)
            hs = lambda blk: _mm(bu[:, blk * sb : (blk + 1) * sb].astype(BF16), c_ref[blk])
            pre = _blockwise(hs, nb) + d_ref[...] * u_t
            z = _mm(_gelu(pre).astype(BF16), wg_ref[...])
            pre_ref[...] = pre
            z_ref[...] = z
            y = z[:, 0:width] * jax.nn.sigmoid(z[:, width : 2 * width])
            y_cm[tile_groups] = y.reshape(n_groups, SUBLANES, width)

        @pl.when((ph == 1) & (t == n_tiles - 1))
        def _():
            stores = _chunk_copies(y_hbm, b, y_cm, sems, False)
            for cp in stores:
                cp.start()
            for cp in stores:
                cp.wait()

    out_tile = lambda b, ph, t: (b, t * ph, 0)
    full = lambda a: pl.BlockSpec(a.shape, lambda b, ph, t: (0,) * a.ndim)
    hbm = pl.BlockSpec(memory_space=pl.ANY)
    res = pl.pallas_call(
        body,
        name="ssm_fwd",
        grid=(batch, 2, n_tiles),
        in_specs=[hbm, full(b_tab), full(c_tab), full(lam8), full(lcl), full(d_skip), full(w_glu)] + [hbm] * ns,
        out_specs=[
            hbm,
            pl.BlockSpec((None, rows, width), out_tile),
            pl.BlockSpec((None, rows, 2 * width), out_tile),
            pl.BlockSpec((None, SUBLANES, 2 * s), lambda b, ph, t: (b, 0, 0)),
        ]
        + [hbm] * ns,
        out_shape=[
            jax.ShapeDtypeStruct((batch, seq, width), F32),
            jax.ShapeDtypeStruct((batch, seq, width), F32),
            jax.ShapeDtypeStruct((batch, seq, 2 * width), F32),
            jax.ShapeDtypeStruct((batch, SUBLANES, 2 * s), F32),
        ]
        + [jax.ShapeDtypeStruct((N_DEV, *a.shape), a.dtype) for a in shards],
        scratch_shapes=[
            pltpu.VMEM((cl, NCH, width), F32),
            pltpu.VMEM((cl, NCH, width), F32),
            pltpu.VMEM((seq, 2 * s), F32),
            pltpu.VMEM((SUBLANES, 2 * s), F32),
            pltpu.SemaphoreType.DMA((NCH,)),
            pltpu.SemaphoreType.DMA((ns, 7)),
            pltpu.SemaphoreType.DMA((ns, 7)),
            pltpu.SemaphoreType.DMA((ns,)),
        ],
        compiler_params=_params("arbitrary", "arbitrary", "arbitrary"),
    )(u, b_tab, c_tab, lam8, lcl, d_skip, w_glu, *shards)
    return res[:4], res[4:]


def _ssm_bwd(u, u_col, pre_p, z_p, dy, init, b_tab, b_tab_t, c_tab_t, lam8, lcl, d_skip, w_glu, ready):
    batch, seq, _ = u.shape
    width = d_skip.shape[1]
    nr = len(ready)
    s = lam8.shape[1] // 2
    nb = b_tab.shape[0]
    sb = 2 * s // nb
    cl = seq // NCH
    rows = _ssm_rows(seq, SSM_ROWS)
    n_tiles = seq // rows
    n_groups = rows // SUBLANES

    def body(u_hbm, pre_ref, z_ref, dy_hbm, init_ref, b_ref, bt_ref, ct_ref, lam_ref, lcl_ref, d_ref, wg_ref, *rest):
        ready_hbm, rest = rest[:nr], rest[nr:]
        du_hbm, gb_ref, gc_ref, gwg_ref, gd_ref, glam_ref = rest[:6]
        landed_hbm, rest = rest[6 : 6 + nr], rest[6 + nr :]
        u_cm, dy_cm, work, hs_all, dpre_all, st, stg, initg, acc, sems, send_sems, recv_sems = rest
        b, ph, t = pl.program_id(0), pl.program_id(1), pl.program_id(2)
        half = s // nb
        exchange_start, exchange_finish = _exchange_steps(
            nr, lambda a, slot: ready_hbm[a].at[slot], landed_hbm, send_sems, recv_sems)
        first = (b == 0) & (ph == 0) & (t == 0)
        last = (b == batch - 1) & (ph == 2) & (t == n_tiles - 1)
        tile = jnp.where(ph == 0, t, n_tiles - 1 - t)
        tile_rows = pl.ds(pl.multiple_of(tile * rows, rows), rows)
        tile_groups = pl.ds(pl.multiple_of(tile * n_groups, n_groups), n_groups)
        lanes = lambda blk: slice(blk * LANES, (blk + 1) * LANES)
        states = lambda blk: slice(blk * sb, (blk + 1) * sb)

        @pl.when(first)
        def _():
            exchange_start()
            acc[...] = jnp.zeros_like(acc)
            gb_ref[...] = jnp.zeros_like(gb_ref)
            gc_ref[...] = jnp.zeros_like(gc_ref)
            gwg_ref[...] = jnp.zeros_like(gwg_ref)
            gd_ref[...] = jnp.zeros_like(gd_ref)

        @pl.when((ph == 0) & (t == 0))
        def _():
            loads = (_chunk_copies(u_hbm, b, u_cm, sems.at[0], True, u_col)
                     + _chunk_copies(dy_hbm, b, dy_cm, sems.at[1], True))
            for cp in loads:
                cp.start()
            st[...] = init_ref[...]
            for blk in range(nb):
                entry = init_ref[:, states(blk)]
                hs_all[0:SUBLANES, blk * half : (blk + 1) * half] = _pack_state(entry[:, 0:half], entry[:, half : 2 * half])
            for cp in loads:
                cp.wait()

        u_t = u_cm[tile_groups].reshape(rows, width)
        u_b = u_t.astype(BF16)

        @pl.when(ph == 0)
        def _():
            for blk in range(nb):
                work[:, states(blk)] = _mm(u_b[:, lanes(blk)], b_ref[blk])
            _scan(work, work, st, lam_ref, n_groups, s, nb, False, True)
            z = z_ref[...]
            dy_t = dy_cm[tile_groups].reshape(rows, width)
            pre = pre_ref[...]
            z1, sig = z[:, 0:width], jax.nn.sigmoid(z[:, width : 2 * width])
            dz = jnp.concatenate([dy_t * sig, dy_t * z1 * sig * (1.0 - sig)], axis=1).astype(BF16)
            gwg_ref[...] += _mm_tn(_gelu(pre).astype(BF16), dz)
            dpre = _mm_nt(dz, wg_ref[...]) * _gelu_grad(pre)
            dpre_all[tile_rows, :] = dpre
            gd_ref[...] += jnp.sum(dpre * u_t, axis=0, keepdims=True)
            dpre_b = dpre.astype(BF16)
            kept = pl.ds(pl.multiple_of(tile * rows + SUBLANES, SUBLANES), rows)
            for blk in range(nb):
                hs = work[:, states(blk)]
                gc_ref[blk] += _mm_tn(dpre_b[:, lanes(blk)], hs.astype(BF16))
                hs_all[kept, blk * half : (blk + 1) * half] = _pack_state(hs[:, 0:half], hs[:, half : 2 * half])

        @pl.when(ph >= 1)
        def _():
            dpre_b = dpre_all[tile_rows, :].astype(BF16)
            for blk in range(nb):
                work[:, states(blk)] = _mm(dpre_b[:, lanes(blk)], ct_ref[blk])

        @pl.when(ph == 1)
        def _():
            @pl.when(t == 0)
            def _():
                stg[...] = jnp.zeros_like(stg)

            _scan(work, work, stg, lam_ref, n_groups, s, nb, True, False)

            @pl.when(t == n_tiles - 1)
            def _():
                _chunk_starts_adjoint(stg, initg, lcl_ref, s, nb)

        @pl.when(ph == 2)
        def _():
            @pl.when(t == 0)
            def _():
                stg[...] = initg[...]

            _scan(work, work, stg, lam_ref, n_groups, s, nb, True, True)
            _lam_adjoint(work, hs_all, tile * n_groups, acc, n_groups, s, nb)
            du = lambda blk: _mm(work[:, states(blk)].astype(BF16), bt_ref[blk])
            du_t = _blockwise(du, nb) + dpre_all[tile_rows, :] * d_ref[...]
            dy_cm[tile_groups] = du_t.reshape(n_groups, SUBLANES, width)
            for blk in range(nb):
                gb_ref[blk] += _mm_tn(u_b[:, lanes(blk)], work[:, states(blk)].astype(BF16))

            @pl.when(t == n_tiles - 1)
            def _():
                stores = _chunk_copies(du_hbm, b, dy_cm, sems.at[0], False)
                for cp in stores:
                    cp.start()
                for cp in stores:
                    cp.wait()

        @pl.when(last)
        def _():
            glam_ref[...] = jnp.sum(acc[...], axis=0, keepdims=True)
            exchange_finish()

    def tile(b, ph, t):
        return (b, jnp.where(ph == 0, t, n_tiles - 1 - t), 0)

    full = lambda a: pl.BlockSpec(a.shape, lambda b, ph, t: (0,) * a.ndim)
    hbm = pl.BlockSpec(memory_space=pl.ANY)
    res = pl.pallas_call(
        body,
        name="ssm_bwd",
        grid=(batch, 3, n_tiles),
        in_specs=[
            hbm,
            pl.BlockSpec((None, rows, width), tile),
            pl.BlockSpec((None, rows, 2 * width), tile),
            hbm,
            pl.BlockSpec((None, SUBLANES, 2 * s), lambda b, ph, t: (b, 0, 0)),
            full(b_tab), full(b_tab_t), full(c_tab_t), full(lam8), full(lcl), full(d_skip), full(w_glu),
        ]
        + [hbm] * nr,
        out_specs=[
            hbm,
            full(b_tab), full(b_tab), full(w_glu), full(d_skip),
            pl.BlockSpec((1, 2 * s), lambda b, ph, t: (0, 0)),
        ]
        + [hbm] * nr,
        out_shape=[
            jax.ShapeDtypeStruct((batch, seq, width), F32),
            jax.ShapeDtypeStruct(b_tab.shape, F32),
            jax.ShapeDtypeStruct(b_tab.shape, F32),
            jax.ShapeDtypeStruct(w_glu.shape, F32),
            jax.ShapeDtypeStruct(d_skip.shape, F32),
            jax.ShapeDtypeStruct((1, 2 * s), F32),
        ]
        + [jax.ShapeDtypeStruct(a.shape, F32) for a in ready],
        scratch_shapes=[
            pltpu.VMEM((cl, NCH, width), F32),
            pltpu.VMEM((cl, NCH, width), F32),
            pltpu.VMEM((rows, 2 * s), F32),
            pltpu.VMEM((seq + SUBLANES, s), jnp.uint32),
            pltpu.VMEM((seq, width), F32),
        ]
        + [pltpu.VMEM((SUBLANES, 2 * s), F32)] * 4
        + [pltpu.SemaphoreType.DMA((2, NCH)), pltpu.SemaphoreType.DMA((nr, 7)), pltpu.SemaphoreType.DMA((nr, 7))],
        compiler_params=_params("arbitrary", "arbitrary", "arbitrary"),
    )(u, pre_p, z_p, dy, init, b_tab, b_tab_t, c_tab_t, lam8, lcl, d_skip, w_glu, *ready)
    return res[:6], res[6:]


def _kv_fwd(mem, g_mem, w_kv):
    batch, n_mem, d = mem.shape
    kvw = w_kv.shape[1]

    def body(mem_ref, g_ref, w_ref, kv_ref):
        m = mem_ref[...]
        kv_ref[...] = _mm((m * _rms(m) * g_ref[...]).astype(BF16), w_ref[...])

    return pl.pallas_call(
        body,
        name="kv_fwd",
        grid=(batch,),
        in_specs=[
            pl.BlockSpec((None, n_mem, d), lambda b: (b, 0, 0)),
            pl.BlockSpec((1, d), lambda b: (0, 0)),
            pl.BlockSpec((d, kvw), lambda b: (0, 0)),
        ],
        out_specs=pl.BlockSpec((None, n_mem, kvw), lambda b: (b, 0, 0)),
        out_shape=jax.ShapeDtypeStruct((batch, n_mem, kvw), F32),
        compiler_params=_params("arbitrary"),
    )(mem, g_mem, w_kv)


def _kv_bwd(mem, dkv, g_mem, w_kv):
    batch, n_mem, d = mem.shape
    kvw = w_kv.shape[1]

    def body(mem_ref, dkv_ref, g_ref, w_ref, gw_ref, gg_ref):
        @pl.when(pl.program_id(0) == 0)
        def _():
            gw_ref[...] = jnp.zeros_like(gw_ref)
            gg_ref[...] = jnp.zeros_like(gg_ref)

        m = mem_ref[...]
        mn = m * _rms(m)
        dkv_b = dkv_ref[...].astype(BF16)
        gw_ref[...] += _mm_tn((mn * g_ref[...]).astype(BF16), dkv_b)
        gg_ref[...] += jnp.sum(_mm_nt(dkv_b, w_ref[...]) * mn, axis=0, keepdims=True)

    return pl.pallas_call(
        body,
        name="kv_bwd",
        grid=(batch,),
        in_specs=[
            pl.BlockSpec((None, n_mem, d), lambda b: (b, 0, 0)),
            pl.BlockSpec((None, n_mem, kvw), lambda b: (b, 0, 0)),
            pl.BlockSpec((1, d), lambda b: (0, 0)),
            pl.BlockSpec((d, kvw), lambda b: (0, 0)),
        ],
        out_specs=[pl.BlockSpec((d, kvw), lambda b: (0, 0)), pl.BlockSpec((1, d), lambda b: (0, 0))],
        out_shape=[jax.ShapeDtypeStruct((d, kvw), F32), jax.ShapeDtypeStruct((1, d), F32)],
        compiler_params=_params("arbitrary"),
    )(mem, dkv, g_mem, w_kv)


def _tail(x2, target2, proj, y_pool, y_ssm, kv, w_out, g_post):
    tokens, d = x2.shape
    batch, n_mem, kvw = kv.shape
    pool_w, ssm_w, att_w, mix = y_pool.shape[1], y_ssm.shape[1], kvw // 2, w_out.shape[0]
    assert (mix - att_w) % att_w == 0 and proj.shape[1] == 2 * mix
    hd = att_w // MEM_HEADS
    cl = _token_tile(tokens // batch, TOKEN_ROWS)
    n_tiles = tokens // cl
    per_seq = tokens // batch // cl
    qk_scale = hd**-0.5

    def body(x_ref, tg_ref, gate_ref, yp_ref, ys_ref, q_ref, kv_ref, w_ref, g_ref,
             dres_ref, dgate_ref, dyp_ref, dys_ref, dq_ref, dkv_ref, gw_hbm, gg_ref, loss_ref, acc, sem):
        i = pl.program_id(0)

        @pl.when(i == 0)
        def _():
            acc[...] = jnp.zeros_like(acc)
            gg_ref[...] = jnp.zeros_like(gg_ref)
            loss_ref[...] = jnp.zeros_like(loss_ref)

        @pl.when(i % per_seq == 0)
        def _():
            dkv_ref[...] = jnp.zeros_like(dkv_ref)

        k = kv_ref[:, 0:att_w].astype(BF16)
        v = kv_ref[:, att_w : 2 * att_w].astype(BF16)
        lane = lax.broadcasted_iota(jnp.int32, (1, att_w), 1)
        heads = [(lane >= h * hd) & (lane < (h + 1) * hd) for h in range(MEM_HEADS)]
        g = g_ref[...]

        def part(rows):
            n_rows = rows.stop - rows.start
            q = q_ref[rows, :]
            probs, q_heads = [], []
            att = jnp.zeros((n_rows, att_w), F32)
            for mask in heads:
                qh = jnp.where(mask, q, 0.0).astype(BF16)
                sc = _mm_nt(qh, k) * qk_scale
                e = jnp.exp(sc - jnp.max(sc, axis=-1, keepdims=True))
                p = e * (1.0 / jnp.sum(e, axis=-1, keepdims=True))
                att = att + jnp.where(mask, _mm(p.astype(BF16), v), 0.0)
                probs.append(p)
                q_heads.append(qh)

            ycat = jnp.concatenate([yp_ref[rows, :], ys_ref[rows, :], att], axis=1)
            gate = gate_ref[rows, :]
            sig = jax.nn.sigmoid(gate)
            silu = gate * sig
            yg = (ycat * silu).astype(BF16)
            out = _mm(yg, w_ref[...])
            r = _rms(out)
            on = out * r
            err = x_ref[rows, :] + on * g - tg_ref[rows, :]
            loss_ref[...] += 0.5 * jnp.sum(jnp.mean(err * err, axis=-1, keepdims=True), axis=0, keepdims=True)
            dres = err * (1.0 / d)
            dres_ref[rows, :] = dres
            gg_ref[...] += jnp.sum(dres * on, axis=0, keepdims=True)
            don = dres * g
            dout = (r * (don - on * jnp.mean(don * on, axis=-1, keepdims=True))).astype(BF16)
            acc[...] += _mm_tn(yg, dout)
            dyg = _mm_nt(dout, w_ref[...])
            dgate_ref[rows, :] = dyg * ycat * (sig * (1.0 + gate * (1.0 - sig)))
            dycat = dyg * silu
            dyp_ref[rows, :] = dycat[:, 0:pool_w]
            dys_ref[rows, :] = dycat[:, pool_w : pool_w + ssm_w]
            datt = dycat[:, pool_w + ssm_w : mix]

            dq = jnp.zeros((n_rows, att_w), F32)
            dk = jnp.zeros((n_mem, att_w), F32)
            dv = jnp.zeros((n_mem, att_w), F32)
            for mask, p, qh in zip(heads, probs, q_heads):
                doh = jnp.where(mask, datt, 0.0).astype(BF16)
                dp = _mm_nt(doh, v)
                ds = (p * (dp - jnp.sum(p * dp, axis=-1, keepdims=True)) * qk_scale).astype(BF16)
                dq = dq + jnp.where(mask, _mm(ds, k), 0.0)
                dk = dk + _mm_tn(ds, qh)
                dv = dv + _mm_tn(p.astype(BF16), doh)
            dq_ref[rows, :] = dq
            dkv_ref[:, 0:att_w] += dk
            dkv_ref[:, att_w : 2 * att_w] += dv

        part(slice(0, cl))

        @pl.when(i == n_tiles - 1)
        def _():
            cp = pltpu.make_async_copy(acc, gw_hbm, sem)
            cp.start()
            cp.wait()

    tok = lambda w: pl.BlockSpec((cl, w), lambda i: (i, 0))
    chunked = tok(ssm_w)
    per_batch = pl.BlockSpec((None, n_mem, kvw), lambda i: (i // per_seq, 0, 0))
    return pl.pallas_call(
        body,
        name="tail",
        grid=(n_tiles,),
        in_specs=[
            tok(d), tok(d), pl.BlockSpec((cl, mix), lambda i: (i, 1)), tok(pool_w), chunked,
            pl.BlockSpec((cl, att_w), lambda i: (i, (mix - att_w) // att_w)), per_batch,
            pl.BlockSpec((mix, d), lambda i: (0, 0)),
            pl.BlockSpec((1, d), lambda i: (0, 0)),
        ],
        out_specs=[
            tok(d), tok(mix), tok(pool_w), chunked, tok(att_w), per_batch,
            pl.BlockSpec(memory_space=pl.ANY),
            pl.BlockSpec((1, d), lambda i: (0, 0)),
            pl.BlockSpec((1, 1), lambda i: (0, 0)),
        ],
        out_shape=[
            jax.ShapeDtypeStruct((tokens, d), F32),
            jax.ShapeDtypeStruct((tokens, mix), F32),
            jax.ShapeDtypeStruct((tokens, pool_w), F32),
            jax.ShapeDtypeStruct((tokens, ssm_w), F32),
            jax.ShapeDtypeStruct((tokens, att_w), F32),
            jax.ShapeDtypeStruct(kv.shape, F32),
            jax.ShapeDtypeStruct((mix, d), F32),
            jax.ShapeDtypeStruct((1, d), F32),
            jax.ShapeDtypeStruct((1, 1), F32),
        ],
        scratch_shapes=[pltpu.VMEM((mix, d), F32), pltpu.SemaphoreType.DMA],
        compiler_params=_params("arbitrary"),
    )(x2, target2, proj, y_pool, y_ssm, proj, kv, w_out, g_post)


def _pack(arrays):
    flat = jnp.concatenate([a.reshape(-1) for a in arrays])
    rows = -(-flat.size // (N_DEV * SUBLANES * LANES)) * N_DEV * SUBLANES
    return jnp.pad(flat, (0, rows * LANES - flat.size)).reshape(rows, LANES)


def _unpack(packed, like):
    flat, out, at = packed.reshape(-1), [], 0
    for a in like:
        out.append(flat[at : at + a.size].reshape(a.shape))
        at += a.size
    return out


def kernel(x, mem, g_pre, w_in, w_pool, pool_scale, a_re, a_im, log_dt, b_re, b_im, c_re, c_im, d_skip, w_glu, g_mem, w_kv, w_out, g_post, loss_target, m_g_pre, m_w_in, m_w_pool, m_pool_scale, m_a_re, m_a_im, m_log_dt, m_b_re, m_b_im, m_c_re, m_c_im, m_d_skip, m_w_glu, m_g_mem, m_w_kv, m_w_out, m_g_post, v_g_pre, v_w_in, v_w_pool, v_pool_scale, v_a_re, v_a_im, v_log_dt, v_b_re, v_b_im, v_c_re, v_c_im, v_d_skip, v_w_glu, v_g_mem, v_w_kv, v_w_out, v_g_post):
    batch, seq, d = x.shape
    cl = seq // NCH
    pool_w, ssm_w = pool_scale.shape[1], d_skip.shape[1]
    att_w = w_kv.shape[2] // 2
    tokens = batch * seq
    x2 = x.reshape(tokens, d)
    target2 = loss_target.reshape(tokens, d)

    wp_blk = jax.scipy.linalg.block_diag(*w_pool[0]).astype(BF16)
    ssm_params = (a_re[0], a_im[0], log_dt[0], b_re[0], b_im[0], c_re[0], c_im[0])
    (lam_row, b_tab, c_tab), tables_vjp = jax.vjp(_ssm_tables, *ssm_params)
    nb = b_tab.shape[0]
    lam8 = jnp.broadcast_to(lam_row, (SUBLANES, lam_row.shape[1]))
    lcl = _lam_power(a_re[0], a_im[0], log_dt[0], float(cl), 1.0, nb)
    b_bf, c_bf = b_tab.astype(BF16), c_tab.astype(BF16)

    proj, w_in_f, (w_glu_g,) = _in_proj(x2, g_pre, w_in[0], [w_glu[0].astype(BF16)])
    proj3 = proj.reshape(batch, seq, proj.shape[1])
    w_glu_f = w_glu_g.transpose(1, 0, 2).reshape(w_glu_g.shape[1], N_DEV * w_glu_g.shape[2])
    y_pool, diff_pool = _pool_fwd(proj, wp_blk, pool_scale, batch, seq)
    (y_ssm, pre_ssm, z_ssm, init_ssm), (w_out_g, w_kv_g) = _ssm_fwd(
        proj3, pool_w, b_bf, c_bf, lam8, lcl, d_skip, w_glu_f, [w_out[0].astype(BF16), w_kv[0].astype(BF16)])
    w_out_f = w_out_g.reshape(N_DEV * w_out_g.shape[1], w_out_g.shape[2])
    w_kv_f = w_kv_g.reshape(N_DEV * w_kv_g.shape[1], w_kv_g.shape[2])
    kv = _kv_fwd(mem, g_mem, w_kv_f)

    dres, dgate, dy_pool, dy_ssm, dq, dkv, gw_out, gg_post, loss_part = _tail(
        x2, target2, proj, y_pool, y_ssm.reshape(tokens, ssm_w), kv, w_out_f, g_post)

    gw_kv, gg_mem = _kv_bwd(mem, dkv, g_mem, w_kv_f)
    du_pool, gwp_dense, g_scale = _pool_bwd(diff_pool, dy_pool, wp_blk, pool_scale, batch, seq)
    gw_kv8 = gw_kv.reshape(N_DEV, -1, gw_kv.shape[1])
    gw_out8 = gw_out.reshape(N_DEV, -1, gw_out.shape[1])
    (du_ssm, gb_tab, gc_tab_t, gw_glu, gd_skip, glam), (kv_landed, out_landed) = _ssm_bwd(
        proj3, pool_w, pre_ssm, z_ssm, dy_ssm.reshape(batch, seq, ssm_w), init_ssm, b_bf, b_bf.transpose(0, 2, 1),
        c_bf.transpose(0, 2, 1), lam8, lcl, d_skip, w_glu_f, [gw_kv8, gw_out8])
    grad_x2, gw_in, gg_pre = _in_proj_bwd(
        x2, dres, du_pool, du_ssm.reshape(tokens, ssm_w), dq, dgate, g_pre,
        w_in_f)

    gw = pool_w // len(POOL_WINDOWS)
    gw_pool = jnp.stack([gwp_dense[i * gw : (i + 1) * gw, i * gw : (i + 1) * gw] for i in range(len(POOL_WINDOWS))])
    g_ssm = tables_vjp((glam, gb_tab, gc_tab_t.transpose(0, 2, 1)))

    small_w = [g_pre, w_pool, pool_scale, a_re, a_im, log_dt, b_re, b_im, c_re, c_im, d_skip, g_mem, g_post]
    small_m = [m_g_pre, m_w_pool, m_pool_scale, m_a_re, m_a_im, m_log_dt, m_b_re, m_b_im, m_c_re, m_c_im, m_d_skip, m_g_mem, m_g_post]
    small_v = [v_g_pre, v_w_pool, v_pool_scale, v_a_re, v_a_im, v_log_dt, v_b_re, v_b_im, v_c_re, v_c_im, v_d_skip, v_g_mem, v_g_post]
    small_g = [gg_pre, gw_pool, g_scale, *g_ssm, gd_skip, gg_mem, gg_post]
    big_g, small_sum = _reduce_all(
        [gw_in, gw_glu.reshape(ssm_w, N_DEV, -1).transpose(1, 0, 2)],
        _pack(small_g + [loss_part]),
        [(gw_kv8, kv_landed), (gw_out8, out_landed)])

    flat2 = lambda a: a.reshape(-1, a.shape[-1])
    sg = _unpack(small_sum, [flat2(a) for a in small_w] + [loss_part])
    loss = sg[-1].reshape(())
    small_names = ["g_pre", "w_pool", "pool_scale", "a_re", "a_im", "log_dt", "b_re", "b_im", "c_re", "c_im",
                   "d_skip", "g_mem", "g_post"]
    names = ["w_in", "w_glu", "w_kv", "w_out"] + small_names
    all_w = [w_in, w_glu, w_kv, w_out] + small_w
    all_m = [m_w_in, m_w_glu, m_w_kv, m_w_out] + small_m
    all_v = [v_w_in, v_w_glu, v_w_kv, v_w_out] + small_v
    updates = _adamw_all(big_g + sg[:-1], [flat2(a) for a in all_w], [flat2(a) for a in all_m], [flat2(a) for a in all_v])
    updates = {name: [t.reshape(a.shape) for t in u] for name, u, a in zip(names, updates, all_w)}

    order = ["g_pre", "w_in", "w_pool", "pool_scale", "a_re", "a_im", "log_dt", "b_re", "b_im", "c_re", "c_im",
             "d_skip", "w_glu", "g_mem", "w_kv", "w_out", "g_post"]
    outs = [[updates[name][kind] for name in order] for kind in range(4)]
    return (loss, grad_x2.reshape(batch, seq, d), *outs[0], *outs[1], *outs[2], *outs[3])
```

```python
import math

import jax
import jax.numpy as jnp
from jax import lax
from jax.experimental import pallas as pl
from jax.experimental.pallas import tpu as pltpu

F32 = jnp.float32
BF16 = jnp.bfloat16
MESH = pl.DeviceIdType.MESH

N_DEV = 8
SUBLANES = 8
LANES = 128
NCH = SUBLANES
VMEM_LIMIT = 60 * 1024 * 1024

IN_PROJ_ROWS = 2048
TOKEN_ROWS = 512
SSM_ROWS = 1024
SCAN_LANES = 512
SCAN_UNROLL = 8

EPS = 1e-6
POOL_WINDOWS = (2, 4, 8, 16)
MEM_HEADS = 4
ADAM_LR, ADAM_B1, ADAM_B2, ADAM_EPS, ADAM_WD, ADAM_STEP = 0.001, 0.9, 0.999, 1e-08, 0.01, 10


def _mm(a, b):
    return jnp.dot(a, b, preferred_element_type=F32)


def _mm_nt(a, b):
    return lax.dot_general(a, b, (((1,), (1,)), ((), ())), preferred_element_type=F32)


def _mm_tn(a, b):
    return lax.dot_general(a, b, (((0,), (0,)), ((), ())), preferred_element_type=F32)


def _params(*sem):
    return pltpu.CompilerParams(dimension_semantics=sem or None, vmem_limit_bytes=VMEM_LIMIT)


def _adamw(w, g, m, v):
    m = ADAM_B1 * m + (1.0 - ADAM_B1) * g
    v = ADAM_B2 * v + (1.0 - ADAM_B2) * (g * g)
    m_hat = m / (1.0 - ADAM_B1**ADAM_STEP)
    v_hat = v / (1.0 - ADAM_B2**ADAM_STEP)
    delta = -ADAM_LR * (m_hat / (jnp.sqrt(v_hat) + ADAM_EPS) + ADAM_WD * w)
    return delta, m, v


def _gelu(x):
    k = math.sqrt(2.0 / math.pi)
    return 0.5 * x * (1.0 + jnp.tanh(k * (x + 0.044715 * x * x * x)))


def _gelu_grad(x):
    k = math.sqrt(2.0 / math.pi)
    th = jnp.tanh(k * (x + 0.044715 * x * x * x))
    return 0.5 * (1.0 + th) + 0.5 * x * (1.0 - th * th) * (k * (1.0 + 3.0 * 0.044715 * x * x))


def _place():
    return lax.axis_index("x"), lax.axis_index("y"), lax.axis_index("c")


def _exchange_steps(n, src_of, landing, send_sems, recv_sems):
    x, y, c = _place()
    me = 4 * x + 2 * y + c
    peers = []
    for j in range(1, N_DEV):
        px = 1 - x if j & 4 else x
        py = 1 - y if j & 2 else y
        pc = 1 - c if j & 1 else c
        peers.append((px, py, pc))

    def copy(a, j, from_slot, to_slot, peer):
        return pltpu.make_async_remote_copy(
            src_ref=src_of(a, to_slot), dst_ref=landing[a].at[from_slot],
            send_sem=send_sems.at[a, j], recv_sem=recv_sems.at[a, j], device_id=peer, device_id_type=MESH)

    def start():
        for a in range(n):
            for j, p in enumerate(peers):
                copy(a, j, me, 4 * p[0] + 2 * p[1] + p[2], p).start()

    def finish():
        for a in range(n):
            for j, p in enumerate(peers):
                slot = 4 * p[0] + 2 * p[1] + p[2]
                copy(a, j, slot, slot, p).wait_recv()
        for a in range(n):
            for j, p in enumerate(peers):
                copy(a, j, me, 4 * p[0] + 2 * p[1] + p[2], p).wait_send()

    return start, finish


def _adamw_all(gs, ws, ms, vs):
    n = len(gs)

    def body(*refs):
        g, w, m, v = refs[:n], refs[n : 2 * n], refs[2 * n : 3 * n], refs[3 * n : 4 * n]
        outs = refs[4 * n :]
        for a in range(n):
            rows = g[a].shape[0]
            chunk = math.gcd(rows, 128)

            def step(i, _, a=a, chunk=chunk):
                r = pl.ds(pl.multiple_of(i * chunk, chunk), chunk)
                grad = g[a][r, :]
                delta, nm, nv = _adamw(w[a][r, :], grad, m[a][r, :], v[a][r, :])
                outs[4 * a][r, :] = grad
                outs[4 * a + 1][r, :] = delta
                outs[4 * a + 2][r, :] = nm
                outs[4 * a + 3][r, :] = nv
                return 0

            lax.fori_loop(0, rows // chunk, step, 0)

    vmem = pl.BlockSpec(memory_space=pltpu.VMEM)
    out_shape = []
    for wa in ws:
        out_shape += [jax.ShapeDtypeStruct(wa.shape, F32)] * 4
    res = pl.pallas_call(
        body,
        name="adamw_all",
        out_shape=out_shape,
        in_specs=[vmem] * (4 * n),
        out_specs=[vmem] * (4 * n),
        compiler_params=_params(),
    )(*gs, *ws, *ms, *vs)
    return [tuple(res[4 * a : 4 * a + 4]) for a in range(n)]


def _reduce_all(parts, small, early):
    n, ne = len(parts), len(early)
    parts4 = [p.reshape(4, 2, *p.shape[1:]) for p in parts]
    blks = [p.shape[1:] for p in parts]

    def body(*refs):
        refs = list(refs)
        take = lambda k: [refs.pop(0) for _ in range(k)]
        part = take(n)
        (small_ref,) = take(1)
        early_in = [take(2) for _ in range(ne)]
        outs = take(n)
        (small_all,) = take(1)
        early_out = take(ne)
        own, r1, got_a1, got_a2, got_b1, got_b2, pass_a, pass_b = (take(n) for _ in range(8))
        early_buf = take(ne)
        small_land, small_mine = take(2)
        s1_send, s1_recv, h_send, h_recv, loc, rs_send, rs_recv, ag_send, ag_recv, early_sems = refs
        x, y, c = _place()
        me = 4 * x + 2 * y + c
        piece = small_mine.shape[0]
        piece_of = lambda slot: pl.ds(pl.multiple_of(slot * piece, SUBLANES), piece)
        rs_start, rs_finish = _exchange_steps(
            1, lambda a, slot: small_ref.at[piece_of(slot)], [small_land], rs_send, rs_recv)
        ag_start, ag_finish = _exchange_steps(1, lambda a, slot: small_mine, [small_all], ag_send, ag_recv)
        landed = [pltpu.make_async_copy(early_in[e][1], early_buf[e], early_sems.at[e, 0]) for e in range(ne)]
        for cp in landed:
            cp.start()
        sibling = (x, y, 1 - c)
        chips = [(1 - x, y), (x, 1 - y), (1 - x, 1 - y)]

        def rowwise(rows, fn):
            chunk = math.gcd(rows, 128)

            def step(i, _):
                fn(pl.ds(pl.multiple_of(i * chunk, chunk), chunk))
                return 0

            lax.fori_loop(0, rows // chunk, step, 0)

        stage1, local = [], []
        for a in range(n):
            cp = pltpu.make_async_remote_copy(
                src_ref=part[a].at[:, 1 - c], dst_ref=r1[a], send_sem=s1_send.at[a], recv_sem=s1_recv.at[a],
                device_id=sibling, device_id_type=MESH)
            cp.start()
            stage1.append(cp)
            lc = pltpu.make_async_copy(part[a].at[:, c], own[a], loc.at[a])
            lc.start()
            local.append(lc)
        rs_start()
        x_nbr, y_nbr = (1 - x, y, c), (x, 1 - y, c)
        mine, mine_x, mine_y = 2 * x + y, 2 * (1 - x) + y, 2 * x + (1 - y)

        def hop(a, k, src, dst, to):
            return pltpu.make_async_remote_copy(
                src_ref=src, dst_ref=dst, send_sem=h_send.at[a, k], recv_sem=h_recv.at[a, k],
                device_id=to, device_id_type=MESH)

        first, second = [], []
        for a in range(n):
            half = blks[a][0] // 2
            up, low = pl.ds(0, half), pl.ds(half, half)
            local[a].wait()
            stage1[a].wait_recv()
            for chip in range(4):

                def add(r, a=a, chip=chip):
                    own[a][chip, r, :] = own[a][chip, r, :] + r1[a][chip, r, :]

                rowwise(blks[a][0], add)
            first.append([
                hop(a, 0, own[a].at[pl.ds(2 * (1 - x), 2), up], got_a1[a], x_nbr),
                hop(a, 2, own[a].at[2 * x + (1 - y), low], got_b1[a].at[x], y_nbr),
                hop(a, 3, own[a].at[2 * (1 - x) + (1 - y), low], got_b1[a].at[1 - x], y_nbr),
            ])
            for cp in first[a]:
                cp.start()
        rs_finish()
        small_land[me] = small_ref[piece_of(me), :]

        def sum_piece(i, _):
            r = pl.ds(pl.multiple_of(i * SUBLANES, SUBLANES), SUBLANES)
            total = small_land[0, r, :]
            for dev in range(1, N_DEV):
                total = total + small_land[dev, r, :]
            small_mine[r, :] = total
            small_all[me, r, :] = total
            return 0

        lax.fori_loop(0, piece // SUBLANES, sum_piece, 0)
        ag_start()
        for a in range(n):
            half = blks[a][0] // 2
            first[a][0].wait_recv()

            def fold_upper(r, a=a):
                own[a][mine, r, :] = own[a][mine, r, :] + got_a1[a][y, r, :]
                pass_a[a][r, :] = own[a][mine_y, r, :] + got_a1[a][1 - y, r, :]

            rowwise(half, fold_upper)
            first[a][1].wait_recv()
            first[a][2].wait_recv()

            def fold_lower(r, a=a, half=half):
                rl = pl.ds(pl.multiple_of(r.start + half, SUBLANES), r.size)
                own[a][mine, rl, :] = own[a][mine, rl, :] + got_b1[a][x, r, :]
                pass_b[a][r, :] = own[a][mine_x, rl, :] + got_b1[a][1 - x, r, :]

            rowwise(half, fold_lower)
            second.append([hop(a, 1, pass_a[a], got_a2[a], y_nbr), hop(a, 4, pass_b[a], got_b2[a], x_nbr)])
            for cp in second[a]:
                cp.start()
        for e in range(ne):
            part_e, _ = early_in[e]
            landed[e].wait()
            own_block = pltpu.make_async_copy(part_e.at[me], early_buf[e].at[me], early_sems.at[e, 1])
            own_block.start()
            own_block.wait()

            def sum_early(r, e=e):
                g = early_buf[e][0, r, :]
                for dev in range(1, N_DEV):
                    g = g + early_buf[e][dev, r, :]
                early_out[e][r, :] = g

            rowwise(early_buf[e].shape[1], sum_early)
        for a in range(n):
            half = blks[a][0] // 2
            second[a][0].wait_recv()
            second[a][1].wait_recv()

            def finish_rows(r, a=a, half=half):
                rl = pl.ds(pl.multiple_of(r.start + half, SUBLANES), r.size)
                outs[a][r, :] = own[a][mine, r, :] + got_a2[a][r, :]
                outs[a][rl, :] = own[a][mine, rl, :] + got_b2[a][r, :]

            rowwise(half, finish_rows)
        ag_finish()
        for cp in stage1 + [cp for group in first + second for cp in group]:
            cp.wait_send()

    vmem = pl.BlockSpec(memory_space=pltpu.VMEM)
    hbm = pl.BlockSpec(memory_space=pl.ANY)
    piece = small.shape[0] // N_DEV
    assert piece * N_DEV == small.shape[0] and piece % SUBLANES == 0
    out_shape = [jax.ShapeDtypeStruct(b, F32) for b in blks]
    out_shape += [jax.ShapeDtypeStruct((N_DEV, piece, LANES), F32)]
    out_shape += [jax.ShapeDtypeStruct(e[0].shape[1:], F32) for e in early]
    halves = [(b[0] // 2, b[1]) for b in blks]
    scratch = (
        [pltpu.VMEM((4, *b), F32) for b in blks]
        + [pltpu.VMEM((4, *b), F32) for b in blks]
        + [pltpu.VMEM((2, *h), F32) for h in halves]
        + [pltpu.VMEM(h, F32) for h in halves]
        + [pltpu.VMEM((2, *h), F32) for h in halves]
        + [pltpu.VMEM(h, F32) for h in halves] * 3
        + [pltpu.VMEM(e[0].shape, F32) for e in early]
        + [pltpu.VMEM((N_DEV, piece, LANES), F32), pltpu.VMEM((piece, LANES), F32)]
        + [pltpu.SemaphoreType.DMA((n,)), pltpu.SemaphoreType.DMA((n,)), pltpu.SemaphoreType.DMA((n, 5)),
           pltpu.SemaphoreType.DMA((n, 5)), pltpu.SemaphoreType.DMA((n,))]
        + [pltpu.SemaphoreType.DMA((1, 7))] * 4
        + [pltpu.SemaphoreType.DMA((ne, 2))]
    )
    res = pl.pallas_call(
        body,
        name="reduce_all",
        out_shape=out_shape,
        in_specs=[hbm] * n + [vmem] + [hbm, hbm] * ne,
        out_specs=[vmem] * (n + 1 + ne),
        scratch_shapes=scratch,
        compiler_params=_params(),
    )(*parts4, small, *[t for e in early for t in e])
    return list(res[:n]) + list(res[n + 1 :]), res[n].reshape(small.shape)


def _rms(x):
    return lax.rsqrt(jnp.mean(x * x, axis=-1, keepdims=True) + EPS)


def _token_tile(tokens, want):
    tile = min(want, tokens // 2)
    assert tokens % tile == 0 and tile % 16 == 0
    return tile


def _in_proj(x2, g_pre, w_in_blk, shards):
    tokens, d = x2.shape
    nb = w_in_blk.shape[1]
    tm = _token_tile(tokens, IN_PROJ_ROWS)
    n_t = tokens // tm
    ns = len(shards)
    x_pos, y_pos, c_pos = _place()
    slot = lambda px, py, pc: 4 * px + 2 * py + pc
    chip_order = [(x_pos, y_pos), (1 - x_pos, y_pos), (x_pos, 1 - y_pos), (1 - x_pos, 1 - y_pos)]
    order = jnp.stack([2 * px + py for px, py in chip_order]).astype(jnp.int32)
    n_chips = len(chip_order)

    def body(order_ref, x_ref, g_ref, w_ref, *rest):
        shard_hbm, proj_ref, w_hbm = rest[:ns], rest[ns], rest[ns + 1]
        gathered = rest[ns + 2 : 2 * ns + 2]
        h_all, land, w_send, w_recv, out_sem, send_sems, recv_sems, own_sems = rest[2 * ns + 2 :]
        j, i = pl.program_id(0), pl.program_id(1)
        x, y, c = _place()
        me, sibling = (x, y, c), (x, y, 1 - c)
        chips = [(1 - x, y), (x, 1 - y), (1 - x, 1 - y)]
        own = [pltpu.make_async_copy(shard_hbm[a], gathered[a].at[slot(*me)], own_sems.at[a]) for a in range(ns)]
        start, finish = _exchange_steps(ns, lambda a, s: shard_hbm[a], gathered, send_sems, recv_sems)

        def copy(k, block, to):
            ref = land.at[slot(*block)]
            return pltpu.make_async_remote_copy(
                src_ref=ref, dst_ref=ref, send_sem=w_send.at[k], recv_sem=w_recv.at[k], device_id=to, device_id_type=MESH)

        first_sends = [copy(0, me, sibling)] + [copy(1 + k, me, (*chip, c)) for k, chip in enumerate(chips)]
        forwards = [copy(4 + k, (*chip, c), sibling) for k, chip in enumerate(chips)]

        @pl.when((j == 0) & (i == 0))
        def _():
            land[slot(*me)] = w_ref[...].astype(BF16)
            for cp in first_sends:
                cp.start()
            start()
            for cp in own:
                cp.start()
            copy(0, sibling, me).wait_recv()

        for k, chip in enumerate(chips):

            @pl.when((j == k) & (i == n_t - 1))
            def _(k=k, chip=chip):
                copy(1 + k, (*chip, c), me).wait_recv()
                forwards[k].start()

            @pl.when((j == k + 1) & (i == 0))
            def _(k=k, chip=chip):
                copy(4 + k, (*chip, 1 - c), me).wait_recv()

        rows = pl.ds(pl.multiple_of(i * tm, tm), tm)

        @pl.when(j == 0)
        def _():
            x_t = x_ref[...]
            h_all[rows, :] = (x_t * _rms(x_t) * g_ref[...]).astype(BF16)

        h = h_all[rows, :]
        for half in range(2):
            proj_ref[:, half * nb : (half + 1) * nb] = _mm(h, land[2 * order_ref[j] + half])

        @pl.when((j == n_chips - 1) & (i == n_t - 1))
        def _():
            for cp in first_sends + forwards:
                cp.wait_send()
            finish()
            for cp in own:
                cp.wait()
            outs = [pltpu.make_async_copy(land.at[blk], w_hbm.at[:, pl.ds(blk * nb, nb)], out_sem.at[blk])
                    for blk in range(N_DEV)]
            for cp in outs:
                cp.start()
            for cp in outs:
                cp.wait()

    hbm = pl.BlockSpec(memory_space=pl.ANY)
    res = pl.pallas_call(
        body,
        name="in_proj",
        grid_spec=pltpu.PrefetchScalarGridSpec(
            num_scalar_prefetch=1,
            grid=(n_chips, n_t),
            in_specs=[
                pl.BlockSpec((tm, d), lambda j, i, order: (jnp.where(j == 0, i, n_t - 1), 0)),
                pl.BlockSpec((1, d), lambda j, i, order: (0, 0)),
                pl.BlockSpec((d, nb), lambda j, i, order: (0, 0)),
            ]
            + [hbm] * ns,
            out_specs=[pl.BlockSpec((tm, 2 * nb), lambda j, i, order: (i, order[j])), hbm] + [hbm] * ns,
            scratch_shapes=[
                pltpu.VMEM((tokens, d), BF16),
                pltpu.VMEM((N_DEV, d, nb), BF16),
                pltpu.SemaphoreType.DMA((7,)),
                pltpu.SemaphoreType.DMA((7,)),
                pltpu.SemaphoreType.DMA((N_DEV,)),
                pltpu.SemaphoreType.DMA((ns, 7)),
                pltpu.SemaphoreType.DMA((ns, 7)),
                pltpu.SemaphoreType.DMA((ns,)),
            ],
        ),
        out_shape=[
            jax.ShapeDtypeStruct((tokens, N_DEV * nb), F32),
            jax.ShapeDtypeStruct((d, N_DEV * nb), BF16),
        ]
        + [jax.ShapeDtypeStruct((N_DEV, *a.shape), a.dtype) for a in shards],
        compiler_params=_params("arbitrary", "arbitrary"),
    )(order, x2, g_pre, w_in_blk, *shards)
    return res[0], res[1], res[2:]


def _in_proj_bwd(x2, dres, du_pool, du_ssm, dq, dgate, g_pre, w_in_f):
    tokens, d = x2.shape
    nb = w_in_f.shape[1] // N_DEV
    pool_w, ssm_w, att_w, mix = du_pool.shape[1], du_ssm.shape[1], dq.shape[1], dgate.shape[1]
    cl = _token_tile(tokens, TOKEN_ROWS)
    n_tiles = tokens // cl

    def body(x_ref, dres_ref, dup_ref, dus_ref, dq_ref, dgate_ref, g_ref, w_ref, gx_ref, gw_hbm, gg_ref, acc, sem):
        i = pl.program_id(0)

        @pl.when(i == 0)
        def _():
            acc[...] = jnp.zeros_like(acc)
            gg_ref[...] = jnp.zeros_like(gg_ref)

        x = x_ref[...]
        r = _rms(x)
        xn = x * r
        g = g_ref[...]
        h = (xn * g).astype(BF16)
        dproj = jnp.concatenate([dup_ref[...], dus_ref[...], dq_ref[...], dgate_ref[...]], axis=1).astype(BF16)
        dh = _mm_nt(dproj, w_ref[...])
        for j in range(N_DEV):
            acc[j] += _mm_tn(h, dproj[:, j * nb : (j + 1) * nb])
        gg_ref[...] += jnp.sum(dh * xn, axis=0, keepdims=True)
        dxn = dh * g
        gx_ref[...] = dres_ref[...] + r * (dxn - xn * jnp.mean(dxn * xn, axis=-1, keepdims=True))

        @pl.when(i == n_tiles - 1)
        def _():
            cp = pltpu.make_async_copy(acc, gw_hbm, sem)
            cp.start()
            cp.wait()

    return pl.pallas_call(
        body,
        name="in_proj_bwd",
        grid=(n_tiles,),
        in_specs=[
            pl.BlockSpec((cl, d), lambda i: (i, 0)),
            pl.BlockSpec((cl, d), lambda i: (i, 0)),
            pl.BlockSpec((cl, pool_w), lambda i: (i, 0)),
            pl.BlockSpec((cl, ssm_w), lambda i: (i, 0)),
            pl.BlockSpec((cl, att_w), lambda i: (i, 0)),
            pl.BlockSpec((cl, mix), lambda i: (i, 0)),
            pl.BlockSpec((1, d), lambda i: (0, 0)),
            pl.BlockSpec(w_in_f.shape, lambda i: (0, 0)),
        ],
        out_specs=[
            pl.BlockSpec((cl, d), lambda i: (i, 0)),
            pl.BlockSpec(memory_space=pl.ANY),
            pl.BlockSpec((1, d), lambda i: (0, 0)),
        ],
        out_shape=[
            jax.ShapeDtypeStruct((tokens, d), F32),
            jax.ShapeDtypeStruct((N_DEV, d, nb), F32),
            jax.ShapeDtypeStruct((1, d), F32),
        ],
        scratch_shapes=[pltpu.VMEM((N_DEV, d, nb), F32), pltpu.SemaphoreType.DMA],
        compiler_params=_params("arbitrary"),
    )(x2, dres, du_pool, du_ssm, dq, dgate, g_pre, w_in_f)


def _pool_geometry(seq, width):
    gw = width // len(POOL_WINDOWS)
    col = lax.broadcasted_iota(jnp.int32, (1, width), 1)
    win = jnp.full((1, width), float(POOL_WINDOWS[-1]), F32)
    for gi in range(len(POOL_WINDOWS) - 2, -1, -1):
        win = jnp.where(col < (gi + 1) * gw, float(POOL_WINDOWS[gi]), win)
    row = lax.broadcasted_iota(jnp.int32, (seq, width), 0)
    filling = 1.0 / (lax.broadcasted_iota(jnp.int32, (seq, 1), 0) + 1).astype(F32)
    inv_cnt = jnp.where(row + 1 < win.astype(jnp.int32), filling, 1.0 / win)
    return win, row, inv_cnt


def _window_sums(a, win, seq, back):
    pad = 2 * POOL_WINDOWS[-1]
    zeros = jnp.zeros((pad, a.shape[1]), F32)
    s = jnp.concatenate([a, zeros] if back else [zeros, a], axis=0)
    sums = []
    k = 1
    while k < POOL_WINDOWS[-1]:
        s = s + pltpu.roll(s, seq + pad - k if back else k, 0)
        k *= 2
        sums.append((k, s))
    out = sums[-1][1]
    for k, s in reversed(sums[:-1]):
        out = jnp.where(win <= float(k), s, out)
    return out[0:seq] if back else out[pad : pad + seq]


def _state_row(z, n_blocks):
    re = jnp.real(z).reshape(n_blocks, -1)
    im = jnp.imag(z).reshape(n_blocks, -1)
    return jnp.concatenate([re, im], axis=1).reshape(1, -1)


def _ssm_tables(a_re, a_im, log_dt, b_re, b_im, c_re, c_im):
    groups, n_state = a_re.shape
    ch = b_re.shape[2]
    nb = groups * ch // LANES
    gl = groups // nb
    lam = lax.complex(a_re, a_im)
    lam_bar = jnp.exp(lam * jnp.exp(log_dt)[:, None])
    b_bar = ((lam_bar - 1.0) / lam)[..., None] * lax.complex(b_re, b_im)
    eye = jnp.eye(gl, dtype=F32)

    def rows_to_state(t):
        return jnp.einsum("sgnc,gh->sgchn", t.reshape(nb, gl, n_state, ch), eye).reshape(nb, gl * ch, gl * n_state)

    def state_to_rows(t):
        return jnp.einsum("sgcn,gh->shngc", t.reshape(nb, gl, ch, n_state), eye).reshape(nb, gl * n_state, gl * ch)

    b_tab = jnp.concatenate([rows_to_state(jnp.real(b_bar)), rows_to_state(jnp.imag(b_bar))], axis=2)
    c_tab = jnp.concatenate([state_to_rows(c_re), -state_to_rows(c_im)], axis=1)
    return _state_row(lam_bar, nb), b_tab, c_tab


def _lam_power(a_re, a_im, log_dt, power, scale, n_blocks):
    return _state_row(scale * jnp.exp(lax.complex(a_re, a_im) * jnp.exp(log_dt)[:, None] * power), n_blocks)


def _state_blocks(s2, n_blocks, width):
    half = s2 // n_blocks // 2
    assert half % width == 0
    return [(b * 2 * half + o, b * 2 * half + half + o) for b in range(n_blocks) for o in range(0, half, width)]


def _scan(src_ref, dst_ref, st_ref, lam8_ref, n_groups, s, n_blocks, reverse, store):
    lb = SCAN_LANES
    for re0, im0 in _state_blocks(2 * s, n_blocks, lb):
        cr, ci = pl.ds(re0, lb), pl.ds(im0, lb)
        lr = lam8_ref[:, cr]
        li = -lam8_ref[:, ci] if reverse else lam8_ref[:, ci]

        unroll = 1 if store else SCAN_UNROLL

        def step(i, carry, cr=cr, ci=ci, lr=lr, li=li):
            hr, hi = carry
            for k in range(unroll):
                grp = i * unroll + k
                grp = n_groups - 1 - grp if reverse else grp
                rows = pl.ds(pl.multiple_of(grp * SUBLANES, SUBLANES), SUBLANES)
                hr, hi = (lr * hr - li * hi + src_ref[rows, cr], lr * hi + li * hr + src_ref[rows, ci])
                if store:
                    dst_ref[rows, cr] = hr
                    dst_ref[rows, ci] = hi
            return hr, hi

        assert n_groups % unroll == 0
        hr, hi = lax.fori_loop(0, n_groups // unroll, step, (st_ref[:, cr], st_ref[:, ci]))
        st_ref[:, cr] = hr
        st_ref[:, ci] = hi


def _pack_state(re, im):
    hi = lax.bitcast_convert_type(re.astype(BF16).astype(F32), jnp.uint32)
    lo = lax.bitcast_convert_type(im.astype(BF16).astype(F32), jnp.uint32)
    return hi | (lo >> 16)


def _unpack_state(word):
    re = lax.bitcast_convert_type(word & jnp.uint32(0xFFFF0000), F32)
    im = lax.bitcast_convert_type(word << 16, F32)
    return re, im


def _lam_adjoint(g_ref, hprev_ref, group0, acc_ref, n_groups, s, n_blocks):
    lb = SCAN_LANES
    half = s // n_blocks
    assert n_groups % SCAN_UNROLL == 0
    for re0, im0 in _state_blocks(2 * s, n_blocks, lb):
        cr, ci = pl.ds(re0, lb), pl.ds(im0, lb)
        ch = pl.ds(re0 // (2 * half) * half + re0 % (2 * half), lb)

        def step(i, carry, cr=cr, ci=ci, ch=ch):
            ar, ai = carry
            for k in range(SCAN_UNROLL):
                grp = i * SCAN_UNROLL + k
                rows = pl.ds(pl.multiple_of(grp * SUBLANES, SUBLANES), SUBLANES)
                gr, gi = g_ref[rows, cr], g_ref[rows, ci]
                hr, hi = _unpack_state(hprev_ref[pl.ds(pl.multiple_of((group0 + grp) * SUBLANES, SUBLANES), SUBLANES), ch])
                ar = ar + hr * gr + hi * gi
                ai = ai + hr * gi - hi * gr
            return ar, ai

        ar, ai = lax.fori_loop(0, n_groups // SCAN_UNROLL, step, (acc_ref[:, cr], acc_ref[:, ci]))
        acc_ref[:, cr] = ar
        acc_ref[:, ci] = ai


def _chunk_starts(st_ref, init_ref, lcl_ref, s, n_blocks):
    w = s // n_blocks
    init_ref[0:1, :] = jnp.zeros((1, 2 * s), F32)
    for re0, im0 in _state_blocks(2 * s, n_blocks, w):
        re, im = pl.ds(re0, w), pl.ds(im0, w)
        ar, ai = lcl_ref[:, re], lcl_ref[:, im]
        cr = jnp.zeros((1, w), F32)
        ci = jnp.zeros((1, w), F32)
        for k in range(1, NCH):
            cr, ci = (ar * cr - ai * ci + st_ref[k - 1 : k, re], ar * ci + ai * cr + st_ref[k - 1 : k, im])
            init_ref[k : k + 1, re] = cr
            init_ref[k : k + 1, im] = ci


def _chunk_starts_adjoint(stg_ref, initg_ref, lcl_ref, s, n_blocks):
    w = s // n_blocks
    initg_ref[NCH - 1 : NCH, :] = jnp.zeros((1, 2 * s), F32)
    for re0, im0 in _state_blocks(2 * s, n_blocks, w):
        re, im = pl.ds(re0, w), pl.ds(im0, w)
        ar, ai = lcl_ref[:, re], -lcl_ref[:, im]
        gr = jnp.zeros((1, w), F32)
        gi = jnp.zeros((1, w), F32)
        for k in range(NCH - 2, -1, -1):
            gr, gi = (stg_ref[k + 1 : k + 2, re] + ar * gr - ai * gi, stg_ref[k + 1 : k + 2, im] + ar * gi + ai * gr)
            initg_ref[k : k + 1, re] = gr
            initg_ref[k : k + 1, im] = gi


def _ssm_rows(seq, want):
    rows = min(want, seq // 2)
    assert seq % rows == 0 and rows % SUBLANES == 0
    return rows


def _chunk_copies(hbm_ref, b, cm_ref, sems, to_cm, col0=0):
    cl, _, width = cm_ref.shape
    copies = []
    for k in range(NCH):
        nat, cm = hbm_ref.at[b, pl.ds(k * cl, cl), pl.ds(col0, width)], cm_ref.at[:, k, :]
        src, dst = (nat, cm) if to_cm else (cm, nat)
        copies.append(pltpu.make_async_copy(src, dst, sems.at[k]))
    return copies


def _blockwise(fn, n_blocks):
    return jnp.concatenate([fn(b) for b in range(n_blocks)], axis=1)


def _ssm_fwd(u, u_col, b_tab, c_tab, lam8, lcl, d_skip, w_glu, shards):
    batch, seq, _ = u.shape
    ns = len(shards)
    width = d_skip.shape[1]
    s = lam8.shape[1] // 2
    nb = b_tab.shape[0]
    sb = 2 * s // nb
    cl = seq // NCH
    rows = _ssm_rows(seq, SSM_ROWS)
    n_tiles = seq // rows
    n_groups = rows // SUBLANES

    def body(u_hbm, b_ref, c_ref, lam_ref, lcl_ref, d_ref, wg_ref, *rest):
        shard_hbm, (y_hbm, pre_ref, z_ref, init_ref) = rest[:ns], rest[ns : ns + 4]
        gathered = rest[ns + 4 : 2 * ns + 4]
        u_cm, y_cm, bu_all, st, sems, send_sems, recv_sems, own_sems = rest[2 * ns + 4 :]
        b, ph, t = pl.program_id(0), pl.program_id(1), pl.program_id(2)
        tile_groups = pl.ds(pl.multiple_of(t * n_groups, n_groups), n_groups)
        x_pos, y_pos, c_pos = _place()
        own = [pltpu.make_async_copy(shard_hbm[a], gathered[a].at[4 * x_pos + 2 * y_pos + c_pos], own_sems.at[a])
               for a in range(ns)]
        exchange_start, exchange_finish = _exchange_steps(
            ns, lambda a, slot: shard_hbm[a], gathered, send_sems, recv_sems)

        @pl.when((b == 0) & (ph == 0) & (t == 0))
        def _():
            exchange_start()
            for cp in own:
                cp.start()

        @pl.when((b == batch - 1) & (ph == 1) & (t == n_tiles - 1))
        def _():
            exchange_finish()
            for cp in own:
                cp.wait()

        @pl.when((ph == 0) & (t == 0))
        def _():
            loads = _chunk_copies(u_hbm, b, u_cm, sems, True, u_col)
            for cp in loads:
                cp.start()
            st[...] = jnp.zeros_like(st)
            for cp in loads:
                cp.wait()

        @pl.when((ph == 1) & (t == 0))
        def _():
            st[...] = init_ref[...]

        u_t = u_cm[tile_groups].reshape(rows, width)
        bu = bu_all.at[pl.ds(pl.multiple_of(t * rows, rows), rows)]

        @pl.when(ph == 0)
        def _():
            u_b = u_t.astype(BF16)
            for blk in range(nb):
                bu[:, blk * sb : (blk + 1) * sb] = _mm(u_b[:, blk * LANES : (blk + 1) * LANES], b_ref[blk])
            _scan(bu, bu, st, lam_ref, n_groups, s, nb, False, False)

        @pl.when((ph == 0) & (t == n_tiles - 1))
        def _():
            _chunk_starts(st, init_ref, lcl_ref, s, nb)

        @pl.when(ph == 1)
        def _():
            _scan(bu, bu, st, lam_ref, n_groups, s, nb, False, True)
            hs = lambda blk: _mm(bu[:, blk * sb : (blk + 1) * sb].astype(BF16), c_ref[blk])
            pre = _blockwise(hs, nb) + d_ref[...] * u_t
            z = _mm(_gelu(pre).astype(BF16), wg_ref[...])
            pre_ref[...] = pre
            z_ref[...] = z
            y = z[:, 0:width] * jax.nn.sigmoid(z[:, width : 2 * width])
            y_cm[tile_groups] = y.reshape(n_groups, SUBLANES, width)

        @pl.when((ph == 1) & (t == n_tiles - 1))
        def _():
            stores = _chunk_copies(y_hbm, b, y_cm, sems, False)
            for cp in stores:
                cp.start()
            for cp in stores:
                cp.wait()

    out_tile = lambda b, ph, t: (b, t * ph, 0)
    full = lambda a: pl.BlockSpec(a.shape, lambda b, ph, t: (0,) * a.ndim)
    hbm = pl.BlockSpec(memory_space=pl.ANY)
    res = pl.pallas_call(
        body,
        name="ssm_fwd",
        grid=(batch, 2, n_tiles),
        in_specs=[hbm, full(b_tab), full(c_tab), full(lam8), full(lcl), full(d_skip), full(w_glu)] + [hbm] * ns,
        out_specs=[
            hbm,
            pl.BlockSpec((None, rows, width), out_tile),
            pl.BlockSpec((None, rows, 2 * width), out_tile),
            pl.BlockSpec((None, SUBLANES, 2 * s), lambda b, ph, t: (b, 0, 0)),
        ]
        + [hbm] * ns,
        out_shape=[
            jax.ShapeDtypeStruct((batch, seq, width), F32),
            jax.ShapeDtypeStruct((batch, seq, width), F32),
            jax.ShapeDtypeStruct((batch, seq, 2 * width), F32),
            jax.ShapeDtypeStruct((batch, SUBLANES, 2 * s), F32),
        ]
        + [jax.ShapeDtypeStruct((N_DEV, *a.shape), a.dtype) for a in shards],
        scratch_shapes=[
            pltpu.VMEM((cl, NCH, width), F32),
            pltpu.VMEM((cl, NCH, width), F32),
            pltpu.VMEM((seq, 2 * s), F32),
            pltpu.VMEM((SUBLANES, 2 * s), F32),
            pltpu.SemaphoreType.DMA((NCH,)),
            pltpu.SemaphoreType.DMA((ns, 7)),
            pltpu.SemaphoreType.DMA((ns, 7)),
            pltpu.SemaphoreType.DMA((ns,)),
        ],
        compiler_params=_params("arbitrary", "arbitrary", "arbitrary"),
    )(u, b_tab, c_tab, lam8, lcl, d_skip, w_glu, *shards)
    return res[:4], res[4:]


def _ssm_bwd(u, u_col, pre_p, z_p, dy, init, b_tab, b_tab_t, c_tab_t, lam8, lcl, d_skip, w_glu, ready):
    batch, seq, _ = u.shape
    width = d_skip.shape[1]
    nr = len(ready)
    s = lam8.shape[1] // 2
    nb = b_tab.shape[0]
    sb = 2 * s // nb
    cl = seq // NCH
    rows = _ssm_rows(seq, SSM_ROWS)
    n_tiles = seq // rows
    n_groups = rows // SUBLANES

    def body(u_hbm, pre_ref, z_ref, dy_hbm, init_ref, b_ref, bt_ref, ct_ref, lam_ref, lcl_ref, d_ref, wg_ref, *rest):
        ready_hbm, rest = rest[:nr], rest[nr:]
        du_hbm, gb_ref, gc_ref, gwg_ref, gd_ref, glam_ref = rest[:6]
        landed_hbm, rest = rest[6 : 6 + nr], rest[6 + nr :]
        u_cm, dy_cm, work, hs_all, dpre_all, st, stg, initg, acc, sems, send_sems, recv_sems = rest
        b, ph, t = pl.program_id(0), pl.program_id(1), pl.program_id(2)
        half = s // nb
        exchange_start, exchange_finish = _exchange_steps(
            nr, lambda a, slot: ready_hbm[a].at[slot], landed_hbm, send_sems, recv_sems)
        first = (b == 0) & (ph == 0) & (t == 0)
        last = (b == batch - 1) & (ph == 2) & (t == n_tiles - 1)
        tile = jnp.where(ph == 0, t, n_tiles - 1 - t)
        tile_rows = pl.ds(pl.multiple_of(tile * rows, rows), rows)
        tile_groups = pl.ds(pl.multiple_of(tile * n_groups, n_groups), n_groups)
        lanes = lambda blk: slice(blk * LANES, (blk + 1) * LANES)
        states = lambda blk: slice(blk * sb, (blk + 1) * sb)

        @pl.when(first)
        def _():
            exchange_start()
            acc[...] = jnp.zeros_like(acc)
            gb_ref[...] = jnp.zeros_like(gb_ref)
            gc_ref[...] = jnp.zeros_like(gc_ref)
            gwg_ref[...] = jnp.zeros_like(gwg_ref)
            gd_ref[...] = jnp.zeros_like(gd_ref)

        @pl.when((ph == 0) & (t == 0))
        def _():
            loads = (_chunk_copies(u_hbm, b, u_cm, sems.at[0], True, u_col)
                     + _chunk_copies(dy_hbm, b, dy_cm, sems.at[1], True))
            for cp in loads:
                cp.start()
            st[...] = init_ref[...]
            for blk in range(nb):
                entry = init_ref[:, states(blk)]
                hs_all[0:SUBLANES, blk * half : (blk + 1) * half] = _pack_state(entry[:, 0:half], entry[:, half : 2 * half])
            for cp in loads:
                cp.wait()

        u_t = u_cm[tile_groups].reshape(rows, width)
        u_b = u_t.astype(BF16)

        @pl.when(ph == 0)
        def _():
            for blk in range(nb):
                work[:, states(blk)] = _mm(u_b[:, lanes(blk)], b_ref[blk])
            _scan(work, work, st, lam_ref, n_groups, s, nb, False, True)
            z = z_ref[...]
            dy_t = dy_cm[tile_groups].reshape(rows, width)
            pre = pre_ref[...]
            z1, sig = z[:, 0:width], jax.nn.sigmoid(z[:, width : 2 * width])
            dz = jnp.concatenate([dy_t * sig, dy_t * z1 * sig * (1.0 - sig)], axis=1).astype(BF16)
            gwg_ref[...] += _mm_tn(_gelu(pre).astype(BF16), dz)
            dpre = _mm_nt(dz, wg_ref[...]) * _gelu_grad(pre)
            dpre_all[tile_rows, :] = dpre
            gd_ref[...] += jnp.sum(dpre * u_t, axis=0, keepdims=True)
            dpre_b = dpre.astype(BF16)
            kept = pl.ds(pl.multiple_of(tile * rows + SUBLANES, SUBLANES), rows)
            for blk in range(nb):
                hs = work[:, states(blk)]
                gc_ref[blk] += _mm_tn(dpre_b[:, lanes(blk)], hs.astype(BF16))
                hs_all[kept, blk * half : (blk + 1) * half] = _pack_state(hs[:, 0:half], hs[:, half : 2 * half])

        @pl.when(ph >= 1)
        def _():
            dpre_b = dpre_all[tile_rows, :].astype(BF16)
            for blk in range(nb):
                work[:, states(blk)] = _mm(dpre_b[:, lanes(blk)], ct_ref[blk])

        @pl.when(ph == 1)
        def _():
            @pl.when(t == 0)
            def _():
                stg[...] = jnp.zeros_like(stg)

            _scan(work, work, stg, lam_ref, n_groups, s, nb, True, False)

            @pl.when(t == n_tiles - 1)
            def _():
                _chunk_starts_adjoint(stg, initg, lcl_ref, s, nb)

        @pl.when(ph == 2)
        def _():
            @pl.when(t == 0)
            def _():
                stg[...] = initg[...]

            _scan(work, work, stg, lam_ref, n_groups, s, nb, True, True)
            _lam_adjoint(work, hs_all, tile * n_groups, acc, n_groups, s, nb)
            du = lambda blk: _mm(work[:, states(blk)].astype(BF16), bt_ref[blk])
            du_t = _blockwise(du, nb) + dpre_all[tile_rows, :] * d_ref[...]
            dy_cm[tile_groups] = du_t.reshape(n_groups, SUBLANES, width)
            for blk in range(nb):
                gb_ref[blk] += _mm_tn(u_b[:, lanes(blk)], work[:, states(blk)].astype(BF16))

            @pl.when(t == n_tiles - 1)
            def _():
                stores = _chunk_copies(du_hbm, b, dy_cm, sems.at[0], False)
                for cp in stores:
                    cp.start()
                for cp in stores:
                    cp.wait()

        @pl.when(last)
        def _():
            glam_ref[...] = jnp.sum(acc[...], axis=0, keepdims=True)
            exchange_finish()

    def tile(b, ph, t):
        return (b, jnp.where(ph == 0, t, n_tiles - 1 - t), 0)

    full = lambda a: pl.BlockSpec(a.shape, lambda b, ph, t: (0,) * a.ndim)
    hbm = pl.BlockSpec(memory_space=pl.ANY)
    res = pl.pallas_call(
        body,
        name="ssm_bwd",
        grid=(batch, 3, n_tiles),
        in_specs=[
            hbm,
            pl.BlockSpec((None, rows, width), tile),
            pl.BlockSpec((None, rows, 2 * width), tile),
            hbm,
            pl.BlockSpec((None, SUBLANES, 2 * s), lambda b, ph, t: (b, 0, 0)),
            full(b_tab), full(b_tab_t), full(c_tab_t), full(lam8), full(lcl), full(d_skip), full(w_glu),
        ]
        + [hbm] * nr,
        out_specs=[
            hbm,
            full(b_tab), full(b_tab), full(w_glu), full(d_skip),
            pl.BlockSpec((1, 2 * s), lambda b, ph, t: (0, 0)),
        ]
        + [hbm] * nr,
        out_shape=[
            jax.ShapeDtypeStruct((batch, seq, width), F32),
            jax.ShapeDtypeStruct(b_tab.shape, F32),
            jax.ShapeDtypeStruct(b_tab.shape, F32),
            jax.ShapeDtypeStruct(w_glu.shape, F32),
            jax.ShapeDtypeStruct(d_skip.shape, F32),
            jax.ShapeDtypeStruct((1, 2 * s), F32),
        ]
        + [jax.ShapeDtypeStruct(a.shape, F32) for a in ready],
        scratch_shapes=[
            pltpu.VMEM((cl, NCH, width), F32),
            pltpu.VMEM((cl, NCH, width), F32),
            pltpu.VMEM((rows, 2 * s), F32),
            pltpu.VMEM((seq + SUBLANES, s), jnp.uint32),
            pltpu.VMEM((seq, width), F32),
        ]
        + [pltpu.VMEM((SUBLANES, 2 * s), F32)] * 4
        + [pltpu.SemaphoreType.DMA((2, NCH)), pltpu.SemaphoreType.DMA((nr, 7)), pltpu.SemaphoreType.DMA((nr, 7))],
        compiler_params=_params("arbitrary", "arbitrary", "arbitrary"),
    )(u, pre_p, z_p, dy, init, b_tab, b_tab_t, c_tab_t, lam8, lcl, d_skip, w_glu, *ready)
    return res[:6], res[6:]


def _pool_kv_fwd(u2, wp_blk, scale, seq, mem, g_mem, w_kv):
    batch, n_mem, d = mem.shape
    kvw = w_kv.shape[1]
    width = scale.shape[1]

    def body(u_ref, wp_ref, s_ref, mem_ref, g_ref, w_ref, y_ref, diff_ref, kv_ref):
        u = u_ref[...]
        win, row, inv_cnt = _pool_geometry(seq, width)
        diff = (_window_sums(u, win, seq, False) * inv_cnt - u).astype(BF16)
        diff_ref[...] = diff
        y_ref[...] = _mm(diff, wp_ref[...]) * s_ref[...]
        m = mem_ref[...]
        kv_ref[...] = _mm((m * _rms(m) * g_ref[...]).astype(BF16), w_ref[...])

    return pl.pallas_call(
        body,
        name="pool_kv_fwd",
        grid=(batch,),
        in_specs=[
            pl.BlockSpec((seq, width), lambda b: (b, 0)),
            pl.BlockSpec((width, width), lambda b: (0, 0)),
            pl.BlockSpec((1, width), lambda b: (0, 0)),
            pl.BlockSpec((None, n_mem, d), lambda b: (b, 0, 0)),
            pl.BlockSpec((1, d), lambda b: (0, 0)),
            pl.BlockSpec((d, kvw), lambda b: (0, 0)),
        ],
        out_specs=[
            pl.BlockSpec((seq, width), lambda b: (b, 0)),
            pl.BlockSpec((seq, width), lambda b: (b, 0)),
            pl.BlockSpec((None, n_mem, kvw), lambda b: (b, 0, 0)),
        ],
        out_shape=[
            jax.ShapeDtypeStruct((u2.shape[0], width), F32),
            jax.ShapeDtypeStruct((u2.shape[0], width), BF16),
            jax.ShapeDtypeStruct((batch, n_mem, kvw), F32),
        ],
        compiler_params=_params("arbitrary"),
    )(u2, wp_blk, scale, mem, g_mem, w_kv)


def _pool_kv_bwd(diff2, dy2, wp_blk, scale, seq, mem, dkv, g_mem, w_kv):
    batch, n_mem, d = mem.shape
    kvw = w_kv.shape[1]
    width = scale.shape[1]

    def body(diff_ref, dy_ref, wp_ref, s_ref, mem_ref, dkv_ref, g_ref, w_ref, du_ref, gwp_ref, gs_ref, gw_ref, gg_ref):
        @pl.when(pl.program_id(0) == 0)
        def _():
            gwp_ref[...] = jnp.zeros_like(gwp_ref)
            gs_ref[...] = jnp.zeros_like(gs_ref)
            gw_ref[...] = jnp.zeros_like(gw_ref)
            gg_ref[...] = jnp.zeros_like(gg_ref)

        diff = diff_ref[...]
        dy = dy_ref[...]
        win, row, inv_cnt = _pool_geometry(seq, width)
        gs_ref[...] += jnp.sum(dy * _mm(diff, wp_ref[...]), axis=0, keepdims=True)
        dys = (dy * s_ref[...]).astype(BF16)
        gwp_ref[...] += _mm_tn(diff, dys)
        dd = _mm_nt(dys, wp_ref[...])
        du_ref[...] = _window_sums(dd * inv_cnt, win, seq, True) - dd

        m = mem_ref[...]
        mn = m * _rms(m)
        dkv_b = dkv_ref[...].astype(BF16)
        gw_ref[...] += _mm_tn((mn * g_ref[...]).astype(BF16), dkv_b)
        gg_ref[...] += jnp.sum(_mm_nt(dkv_b, w_ref[...]) * mn, axis=0, keepdims=True)

    return pl.pallas_call(
        body,
        name="pool_kv_bwd",
        grid=(batch,),
        in_specs=[
            pl.BlockSpec((seq, width), lambda b: (b, 0)),
            pl.BlockSpec((seq, width), lambda b: (b, 0)),
            pl.BlockSpec((width, width), lambda b: (0, 0)),
            pl.BlockSpec((1, width), lambda b: (0, 0)),
            pl.BlockSpec((None, n_mem, d), lambda b: (b, 0, 0)),
            pl.BlockSpec((None, n_mem, kvw), lambda b: (b, 0, 0)),
            pl.BlockSpec((1, d), lambda b: (0, 0)),
            pl.BlockSpec((d, kvw), lambda b: (0, 0)),
        ],
        out_specs=[
            pl.BlockSpec((seq, width), lambda b: (b, 0)),
            pl.BlockSpec((width, width), lambda b: (0, 0)),
            pl.BlockSpec((1, width), lambda b: (0, 0)),
            pl.BlockSpec((d, kvw), lambda b: (0, 0)),
            pl.BlockSpec((1, d), lambda b: (0, 0)),
        ],
        out_shape=[
            jax.ShapeDtypeStruct(dy2.shape, F32),
            jax.ShapeDtypeStruct((width, width), F32),
            jax.ShapeDtypeStruct((1, width), F32),
            jax.ShapeDtypeStruct((d, kvw), F32),
            jax.ShapeDtypeStruct((1, d), F32),
        ],
        compiler_params=_params("arbitrary"),
    )(diff2, dy2, wp_blk, scale, mem, dkv, g_mem, w_kv)


def _tail(x2, target2, proj, y_pool, y_ssm, kv, w_out, g_post):
    tokens, d = x2.shape
    batch, n_mem, kvw = kv.shape
    pool_w, ssm_w, att_w, mix = y_pool.shape[1], y_ssm.shape[1], kvw // 2, w_out.shape[0]
    assert (mix - att_w) % att_w == 0 and proj.shape[1] == 2 * mix
    hd = att_w // MEM_HEADS
    cl = _token_tile(tokens // batch, TOKEN_ROWS)
    n_tiles = tokens // cl
    per_seq = tokens // batch // cl
    qk_scale = hd**-0.5

    def body(x_ref, tg_ref, gate_ref, yp_ref, ys_ref, q_ref, kv_ref, w_ref, g_ref,
             dres_ref, dgate_ref, dyp_ref, dys_ref, dq_ref, dkv_ref, gw_hbm, gg_ref, loss_ref, acc, sem):
        i = pl.program_id(0)

        @pl.when(i == 0)
        def _():
            acc[...] = jnp.zeros_like(acc)
            gg_ref[...] = jnp.zeros_like(gg_ref)
            loss_ref[...] = jnp.zeros_like(loss_ref)

        @pl.when(i % per_seq == 0)
        def _():
            dkv_ref[...] = jnp.zeros_like(dkv_ref)

        k = kv_ref[:, 0:att_w].astype(BF16)
        v = kv_ref[:, att_w : 2 * att_w].astype(BF16)
        lane = lax.broadcasted_iota(jnp.int32, (1, att_w), 1)
        heads = [(lane >= h * hd) & (lane < (h + 1) * hd) for h in range(MEM_HEADS)]
        g = g_ref[...]

        def part(rows):
            n_rows = rows.stop - rows.start
            q = q_ref[rows, :]
            probs, q_heads = [], []
            att = jnp.zeros((n_rows, att_w), F32)
            for mask in heads:
                qh = jnp.where(mask, q, 0.0).astype(BF16)
                sc = _mm_nt(qh, k) * qk_scale
                e = jnp.exp(sc - jnp.max(sc, axis=-1, keepdims=True))
                p = e * (1.0 / jnp.sum(e, axis=-1, keepdims=True))
                att = att + jnp.where(mask, _mm(p.astype(BF16), v), 0.0)
                probs.append(p)
                q_heads.append(qh)

            ycat = jnp.concatenate([yp_ref[rows, :], ys_ref[rows, :], att], axis=1)
            gate = gate_ref[rows, :]
            sig = jax.nn.sigmoid(gate)
            silu = gate * sig
            yg = (ycat * silu).astype(BF16)
            out = _mm(yg, w_ref[...])
            r = _rms(out)
            on = out * r
            err = x_ref[rows, :] + on * g - tg_ref[rows, :]
            loss_ref[...] += 0.5 * jnp.sum(jnp.mean(err * err, axis=-1, keepdims=True), axis=0, keepdims=True)
            dres = err * (1.0 / d)
            dres_ref[rows, :] = dres
            gg_ref[...] += jnp.sum(dres * on, axis=0, keepdims=True)
            don = dres * g
            dout = (r * (don - on * jnp.mean(don * on, axis=-1, keepdims=True))).astype(BF16)
            acc[...] += _mm_tn(yg, dout)
            dyg = _mm_nt(dout, w_ref[...])
            dgate_ref[rows, :] = dyg * ycat * (sig * (1.0 + gate * (1.0 - sig)))
            dycat = dyg * silu
            dyp_ref[rows, :] = dycat[:, 0:pool_w]
            dys_ref[rows, :] = dycat[:, pool_w : pool_w + ssm_w]
            datt = dycat[:, pool_w + ssm_w : mix]

            dq = jnp.zeros((n_rows, att_w), F32)
            dk = jnp.zeros((n_mem, att_w), F32)
            dv = jnp.zeros((n_mem, att_w), F32)
            for mask, p, qh in zip(heads, probs, q_heads):
                doh = jnp.where(mask, datt, 0.0).astype(BF16)
                dp = _mm_nt(doh, v)
                ds = (p * (dp - jnp.sum(p * dp, axis=-1, keepdims=True)) * qk_scale).astype(BF16)
                dq = dq + jnp.where(mask, _mm(ds, k), 0.0)
                dk = dk + _mm_tn(ds, qh)
                dv = dv + _mm_tn(p.astype(BF16), doh)
            dq_ref[rows, :] = dq
            dkv_ref[:, 0:att_w] += dk
            dkv_ref[:, att_w : 2 * att_w] += dv

        part(slice(0, cl))

        @pl.when(i == n_tiles - 1)
        def _():
            cp = pltpu.make_async_copy(acc, gw_hbm, sem)
            cp.start()
            cp.wait()

    tok = lambda w: pl.BlockSpec((cl, w), lambda i: (i, 0))
    chunked = tok(ssm_w)
    per_batch = pl.BlockSpec((None, n_mem, kvw), lambda i: (i // per_seq, 0, 0))
    return pl.pallas_call(
        body,
        name="tail",
        grid=(n_tiles,),
        in_specs=[
            tok(d), tok(d), pl.BlockSpec((cl, mix), lambda i: (i, 1)), tok(pool_w), chunked,
            pl.BlockSpec((cl, att_w), lambda i: (i, (mix - att_w) // att_w)), per_batch,
            pl.BlockSpec((mix, d), lambda i: (0, 0)),
            pl.BlockSpec((1, d), lambda i: (0, 0)),
        ],
        out_specs=[
            tok(d), tok(mix), tok(pool_w), chunked, tok(att_w), per_batch,
            pl.BlockSpec(memory_space=pl.ANY),
            pl.BlockSpec((1, d), lambda i: (0, 0)),
            pl.BlockSpec((1, 1), lambda i: (0, 0)),
        ],
        out_shape=[
            jax.ShapeDtypeStruct((tokens, d), F32),
            jax.ShapeDtypeStruct((tokens, mix), F32),
            jax.ShapeDtypeStruct((tokens, pool_w), F32),
            jax.ShapeDtypeStruct((tokens, ssm_w), F32),
            jax.ShapeDtypeStruct((tokens, att_w), F32),
            jax.ShapeDtypeStruct(kv.shape, F32),
            jax.ShapeDtypeStruct((mix, d), F32),
            jax.ShapeDtypeStruct((1, d), F32),
            jax.ShapeDtypeStruct((1, 1), F32),
        ],
        scratch_shapes=[pltpu.VMEM((mix, d), F32), pltpu.SemaphoreType.DMA],
        compiler_params=_params("arbitrary"),
    )(x2, target2, proj, y_pool, y_ssm, proj, kv, w_out, g_post)


def _pack(arrays):
    flat = jnp.concatenate([a.reshape(-1) for a in arrays])
    rows = -(-flat.size // (N_DEV * SUBLANES * LANES)) * N_DEV * SUBLANES
    return jnp.pad(flat, (0, rows * LANES - flat.size)).reshape(rows, LANES)


def _unpack(packed, like):
    flat, out, at = packed.reshape(-1), [], 0
    for a in like:
        out.append(flat[at : at + a.size].reshape(a.shape))
        at += a.size
    return out


def kernel(x, mem, g_pre, w_in, w_pool, pool_scale, a_re, a_im, log_dt, b_re, b_im, c_re, c_im, d_skip, w_glu, g_mem, w_kv, w_out, g_post, loss_target, m_g_pre, m_w_in, m_w_pool, m_pool_scale, m_a_re, m_a_im, m_log_dt, m_b_re, m_b_im, m_c_re, m_c_im, m_d_skip, m_w_glu, m_g_mem, m_w_kv, m_w_out, m_g_post, v_g_pre, v_w_in, v_w_pool, v_pool_scale, v_a_re, v_a_im, v_log_dt, v_b_re, v_b_im, v_c_re, v_c_im, v_d_skip, v_w_glu, v_g_mem, v_w_kv, v_w_out, v_g_post):
    batch, seq, d = x.shape
    cl = seq // NCH
    pool_w, ssm_w = pool_scale.shape[1], d_skip.shape[1]
    att_w = w_kv.shape[2] // 2
    tokens = batch * seq
    x2 = x.reshape(tokens, d)
    target2 = loss_target.reshape(tokens, d)

    wp_blk = jax.scipy.linalg.block_diag(*w_pool[0]).astype(BF16)
    ssm_params = (a_re[0], a_im[0], log_dt[0], b_re[0], b_im[0], c_re[0], c_im[0])
    (lam_row, b_tab, c_tab), tables_vjp = jax.vjp(_ssm_tables, *ssm_params)
    nb = b_tab.shape[0]
    lam8 = jnp.broadcast_to(lam_row, (SUBLANES, lam_row.shape[1]))
    lcl = _lam_power(a_re[0], a_im[0], log_dt[0], float(cl), 1.0, nb)
    b_bf, c_bf = b_tab.astype(BF16), c_tab.astype(BF16)

    proj, w_in_f, (w_glu_g,) = _in_proj(x2, g_pre, w_in[0], [w_glu[0].astype(BF16)])
    proj3 = proj.reshape(batch, seq, proj.shape[1])
    w_glu_f = w_glu_g.transpose(1, 0, 2).reshape(w_glu_g.shape[1], N_DEV * w_glu_g.shape[2])
    (y_ssm, pre_ssm, z_ssm, init_ssm), (w_out_g, w_kv_g) = _ssm_fwd(
        proj3, pool_w, b_bf, c_bf, lam8, lcl, d_skip, w_glu_f, [w_out[0].astype(BF16), w_kv[0].astype(BF16)])
    w_out_f = w_out_g.reshape(N_DEV * w_out_g.shape[1], w_out_g.shape[2])
    w_kv_f = w_kv_g.reshape(N_DEV * w_kv_g.shape[1], w_kv_g.shape[2])
    y_pool, diff_pool, kv = _pool_kv_fwd(proj, wp_blk, pool_scale, seq, mem, g_mem, w_kv_f)

    dres, dgate, dy_pool, dy_ssm, dq, dkv, gw_out, gg_post, loss_part = _tail(
        x2, target2, proj, y_pool, y_ssm.reshape(tokens, ssm_w), kv, w_out_f, g_post)

    du_pool, gwp_dense, g_scale, gw_kv, gg_mem = _pool_kv_bwd(
        diff_pool, dy_pool, wp_blk, pool_scale, seq, mem, dkv, g_mem, w_kv_f)
    gw_kv8 = gw_kv.reshape(N_DEV, -1, gw_kv.shape[1])
    gw_out8 = gw_out.reshape(N_DEV, -1, gw_out.shape[1])
    (du_ssm, gb_tab, gc_tab_t, gw_glu, gd_skip, glam), (kv_landed, out_landed) = _ssm_bwd(
        proj3, pool_w, pre_ssm, z_ssm, dy_ssm.reshape(batch, seq, ssm_w), init_ssm, b_bf, b_bf.transpose(0, 2, 1),
        c_bf.transpose(0, 2, 1), lam8, lcl, d_skip, w_glu_f, [gw_kv8, gw_out8])
    grad_x2, gw_in, gg_pre = _in_proj_bwd(
        x2, dres, du_pool, du_ssm.reshape(tokens, ssm_w), dq, dgate, g_pre,
        w_in_f)

    gw = pool_w // len(POOL_WINDOWS)
    gw_pool = jnp.stack([gwp_dense[i * gw : (i + 1) * gw, i * gw : (i + 1) * gw] for i in range(len(POOL_WINDOWS))])
    g_ssm = tables_vjp((glam, gb_tab, gc_tab_t.transpose(0, 2, 1)))

    small_w = [g_pre, w_pool, pool_scale, a_re, a_im, log_dt, b_re, b_im, c_re, c_im, d_skip, g_mem, g_post]
    small_m = [m_g_pre, m_w_pool, m_pool_scale, m_a_re, m_a_im, m_log_dt, m_b_re, m_b_im, m_c_re, m_c_im, m_d_skip, m_g_mem, m_g_post]
    small_v = [v_g_pre, v_w_pool, v_pool_scale, v_a_re, v_a_im, v_log_dt, v_b_re, v_b_im, v_c_re, v_c_im, v_d_skip, v_g_mem, v_g_post]
    small_g = [gg_pre, gw_pool, g_scale, *g_ssm, gd_skip, gg_mem, gg_post]
    big_g, small_sum = _reduce_all(
        [gw_in, gw_glu.reshape(ssm_w, N_DEV, -1).transpose(1, 0, 2)],
        _pack(small_g + [loss_part]),
        [(gw_kv8, kv_landed), (gw_out8, out_landed)])

    flat2 = lambda a: a.reshape(-1, a.shape[-1])
    sg = _unpack(small_sum, [flat2(a) for a in small_w] + [loss_part])
    loss = sg[-1].reshape(())
    small_names = ["g_pre", "w_pool", "pool_scale", "a_re", "a_im", "log_dt", "b_re", "b_im", "c_re", "c_im",
                   "d_skip", "g_mem", "g_post"]
    names = ["w_in", "w_glu", "w_kv", "w_out"] + small_names
    all_w = [w_in, w_glu, w_kv, w_out] + small_w
    all_m = [m_w_in, m_w_glu, m_w_kv, m_w_out] + small_m
    all_v = [v_w_in, v_w_glu, v_w_kv, v_w_out] + small_v
    updates = _adamw_all(big_g + sg[:-1], [flat2(a) for a in all_w], [flat2(a) for a in all_m], [flat2(a) for a in all_v])
    updates = {name: [t.reshape(a.shape) for t in u] for name, u, a in zip(names, updates, all_w)}

    order = ["g_pre", "w_in", "w_pool", "pool_scale", "a_re", "a_im", "log_dt", "b_re", "b_im", "c_re", "c_im",
             "d_skip", "w_glu", "g_mem", "w_kv", "w_out", "g_post"]
    outs = [[updates[name][kind] for name in order] for kind in range(4)]
    return (loss, grad_x2.reshape(batch, seq, d), *outs[0], *outs[1], *outs[2], *outs[3])
```

```python
import math

import jax
import jax.numpy as jnp
from jax import lax
from jax.experimental import pallas as pl
from jax.experimental.pallas import tpu as pltpu

F32 = jnp.float32
BF16 = jnp.bfloat16
MESH = pl.DeviceIdType.MESH

N_DEV = 8
SUBLANES = 8
LANES = 128
NCH = SUBLANES
VMEM_LIMIT = 60 * 1024 * 1024

IN_PROJ_ROWS = 2048
TOKEN_ROWS = 512
SSM_ROWS = 1024
SCAN_LANES = 512
SCAN_UNROLL = 16

EPS = 1e-6
POOL_WINDOWS = (2, 4, 8, 16)
MEM_HEADS = 4
ADAM_LR, ADAM_B1, ADAM_B2, ADAM_EPS, ADAM_WD, ADAM_STEP = 0.001, 0.9, 0.999, 1e-08, 0.01, 10


def _mm(a, b):
    return jnp.dot(a, b, preferred_element_type=F32)


def _mm_nt(a, b):
    return lax.dot_general(a, b, (((1,), (1,)), ((), ())), preferred_element_type=F32)


def _mm_tn(a, b):
    return lax.dot_general(a, b, (((0,), (0,)), ((), ())), preferred_element_type=F32)


def _params(*sem):
    return pltpu.CompilerParams(dimension_semantics=sem or None, vmem_limit_bytes=VMEM_LIMIT)


def _adamw(w, g, m, v):
    m = ADAM_B1 * m + (1.0 - ADAM_B1) * g
    v = ADAM_B2 * v + (1.0 - ADAM_B2) * (g * g)
    m_hat = m / (1.0 - ADAM_B1**ADAM_STEP)
    v_hat = v / (1.0 - ADAM_B2**ADAM_STEP)
    delta = -ADAM_LR * (m_hat / (jnp.sqrt(v_hat) + ADAM_EPS) + ADAM_WD * w)
    return delta, m, v


def _gelu(x):
    k = math.sqrt(2.0 / math.pi)
    return 0.5 * x * (1.0 + jnp.tanh(k * (x + 0.044715 * x * x * x)))


def _gelu_grad(x):
    k = math.sqrt(2.0 / math.pi)
    th = jnp.tanh(k * (x + 0.044715 * x * x * x))
    return 0.5 * (1.0 + th) + 0.5 * x * (1.0 - th * th) * (k * (1.0 + 3.0 * 0.044715 * x * x))


def _place():
    return lax.axis_index("x"), lax.axis_index("y"), lax.axis_index("c")


def _exchange_steps(n, src_of, landing, send_sems, recv_sems):
    x, y, c = _place()
    me = 4 * x + 2 * y + c
    peers = []
    for j in range(1, N_DEV):
        px = 1 - x if j & 4 else x
        py = 1 - y if j & 2 else y
        pc = 1 - c if j & 1 else c
        peers.append((px, py, pc))

    def copy(a, j, from_slot, to_slot, peer):
        return pltpu.make_async_remote_copy(
            src_ref=src_of(a, to_slot), dst_ref=landing[a].at[from_slot],
            send_sem=send_sems.at[a, j], recv_sem=recv_sems.at[a, j], device_id=peer, device_id_type=MESH)

    def start():
        for a in range(n):
            for j, p in enumerate(peers):
                copy(a, j, me, 4 * p[0] + 2 * p[1] + p[2], p).start()

    def finish():
        for a in range(n):
            for j, p in enumerate(peers):
                slot = 4 * p[0] + 2 * p[1] + p[2]
                copy(a, j, slot, slot, p).wait_recv()
        for a in range(n):
            for j, p in enumerate(peers):
                copy(a, j, me, 4 * p[0] + 2 * p[1] + p[2], p).wait_send()

    return start, finish


def _adamw_all(gs, ws, ms, vs):
    n = len(gs)

    def body(*refs):
        g, w, m, v = refs[:n], refs[n : 2 * n], refs[2 * n : 3 * n], refs[3 * n : 4 * n]
        outs = refs[4 * n :]
        for a in range(n):
            rows = g[a].shape[0]
            chunk = math.gcd(rows, 128)

            def step(i, _, a=a, chunk=chunk):
                r = pl.ds(pl.multiple_of(i * chunk, chunk), chunk)
                grad = g[a][r, :]
                delta, nm, nv = _adamw(w[a][r, :], grad, m[a][r, :], v[a][r, :])
                outs[4 * a][r, :] = grad
                outs[4 * a + 1][r, :] = delta
                outs[4 * a + 2][r, :] = nm
                outs[4 * a + 3][r, :] = nv
                return 0

            lax.fori_loop(0, rows // chunk, step, 0)

    vmem = pl.BlockSpec(memory_space=pltpu.VMEM)
    out_shape = []
    for wa in ws:
        out_shape += [jax.ShapeDtypeStruct(wa.shape, F32)] * 4
    res = pl.pallas_call(
        body,
        name="adamw_all",
        out_shape=out_shape,
        in_specs=[vmem] * (4 * n),
        out_specs=[vmem] * (4 * n),
        compiler_params=_params(),
    )(*gs, *ws, *ms, *vs)
    return [tuple(res[4 * a : 4 * a + 4]) for a in range(n)]


def _reduce_all(parts, small, early):
    n, ne = len(parts), len(early)
    parts4 = [p.reshape(4, 2, *p.shape[1:]) for p in parts]
    blks = [p.shape[1:] for p in parts]

    def body(*refs):
        refs = list(refs)
        take = lambda k: [refs.pop(0) for _ in range(k)]
        part = take(n)
        (small_ref,) = take(1)
        early_in = [take(2) for _ in range(ne)]
        outs = take(n)
        (small_all,) = take(1)
        early_out = take(ne)
        own, r1, got_a1, got_a2, got_b1, got_b2, pass_a, pass_b = (take(n) for _ in range(8))
        early_buf = take(ne)
        small_land, small_mine = take(2)
        s1_send, s1_recv, h_send, h_recv, loc, rs_send, rs_recv, ag_send, ag_recv, early_sems = refs
        x, y, c = _place()
        me = 4 * x + 2 * y + c
        piece = small_mine.shape[0]
        piece_of = lambda slot: pl.ds(pl.multiple_of(slot * piece, SUBLANES), piece)
        rs_start, rs_finish = _exchange_steps(
            1, lambda a, slot: small_ref.at[piece_of(slot)], [small_land], rs_send, rs_recv)
        ag_start, ag_finish = _exchange_steps(1, lambda a, slot: small_mine, [small_all], ag_send, ag_recv)
        landed = [pltpu.make_async_copy(early_in[e][1], early_buf[e], early_sems.at[e, 0]) for e in range(ne)]
        for cp in landed:
            cp.start()
        sibling = (x, y, 1 - c)
        chips = [(1 - x, y), (x, 1 - y), (1 - x, 1 - y)]

        def rowwise(rows, fn):
            chunk = math.gcd(rows, 128)

            def step(i, _):
                fn(pl.ds(pl.multiple_of(i * chunk, chunk), chunk))
                return 0

            lax.fori_loop(0, rows // chunk, step, 0)

        stage1, local = [], []
        for a in range(n):
            cp = pltpu.make_async_remote_copy(
                src_ref=part[a].at[:, 1 - c], dst_ref=r1[a], send_sem=s1_send.at[a], recv_sem=s1_recv.at[a],
                device_id=sibling, device_id_type=MESH)
            cp.start()
            stage1.append(cp)
            lc = pltpu.make_async_copy(part[a].at[:, c], own[a], loc.at[a])
            lc.start()
            local.append(lc)
        rs_start()
        x_nbr, y_nbr = (1 - x, y, c), (x, 1 - y, c)
        mine, mine_x, mine_y = 2 * x + y, 2 * (1 - x) + y, 2 * x + (1 - y)

        def hop(a, k, src, dst, to):
            return pltpu.make_async_remote_copy(
                src_ref=src, dst_ref=dst, send_sem=h_send.at[a, k], recv_sem=h_recv.at[a, k],
                device_id=to, device_id_type=MESH)

        first, second = [], []
        for a in range(n):
            half = blks[a][0] // 2
            up, low = pl.ds(0, half), pl.ds(half, half)
            local[a].wait()
            stage1[a].wait_recv()
            for chip in range(4):

                def add(r, a=a, chip=chip):
                    own[a][chip, r, :] = own[a][chip, r, :] + r1[a][chip, r, :]

                rowwise(blks[a][0], add)
            first.append([
                hop(a, 0, own[a].at[pl.ds(2 * (1 - x), 2), up], got_a1[a], x_nbr),
                hop(a, 2, own[a].at[2 * x + (1 - y), low], got_b1[a].at[x], y_nbr),
                hop(a, 3, own[a].at[2 * (1 - x) + (1 - y), low], got_b1[a].at[1 - x], y_nbr),
            ])
            for cp in first[a]:
                cp.start()
        rs_finish()
        small_land[me] = small_ref[piece_of(me), :]

        def sum_piece(i, _):
            r = pl.ds(pl.multiple_of(i * SUBLANES, SUBLANES), SUBLANES)
            total = small_land[0, r, :]
            for dev in range(1, N_DEV):
                total = total + small_land[dev, r, :]
            small_mine[r, :] = total
            small_all[me, r, :] = total
            return 0

        lax.fori_loop(0, piece // SUBLANES, sum_piece, 0)
        ag_start()
        for a in range(n):
            half = blks[a][0] // 2
            first[a][0].wait_recv()

            def fold_upper(r, a=a):
                own[a][mine, r, :] = own[a][mine, r, :] + got_a1[a][y, r, :]
                pass_a[a][r, :] = own[a][mine_y, r, :] + got_a1[a][1 - y, r, :]

            rowwise(half, fold_upper)
            first[a][1].wait_recv()
            first[a][2].wait_recv()

            def fold_lower(r, a=a, half=half):
                rl = pl.ds(pl.multiple_of(r.start + half, SUBLANES), r.size)
                own[a][mine, rl, :] = own[a][mine, rl, :] + got_b1[a][x, r, :]
                pass_b[a][r, :] = own[a][mine_x, rl, :] + got_b1[a][1 - x, r, :]

            rowwise(half, fold_lower)
            second.append([hop(a, 1, pass_a[a], got_a2[a], y_nbr), hop(a, 4, pass_b[a], got_b2[a], x_nbr)])
            for cp in second[a]:
                cp.start()
        for e in range(ne):
            part_e, _ = early_in[e]
            landed[e].wait()
            own_block = pltpu.make_async_copy(part_e.at[me], early_buf[e].at[me], early_sems.at[e, 1])
            own_block.start()
            own_block.wait()

            def sum_early(r, e=e):
                g = early_buf[e][0, r, :]
                for dev in range(1, N_DEV):
                    g = g + early_buf[e][dev, r, :]
                early_out[e][r, :] = g

            rowwise(early_buf[e].shape[1], sum_early)
        for a in range(n):
            half = blks[a][0] // 2
            second[a][0].wait_recv()
            second[a][1].wait_recv()

            def finish_rows(r, a=a, half=half):
                rl = pl.ds(pl.multiple_of(r.start + half, SUBLANES), r.size)
                outs[a][r, :] = own[a][mine, r, :] + got_a2[a][r, :]
                outs[a][rl, :] = own[a][mine, rl, :] + got_b2[a][r, :]

            rowwise(half, finish_rows)
        ag_finish()
        for cp in stage1 + [cp for group in first + second for cp in group]:
            cp.wait_send()

    vmem = pl.BlockSpec(memory_space=pltpu.VMEM)
    hbm = pl.BlockSpec(memory_space=pl.ANY)
    piece = small.shape[0] // N_DEV
    assert piece * N_DEV == small.shape[0] and piece % SUBLANES == 0
    out_shape = [jax.ShapeDtypeStruct(b, F32) for b in blks]
    out_shape += [jax.ShapeDtypeStruct((N_DEV, piece, LANES), F32)]
    out_shape += [jax.ShapeDtypeStruct(e[0].shape[1:], F32) for e in early]
    halves = [(b[0] // 2, b[1]) for b in blks]
    scratch = (
        [pltpu.VMEM((4, *b), F32) for b in blks]
        + [pltpu.VMEM((4, *b), F32) for b in blks]
        + [pltpu.VMEM((2, *h), F32) for h in halves]
        + [pltpu.VMEM(h, F32) for h in halves]
        + [pltpu.VMEM((2, *h), F32) for h in halves]
        + [pltpu.VMEM(h, F32) for h in halves] * 3
        + [pltpu.VMEM(e[0].shape, F32) for e in early]
        + [pltpu.VMEM((N_DEV, piece, LANES), F32), pltpu.VMEM((piece, LANES), F32)]
        + [pltpu.SemaphoreType.DMA((n,)), pltpu.SemaphoreType.DMA((n,)), pltpu.SemaphoreType.DMA((n, 5)),
           pltpu.SemaphoreType.DMA((n, 5)), pltpu.SemaphoreType.DMA((n,))]
        + [pltpu.SemaphoreType.DMA((1, 7))] * 4
        + [pltpu.SemaphoreType.DMA((ne, 2))]
    )
    res = pl.pallas_call(
        body,
        name="reduce_all",
        out_shape=out_shape,
        in_specs=[hbm] * n + [vmem] + [hbm, hbm] * ne,
        out_specs=[vmem] * (n + 1 + ne),
        scratch_shapes=scratch,
        compiler_params=_params(),
    )(*parts4, small, *[t for e in early for t in e])
    return list(res[:n]) + list(res[n + 1 :]), res[n].reshape(small.shape)


def _rms(x):
    return lax.rsqrt(jnp.mean(x * x, axis=-1, keepdims=True) + EPS)


def _token_tile(tokens, want):
    tile = min(want, tokens // 2)
    assert tokens % tile == 0 and tile % 16 == 0
    return tile


def _in_proj(x2, g_pre, w_in_blk, shards):
    tokens, d = x2.shape
    nb = w_in_blk.shape[1]
    tm = _token_tile(tokens, IN_PROJ_ROWS)
    n_t = tokens // tm
    ns = len(shards)
    x_pos, y_pos, c_pos = _place()
    slot = lambda px, py, pc: 4 * px + 2 * py + pc
    chip_order = [(x_pos, y_pos), (1 - x_pos, y_pos), (x_pos, 1 - y_pos), (1 - x_pos, 1 - y_pos)]
    order = jnp.stack([2 * px + py for px, py in chip_order]).astype(jnp.int32)
    n_chips = len(chip_order)

    def body(order_ref, x_ref, g_ref, w_ref, *rest):
        shard_hbm, proj_ref, w_hbm = rest[:ns], rest[ns], rest[ns + 1]
        gathered = rest[ns + 2 : 2 * ns + 2]
        h_all, land, w_send, w_recv, out_sem, send_sems, recv_sems, own_sems = rest[2 * ns + 2 :]
        j, i = pl.program_id(0), pl.program_id(1)
        x, y, c = _place()
        me, sibling = (x, y, c), (x, y, 1 - c)
        chips = [(1 - x, y), (x, 1 - y), (1 - x, 1 - y)]
        own = [pltpu.make_async_copy(shard_hbm[a], gathered[a].at[slot(*me)], own_sems.at[a]) for a in range(ns)]
        start, finish = _exchange_steps(ns, lambda a, s: shard_hbm[a], gathered, send_sems, recv_sems)

        def copy(k, block, to):
            ref = land.at[slot(*block)]
            return pltpu.make_async_remote_copy(
                src_ref=ref, dst_ref=ref, send_sem=w_send.at[k], recv_sem=w_recv.at[k], device_id=to, device_id_type=MESH)

        first_sends = [copy(0, me, sibling)] + [copy(1 + k, me, (*chip, c)) for k, chip in enumerate(chips)]
        forwards = [copy(4 + k, (*chip, c), sibling) for k, chip in enumerate(chips)]

        @pl.when((j == 0) & (i == 0))
        def _():
            land[slot(*me)] = w_ref[...].astype(BF16)
            for cp in first_sends:
                cp.start()
            start()
            for cp in own:
                cp.start()
            copy(0, sibling, me).wait_recv()

        for k, chip in enumerate(chips):

            @pl.when((j == k) & (i == n_t - 1))
            def _(k=k, chip=chip):
                copy(1 + k, (*chip, c), me).wait_recv()
                forwards[k].start()

            @pl.when((j == k + 1) & (i == 0))
            def _(k=k, chip=chip):
                copy(4 + k, (*chip, 1 - c), me).wait_recv()

        rows = pl.ds(pl.multiple_of(i * tm, tm), tm)

        @pl.when(j == 0)
        def _():
            x_t = x_ref[...]
            h_all[rows, :] = (x_t * _rms(x_t) * g_ref[...]).astype(BF16)

        h = h_all[rows, :]
        for half in range(2):
            proj_ref[:, half * nb : (half + 1) * nb] = _mm(h, land[2 * order_ref[j] + half])

        @pl.when((j == n_chips - 1) & (i == n_t - 1))
        def _():
            for cp in first_sends + forwards:
                cp.wait_send()
            finish()
            for cp in own:
                cp.wait()
            outs = [pltpu.make_async_copy(land.at[blk], w_hbm.at[:, pl.ds(blk * nb, nb)], out_sem.at[blk])
                    for blk in range(N_DEV)]
            for cp in outs:
                cp.start()
            for cp in outs:
                cp.wait()

    hbm = pl.BlockSpec(memory_space=pl.ANY)
    res = pl.pallas_call(
        body,
        name="in_proj",
        grid_spec=pltpu.PrefetchScalarGridSpec(
            num_scalar_prefetch=1,
            grid=(n_chips, n_t),
            in_specs=[
                pl.BlockSpec((tm, d), lambda j, i, order: (jnp.where(j == 0, i, n_t - 1), 0)),
                pl.BlockSpec((1, d), lambda j, i, order: (0, 0)),
                pl.BlockSpec((d, nb), lambda j, i, order: (0, 0)),
            ]
            + [hbm] * ns,
            out_specs=[pl.BlockSpec((tm, 2 * nb), lambda j, i, order: (i, order[j])), hbm] + [hbm] * ns,
            scratch_shapes=[
                pltpu.VMEM((tokens, d), BF16),
                pltpu.VMEM((N_DEV, d, nb), BF16),
                pltpu.SemaphoreType.DMA((7,)),
                pltpu.SemaphoreType.DMA((7,)),
                pltpu.SemaphoreType.DMA((N_DEV,)),
                pltpu.SemaphoreType.DMA((ns, 7)),
                pltpu.SemaphoreType.DMA((ns, 7)),
                pltpu.SemaphoreType.DMA((ns,)),
            ],
        ),
        out_shape=[
            jax.ShapeDtypeStruct((tokens, N_DEV * nb), F32),
            jax.ShapeDtypeStruct((d, N_DEV * nb), BF16),
        ]
        + [jax.ShapeDtypeStruct((N_DEV, *a.shape), a.dtype) for a in shards],
        compiler_params=_params("arbitrary", "arbitrary"),
    )(order, x2, g_pre, w_in_blk, *shards)
    return res[0], res[1], res[2:]


def _in_proj_bwd(x2, dres, du_pool, du_ssm, dq, dgate, g_pre, w_in_f):
    tokens, d = x2.shape
    nb = w_in_f.shape[1] // N_DEV
    pool_w, ssm_w, att_w, mix = du_pool.shape[1], du_ssm.shape[1], dq.shape[1], dgate.shape[1]
    cl = _token_tile(tokens, TOKEN_ROWS)
    n_tiles = tokens // cl

    def body(x_ref, dres_ref, dup_ref, dus_ref, dq_ref, dgate_ref, g_ref, w_ref, gx_ref, gw_hbm, gg_ref, acc, sem):
        i = pl.program_id(0)

        @pl.when(i == 0)
        def _():
            acc[...] = jnp.zeros_like(acc)
            gg_ref[...] = jnp.zeros_like(gg_ref)

        x = x_ref[...]
        r = _rms(x)
        xn = x * r
        g = g_ref[...]
        h = (xn * g).astype(BF16)
        dproj = jnp.concatenate([dup_ref[...], dus_ref[...], dq_ref[...], dgate_ref[...]], axis=1).astype(BF16)
        dh = _mm_nt(dproj, w_ref[...])
        for j in range(N_DEV):
            acc[j] += _mm_tn(h, dproj[:, j * nb : (j + 1) * nb])
        gg_ref[...] += jnp.sum(dh * xn, axis=0, keepdims=True)
        dxn = dh * g
        gx_ref[...] = dres_ref[...] + r * (dxn - xn * jnp.mean(dxn * xn, axis=-1, keepdims=True))

        @pl.when(i == n_tiles - 1)
        def _():
            cp = pltpu.make_async_copy(acc, gw_hbm, sem)
            cp.start()
            cp.wait()

    return pl.pallas_call(
        body,
        name="in_proj_bwd",
        grid=(n_tiles,),
        in_specs=[
            pl.BlockSpec((cl, d), lambda i: (i, 0)),
            pl.BlockSpec((cl, d), lambda i: (i, 0)),
            pl.BlockSpec((cl, pool_w), lambda i: (i, 0)),
            pl.BlockSpec((cl, ssm_w), lambda i: (i, 0)),
            pl.BlockSpec((cl, att_w), lambda i: (i, 0)),
            pl.BlockSpec((cl, mix), lambda i: (i, 0)),
            pl.BlockSpec((1, d), lambda i: (0, 0)),
            pl.BlockSpec(w_in_f.shape, lambda i: (0, 0)),
        ],
        out_specs=[
            pl.BlockSpec((cl, d), lambda i: (i, 0)),
            pl.BlockSpec(memory_space=pl.ANY),
            pl.BlockSpec((1, d), lambda i: (0, 0)),
        ],
        out_shape=[
            jax.ShapeDtypeStruct((tokens, d), F32),
            jax.ShapeDtypeStruct((N_DEV, d, nb), F32),
            jax.ShapeDtypeStruct((1, d), F32),
        ],
        scratch_shapes=[pltpu.VMEM((N_DEV, d, nb), F32), pltpu.SemaphoreType.DMA],
        compiler_params=_params("arbitrary"),
    )(x2, dres, du_pool, du_ssm, dq, dgate, g_pre, w_in_f)


def _pool_geometry(seq, width):
    gw = width // len(POOL_WINDOWS)
    col = lax.broadcasted_iota(jnp.int32, (1, width), 1)
    win = jnp.full((1, width), float(POOL_WINDOWS[-1]), F32)
    for gi in range(len(POOL_WINDOWS) - 2, -1, -1):
        win = jnp.where(col < (gi + 1) * gw, float(POOL_WINDOWS[gi]), win)
    row = lax.broadcasted_iota(jnp.int32, (seq, width), 0)
    filling = 1.0 / (lax.broadcasted_iota(jnp.int32, (seq, 1), 0) + 1).astype(F32)
    inv_cnt = jnp.where(row + 1 < win.astype(jnp.int32), filling, 1.0 / win)
    return win, row, inv_cnt


def _window_sums(a, win, seq, back):
    pad = 2 * POOL_WINDOWS[-1]
    zeros = jnp.zeros((pad, a.shape[1]), F32)
    s = jnp.concatenate([a, zeros] if back else [zeros, a], axis=0)
    sums = []
    k = 1
    while k < POOL_WINDOWS[-1]:
        s = s + pltpu.roll(s, seq + pad - k if back else k, 0)
        k *= 2
        sums.append((k, s))
    out = sums[-1][1]
    for k, s in reversed(sums[:-1]):
        out = jnp.where(win <= float(k), s, out)
    return out[0:seq] if back else out[pad : pad + seq]


def _state_row(z, n_blocks):
    re = jnp.real(z).reshape(n_blocks, -1)
    im = jnp.imag(z).reshape(n_blocks, -1)
    return jnp.concatenate([re, im], axis=1).reshape(1, -1)


def _ssm_tables(a_re, a_im, log_dt, b_re, b_im, c_re, c_im):
    groups, n_state = a_re.shape
    ch = b_re.shape[2]
    nb = groups * ch // LANES
    gl = groups // nb
    lam = lax.complex(a_re, a_im)
    lam_bar = jnp.exp(lam * jnp.exp(log_dt)[:, None])
    b_bar = ((lam_bar - 1.0) / lam)[..., None] * lax.complex(b_re, b_im)
    eye = jnp.eye(gl, dtype=F32)

    def rows_to_state(t):
        return jnp.einsum("sgnc,gh->sgchn", t.reshape(nb, gl, n_state, ch), eye).reshape(nb, gl * ch, gl * n_state)

    def state_to_rows(t):
        return jnp.einsum("sgcn,gh->shngc", t.reshape(nb, gl, ch, n_state), eye).reshape(nb, gl * n_state, gl * ch)

    b_tab = jnp.concatenate([rows_to_state(jnp.real(b_bar)), rows_to_state(jnp.imag(b_bar))], axis=2)
    c_tab = jnp.concatenate([state_to_rows(c_re), -state_to_rows(c_im)], axis=1)
    return _state_row(lam_bar, nb), b_tab, c_tab


def _lam_power(a_re, a_im, log_dt, power, scale, n_blocks):
    return _state_row(scale * jnp.exp(lax.complex(a_re, a_im) * jnp.exp(log_dt)[:, None] * power), n_blocks)


def _state_blocks(s2, n_blocks, width):
    half = s2 // n_blocks // 2
    assert half % width == 0
    return [(b * 2 * half + o, b * 2 * half + half + o) for b in range(n_blocks) for o in range(0, half, width)]


def _scan(src_ref, dst_ref, st_ref, lam8_ref, n_groups, s, n_blocks, reverse, store):
    lb = SCAN_LANES
    for re0, im0 in _state_blocks(2 * s, n_blocks, lb):
        cr, ci = pl.ds(re0, lb), pl.ds(im0, lb)
        lr = lam8_ref[:, cr]
        li = -lam8_ref[:, ci] if reverse else lam8_ref[:, ci]

        unroll = 1 if store else SCAN_UNROLL

        def step(i, carry, cr=cr, ci=ci, lr=lr, li=li):
            hr, hi = carry
            for k in range(unroll):
                grp = i * unroll + k
                grp = n_groups - 1 - grp if reverse else grp
                rows = pl.ds(pl.multiple_of(grp * SUBLANES, SUBLANES), SUBLANES)
                hr, hi = (lr * hr - li * hi + src_ref[rows, cr], lr * hi + li * hr + src_ref[rows, ci])
                if store:
                    dst_ref[rows, cr] = hr
                    dst_ref[rows, ci] = hi
            return hr, hi

        assert n_groups % unroll == 0
        hr, hi = lax.fori_loop(0, n_groups // unroll, step, (st_ref[:, cr], st_ref[:, ci]))
        st_ref[:, cr] = hr
        st_ref[:, ci] = hi


def _pack_state(re, im):
    hi = lax.bitcast_convert_type(re.astype(BF16).astype(F32), jnp.uint32)
    lo = lax.bitcast_convert_type(im.astype(BF16).astype(F32), jnp.uint32)
    return hi | (lo >> 16)


def _unpack_state(word):
    re = lax.bitcast_convert_type(word & jnp.uint32(0xFFFF0000), F32)
    im = lax.bitcast_convert_type(word << 16, F32)
    return re, im


def _lam_adjoint(g_ref, hprev_ref, group0, acc_ref, n_groups, s, n_blocks):
    lb = SCAN_LANES
    half = s // n_blocks
    assert n_groups % SCAN_UNROLL == 0
    for re0, im0 in _state_blocks(2 * s, n_blocks, lb):
        cr, ci = pl.ds(re0, lb), pl.ds(im0, lb)
        ch = pl.ds(re0 // (2 * half) * half + re0 % (2 * half), lb)

        def step(i, carry, cr=cr, ci=ci, ch=ch):
            ar, ai = carry
            for k in range(SCAN_UNROLL):
                grp = i * SCAN_UNROLL + k
                rows = pl.ds(pl.multiple_of(grp * SUBLANES, SUBLANES), SUBLANES)
                gr, gi = g_ref[rows, cr], g_ref[rows, ci]
                hr, hi = _unpack_state(hprev_ref[pl.ds(pl.multiple_of((group0 + grp) * SUBLANES, SUBLANES), SUBLANES), ch])
                ar = ar + hr * gr + hi * gi
                ai = ai + hr * gi - hi * gr
            return ar, ai

        ar, ai = lax.fori_loop(0, n_groups // SCAN_UNROLL, step, (acc_ref[:, cr], acc_ref[:, ci]))
        acc_ref[:, cr] = ar
        acc_ref[:, ci] = ai


def _chunk_starts(st_ref, init_ref, lcl_ref, s, n_blocks):
    w = s // n_blocks
    init_ref[0:1, :] = jnp.zeros((1, 2 * s), F32)
    for re0, im0 in _state_blocks(2 * s, n_blocks, w):
        re, im = pl.ds(re0, w), pl.ds(im0, w)
        ar, ai = lcl_ref[:, re], lcl_ref[:, im]
        cr = jnp.zeros((1, w), F32)
        ci = jnp.zeros((1, w), F32)
        for k in range(1, NCH):
            cr, ci = (ar * cr - ai * ci + st_ref[k - 1 : k, re], ar * ci + ai * cr + st_ref[k - 1 : k, im])
            init_ref[k : k + 1, re] = cr
            init_ref[k : k + 1, im] = ci


def _chunk_starts_adjoint(stg_ref, initg_ref, lcl_ref, s, n_blocks):
    w = s // n_blocks
    initg_ref[NCH - 1 : NCH, :] = jnp.zeros((1, 2 * s), F32)
    for re0, im0 in _state_blocks(2 * s, n_blocks, w):
        re, im = pl.ds(re0, w), pl.ds(im0, w)
        ar, ai = lcl_ref[:, re], -lcl_ref[:, im]
        gr = jnp.zeros((1, w), F32)
        gi = jnp.zeros((1, w), F32)
        for k in range(NCH - 2, -1, -1):
            gr, gi = (stg_ref[k + 1 : k + 2, re] + ar * gr - ai * gi, stg_ref[k + 1 : k + 2, im] + ar * gi + ai * gr)
            initg_ref[k : k + 1, re] = gr
            initg_ref[k : k + 1, im] = gi


def _ssm_rows(seq, want):
    rows = min(want, seq // 2)
    assert seq % rows == 0 and rows % SUBLANES == 0
    return rows


def _chunk_copies(hbm_ref, b, cm_ref, sems, to_cm, col0=0):
    cl, _, width = cm_ref.shape
    copies = []
    for k in range(NCH):
        nat, cm = hbm_ref.at[b, pl.ds(k * cl, cl), pl.ds(col0, width)], cm_ref.at[:, k, :]
        src, dst = (nat, cm) if to_cm else (cm, nat)
        copies.append(pltpu.make_async_copy(src, dst, sems.at[k]))
    return copies


def _blockwise(fn, n_blocks):
    return jnp.concatenate([fn(b) for b in range(n_blocks)], axis=1)


def _ssm_fwd(u, u_col, b_tab, c_tab, lam8, lcl, d_skip, w_glu, shards):
    batch, seq, _ = u.shape
    ns = len(shards)
    width = d_skip.shape[1]
    s = lam8.shape[1] // 2
    nb = b_tab.shape[0]
    sb = 2 * s // nb
    cl = seq // NCH
    rows = _ssm_rows(seq, SSM_ROWS)
    n_tiles = seq // rows
    n_groups = rows // SUBLANES

    def body(u_hbm, b_ref, c_ref, lam_ref, lcl_ref, d_ref, wg_ref, *rest):
        shard_hbm, (y_hbm, pre_ref, z_ref, init_ref) = rest[:ns], rest[ns : ns + 4]
        gathered = rest[ns + 4 : 2 * ns + 4]
        u_cm, y_cm, bu_all, st, sems, send_sems, recv_sems, own_sems = rest[2 * ns + 4 :]
        b, ph, t = pl.program_id(0), pl.program_id(1), pl.program_id(2)
        tile_groups = pl.ds(pl.multiple_of(t * n_groups, n_groups), n_groups)
        x_pos, y_pos, c_pos = _place()
        own = [pltpu.make_async_copy(shard_hbm[a], gathered[a].at[4 * x_pos + 2 * y_pos + c_pos], own_sems.at[a])
               for a in range(ns)]
        exchange_start, exchange_finish = _exchange_steps(
            ns, lambda a, slot: shard_hbm[a], gathered, send_sems, recv_sems)

        @pl.when((b == 0) & (ph == 0) & (t == 0))
        def _():
            exchange_start()
            for cp in own:
                cp.start()

        @pl.when((b == batch - 1) & (ph == 1) & (t == n_tiles - 1))
        def _():
            exchange_finish()
            for cp in own:
                cp.wait()

        @pl.when((ph == 0) & (t == 0))
        def _():
            loads = _chunk_copies(u_hbm, b, u_cm, sems, True, u_col)
            for cp in loads:
                cp.start()
            st[...] = jnp.zeros_like(st)
            for cp in loads:
                cp.wait()

        @pl.when((ph == 1) & (t == 0))
        def _():
            st[...] = init_ref[...]

        u_t = u_cm[tile_groups].reshape(rows, width)
        bu = bu_all.at[pl.ds(pl.multiple_of(t * rows, rows), rows)]

        @pl.when(ph == 0)
        def _():
            u_b = u_t.astype(BF16)
            for blk in range(nb):
                bu[:, blk * sb : (blk + 1) * sb] = _mm(u_b[:, blk * LANES : (blk + 1) * LANES], b_ref[blk])
            _scan(bu, bu, st, lam_ref, n_groups, s, nb, False, False)

        @pl.when((ph == 0) & (t == n_tiles - 1))
        def _():
            _chunk_starts(st, init_ref, lcl_ref, s, nb)

        @pl.when(ph == 1)
        def _():
            _scan(bu, bu, st, lam_ref, n_groups, s, nb, False, True)
            hs = lambda blk: _mm(bu[:, blk * sb : (blk + 1) * sb].astype(BF16), c_ref[blk])
            pre = _blockwise(hs, nb) + d_ref[...] * u_t
            z = _mm(_gelu(pre).astype(BF16), wg_ref[...])
            pre_ref[...] = pre
            z_ref[...] = z
            y = z[:, 0:width] * jax.nn.sigmoid(z[:, width : 2 * width])
            y_cm[tile_groups] = y.reshape(n_groups, SUBLANES, width)

        @pl.when((ph == 1) & (t == n_tiles - 1))
        def _():
            stores = _chunk_copies(y_hbm, b, y_cm, sems, False)
            for cp in stores:
                cp.start()
            for cp in stores:
                cp.wait()

    out_tile = lambda b, ph, t: (b, t * ph, 0)
    full = lambda a: pl.BlockSpec(a.shape, lambda b, ph, t: (0,) * a.ndim)
    hbm = pl.BlockSpec(memory_space=pl.ANY)
    res = pl.pallas_call(
        body,
        name="ssm_fwd",
        grid=(batch, 2, n_tiles),
        in_specs=[hbm, full(b_tab), full(c_tab), full(lam8), full(lcl), full(d_skip), full(w_glu)] + [hbm] * ns,
        out_specs=[
            hbm,
            pl.BlockSpec((None, rows, width), out_tile),
            pl.BlockSpec((None, rows, 2 * width), out_tile),
            pl.BlockSpec((None, SUBLANES, 2 * s), lambda b, ph, t: (b, 0, 0)),
        ]
        + [hbm] * ns,
        out_shape=[
            jax.ShapeDtypeStruct((batch, seq, width), F32),
            jax.ShapeDtypeStruct((batch, seq, width), F32),
            jax.ShapeDtypeStruct((batch, seq, 2 * width), F32),
            jax.ShapeDtypeStruct((batch, SUBLANES, 2 * s), F32),
        ]
        + [jax.ShapeDtypeStruct((N_DEV, *a.shape), a.dtype) for a in shards],
        scratch_shapes=[
            pltpu.VMEM((cl, NCH, width), F32),
            pltpu.VMEM((cl, NCH, width), F32),
            pltpu.VMEM((seq, 2 * s), F32),
            pltpu.VMEM((SUBLANES, 2 * s), F32),
            pltpu.SemaphoreType.DMA((NCH,)),
            pltpu.SemaphoreType.DMA((ns, 7)),
            pltpu.SemaphoreType.DMA((ns, 7)),
            pltpu.SemaphoreType.DMA((ns,)),
        ],
        compiler_params=_params("arbitrary", "arbitrary", "arbitrary"),
    )(u, b_tab, c_tab, lam8, lcl, d_skip, w_glu, *shards)
    return res[:4], res[4:]


def _ssm_bwd(u, u_col, pre_p, z_p, dy, init, b_tab, b_tab_t, c_tab_t, lam8, lcl, d_skip, w_glu, ready):
    batch, seq, _ = u.shape
    width = d_skip.shape[1]
    nr = len(ready)
    s = lam8.shape[1] // 2
    nb = b_tab.shape[0]
    sb = 2 * s // nb
    cl = seq // NCH
    rows = _ssm_rows(seq, SSM_ROWS)
    n_tiles = seq // rows
    n_groups = rows // SUBLANES

    def body(u_hbm, pre_ref, z_ref, dy_hbm, init_ref, b_ref, bt_ref, ct_ref, lam_ref, lcl_ref, d_ref, wg_ref, *rest):
        ready_hbm, rest = rest[:nr], rest[nr:]
        du_hbm, gb_ref, gc_ref, gwg_ref, gd_ref, glam_ref = rest[:6]
        landed_hbm, rest = rest[6 : 6 + nr], rest[6 + nr :]
        u_cm, dy_cm, work, hs_all, dpre_all, st, stg, initg, acc, sems, send_sems, recv_sems = rest
        b, ph, t = pl.program_id(0), pl.program_id(1), pl.program_id(2)
        half = s // nb
        exchange_start, exchange_finish = _exchange_steps(
            nr, lambda a, slot: ready_hbm[a].at[slot], landed_hbm, send_sems, recv_sems)
        first = (b == 0) & (ph == 0) & (t == 0)
        last = (b == batch - 1) & (ph == 2) & (t == n_tiles - 1)
        tile = jnp.where(ph == 0, t, n_tiles - 1 - t)
        tile_rows = pl.ds(pl.multiple_of(tile * rows, rows), rows)
        tile_groups = pl.ds(pl.multiple_of(tile * n_groups, n_groups), n_groups)
        lanes = lambda blk: slice(blk * LANES, (blk + 1) * LANES)
        states = lambda blk: slice(blk * sb, (blk + 1) * sb)

        @pl.when(first)
        def _():
            exchange_start()
            acc[...] = jnp.zeros_like(acc)
            gb_ref[...] = jnp.zeros_like(gb_ref)
            gc_ref[...] = jnp.zeros_like(gc_ref)
            gwg_ref[...] = jnp.zeros_like(gwg_ref)
            gd_ref[...] = jnp.zeros_like(gd_ref)

        @pl.when((ph == 0) & (t == 0))
        def _():
            loads = (_chunk_copies(u_hbm, b, u_cm, sems.at[0], True, u_col)
                     + _chunk_copies(dy_hbm, b, dy_cm, sems.at[1], True))
            for cp in loads:
                cp.start()
            st[...] = init_ref[...]
            for blk in range(nb):
                entry = init_ref[:, states(blk)]
                hs_all[0:SUBLANES, blk * half : (blk + 1) * half] = _pack_state(entry[:, 0:half], entry[:, half : 2 * half])
            for cp in loads:
                cp.wait()

        u_t = u_cm[tile_groups].reshape(rows, width)
        u_b = u_t.astype(BF16)

        @pl.when(ph == 0)
        def _():
            for blk in range(nb):
                work[:, states(blk)] = _mm(u_b[:, lanes(blk)], b_ref[blk])
            _scan(work, work, st, lam_ref, n_groups, s, nb, False, True)
            z = z_ref[...]
            dy_t = dy_cm[tile_groups].reshape(rows, width)
            pre = pre_ref[...]
            z1, sig = z[:, 0:width], jax.nn.sigmoid(z[:, width : 2 * width])
            dz = jnp.concatenate([dy_t * sig, dy_t * z1 * sig * (1.0 - sig)], axis=1).astype(BF16)
            gwg_ref[...] += _mm_tn(_gelu(pre).astype(BF16), dz)
            dpre = _mm_nt(dz, wg_ref[...]) * _gelu_grad(pre)
            dpre_all[tile_rows, :] = dpre
            gd_ref[...] += jnp.sum(dpre * u_t, axis=0, keepdims=True)
            dpre_b = dpre.astype(BF16)
            kept = pl.ds(pl.multiple_of(tile * rows + SUBLANES, SUBLANES), rows)
            for blk in range(nb):
                hs = work[:, states(blk)]
                gc_ref[blk] += _mm_tn(dpre_b[:, lanes(blk)], hs.astype(BF16))
                hs_all[kept, blk * half : (blk + 1) * half] = _pack_state(hs[:, 0:half], hs[:, half : 2 * half])

        @pl.when(ph >= 1)
        def _():
            dpre_b = dpre_all[tile_rows, :].astype(BF16)
            for blk in range(nb):
                work[:, states(blk)] = _mm(dpre_b[:, lanes(blk)], ct_ref[blk])

        @pl.when(ph == 1)
        def _():
            @pl.when(t == 0)
            def _():
                stg[...] = jnp.zeros_like(stg)

            _scan(work, work, stg, lam_ref, n_groups, s, nb, True, False)

            @pl.when(t == n_tiles - 1)
            def _():
                _chunk_starts_adjoint(stg, initg, lcl_ref, s, nb)

        @pl.when(ph == 2)
        def _():
            @pl.when(t == 0)
            def _():
                stg[...] = initg[...]

            _scan(work, work, stg, lam_ref, n_groups, s, nb, True, True)
            _lam_adjoint(work, hs_all, tile * n_groups, acc, n_groups, s, nb)
            du = lambda blk: _mm(work[:, states(blk)].astype(BF16), bt_ref[blk])
            du_t = _blockwise(du, nb) + dpre_all[tile_rows, :] * d_ref[...]
            dy_cm[tile_groups] = du_t.reshape(n_groups, SUBLANES, width)
            for blk in range(nb):
                gb_ref[blk] += _mm_tn(u_b[:, lanes(blk)], work[:, states(blk)].astype(BF16))

            @pl.when(t == n_tiles - 1)
            def _():
                stores = _chunk_copies(du_hbm, b, dy_cm, sems.at[0], False)
                for cp in stores:
                    cp.start()
                for cp in stores:
                    cp.wait()

        @pl.when(last)
        def _():
            glam_ref[...] = jnp.sum(acc[...], axis=0, keepdims=True)
            exchange_finish()

    def tile(b, ph, t):
        return (b, jnp.where(ph == 0, t, n_tiles - 1 - t), 0)

    full = lambda a: pl.BlockSpec(a.shape, lambda b, ph, t: (0,) * a.ndim)
    hbm = pl.BlockSpec(memory_space=pl.ANY)
    res = pl.pallas_call(
        body,
        name="ssm_bwd",
        grid=(batch, 3, n_tiles),
        in_specs=[
            hbm,
            pl.BlockSpec((None, rows, width), tile),
            pl.BlockSpec((None, rows, 2 * width), tile),
            hbm,
            pl.BlockSpec((None, SUBLANES, 2 * s), lambda b, ph, t: (b, 0, 0)),
            full(b_tab), full(b_tab_t), full(c_tab_t), full(lam8), full(lcl), full(d_skip), full(w_glu),
        ]
        + [hbm] * nr,
        out_specs=[
            hbm,
            full(b_tab), full(b_tab), full(w_glu), full(d_skip),
            pl.BlockSpec((1, 2 * s), lambda b, ph, t: (0, 0)),
        ]
        + [hbm] * nr,
        out_shape=[
            jax.ShapeDtypeStruct((batch, seq, width), F32),
            jax.ShapeDtypeStruct(b_tab.shape, F32),
            jax.ShapeDtypeStruct(b_tab.shape, F32),
            jax.ShapeDtypeStruct(w_glu.shape, F32),
            jax.ShapeDtypeStruct(d_skip.shape, F32),
            jax.ShapeDtypeStruct((1, 2 * s), F32),
        ]
        + [jax.ShapeDtypeStruct(a.shape, F32) for a in ready],
        scratch_shapes=[
            pltpu.VMEM((cl, NCH, width), F32),
            pltpu.VMEM((cl, NCH, width), F32),
            pltpu.VMEM((rows, 2 * s), F32),
            pltpu.VMEM((seq + SUBLANES, s), jnp.uint32),
            pltpu.VMEM((seq, width), F32),
        ]
        + [pltpu.VMEM((SUBLANES, 2 * s), F32)] * 4
        + [pltpu.SemaphoreType.DMA((2, NCH)), pltpu.SemaphoreType.DMA((nr, 7)), pltpu.SemaphoreType.DMA((nr, 7))],
        compiler_params=_params("arbitrary", "arbitrary", "arbitrary"),
    )(u, pre_p, z_p, dy, init, b_tab, b_tab_t, c_tab_t, lam8, lcl, d_skip, w_glu, *ready)
    return res[:6], res[6:]


def _pool_kv_fwd(u2, wp_blk, scale, seq, mem, g_mem, w_kv):
    batch, n_mem, d = mem.shape
    kvw = w_kv.shape[1]
    width = scale.shape[1]

    def body(u_ref, wp_ref, s_ref, mem_ref, g_ref, w_ref, y_ref, diff_ref, kv_ref):
        u = u_ref[...]
        win, row, inv_cnt = _pool_geometry(seq, width)
        diff = (_window_sums(u, win, seq, False) * inv_cnt - u).astype(BF16)
        diff_ref[...] = diff
        y_ref[...] = _mm(diff, wp_ref[...]) * s_ref[...]
        m = mem_ref[...]
        kv_ref[...] = _mm((m * _rms(m) * g_ref[...]).astype(BF16), w_ref[...])

    return pl.pallas_call(
        body,
        name="pool_kv_fwd",
        grid=(batch,),
        in_specs=[
            pl.BlockSpec((seq, width), lambda b: (b, 0)),
            pl.BlockSpec((width, width), lambda b: (0, 0)),
            pl.BlockSpec((1, width), lambda b: (0, 0)),
            pl.BlockSpec((None, n_mem, d), lambda b: (b, 0, 0)),
            pl.BlockSpec((1, d), lambda b: (0, 0)),
            pl.BlockSpec((d, kvw), lambda b: (0, 0)),
        ],
        out_specs=[
            pl.BlockSpec((seq, width), lambda b: (b, 0)),
            pl.BlockSpec((seq, width), lambda b: (b, 0)),
            pl.BlockSpec((None, n_mem, kvw), lambda b: (b, 0, 0)),
        ],
        out_shape=[
            jax.ShapeDtypeStruct((u2.shape[0], width), F32),
            jax.ShapeDtypeStruct((u2.shape[0], width), BF16),
            jax.ShapeDtypeStruct((batch, n_mem, kvw), F32),
        ],
        compiler_params=_params("arbitrary"),
    )(u2, wp_blk, scale, mem, g_mem, w_kv)


def _pool_kv_bwd(diff2, dy2, wp_blk, scale, seq, mem, dkv, g_mem, w_kv):
    batch, n_mem, d = mem.shape
    kvw = w_kv.shape[1]
    width = scale.shape[1]

    def body(diff_ref, dy_ref, wp_ref, s_ref, mem_ref, dkv_ref, g_ref, w_ref, du_ref, gwp_ref, gs_ref, gw_ref, gg_ref):
        @pl.when(pl.program_id(0) == 0)
        def _():
            gwp_ref[...] = jnp.zeros_like(gwp_ref)
            gs_ref[...] = jnp.zeros_like(gs_ref)
            gw_ref[...] = jnp.zeros_like(gw_ref)
            gg_ref[...] = jnp.zeros_like(gg_ref)

        diff = diff_ref[...]
        dy = dy_ref[...]
        win, row, inv_cnt = _pool_geometry(seq, width)
        gs_ref[...] += jnp.sum(dy * _mm(diff, wp_ref[...]), axis=0, keepdims=True)
        dys = (dy * s_ref[...]).astype(BF16)
        gwp_ref[...] += _mm_tn(diff, dys)
        dd = _mm_nt(dys, wp_ref[...])
        du_ref[...] = _window_sums(dd * inv_cnt, win, seq, True) - dd

        m = mem_ref[...]
        mn = m * _rms(m)
        dkv_b = dkv_ref[...].astype(BF16)
        gw_ref[...] += _mm_tn((mn * g_ref[...]).astype(BF16), dkv_b)
        gg_ref[...] += jnp.sum(_mm_nt(dkv_b, w_ref[...]) * mn, axis=0, keepdims=True)

    return pl.pallas_call(
        body,
        name="pool_kv_bwd",
        grid=(batch,),
        in_specs=[
            pl.BlockSpec((seq, width), lambda b: (b, 0)),
            pl.BlockSpec((seq, width), lambda b: (b, 0)),
            pl.BlockSpec((width, width), lambda b: (0, 0)),
            pl.BlockSpec((1, width), lambda b: (0, 0)),
            pl.BlockSpec((None, n_mem, d), lambda b: (b, 0, 0)),
            pl.BlockSpec((None, n_mem, kvw), lambda b: (b, 0, 0)),
            pl.BlockSpec((1, d), lambda b: (0, 0)),
            pl.BlockSpec((d, kvw), lambda b: (0, 0)),
        ],
        out_specs=[
            pl.BlockSpec((seq, width), lambda b: (b, 0)),
            pl.BlockSpec((width, width), lambda b: (0, 0)),
            pl.BlockSpec((1, width), lambda b: (0, 0)),
            pl.BlockSpec((d, kvw), lambda b: (0, 0)),
            pl.BlockSpec((1, d), lambda b: (0, 0)),
        ],
        out_shape=[
            jax.ShapeDtypeStruct(dy2.shape, F32),
            jax.ShapeDtypeStruct((width, width), F32),
            jax.ShapeDtypeStruct((1, width), F32),
            jax.ShapeDtypeStruct((d, kvw), F32),
            jax.ShapeDtypeStruct((1, d), F32),
        ],
        compiler_params=_params("arbitrary"),
    )(diff2, dy2, wp_blk, scale, mem, dkv, g_mem, w_kv)


def _tail(x2, target2, proj, y_pool, y_ssm, kv, w_out, g_post):
    tokens, d = x2.shape
    batch, n_mem, kvw = kv.shape
    pool_w, ssm_w, att_w, mix = y_pool.shape[1], y_ssm.shape[1], kvw // 2, w_out.shape[0]
    assert (mix - att_w) % att_w == 0 and proj.shape[1] == 2 * mix
    hd = att_w // MEM_HEADS
    cl = _token_tile(tokens // batch, TOKEN_ROWS)
    n_tiles = tokens // cl
    per_seq = tokens // batch // cl
    qk_scale = hd**-0.5

    def body(x_ref, tg_ref, gate_ref, yp_ref, ys_ref, q_ref, kv_ref, w_ref, g_ref,
             dres_ref, dgate_ref, dyp_ref, dys_ref, dq_ref, dkv_ref, gw_hbm, gg_ref, loss_ref, acc, sem):
        i = pl.program_id(0)

        @pl.when(i == 0)
        def _():
            acc[...] = jnp.zeros_like(acc)
            gg_ref[...] = jnp.zeros_like(gg_ref)
            loss_ref[...] = jnp.zeros_like(loss_ref)

        @pl.when(i % per_seq == 0)
        def _():
            dkv_ref[...] = jnp.zeros_like(dkv_ref)

        k = kv_ref[:, 0:att_w].astype(BF16)
        v = kv_ref[:, att_w : 2 * att_w].astype(BF16)
        lane = lax.broadcasted_iota(jnp.int32, (1, att_w), 1)
        heads = [(lane >= h * hd) & (lane < (h + 1) * hd) for h in range(MEM_HEADS)]
        g = g_ref[...]

        def part(rows):
            n_rows = rows.stop - rows.start
            q = q_ref[rows, :]
            probs, q_heads = [], []
            att = jnp.zeros((n_rows, att_w), F32)
            for mask in heads:
                qh = jnp.where(mask, q, 0.0).astype(BF16)
                sc = _mm_nt(qh, k) * qk_scale
                e = jnp.exp(sc - jnp.max(sc, axis=-1, keepdims=True))
                p = e * (1.0 / jnp.sum(e, axis=-1, keepdims=True))
                att = att + jnp.where(mask, _mm(p.astype(BF16), v), 0.0)
                probs.append(p)
                q_heads.append(qh)

            ycat = jnp.concatenate([yp_ref[rows, :], ys_ref[rows, :], att], axis=1)
            gate = gate_ref[rows, :]
            sig = jax.nn.sigmoid(gate)
            silu = gate * sig
            yg = (ycat * silu).astype(BF16)
            out = _mm(yg, w_ref[...])
            r = _rms(out)
            on = out * r
            err = x_ref[rows, :] + on * g - tg_ref[rows, :]
            loss_ref[...] += 0.5 * jnp.sum(jnp.mean(err * err, axis=-1, keepdims=True), axis=0, keepdims=True)
            dres = err * (1.0 / d)
            dres_ref[rows, :] = dres
            gg_ref[...] += jnp.sum(dres * on, axis=0, keepdims=True)
            don = dres * g
            dout = (r * (don - on * jnp.mean(don * on, axis=-1, keepdims=True))).astype(BF16)
            acc[...] += _mm_tn(yg, dout)
            dyg = _mm_nt(dout, w_ref[...])
            dgate_ref[rows, :] = dyg * ycat * (sig * (1.0 + gate * (1.0 - sig)))
            dycat = dyg * silu
            dyp_ref[rows, :] = dycat[:, 0:pool_w]
            dys_ref[rows, :] = dycat[:, pool_w : pool_w + ssm_w]
            datt = dycat[:, pool_w + ssm_w : mix]

            dq = jnp.zeros((n_rows, att_w), F32)
            dk = jnp.zeros((n_mem, att_w), F32)
            dv = jnp.zeros((n_mem, att_w), F32)
            for mask, p, qh in zip(heads, probs, q_heads):
                doh = jnp.where(mask, datt, 0.0).astype(BF16)
                dp = _mm_nt(doh, v)
                ds = (p * (dp - jnp.sum(p * dp, axis=-1, keepdims=True)) * qk_scale).astype(BF16)
                dq = dq + jnp.where(mask, _mm(ds, k), 0.0)
                dk = dk + _mm_tn(ds, qh)
                dv = dv + _mm_tn(p.astype(BF16), doh)
            dq_ref[rows, :] = dq
            dkv_ref[:, 0:att_w] += dk
            dkv_ref[:, att_w : 2 * att_w] += dv

        part(slice(0, cl))

        @pl.when(i == n_tiles - 1)
        def _():
            cp = pltpu.make_async_copy(acc, gw_hbm, sem)
            cp.start()
            cp.wait()

    tok = lambda w: pl.BlockSpec((cl, w), lambda i: (i, 0))
    chunked = tok(ssm_w)
    per_batch = pl.BlockSpec((None, n_mem, kvw), lambda i: (i // per_seq, 0, 0))
    return pl.pallas_call(
        body,
        name="tail",
        grid=(n_tiles,),
        in_specs=[
            tok(d), tok(d), pl.BlockSpec((cl, mix), lambda i: (i, 1)), tok(pool_w), chunked,
            pl.BlockSpec((cl, att_w), lambda i: (i, (mix - att_w) // att_w)), per_batch,
            pl.BlockSpec((mix, d), lambda i: (0, 0)),
            pl.BlockSpec((1, d), lambda i: (0, 0)),
        ],
        out_specs=[
            tok(d), tok(mix), tok(pool_w), chunked, tok(att_w), per_batch,
            pl.BlockSpec(memory_space=pl.ANY),
            pl.BlockSpec((1, d), lambda i: (0, 0)),
            pl.BlockSpec((1, 1), lambda i: (0, 0)),
        ],
        out_shape=[
            jax.ShapeDtypeStruct((tokens, d), F32),
            jax.ShapeDtypeStruct((tokens, mix), F32),
            jax.ShapeDtypeStruct((tokens, pool_w), F32),
            jax.ShapeDtypeStruct((tokens, ssm_w), F32),
            jax.ShapeDtypeStruct((tokens, att_w), F32),
            jax.ShapeDtypeStruct(kv.shape, F32),
            jax.ShapeDtypeStruct((mix, d), F32),
            jax.ShapeDtypeStruct((1, d), F32),
            jax.ShapeDtypeStruct((1, 1), F32),
        ],
        scratch_shapes=[pltpu.VMEM((mix, d), F32), pltpu.SemaphoreType.DMA],
        compiler_params=_params("arbitrary"),
    )(x2, target2, proj, y_pool, y_ssm, proj, kv, w_out, g_post)


def _pack(arrays):
    flat = jnp.concatenate([a.reshape(-1) for a in arrays])
    rows = -(-flat.size // (N_DEV * SUBLANES * LANES)) * N_DEV * SUBLANES
    return jnp.pad(flat, (0, rows * LANES - flat.size)).reshape(rows, LANES)


def _unpack(packed, like):
    flat, out, at = packed.reshape(-1), [], 0
    for a in like:
        out.append(flat[at : at + a.size].reshape(a.shape))
        at += a.size
    return out


def kernel(x, mem, g_pre, w_in, w_pool, pool_scale, a_re, a_im, log_dt, b_re, b_im, c_re, c_im, d_skip, w_glu, g_mem, w_kv, w_out, g_post, loss_target, m_g_pre, m_w_in, m_w_pool, m_pool_scale, m_a_re, m_a_im, m_log_dt, m_b_re, m_b_im, m_c_re, m_c_im, m_d_skip, m_w_glu, m_g_mem, m_w_kv, m_w_out, m_g_post, v_g_pre, v_w_in, v_w_pool, v_pool_scale, v_a_re, v_a_im, v_log_dt, v_b_re, v_b_im, v_c_re, v_c_im, v_d_skip, v_w_glu, v_g_mem, v_w_kv, v_w_out, v_g_post):
    batch, seq, d = x.shape
    cl = seq // NCH
    pool_w, ssm_w = pool_scale.shape[1], d_skip.shape[1]
    att_w = w_kv.shape[2] // 2
    tokens = batch * seq
    x2 = x.reshape(tokens, d)
    target2 = loss_target.reshape(tokens, d)

    wp_blk = jax.scipy.linalg.block_diag(*w_pool[0]).astype(BF16)
    ssm_params = (a_re[0], a_im[0], log_dt[0], b_re[0], b_im[0], c_re[0], c_im[0])
    (lam_row, b_tab, c_tab), tables_vjp = jax.vjp(_ssm_tables, *ssm_params)
    nb = b_tab.shape[0]
    lam8 = jnp.broadcast_to(lam_row, (SUBLANES, lam_row.shape[1]))
    lcl = _lam_power(a_re[0], a_im[0], log_dt[0], float(cl), 1.0, nb)
    b_bf, c_bf = b_tab.astype(BF16), c_tab.astype(BF16)

    proj, w_in_f, (w_glu_g,) = _in_proj(x2, g_pre, w_in[0], [w_glu[0].astype(BF16)])
    proj3 = proj.reshape(batch, seq, proj.shape[1])
    w_glu_f = w_glu_g.transpose(1, 0, 2).reshape(w_glu_g.shape[1], N_DEV * w_glu_g.shape[2])
    (y_ssm, pre_ssm, z_ssm, init_ssm), (w_out_g, w_kv_g) = _ssm_fwd(
        proj3, pool_w, b_bf, c_bf, lam8, lcl, d_skip, w_glu_f, [w_out[0].astype(BF16), w_kv[0].astype(BF16)])
    w_out_f = w_out_g.reshape(N_DEV * w_out_g.shape[1], w_out_g.shape[2])
    w_kv_f = w_kv_g.reshape(N_DEV * w_kv_g.shape[1], w_kv_g.shape[2])
    y_pool, diff_pool, kv = _pool_kv_fwd(proj, wp_blk, pool_scale, seq, mem, g_mem, w_kv_f)

    dres, dgate, dy_pool, dy_ssm, dq, dkv, gw_out, gg_post, loss_part = _tail(
        x2, target2, proj, y_pool, y_ssm.reshape(tokens, ssm_w), kv, w_out_f, g_post)

    du_pool, gwp_dense, g_scale, gw_kv, gg_mem = _pool_kv_bwd(
        diff_pool, dy_pool, wp_blk, pool_scale, seq, mem, dkv, g_mem, w_kv_f)
    gw_kv8 = gw_kv.reshape(N_DEV, -1, gw_kv.shape[1])
    gw_out8 = gw_out.reshape(N_DEV, -1, gw_out.shape[1])
    (du_ssm, gb_tab, gc_tab_t, gw_glu, gd_skip, glam), (kv_landed, out_landed) = _ssm_bwd(
        proj3, pool_w, pre_ssm, z_ssm, dy_ssm.reshape(batch, seq, ssm_w), init_ssm, b_bf, b_bf.transpose(0, 2, 1),
        c_bf.transpose(0, 2, 1), lam8, lcl, d_skip, w_glu_f, [gw_kv8, gw_out8])
    grad_x2, gw_in, gg_pre = _in_proj_bwd(
        x2, dres, du_pool, du_ssm.reshape(tokens, ssm_w), dq, dgate, g_pre,
        w_in_f)

    gw = pool_w // len(POOL_WINDOWS)
    gw_pool = jnp.stack([gwp_dense[i * gw : (i + 1) * gw, i * gw : (i + 1) * gw] for i in range(len(POOL_WINDOWS))])
    g_ssm = tables_vjp((glam, gb_tab, gc_tab_t.transpose(0, 2, 1)))

    small_w = [g_pre, w_pool, pool_scale, a_re, a_im, log_dt, b_re, b_im, c_re, c_im, d_skip, g_mem, g_post]
    small_m = [m_g_pre, m_w_pool, m_pool_scale, m_a_re, m_a_im, m_log_dt, m_b_re, m_b_im, m_c_re, m_c_im, m_d_skip, m_g_mem, m_g_post]
    small_v = [v_g_pre, v_w_pool, v_pool_scale, v_a_re, v_a_im, v_log_dt, v_b_re, v_b_im, v_c_re, v_c_im, v_d_skip, v_g_mem, v_g_post]
    small_g = [gg_pre, gw_pool, g_scale, *g_ssm, gd_skip, gg_mem, gg_post]
    big_g, small_sum = _reduce_all(
        [gw_in, gw_glu.reshape(ssm_w, N_DEV, -1).transpose(1, 0, 2)],
        _pack(small_g + [loss_part]),
        [(gw_kv8, kv_landed), (gw_out8, out_landed)])

    flat2 = lambda a: a.reshape(-1, a.shape[-1])
    sg = _unpack(small_sum, [flat2(a) for a in small_w] + [loss_part])
    loss = sg[-1].reshape(())
    small_names = ["g_pre", "w_pool", "pool_scale", "a_re", "a_im", "log_dt", "b_re", "b_im", "c_re", "c_im",
                   "d_skip", "g_mem", "g_post"]
    names = ["w_in", "w_glu", "w_kv", "w_out"] + small_names
    all_w = [w_in, w_glu, w_kv, w_out] + small_w
    all_m = [m_w_in, m_w_glu, m_w_kv, m_w_out] + small_m
    all_v = [v_w_in, v_w_glu, v_w_kv, v_w_out] + small_v
    updates = _adamw_all(big_g + sg[:-1], [flat2(a) for a in all_w], [flat2(a) for a in all_m], [flat2(a) for a in all_v])
    updates = {name: [t.reshape(a.shape) for t in u] for name, u, a in zip(names, updates, all_w)}

    order = ["g_pre", "w_in", "w_pool", "pool_scale", "a_re", "a_im", "log_dt", "b_re", "b_im", "c_re", "c_im",
             "d_skip", "w_glu", "g_mem", "w_kv", "w_out", "g_post"]
    outs = [[updates[name][kind] for name in order] for kind in range(4)]
    return (loss, grad_x2.reshape(batch, seq, d), *outs[0], *outs[1], *outs[2], *outs[3])
```
